```python
import math
import jax, jax.numpy as jnp
from jax import lax
import numpy as np

D_MODEL = 1024
BATCH = 16
SEQ = 256
DEPTH = 2
DEC_BATCH = 2
DEC_SEQ = 2048
PAST_LEN = 512

GRID_W = 64
HEAD_DIM = 64
SCALE = HEAD_DIM ** -0.5
ROPE_THETA = 10000.0
EPS = 1e-6
A_HEADS = 8
A_KV_HEADS = 2
A_WINDOW = 128
A_BLOCK = 128
B_HEADS = 8
NA_ROWS = 8
NA_COLS = 16
C_HEADS = D_MODEL // (2 * HEAD_DIM)
C_QBLOCK = 128
N_EXPERTS = 64
TOP_K = 6
N_GROUPS = 8
TOPK_GROUPS = 4
EXPERT_DIM = 256
SHARED_DIM = 256
ROUTED_SCALE = 2.5
N_AB = (DEPTH + 1) // 2
N_C = DEPTH // 2
A_Q_W = A_HEADS * HEAD_DIM
A_KV_W = A_KV_HEADS * HEAD_DIM
B_W = B_HEADS * HEAD_DIM
AB_SPLITS = [A_Q_W, A_Q_W + A_KV_W, A_Q_W + 2 * A_KV_W, A_Q_W + 2 * A_KV_W + B_W, A_Q_W + 2 * A_KV_W + 2 * B_W]
AB_IN = A_Q_W + 2 * A_KV_W + 3 * B_W
AB_OUT = A_Q_W + B_W
C_W = C_HEADS * 2 * HEAD_DIM
C_IN = 3 * C_W
C_OUT = C_W

kernel_name = "hybrid_dit_window_na_diffattn_moe_step"

f32 = jnp.float32


def _rms_norm(x, g):
    xf = x.astype(f32)
    y = xf * lax.rsqrt(jnp.mean(xf * xf, axis=-1, keepdims=True) + EPS)
    return (y * g.astype(f32)).astype(x.dtype)


def _adaln(cond, w_mod, b_mod):
    m = jnp.dot(jax.nn.silu(cond), w_mod) + b_mod
    return [t[:, None, :] for t in jnp.split(m, 6, axis=-1)]


def _joint_softmax(blocks):
    sizes = [b.shape[-1] for b in blocks]
    p = jax.nn.softmax(jnp.concatenate([b.astype(f32) for b in blocks], axis=-1), axis=-1)
    return jnp.split(p, np.cumsum(sizes)[:-1].tolist(), axis=-1)


def _rot_half(x, cos, sin):
    x1, x2 = jnp.split(x, 2, axis=-1)
    return jnp.concatenate([x1 * cos - x2 * sin, x2 * cos + x1 * sin], axis=-1)


def _axial_rope(x):
    L = x.shape[1]
    nq = HEAD_DIM // 4
    t = jnp.arange(L)
    inv = jnp.power(ROPE_THETA, -jnp.arange(nq, dtype=f32) / nq)
    shp = (L,) + (1,) * (x.ndim - 3) + (nq,)
    ang_r = ((t // GRID_W).astype(f32)[:, None] * inv).reshape(shp)
    ang_c = ((t % GRID_W).astype(f32)[:, None] * inv).reshape(shp)
    xr, xc = jnp.split(x, 2, axis=-1)
    dt = x.dtype
    return jnp.concatenate([
        _rot_half(xr, jnp.cos(ang_r).astype(dt), jnp.sin(ang_r).astype(dt)),
        _rot_half(xc, jnp.cos(ang_c).astype(dt), jnp.sin(ang_c).astype(dt))], axis=-1)


def _project_ab(h, w_in):
    B, L, _ = h.shape
    qa, ka, va, qb, kb, vb = jnp.split(jnp.dot(h, w_in), AB_SPLITS, axis=-1)
    return (qa.reshape(B, L, A_HEADS, HEAD_DIM), ka.reshape(B, L, A_KV_HEADS, HEAD_DIM),
            va.reshape(B, L, A_KV_HEADS, HEAD_DIM), qb.reshape(B, L, B_HEADS, HEAD_DIM),
            kb.reshape(B, L, B_HEADS, HEAD_DIM), vb.reshape(B, L, B_HEADS, HEAD_DIM))


def _project_c(h, w_in):
    B, L, _ = h.shape
    q, k, v = jnp.split(jnp.dot(h, w_in), 3, axis=-1)
    return (q.reshape(B, L, C_HEADS, 2, HEAD_DIM), k.reshape(B, L, C_HEADS, 2, HEAD_DIM),
            v.reshape(B, L, C_HEADS, 2 * HEAD_DIM))


def _context_attention(q, k, v, sink=None):
    B, S, H, _ = q.shape
    KV = k.shape[2]
    G = H // KV
    qg = q.reshape(B, S, KV, G, HEAD_DIM)
    s = jnp.einsum('bqkgd,bpkd->bkgqp', qg, k).astype(f32) * SCALE
    if sink is None:
        p = jax.nn.softmax(s, axis=-1)
    else:
        s_sink = jnp.broadcast_to(sink.astype(f32).reshape(1, KV, G, 1, 1), s.shape[:-1] + (1,))
        _, p = _joint_softmax([s_sink, s])
    o = jnp.einsum('bkgqp,bpkd->bqkgd', p.astype(v.dtype), v)
    return o.reshape(B, S, H * HEAD_DIM)


def _window_attention(q, k, v, ck, cv, sink):
    B, L, H, _ = q.shape
    KV = k.shape[2]
    G = H // KV
    nb = L // A_BLOCK
    qb = q.reshape(B, nb, A_BLOCK, KV, G, HEAD_DIM)
    pad = ((0, 0), (A_BLOCK, A_BLOCK), (0, 0), (0, 0))
    kp = jnp.pad(k, pad).reshape(B, nb + 2, A_BLOCK, KV, HEAD_DIM)
    vp = jnp.pad(v, pad).reshape(B, nb + 2, A_BLOCK, KV, HEAD_DIM)
    kw = jnp.concatenate([kp[:, :-2], kp[:, 1:-1], kp[:, 2:]], axis=2)
    vw = jnp.concatenate([vp[:, :-2], vp[:, 1:-1], vp[:, 2:]], axis=2)
    qpos = jnp.arange(nb)[:, None] * A_BLOCK + jnp.arange(A_BLOCK)[None, :]
    kpos = jnp.arange(nb)[:, None] * A_BLOCK - A_BLOCK + jnp.arange(3 * A_BLOCK)[None, :]
    mask = ((jnp.abs(qpos[:, :, None] - kpos[:, None, :]) <= A_WINDOW)
            & (kpos[:, None, :] >= 0) & (kpos[:, None, :] < L))
    s_loc = jnp.einsum('bnqkgd,bnjkd->bnkgqj', qb, kw).astype(f32) * SCALE
    s_loc = jnp.where(mask[None, :, None, None], s_loc, -jnp.inf)
    s_ctx = jnp.einsum('bnqkgd,bpkd->bnkgqp', qb, ck).astype(f32) * SCALE
    s_sink = jnp.broadcast_to(sink.astype(f32).reshape(1, 1, KV, G, 1, 1), s_loc.shape[:-1] + (1,))
    _, p_ctx, p_loc = _joint_softmax([s_sink, s_ctx, s_loc])
    o = (jnp.einsum('bnkgqj,bnjkd->bnqkgd', p_loc.astype(v.dtype), vw)
         + jnp.einsum('bnkgqp,bpkd->bnqkgd', p_ctx.astype(v.dtype), cv))
    return o.reshape(B, L, H * HEAD_DIM)


def _neighbourhood_attention(q, k, v, ck, cv, rel_bias):
    B, L, H, _ = q.shape
    W = GRID_W
    rows = L // W
    kh = min(NA_ROWS, rows)
    r = jnp.arange(rows)
    row_start = jnp.clip(r - kh // 2, 0, rows - kh)
    row_idx = row_start[:, None] + jnp.arange(kh)[None, :]
    q5 = q.reshape(B, rows, W, H, HEAD_DIM)
    kg = k.reshape(B, rows, W, H, HEAD_DIM)[:, row_idx].reshape(B, rows, kh * W, H, HEAD_DIM)
    vg = v.reshape(B, rows, W, H, HEAD_DIM)[:, row_idx].reshape(B, rows, kh * W, H, HEAD_DIM)
    col = jnp.arange(W)
    col_start = jnp.clip(col - NA_COLS // 2, 0, W - NA_COLS)
    col_mask = (col[None, :] >= col_start[:, None]) & (col[None, :] < col_start[:, None] + NA_COLS)
    mask = jnp.tile(col_mask, (1, kh))
    dr_idx = row_idx - r[:, None] + (NA_ROWS - 1)
    dc_idx = jnp.clip(col[None, :] - col[:, None] + (NA_COLS - 1), 0, 2 * NA_COLS - 2)
    bias = rel_bias.astype(f32)[:, dr_idx][..., dc_idx]
    bias = bias.transpose(0, 1, 3, 2, 4).reshape(H, rows, W, kh * W)
    s_loc = jnp.einsum('brqhd,brjhd->bhrqj', q5, kg).astype(f32) * SCALE + bias[None]
    s_loc = jnp.where(mask[None, None, None], s_loc, -jnp.inf)
    s_ctx = jnp.einsum('brqhd,bphd->bhrqp', q5, ck).astype(f32) * SCALE
    p_ctx, p_loc = _joint_softmax([s_ctx, s_loc])
    o = (jnp.einsum('bhrqj,brjhd->brqhd', p_loc.astype(v.dtype), vg)
         + jnp.einsum('bhrqp,bphd->brqhd', p_ctx.astype(v.dtype), cv))
    return o.reshape(B, L, H * HEAD_DIM)


def _diff_lambda(lq1, lk1, lq2, lk2, lam_init):
    return (jnp.exp(jnp.sum(lq1.astype(f32) * lk1.astype(f32)))
            - jnp.exp(jnp.sum(lq2.astype(f32) * lk2.astype(f32))) + lam_init)


def _diff_core(q, k, v, lam):
    s = jnp.einsum('bqhcd,bkhcd->bhcqk', q, k).astype(f32) * SCALE
    p = jax.nn.softmax(s, axis=-1)
    a = p[:, :, 0] - lam * p[:, :, 1]
    return jnp.einsum('bhqk,bkhe->bqhe', a.astype(v.dtype), v)


def _diff_latent(q, k, v, ck, cv, lam):
    B, L = q.shape[:2]
    nb = L // C_QBLOCK
    k_all = jnp.concatenate([ck, k], axis=1)
    v_all = jnp.concatenate([cv, v], axis=1)
    qb = q.reshape(B, nb, C_QBLOCK, C_HEADS, 2, HEAD_DIM).transpose(1, 0, 2, 3, 4, 5)
    o = lax.map(lambda qq: _diff_core(qq, k_all, v_all, lam), qb)
    return o.transpose(1, 0, 2, 3, 4).reshape(B, L, C_HEADS, 2 * HEAD_DIM)


def _diff_out(o, subln, lam_init, w_out):
    B, L = o.shape[:2]
    o = _rms_norm(o, subln) * (1.0 - lam_init)
    return jnp.dot(o.reshape(B, L, C_OUT), w_out)


def _moe(h, router_w, router_bias, w_gate, w_up, w_down, s_gate, s_up, s_down):
    shp = h.shape
    t = h.reshape(-1, shp[-1])
    n = t.shape[0]
    scores = jax.nn.sigmoid(jnp.dot(t, router_w).astype(f32))
    biased = scores + router_bias.astype(f32)
    grouped = biased.reshape(n, N_GROUPS, N_EXPERTS // N_GROUPS)
    group_score = lax.top_k(grouped, 2)[0].sum(-1)
    _, top_groups = lax.top_k(group_score, TOPK_GROUPS)
    group_mask = jax.nn.one_hot(top_groups, N_GROUPS, dtype=f32).sum(1)
    expert_mask = jnp.repeat(group_mask, N_EXPERTS // N_GROUPS, axis=1) > 0
    _, top_e = lax.top_k(jnp.where(expert_mask, biased, -jnp.inf), TOP_K)
    w = jnp.take_along_axis(scores, top_e, axis=-1)
    w = w / jnp.sum(w, axis=-1, keepdims=True) * ROUTED_SCALE
    gates = jnp.sum(jax.nn.one_hot(top_e, N_EXPERTS, dtype=f32) * w[..., None], axis=1)
    hg = jnp.einsum('nd,edf->nef', t, w_gate)
    hu = jnp.einsum('nd,edf->nef', t, w_up)
    act = jax.nn.silu(hg) * hu * gates[:, :, None].astype(t.dtype)
    routed = jnp.einsum('nef,efd->nd', act, w_down)
    shared = jnp.dot(jax.nn.silu(jnp.dot(t, s_gate)) * jnp.dot(t, s_up), s_down)
    return (routed + shared).reshape(shp)


def setup_inputs(seed: int = 0) -> dict:
    key = jax.random.key(seed)
    ks = jax.random.split(key, 40)
    nrm = lambda k, s, sc: jax.random.normal(k, s, f32) * sc
    D = D_MODEL
    return {
        "x_prompt": nrm(ks[0], (BATCH, SEQ, D), 1.0),
        "x_sample": nrm(ks[1], (DEC_BATCH, DEC_SEQ, D), 1.0),
        "cache_a_k": nrm(ks[2], (DEC_BATCH, N_AB, PAST_LEN, A_KV_HEADS, HEAD_DIM), 1.0),
        "cache_a_v": nrm(ks[3], (DEC_BATCH, N_AB, PAST_LEN, A_KV_HEADS, HEAD_DIM), 1.0),
        "cache_b_k": nrm(ks[4], (DEC_BATCH, N_AB, PAST_LEN, B_HEADS, HEAD_DIM), 1.0),
        "cache_b_v": nrm(ks[5], (DEC_BATCH, N_AB, PAST_LEN, B_HEADS, HEAD_DIM), 1.0),
        "cache_c_k": nrm(ks[6], (DEC_BATCH, N_C, PAST_LEN, C_HEADS, 2, HEAD_DIM), 1.0),
        "cache_c_v": nrm(ks[7], (DEC_BATCH, N_C, PAST_LEN, C_HEADS, 2 * HEAD_DIM), 1.0),
        "c": nrm(ks[8], (DEC_BATCH, D), 1.0),
        "c_ctx": nrm(ks[9], (D,), 1.0),
        "w_mod": nrm(ks[10], (DEPTH, D, 6 * D), 0.5 * D ** -0.5),
        "b_mod": nrm(ks[11], (DEPTH, 6 * D), 0.02),
        "norm_mix": 1.0 + nrm(ks[12], (DEPTH, D), 0.02),
        "norm_ffn": 1.0 + nrm(ks[13], (DEPTH, D), 0.02),
        "w_in_ab": nrm(ks[14], (N_AB, D, AB_IN), D ** -0.5),
        "w_out_ab": nrm(ks[15], (N_AB, AB_OUT, D), AB_OUT ** -0.5),
        "sink_a": nrm(ks[16], (N_AB, A_HEADS), 0.5),
        "rel_bias_b": nrm(ks[17], (N_AB, B_HEADS, 2 * NA_ROWS - 1, 2 * NA_COLS - 1), 0.5),
        "w_in_c": nrm(ks[18], (N_C, D, C_IN), D ** -0.5),
        "w_out_c": nrm(ks[19], (N_C, C_OUT, D), C_OUT ** -0.5),
        "lam_q1": nrm(ks[20], (N_C, HEAD_DIM), 0.1),
        "lam_k1": nrm(ks[21], (N_C, HEAD_DIM), 0.1),
        "lam_q2": nrm(ks[22], (N_C, HEAD_DIM), 0.1),
        "lam_k2": nrm(ks[23], (N_C, HEAD_DIM), 0.1),
        "subln_c": 1.0 + nrm(ks[24], (N_C, 2 * HEAD_DIM), 0.02),
        "router_w": nrm(ks[25], (DEPTH, D, N_EXPERTS), D ** -0.5),
        "router_bias": nrm(ks[26], (DEPTH, N_EXPERTS), 0.01),
        "exp_w_gate": nrm(ks[27], (DEPTH, N_EXPERTS, D, EXPERT_DIM), D ** -0.5),
        "exp_w_up": nrm(ks[28], (DEPTH, N_EXPERTS, D, EXPERT_DIM), D ** -0.5),
        "exp_w_down": nrm(ks[29], (DEPTH, N_EXPERTS, EXPERT_DIM, D), EXPERT_DIM ** -0.5),
        "sh_w_gate": nrm(ks[30], (DEPTH, D, SHARED_DIM), D ** -0.5),
        "sh_w_up": nrm(ks[31], (DEPTH, D, SHARED_DIM), D ** -0.5),
        "sh_w_down": nrm(ks[32], (DEPTH, SHARED_DIM, D), SHARED_DIM ** -0.5),
        "final_norm": 1.0 + nrm(ks[33], (D,), 0.02),
    }


def reference(x_prompt, x_sample, cache_a_k, cache_a_v, cache_b_k, cache_b_v, cache_c_k, cache_c_v,
              c, c_ctx, w_mod, b_mod, norm_mix, norm_ffn, w_in_ab, w_out_ab, sink_a, rel_bias_b,
              w_in_c, w_out_c, lam_q1, lam_k1, lam_q2, lam_k2, subln_c, router_w, router_bias,
              exp_w_gate, exp_w_up, exp_w_down, sh_w_gate, sh_w_up, sh_w_down, final_norm):
    xp, xs = x_prompt, x_sample
    new_ak, new_av, new_bk, new_bv, new_ck, new_cv = [], [], [], [], [], []
    for layer in range(DEPTH):
        i = layer // 2
        sh1_p, sc1_p, g1_p, sh2_p, sc2_p, g2_p = _adaln(c_ctx[None, :], w_mod[layer], b_mod[layer])
        sh1_s, sc1_s, g1_s, sh2_s, sc2_s, g2_s = _adaln(c, w_mod[layer], b_mod[layer])
        hp = _rms_norm(xp, norm_mix[layer]) * (1 + sc1_p) + sh1_p
        hs = _rms_norm(xs, norm_mix[layer]) * (1 + sc1_s) + sh1_s
        if layer % 2 == 0:
            qa, ka, va, qb, kb, vb = _project_ab(hp, w_in_ab[i])
            oa = _context_attention(qa, ka, va, sink_a[i])
            ob = _context_attention(qb, kb, vb)
            yp = jnp.dot(jnp.concatenate([oa, ob], axis=-1), w_out_ab[i])
            new_ak.append(ka); new_av.append(va); new_bk.append(kb); new_bv.append(vb)
            qa, ka, va, qb, kb, vb = _project_ab(hs, w_in_ab[i])
            qa = _axial_rope(qa)
            ka = _axial_rope(ka)
            oa = _window_attention(qa, ka, va, cache_a_k[:, i], cache_a_v[:, i], sink_a[i])
            ob = _neighbourhood_attention(qb, kb, vb, cache_b_k[:, i], cache_b_v[:, i], rel_bias_b[i])
            ys = jnp.dot(jnp.concatenate([oa, ob], axis=-1), w_out_ab[i])
        else:
            lam_init = 0.8 - 0.6 * math.exp(-0.3 * layer)
            lam = _diff_lambda(lam_q1[i], lam_k1[i], lam_q2[i], lam_k2[i], lam_init)
            q, k, v = _project_c(hp, w_in_c[i])
            yp = _diff_out(_diff_core(q, k, v, lam), subln_c[i], lam_init, w_out_c[i])
            new_ck.append(k); new_cv.append(v)
            q, k, v = _project_c(hs, w_in_c[i])
            q = _axial_rope(q)
            k = _axial_rope(k)
            ys = _diff_out(_diff_latent(q, k, v, cache_c_k[:, i], cache_c_v[:, i], lam),
                           subln_c[i], lam_init, w_out_c[i])
        xp = xp + g1_p * yp
        xs = xs + g1_s * ys
        hp = _rms_norm(xp, norm_ffn[layer]) * (1 + sc2_p) + sh2_p
        hs = _rms_norm(xs, norm_ffn[layer]) * (1 + sc2_s) + sh2_s
        moe_w = (router_w[layer], router_bias[layer], exp_w_gate[layer], exp_w_up[layer], exp_w_down[layer],
                 sh_w_gate[layer], sh_w_up[layer], sh_w_down[layer])
        xp = xp + g2_p * _moe(hp, *moe_w)
        xs = xs + g2_s * _moe(hs, *moe_w)
    y_prompt = _rms_norm(xp, final_norm)
    y_sample = _rms_norm(xs, final_norm)
    new_a_k = jnp.stack(new_ak, axis=1)
    new_a_v = jnp.stack(new_av, axis=1)
    new_b_k = jnp.stack(new_bk, axis=1)
    new_b_v = jnp.stack(new_bv, axis=1)
    new_c_k = jnp.stack(new_ck, axis=1)
    new_c_v = jnp.stack(new_cv, axis=1)
    return (y_prompt, y_sample, new_a_k, new_a_v, new_b_k, new_b_v, new_c_k, new_c_v)
```

```python
import functools
import math

import jax
import jax.numpy as jnp
import numpy as np
from jax import lax
from jax.experimental import pallas as pl
from jax.experimental.pallas import tpu as pltpu

f32 = jnp.float32
bf16 = jnp.bfloat16
i32 = jnp.int32

D_MODEL = 1024
N_PROMPT_BATCH = 16
PROMPT_SEQ = 256
DEPTH = 2
N_SAMPLE_BATCH = 2
SAMPLE_SEQ = 2048
PAST_LEN = 512
GRID_W = 64
HEAD_DIM = 64
ROPE_THETA = 10000.0
EPS = 1e-6
A_WINDOW = 128
NA_ROWS = 8
NA_COLS = 16
N_EXPERTS = 64
TOP_K = 6
N_GROUPS = 8
TOPK_GROUPS = 4
EXPERT_DIM = 256
ROUTED_SCALE = 2.5
Q_SCALE = HEAD_DIM ** -0.5

N_PROMPT = N_PROMPT_BATCH * PROMPT_SEQ
N_SAMPLE = N_SAMPLE_BATCH * SAMPLE_SEQ
N_TOK = N_PROMPT + N_SAMPLE

LANES = 128
TM = 256
N_PROMPT_TILES = N_PROMPT // TM
N_TILES = N_TOK // TM
TILES_PER_SAMPLE = SAMPLE_SEQ // TM
QB = 128
CHUNK = 16
SLOTS = -(-(TM * TOP_K + N_EXPERTS * (CHUNK - 1)) // 256) * 256
SLOT_CHUNKS = SLOTS // CHUNK
XW = D_MODEL + LANES
GM = 256
_MAX_SORTED = TM * TOP_K * N_TILES + N_TILES * N_EXPERTS * (CHUNK - 1) + N_EXPERTS * (GM - CHUNK)
G_TILES = -(-_MAX_SORTED // GM)
G_CHUNKS = GM // CHUNK
VMEM_LIMIT = 56 * 1024 * 1024

NEG = -1e30


def _cparams(sem):
    return pltpu.CompilerParams(dimension_semantics=sem, vmem_limit_bytes=VMEM_LIMIT)


def _mod_row(i):
    return jnp.where(i < N_PROMPT_TILES, 0, 1 + (i - N_PROMPT_TILES) // TILES_PER_SAMPLE)


def _norm_mod(x, g, scale, shift):
    y = x * lax.rsqrt(jnp.mean(x * x, axis=-1, keepdims=True) + EPS)
    return (y * g) * (1.0 + scale) + shift


def _silu(x):
    return x * jax.nn.sigmoid(x)


def _dot(a, b):
    return jnp.dot(a, b, preferred_element_type=f32)


def _dot_nt(a, b):
    return lax.dot_general(a, b, (((1,), (1,)), ((), ())), preferred_element_type=f32)


ADA_COLS = 1536


def _adaln_kernel(cond_ref, w_ref, b_ref, o_ref):
    s = _silu(cond_ref[...]).astype(bf16)
    o_ref[...] = _dot(s, w_ref[...].astype(bf16)) + b_ref[...]


def _adaln(cond8, w_mod, b_mod):
    n6 = 6 * D_MODEL
    return pl.pallas_call(
        _adaln_kernel,
        grid=(DEPTH, n6 // ADA_COLS),
        in_specs=[
            pl.BlockSpec((8, D_MODEL), lambda l, j: (0, 0)),
            pl.BlockSpec((None, D_MODEL, ADA_COLS), lambda l, j: (l, 0, j)),
            pl.BlockSpec((None, 1, ADA_COLS), lambda l, j: (l, 0, j)),
        ],
        out_specs=pl.BlockSpec((None, 8, ADA_COLS), lambda l, j: (l, 0, j)),
        out_shape=jax.ShapeDtypeStruct((DEPTH, 8, n6), f32),
        compiler_params=_cparams(("parallel", "parallel")),
        name="adaln",
    )(cond8, w_mod, b_mod.reshape(DEPTH, 1, n6))


def _rope_block(blk, cos, sin_a, sin_b):
    return blk * cos + pltpu.roll(blk, LANES - 16, 1) * sin_a + pltpu.roll(blk, 16, 1) * sin_b


def _inproj_kernel(x_ref, mod_ref, g_ref, w_ref, cos_ref, sa_ref, sb_ref, qkv_ref, *kv_refs, chunks):
    i = pl.program_id(0)
    h = _norm_mod(x_ref[...], g_ref[...], mod_ref[1:2, :], mod_ref[0:1, :]).astype(bf16)
    is_prompt = i < N_PROMPT_TILES
    for c0, c1, rope_blocks, kv_out in chunks:
        acc = _dot(h, w_ref[:, c0:c1])
        if rope_blocks:
            @pl.when(jnp.logical_not(is_prompt))
            def _():
                cos, sa, sb = cos_ref[...], sa_ref[...], sb_ref[...]
                for b in range((c1 - c0) // LANES):
                    blk = acc[:, b * LANES:(b + 1) * LANES]
                    if b in rope_blocks:
                        blk = _rope_block(blk, cos, sa, sb)
                    qkv_ref[:, c0 + b * LANES:c0 + (b + 1) * LANES] = blk.astype(bf16)

            @pl.when(is_prompt)
            def _():
                qkv_ref[:, c0:c1] = acc.astype(bf16)
        else:
            qkv_ref[:, c0:c1] = acc.astype(bf16)
        if kv_out:
            @pl.when(is_prompt)
            def _():
                for ridx, a0, a1, o0 in kv_out:
                    kv_refs[ridx][:, o0:o0 + (a1 - a0)] = acc[:, a0:a1]


def _inproj(x, mod_l, g, w, rope_tabs, chunks, kv_widths):
    n = w.shape[1]
    cos, sa, sb = rope_tabs

    def rope_idx(i):
        return (jnp.where(i < N_PROMPT_TILES, 0, (i - N_PROMPT_TILES) % TILES_PER_SAMPLE), 0)

    def kv_idx(i):
        return (jnp.minimum(i, N_PROMPT_TILES - 1), 0)

    return pl.pallas_call(
        functools.partial(_inproj_kernel, chunks=chunks),
        grid=(N_TILES,),
        in_specs=[
            pl.BlockSpec((TM, D_MODEL), lambda i: (i, 0)),
            pl.BlockSpec((None, 6, D_MODEL), lambda i: (_mod_row(i), 0, 0)),
            pl.BlockSpec((1, D_MODEL), lambda i: (0, 0)),
            pl.BlockSpec((D_MODEL, n), lambda i: (0, 0)),
            pl.BlockSpec((TM, LANES), rope_idx),
            pl.BlockSpec((TM, LANES), rope_idx),
            pl.BlockSpec((TM, LANES), rope_idx),
        ],
        out_specs=[pl.BlockSpec((TM, n), lambda i: (i, 0))]
        + [pl.BlockSpec((TM, wd), kv_idx) for wd in kv_widths],
        out_shape=[jax.ShapeDtypeStruct((N_TOK, n), bf16)]
        + [jax.ShapeDtypeStruct((N_PROMPT, wd), f32) for wd in kv_widths],
        compiler_params=_cparams(("arbitrary",)),
        name="inproj",
    )(x, mod_l, g, w, cos, sa, sb)


def _rope_tables():
    nq = HEAD_DIM // 4
    t = jnp.arange(SAMPLE_SEQ)
    inv = jnp.power(ROPE_THETA, -jnp.arange(nq, dtype=f32) / nq)
    ang_r = (t // GRID_W).astype(f32)[:, None] * inv
    ang_c = (t % GRID_W).astype(f32)[:, None] * inv
    zero = jnp.zeros_like(ang_r)

    def head(fr, fc):
        return jnp.concatenate([fr[0], fr[1], fc[0], fc[1]], axis=-1)

    cos = head((jnp.cos(ang_r), jnp.cos(ang_r)), (jnp.cos(ang_c), jnp.cos(ang_c)))
    sin_a = head((-jnp.sin(ang_r), zero), (-jnp.sin(ang_c), zero))
    sin_b = head((zero, jnp.sin(ang_r)), (zero, jnp.sin(ang_c)))
    two = lambda a: jnp.concatenate([a, a], axis=-1)
    return two(cos), two(sin_a), two(sin_b)


def _lane_lo(shape):
    return lax.broadcasted_iota(i32, shape, len(shape) - 1) < HEAD_DIM


def _half(q, lo_mask, half):
    keep = lo_mask if half == 0 else jnp.logical_not(lo_mask)
    return jnp.where(keep, q, jnp.zeros_like(q)) * Q_SCALE


def _swap_halves(x):
    return pltpu.roll(x.astype(f32), HEAD_DIM, 1).astype(x.dtype)


def _softmax_pv(score_blocks, value_blocks, sink=None):
    m = functools.reduce(jnp.maximum, [jnp.max(s, axis=-1, keepdims=True) for s in score_blocks])
    if sink is not None:
        m = jnp.maximum(m, sink)
    es = [jnp.exp(s - m) for s in score_blocks]
    den = functools.reduce(lambda a, b: a + b, [jnp.sum(e, axis=-1, keepdims=True) for e in es])
    if sink is not None:
        den = den + jnp.exp(sink - m)
    o = functools.reduce(lambda a, b: a + b, [_dot(e.astype(bf16), v) for e, v in zip(es, value_blocks)])
    return o * (1.0 / den)


L0_QA, L0_QB, L0_KB, L0_VB, L0_KA, L0_VA, L0_N = 0, 512, 1024, 1536, 2048, 2176, 2304


def _ctx0_kernel(sink_ref, qkv_ref, o_ref):
    lo = _lane_lo((1, LANES))
    k_a = qkv_ref[:, L0_KA:L0_KA + LANES]
    v_a = qkv_ref[:, L0_VA:L0_VA + LANES]
    k_a_sw, v_a_sw = _swap_halves(k_a), _swap_halves(v_a)
    for j in range(4):
        q_a = qkv_ref[:, L0_QA + j * LANES:L0_QA + (j + 1) * LANES]
        q_b = qkv_ref[:, L0_QB + j * LANES:L0_QB + (j + 1) * LANES]
        k_b = qkv_ref[:, L0_KB + j * LANES:L0_KB + (j + 1) * LANES]
        v_b = qkv_ref[:, L0_VB + j * LANES:L0_VB + (j + 1) * LANES]
        kv_head = j // 2
        outs_a, outs_b = [], []
        for half in range(2):
            k_use, v_use = (k_a, v_a) if kv_head == half else (k_a_sw, v_a_sw)
            s = _dot_nt(_half(q_a, lo, half), k_use)
            outs_a.append(_softmax_pv([s], [v_use], sink=sink_ref[2 * j + half]))
            s = _dot_nt(_half(q_b, lo, half), k_b)
            outs_b.append(_softmax_pv([s], [v_b]))
        o_ref[:, j * LANES:(j + 1) * LANES] = jnp.where(lo, outs_a[0], outs_a[1]).astype(bf16)
        o_ref[:, 512 + j * LANES:512 + (j + 1) * LANES] = jnp.where(lo, outs_b[0], outs_b[1]).astype(bf16)


def _ctx0(sink, qkv):
    return pl.pallas_call(
        _ctx0_kernel,
        grid=(N_PROMPT_BATCH,),
        in_specs=[
            pl.BlockSpec(memory_space=pltpu.SMEM),
            pl.BlockSpec((PROMPT_SEQ, L0_N), lambda b: (b, 0)),
        ],
        out_specs=pl.BlockSpec((PROMPT_SEQ, D_MODEL), lambda b: (b, 0)),
        out_shape=jax.ShapeDtypeStruct((N_PROMPT, D_MODEL), bf16),
        compiler_params=_cparams(("parallel",)),
        name="ctx0",
    )(sink, qkv)


WIN_KEYS = 3 * QB
NA_KEY_ROWS = 10
NA_KEYS = NA_KEY_ROWS * GRID_W
N_QB = SAMPLE_SEQ // QB
N_NA_PATTERNS = 5
_PROMPT_QBLOCKS = N_PROMPT // QB


def _na_pattern(n):
    return jnp.where(n < 2, n, jnp.where(n > N_QB - 3, n - (N_QB - 5), 2))


def _lat0_kernel(sink_ref, q_ref, kvb_ref, kva_ref, cak_ref, cav_ref, cbk_ref, cbv_ref, nab_ref, o_ref):
    n = pl.program_id(1)
    lo = _lane_lo((1, LANES))
    kstart = pl.multiple_of(jnp.clip((n - 1) * QB, 0, SAMPLE_SEQ - WIN_KEYS), QB)
    k_a = kva_ref[pl.ds(kstart, WIN_KEYS), 0:LANES]
    v_a = kva_ref[pl.ds(kstart, WIN_KEYS), LANES:2 * LANES]
    c_k = cak_ref[...].astype(bf16)
    c_v = cav_ref[...].astype(bf16)
    k_a_sw, v_a_sw, c_k_sw, c_v_sw = (_swap_halves(t) for t in (k_a, v_a, c_k, c_v))
    qpos = n * QB + lax.broadcasted_iota(i32, (QB, WIN_KEYS), 0)
    kpos = kstart + lax.broadcasted_iota(i32, (QB, WIN_KEYS), 1)
    in_window = jnp.abs(qpos - kpos) <= A_WINDOW
    krow = jnp.clip(2 * n - NA_ROWS // 2, 0, SAMPLE_SEQ // GRID_W - NA_KEY_ROWS)
    ktok = pl.multiple_of(krow * GRID_W, QB)
    for j in range(4):
        q_a = q_ref[:, L0_QA + j * LANES:L0_QA + (j + 1) * LANES]
        q_b = q_ref[:, L0_QB + j * LANES:L0_QB + (j + 1) * LANES]
        k_b = kvb_ref[pl.ds(ktok, NA_KEYS), j * LANES:(j + 1) * LANES]
        v_b = kvb_ref[pl.ds(ktok, NA_KEYS), 512 + j * LANES:512 + (j + 1) * LANES]
        cb_k = cbk_ref[:, j * LANES:(j + 1) * LANES].astype(bf16)
        cb_v = cbv_ref[:, j * LANES:(j + 1) * LANES].astype(bf16)
        kv_head = j // 2
        outs_a, outs_b = [], []
        for half in range(2):
            same = kv_head == half
            qh = _half(q_a, lo, half)
            s_loc = jnp.where(in_window, _dot_nt(qh, k_a if same else k_a_sw), NEG)
            s_ctx = _dot_nt(qh, c_k if same else c_k_sw)
            outs_a.append(_softmax_pv([s_ctx, s_loc], [c_v if same else c_v_sw, v_a if same else v_a_sw],
                                      sink=sink_ref[2 * j + half]))
            qh = _half(q_b, lo, half)
            s_loc = _dot_nt(qh, k_b) + nab_ref[2 * j + half]
            s_ctx = _dot_nt(qh, cb_k)
            outs_b.append(_softmax_pv([s_ctx, s_loc], [cb_v, v_b]))
        o_ref[:, j * LANES:(j + 1) * LANES] = jnp.where(lo, outs_a[0], outs_a[1]).astype(bf16)
        o_ref[:, 512 + j * LANES:512 + (j + 1) * LANES] = jnp.where(lo, outs_b[0], outs_b[1]).astype(bf16)


def _lat0(sink, qkv, cak, cav, cbk, cbv, nab):
    sb = N_PROMPT // SAMPLE_SEQ
    return pl.pallas_call(
        _lat0_kernel,
        grid=(N_SAMPLE_BATCH, N_QB),
        in_specs=[
            pl.BlockSpec(memory_space=pltpu.SMEM),
            pl.BlockSpec((QB, 1024), lambda b, n: (_PROMPT_QBLOCKS + b * N_QB + n, 0)),
            pl.BlockSpec((SAMPLE_SEQ, 1024), lambda b, n: (sb + b, 1)),
            pl.BlockSpec((SAMPLE_SEQ, 256), lambda b, n: (sb + b, L0_KA // 256)),
            pl.BlockSpec((None, PAST_LEN, LANES), lambda b, n: (b, 0, 0)),
            pl.BlockSpec((None, PAST_LEN, LANES), lambda b, n: (b, 0, 0)),
            pl.BlockSpec((None, PAST_LEN, 512), lambda b, n: (b, 0, 0)),
            pl.BlockSpec((None, PAST_LEN, 512), lambda b, n: (b, 0, 0)),
            pl.BlockSpec((None, 8, QB, NA_KEYS), lambda b, n: (_na_pattern(n), 0, 0, 0)),
        ],
        out_specs=pl.BlockSpec((QB, D_MODEL), lambda b, n: (b * N_QB + n, 0)),
        out_shape=jax.ShapeDtypeStruct((N_SAMPLE, D_MODEL), bf16),
        compiler_params=_cparams(("parallel", "arbitrary")),
        name="lat0",
    )(sink, qkv, qkv, qkv, cak, cav, cbk, cbv, nab)


def _na_bias_table(rel_bias):
    rows = SAMPLE_SEQ // GRID_W
    n = np.array([0, 1, 2, N_QB - 2, N_QB - 1])[:, None, None]
    qi = np.arange(QB)[None, :, None]
    ki = np.arange(NA_KEYS)[None, None, :]
    r = 2 * n + qi // GRID_W
    c = qi % GRID_W
    kr = np.clip(2 * n - NA_ROWS // 2, 0, rows - NA_KEY_ROWS) + ki // GRID_W
    kc = ki % GRID_W
    rs = np.clip(r - NA_ROWS // 2, 0, rows - NA_ROWS)
    cs = np.clip(c - NA_COLS // 2, 0, GRID_W - NA_COLS)
    valid = (kr >= rs) & (kr < rs + NA_ROWS) & (kc >= cs) & (kc < cs + NA_COLS)
    dr = np.clip(kr - r + NA_ROWS - 1, 0, 2 * NA_ROWS - 2)
    dc = np.clip(kc - c + NA_COLS - 1, 0, 2 * NA_COLS - 2)
    dr, dc = np.broadcast_to(dr, valid.shape), np.broadcast_to(dc, valid.shape)
    bias = rel_bias.astype(f32)[:, dr, dc]
    return jnp.where(valid[None], bias, NEG).transpose(1, 0, 2, 3)


def _diff_lambda(lam_ref, lam_init):
    lv = lam_ref[...]
    s1 = jnp.sum(lv[0:1, :] * lv[1:2, :], axis=-1, keepdims=True)
    s2 = jnp.sum(lv[2:3, :] * lv[3:4, :], axis=-1, keepdims=True)
    return jnp.exp(s1) - jnp.exp(s2) + lam_init


def _softmax_parts(score_blocks):
    m = functools.reduce(jnp.maximum, [jnp.max(s, axis=-1, keepdims=True) for s in score_blocks])
    es = [jnp.exp(s - m) for s in score_blocks]
    den = functools.reduce(lambda a, b: a + b, [jnp.sum(e, axis=-1, keepdims=True) for e in es])
    return es, 1.0 / den


def _diff_head(q, key_blocks, value_blocks, lam, subln, lo, lam_init):
    q1, q2 = _half(q, lo, 0), _half(q, lo, 1)
    e1, r1 = _softmax_parts([_dot_nt(q1, k) for k in key_blocks])
    e2, r2 = _softmax_parts([_dot_nt(q2, k) for k in key_blocks])
    r2 = r2 * lam
    o = functools.reduce(lambda a, b: a + b,
                         [_dot((a * r1 - b * r2).astype(bf16), v) for a, b, v in zip(e1, e2, value_blocks)])
    o = o * lax.rsqrt(jnp.mean(o * o, axis=-1, keepdims=True) + EPS)
    return (o * subln) * (1.0 - lam_init)


def _ctx1_kernel(lam_ref, subln_ref, qkv_ref, o_ref, *, lam_init):
    lo = _lane_lo((1, LANES))
    lam = _diff_lambda(lam_ref, lam_init)
    subln = subln_ref[...]
    for h in range(8):
        q = qkv_ref[:, h * LANES:(h + 1) * LANES]
        k = qkv_ref[:, 1024 + h * LANES:1024 + (h + 1) * LANES]
        v = qkv_ref[:, 2048 + h * LANES:2048 + (h + 1) * LANES]
        o_ref[:, h * LANES:(h + 1) * LANES] = _diff_head(q, [k], [v], lam, subln, lo, lam_init).astype(bf16)


def _ctx1(lamv, subln, qkv, lam_init):
    return pl.pallas_call(
        functools.partial(_ctx1_kernel, lam_init=lam_init),
        grid=(N_PROMPT_BATCH,),
        in_specs=[
            pl.BlockSpec((8, HEAD_DIM), lambda b: (0, 0)),
            pl.BlockSpec((1, LANES), lambda b: (0, 0)),
            pl.BlockSpec((PROMPT_SEQ, 3 * D_MODEL), lambda b: (b, 0)),
        ],
        out_specs=pl.BlockSpec((PROMPT_SEQ, D_MODEL), lambda b: (b, 0)),
        out_shape=jax.ShapeDtypeStruct((N_PROMPT, D_MODEL), bf16),
        compiler_params=_cparams(("parallel",)),
        name="ctx1",
    )(lamv, subln, qkv)


def _lat1_kernel(lam_ref, subln_ref, q_ref, k_ref, v_ref, ck_ref, cv_ref, o_ref, *, lam_init):
    lo = _lane_lo((1, LANES))
    lam = _diff_lambda(lam_ref, lam_init)
    subln = subln_ref[...]
    for h in range(8):
        sl = slice(h * LANES, (h + 1) * LANES)
        o_ref[:, sl] = _diff_head(q_ref[:, sl], [ck_ref[:, sl].astype(bf16), k_ref[:, sl]],
                                  [cv_ref[:, sl].astype(bf16), v_ref[:, sl]],
                                  lam, subln, lo, lam_init).astype(bf16)


def _lat1(lamv, subln, qkv, ck, cv, lam_init):
    sb = N_PROMPT // SAMPLE_SEQ
    nq = SAMPLE_SEQ // TM
    return pl.pallas_call(
        functools.partial(_lat1_kernel, lam_init=lam_init),
        grid=(N_SAMPLE_BATCH, nq),
        in_specs=[
            pl.BlockSpec((8, HEAD_DIM), lambda b, n: (0, 0)),
            pl.BlockSpec((1, LANES), lambda b, n: (0, 0)),
            pl.BlockSpec((TM, D_MODEL), lambda b, n: (N_PROMPT_TILES + b * nq + n, 0)),
            pl.BlockSpec((SAMPLE_SEQ, D_MODEL), lambda b, n: (sb + b, 1)),
            pl.BlockSpec((SAMPLE_SEQ, D_MODEL), lambda b, n: (sb + b, 2)),
            pl.BlockSpec((None, PAST_LEN, D_MODEL), lambda b, n: (b, 0, 0)),
            pl.BlockSpec((None, PAST_LEN, D_MODEL), lambda b, n: (b, 0, 0)),
        ],
        out_specs=pl.BlockSpec((TM, D_MODEL), lambda b, n: (b * nq + n, 0)),
        out_shape=jax.ShapeDtypeStruct((N_SAMPLE, D_MODEL), bf16),
        compiler_params=_cparams(("parallel", "arbitrary")),
        name="lat1",
    )(lamv, subln, qkv, qkv, qkv, ck, cv)


def _outproj_kernel(x_ref, op_ref, os_ref, mod_ref, w_ref, o_ref):
    i = pl.program_id(0)
    attn = jnp.where(i < N_PROMPT_TILES, op_ref[...], os_ref[...])
    o_ref[...] = x_ref[...] + mod_ref[2:3, :] * _dot(attn, w_ref[...])


def _outproj(x, o_prompt, o_sample, mod_l, w):
    return pl.pallas_call(
        _outproj_kernel,
        grid=(N_TILES,),
        in_specs=[
            pl.BlockSpec((TM, D_MODEL), lambda i: (i, 0)),
            pl.BlockSpec((TM, D_MODEL), lambda i: (jnp.minimum(i, N_PROMPT_TILES - 1), 0)),
            pl.BlockSpec((TM, D_MODEL), lambda i: (jnp.maximum(i - N_PROMPT_TILES, 0), 0)),
            pl.BlockSpec((None, 6, D_MODEL), lambda i: (_mod_row(i), 0, 0)),
            pl.BlockSpec((D_MODEL, D_MODEL), lambda i: (0, 0)),
        ],
        out_specs=pl.BlockSpec((TM, D_MODEL), lambda i: (i, 0)),
        out_shape=jax.ShapeDtypeStruct((N_TOK, D_MODEL), f32),
        compiler_params=_cparams(("parallel",)),
        name="outproj",
    )(x, o_prompt, o_sample, mod_l, w)


def _route_kernel(x_ref, mod_ref, g_ref, rwt_ref, rb_ref, xloc_ref, slots_ref, len_ref):
    ng, ge = N_GROUPS, N_EXPERTS // N_GROUPS
    h = _norm_mod(x_ref[...], g_ref[...], mod_ref[4:5, :], mod_ref[3:4, :])
    logits = lax.dot_general(rwt_ref[...], h, (((1,), (1,)), ((), ())),
                             precision=lax.Precision.HIGHEST, preferred_element_type=f32)
    scores = jax.nn.sigmoid(logits)
    biased = scores + rb_ref[...]
    s3 = scores.reshape(ng, ge, TM)
    b3 = biased.reshape(ng, ge, TM)
    in_group = lax.broadcasted_iota(i32, (ng, ge, TM), 1).astype(f32)
    group_id = lax.broadcasted_iota(i32, (ng, 1, TM), 0).astype(f32)
    expert_id = lax.broadcasted_iota(i32, (ng, ge, TM), 0).astype(f32) * ge + in_group

    def max01(a):
        return jnp.max(jnp.max(a, axis=0, keepdims=True), axis=1, keepdims=True)

    def min01(a):
        return jnp.min(jnp.min(a, axis=0, keepdims=True), axis=1, keepdims=True)

    def sum01(a):
        return jnp.sum(jnp.sum(a, axis=0, keepdims=True), axis=1, keepdims=True)

    m1 = jnp.max(b3, axis=1, keepdims=True)
    first = jnp.min(jnp.where(b3 == m1, in_group, ge), axis=1, keepdims=True)
    m2 = jnp.max(jnp.where(in_group == first, -jnp.inf, b3), axis=1, keepdims=True)
    gscore = m1 + m2
    gsel = jnp.zeros((ng, 1, TM), f32)
    for _ in range(TOPK_GROUPS):
        gm = jnp.max(gscore, axis=0, keepdims=True)
        gi = jnp.min(jnp.where(gscore == gm, group_id, ng), axis=0, keepdims=True)
        hit = group_id == gi
        gsel = jnp.where(hit, 1.0, gsel)
        gscore = jnp.where(hit, -jnp.inf, gscore)
    cand = jnp.where(jnp.broadcast_to(gsel, (ng, ge, TM)) > 0.0, b3, -jnp.inf)
    top_e, top_w = [], []
    for _ in range(TOP_K):
        em = max01(cand)
        ei = min01(jnp.where(cand == em, expert_id, N_EXPERTS))
        hit = expert_id == ei
        top_e.append(ei)
        top_w.append(sum01(jnp.where(hit, s3, 0.0)))
        cand = jnp.where(hit, -jnp.inf, cand)
    wsum = functools.reduce(lambda a, b: a + b, top_w)
    gates3 = jnp.zeros((ng, ge, TM), f32)
    sel3 = jnp.zeros((ng, ge, TM), f32)
    for ei, w in zip(top_e, top_w):
        hit = expert_id == ei
        gates3 = jnp.where(hit, w / wsum * ROUTED_SCALE, gates3)
        sel3 = jnp.where(hit, 1.0, sel3)
    gates = gates3.reshape(N_EXPERTS, TM)
    sel = sel3.reshape(N_EXPERTS, TM)

    cnt = jnp.sum(sel, axis=1, keepdims=True)
    run_len = jnp.ceil(cnt * (1.0 / CHUNK)) * CHUNK
    r_i = lax.broadcasted_iota(i32, (N_EXPERTS, N_EXPERTS), 0)
    c_i = lax.broadcasted_iota(i32, (N_EXPERTS, N_EXPERTS), 1)
    lower = jnp.where(c_i < r_i, 1.0, 0.0).astype(bf16)
    run_off = _dot(lower, jnp.broadcast_to(run_len, (N_EXPERTS, LANES)).astype(bf16))[:, 0:1]
    t_r = lax.broadcasted_iota(i32, (TM, TM), 0)
    t_c = lax.broadcasted_iota(i32, (TM, TM), 1)
    before = jnp.where(t_r < t_c, 1.0, 0.0).astype(bf16)
    rank = _dot(sel.astype(bf16), before)
    slot3 = (run_off + rank).reshape(ng, ge, TM)
    slots = [sum01(jnp.where(expert_id == ei, slot3, 0.0)).reshape(1, TM).astype(i32) for ei in top_e]
    for k in range(TOP_K):
        slots_ref[k:k + 1, :] = slots[k]
    slots_ref[TOP_K:8, :] = jnp.full((8 - TOP_K, TM), -1, i32)
    len_ref[...] = jnp.broadcast_to(run_len, (N_EXPERTS, LANES)).astype(i32)

    hb = h.astype(bf16)
    g_hi = gates.astype(bf16)
    g_lo = (gates - g_hi.astype(f32)).astype(bf16)
    g_parts = jnp.concatenate([g_hi, g_lo], axis=0)
    rows = 256

    def body(c, carry):
        base = pl.multiple_of(c * rows, rows)
        row_id = base + lax.broadcasted_iota(i32, (rows, TM), 0)
        p = jnp.zeros((rows, TM), f32)
        for k in range(TOP_K):
            p = jnp.where(row_id == slots[k], 1.0, p)
        p = p.astype(bf16)
        xloc_ref[pl.ds(base, rows), 0:D_MODEL] = _dot(p, hb).astype(bf16)
        xloc_ref[pl.ds(base, rows), D_MODEL:XW] = _dot_nt(p, g_parts).astype(bf16)
        return carry

    lax.fori_loop(0, SLOTS // rows, body, 0)


def _route(x, mod_l, g, rwt, rb):
    return pl.pallas_call(
        _route_kernel,
        grid=(N_TILES,),
        in_specs=[
            pl.BlockSpec((TM, D_MODEL), lambda i: (i, 0)),
            pl.BlockSpec((None, 6, D_MODEL), lambda i: (_mod_row(i), 0, 0)),
            pl.BlockSpec((1, D_MODEL), lambda i: (0, 0)),
            pl.BlockSpec((N_EXPERTS, D_MODEL), lambda i: (0, 0)),
            pl.BlockSpec((N_EXPERTS, 1), lambda i: (0, 0)),
        ],
        out_specs=[
            pl.BlockSpec((SLOTS, XW), lambda i: (i, 0)),
            pl.BlockSpec((None, 8, TM), lambda i: (i, 0, 0)),
            pl.BlockSpec((None, N_EXPERTS, LANES), lambda i: (i, 0, 0)),
        ],
        out_shape=[
            jax.ShapeDtypeStruct((N_TILES * SLOTS, XW), bf16),
            jax.ShapeDtypeStruct((N_TILES, 8, TM), i32),
            jax.ShapeDtypeStruct((N_TILES, N_EXPERTS, LANES), i32),
        ],
        compiler_params=_cparams(("parallel",)),
        name="route",
    )(x, mod_l, g, rwt, rb)


def _moe_plan(run_len):
    nt, ne = run_len.shape
    off_loc = jnp.cumsum(run_len, axis=1) - run_len
    n_e = jnp.sum(run_len, axis=0)
    n_pad = -(-n_e // GM) * GM
    g_end = jnp.cumsum(n_pad)
    g_start = g_end - n_pad
    before = jnp.cumsum(run_len, axis=0) - run_len
    run_dst = g_start[None, :] + before
    run_src = jnp.arange(nt, dtype=i32)[:, None] * SLOTS + off_loc
    dst_f = run_dst.T.reshape(-1)
    src_f = run_src.T.reshape(-1)
    len_f = run_len.T.reshape(-1)
    rows = jnp.arange(G_TILES * G_CHUNKS, dtype=i32) * CHUNK
    j = jnp.clip(jnp.searchsorted(dst_f, rows, side="right") - 1, 0, nt * ne - 1)
    within = rows - dst_f[j]
    chunk_src = jnp.where(within < len_f[j], src_f[j] + within, 0).astype(i32)
    tile_rows = jnp.arange(G_TILES, dtype=i32) * GM
    tile_valid = (tile_rows < g_end[-1]).astype(i32)
    tile_expert = jnp.clip(jnp.searchsorted(g_start, tile_rows, side="right") - 1, 0, ne - 1).astype(i32)
    tile_expert = jnp.where(tile_valid == 1, tile_expert, ne - 1)
    loc_rows = jnp.arange(SLOT_CHUNKS, dtype=i32) * CHUNK
    e_loc = jnp.clip(jax.vmap(lambda o: jnp.searchsorted(o, loc_rows, side="right"))(off_loc) - 1, 0, ne - 1)
    within = loc_rows[None, :] - jnp.take_along_axis(off_loc, e_loc, axis=1)
    ok = within < jnp.take_along_axis(run_len, e_loc, axis=1)
    chunk_map = jnp.where(ok, (jnp.take_along_axis(run_dst, e_loc, axis=1) + within) // CHUNK, -1).astype(i32)
    return tile_expert, tile_valid, chunk_src, chunk_map.reshape(-1)


def _gmm_copy(xloc_hbm, xbuf, sem, src_row, slot, c):
    return pltpu.make_async_copy(xloc_hbm.at[pl.ds(src_row, CHUNK)],
                                 xbuf.at[slot, pl.ds(c * CHUNK, CHUNK)], sem.at[slot])


def _gmm_kernel(te_ref, tv_ref, cs_ref, xloc_hbm, wg_ref, wu_ref, wd_ref, y_ref, xbuf, sem):
    i = pl.program_id(0)
    slot = i % 2

    def start(tile, s):
        for c in range(G_CHUNKS):
            src = pl.multiple_of(cs_ref[tile * G_CHUNKS + c], CHUNK)
            _gmm_copy(xloc_hbm, xbuf, sem, src, s, c).start()

    valid = tv_ref[i] == 1

    @pl.when(jnp.logical_and(i == 0, valid))
    def _():
        start(0, 0)

    nxt = jnp.minimum(i + 1, pl.num_programs(0) - 1)

    @pl.when(jnp.logical_and(i + 1 < pl.num_programs(0), tv_ref[nxt] == 1))
    def _():
        start(i + 1, 1 - slot)

    @pl.when(valid)
    def _():
        for c in range(G_CHUNKS):
            _gmm_copy(xloc_hbm, xbuf, sem, 0, slot, c).wait()
        x = xbuf[slot]
        e = te_ref[i]
        lane = lax.broadcasted_iota(i32, (1, LANES), 1)
        gate_lanes = jnp.logical_or(lane == e, lane == e + N_EXPERTS)
        gate = jnp.sum(jnp.where(gate_lanes, x[:, D_MODEL:XW].astype(f32), 0.0), axis=1, keepdims=True)
        xa = x[:, 0:D_MODEL]
        act = _silu(_dot(xa, wg_ref[...])) * _dot(xa, wu_ref[...]) * gate
        y_ref[...] = _dot(act.astype(bf16), wd_ref[...]).astype(bf16)

    @pl.when(jnp.logical_not(valid))
    def _():
        y_ref[...] = jnp.zeros(y_ref.shape, y_ref.dtype)


def _gmm(tile_expert, tile_valid, chunk_src, xloc, wg, wu, wd):
    grid_spec = pltpu.PrefetchScalarGridSpec(
        num_scalar_prefetch=3,
        grid=(G_TILES,),
        in_specs=[
            pl.BlockSpec(memory_space=pl.ANY),
            pl.BlockSpec((None, D_MODEL, EXPERT_DIM), lambda i, te, tv, cs: (te[i], 0, 0)),
            pl.BlockSpec((None, D_MODEL, EXPERT_DIM), lambda i, te, tv, cs: (te[i], 0, 0)),
            pl.BlockSpec((None, EXPERT_DIM, D_MODEL), lambda i, te, tv, cs: (te[i], 0, 0)),
        ],
        out_specs=pl.BlockSpec((GM, D_MODEL), lambda i, te, tv, cs: (i, 0)),
        scratch_shapes=[pltpu.VMEM((2, GM, XW), bf16), pltpu.SemaphoreType.DMA((2,))],
    )
    return pl.pallas_call(
        _gmm_kernel,
        grid_spec=grid_spec,
        out_shape=jax.ShapeDtypeStruct((G_TILES * GM, D_MODEL), bf16),
        compiler_params=_cparams(("arbitrary",)),
        name="gmm",
    )(tile_expert, tile_valid, chunk_src, xloc, wg, wu, wd)


def _combine_copy(y_hbm, ybuf, sem, sorted_chunk, slot, c):
    return pltpu.make_async_copy(y_hbm.at[pl.ds(pl.multiple_of(sorted_chunk * CHUNK, CHUNK), CHUNK)],
                                 ybuf.at[slot, pl.ds(pl.multiple_of(c * CHUNK, CHUNK), CHUNK)], sem.at[slot])


def _combine_kernel(cm_ref, y_hbm, slots_ref, x_ref, mod_ref, g_ref, sg_ref, su_ref, sd_ref, o_ref, ybuf, sem):
    i = pl.program_id(0)
    slot = i % 2

    def for_chunks(tile, s, wait):
        def body(c, carry):
            sc = cm_ref[tile * SLOT_CHUNKS + c]

            @pl.when(sc >= 0)
            def _():
                cp = _combine_copy(y_hbm, ybuf, sem, sc, s, c)
                cp.wait() if wait else cp.start()

            return carry

        lax.fori_loop(0, SLOT_CHUNKS, body, 0)

    @pl.when(i == 0)
    def _():
        ybuf[...] = jnp.zeros(ybuf.shape, ybuf.dtype)
        for_chunks(0, 0, False)

    @pl.when(i + 1 < pl.num_programs(0))
    def _():
        for_chunks(i + 1, 1 - slot, False)

    for_chunks(i, slot, True)

    x = x_ref[...]
    hb = _norm_mod(x, g_ref[...], mod_ref[4:5, :], mod_ref[3:4, :]).astype(bf16)
    shared = _dot((_silu(_dot(hb, sg_ref[...])) * _dot(hb, su_ref[...])).astype(bf16), sd_ref[...])
    row_id = lax.broadcasted_iota(i32, (SLOTS, TM), 0)
    p = jnp.zeros((SLOTS, TM), f32)
    for k in range(TOP_K):
        p = jnp.where(row_id == slots_ref[k:k + 1, :], 1.0, p)
    routed = lax.dot_general(p.astype(bf16), ybuf[slot], (((0,), (0,)), ((), ())), preferred_element_type=f32)
    o_ref[...] = x + mod_ref[5:6, :] * (routed + shared)


def _combine(chunk_map, y, slots, x, mod_l, g, sg, su, sd):
    shd = sg.shape[1]
    grid_spec = pltpu.PrefetchScalarGridSpec(
        num_scalar_prefetch=1,
        grid=(N_TILES,),
        in_specs=[
            pl.BlockSpec(memory_space=pl.ANY),
            pl.BlockSpec((None, 8, TM), lambda i, cm: (i, 0, 0)),
            pl.BlockSpec((TM, D_MODEL), lambda i, cm: (i, 0)),
            pl.BlockSpec((None, 6, D_MODEL), lambda i, cm: (_mod_row(i), 0, 0)),
            pl.BlockSpec((1, D_MODEL), lambda i, cm: (0, 0)),
            pl.BlockSpec((D_MODEL, shd), lambda i, cm: (0, 0)),
            pl.BlockSpec((D_MODEL, shd), lambda i, cm: (0, 0)),
            pl.BlockSpec((shd, D_MODEL), lambda i, cm: (0, 0)),
        ],
        out_specs=pl.BlockSpec((TM, D_MODEL), lambda i, cm: (i, 0)),
        scratch_shapes=[pltpu.VMEM((2, SLOTS, D_MODEL), bf16), pltpu.SemaphoreType.DMA((2,))],
    )
    return pl.pallas_call(
        _combine_kernel,
        grid_spec=grid_spec,
        out_shape=jax.ShapeDtypeStruct((N_TOK, D_MODEL), f32),
        compiler_params=_cparams(("arbitrary",)),
        name="combine",
    )(chunk_map, y, slots, x, mod_l, g, sg, su, sd)


def _moe(x, mod_l, g, rwt, rb, wg, wu, wd, sg, su, sd):
    xloc, slots, run_len = _route(x, mod_l, g, rwt, rb)
    tile_expert, tile_valid, chunk_src, chunk_map = _moe_plan(run_len[:, :, 0])
    y = _gmm(tile_expert, tile_valid, chunk_src, xloc, wg, wu, wd)
    return _combine(chunk_map, y, slots, x, mod_l, g, sg, su, sd)


def _final_kernel(x_ref, g_ref, yp_ref, ys_ref):
    i = pl.program_id(0)
    x = x_ref[...]
    y = (x * lax.rsqrt(jnp.mean(x * x, axis=-1, keepdims=True) + EPS)) * g_ref[...]

    @pl.when(i < N_PROMPT_TILES)
    def _():
        yp_ref[...] = y

    @pl.when(i >= N_PROMPT_TILES)
    def _():
        ys_ref[...] = y


def _final(x, g):
    return pl.pallas_call(
        _final_kernel,
        grid=(N_TILES,),
        in_specs=[
            pl.BlockSpec((TM, D_MODEL), lambda i: (i, 0)),
            pl.BlockSpec((1, D_MODEL), lambda i: (0, 0)),
        ],
        out_specs=[
            pl.BlockSpec((TM, D_MODEL), lambda i: (jnp.minimum(i, N_PROMPT_TILES - 1), 0)),
            pl.BlockSpec((TM, D_MODEL), lambda i: (jnp.maximum(i - N_PROMPT_TILES, 0), 0)),
        ],
        out_shape=[jax.ShapeDtypeStruct((N_PROMPT, D_MODEL), f32), jax.ShapeDtypeStruct((N_SAMPLE, D_MODEL), f32)],
        compiler_params=_cparams(("arbitrary",)),
        name="final_norm",
    )(x, g)


def _permute_w_in_ab(w):
    return jnp.concatenate([w[:, 0:512], w[:, 768:1280], w[:, 1280:1792], w[:, 1792:2304],
                            w[:, 512:640], w[:, 640:768]], axis=1)


_L0_CHUNKS = (
    (0, 512, (0, 1, 2, 3), ()),
    (512, 1024, (), ()),
    (1024, 1536, (), ((2, 0, 512, 0),)),
    (1536, 2048, (), ((3, 0, 512, 0),)),
    (2048, 2304, (0,), ((0, 0, 128, 0), (1, 128, 256, 0))),
)
_L1_CHUNKS = (
    (0, 512, (0, 1, 2, 3), ()),
    (512, 1024, (0, 1, 2, 3), ()),
    (1024, 1536, (0, 1, 2, 3), ((0, 0, 512, 0),)),
    (1536, 2048, (0, 1, 2, 3), ((0, 0, 512, 512),)),
    (2048, 2560, (), ((1, 0, 512, 0),)),
    (2560, 3072, (), ((1, 0, 512, 512),)),
)


def kernel(x_prompt, x_sample, cache_a_k, cache_a_v, cache_b_k, cache_b_v, cache_c_k, cache_c_v, c, c_ctx, w_mod, b_mod, norm_mix, norm_ffn, w_in_ab, w_out_ab, sink_a, rel_bias_b, w_in_c, w_out_c, lam_q1, lam_k1, lam_q2, lam_k2, subln_c, router_w, router_bias, exp_w_gate, exp_w_up, exp_w_down, sh_w_gate, sh_w_up, sh_w_down, final_norm):
    x = jnp.concatenate([x_prompt.reshape(N_PROMPT, D_MODEL), x_sample.reshape(N_SAMPLE, D_MODEL)], axis=0)
    cond8 = jnp.concatenate([c_ctx[None, :], c, jnp.zeros((8 - 1 - N_SAMPLE_BATCH, D_MODEL), f32)], axis=0)
    mod = _adaln(cond8, w_mod, b_mod).reshape(DEPTH, 8, 6, D_MODEL)
    rope_tabs = _rope_tables()
    new_kv = {}
    for layer in range(DEPTH):
        li = layer // 2
        mod_l = mod[layer]
        g_mix = norm_mix[layer][None, :]
        g_ffn = norm_ffn[layer][None, :]
        if layer % 2 == 0:
            w_in = _permute_w_in_ab(w_in_ab[li]).astype(bf16)
            qkv, ak, av, bk, bv = _inproj(x, mod_l, g_mix, w_in, rope_tabs, _L0_CHUNKS, (128, 128, 512, 512))
            new_kv["a_k"], new_kv["a_v"], new_kv["b_k"], new_kv["b_v"] = ak, av, bk, bv
            o_p = _ctx0(sink_a[li], qkv)
            o_s = _lat0(sink_a[li], qkv,
                        cache_a_k[:, li].reshape(N_SAMPLE_BATCH, PAST_LEN, LANES),
                        cache_a_v[:, li].reshape(N_SAMPLE_BATCH, PAST_LEN, LANES),
                        cache_b_k[:, li].reshape(N_SAMPLE_BATCH, PAST_LEN, 512),
                        cache_b_v[:, li].reshape(N_SAMPLE_BATCH, PAST_LEN, 512),
                        _na_bias_table(rel_bias_b[li]))
            w_out = w_out_ab[li].astype(bf16)
        else:
            lam_init = 0.8 - 0.6 * math.exp(-0.3 * layer)
            qkv, ck, cv = _inproj(x, mod_l, g_mix, w_in_c[li].astype(bf16), rope_tabs, _L1_CHUNKS, (1024, 1024))
            new_kv["c_k"], new_kv["c_v"] = ck, cv
            lamv = jnp.concatenate([lam_q1[li][None], lam_k1[li][None], lam_q2[li][None], lam_k2[li][None],
                                    jnp.zeros((4, HEAD_DIM), f32)], axis=0)
            subln = subln_c[li][None, :]
            o_p = _ctx1(lamv, subln, qkv, lam_init)
            o_s = _lat1(lamv, subln, qkv,
                        cache_c_k[:, li].reshape(N_SAMPLE_BATCH, PAST_LEN, D_MODEL),
                        cache_c_v[:, li].reshape(N_SAMPLE_BATCH, PAST_LEN, D_MODEL), lam_init)
            w_out = w_out_c[li].astype(bf16)
        x = _outproj(x, o_p, o_s, mod_l, w_out)
        x = _moe(x, mod_l, g_ffn, router_w[layer].T, router_bias[layer][:, None],
                 exp_w_gate[layer].astype(bf16), exp_w_up[layer].astype(bf16), exp_w_down[layer].astype(bf16),
                 sh_w_gate[layer].astype(bf16), sh_w_up[layer].astype(bf16), sh_w_down[layer].astype(bf16))
    y_prompt, y_sample = _final(x, final_norm[None, :])
    nb, s = N_PROMPT_BATCH, PROMPT_SEQ
    return (y_prompt.reshape(nb, s, D_MODEL), y_sample.reshape(N_SAMPLE_BATCH, SAMPLE_SEQ, D_MODEL),
            new_kv["a_k"].reshape(nb, 1, s, 2, HEAD_DIM), new_kv["a_v"].reshape(nb, 1, s, 2, HEAD_DIM),
            new_kv["b_k"].reshape(nb, 1, s, 8, HEAD_DIM), new_kv["b_v"].reshape(nb, 1, s, 8, HEAD_DIM),
            new_kv["c_k"].reshape(nb, 1, s, 8, 2, HEAD_DIM), new_kv["c_v"].reshape(nb, 1, s, 8, 2 * HEAD_DIM))
```

```python
import functools
import math

import jax
import jax.numpy as jnp
import numpy as np
from jax import lax
from jax.experimental import pallas as pl
from jax.experimental.pallas import tpu as pltpu

f32 = jnp.float32
bf16 = jnp.bfloat16
i32 = jnp.int32

D_MODEL = 1024
N_PROMPT_BATCH = 16
PROMPT_SEQ = 256
DEPTH = 2
N_SAMPLE_BATCH = 2
SAMPLE_SEQ = 2048
PAST_LEN = 512
GRID_W = 64
HEAD_DIM = 64
ROPE_THETA = 10000.0
EPS = 1e-6
A_WINDOW = 128
NA_ROWS = 8
NA_COLS = 16
N_EXPERTS = 64
TOP_K = 6
N_GROUPS = 8
TOPK_GROUPS = 4
EXPERT_DIM = 256
ROUTED_SCALE = 2.5
Q_SCALE = HEAD_DIM ** -0.5

N_PROMPT = N_PROMPT_BATCH * PROMPT_SEQ
N_SAMPLE = N_SAMPLE_BATCH * SAMPLE_SEQ
N_TOK = N_PROMPT + N_SAMPLE

LANES = 128
TM = 256
N_PROMPT_TILES = N_PROMPT // TM
N_TILES = N_TOK // TM
TILES_PER_SAMPLE = SAMPLE_SEQ // TM
QB = 128
CHUNK = 16
SLOTS = -(-(TM * TOP_K + N_EXPERTS * (CHUNK - 1)) // 256) * 256
SLOT_CHUNKS = SLOTS // CHUNK
XW = D_MODEL + LANES
GM = 256
_MAX_SORTED = TM * TOP_K * N_TILES + N_TILES * N_EXPERTS * (CHUNK - 1) + N_EXPERTS * (GM - CHUNK)
G_TILES = -(-_MAX_SORTED // GM)
G_CHUNKS = GM // CHUNK
VMEM_LIMIT = 56 * 1024 * 1024

NEG = -1e30


def _cparams(sem):
    return pltpu.CompilerParams(dimension_semantics=sem, vmem_limit_bytes=VMEM_LIMIT)


def _mod_row(i):
    return jnp.where(i < N_PROMPT_TILES, 0, 1 + (i - N_PROMPT_TILES) // TILES_PER_SAMPLE)


def _x_specs(parts):
    if len(parts) == 1:
        return [pl.BlockSpec((TM, D_MODEL), lambda i, *_: (i, 0))]
    return [pl.BlockSpec((TM, D_MODEL), lambda i, *_: (jnp.minimum(i, N_PROMPT_TILES - 1), 0)),
            pl.BlockSpec((TM, D_MODEL), lambda i, *_: (jnp.maximum(i - N_PROMPT_TILES, 0), 0))]


def _load_x(i, x_refs):
    if len(x_refs) == 1:
        return x_refs[0][...]
    return jnp.where(i < N_PROMPT_TILES, x_refs[0][...], x_refs[1][...])


def _norm_mod(x, g, scale, shift):
    y = x * lax.rsqrt(jnp.mean(x * x, axis=-1, keepdims=True) + EPS)
    return (y * g) * (1.0 + scale) + shift


def _silu(x):
    return x * jax.nn.sigmoid(x)


def _dot(a, b):
    return jnp.dot(a, b, preferred_element_type=f32)


def _dot_nt(a, b):
    return lax.dot_general(a, b, (((1,), (1,)), ((), ())), preferred_element_type=f32)


ADA_COLS = 1536


def _adaln_kernel(cond_ref, w_ref, b_ref, o_ref):
    s = _silu(cond_ref[...]).astype(bf16)
    o_ref[...] = _dot(s, w_ref[...].astype(bf16)) + b_ref[...]


def _adaln(cond8, w_mod, b_mod):
    n6 = 6 * D_MODEL
    return pl.pallas_call(
        _adaln_kernel,
        grid=(DEPTH, n6 // ADA_COLS),
        in_specs=[
            pl.BlockSpec((8, D_MODEL), lambda l, j: (0, 0)),
            pl.BlockSpec((None, D_MODEL, ADA_COLS), lambda l, j: (l, 0, j)),
            pl.BlockSpec((None, 1, ADA_COLS), lambda l, j: (l, 0, j)),
        ],
        out_specs=pl.BlockSpec((None, 8, ADA_COLS), lambda l, j: (l, 0, j)),
        out_shape=jax.ShapeDtypeStruct((DEPTH, 8, n6), f32),
        compiler_params=_cparams(("parallel", "parallel")),
        name="adaln",
    )(cond8, w_mod, b_mod.reshape(DEPTH, 1, n6))


def _rope_block(blk, cos, sin_a, sin_b):
    return blk * cos + pltpu.roll(blk, LANES - 16, 1) * sin_a + pltpu.roll(blk, 16, 1) * sin_b


def _inproj_kernel(*refs, chunks, n_x):
    x_refs, kv_refs = refs[:n_x], refs[n_x + 7:]
    mod_ref, g_ref, w_ref, cos_ref, sa_ref, sb_ref, qkv_ref = refs[n_x:n_x + 7]
    i = pl.program_id(0)
    h = _norm_mod(_load_x(i, x_refs), g_ref[...], mod_ref[1:2, :], mod_ref[0:1, :]).astype(bf16)
    is_prompt = i < N_PROMPT_TILES
    for c0, c1, rope_blocks, kv_out in chunks:
        acc = _dot(h, w_ref[:, c0:c1])
        if rope_blocks:
            @pl.when(jnp.logical_not(is_prompt))
            def _():
                cos, sa, sb = cos_ref[...], sa_ref[...], sb_ref[...]
                for b in range((c1 - c0) // LANES):
                    blk = acc[:, b * LANES:(b + 1) * LANES]
                    if b in rope_blocks:
                        blk = _rope_block(blk, cos, sa, sb)
                    qkv_ref[:, c0 + b * LANES:c0 + (b + 1) * LANES] = blk.astype(bf16)

            @pl.when(is_prompt)
            def _():
                qkv_ref[:, c0:c1] = acc.astype(bf16)
        else:
            qkv_ref[:, c0:c1] = acc.astype(bf16)
        if kv_out:
            @pl.when(is_prompt)
            def _():
                for ridx, a0, a1, o0 in kv_out:
                    kv_refs[ridx][:, o0:o0 + (a1 - a0)] = acc[:, a0:a1]


def _inproj(x_parts, mod_l, g, w, rope_tabs, chunks, kv_widths):
    n = w.shape[1]
    cos, sa, sb = rope_tabs

    def rope_idx(i):
        return (jnp.where(i < N_PROMPT_TILES, 0, (i - N_PROMPT_TILES) % TILES_PER_SAMPLE), 0)

    def kv_idx(i):
        return (jnp.minimum(i, N_PROMPT_TILES - 1), 0)

    return pl.pallas_call(
        functools.partial(_inproj_kernel, chunks=chunks, n_x=len(x_parts)),
        grid=(N_TILES,),
        in_specs=_x_specs(x_parts) + [
            pl.BlockSpec((None, 6, D_MODEL), lambda i: (_mod_row(i), 0, 0)),
            pl.BlockSpec((1, D_MODEL), lambda i: (0, 0)),
            pl.BlockSpec((D_MODEL, n), lambda i: (0, 0)),
            pl.BlockSpec((TM, LANES), rope_idx),
            pl.BlockSpec((TM, LANES), rope_idx),
            pl.BlockSpec((TM, LANES), rope_idx),
        ],
        out_specs=[pl.BlockSpec((TM, n), lambda i: (i, 0))]
        + [pl.BlockSpec((TM, wd), kv_idx) for wd in kv_widths],
        out_shape=[jax.ShapeDtypeStruct((N_TOK, n), bf16)]
        + [jax.ShapeDtypeStruct((N_PROMPT, wd), f32) for wd in kv_widths],
        compiler_params=_cparams(("arbitrary",)),
        name="inproj",
    )(*x_parts, mod_l, g, w, cos, sa, sb)


def _rope_tables():
    nq = HEAD_DIM // 4
    t = jnp.arange(SAMPLE_SEQ)
    inv = jnp.power(ROPE_THETA, -jnp.arange(nq, dtype=f32) / nq)
    ang_r = (t // GRID_W).astype(f32)[:, None] * inv
    ang_c = (t % GRID_W).astype(f32)[:, None] * inv
    zero = jnp.zeros_like(ang_r)

    def head(fr, fc):
        return jnp.concatenate([fr[0], fr[1], fc[0], fc[1]], axis=-1)

    cos = head((jnp.cos(ang_r), jnp.cos(ang_r)), (jnp.cos(ang_c), jnp.cos(ang_c)))
    sin_a = head((-jnp.sin(ang_r), zero), (-jnp.sin(ang_c), zero))
    sin_b = head((zero, jnp.sin(ang_r)), (zero, jnp.sin(ang_c)))
    two = lambda a: jnp.concatenate([a, a], axis=-1)
    return two(cos), two(sin_a), two(sin_b)


def _lane_lo(shape):
    return lax.broadcasted_iota(i32, shape, len(shape) - 1) < HEAD_DIM


def _half(q, lo_mask, half):
    keep = lo_mask if half == 0 else jnp.logical_not(lo_mask)
    return jnp.where(keep, q, jnp.zeros_like(q)) * Q_SCALE


def _swap_halves(x):
    return pltpu.roll(x.astype(f32), HEAD_DIM, 1).astype(x.dtype)


def _softmax_pv(score_blocks, value_blocks, sink=None):
    m = functools.reduce(jnp.maximum, [jnp.max(s, axis=-1, keepdims=True) for s in score_blocks])
    if sink is not None:
        m = jnp.maximum(m, sink)
    es = [jnp.exp(s - m) for s in score_blocks]
    den = functools.reduce(lambda a, b: a + b, [jnp.sum(e, axis=-1, keepdims=True) for e in es])
    if sink is not None:
        den = den + jnp.exp(sink - m)
    o = functools.reduce(lambda a, b: a + b, [_dot(e.astype(bf16), v) for e, v in zip(es, value_blocks)])
    return o * (1.0 / den)


L0_QA, L0_QB, L0_KB, L0_VB, L0_KA, L0_VA, L0_N = 0, 512, 1024, 1536, 2048, 2176, 2304


def _ctx0_kernel(sink_ref, qkv_ref, o_ref):
    lo = _lane_lo((1, LANES))
    k_a = qkv_ref[:, L0_KA:L0_KA + LANES]
    v_a = qkv_ref[:, L0_VA:L0_VA + LANES]
    k_a_sw, v_a_sw = _swap_halves(k_a), _swap_halves(v_a)
    for j in range(4):
        q_a = qkv_ref[:, L0_QA + j * LANES:L0_QA + (j + 1) * LANES]
        q_b = qkv_ref[:, L0_QB + j * LANES:L0_QB + (j + 1) * LANES]
        k_b = qkv_ref[:, L0_KB + j * LANES:L0_KB + (j + 1) * LANES]
        v_b = qkv_ref[:, L0_VB + j * LANES:L0_VB + (j + 1) * LANES]
        kv_head = j // 2
        outs_a, outs_b = [], []
        for half in range(2):
            k_use, v_use = (k_a, v_a) if kv_head == half else (k_a_sw, v_a_sw)
            s = _dot_nt(_half(q_a, lo, half), k_use)
            outs_a.append(_softmax_pv([s], [v_use], sink=sink_ref[2 * j + half]))
            s = _dot_nt(_half(q_b, lo, half), k_b)
            outs_b.append(_softmax_pv([s], [v_b]))
        o_ref[:, j * LANES:(j + 1) * LANES] = jnp.where(lo, outs_a[0], outs_a[1]).astype(bf16)
        o_ref[:, 512 + j * LANES:512 + (j + 1) * LANES] = jnp.where(lo, outs_b[0], outs_b[1]).astype(bf16)


def _ctx0(sink, qkv):
    return pl.pallas_call(
        _ctx0_kernel,
        grid=(N_PROMPT_BATCH,),
        in_specs=[
            pl.BlockSpec(memory_space=pltpu.SMEM),
            pl.BlockSpec((PROMPT_SEQ, L0_N), lambda b: (b, 0)),
        ],
        out_specs=pl.BlockSpec((PROMPT_SEQ, D_MODEL), lambda b: (b, 0)),
        out_shape=jax.ShapeDtypeStruct((N_PROMPT, D_MODEL), bf16),
        compiler_params=_cparams(("parallel",)),
        name="ctx0",
    )(sink, qkv)


WIN_KEYS = 3 * QB
NA_KEY_ROWS = 10
NA_KEYS = NA_KEY_ROWS * GRID_W
N_QB = SAMPLE_SEQ // QB
N_NA_PATTERNS = 5
_PROMPT_QBLOCKS = N_PROMPT // QB


def _na_pattern(n):
    return jnp.where(n < 2, n, jnp.where(n > N_QB - 3, n - (N_QB - 5), 2))


def _lat0_kernel(sink_ref, q_ref, kvb_ref, kva_ref, cak_ref, cav_ref, cbk_ref, cbv_ref, nab_ref, o_ref):
    n = pl.program_id(1)
    lo = _lane_lo((1, LANES))
    kstart = pl.multiple_of(jnp.clip((n - 1) * QB, 0, SAMPLE_SEQ - WIN_KEYS), QB)
    k_a = kva_ref[pl.ds(kstart, WIN_KEYS), 0:LANES]
    v_a = kva_ref[pl.ds(kstart, WIN_KEYS), LANES:2 * LANES]
    c_k = cak_ref[...].astype(bf16)
    c_v = cav_ref[...].astype(bf16)
    k_a_sw, v_a_sw, c_k_sw, c_v_sw = (_swap_halves(t) for t in (k_a, v_a, c_k, c_v))
    qpos = n * QB + lax.broadcasted_iota(i32, (QB, WIN_KEYS), 0)
    kpos = kstart + lax.broadcasted_iota(i32, (QB, WIN_KEYS), 1)
    in_window = jnp.abs(qpos - kpos) <= A_WINDOW
    krow = jnp.clip(2 * n - NA_ROWS // 2, 0, SAMPLE_SEQ // GRID_W - NA_KEY_ROWS)
    ktok = pl.multiple_of(krow * GRID_W, QB)
    for j in range(4):
        q_a = q_ref[:, L0_QA + j * LANES:L0_QA + (j + 1) * LANES]
        q_b = q_ref[:, L0_QB + j * LANES:L0_QB + (j + 1) * LANES]
        k_b = kvb_ref[pl.ds(ktok, NA_KEYS), j * LANES:(j + 1) * LANES]
        v_b = kvb_ref[pl.ds(ktok, NA_KEYS), 512 + j * LANES:512 + (j + 1) * LANES]
        cb_k = cbk_ref[:, j * LANES:(j + 1) * LANES].astype(bf16)
        cb_v = cbv_ref[:, j * LANES:(j + 1) * LANES].astype(bf16)
        kv_head = j // 2
        outs_a, outs_b = [], []
        for half in range(2):
            same = kv_head == half
            qh = _half(q_a, lo, half)
            s_loc = jnp.where(in_window, _dot_nt(qh, k_a if same else k_a_sw), NEG)
            s_ctx = _dot_nt(qh, c_k if same else c_k_sw)
            outs_a.append(_softmax_pv([s_ctx, s_loc], [c_v if same else c_v_sw, v_a if same else v_a_sw],
                                      sink=sink_ref[2 * j + half]))
            qh = _half(q_b, lo, half)
            s_loc = _dot_nt(qh, k_b) + nab_ref[2 * j + half]
            s_ctx = _dot_nt(qh, cb_k)
            outs_b.append(_softmax_pv([s_ctx, s_loc], [cb_v, v_b]))
        o_ref[:, j * LANES:(j + 1) * LANES] = jnp.where(lo, outs_a[0], outs_a[1]).astype(bf16)
        o_ref[:, 512 + j * LANES:512 + (j + 1) * LANES] = jnp.where(lo, outs_b[0], outs_b[1]).astype(bf16)


def _lat0(sink, qkv, cak, cav, cbk, cbv, nab):
    sb = N_PROMPT // SAMPLE_SEQ
    return pl.pallas_call(
        _lat0_kernel,
        grid=(N_SAMPLE_BATCH, N_QB),
        in_specs=[
            pl.BlockSpec(memory_space=pltpu.SMEM),
            pl.BlockSpec((QB, 1024), lambda b, n: (_PROMPT_QBLOCKS + b * N_QB + n, 0)),
            pl.BlockSpec((SAMPLE_SEQ, 1024), lambda b, n: (sb + b, 1)),
            pl.BlockSpec((SAMPLE_SEQ, 256), lambda b, n: (sb + b, L0_KA // 256)),
            pl.BlockSpec((None, PAST_LEN, LANES), lambda b, n: (b, 0, 0)),
            pl.BlockSpec((None, PAST_LEN, LANES), lambda b, n: (b, 0, 0)),
            pl.BlockSpec((None, PAST_LEN, 512), lambda b, n: (b, 0, 0)),
            pl.BlockSpec((None, PAST_LEN, 512), lambda b, n: (b, 0, 0)),
            pl.BlockSpec((None, 8, QB, NA_KEYS), lambda b, n: (_na_pattern(n), 0, 0, 0)),
        ],
        out_specs=pl.BlockSpec((QB, D_MODEL), lambda b, n: (b * N_QB + n, 0)),
        out_shape=jax.ShapeDtypeStruct((N_SAMPLE, D_MODEL), bf16),
        compiler_params=_cparams(("parallel", "arbitrary")),
        name="lat0",
    )(sink, qkv, qkv, qkv, cak, cav, cbk, cbv, nab)


def _na_bias_table(rel_bias):
    rows = SAMPLE_SEQ // GRID_W
    n_dr, n_dc = 2 * NA_ROWS - 1, 2 * NA_COLS - 1
    c = np.arange(GRID_W)[:, None]
    kc = np.arange(GRID_W)[None, :]
    cs = np.clip(c - NA_COLS // 2, 0, GRID_W - NA_COLS)
    col_ok = (kc >= cs) & (kc < cs + NA_COLS)
    col_hot = ((kc - c + NA_COLS - 1)[None] == np.arange(n_dc)[:, None, None]) & col_ok[None]
    row_hot = np.zeros((N_NA_PATTERNS, 2, NA_KEY_ROWS, n_dr), np.float32)
    for p, n in enumerate((0, 1, 2, N_QB - 2, N_QB - 1)):
        k0 = int(np.clip(2 * n - NA_ROWS // 2, 0, rows - NA_KEY_ROWS))
        for rq in range(2):
            r = 2 * n + rq
            rs = int(np.clip(r - NA_ROWS // 2, 0, rows - NA_ROWS))
            for kl in range(NA_KEY_ROWS):
                if rs <= k0 + kl < rs + NA_ROWS:
                    row_hot[p, rq, kl, k0 + kl - r + NA_ROWS - 1] = 1.0
    ok = (row_hot.sum(-1) > 0)[:, None, :, None, :, None] & col_ok[None, None, None, :, None, :]
    hp = lax.Precision.HIGHEST
    toeplitz = jnp.einsum("hdx,xck->hdck", rel_bias.astype(f32), col_hot.astype(np.float32), precision=hp)
    tab = jnp.einsum("prkd,hdcx->phrckx", row_hot, toeplitz, precision=hp)
    tab = tab + np.where(ok, 0.0, NEG).astype(np.float32)
    return tab.reshape(N_NA_PATTERNS, 8, QB, NA_KEYS)


def _diff_lambda(lam_ref, lam_init):
    lv = lam_ref[...]
    s1 = jnp.sum(lv[0:1, :] * lv[1:2, :], axis=-1, keepdims=True)
    s2 = jnp.sum(lv[2:3, :] * lv[3:4, :], axis=-1, keepdims=True)
    return jnp.exp(s1) - jnp.exp(s2) + lam_init


def _softmax_parts(score_blocks):
    m = functools.reduce(jnp.maximum, [jnp.max(s, axis=-1, keepdims=True) for s in score_blocks])
    es = [jnp.exp(s - m) for s in score_blocks]
    den = functools.reduce(lambda a, b: a + b, [jnp.sum(e, axis=-1, keepdims=True) for e in es])
    return es, 1.0 / den


def _diff_head(q, key_blocks, value_blocks, lam, subln, lo, lam_init):
    q1, q2 = _half(q, lo, 0), _half(q, lo, 1)
    e1, r1 = _softmax_parts([_dot_nt(q1, k) for k in key_blocks])
    e2, r2 = _softmax_parts([_dot_nt(q2, k) for k in key_blocks])
    r2 = r2 * lam
    o = functools.reduce(lambda a, b: a + b,
                         [_dot((a * r1 - b * r2).astype(bf16), v) for a, b, v in zip(e1, e2, value_blocks)])
    o = o * lax.rsqrt(jnp.mean(o * o, axis=-1, keepdims=True) + EPS)
    return (o * subln) * (1.0 - lam_init)


def _ctx1_kernel(lam_ref, subln_ref, qkv_ref, o_ref, *, lam_init):
    lo = _lane_lo((1, LANES))
    lam = _diff_lambda(lam_ref, lam_init)
    subln = subln_ref[...]
    for h in range(8):
        q = qkv_ref[:, h * LANES:(h + 1) * LANES]
        k = qkv_ref[:, 1024 + h * LANES:1024 + (h + 1) * LANES]
        v = qkv_ref[:, 2048 + h * LANES:2048 + (h + 1) * LANES]
        o_ref[:, h * LANES:(h + 1) * LANES] = _diff_head(q, [k], [v], lam, subln, lo, lam_init).astype(bf16)


def _ctx1(lamv, subln, qkv, lam_init):
    return pl.pallas_call(
        functools.partial(_ctx1_kernel, lam_init=lam_init),
        grid=(N_PROMPT_BATCH,),
        in_specs=[
            pl.BlockSpec((8, HEAD_DIM), lambda b: (0, 0)),
            pl.BlockSpec((1, LANES), lambda b: (0, 0)),
            pl.BlockSpec((PROMPT_SEQ, 3 * D_MODEL), lambda b: (b, 0)),
        ],
        out_specs=pl.BlockSpec((PROMPT_SEQ, D_MODEL), lambda b: (b, 0)),
        out_shape=jax.ShapeDtypeStruct((N_PROMPT, D_MODEL), bf16),
        compiler_params=_cparams(("parallel",)),
        name="ctx1",
    )(lamv, subln, qkv)


def _lat1_kernel(lam_ref, subln_ref, q_ref, k_ref, v_ref, ck_ref, cv_ref, o_ref, *, lam_init):
    lo = _lane_lo((1, LANES))
    lam = _diff_lambda(lam_ref, lam_init)
    subln = subln_ref[...]
    for h in range(8):
        sl = slice(h * LANES, (h + 1) * LANES)
        o_ref[:, sl] = _diff_head(q_ref[:, sl], [ck_ref[:, sl].astype(bf16), k_ref[:, sl]],
                                  [cv_ref[:, sl].astype(bf16), v_ref[:, sl]],
                                  lam, subln, lo, lam_init).astype(bf16)


def _lat1(lamv, subln, qkv, ck, cv, lam_init):
    sb = N_PROMPT // SAMPLE_SEQ
    nq = SAMPLE_SEQ // TM
    return pl.pallas_call(
        functools.partial(_lat1_kernel, lam_init=lam_init),
        grid=(N_SAMPLE_BATCH, nq),
        in_specs=[
            pl.BlockSpec((8, HEAD_DIM), lambda b, n: (0, 0)),
            pl.BlockSpec((1, LANES), lambda b, n: (0, 0)),
            pl.BlockSpec((TM, D_MODEL), lambda b, n: (N_PROMPT_TILES + b * nq + n, 0)),
            pl.BlockSpec((SAMPLE_SEQ, D_MODEL), lambda b, n: (sb + b, 1)),
            pl.BlockSpec((SAMPLE_SEQ, D_MODEL), lambda b, n: (sb + b, 2)),
            pl.BlockSpec((None, PAST_LEN, D_MODEL), lambda b, n: (b, 0, 0)),
            pl.BlockSpec((None, PAST_LEN, D_MODEL), lambda b, n: (b, 0, 0)),
        ],
        out_specs=pl.BlockSpec((TM, D_MODEL), lambda b, n: (b * nq + n, 0)),
        out_shape=jax.ShapeDtypeStruct((N_SAMPLE, D_MODEL), bf16),
        compiler_params=_cparams(("parallel", "arbitrary")),
        name="lat1",
    )(lamv, subln, qkv, qkv, qkv, ck, cv)


def _outproj_kernel(*refs, n_x):
    x_refs = refs[:n_x]
    op_ref, os_ref, mod_ref, w_ref, o_ref = refs[n_x:]
    i = pl.program_id(0)
    attn = jnp.where(i < N_PROMPT_TILES, op_ref[...], os_ref[...])
    o_ref[...] = _load_x(i, x_refs) + mod_ref[2:3, :] * _dot(attn, w_ref[...])


def _outproj(x_parts, o_prompt, o_sample, mod_l, w):
    return pl.pallas_call(
        functools.partial(_outproj_kernel, n_x=len(x_parts)),
        grid=(N_TILES,),
        in_specs=_x_specs(x_parts) + [
            pl.BlockSpec((TM, D_MODEL), lambda i: (jnp.minimum(i, N_PROMPT_TILES - 1), 0)),
            pl.BlockSpec((TM, D_MODEL), lambda i: (jnp.maximum(i - N_PROMPT_TILES, 0), 0)),
            pl.BlockSpec((None, 6, D_MODEL), lambda i: (_mod_row(i), 0, 0)),
            pl.BlockSpec((D_MODEL, D_MODEL), lambda i: (0, 0)),
        ],
        out_specs=pl.BlockSpec((TM, D_MODEL), lambda i: (i, 0)),
        out_shape=jax.ShapeDtypeStruct((N_TOK, D_MODEL), f32),
        compiler_params=_cparams(("parallel",)),
        name="outproj",
    )(*x_parts, o_prompt, o_sample, mod_l, w)


def _route_kernel(x_ref, mod_ref, g_ref, rwt_ref, rb_ref, xloc_ref, slots_ref, len_ref):
    ng, ge = N_GROUPS, N_EXPERTS // N_GROUPS
    h = _norm_mod(x_ref[...], g_ref[...], mod_ref[4:5, :], mod_ref[3:4, :])
    logits = lax.dot_general(rwt_ref[...], h, (((1,), (1,)), ((), ())),
                             precision=lax.Precision.HIGHEST, preferred_element_type=f32)
    scores = jax.nn.sigmoid(logits)
    biased = scores + rb_ref[...]
    s3 = scores.reshape(ng, ge, TM)
    b3 = biased.reshape(ng, ge, TM)
    in_group = lax.broadcasted_iota(i32, (ng, ge, TM), 1).astype(f32)
    group_id = lax.broadcasted_iota(i32, (ng, 1, TM), 0).astype(f32)
    expert_id = lax.broadcasted_iota(i32, (ng, ge, TM), 0).astype(f32) * ge + in_group

    def max01(a):
        return jnp.max(jnp.max(a, axis=0, keepdims=True), axis=1, keepdims=True)

    def min01(a):
        return jnp.min(jnp.min(a, axis=0, keepdims=True), axis=1, keepdims=True)

    def sum01(a):
        return jnp.sum(jnp.sum(a, axis=0, keepdims=True), axis=1, keepdims=True)

    m1 = jnp.max(b3, axis=1, keepdims=True)
    first = jnp.min(jnp.where(b3 == m1, in_group, ge), axis=1, keepdims=True)
    m2 = jnp.max(jnp.where(in_group == first, -jnp.inf, b3), axis=1, keepdims=True)
    gscore = m1 + m2
    gsel = jnp.zeros((ng, 1, TM), f32)
    for _ in range(TOPK_GROUPS):
        gm = jnp.max(gscore, axis=0, keepdims=True)
        gi = jnp.min(jnp.where(gscore == gm, group_id, ng), axis=0, keepdims=True)
        hit = group_id == gi
        gsel = jnp.where(hit, 1.0, gsel)
        gscore = jnp.where(hit, -jnp.inf, gscore)
    cand = jnp.where(jnp.broadcast_to(gsel, (ng, ge, TM)) > 0.0, b3, -jnp.inf)
    top_e, top_w = [], []
    for _ in range(TOP_K):
        em = max01(cand)
        ei = min01(jnp.where(cand == em, expert_id, N_EXPERTS))
        hit = expert_id == ei
        top_e.append(ei)
        top_w.append(sum01(jnp.where(hit, s3, 0.0)))
        cand = jnp.where(hit, -jnp.inf, cand)
    wsum = functools.reduce(lambda a, b: a + b, top_w)
    gates3 = jnp.zeros((ng, ge, TM), f32)
    sel3 = jnp.zeros((ng, ge, TM), f32)
    for ei, w in zip(top_e, top_w):
        hit = expert_id == ei
        gates3 = jnp.where(hit, w / wsum * ROUTED_SCALE, gates3)
        sel3 = jnp.where(hit, 1.0, sel3)
    gates = gates3.reshape(N_EXPERTS, TM)
    sel = sel3.reshape(N_EXPERTS, TM)

    cnt = jnp.sum(sel, axis=1, keepdims=True)
    run_len = jnp.ceil(cnt * (1.0 / CHUNK)) * CHUNK
    r_i = lax.broadcasted_iota(i32, (N_EXPERTS, N_EXPERTS), 0)
    c_i = lax.broadcasted_iota(i32, (N_EXPERTS, N_EXPERTS), 1)
    lower = jnp.where(c_i < r_i, 1.0, 0.0).astype(bf16)
    run_off = _dot(lower, jnp.broadcast_to(run_len, (N_EXPERTS, LANES)).astype(bf16))[:, 0:1]
    t_r = lax.broadcasted_iota(i32, (TM, TM), 0)
    t_c = lax.broadcasted_iota(i32, (TM, TM), 1)
    before = jnp.where(t_r < t_c, 1.0, 0.0).astype(bf16)
    rank = _dot(sel.astype(bf16), before)
    slot3 = (run_off + rank).reshape(ng, ge, TM)
    slots = [sum01(jnp.where(expert_id == ei, slot3, 0.0)).reshape(1, TM).astype(i32) for ei in top_e]
    for k in range(TOP_K):
        slots_ref[k:k + 1, :] = slots[k]
    slots_ref[TOP_K:8, :] = jnp.full((8 - TOP_K, TM), -1, i32)
    len_ref[...] = jnp.broadcast_to(run_len, (N_EXPERTS, LANES)).astype(i32)

    hb = h.astype(bf16)
    g_hi = gates.astype(bf16)
    g_lo = (gates - g_hi.astype(f32)).astype(bf16)
    g_parts = jnp.concatenate([g_hi, g_lo], axis=0)
    rows = 256

    def body(c, carry):
        base = pl.multiple_of(c * rows, rows)
        row_id = base + lax.broadcasted_iota(i32, (rows, TM), 0)
        p = jnp.zeros((rows, TM), f32)
        for k in range(TOP_K):
            p = jnp.where(row_id == slots[k], 1.0, p)
        p = p.astype(bf16)
        xloc_ref[pl.ds(base, rows), 0:D_MODEL] = _dot(p, hb).astype(bf16)
        xloc_ref[pl.ds(base, rows), D_MODEL:XW] = _dot_nt(p, g_parts).astype(bf16)
        return carry

    lax.fori_loop(0, SLOTS // rows, body, 0)


def _route(x, mod_l, g, rwt, rb):
    return pl.pallas_call(
        _route_kernel,
        grid=(N_TILES,),
        in_specs=[
            pl.BlockSpec((TM, D_MODEL), lambda i: (i, 0)),
            pl.BlockSpec((None, 6, D_MODEL), lambda i: (_mod_row(i), 0, 0)),
            pl.BlockSpec((1, D_MODEL), lambda i: (0, 0)),
            pl.BlockSpec((N_EXPERTS, D_MODEL), lambda i: (0, 0)),
            pl.BlockSpec((N_EXPERTS, 1), lambda i: (0, 0)),
        ],
        out_specs=[
            pl.BlockSpec((SLOTS, XW), lambda i: (i, 0)),
            pl.BlockSpec((None, 8, TM), lambda i: (i, 0, 0)),
            pl.BlockSpec((None, N_EXPERTS, LANES), lambda i: (i, 0, 0)),
        ],
        out_shape=[
            jax.ShapeDtypeStruct((N_TILES * SLOTS, XW), bf16),
            jax.ShapeDtypeStruct((N_TILES, 8, TM), i32),
            jax.ShapeDtypeStruct((N_TILES, N_EXPERTS, LANES), i32),
        ],
        compiler_params=_cparams(("parallel",)),
        name="route",
    )(x, mod_l, g, rwt, rb)


def _moe_plan(run_len):
    nt, ne = run_len.shape

    def excl_cumsum(a):
        n = a.shape[-1]
        earlier = np.arange(n)[None, :] < np.arange(n)[:, None]
        return jnp.sum(jnp.where(earlier, a[..., None, :], 0), axis=-1)

    def first_diff(a):
        return a - jnp.concatenate([jnp.zeros_like(a[..., :1]), a[..., :-1]], axis=-1)

    off_loc = excl_cumsum(run_len)
    before = excl_cumsum(run_len.T).T
    n_e = jnp.sum(run_len, axis=0)
    n_pad = -(-n_e // GM) * GM
    g_start = excl_cumsum(n_pad)
    total = jnp.sum(n_pad)
    run_dst = g_start[None, :] + before
    run_src = jnp.arange(nt, dtype=i32)[:, None] * SLOTS + off_loc
    dst_f = run_dst.T.reshape(-1)
    shift_f = first_diff((run_src - run_dst).T.reshape(-1))
    rows = jnp.arange(G_TILES * G_CHUNKS, dtype=i32) * CHUNK
    shift = jnp.sum(jnp.where(dst_f[None, :] <= rows[:, None], shift_f[None, :], 0), axis=1)
    in_run = jnp.any((g_start[None, :] <= rows[:, None]) & (rows[:, None] < (g_start + n_e)[None, :]), axis=1)
    chunk_src = jnp.where(in_run, rows + shift, 0).astype(i32)
    tile_rows = jnp.arange(G_TILES, dtype=i32) * GM
    tile_valid = (tile_rows < total).astype(i32)
    tile_expert = jnp.sum((g_start[None, :] <= tile_rows[:, None]).astype(i32), axis=1) - 1
    tile_expert = jnp.where(tile_valid == 1, tile_expert, ne - 1).astype(i32)
    loc_rows = jnp.arange(SLOT_CHUNKS, dtype=i32) * CHUNK
    shift_l = first_diff(run_dst - off_loc)
    shift = jnp.sum(jnp.where(off_loc[:, None, :] <= loc_rows[None, :, None], shift_l[:, None, :], 0), axis=2)
    used = jnp.sum(run_len, axis=1)
    chunk_map = jnp.where(loc_rows[None, :] < used[:, None], (loc_rows[None, :] + shift) // CHUNK, -1).astype(i32)
    return tile_expert, tile_valid, chunk_src, chunk_map.reshape(-1)


def _gmm_copy(xloc_hbm, xbuf, sem, src_row, slot, c):
    return pltpu.make_async_copy(xloc_hbm.at[pl.ds(src_row, CHUNK)],
                                 xbuf.at[slot, pl.ds(c * CHUNK, CHUNK)], sem.at[slot])


def _gmm_kernel(te_ref, tv_ref, cs_ref, xloc_hbm, wg_ref, wu_ref, wd_ref, y_ref, xbuf, wg_b, wu_b, wd_b, sem):
    i = pl.program_id(0)
    slot = i % 2

    def start(tile, s):
        for c in range(G_CHUNKS):
            src = pl.multiple_of(cs_ref[tile * G_CHUNKS + c], CHUNK)
            _gmm_copy(xloc_hbm, xbuf, sem, src, s, c).start()

    valid = tv_ref[i] == 1

    @pl.when(jnp.logical_and(i == 0, valid))
    def _():
        start(0, 0)

    nxt = jnp.minimum(i + 1, pl.num_programs(0) - 1)

    @pl.when(jnp.logical_and(i + 1 < pl.num_programs(0), tv_ref[nxt] == 1))
    def _():
        start(i + 1, 1 - slot)

    e = te_ref[i]

    @pl.when(jnp.logical_and(valid, jnp.logical_or(i == 0, e != te_ref[jnp.maximum(i - 1, 0)])))
    def _():
        wg_b[...] = wg_ref[...].astype(bf16)
        wu_b[...] = wu_ref[...].astype(bf16)
        wd_b[...] = wd_ref[...].astype(bf16)

    @pl.when(valid)
    def _():
        for c in range(G_CHUNKS):
            _gmm_copy(xloc_hbm, xbuf, sem, 0, slot, c).wait()
        lane = lax.broadcasted_iota(i32, (1, LANES), 1)
        gate_lanes = jnp.logical_or(lane == e, lane == e + N_EXPERTS)
        half = GM // 2
        for r in range(2):
            x = xbuf[slot, r * half:(r + 1) * half, :]
            gate = jnp.sum(jnp.where(gate_lanes, x[:, D_MODEL:XW].astype(f32), 0.0), axis=1, keepdims=True)
            xa = x[:, 0:D_MODEL]
            act = _silu(_dot(xa, wg_b[...])) * _dot(xa, wu_b[...]) * gate
            y_ref[r * half:(r + 1) * half, :] = _dot(act.astype(bf16), wd_b[...]).astype(bf16)

    @pl.when(jnp.logical_not(valid))
    def _():
        y_ref[...] = jnp.zeros(y_ref.shape, y_ref.dtype)


def _gmm(tile_expert, tile_valid, chunk_src, xloc, wg, wu, wd, layer):
    grid_spec = pltpu.PrefetchScalarGridSpec(
        num_scalar_prefetch=3,
        grid=(G_TILES,),
        in_specs=[
            pl.BlockSpec(memory_space=pl.ANY),
            pl.BlockSpec((None, None, D_MODEL, EXPERT_DIM), lambda i, te, tv, cs: (layer, te[i], 0, 0)),
            pl.BlockSpec((None, None, D_MODEL, EXPERT_DIM), lambda i, te, tv, cs: (layer, te[i], 0, 0)),
            pl.BlockSpec((None, None, EXPERT_DIM, D_MODEL), lambda i, te, tv, cs: (layer, te[i], 0, 0)),
        ],
        out_specs=pl.BlockSpec((GM, D_MODEL), lambda i, te, tv, cs: (i, 0)),
        scratch_shapes=[pltpu.VMEM((2, GM, XW), bf16),
                        pltpu.VMEM((D_MODEL, EXPERT_DIM), bf16), pltpu.VMEM((D_MODEL, EXPERT_DIM), bf16),
                        pltpu.VMEM((EXPERT_DIM, D_MODEL), bf16), pltpu.SemaphoreType.DMA((2,))],
    )
    return pl.pallas_call(
        _gmm_kernel,
        grid_spec=grid_spec,
        out_shape=jax.ShapeDtypeStruct((G_TILES * GM, D_MODEL), bf16),
        compiler_params=_cparams(("arbitrary",)),
        name="gmm",
    )(tile_expert, tile_valid, chunk_src, xloc, wg, wu, wd)


def _combine_copy(y_hbm, ybuf, sem, sorted_chunk, slot, c):
    return pltpu.make_async_copy(y_hbm.at[pl.ds(pl.multiple_of(sorted_chunk * CHUNK, CHUNK), CHUNK)],
                                 ybuf.at[slot, pl.ds(pl.multiple_of(c * CHUNK, CHUNK), CHUNK)], sem.at[slot])


def _combine_kernel(cm_ref, y_hbm, slots_ref, x_ref, mod_ref, g_ref, sg_ref, su_ref, sd_ref, o_ref, ybuf, sem):
    i = pl.program_id(0)
    slot = i % 2

    def for_chunks(tile, s, wait):
        def body(c, carry):
            sc = cm_ref[tile * SLOT_CHUNKS + c]

            @pl.when(sc >= 0)
            def _():
                cp = _combine_copy(y_hbm, ybuf, sem, sc, s, c)
                cp.wait() if wait else cp.start()

            return carry

        lax.fori_loop(0, SLOT_CHUNKS, body, 0)

    @pl.when(i == 0)
    def _():
        ybuf[...] = jnp.zeros(ybuf.shape, ybuf.dtype)
        for_chunks(0, 0, False)

    @pl.when(i + 1 < pl.num_programs(0))
    def _():
        for_chunks(i + 1, 1 - slot, False)

    for_chunks(i, slot, True)

    x = x_ref[...]
    hb = _norm_mod(x, g_ref[...], mod_ref[4:5, :], mod_ref[3:4, :]).astype(bf16)
    shared = _dot((_silu(_dot(hb, sg_ref[...])) * _dot(hb, su_ref[...])).astype(bf16), sd_ref[...])
    row_id = lax.broadcasted_iota(i32, (SLOTS, TM), 0)
    p = jnp.zeros((SLOTS, TM), f32)
    for k in range(TOP_K):
        p = jnp.where(row_id == slots_ref[k:k + 1, :], 1.0, p)
    routed = lax.dot_general(p.astype(bf16), ybuf[slot], (((0,), (0,)), ((), ())), preferred_element_type=f32)
    o_ref[...] = x + mod_ref[5:6, :] * (routed + shared)


def _combine(chunk_map, y, slots, x, mod_l, g, sg, su, sd):
    shd = sg.shape[1]
    grid_spec = pltpu.PrefetchScalarGridSpec(
        num_scalar_prefetch=1,
        grid=(N_TILES,),
        in_specs=[
            pl.BlockSpec(memory_space=pl.ANY),
            pl.BlockSpec((None, 8, TM), lambda i, cm: (i, 0, 0)),
            pl.BlockSpec((TM, D_MODEL), lambda i, cm: (i, 0)),
            pl.BlockSpec((None, 6, D_MODEL), lambda i, cm: (_mod_row(i), 0, 0)),
            pl.BlockSpec((1, D_MODEL), lambda i, cm: (0, 0)),
            pl.BlockSpec((D_MODEL, shd), lambda i, cm: (0, 0)),
            pl.BlockSpec((D_MODEL, shd), lambda i, cm: (0, 0)),
            pl.BlockSpec((shd, D_MODEL), lambda i, cm: (0, 0)),
        ],
        out_specs=pl.BlockSpec((TM, D_MODEL), lambda i, cm: (i, 0)),
        scratch_shapes=[pltpu.VMEM((2, SLOTS, D_MODEL), bf16), pltpu.SemaphoreType.DMA((2,))],
    )
    return pl.pallas_call(
        _combine_kernel,
        grid_spec=grid_spec,
        out_shape=jax.ShapeDtypeStruct((N_TOK, D_MODEL), f32),
        compiler_params=_cparams(("arbitrary",)),
        name="combine",
    )(chunk_map, y, slots, x, mod_l, g, sg, su, sd)


def _moe(x, mod_l, g, rwt, rb, wg, wu, wd, layer, sg, su, sd):
    xloc, slots, run_len = _route(x, mod_l, g, rwt, rb)
    tile_expert, tile_valid, chunk_src, chunk_map = _moe_plan(run_len[:, :, 0])
    y = _gmm(tile_expert, tile_valid, chunk_src, xloc, wg, wu, wd, layer)
    return _combine(chunk_map, y, slots, x, mod_l, g, sg, su, sd)


def _final_kernel(x_ref, g_ref, yp_ref, ys_ref):
    i = pl.program_id(0)
    x = x_ref[...]
    y = (x * lax.rsqrt(jnp.mean(x * x, axis=-1, keepdims=True) + EPS)) * g_ref[...]

    @pl.when(i < N_PROMPT_TILES)
    def _():
        yp_ref[...] = y

    @pl.when(i >= N_PROMPT_TILES)
    def _():
        ys_ref[...] = y


def _final(x, g):
    return pl.pallas_call(
        _final_kernel,
        grid=(N_TILES,),
        in_specs=[
            pl.BlockSpec((TM, D_MODEL), lambda i: (i, 0)),
            pl.BlockSpec((1, D_MODEL), lambda i: (0, 0)),
        ],
        out_specs=[
            pl.BlockSpec((TM, D_MODEL), lambda i: (jnp.minimum(i, N_PROMPT_TILES - 1), 0)),
            pl.BlockSpec((TM, D_MODEL), lambda i: (jnp.maximum(i - N_PROMPT_TILES, 0), 0)),
        ],
        out_shape=[jax.ShapeDtypeStruct((N_PROMPT, D_MODEL), f32), jax.ShapeDtypeStruct((N_SAMPLE, D_MODEL), f32)],
        compiler_params=_cparams(("arbitrary",)),
        name="final_norm",
    )(x, g)


def _permute_w_in_ab(w):
    return jnp.concatenate([w[:, 0:512], w[:, 768:1280], w[:, 1280:1792], w[:, 1792:2304],
                            w[:, 512:640], w[:, 640:768]], axis=1)


_L0_CHUNKS = (
    (0, 512, (0, 1, 2, 3), ()),
    (512, 1024, (), ()),
    (1024, 1536, (), ((2, 0, 512, 0),)),
    (1536, 2048, (), ((3, 0, 512, 0),)),
    (2048, 2304, (0,), ((0, 0, 128, 0), (1, 128, 256, 0))),
)
_L1_CHUNKS = (
    (0, 512, (0, 1, 2, 3), ()),
    (512, 1024, (0, 1, 2, 3), ()),
    (1024, 1536, (0, 1, 2, 3), ((0, 0, 512, 0),)),
    (1536, 2048, (0, 1, 2, 3), ((0, 0, 512, 512),)),
    (2048, 2560, (), ((1, 0, 512, 0),)),
    (2560, 3072, (), ((1, 0, 512, 512),)),
)


def kernel(x_prompt, x_sample, cache_a_k, cache_a_v, cache_b_k, cache_b_v, cache_c_k, cache_c_v, c, c_ctx, w_mod, b_mod, norm_mix, norm_ffn, w_in_ab, w_out_ab, sink_a, rel_bias_b, w_in_c, w_out_c, lam_q1, lam_k1, lam_q2, lam_k2, subln_c, router_w, router_bias, exp_w_gate, exp_w_up, exp_w_down, sh_w_gate, sh_w_up, sh_w_down, final_norm):
    x = (x_prompt.reshape(N_PROMPT, D_MODEL), x_sample.reshape(N_SAMPLE, D_MODEL))
    cond8 = jnp.concatenate([c_ctx[None, :], c, jnp.zeros((8 - 1 - N_SAMPLE_BATCH, D_MODEL), f32)], axis=0)
    mod = _adaln(cond8, w_mod, b_mod).reshape(DEPTH, 8, 6, D_MODEL)
    rope_tabs = _rope_tables()
    new_kv = {}
    for layer in range(DEPTH):
        li = layer // 2
        mod_l = mod[layer]
        g_mix = norm_mix[layer][None, :]
        g_ffn = norm_ffn[layer][None, :]
        if layer % 2 == 0:
            w_in = _permute_w_in_ab(w_in_ab[li]).astype(bf16)
            qkv, ak, av, bk, bv = _inproj(x, mod_l, g_mix, w_in, rope_tabs, _L0_CHUNKS, (128, 128, 512, 512))
            new_kv["a_k"], new_kv["a_v"], new_kv["b_k"], new_kv["b_v"] = ak, av, bk, bv
            o_p = _ctx0(sink_a[li], qkv)
            o_s = _lat0(sink_a[li], qkv,
                        cache_a_k[:, li].reshape(N_SAMPLE_BATCH, PAST_LEN, LANES),
                        cache_a_v[:, li].reshape(N_SAMPLE_BATCH, PAST_LEN, LANES),
                        cache_b_k[:, li].reshape(N_SAMPLE_BATCH, PAST_LEN, 512),
                        cache_b_v[:, li].reshape(N_SAMPLE_BATCH, PAST_LEN, 512),
                        _na_bias_table(rel_bias_b[li]))
            w_out = w_out_ab[li].astype(bf16)
        else:
            lam_init = 0.8 - 0.6 * math.exp(-0.3 * layer)
            qkv, ck, cv = _inproj(x, mod_l, g_mix, w_in_c[li].astype(bf16), rope_tabs, _L1_CHUNKS, (1024, 1024))
            new_kv["c_k"], new_kv["c_v"] = ck, cv
            lamv = jnp.concatenate([lam_q1[li][None], lam_k1[li][None], lam_q2[li][None], lam_k2[li][None],
                                    jnp.zeros((4, HEAD_DIM), f32)], axis=0)
            subln = subln_c[li][None, :]
            o_p = _ctx1(lamv, subln, qkv, lam_init)
            o_s = _lat1(lamv, subln, qkv,
                        cache_c_k[:, li].reshape(N_SAMPLE_BATCH, PAST_LEN, D_MODEL),
                        cache_c_v[:, li].reshape(N_SAMPLE_BATCH, PAST_LEN, D_MODEL), lam_init)
            w_out = w_out_c[li].astype(bf16)
        x = _outproj(x, o_p, o_s, mod_l, w_out)
        x = _moe(x, mod_l, g_ffn, router_w[layer].T, router_bias[layer][:, None],
                 exp_w_gate, exp_w_up, exp_w_down, layer,
                 sh_w_gate[layer].astype(bf16), sh_w_up[layer].astype(bf16), sh_w_down[layer].astype(bf16))
        x = (x,)
    y_prompt, y_sample = _final(x[0], final_norm[None, :])
    nb, s = N_PROMPT_BATCH, PROMPT_SEQ
    return (y_prompt.reshape(nb, s, D_MODEL), y_sample.reshape(N_SAMPLE_BATCH, SAMPLE_SEQ, D_MODEL),
            new_kv["a_k"].reshape(nb, 1, s, 2, HEAD_DIM), new_kv["a_v"].reshape(nb, 1, s, 2, HEAD_DIM),
            new_kv["b_k"].reshape(nb, 1, s, 8, HEAD_DIM), new_kv["b_v"].reshape(nb, 1, s, 8, HEAD_DIM),
            new_kv["c_k"].reshape(nb, 1, s, 8, 2, HEAD_DIM), new_kv["c_v"].reshape(nb, 1, s, 8, 2 * HEAD_DIM))
```

```python
import functools
import math

import jax
import jax.numpy as jnp
import numpy as np
from jax import lax
from jax.experimental import pallas as pl
from jax.experimental.pallas import tpu as pltpu

f32 = jnp.float32
bf16 = jnp.bfloat16
i32 = jnp.int32

D_MODEL = 1024
N_PROMPT_BATCH = 16
PROMPT_SEQ = 256
DEPTH = 2
N_SAMPLE_BATCH = 2
SAMPLE_SEQ = 2048
PAST_LEN = 512
GRID_W = 64
HEAD_DIM = 64
ROPE_THETA = 10000.0
EPS = 1e-6
A_WINDOW = 128
NA_ROWS = 8
NA_COLS = 16
N_EXPERTS = 64
TOP_K = 6
N_GROUPS = 8
TOPK_GROUPS = 4
EXPERT_DIM = 256
ROUTED_SCALE = 2.5
Q_SCALE = HEAD_DIM ** -0.5

N_PROMPT = N_PROMPT_BATCH * PROMPT_SEQ
N_SAMPLE = N_SAMPLE_BATCH * SAMPLE_SEQ
N_TOK = N_PROMPT + N_SAMPLE

LANES = 128
TM = 256
N_PROMPT_TILES = N_PROMPT // TM
N_TILES = N_TOK // TM
TILES_PER_SAMPLE = SAMPLE_SEQ // TM
QB = 128
CHUNK = 16
SLOTS = -(-(TM * TOP_K + N_EXPERTS * (CHUNK - 1)) // 256) * 256
SLOT_CHUNKS = SLOTS // CHUNK
XW = D_MODEL + LANES
GM = 256
_MAX_SORTED = TM * TOP_K * N_TILES + N_TILES * N_EXPERTS * (CHUNK - 1) + N_EXPERTS * (GM - CHUNK)
G_TILES = -(-_MAX_SORTED // GM)
G_CHUNKS = GM // CHUNK
VMEM_LIMIT = 56 * 1024 * 1024

NEG = -1e30


def _cparams(sem):
    return pltpu.CompilerParams(dimension_semantics=sem, vmem_limit_bytes=VMEM_LIMIT)


def _mod_row(i):
    return jnp.where(i < N_PROMPT_TILES, 0, 1 + (i - N_PROMPT_TILES) // TILES_PER_SAMPLE)


def _x_specs(parts):
    if len(parts) == 1:
        return [pl.BlockSpec((TM, D_MODEL), lambda i, *_: (i, 0))]
    return [pl.BlockSpec((TM, D_MODEL), lambda i, *_: (jnp.minimum(i, N_PROMPT_TILES - 1), 0)),
            pl.BlockSpec((TM, D_MODEL), lambda i, *_: (jnp.maximum(i - N_PROMPT_TILES, 0), 0))]


def _load_x(i, x_refs):
    if len(x_refs) == 1:
        return x_refs[0][...]
    return jnp.where(i < N_PROMPT_TILES, x_refs[0][...], x_refs[1][...])


def _norm_mod(x, g, scale, shift):
    y = x * lax.rsqrt(jnp.mean(x * x, axis=-1, keepdims=True) + EPS)
    return (y * g) * (1.0 + scale) + shift


def _silu(x):
    return x * jax.nn.sigmoid(x)


def _dot(a, b):
    return jnp.dot(a, b, preferred_element_type=f32)


def _dot_nt(a, b):
    return lax.dot_general(a, b, (((1,), (1,)), ((), ())), preferred_element_type=f32)


ADA_COLS = 1536


def _adaln_kernel(cond_ref, w_ref, b_ref, o_ref):
    s = _silu(cond_ref[...]).astype(bf16)
    o_ref[...] = _dot(s, w_ref[...].astype(bf16)) + b_ref[...]


def _adaln(cond8, w_mod, b_mod):
    n6 = 6 * D_MODEL
    return pl.pallas_call(
        _adaln_kernel,
        grid=(DEPTH, n6 // ADA_COLS),
        in_specs=[
            pl.BlockSpec((8, D_MODEL), lambda l, j: (0, 0)),
            pl.BlockSpec((None, D_MODEL, ADA_COLS), lambda l, j: (l, 0, j)),
            pl.BlockSpec((None, 1, ADA_COLS), lambda l, j: (l, 0, j)),
        ],
        out_specs=pl.BlockSpec((None, 8, ADA_COLS), lambda l, j: (l, 0, j)),
        out_shape=jax.ShapeDtypeStruct((DEPTH, 8, n6), f32),
        compiler_params=_cparams(("parallel", "parallel")),
        name="adaln",
    )(cond8, w_mod, b_mod.reshape(DEPTH, 1, n6))


def _rope_block(blk, cos, sin_a, sin_b):
    return blk * cos + pltpu.roll(blk, LANES - 16, 1) * sin_a + pltpu.roll(blk, 16, 1) * sin_b


def _inproj_kernel(*refs, chunks, n_x):
    x_refs, kv_refs = refs[:n_x], refs[n_x + 7:]
    mod_ref, g_ref, w_ref, cos_ref, sa_ref, sb_ref, qkv_ref = refs[n_x:n_x + 7]
    i = pl.program_id(0)
    h = _norm_mod(_load_x(i, x_refs), g_ref[...], mod_ref[1:2, :], mod_ref[0:1, :]).astype(bf16)
    is_prompt = i < N_PROMPT_TILES
    for c0, c1, rope_blocks, kv_out in chunks:
        acc = _dot(h, w_ref[:, c0:c1])
        if rope_blocks:
            @pl.when(jnp.logical_not(is_prompt))
            def _():
                cos, sa, sb = cos_ref[...], sa_ref[...], sb_ref[...]
                for b in range((c1 - c0) // LANES):
                    blk = acc[:, b * LANES:(b + 1) * LANES]
                    if b in rope_blocks:
                        blk = _rope_block(blk, cos, sa, sb)
                    qkv_ref[:, c0 + b * LANES:c0 + (b + 1) * LANES] = blk.astype(bf16)

            @pl.when(is_prompt)
            def _():
                qkv_ref[:, c0:c1] = acc.astype(bf16)
        else:
            qkv_ref[:, c0:c1] = acc.astype(bf16)
        if kv_out:
            @pl.when(is_prompt)
            def _():
                for ridx, a0, a1, o0 in kv_out:
                    kv_refs[ridx][:, o0:o0 + (a1 - a0)] = acc[:, a0:a1]


def _inproj(x_parts, mod_l, g, w, rope_tabs, chunks, kv_widths):
    n = w.shape[1]
    cos, sa, sb = rope_tabs

    def rope_idx(i):
        return (jnp.where(i < N_PROMPT_TILES, 0, (i - N_PROMPT_TILES) % TILES_PER_SAMPLE), 0)

    def kv_idx(i):
        return (jnp.minimum(i, N_PROMPT_TILES - 1), 0)

    return pl.pallas_call(
        functools.partial(_inproj_kernel, chunks=chunks, n_x=len(x_parts)),
        grid=(N_TILES,),
        in_specs=_x_specs(x_parts) + [
            pl.BlockSpec((None, 6, D_MODEL), lambda i: (_mod_row(i), 0, 0)),
            pl.BlockSpec((1, D_MODEL), lambda i: (0, 0)),
            pl.BlockSpec((D_MODEL, n), lambda i: (0, 0)),
            pl.BlockSpec((TM, LANES), rope_idx),
            pl.BlockSpec((TM, LANES), rope_idx),
            pl.BlockSpec((TM, LANES), rope_idx),
        ],
        out_specs=[pl.BlockSpec((TM, n), lambda i: (i, 0))]
        + [pl.BlockSpec((TM, wd), kv_idx) for wd in kv_widths],
        out_shape=[jax.ShapeDtypeStruct((N_TOK, n), bf16)]
        + [jax.ShapeDtypeStruct((N_PROMPT, wd), f32) for wd in kv_widths],
        compiler_params=_cparams(("arbitrary",)),
        name="inproj",
    )(*x_parts, mod_l, g, w, cos, sa, sb)


def _rope_tables():
    nq = HEAD_DIM // 4
    t = jnp.arange(SAMPLE_SEQ)
    inv = jnp.power(ROPE_THETA, -jnp.arange(nq, dtype=f32) / nq)
    ang_r = (t // GRID_W).astype(f32)[:, None] * inv
    ang_c = (t % GRID_W).astype(f32)[:, None] * inv
    zero = jnp.zeros_like(ang_r)

    def head(fr, fc):
        return jnp.concatenate([fr[0], fr[1], fc[0], fc[1]], axis=-1)

    cos = head((jnp.cos(ang_r), jnp.cos(ang_r)), (jnp.cos(ang_c), jnp.cos(ang_c)))
    sin_a = head((-jnp.sin(ang_r), zero), (-jnp.sin(ang_c), zero))
    sin_b = head((zero, jnp.sin(ang_r)), (zero, jnp.sin(ang_c)))
    two = lambda a: jnp.concatenate([a, a], axis=-1)
    return two(cos), two(sin_a), two(sin_b)


def _lane_lo(shape):
    return lax.broadcasted_iota(i32, shape, len(shape) - 1) < HEAD_DIM


def _half(q, lo_mask, half):
    keep = lo_mask if half == 0 else jnp.logical_not(lo_mask)
    return jnp.where(keep, q, jnp.zeros_like(q)) * Q_SCALE


def _swap_halves(x):
    return pltpu.roll(x.astype(f32), HEAD_DIM, 1).astype(x.dtype)


def _softmax_pv(score_blocks, value_blocks, sink=None):
    m = functools.reduce(jnp.maximum, [jnp.max(s, axis=-1, keepdims=True) for s in score_blocks])
    if sink is not None:
        m = jnp.maximum(m, sink)
    es = [jnp.exp(s - m) for s in score_blocks]
    den = functools.reduce(lambda a, b: a + b, [jnp.sum(e, axis=-1, keepdims=True) for e in es])
    if sink is not None:
        den = den + jnp.exp(sink - m)
    o = functools.reduce(lambda a, b: a + b, [_dot(e.astype(bf16), v) for e, v in zip(es, value_blocks)])
    return o * (1.0 / den)


L0_QA, L0_QB, L0_KB, L0_VB, L0_KA, L0_VA, L0_N = 0, 512, 1024, 1536, 2048, 2176, 2304


def _ctx0_kernel(sink_ref, qkv_ref, o_ref):
    lo = _lane_lo((1, LANES))
    k_a = qkv_ref[:, L0_KA:L0_KA + LANES]
    v_a = qkv_ref[:, L0_VA:L0_VA + LANES]
    k_a_sw, v_a_sw = _swap_halves(k_a), _swap_halves(v_a)
    for j in range(4):
        q_a = qkv_ref[:, L0_QA + j * LANES:L0_QA + (j + 1) * LANES]
        q_b = qkv_ref[:, L0_QB + j * LANES:L0_QB + (j + 1) * LANES]
        k_b = qkv_ref[:, L0_KB + j * LANES:L0_KB + (j + 1) * LANES]
        v_b = qkv_ref[:, L0_VB + j * LANES:L0_VB + (j + 1) * LANES]
        kv_head = j // 2
        outs_a, outs_b = [], []
        for half in range(2):
            k_use, v_use = (k_a, v_a) if kv_head == half else (k_a_sw, v_a_sw)
            s = _dot_nt(_half(q_a, lo, half), k_use)
            outs_a.append(_softmax_pv([s], [v_use], sink=sink_ref[2 * j + half]))
            s = _dot_nt(_half(q_b, lo, half), k_b)
            outs_b.append(_softmax_pv([s], [v_b]))
        o_ref[:, j * LANES:(j + 1) * LANES] = jnp.where(lo, outs_a[0], outs_a[1]).astype(bf16)
        o_ref[:, 512 + j * LANES:512 + (j + 1) * LANES] = jnp.where(lo, outs_b[0], outs_b[1]).astype(bf16)


def _ctx0(sink, qkv):
    return pl.pallas_call(
        _ctx0_kernel,
        grid=(N_PROMPT_BATCH,),
        in_specs=[
            pl.BlockSpec(memory_space=pltpu.SMEM),
            pl.BlockSpec((PROMPT_SEQ, L0_N), lambda b: (b, 0)),
        ],
        out_specs=pl.BlockSpec((PROMPT_SEQ, D_MODEL), lambda b: (b, 0)),
        out_shape=jax.ShapeDtypeStruct((N_PROMPT, D_MODEL), bf16),
        compiler_params=_cparams(("parallel",)),
        name="ctx0",
    )(sink, qkv)


WIN_KEYS = 3 * QB
NA_KEY_ROWS = 10
NA_KEYS = NA_KEY_ROWS * GRID_W
N_QB = SAMPLE_SEQ // QB
N_NA_PATTERNS = 5
_PROMPT_QBLOCKS = N_PROMPT // QB


def _na_pattern(n):
    return jnp.where(n < 2, n, jnp.where(n > N_QB - 3, n - (N_QB - 5), 2))


def _lat0_kernel(sink_ref, q_ref, kvb_ref, kva_ref, cak_ref, cav_ref, cbk_ref, cbv_ref, nab_ref, o_ref):
    n = pl.program_id(1)
    lo = _lane_lo((1, LANES))
    kstart = pl.multiple_of(jnp.clip((n - 1) * QB, 0, SAMPLE_SEQ - WIN_KEYS), QB)
    k_a = kva_ref[pl.ds(kstart, WIN_KEYS), 0:LANES]
    v_a = kva_ref[pl.ds(kstart, WIN_KEYS), LANES:2 * LANES]
    c_k = cak_ref[...].astype(bf16)
    c_v = cav_ref[...].astype(bf16)
    k_a_sw, v_a_sw, c_k_sw, c_v_sw = (_swap_halves(t) for t in (k_a, v_a, c_k, c_v))
    qpos = n * QB + lax.broadcasted_iota(i32, (QB, WIN_KEYS), 0)
    kpos = kstart + lax.broadcasted_iota(i32, (QB, WIN_KEYS), 1)
    in_window = jnp.abs(qpos - kpos) <= A_WINDOW
    krow = jnp.clip(2 * n - NA_ROWS // 2, 0, SAMPLE_SEQ // GRID_W - NA_KEY_ROWS)
    ktok = pl.multiple_of(krow * GRID_W, QB)
    for j in range(4):
        q_a = q_ref[:, L0_QA + j * LANES:L0_QA + (j + 1) * LANES]
        q_b = q_ref[:, L0_QB + j * LANES:L0_QB + (j + 1) * LANES]
        k_b = kvb_ref[pl.ds(ktok, NA_KEYS), j * LANES:(j + 1) * LANES]
        v_b = kvb_ref[pl.ds(ktok, NA_KEYS), 512 + j * LANES:512 + (j + 1) * LANES]
        cb_k = cbk_ref[:, j * LANES:(j + 1) * LANES].astype(bf16)
        cb_v = cbv_ref[:, j * LANES:(j + 1) * LANES].astype(bf16)
        kv_head = j // 2
        outs_a, outs_b = [], []
        for half in range(2):
            same = kv_head == half
            qh = _half(q_a, lo, half)
            s_loc = jnp.where(in_window, _dot_nt(qh, k_a if same else k_a_sw), NEG)
            s_ctx = _dot_nt(qh, c_k if same else c_k_sw)
            outs_a.append(_softmax_pv([s_ctx, s_loc], [c_v if same else c_v_sw, v_a if same else v_a_sw],
                                      sink=sink_ref[2 * j + half]))
            qh = _half(q_b, lo, half)
            s_loc = _dot_nt(qh, k_b) + nab_ref[2 * j + half]
            s_ctx = _dot_nt(qh, cb_k)
            outs_b.append(_softmax_pv([s_ctx, s_loc], [cb_v, v_b]))
        o_ref[:, j * LANES:(j + 1) * LANES] = jnp.where(lo, outs_a[0], outs_a[1]).astype(bf16)
        o_ref[:, 512 + j * LANES:512 + (j + 1) * LANES] = jnp.where(lo, outs_b[0], outs_b[1]).astype(bf16)


def _lat0(sink, qkv, cak, cav, cbk, cbv, nab):
    sb = N_PROMPT // SAMPLE_SEQ
    return pl.pallas_call(
        _lat0_kernel,
        grid=(N_SAMPLE_BATCH, N_QB),
        in_specs=[
            pl.BlockSpec(memory_space=pltpu.SMEM),
            pl.BlockSpec((QB, 1024), lambda b, n: (_PROMPT_QBLOCKS + b * N_QB + n, 0)),
            pl.BlockSpec((SAMPLE_SEQ, 1024), lambda b, n: (sb + b, 1)),
            pl.BlockSpec((SAMPLE_SEQ, 256), lambda b, n: (sb + b, L0_KA // 256)),
            pl.BlockSpec((None, PAST_LEN, LANES), lambda b, n: (b, 0, 0)),
            pl.BlockSpec((None, PAST_LEN, LANES), lambda b, n: (b, 0, 0)),
            pl.BlockSpec((None, PAST_LEN, 512), lambda b, n: (b, 0, 0)),
            pl.BlockSpec((None, PAST_LEN, 512), lambda b, n: (b, 0, 0)),
            pl.BlockSpec((None, 8, QB, NA_KEYS), lambda b, n: (_na_pattern(n), 0, 0, 0)),
        ],
        out_specs=pl.BlockSpec((QB, D_MODEL), lambda b, n: (b * N_QB + n, 0)),
        out_shape=jax.ShapeDtypeStruct((N_SAMPLE, D_MODEL), bf16),
        compiler_params=_cparams(("parallel", "arbitrary")),
        name="lat0",
    )(sink, qkv, qkv, qkv, cak, cav, cbk, cbv, nab)


def _na_bias_table(rel_bias):
    rows = SAMPLE_SEQ // GRID_W
    n_dr, n_dc = 2 * NA_ROWS - 1, 2 * NA_COLS - 1
    c = np.arange(GRID_W)[:, None]
    kc = np.arange(GRID_W)[None, :]
    cs = np.clip(c - NA_COLS // 2, 0, GRID_W - NA_COLS)
    col_ok = (kc >= cs) & (kc < cs + NA_COLS)
    col_hot = ((kc - c + NA_COLS - 1)[None] == np.arange(n_dc)[:, None, None]) & col_ok[None]
    row_hot = np.zeros((N_NA_PATTERNS, 2, NA_KEY_ROWS, n_dr), np.float32)
    for p, n in enumerate((0, 1, 2, N_QB - 2, N_QB - 1)):
        k0 = int(np.clip(2 * n - NA_ROWS // 2, 0, rows - NA_KEY_ROWS))
        for rq in range(2):
            r = 2 * n + rq
            rs = int(np.clip(r - NA_ROWS // 2, 0, rows - NA_ROWS))
            for kl in range(NA_KEY_ROWS):
                if rs <= k0 + kl < rs + NA_ROWS:
                    row_hot[p, rq, kl, k0 + kl - r + NA_ROWS - 1] = 1.0
    ok = (row_hot.sum(-1) > 0)[:, None, :, None, :, None] & col_ok[None, None, None, :, None, :]
    hp = lax.Precision.HIGHEST
    toeplitz = jnp.einsum("hdx,xck->hdck", rel_bias.astype(f32), col_hot.astype(np.float32), precision=hp)
    tab = jnp.einsum("prkd,hdcx->phrckx", row_hot, toeplitz, precision=hp)
    tab = tab + np.where(ok, 0.0, NEG).astype(np.float32)
    return tab.reshape(N_NA_PATTERNS, 8, QB, NA_KEYS)


def _diff_lambda(lam_ref, lam_init):
    lv = lam_ref[...]
    s1 = jnp.sum(lv[0:1, :] * lv[1:2, :], axis=-1, keepdims=True)
    s2 = jnp.sum(lv[2:3, :] * lv[3:4, :], axis=-1, keepdims=True)
    return jnp.exp(s1) - jnp.exp(s2) + lam_init


def _softmax_parts(score_blocks):
    m = functools.reduce(jnp.maximum, [jnp.max(s, axis=-1, keepdims=True) for s in score_blocks])
    es = [jnp.exp(s - m) for s in score_blocks]
    den = functools.reduce(lambda a, b: a + b, [jnp.sum(e, axis=-1, keepdims=True) for e in es])
    return es, 1.0 / den


def _diff_head(q, key_blocks, value_blocks, lam, subln, lo, lam_init):
    q1, q2 = _half(q, lo, 0), _half(q, lo, 1)
    e1, r1 = _softmax_parts([_dot_nt(q1, k) for k in key_blocks])
    e2, r2 = _softmax_parts([_dot_nt(q2, k) for k in key_blocks])
    r2 = r2 * lam
    o = functools.reduce(lambda a, b: a + b,
                         [_dot((a * r1 - b * r2).astype(bf16), v) for a, b, v in zip(e1, e2, value_blocks)])
    o = o * lax.rsqrt(jnp.mean(o * o, axis=-1, keepdims=True) + EPS)
    return (o * subln) * (1.0 - lam_init)


def _ctx1_kernel(lam_ref, subln_ref, qkv_ref, o_ref, *, lam_init):
    lo = _lane_lo((1, LANES))
    lam = _diff_lambda(lam_ref, lam_init)
    subln = subln_ref[...]
    for h in range(8):
        q = qkv_ref[:, h * LANES:(h + 1) * LANES]
        k = qkv_ref[:, 1024 + h * LANES:1024 + (h + 1) * LANES]
        v = qkv_ref[:, 2048 + h * LANES:2048 + (h + 1) * LANES]
        o_ref[:, h * LANES:(h + 1) * LANES] = _diff_head(q, [k], [v], lam, subln, lo, lam_init).astype(bf16)


def _ctx1(lamv, subln, qkv, lam_init):
    return pl.pallas_call(
        functools.partial(_ctx1_kernel, lam_init=lam_init),
        grid=(N_PROMPT_BATCH,),
        in_specs=[
            pl.BlockSpec((8, HEAD_DIM), lambda b: (0, 0)),
            pl.BlockSpec((1, LANES), lambda b: (0, 0)),
            pl.BlockSpec((PROMPT_SEQ, 3 * D_MODEL), lambda b: (b, 0)),
        ],
        out_specs=pl.BlockSpec((PROMPT_SEQ, D_MODEL), lambda b: (b, 0)),
        out_shape=jax.ShapeDtypeStruct((N_PROMPT, D_MODEL), bf16),
        compiler_params=_cparams(("parallel",)),
        name="ctx1",
    )(lamv, subln, qkv)


def _lat1_kernel(lam_ref, subln_ref, q_ref, k_ref, v_ref, ck_ref, cv_ref, o_ref, *, lam_init):
    lo = _lane_lo((1, LANES))
    lam = _diff_lambda(lam_ref, lam_init)
    subln = subln_ref[...]
    for h in range(8):
        sl = slice(h * LANES, (h + 1) * LANES)
        o_ref[:, sl] = _diff_head(q_ref[:, sl], [ck_ref[:, sl].astype(bf16), k_ref[:, sl]],
                                  [cv_ref[:, sl].astype(bf16), v_ref[:, sl]],
                                  lam, subln, lo, lam_init).astype(bf16)


def _lat1(lamv, subln, qkv, ck, cv, lam_init):
    sb = N_PROMPT // SAMPLE_SEQ
    nq = SAMPLE_SEQ // TM
    return pl.pallas_call(
        functools.partial(_lat1_kernel, lam_init=lam_init),
        grid=(N_SAMPLE_BATCH, nq),
        in_specs=[
            pl.BlockSpec((8, HEAD_DIM), lambda b, n: (0, 0)),
            pl.BlockSpec((1, LANES), lambda b, n: (0, 0)),
            pl.BlockSpec((TM, D_MODEL), lambda b, n: (N_PROMPT_TILES + b * nq + n, 0)),
            pl.BlockSpec((SAMPLE_SEQ, D_MODEL), lambda b, n: (sb + b, 1)),
            pl.BlockSpec((SAMPLE_SEQ, D_MODEL), lambda b, n: (sb + b, 2)),
            pl.BlockSpec((None, PAST_LEN, D_MODEL), lambda b, n: (b, 0, 0)),
            pl.BlockSpec((None, PAST_LEN, D_MODEL), lambda b, n: (b, 0, 0)),
        ],
        out_specs=pl.BlockSpec((TM, D_MODEL), lambda b, n: (b * nq + n, 0)),
        out_shape=jax.ShapeDtypeStruct((N_SAMPLE, D_MODEL), bf16),
        compiler_params=_cparams(("parallel", "arbitrary")),
        name="lat1",
    )(lamv, subln, qkv, qkv, qkv, ck, cv)


def _outproj_kernel(*refs, n_x):
    x_refs = refs[:n_x]
    op_ref, os_ref, mod_ref, w_ref, o_ref = refs[n_x:]
    i = pl.program_id(0)
    attn = jnp.where(i < N_PROMPT_TILES, op_ref[...], os_ref[...])
    o_ref[...] = _load_x(i, x_refs) + mod_ref[2:3, :] * _dot(attn, w_ref[...])


def _outproj(x_parts, o_prompt, o_sample, mod_l, w):
    return pl.pallas_call(
        functools.partial(_outproj_kernel, n_x=len(x_parts)),
        grid=(N_TILES,),
        in_specs=_x_specs(x_parts) + [
            pl.BlockSpec((TM, D_MODEL), lambda i: (jnp.minimum(i, N_PROMPT_TILES - 1), 0)),
            pl.BlockSpec((TM, D_MODEL), lambda i: (jnp.maximum(i - N_PROMPT_TILES, 0), 0)),
            pl.BlockSpec((None, 6, D_MODEL), lambda i: (_mod_row(i), 0, 0)),
            pl.BlockSpec((D_MODEL, D_MODEL), lambda i: (0, 0)),
        ],
        out_specs=pl.BlockSpec((TM, D_MODEL), lambda i: (i, 0)),
        out_shape=jax.ShapeDtypeStruct((N_TOK, D_MODEL), f32),
        compiler_params=_cparams(("parallel",)),
        name="outproj",
    )(*x_parts, o_prompt, o_sample, mod_l, w)


def _route_kernel(x_ref, mod_ref, g_ref, rwt_ref, rb_ref, xloc_ref, slots_ref, len_ref):
    ng, ge = N_GROUPS, N_EXPERTS // N_GROUPS
    h = _norm_mod(x_ref[...], g_ref[...], mod_ref[4:5, :], mod_ref[3:4, :])
    logits = lax.dot_general(rwt_ref[...], h, (((1,), (1,)), ((), ())),
                             precision=lax.Precision.HIGHEST, preferred_element_type=f32)
    scores = jax.nn.sigmoid(logits)
    biased = scores + rb_ref[...]
    s3 = scores.reshape(ng, ge, TM)
    b3 = biased.reshape(ng, ge, TM)
    in_group = lax.broadcasted_iota(i32, (ng, ge, TM), 1).astype(f32)
    group_id = lax.broadcasted_iota(i32, (ng, 1, TM), 0).astype(f32)
    expert_id = lax.broadcasted_iota(i32, (ng, ge, TM), 0).astype(f32) * ge + in_group

    def max01(a):
        return jnp.max(jnp.max(a, axis=0, keepdims=True), axis=1, keepdims=True)

    def min01(a):
        return jnp.min(jnp.min(a, axis=0, keepdims=True), axis=1, keepdims=True)

    def sum01(a):
        return jnp.sum(jnp.sum(a, axis=0, keepdims=True), axis=1, keepdims=True)

    m1 = jnp.max(b3, axis=1, keepdims=True)
    first = jnp.min(jnp.where(b3 == m1, in_group, ge), axis=1, keepdims=True)
    m2 = jnp.max(jnp.where(in_group == first, -jnp.inf, b3), axis=1, keepdims=True)
    gscore = m1 + m2
    gsel = jnp.zeros((ng, 1, TM), f32)
    for _ in range(TOPK_GROUPS):
        gm = jnp.max(gscore, axis=0, keepdims=True)
        gi = jnp.min(jnp.where(gscore == gm, group_id, ng), axis=0, keepdims=True)
        hit = group_id == gi
        gsel = jnp.where(hit, 1.0, gsel)
        gscore = jnp.where(hit, -jnp.inf, gscore)
    cand = jnp.where(jnp.broadcast_to(gsel, (ng, ge, TM)) > 0.0, b3, -jnp.inf)
    top_e, top_w = [], []
    for _ in range(TOP_K):
        em = max01(cand)
        ei = min01(jnp.where(cand == em, expert_id, N_EXPERTS))
        hit = expert_id == ei
        top_e.append(ei)
        top_w.append(sum01(jnp.where(hit, s3, 0.0)))
        cand = jnp.where(hit, -jnp.inf, cand)
    wsum = functools.reduce(lambda a, b: a + b, top_w)
    gates3 = jnp.zeros((ng, ge, TM), f32)
    sel3 = jnp.zeros((ng, ge, TM), f32)
    for ei, w in zip(top_e, top_w):
        hit = expert_id == ei
        gates3 = jnp.where(hit, w / wsum * ROUTED_SCALE, gates3)
        sel3 = jnp.where(hit, 1.0, sel3)
    gates = gates3.reshape(N_EXPERTS, TM)
    sel = sel3.reshape(N_EXPERTS, TM)

    cnt = jnp.sum(sel, axis=1, keepdims=True)
    run_len = jnp.ceil(cnt * (1.0 / CHUNK)) * CHUNK
    r_i = lax.broadcasted_iota(i32, (N_EXPERTS, N_EXPERTS), 0)
    c_i = lax.broadcasted_iota(i32, (N_EXPERTS, N_EXPERTS), 1)
    lower = jnp.where(c_i < r_i, 1.0, 0.0).astype(bf16)
    run_off = _dot(lower, jnp.broadcast_to(run_len, (N_EXPERTS, LANES)).astype(bf16))[:, 0:1]
    t_r = lax.broadcasted_iota(i32, (TM, TM), 0)
    t_c = lax.broadcasted_iota(i32, (TM, TM), 1)
    before = jnp.where(t_r < t_c, 1.0, 0.0).astype(bf16)
    rank = _dot(sel.astype(bf16), before)
    slot3 = (run_off + rank).reshape(ng, ge, TM)
    slots = [sum01(jnp.where(expert_id == ei, slot3, 0.0)).reshape(1, TM).astype(i32) for ei in top_e]
    for k in range(TOP_K):
        slots_ref[k:k + 1, :] = slots[k]
    slots_ref[TOP_K:8, :] = jnp.full((8 - TOP_K, TM), -1, i32)
    len_ref[...] = jnp.broadcast_to(run_len, (N_EXPERTS, LANES)).astype(i32)

    hb = h.astype(bf16)
    g_hi = gates.astype(bf16)
    g_lo = (gates - g_hi.astype(f32)).astype(bf16)
    g_parts = jnp.concatenate([g_hi, g_lo], axis=0)
    rows = 256

    def body(c, carry):
        base = pl.multiple_of(c * rows, rows)
        row_id = base + lax.broadcasted_iota(i32, (rows, TM), 0)
        p = jnp.zeros((rows, TM), f32)
        for k in range(TOP_K):
            p = jnp.where(row_id == slots[k], 1.0, p)
        p = p.astype(bf16)
        xloc_ref[pl.ds(base, rows), 0:D_MODEL] = _dot(p, hb).astype(bf16)
        xloc_ref[pl.ds(base, rows), D_MODEL:XW] = _dot_nt(p, g_parts).astype(bf16)
        return carry

    def zero_body(c, carry):
        base = pl.multiple_of(c * rows, rows)
        xloc_ref[pl.ds(base, rows), :] = jnp.zeros((rows, XW), bf16)
        return carry

    n_used = (jnp.sum(run_len).astype(i32) + (rows - 1)) // rows
    lax.fori_loop(0, n_used, body, 0)
    lax.fori_loop(n_used, SLOTS // rows, zero_body, 0)


def _route(x, mod_l, g, rwt, rb):
    return pl.pallas_call(
        _route_kernel,
        grid=(N_TILES,),
        in_specs=[
            pl.BlockSpec((TM, D_MODEL), lambda i: (i, 0)),
            pl.BlockSpec((None, 6, D_MODEL), lambda i: (_mod_row(i), 0, 0)),
            pl.BlockSpec((1, D_MODEL), lambda i: (0, 0)),
            pl.BlockSpec((N_EXPERTS, D_MODEL), lambda i: (0, 0)),
            pl.BlockSpec((N_EXPERTS, 1), lambda i: (0, 0)),
        ],
        out_specs=[
            pl.BlockSpec((SLOTS, XW), lambda i: (i, 0)),
            pl.BlockSpec((None, 8, TM), lambda i: (i, 0, 0)),
            pl.BlockSpec((None, N_EXPERTS, LANES), lambda i: (i, 0, 0)),
        ],
        out_shape=[
            jax.ShapeDtypeStruct((N_TILES * SLOTS, XW), bf16),
            jax.ShapeDtypeStruct((N_TILES, 8, TM), i32),
            jax.ShapeDtypeStruct((N_TILES, N_EXPERTS, LANES), i32),
        ],
        compiler_params=_cparams(("parallel",)),
        name="route",
    )(x, mod_l, g, rwt, rb)


def _moe_plan(run_len):
    nt, ne = run_len.shape

    def excl_cumsum(a):
        n = a.shape[-1]
        earlier = np.arange(n)[None, :] < np.arange(n)[:, None]
        return jnp.sum(jnp.where(earlier, a[..., None, :], 0), axis=-1)

    def first_diff(a):
        return a - jnp.concatenate([jnp.zeros_like(a[..., :1]), a[..., :-1]], axis=-1)

    off_loc = excl_cumsum(run_len)
    before = excl_cumsum(run_len.T).T
    n_e = jnp.sum(run_len, axis=0)
    n_pad = -(-n_e // GM) * GM
    g_start = excl_cumsum(n_pad)
    total = jnp.sum(n_pad)
    run_dst = g_start[None, :] + before
    run_src = jnp.arange(nt, dtype=i32)[:, None] * SLOTS + off_loc
    dst_f = run_dst.T.reshape(-1)
    shift_f = first_diff((run_src - run_dst).T.reshape(-1))
    rows = jnp.arange(G_TILES * G_CHUNKS, dtype=i32) * CHUNK
    shift = jnp.sum(jnp.where(dst_f[None, :] <= rows[:, None], shift_f[None, :], 0), axis=1)
    in_run = jnp.any((g_start[None, :] <= rows[:, None]) & (rows[:, None] < (g_start + n_e)[None, :]), axis=1)
    chunk_src = jnp.where(in_run, rows + shift, 0).astype(i32)
    loc_rows = jnp.arange(SLOT_CHUNKS, dtype=i32) * CHUNK
    shift_l = first_diff(run_dst - off_loc)
    shift = jnp.sum(jnp.where(off_loc[:, None, :] <= loc_rows[None, :, None], shift_l[:, None, :], 0), axis=2)
    used = jnp.sum(run_len, axis=1)
    chunk_map = jnp.where(loc_rows[None, :] < used[:, None], (loc_rows[None, :] + shift) // CHUNK, 0).astype(i32)
    return (g_start // GM).astype(i32), (n_pad // GM).astype(i32), chunk_src, chunk_map.reshape(-1)


def _gmm_in_copy(xloc_hbm, xbuf, sem, src_row, slot, c):
    return pltpu.make_async_copy(xloc_hbm.at[pl.ds(src_row, CHUNK)],
                                 xbuf.at[slot, pl.ds(c * CHUNK, CHUNK)], sem.at[slot])


def _gmm_out_copy(ybuf, y_hbm, sem, tile, slot):
    return pltpu.make_async_copy(ybuf.at[slot], y_hbm.at[pl.ds(pl.multiple_of(tile * GM, GM), GM)], sem.at[slot])


def _gmm_kernel(t0_ref, nt_ref, cs_ref, xloc_hbm, wg_ref, wu_ref, wd_ref, y_hbm,
                xbuf, ybuf, zbuf, wg_b, wu_b, wd_b, in_sem, out_sem, zsem):
    e = pl.program_id(0)
    last = pl.num_programs(0) - 1
    n_tiles = nt_ref[e]
    first_tile = t0_ref[e]
    total_tiles = t0_ref[last] + nt_ref[last]

    def start_in(tile):
        for c in range(G_CHUNKS):
            src = pl.multiple_of(cs_ref[tile * G_CHUNKS + c], CHUNK)
            _gmm_in_copy(xloc_hbm, xbuf, in_sem, src, tile % 2, c).start()

    @pl.when(e == 0)
    def _():
        start_in(0)
        zbuf[...] = jnp.zeros(zbuf.shape, zbuf.dtype)

    def tail_copies(fn):
        for j in range(_GMM_TAIL_PER_STEP):
            tile = total_tiles + e + j * N_EXPERTS

            @pl.when(tile < G_TILES)
            def _():
                fn(pltpu.make_async_copy(zbuf, y_hbm.at[pl.ds(pl.multiple_of(tile * GM, GM), GM)], zsem.at[0]))

    tail_copies(lambda cp: cp.start())

    @pl.when(n_tiles > 0)
    def _():
        wg_b[...] = wg_ref[...].astype(bf16)
        wu_b[...] = wu_ref[...].astype(bf16)
        wd_b[...] = wd_ref[...].astype(bf16)

    lane = lax.broadcasted_iota(i32, (1, LANES), 1)
    gate_lanes = jnp.logical_or(lane == e, lane == e + N_EXPERTS)

    def body(t, carry):
        tile = first_tile + t
        slot = tile % 2

        @pl.when(tile + 1 < total_tiles)
        def _():
            start_in(tile + 1)

        for c in range(G_CHUNKS):
            _gmm_in_copy(xloc_hbm, xbuf, in_sem, 0, slot, c).wait()

        @pl.when(tile >= 2)
        def _():
            _gmm_out_copy(ybuf, y_hbm, out_sem, tile - 2, slot).wait()

        x = xbuf[slot]
        gate = jnp.sum(jnp.where(gate_lanes, x[:, D_MODEL:XW].astype(f32), 0.0), axis=1, keepdims=True)
        xa = x[:, 0:D_MODEL]
        act = _silu(_dot(xa, wg_b[...])) * _dot(xa, wu_b[...]) * gate
        ybuf[slot] = _dot(act.astype(bf16), wd_b[...]).astype(bf16)
        _gmm_out_copy(ybuf, y_hbm, out_sem, tile, slot).start()
        return carry

    lax.fori_loop(0, n_tiles, body, 0)
    tail_copies(lambda cp: cp.wait())

    @pl.when(e == last)
    def _():
        _gmm_out_copy(ybuf, y_hbm, out_sem, total_tiles - 2, total_tiles % 2).wait()
        _gmm_out_copy(ybuf, y_hbm, out_sem, total_tiles - 1, (total_tiles - 1) % 2).wait()


_GMM_TAIL_PER_STEP = -(-(G_TILES - TM * TOP_K * N_TILES // GM) // N_EXPERTS)


def _gmm(tile_start, n_tiles, chunk_src, xloc, wg, wu, wd, layer):
    grid_spec = pltpu.PrefetchScalarGridSpec(
        num_scalar_prefetch=3,
        grid=(N_EXPERTS,),
        in_specs=[
            pl.BlockSpec(memory_space=pl.ANY),
            pl.BlockSpec((None, None, D_MODEL, EXPERT_DIM), lambda e, t0, nt, cs: (layer, e, 0, 0)),
            pl.BlockSpec((None, None, D_MODEL, EXPERT_DIM), lambda e, t0, nt, cs: (layer, e, 0, 0)),
            pl.BlockSpec((None, None, EXPERT_DIM, D_MODEL), lambda e, t0, nt, cs: (layer, e, 0, 0)),
        ],
        out_specs=pl.BlockSpec(memory_space=pl.ANY),
        scratch_shapes=[pltpu.VMEM((2, GM, XW), bf16), pltpu.VMEM((2, GM, D_MODEL), bf16),
                        pltpu.VMEM((GM, D_MODEL), bf16),
                        pltpu.VMEM((D_MODEL, EXPERT_DIM), bf16), pltpu.VMEM((D_MODEL, EXPERT_DIM), bf16),
                        pltpu.VMEM((EXPERT_DIM, D_MODEL), bf16),
                        pltpu.SemaphoreType.DMA((2,)), pltpu.SemaphoreType.DMA((2,)), pltpu.SemaphoreType.DMA((1,))],
    )
    return pl.pallas_call(
        _gmm_kernel,
        grid_spec=grid_spec,
        out_shape=jax.ShapeDtypeStruct((G_TILES * GM, D_MODEL), bf16),
        compiler_params=_cparams(("arbitrary",)),
        name="gmm",
    )(tile_start, n_tiles, chunk_src, xloc, wg, wu, wd)


def _combine_copy(y_hbm, ybuf, sem, sorted_chunk, slot, c):
    return pltpu.make_async_copy(y_hbm.at[pl.ds(pl.multiple_of(sorted_chunk * CHUNK, CHUNK), CHUNK)],
                                 ybuf.at[slot, pl.ds(pl.multiple_of(c * CHUNK, CHUNK), CHUNK)], sem.at[slot])


def _combine_kernel(cm_ref, y_hbm, slots_ref, x_ref, mod_ref, g_ref, sg_ref, su_ref, sd_ref, o_ref, ybuf, sem):
    i = pl.program_id(0)
    n = pl.num_programs(0)
    slot = i % 2

    def start(tile, s):
        for c in range(SLOT_CHUNKS):
            _combine_copy(y_hbm, ybuf, sem, cm_ref[tile * SLOT_CHUNKS + c], s, c).start()

    def wait(s):
        for c in range(SLOT_CHUNKS):
            _combine_copy(y_hbm, ybuf, sem, 0, s, c).wait()

    @pl.when(i == 0)
    def _():
        start(0, 0)

    wait(slot)
    start((i + 1) % n, 1 - slot)

    x = x_ref[...]
    hb = _norm_mod(x, g_ref[...], mod_ref[4:5, :], mod_ref[3:4, :]).astype(bf16)
    shared = _dot((_silu(_dot(hb, sg_ref[...])) * _dot(hb, su_ref[...])).astype(bf16), sd_ref[...])
    row_id = lax.broadcasted_iota(i32, (SLOTS, TM), 0)
    p = jnp.zeros((SLOTS, TM), f32)
    for k in range(TOP_K):
        p = jnp.where(row_id == slots_ref[k:k + 1, :], 1.0, p)
    routed = lax.dot_general(p.astype(bf16), ybuf[slot], (((0,), (0,)), ((), ())), preferred_element_type=f32)
    o_ref[...] = x + mod_ref[5:6, :] * (routed + shared)

    @pl.when(i == n - 1)
    def _():
        wait(1 - slot)


def _combine(chunk_map, y, slots, x, mod_l, g, sg, su, sd):
    shd = sg.shape[1]
    grid_spec = pltpu.PrefetchScalarGridSpec(
        num_scalar_prefetch=1,
        grid=(N_TILES,),
        in_specs=[
            pl.BlockSpec(memory_space=pl.ANY),
            pl.BlockSpec((None, 8, TM), lambda i, cm: (i, 0, 0)),
            pl.BlockSpec((TM, D_MODEL), lambda i, cm: (i, 0)),
            pl.BlockSpec((None, 6, D_MODEL), lambda i, cm: (_mod_row(i), 0, 0)),
            pl.BlockSpec((1, D_MODEL), lambda i, cm: (0, 0)),
            pl.BlockSpec((D_MODEL, shd), lambda i, cm: (0, 0)),
            pl.BlockSpec((D_MODEL, shd), lambda i, cm: (0, 0)),
            pl.BlockSpec((shd, D_MODEL), lambda i, cm: (0, 0)),
        ],
        out_specs=pl.BlockSpec((TM, D_MODEL), lambda i, cm: (i, 0)),
        scratch_shapes=[pltpu.VMEM((2, SLOTS, D_MODEL), bf16), pltpu.SemaphoreType.DMA((2,))],
    )
    return pl.pallas_call(
        _combine_kernel,
        grid_spec=grid_spec,
        out_shape=jax.ShapeDtypeStruct((N_TOK, D_MODEL), f32),
        compiler_params=_cparams(("arbitrary",)),
        name="combine",
    )(chunk_map, y, slots, x, mod_l, g, sg, su, sd)


def _moe(x, mod_l, g, rwt, rb, wg, wu, wd, layer, sg, su, sd):
    xloc, slots, run_len = _route(x, mod_l, g, rwt, rb)
    tile_start, n_tiles, chunk_src, chunk_map = _moe_plan(run_len[:, :, 0])
    y = _gmm(tile_start, n_tiles, chunk_src, xloc, wg, wu, wd, layer)
    return _combine(chunk_map, y, slots, x, mod_l, g, sg, su, sd)


def _final_kernel(x_ref, g_ref, yp_ref, ys_ref):
    i = pl.program_id(0)
    x = x_ref[...]
    y = (x * lax.rsqrt(jnp.mean(x * x, axis=-1, keepdims=True) + EPS)) * g_ref[...]

    @pl.when(i < N_PROMPT_TILES)
    def _():
        yp_ref[...] = y

    @pl.when(i >= N_PROMPT_TILES)
    def _():
        ys_ref[...] = y


def _final(x, g):
    return pl.pallas_call(
        _final_kernel,
        grid=(N_TILES,),
        in_specs=[
            pl.BlockSpec((TM, D_MODEL), lambda i: (i, 0)),
            pl.BlockSpec((1, D_MODEL), lambda i: (0, 0)),
        ],
        out_specs=[
            pl.BlockSpec((TM, D_MODEL), lambda i: (jnp.minimum(i, N_PROMPT_TILES - 1), 0)),
            pl.BlockSpec((TM, D_MODEL), lambda i: (jnp.maximum(i - N_PROMPT_TILES, 0), 0)),
        ],
        out_shape=[jax.ShapeDtypeStruct((N_PROMPT, D_MODEL), f32), jax.ShapeDtypeStruct((N_SAMPLE, D_MODEL), f32)],
        compiler_params=_cparams(("arbitrary",)),
        name="final_norm",
    )(x, g)


def _permute_w_in_ab(w):
    return jnp.concatenate([w[:, 0:512], w[:, 768:1280], w[:, 1280:1792], w[:, 1792:2304],
                            w[:, 512:640], w[:, 640:768]], axis=1)


_L0_CHUNKS = (
    (0, 512, (0, 1, 2, 3), ()),
    (512, 1024, (), ()),
    (1024, 1536, (), ((2, 0, 512, 0),)),
    (1536, 2048, (), ((3, 0, 512, 0),)),
    (2048, 2304, (0,), ((0, 0, 128, 0), (1, 128, 256, 0))),
)
_L1_CHUNKS = (
    (0, 512, (0, 1, 2, 3), ()),
    (512, 1024, (0, 1, 2, 3), ()),
    (1024, 1536, (0, 1, 2, 3), ((0, 0, 512, 0),)),
    (1536, 2048, (0, 1, 2, 3), ((0, 0, 512, 512),)),
    (2048, 2560, (), ((1, 0, 512, 0),)),
    (2560, 3072, (), ((1, 0, 512, 512),)),
)


def kernel(x_prompt, x_sample, cache_a_k, cache_a_v, cache_b_k, cache_b_v, cache_c_k, cache_c_v, c, c_ctx, w_mod, b_mod, norm_mix, norm_ffn, w_in_ab, w_out_ab, sink_a, rel_bias_b, w_in_c, w_out_c, lam_q1, lam_k1, lam_q2, lam_k2, subln_c, router_w, router_bias, exp_w_gate, exp_w_up, exp_w_down, sh_w_gate, sh_w_up, sh_w_down, final_norm):
    x = (x_prompt.reshape(N_PROMPT, D_MODEL), x_sample.reshape(N_SAMPLE, D_MODEL))
    cond8 = jnp.concatenate([c_ctx[None, :], c, jnp.zeros((8 - 1 - N_SAMPLE_BATCH, D_MODEL), f32)], axis=0)
    mod = _adaln(cond8, w_mod, b_mod).reshape(DEPTH, 8, 6, D_MODEL)
    rope_tabs = _rope_tables()
    new_kv = {}
    for layer in range(DEPTH):
        li = layer // 2
        mod_l = mod[layer]
        g_mix = norm_mix[layer][None, :]
        g_ffn = norm_ffn[layer][None, :]
        if layer % 2 == 0:
            w_in = _permute_w_in_ab(w_in_ab[li]).astype(bf16)
            qkv, ak, av, bk, bv = _inproj(x, mod_l, g_mix, w_in, rope_tabs, _L0_CHUNKS, (128, 128, 512, 512))
            new_kv["a_k"], new_kv["a_v"], new_kv["b_k"], new_kv["b_v"] = ak, av, bk, bv
            o_p = _ctx0(sink_a[li], qkv)
            o_s = _lat0(sink_a[li], qkv,
                        cache_a_k[:, li].reshape(N_SAMPLE_BATCH, PAST_LEN, LANES),
                        cache_a_v[:, li].reshape(N_SAMPLE_BATCH, PAST_LEN, LANES),
                        cache_b_k[:, li].reshape(N_SAMPLE_BATCH, PAST_LEN, 512),
                        cache_b_v[:, li].reshape(N_SAMPLE_BATCH, PAST_LEN, 512),
                        _na_bias_table(rel_bias_b[li]))
            w_out = w_out_ab[li].astype(bf16)
        else:
            lam_init = 0.8 - 0.6 * math.exp(-0.3 * layer)
            qkv, ck, cv = _inproj(x, mod_l, g_mix, w_in_c[li].astype(bf16), rope_tabs, _L1_CHUNKS, (1024, 1024))
            new_kv["c_k"], new_kv["c_v"] = ck, cv
            lamv = jnp.concatenate([lam_q1[li][None], lam_k1[li][None], lam_q2[li][None], lam_k2[li][None],
                                    jnp.zeros((4, HEAD_DIM), f32)], axis=0)
            subln = subln_c[li][None, :]
            o_p = _ctx1(lamv, subln, qkv, lam_init)
            o_s = _lat1(lamv, subln, qkv,
                        cache_c_k[:, li].reshape(N_SAMPLE_BATCH, PAST_LEN, D_MODEL),
                        cache_c_v[:, li].reshape(N_SAMPLE_BATCH, PAST_LEN, D_MODEL), lam_init)
            w_out = w_out_c[li].astype(bf16)
        x = _outproj(x, o_p, o_s, mod_l, w_out)
        x = _moe(x, mod_l, g_ffn, router_w[layer].T, router_bias[layer][:, None],
                 exp_w_gate, exp_w_up, exp_w_down, layer,
                 sh_w_gate[layer].astype(bf16), sh_w_up[layer].astype(bf16), sh_w_down[layer].astype(bf16))
        x = (x,)
    y_prompt, y_sample = _final(x[0], final_norm[None, :])
    nb, s = N_PROMPT_BATCH, PROMPT_SEQ
    return (y_prompt.reshape(nb, s, D_MODEL), y_sample.reshape(N_SAMPLE_BATCH, SAMPLE_SEQ, D_MODEL),
            new_kv["a_k"].reshape(nb, 1, s, 2, HEAD_DIM), new_kv["a_v"].reshape(nb, 1, s, 2, HEAD_DIM),
            new_kv["b_k"].reshape(nb, 1, s, 8, HEAD_DIM), new_kv["b_v"].reshape(nb, 1, s, 8, HEAD_DIM),
            new_kv["c_k"].reshape(nb, 1, s, 8, 2, HEAD_DIM), new_kv["c_v"].reshape(nb, 1, s, 8, 2 * HEAD_DIM))
```

```python
import functools
import math

import jax
import jax.numpy as jnp
import numpy as np
from jax import lax
from jax.experimental import pallas as pl
from jax.experimental.pallas import tpu as pltpu

f32 = jnp.float32
bf16 = jnp.bfloat16
i32 = jnp.int32

D_MODEL = 1024
N_PROMPT_BATCH = 16
PROMPT_SEQ = 256
DEPTH = 2
N_SAMPLE_BATCH = 2
SAMPLE_SEQ = 2048
PAST_LEN = 512
GRID_W = 64
HEAD_DIM = 64
ROPE_THETA = 10000.0
EPS = 1e-6
A_WINDOW = 128
NA_ROWS = 8
NA_COLS = 16
N_EXPERTS = 64
TOP_K = 6
N_GROUPS = 8
TOPK_GROUPS = 4
EXPERT_DIM = 256
ROUTED_SCALE = 2.5
Q_SCALE = HEAD_DIM ** -0.5

N_PROMPT = N_PROMPT_BATCH * PROMPT_SEQ
N_SAMPLE = N_SAMPLE_BATCH * SAMPLE_SEQ
N_TOK = N_PROMPT + N_SAMPLE

LANES = 128
TM = 256
N_PROMPT_TILES = N_PROMPT // TM
N_TILES = N_TOK // TM
TILES_PER_SAMPLE = SAMPLE_SEQ // TM
QB = 128
CHUNK = 16
SLOTS = -(-(TM * TOP_K + N_EXPERTS * (CHUNK - 1)) // 256) * 256
SLOT_CHUNKS = SLOTS // CHUNK
ROUTE_ROWS = 512
XW = D_MODEL + LANES
GM = 256
_MAX_SORTED = TM * TOP_K * N_TILES + N_TILES * N_EXPERTS * (CHUNK - 1) + N_EXPERTS * (GM - CHUNK)
G_TILES = -(-_MAX_SORTED // GM)
G_CHUNKS = GM // CHUNK
GMM_SLOTS = 4
VMEM_LIMIT = 56 * 1024 * 1024

NEG = -1e30


def _cparams(sem):
    return pltpu.CompilerParams(dimension_semantics=sem, vmem_limit_bytes=VMEM_LIMIT)


def _mod_row(i):
    return jnp.where(i < N_PROMPT_TILES, 0, 1 + (i - N_PROMPT_TILES) // TILES_PER_SAMPLE)


def _x_specs(parts):
    if len(parts) == 1:
        return [pl.BlockSpec((TM, D_MODEL), lambda i, *_: (i, 0))]
    return [pl.BlockSpec((TM, D_MODEL), lambda i, *_: (jnp.minimum(i, N_PROMPT_TILES - 1), 0)),
            pl.BlockSpec((TM, D_MODEL), lambda i, *_: (jnp.maximum(i - N_PROMPT_TILES, 0), 0))]


def _load_x(i, x_refs):
    if len(x_refs) == 1:
        return x_refs[0][...]
    return jnp.where(i < N_PROMPT_TILES, x_refs[0][...], x_refs[1][...])


def _norm_mod(x, g, scale, shift):
    y = x * lax.rsqrt(jnp.mean(x * x, axis=-1, keepdims=True) + EPS)
    return (y * g) * (1.0 + scale) + shift


def _silu(x):
    return x * jax.nn.sigmoid(x)


def _dot(a, b):
    return jnp.dot(a, b, preferred_element_type=f32)


def _dot_nt(a, b):
    return lax.dot_general(a, b, (((1,), (1,)), ((), ())), preferred_element_type=f32)


ADA_COLS = 1536


def _adaln_kernel(cond_ref, w_ref, b_ref, o_ref):
    s = _silu(cond_ref[...]).astype(bf16)
    o_ref[...] = _dot(s, w_ref[...].astype(bf16)) + b_ref[...]


def _adaln(cond8, w_mod, b_mod):
    n6 = 6 * D_MODEL
    return pl.pallas_call(
        _adaln_kernel,
        grid=(DEPTH, n6 // ADA_COLS),
        in_specs=[
            pl.BlockSpec((8, D_MODEL), lambda l, j: (0, 0)),
            pl.BlockSpec((None, D_MODEL, ADA_COLS), lambda l, j: (l, 0, j)),
            pl.BlockSpec((None, 1, ADA_COLS), lambda l, j: (l, 0, j)),
        ],
        out_specs=pl.BlockSpec((None, 8, ADA_COLS), lambda l, j: (l, 0, j)),
        out_shape=jax.ShapeDtypeStruct((DEPTH, 8, n6), f32),
        compiler_params=_cparams(("parallel", "parallel")),
        name="adaln",
    )(cond8, w_mod, b_mod.reshape(DEPTH, 1, n6))


def _rope_block(blk, cos, sin_a, sin_b):
    return blk * cos + pltpu.roll(blk, LANES - 16, 1) * sin_a + pltpu.roll(blk, 16, 1) * sin_b


def _inproj_kernel(*refs, chunks, n_x):
    x_refs, kv_refs = refs[:n_x], refs[n_x + 7:]
    mod_ref, g_ref, w_ref, cos_ref, sa_ref, sb_ref, qkv_ref = refs[n_x:n_x + 7]
    i = pl.program_id(0)
    h = _norm_mod(_load_x(i, x_refs), g_ref[...], mod_ref[1:2, :], mod_ref[0:1, :]).astype(bf16)
    is_prompt = i < N_PROMPT_TILES
    for c0, c1, rope_blocks, kv_out in chunks:
        acc = _dot(h, w_ref[:, c0:c1])
        if rope_blocks:
            @pl.when(jnp.logical_not(is_prompt))
            def _():
                cos, sa, sb = cos_ref[...], sa_ref[...], sb_ref[...]
                for b in range((c1 - c0) // LANES):
                    blk = acc[:, b * LANES:(b + 1) * LANES]
                    if b in rope_blocks:
                        blk = _rope_block(blk, cos, sa, sb)
                    qkv_ref[:, c0 + b * LANES:c0 + (b + 1) * LANES] = blk.astype(bf16)

            @pl.when(is_prompt)
            def _():
                qkv_ref[:, c0:c1] = acc.astype(bf16)
        else:
            qkv_ref[:, c0:c1] = acc.astype(bf16)
        if kv_out:
            @pl.when(is_prompt)
            def _():
                for ridx, a0, a1, o0 in kv_out:
                    kv_refs[ridx][:, o0:o0 + (a1 - a0)] = acc[:, a0:a1]


def _inproj(x_parts, mod_l, g, w, rope_tabs, chunks, kv_widths):
    n = w.shape[1]
    cos, sa, sb = rope_tabs

    def rope_idx(i):
        return (jnp.where(i < N_PROMPT_TILES, 0, (i - N_PROMPT_TILES) % TILES_PER_SAMPLE), 0)

    def kv_idx(i):
        return (jnp.minimum(i, N_PROMPT_TILES - 1), 0)

    return pl.pallas_call(
        functools.partial(_inproj_kernel, chunks=chunks, n_x=len(x_parts)),
        grid=(N_TILES,),
        in_specs=_x_specs(x_parts) + [
            pl.BlockSpec((None, 6, D_MODEL), lambda i: (_mod_row(i), 0, 0)),
            pl.BlockSpec((1, D_MODEL), lambda i: (0, 0)),
            pl.BlockSpec((D_MODEL, n), lambda i: (0, 0)),
            pl.BlockSpec((TM, LANES), rope_idx),
            pl.BlockSpec((TM, LANES), rope_idx),
            pl.BlockSpec((TM, LANES), rope_idx),
        ],
        out_specs=[pl.BlockSpec((TM, n), lambda i: (i, 0))]
        + [pl.BlockSpec((TM, wd), kv_idx) for wd in kv_widths],
        out_shape=[jax.ShapeDtypeStruct((N_TOK, n), bf16)]
        + [jax.ShapeDtypeStruct((N_PROMPT, wd), f32) for wd in kv_widths],
        compiler_params=_cparams(("arbitrary",)),
        name="inproj",
    )(*x_parts, mod_l, g, w, cos, sa, sb)


def _rope_tables():
    nq = HEAD_DIM // 4
    t = jnp.arange(SAMPLE_SEQ)
    inv = jnp.power(ROPE_THETA, -jnp.arange(nq, dtype=f32) / nq)
    ang_r = (t // GRID_W).astype(f32)[:, None] * inv
    ang_c = (t % GRID_W).astype(f32)[:, None] * inv
    zero = jnp.zeros_like(ang_r)

    def head(fr, fc):
        return jnp.concatenate([fr[0], fr[1], fc[0], fc[1]], axis=-1)

    cos = head((jnp.cos(ang_r), jnp.cos(ang_r)), (jnp.cos(ang_c), jnp.cos(ang_c)))
    sin_a = head((-jnp.sin(ang_r), zero), (-jnp.sin(ang_c), zero))
    sin_b = head((zero, jnp.sin(ang_r)), (zero, jnp.sin(ang_c)))
    two = lambda a: jnp.concatenate([a, a], axis=-1)
    return two(cos), two(sin_a), two(sin_b)


def _lane_lo(shape):
    return lax.broadcasted_iota(i32, shape, len(shape) - 1) < HEAD_DIM


def _half(q, lo_mask, half):
    keep = lo_mask if half == 0 else jnp.logical_not(lo_mask)
    return jnp.where(keep, q, jnp.zeros_like(q)) * Q_SCALE


def _swap_halves(x):
    return pltpu.roll(x.astype(f32), HEAD_DIM, 1).astype(x.dtype)


def _softmax_pv(score_blocks, value_blocks, sink=None):
    m = functools.reduce(jnp.maximum, [jnp.max(s, axis=-1, keepdims=True) for s in score_blocks])
    if sink is not None:
        m = jnp.maximum(m, sink)
    es = [jnp.exp(s - m) for s in score_blocks]
    den = functools.reduce(lambda a, b: a + b, [jnp.sum(e, axis=-1, keepdims=True) for e in es])
    if sink is not None:
        den = den + jnp.exp(sink - m)
    o = functools.reduce(lambda a, b: a + b, [_dot(e.astype(bf16), v) for e, v in zip(es, value_blocks)])
    return o * (1.0 / den)


L0_QA, L0_QB, L0_KB, L0_VB, L0_KA, L0_VA, L0_N = 0, 512, 1024, 1536, 2048, 2176, 2304


def _ctx0_kernel(sink_ref, qkv_ref, o_ref):
    lo = _lane_lo((1, LANES))
    k_a = qkv_ref[:, L0_KA:L0_KA + LANES]
    v_a = qkv_ref[:, L0_VA:L0_VA + LANES]
    k_a_sw, v_a_sw = _swap_halves(k_a), _swap_halves(v_a)
    for j in range(4):
        q_a = qkv_ref[:, L0_QA + j * LANES:L0_QA + (j + 1) * LANES]
        q_b = qkv_ref[:, L0_QB + j * LANES:L0_QB + (j + 1) * LANES]
        k_b = qkv_ref[:, L0_KB + j * LANES:L0_KB + (j + 1) * LANES]
        v_b = qkv_ref[:, L0_VB + j * LANES:L0_VB + (j + 1) * LANES]
        kv_head = j // 2
        outs_a, outs_b = [], []
        for half in range(2):
            k_use, v_use = (k_a, v_a) if kv_head == half else (k_a_sw, v_a_sw)
            s = _dot_nt(_half(q_a, lo, half), k_use)
            outs_a.append(_softmax_pv([s], [v_use], sink=sink_ref[2 * j + half]))
            s = _dot_nt(_half(q_b, lo, half), k_b)
            outs_b.append(_softmax_pv([s], [v_b]))
        o_ref[:, j * LANES:(j + 1) * LANES] = jnp.where(lo, outs_a[0], outs_a[1]).astype(bf16)
        o_ref[:, 512 + j * LANES:512 + (j + 1) * LANES] = jnp.where(lo, outs_b[0], outs_b[1]).astype(bf16)


def _ctx0(sink, qkv):
    return pl.pallas_call(
        _ctx0_kernel,
        grid=(N_PROMPT_BATCH,),
        in_specs=[
            pl.BlockSpec(memory_space=pltpu.SMEM),
            pl.BlockSpec((PROMPT_SEQ, L0_N), lambda b: (b, 0)),
        ],
        out_specs=pl.BlockSpec((PROMPT_SEQ, D_MODEL), lambda b: (b, 0)),
        out_shape=jax.ShapeDtypeStruct((N_PROMPT, D_MODEL), bf16),
        compiler_params=_cparams(("parallel",)),
        name="ctx0",
    )(sink, qkv)


WIN_KEYS = 3 * QB
NA_KEY_ROWS = 10
NA_KEYS = NA_KEY_ROWS * GRID_W
N_QB = SAMPLE_SEQ // QB
N_NA_PATTERNS = 5
_PROMPT_QBLOCKS = N_PROMPT // QB


def _na_pattern(n):
    return jnp.where(n < 2, n, jnp.where(n > N_QB - 3, n - (N_QB - 5), 2))


def _lat0_kernel(sink_ref, q_ref, kvb_ref, kva_ref, cak_ref, cav_ref, cbk_ref, cbv_ref, nab_ref, o_ref):
    n = pl.program_id(1)
    lo = _lane_lo((1, LANES))
    kstart = pl.multiple_of(jnp.clip((n - 1) * QB, 0, SAMPLE_SEQ - WIN_KEYS), QB)
    k_a = kva_ref[pl.ds(kstart, WIN_KEYS), 0:LANES]
    v_a = kva_ref[pl.ds(kstart, WIN_KEYS), LANES:2 * LANES]
    c_k = cak_ref[...].astype(bf16)
    c_v = cav_ref[...].astype(bf16)
    k_a_sw, v_a_sw, c_k_sw, c_v_sw = (_swap_halves(t) for t in (k_a, v_a, c_k, c_v))
    qpos = n * QB + lax.broadcasted_iota(i32, (QB, WIN_KEYS), 0)
    kpos = kstart + lax.broadcasted_iota(i32, (QB, WIN_KEYS), 1)
    in_window = jnp.abs(qpos - kpos) <= A_WINDOW
    krow = jnp.clip(2 * n - NA_ROWS // 2, 0, SAMPLE_SEQ // GRID_W - NA_KEY_ROWS)
    ktok = pl.multiple_of(krow * GRID_W, QB)
    for j in range(4):
        q_a = q_ref[:, L0_QA + j * LANES:L0_QA + (j + 1) * LANES]
        q_b = q_ref[:, L0_QB + j * LANES:L0_QB + (j + 1) * LANES]
        k_b = kvb_ref[pl.ds(ktok, NA_KEYS), j * LANES:(j + 1) * LANES]
        v_b = kvb_ref[pl.ds(ktok, NA_KEYS), 512 + j * LANES:512 + (j + 1) * LANES]
        cb_k = cbk_ref[:, j * LANES:(j + 1) * LANES].astype(bf16)
        cb_v = cbv_ref[:, j * LANES:(j + 1) * LANES].astype(bf16)
        kv_head = j // 2
        outs_a, outs_b = [], []
        for half in range(2):
            same = kv_head == half
            qh = _half(q_a, lo, half)
            s_loc = jnp.where(in_window, _dot_nt(qh, k_a if same else k_a_sw), NEG)
            s_ctx = _dot_nt(qh, c_k if same else c_k_sw)
            outs_a.append(_softmax_pv([s_ctx, s_loc], [c_v if same else c_v_sw, v_a if same else v_a_sw],
                                      sink=sink_ref[2 * j + half]))
            qh = _half(q_b, lo, half)
            s_loc = _dot_nt(qh, k_b) + nab_ref[2 * j + half]
            s_ctx = _dot_nt(qh, cb_k)
            outs_b.append(_softmax_pv([s_ctx, s_loc], [cb_v, v_b]))
        o_ref[:, j * LANES:(j + 1) * LANES] = jnp.where(lo, outs_a[0], outs_a[1]).astype(bf16)
        o_ref[:, 512 + j * LANES:512 + (j + 1) * LANES] = jnp.where(lo, outs_b[0], outs_b[1]).astype(bf16)


def _lat0(sink, qkv, cak, cav, cbk, cbv, nab):
    sb = N_PROMPT // SAMPLE_SEQ
    return pl.pallas_call(
        _lat0_kernel,
        grid=(N_SAMPLE_BATCH, N_QB),
        in_specs=[
            pl.BlockSpec(memory_space=pltpu.SMEM),
            pl.BlockSpec((QB, 1024), lambda b, n: (_PROMPT_QBLOCKS + b * N_QB + n, 0)),
            pl.BlockSpec((SAMPLE_SEQ, 1024), lambda b, n: (sb + b, 1)),
            pl.BlockSpec((SAMPLE_SEQ, 256), lambda b, n: (sb + b, L0_KA // 256)),
            pl.BlockSpec((None, PAST_LEN, LANES), lambda b, n: (b, 0, 0)),
            pl.BlockSpec((None, PAST_LEN, LANES), lambda b, n: (b, 0, 0)),
            pl.BlockSpec((None, PAST_LEN, 512), lambda b, n: (b, 0, 0)),
            pl.BlockSpec((None, PAST_LEN, 512), lambda b, n: (b, 0, 0)),
            pl.BlockSpec((None, 8, QB, NA_KEYS), lambda b, n: (_na_pattern(n), 0, 0, 0)),
        ],
        out_specs=pl.BlockSpec((QB, D_MODEL), lambda b, n: (b * N_QB + n, 0)),
        out_shape=jax.ShapeDtypeStruct((N_SAMPLE, D_MODEL), bf16),
        compiler_params=_cparams(("parallel", "arbitrary")),
        name="lat0",
    )(sink, qkv, qkv, qkv, cak, cav, cbk, cbv, nab)


def _na_bias_table(rel_bias):
    rows = SAMPLE_SEQ // GRID_W
    n_dr, n_dc = 2 * NA_ROWS - 1, 2 * NA_COLS - 1
    c = np.arange(GRID_W)[:, None]
    kc = np.arange(GRID_W)[None, :]
    cs = np.clip(c - NA_COLS // 2, 0, GRID_W - NA_COLS)
    col_ok = (kc >= cs) & (kc < cs + NA_COLS)
    col_hot = ((kc - c + NA_COLS - 1)[None] == np.arange(n_dc)[:, None, None]) & col_ok[None]
    row_hot = np.zeros((N_NA_PATTERNS, 2, NA_KEY_ROWS, n_dr), np.float32)
    for p, n in enumerate((0, 1, 2, N_QB - 2, N_QB - 1)):
        k0 = int(np.clip(2 * n - NA_ROWS // 2, 0, rows - NA_KEY_ROWS))
        for rq in range(2):
            r = 2 * n + rq
            rs = int(np.clip(r - NA_ROWS // 2, 0, rows - NA_ROWS))
            for kl in range(NA_KEY_ROWS):
                if rs <= k0 + kl < rs + NA_ROWS:
                    row_hot[p, rq, kl, k0 + kl - r + NA_ROWS - 1] = 1.0
    ok = (row_hot.sum(-1) > 0)[:, None, :, None, :, None] & col_ok[None, None, None, :, None, :]
    hp = lax.Precision.HIGHEST
    toeplitz = jnp.einsum("hdx,xck->hdck", rel_bias.astype(f32), col_hot.astype(np.float32), precision=hp)
    tab = jnp.einsum("prkd,hdcx->phrckx", row_hot, toeplitz, precision=hp)
    tab = tab + np.where(ok, 0.0, NEG).astype(np.float32)
    return tab.reshape(N_NA_PATTERNS, 8, QB, NA_KEYS)


def _diff_lambda(lam_ref, lam_init):
    lv = lam_ref[...]
    s1 = jnp.sum(lv[0:1, :] * lv[1:2, :], axis=-1, keepdims=True)
    s2 = jnp.sum(lv[2:3, :] * lv[3:4, :], axis=-1, keepdims=True)
    return jnp.exp(s1) - jnp.exp(s2) + lam_init


def _softmax_parts(score_blocks):
    m = functools.reduce(jnp.maximum, [jnp.max(s, axis=-1, keepdims=True) for s in score_blocks])
    es = [jnp.exp(s - m) for s in score_blocks]
    den = functools.reduce(lambda a, b: a + b, [jnp.sum(e, axis=-1, keepdims=True) for e in es])
    return es, 1.0 / den


def _diff_head(q, key_blocks, value_blocks, lam, subln, lo, lam_init):
    q1, q2 = _half(q, lo, 0), _half(q, lo, 1)
    e1, r1 = _softmax_parts([_dot_nt(q1, k) for k in key_blocks])
    e2, r2 = _softmax_parts([_dot_nt(q2, k) for k in key_blocks])
    r2 = r2 * lam
    o = functools.reduce(lambda a, b: a + b,
                         [_dot((a * r1 - b * r2).astype(bf16), v) for a, b, v in zip(e1, e2, value_blocks)])
    o = o * lax.rsqrt(jnp.mean(o * o, axis=-1, keepdims=True) + EPS)
    return (o * subln) * (1.0 - lam_init)


def _ctx1_kernel(lam_ref, subln_ref, qkv_ref, o_ref, *, lam_init):
    lo = _lane_lo((1, LANES))
    lam = _diff_lambda(lam_ref, lam_init)
    subln = subln_ref[...]
    for h in range(8):
        q = qkv_ref[:, h * LANES:(h + 1) * LANES]
        k = qkv_ref[:, 1024 + h * LANES:1024 + (h + 1) * LANES]
        v = qkv_ref[:, 2048 + h * LANES:2048 + (h + 1) * LANES]
        o_ref[:, h * LANES:(h + 1) * LANES] = _diff_head(q, [k], [v], lam, subln, lo, lam_init).astype(bf16)


def _ctx1(lamv, subln, qkv, lam_init):
    return pl.pallas_call(
        functools.partial(_ctx1_kernel, lam_init=lam_init),
        grid=(N_PROMPT_BATCH,),
        in_specs=[
            pl.BlockSpec((8, HEAD_DIM), lambda b: (0, 0)),
            pl.BlockSpec((1, LANES), lambda b: (0, 0)),
            pl.BlockSpec((PROMPT_SEQ, 3 * D_MODEL), lambda b: (b, 0)),
        ],
        out_specs=pl.BlockSpec((PROMPT_SEQ, D_MODEL), lambda b: (b, 0)),
        out_shape=jax.ShapeDtypeStruct((N_PROMPT, D_MODEL), bf16),
        compiler_params=_cparams(("parallel",)),
        name="ctx1",
    )(lamv, subln, qkv)


def _lat1_kernel(lam_ref, subln_ref, q_ref, k_ref, v_ref, ck_ref, cv_ref, o_ref, *, lam_init):
    lo = _lane_lo((1, LANES))
    lam = _diff_lambda(lam_ref, lam_init)
    subln = subln_ref[...]
    for h in range(8):
        sl = slice(h * LANES, (h + 1) * LANES)
        o_ref[:, sl] = _diff_head(q_ref[:, sl], [ck_ref[:, sl].astype(bf16), k_ref[:, sl]],
                                  [cv_ref[:, sl].astype(bf16), v_ref[:, sl]],
                                  lam, subln, lo, lam_init).astype(bf16)


def _lat1(lamv, subln, qkv, ck, cv, lam_init):
    sb = N_PROMPT // SAMPLE_SEQ
    nq = SAMPLE_SEQ // TM
    return pl.pallas_call(
        functools.partial(_lat1_kernel, lam_init=lam_init),
        grid=(N_SAMPLE_BATCH, nq),
        in_specs=[
            pl.BlockSpec((8, HEAD_DIM), lambda b, n: (0, 0)),
            pl.BlockSpec((1, LANES), lambda b, n: (0, 0)),
            pl.BlockSpec((TM, D_MODEL), lambda b, n: (N_PROMPT_TILES + b * nq + n, 0)),
            pl.BlockSpec((SAMPLE_SEQ, D_MODEL), lambda b, n: (sb + b, 1)),
            pl.BlockSpec((SAMPLE_SEQ, D_MODEL), lambda b, n: (sb + b, 2)),
            pl.BlockSpec((None, PAST_LEN, D_MODEL), lambda b, n: (b, 0, 0)),
            pl.BlockSpec((None, PAST_LEN, D_MODEL), lambda b, n: (b, 0, 0)),
        ],
        out_specs=pl.BlockSpec((TM, D_MODEL), lambda b, n: (b * nq + n, 0)),
        out_shape=jax.ShapeDtypeStruct((N_SAMPLE, D_MODEL), bf16),
        compiler_params=_cparams(("parallel", "arbitrary")),
        name="lat1",
    )(lamv, subln, qkv, qkv, qkv, ck, cv)


def _outproj_kernel(*refs, n_x):
    x_refs = refs[:n_x]
    op_ref, os_ref, mod_ref, w_ref, o_ref = refs[n_x:]
    i = pl.program_id(0)
    attn = jnp.where(i < N_PROMPT_TILES, op_ref[...], os_ref[...])
    o_ref[...] = _load_x(i, x_refs) + mod_ref[2:3, :] * _dot(attn, w_ref[...])


def _outproj(x_parts, o_prompt, o_sample, mod_l, w):
    return pl.pallas_call(
        functools.partial(_outproj_kernel, n_x=len(x_parts)),
        grid=(N_TILES,),
        in_specs=_x_specs(x_parts) + [
            pl.BlockSpec((TM, D_MODEL), lambda i: (jnp.minimum(i, N_PROMPT_TILES - 1), 0)),
            pl.BlockSpec((TM, D_MODEL), lambda i: (jnp.maximum(i - N_PROMPT_TILES, 0), 0)),
            pl.BlockSpec((None, 6, D_MODEL), lambda i: (_mod_row(i), 0, 0)),
            pl.BlockSpec((D_MODEL, D_MODEL), lambda i: (0, 0)),
        ],
        out_specs=pl.BlockSpec((TM, D_MODEL), lambda i: (i, 0)),
        out_shape=jax.ShapeDtypeStruct((N_TOK, D_MODEL), f32),
        compiler_params=_cparams(("parallel",)),
        name="outproj",
    )(*x_parts, o_prompt, o_sample, mod_l, w)


def _split_bf16(a):
    hi = a.astype(bf16)
    return hi, (a - hi.astype(f32)).astype(bf16)


def _route_kernel(x_ref, mod_ref, g_ref, rwt_ref, rb_ref, xloc_ref, p_ref, len_ref):
    ng, ge = N_GROUPS, N_EXPERTS // N_GROUPS
    h = _norm_mod(x_ref[...], g_ref[...], mod_ref[4:5, :], mod_ref[3:4, :])
    h_hi, h_lo = _split_bf16(h)
    w_hi, w_lo = _split_bf16(rwt_ref[...])
    logits = _dot_nt(w_hi, h_hi) + (_dot_nt(w_hi, h_lo) + _dot_nt(w_lo, h_hi))
    scores = jax.nn.sigmoid(logits)
    biased = scores + rb_ref[...]
    s3 = scores.reshape(ng, ge, TM)
    b3 = biased.reshape(ng, ge, TM)
    in_group = lax.broadcasted_iota(i32, (ng, ge, TM), 1).astype(f32)
    group_id = lax.broadcasted_iota(i32, (ng, 1, TM), 0).astype(f32)
    expert_id = lax.broadcasted_iota(i32, (ng, ge, TM), 0).astype(f32) * ge + in_group

    def max01(a):
        return jnp.max(jnp.max(a, axis=0, keepdims=True), axis=1, keepdims=True)

    def min01(a):
        return jnp.min(jnp.min(a, axis=0, keepdims=True), axis=1, keepdims=True)

    def sum01(a):
        return jnp.sum(jnp.sum(a, axis=0, keepdims=True), axis=1, keepdims=True)

    m1 = jnp.max(b3, axis=1, keepdims=True)
    first = jnp.min(jnp.where(b3 == m1, in_group, ge), axis=1, keepdims=True)
    m2 = jnp.max(jnp.where(in_group == first, -jnp.inf, b3), axis=1, keepdims=True)
    gscore = m1 + m2
    gsel = jnp.zeros((ng, 1, TM), f32)
    for _ in range(TOPK_GROUPS):
        gm = jnp.max(gscore, axis=0, keepdims=True)
        gi = jnp.min(jnp.where(gscore == gm, group_id, ng), axis=0, keepdims=True)
        hit = group_id == gi
        gsel = jnp.where(hit, 1.0, gsel)
        gscore = jnp.where(hit, -jnp.inf, gscore)
    cand = jnp.where(jnp.broadcast_to(gsel, (ng, ge, TM)) > 0.0, b3, -jnp.inf)
    top_e, top_w = [], []
    for _ in range(TOP_K):
        em = max01(cand)
        ei = min01(jnp.where(cand == em, expert_id, N_EXPERTS))
        hit = expert_id == ei
        top_e.append(ei)
        top_w.append(sum01(jnp.where(hit, s3, 0.0)))
        cand = jnp.where(hit, -jnp.inf, cand)
    wsum = functools.reduce(lambda a, b: a + b, top_w)
    gates3 = jnp.zeros((ng, ge, TM), f32)
    sel3 = jnp.zeros((ng, ge, TM), f32)
    for ei, w in zip(top_e, top_w):
        hit = expert_id == ei
        gates3 = jnp.where(hit, w / wsum * ROUTED_SCALE, gates3)
        sel3 = jnp.where(hit, 1.0, sel3)
    gates = gates3.reshape(N_EXPERTS, TM)
    sel = sel3.reshape(N_EXPERTS, TM)

    cnt = jnp.sum(sel, axis=1, keepdims=True)
    run_len = jnp.ceil(cnt * (1.0 / CHUNK)) * CHUNK
    r_i = lax.broadcasted_iota(i32, (N_EXPERTS, N_EXPERTS), 0)
    c_i = lax.broadcasted_iota(i32, (N_EXPERTS, N_EXPERTS), 1)
    lower = jnp.where(c_i < r_i, 1.0, 0.0).astype(bf16)
    run_off = _dot(lower, jnp.broadcast_to(run_len, (N_EXPERTS, LANES)).astype(bf16))[:, 0:1]
    t_r = lax.broadcasted_iota(i32, (TM, TM), 0)
    t_c = lax.broadcasted_iota(i32, (TM, TM), 1)
    before = jnp.where(t_r < t_c, 1.0, 0.0).astype(bf16)
    rank = _dot(sel.astype(bf16), before)
    slot3 = (run_off + rank).reshape(ng, ge, TM)
    slots = [sum01(jnp.where(expert_id == ei, slot3, 0.0)).reshape(1, TM).astype(i32) for ei in top_e]
    len_ref[...] = jnp.broadcast_to(run_len, (N_EXPERTS, LANES)).astype(i32)

    g_parts = jnp.concatenate(_split_bf16(gates), axis=0)
    rows = ROUTE_ROWS

    def body(c, carry):
        base = pl.multiple_of(c * rows, rows)
        row_id = base + lax.broadcasted_iota(i32, (rows, TM), 0)
        p = jnp.zeros((rows, TM), f32)
        for k in range(TOP_K):
            p = jnp.where(row_id == slots[k], 1.0, p)
        p = p.astype(bf16)
        p_ref[pl.ds(base, rows), :] = p
        xloc_ref[pl.ds(base, rows), 0:D_MODEL] = _dot(p, h_hi).astype(bf16)
        xloc_ref[pl.ds(base, rows), D_MODEL:XW] = _dot_nt(p, g_parts).astype(bf16)
        return carry

    def zero_body(c, carry):
        base = pl.multiple_of(c * rows, rows)
        p_ref[pl.ds(base, rows), :] = jnp.zeros((rows, TM), bf16)
        xloc_ref[pl.ds(base, rows), :] = jnp.zeros((rows, XW), bf16)
        return carry

    n_used = (jnp.sum(run_len).astype(i32) + (rows - 1)) // rows
    lax.fori_loop(0, n_used, body, 0)
    lax.fori_loop(n_used, SLOTS // rows, zero_body, 0)


def _route(x, mod_l, g, rwt, rb):
    return pl.pallas_call(
        _route_kernel,
        grid=(N_TILES,),
        in_specs=[
            pl.BlockSpec((TM, D_MODEL), lambda i: (i, 0)),
            pl.BlockSpec((None, 6, D_MODEL), lambda i: (_mod_row(i), 0, 0)),
            pl.BlockSpec((1, D_MODEL), lambda i: (0, 0)),
            pl.BlockSpec((N_EXPERTS, D_MODEL), lambda i: (0, 0)),
            pl.BlockSpec((N_EXPERTS, 1), lambda i: (0, 0)),
        ],
        out_specs=[
            pl.BlockSpec((SLOTS, XW), lambda i: (i, 0)),
            pl.BlockSpec((SLOTS, TM), lambda i: (i, 0)),
            pl.BlockSpec((None, N_EXPERTS, LANES), lambda i: (i, 0, 0)),
        ],
        out_shape=[
            jax.ShapeDtypeStruct((N_TILES * SLOTS, XW), bf16),
            jax.ShapeDtypeStruct((N_TILES * SLOTS, TM), bf16),
            jax.ShapeDtypeStruct((N_TILES, N_EXPERTS, LANES), i32),
        ],
        compiler_params=_cparams(("parallel",)),
        name="route",
    )(x, mod_l, g, rwt, rb)


def _moe_plan(run_len):
    nt, ne = run_len.shape

    def excl_cumsum(a):
        n = a.shape[-1]
        earlier = np.arange(n)[None, :] < np.arange(n)[:, None]
        return jnp.sum(jnp.where(earlier, a[..., None, :], 0), axis=-1)

    def first_diff(a):
        return a - jnp.concatenate([jnp.zeros_like(a[..., :1]), a[..., :-1]], axis=-1)

    off_loc = excl_cumsum(run_len)
    before = excl_cumsum(run_len.T).T
    n_e = jnp.sum(run_len, axis=0)
    n_pad = -(-n_e // GM) * GM
    g_start = excl_cumsum(n_pad)
    total = jnp.sum(n_pad)
    run_dst = g_start[None, :] + before
    run_src = jnp.arange(nt, dtype=i32)[:, None] * SLOTS + off_loc
    dst_f = run_dst.T.reshape(-1)
    shift_f = first_diff((run_src - run_dst).T.reshape(-1))
    rows = jnp.arange(G_TILES * G_CHUNKS, dtype=i32) * CHUNK
    shift = jnp.sum(jnp.where(dst_f[None, :] <= rows[:, None], shift_f[None, :], 0), axis=1)
    in_run = jnp.any((g_start[None, :] <= rows[:, None]) & (rows[:, None] < (g_start + n_e)[None, :]), axis=1)
    chunk_src = jnp.where(in_run, rows + shift, 0).astype(i32)
    loc_rows = jnp.arange(SLOT_CHUNKS, dtype=i32) * CHUNK
    shift_l = first_diff(run_dst - off_loc)
    shift = jnp.sum(jnp.where(off_loc[:, None, :] <= loc_rows[None, :, None], shift_l[:, None, :], 0), axis=2)
    used = jnp.sum(run_len, axis=1)
    chunk_map = jnp.where(loc_rows[None, :] < used[:, None], (loc_rows[None, :] + shift) // CHUNK, 0).astype(i32)
    return (g_start // GM).astype(i32), (n_pad // GM).astype(i32), chunk_src, chunk_map.reshape(-1)


def _gmm_in_copy(xloc_hbm, xbuf, sem, src_row, slot, c):
    return pltpu.make_async_copy(xloc_hbm.at[pl.ds(src_row, CHUNK)],
                                 xbuf.at[slot, pl.ds(c * CHUNK, CHUNK)], sem.at[slot])


def _gmm_out_copy(ybuf, y_hbm, sem, tile, slot):
    return pltpu.make_async_copy(ybuf.at[slot], y_hbm.at[pl.ds(pl.multiple_of(tile * GM, GM), GM)], sem.at[slot])


def _gmm_kernel(t0_ref, nt_ref, cs_ref, xloc_hbm, wg_ref, wu_ref, wd_ref, y_hbm,
                xbuf, ybuf, zbuf, wg_b, wu_b, wd_b, in_sem, out_sem, zsem):
    e = pl.program_id(0)
    last = pl.num_programs(0) - 1
    n_tiles = nt_ref[e]
    first_tile = t0_ref[e]
    total_tiles = t0_ref[last] + nt_ref[last]

    def start_in(tile):
        for c in range(G_CHUNKS):
            src = pl.multiple_of(cs_ref[tile * G_CHUNKS + c], CHUNK)
            _gmm_in_copy(xloc_hbm, xbuf, in_sem, src, tile % GMM_SLOTS, c).start()

    @pl.when(e == 0)
    def _():
        for tile in range(GMM_SLOTS - 1):
            start_in(tile)
        zbuf[...] = jnp.zeros(zbuf.shape, zbuf.dtype)

    def tail_copies(fn):
        for j in range(_GMM_TAIL_PER_STEP):
            tile = total_tiles + e + j * N_EXPERTS

            @pl.when(tile < G_TILES)
            def _():
                fn(pltpu.make_async_copy(zbuf, y_hbm.at[pl.ds(pl.multiple_of(tile * GM, GM), GM)], zsem.at[0]))

    tail_copies(lambda cp: cp.start())

    @pl.when(n_tiles > 0)
    def _():
        wg_b[...] = wg_ref[...].astype(bf16)
        wu_b[...] = wu_ref[...].astype(bf16)
        wd_b[...] = wd_ref[...].astype(bf16)

    lane = lax.broadcasted_iota(i32, (1, LANES), 1)
    gate_lanes = jnp.logical_or(lane == e, lane == e + N_EXPERTS)

    def body(t, carry):
        tile = first_tile + t
        slot = tile % GMM_SLOTS

        @pl.when(tile + (GMM_SLOTS - 1) < total_tiles)
        def _():
            start_in(tile + (GMM_SLOTS - 1))

        for c in range(G_CHUNKS):
            _gmm_in_copy(xloc_hbm, xbuf, in_sem, 0, slot, c).wait()

        @pl.when(tile >= GMM_SLOTS)
        def _():
            _gmm_out_copy(ybuf, y_hbm, out_sem, tile - GMM_SLOTS, slot).wait()

        x = xbuf[slot]
        gate = jnp.sum(jnp.where(gate_lanes, x[:, D_MODEL:XW].astype(f32), 0.0), axis=1, keepdims=True)
        xa = x[:, 0:D_MODEL]
        act = _silu(_dot(xa, wg_b[...])) * _dot(xa, wu_b[...]) * gate
        ybuf[slot] = _dot(act.astype(bf16), wd_b[...]).astype(bf16)
        _gmm_out_copy(ybuf, y_hbm, out_sem, tile, slot).start()
        return carry

    lax.fori_loop(0, n_tiles, body, 0)
    tail_copies(lambda cp: cp.wait())

    @pl.when(e == last)
    def _():
        for back in range(1, GMM_SLOTS + 1):
            tile = total_tiles - back
            _gmm_out_copy(ybuf, y_hbm, out_sem, tile, tile % GMM_SLOTS).wait()


_GMM_TAIL_PER_STEP = -(-(G_TILES - TM * TOP_K * N_TILES // GM) // N_EXPERTS)


def _gmm(tile_start, n_tiles, chunk_src, xloc, wg, wu, wd, layer):
    grid_spec = pltpu.PrefetchScalarGridSpec(
        num_scalar_prefetch=3,
        grid=(N_EXPERTS,),
        in_specs=[
            pl.BlockSpec(memory_space=pl.ANY),
            pl.BlockSpec((None, None, D_MODEL, EXPERT_DIM), lambda e, t0, nt, cs: (layer, e, 0, 0)),
            pl.BlockSpec((None, None, D_MODEL, EXPERT_DIM), lambda e, t0, nt, cs: (layer, e, 0, 0)),
            pl.BlockSpec((None, None, EXPERT_DIM, D_MODEL), lambda e, t0, nt, cs: (layer, e, 0, 0)),
        ],
        out_specs=pl.BlockSpec(memory_space=pl.ANY),
        scratch_shapes=[pltpu.VMEM((GMM_SLOTS, GM, XW), bf16), pltpu.VMEM((GMM_SLOTS, GM, D_MODEL), bf16),
                        pltpu.VMEM((GM, D_MODEL), bf16),
                        pltpu.VMEM((D_MODEL, EXPERT_DIM), bf16), pltpu.VMEM((D_MODEL, EXPERT_DIM), bf16),
                        pltpu.VMEM((EXPERT_DIM, D_MODEL), bf16),
                        pltpu.SemaphoreType.DMA((GMM_SLOTS,)), pltpu.SemaphoreType.DMA((GMM_SLOTS,)),
                        pltpu.SemaphoreType.DMA((1,))],
    )
    return pl.pallas_call(
        _gmm_kernel,
        grid_spec=grid_spec,
        out_shape=jax.ShapeDtypeStruct((G_TILES * GM, D_MODEL), bf16),
        compiler_params=_cparams(("arbitrary",)),
        name="gmm",
    )(tile_start, n_tiles, chunk_src, xloc, wg, wu, wd)


def _combine_copy(y_hbm, ybuf, sem, sorted_chunk, slot, c):
    return pltpu.make_async_copy(y_hbm.at[pl.ds(pl.multiple_of(sorted_chunk * CHUNK, CHUNK), CHUNK)],
                                 ybuf.at[slot, pl.ds(pl.multiple_of(c * CHUNK, CHUNK), CHUNK)], sem.at[slot])


def _combine_kernel(cm_ref, y_hbm, p_ref, x_ref, mod_ref, g_ref, sg_ref, su_ref, sd_ref, o_ref, ybuf, sem):
    i = pl.program_id(0)
    n = pl.num_programs(0)
    slot = i % 2

    def start(tile, s):
        for c in range(SLOT_CHUNKS):
            _combine_copy(y_hbm, ybuf, sem, cm_ref[tile * SLOT_CHUNKS + c], s, c).start()

    def wait(s):
        for c in range(SLOT_CHUNKS):
            _combine_copy(y_hbm, ybuf, sem, 0, s, c).wait()

    @pl.when(i == 0)
    def _():
        start(0, 0)

    wait(slot)
    start((i + 1) % n, 1 - slot)

    x = x_ref[...]
    hb = _norm_mod(x, g_ref[...], mod_ref[4:5, :], mod_ref[3:4, :]).astype(bf16)
    shared = _dot((_silu(_dot(hb, sg_ref[...])) * _dot(hb, su_ref[...])).astype(bf16), sd_ref[...])
    routed = lax.dot_general(p_ref[...], ybuf[slot], (((0,), (0,)), ((), ())), preferred_element_type=f32)
    o_ref[...] = x + mod_ref[5:6, :] * (routed + shared)

    @pl.when(i == n - 1)
    def _():
        wait(1 - slot)


def _combine(chunk_map, y, onehot, x, mod_l, g, sg, su, sd):
    shd = sg.shape[1]
    grid_spec = pltpu.PrefetchScalarGridSpec(
        num_scalar_prefetch=1,
        grid=(N_TILES,),
        in_specs=[
            pl.BlockSpec(memory_space=pl.ANY),
            pl.BlockSpec((SLOTS, TM), lambda i, cm: (i, 0)),
            pl.BlockSpec((TM, D_MODEL), lambda i, cm: (i, 0)),
            pl.BlockSpec((None, 6, D_MODEL), lambda i, cm: (_mod_row(i), 0, 0)),
            pl.BlockSpec((1, D_MODEL), lambda i, cm: (0, 0)),
            pl.BlockSpec((D_MODEL, shd), lambda i, cm: (0, 0)),
            pl.BlockSpec((D_MODEL, shd), lambda i, cm: (0, 0)),
            pl.BlockSpec((shd, D_MODEL), lambda i, cm: (0, 0)),
        ],
        out_specs=pl.BlockSpec((TM, D_MODEL), lambda i, cm: (i, 0)),
        scratch_shapes=[pltpu.VMEM((2, SLOTS, D_MODEL), bf16), pltpu.SemaphoreType.DMA((2,))],
    )
    return pl.pallas_call(
        _combine_kernel,
        grid_spec=grid_spec,
        out_shape=jax.ShapeDtypeStruct((N_TOK, D_MODEL), f32),
        compiler_params=_cparams(("arbitrary",)),
        name="combine",
    )(chunk_map, y, onehot, x, mod_l, g, sg, su, sd)


def _moe(x, mod_l, g, rwt, rb, wg, wu, wd, layer, sg, su, sd):
    xloc, onehot, run_len = _route(x, mod_l, g, rwt, rb)
    tile_start, n_tiles, chunk_src, chunk_map = _moe_plan(run_len[:, :, 0])
    y = _gmm(tile_start, n_tiles, chunk_src, xloc, wg, wu, wd, layer)
    return _combine(chunk_map, y, onehot, x, mod_l, g, sg, su, sd)


def _final_kernel(x_ref, g_ref, yp_ref, ys_ref):
    i = pl.program_id(0)
    x = x_ref[...]
    y = (x * lax.rsqrt(jnp.mean(x * x, axis=-1, keepdims=True) + EPS)) * g_ref[...]

    @pl.when(i < N_PROMPT_TILES)
    def _():
        yp_ref[...] = y

    @pl.when(i >= N_PROMPT_TILES)
    def _():
        ys_ref[...] = y


def _final(x, g):
    return pl.pallas_call(
        _final_kernel,
        grid=(N_TILES,),
        in_specs=[
            pl.BlockSpec((TM, D_MODEL), lambda i: (i, 0)),
            pl.BlockSpec((1, D_MODEL), lambda i: (0, 0)),
        ],
        out_specs=[
            pl.BlockSpec((TM, D_MODEL), lambda i: (jnp.minimum(i, N_PROMPT_TILES - 1), 0)),
            pl.BlockSpec((TM, D_MODEL), lambda i: (jnp.maximum(i - N_PROMPT_TILES, 0), 0)),
        ],
        out_shape=[jax.ShapeDtypeStruct((N_PROMPT, D_MODEL), f32), jax.ShapeDtypeStruct((N_SAMPLE, D_MODEL), f32)],
        compiler_params=_cparams(("arbitrary",)),
        name="final_norm",
    )(x, g)


def _permute_w_in_ab(w):
    return jnp.concatenate([w[:, 0:512], w[:, 768:1280], w[:, 1280:1792], w[:, 1792:2304],
                            w[:, 512:640], w[:, 640:768]], axis=1)


_L0_CHUNKS = (
    (0, 512, (0, 1, 2, 3), ()),
    (512, 1024, (), ()),
    (1024, 1536, (), ((2, 0, 512, 0),)),
    (1536, 2048, (), ((3, 0, 512, 0),)),
    (2048, 2304, (0,), ((0, 0, 128, 0), (1, 128, 256, 0))),
)
_L1_CHUNKS = (
    (0, 512, (0, 1, 2, 3), ()),
    (512, 1024, (0, 1, 2, 3), ()),
    (1024, 1536, (0, 1, 2, 3), ((0, 0, 512, 0),)),
    (1536, 2048, (0, 1, 2, 3), ((0, 0, 512, 512),)),
    (2048, 2560, (), ((1, 0, 512, 0),)),
    (2560, 3072, (), ((1, 0, 512, 512),)),
)


def kernel(x_prompt, x_sample, cache_a_k, cache_a_v, cache_b_k, cache_b_v, cache_c_k, cache_c_v, c, c_ctx, w_mod, b_mod, norm_mix, norm_ffn, w_in_ab, w_out_ab, sink_a, rel_bias_b, w_in_c, w_out_c, lam_q1, lam_k1, lam_q2, lam_k2, subln_c, router_w, router_bias, exp_w_gate, exp_w_up, exp_w_down, sh_w_gate, sh_w_up, sh_w_down, final_norm):
    x = (x_prompt.reshape(N_PROMPT, D_MODEL), x_sample.reshape(N_SAMPLE, D_MODEL))
    cond8 = jnp.concatenate([c_ctx[None, :], c, jnp.zeros((8 - 1 - N_SAMPLE_BATCH, D_MODEL), f32)], axis=0)
    mod = _adaln(cond8, w_mod, b_mod).reshape(DEPTH, 8, 6, D_MODEL)
    rope_tabs = _rope_tables()
    new_kv = {}
    for layer in range(DEPTH):
        li = layer // 2
        mod_l = mod[layer]
        g_mix = norm_mix[layer][None, :]
        g_ffn = norm_ffn[layer][None, :]
        if layer % 2 == 0:
            w_in = _permute_w_in_ab(w_in_ab[li]).astype(bf16)
            qkv, ak, av, bk, bv = _inproj(x, mod_l, g_mix, w_in, rope_tabs, _L0_CHUNKS, (128, 128, 512, 512))
            new_kv["a_k"], new_kv["a_v"], new_kv["b_k"], new_kv["b_v"] = ak, av, bk, bv
            o_p = _ctx0(sink_a[li], qkv)
            o_s = _lat0(sink_a[li], qkv,
                        cache_a_k[:, li].reshape(N_SAMPLE_BATCH, PAST_LEN, LANES),
                        cache_a_v[:, li].reshape(N_SAMPLE_BATCH, PAST_LEN, LANES),
                        cache_b_k[:, li].reshape(N_SAMPLE_BATCH, PAST_LEN, 512),
                        cache_b_v[:, li].reshape(N_SAMPLE_BATCH, PAST_LEN, 512),
                        _na_bias_table(rel_bias_b[li]))
            w_out = w_out_ab[li].astype(bf16)
        else:
            lam_init = 0.8 - 0.6 * math.exp(-0.3 * layer)
            qkv, ck, cv = _inproj(x, mod_l, g_mix, w_in_c[li].astype(bf16), rope_tabs, _L1_CHUNKS, (1024, 1024))
            new_kv["c_k"], new_kv["c_v"] = ck, cv
            lamv = jnp.concatenate([lam_q1[li][None], lam_k1[li][None], lam_q2[li][None], lam_k2[li][None],
                                    jnp.zeros((4, HEAD_DIM), f32)], axis=0)
            subln = subln_c[li][None, :]
            o_p = _ctx1(lamv, subln, qkv, lam_init)
            o_s = _lat1(lamv, subln, qkv,
                        cache_c_k[:, li].reshape(N_SAMPLE_BATCH, PAST_LEN, D_MODEL),
                        cache_c_v[:, li].reshape(N_SAMPLE_BATCH, PAST_LEN, D_MODEL), lam_init)
            w_out = w_out_c[li].astype(bf16)
        x = _outproj(x, o_p, o_s, mod_l, w_out)
        x = _moe(x, mod_l, g_ffn, router_w[layer].T, router_bias[layer][:, None],
                 exp_w_gate, exp_w_up, exp_w_down, layer,
                 sh_w_gate[layer].astype(bf16), sh_w_up[layer].astype(bf16), sh_w_down[layer].astype(bf16))
        x = (x,)
    y_prompt, y_sample = _final(x[0], final_norm[None, :])
    nb, s = N_PROMPT_BATCH, PROMPT_SEQ
    return (y_prompt.reshape(nb, s, D_MODEL), y_sample.reshape(N_SAMPLE_BATCH, SAMPLE_SEQ, D_MODEL),
            new_kv["a_k"].reshape(nb, 1, s, 2, HEAD_DIM), new_kv["a_v"].reshape(nb, 1, s, 2, HEAD_DIM),
            new_kv["b_k"].reshape(nb, 1, s, 8, HEAD_DIM), new_kv["b_v"].reshape(nb, 1, s, 8, HEAD_DIM),
            new_kv["c_k"].reshape(nb, 1, s, 8, 2, HEAD_DIM), new_kv["c_v"].reshape(nb, 1, s, 8, 2 * HEAD_DIM))
```

```python
import functools
import math

import jax
import jax.numpy as jnp
import numpy as np
from jax import lax
from jax.experimental import pallas as pl
from jax.experimental.pallas import tpu as pltpu

f32 = jnp.float32
bf16 = jnp.bfloat16
i32 = jnp.int32

D_MODEL = 1024
N_PROMPT_BATCH = 16
PROMPT_SEQ = 256
DEPTH = 2
N_SAMPLE_BATCH = 2
SAMPLE_SEQ = 2048
PAST_LEN = 512
GRID_W = 64
HEAD_DIM = 64
ROPE_THETA = 10000.0
EPS = 1e-6
A_WINDOW = 128
NA_ROWS = 8
NA_COLS = 16
N_EXPERTS = 64
TOP_K = 6
N_GROUPS = 8
TOPK_GROUPS = 4
EXPERT_DIM = 256
ROUTED_SCALE = 2.5
Q_SCALE = HEAD_DIM ** -0.5

N_PROMPT = N_PROMPT_BATCH * PROMPT_SEQ
N_SAMPLE = N_SAMPLE_BATCH * SAMPLE_SEQ
N_TOK = N_PROMPT + N_SAMPLE

LANES = 128
TM = 256
N_PROMPT_TILES = N_PROMPT // TM
N_TILES = N_TOK // TM
TILES_PER_SAMPLE = SAMPLE_SEQ // TM
QB = 128
CHUNK = 16
SLOTS = -(-(TM * TOP_K + N_EXPERTS * (CHUNK - 1)) // 256) * 256
SLOT_CHUNKS = SLOTS // CHUNK
ROUTE_ROWS = 512
XW = D_MODEL + LANES
GM = 256
_MAX_SORTED = TM * TOP_K * N_TILES + N_TILES * N_EXPERTS * (CHUNK - 1) + N_EXPERTS * (GM - CHUNK)
G_TILES = -(-_MAX_SORTED // GM)
G_CHUNKS = GM // CHUNK
GMM_SLOTS = 4
VMEM_LIMIT = 56 * 1024 * 1024

NEG = -1e30


def _cparams(sem):
    return pltpu.CompilerParams(dimension_semantics=sem, vmem_limit_bytes=VMEM_LIMIT)


def _mod_row(i):
    return jnp.where(i < N_PROMPT_TILES, 0, 1 + (i - N_PROMPT_TILES) // TILES_PER_SAMPLE)


def _x_specs(parts):
    if len(parts) == 1:
        return [pl.BlockSpec((TM, D_MODEL), lambda i, *_: (i, 0))]
    return [pl.BlockSpec((TM, D_MODEL), lambda i, *_: (jnp.minimum(i, N_PROMPT_TILES - 1), 0)),
            pl.BlockSpec((TM, D_MODEL), lambda i, *_: (jnp.maximum(i - N_PROMPT_TILES, 0), 0))]


def _load_x(i, x_refs):
    if len(x_refs) == 1:
        return x_refs[0][...]
    return jnp.where(i < N_PROMPT_TILES, x_refs[0][...], x_refs[1][...])


def _norm_mod(x, g, scale, shift):
    y = x * lax.rsqrt(jnp.mean(x * x, axis=-1, keepdims=True) + EPS)
    return (y * g) * (1.0 + scale) + shift


def _silu(x):
    return x * jax.nn.sigmoid(x)


def _dot(a, b):
    return jnp.dot(a, b, preferred_element_type=f32)


def _dot_nt(a, b):
    return lax.dot_general(a, b, (((1,), (1,)), ((), ())), preferred_element_type=f32)


ADA_COLS = 1536


def _adaln_kernel(cond_ref, w_ref, b_ref, o_ref):
    s = _silu(cond_ref[...]).astype(bf16)
    o_ref[...] = _dot(s, w_ref[...].astype(bf16)) + b_ref[...]


def _adaln(cond8, w_mod, b_mod):
    n6 = 6 * D_MODEL
    return pl.pallas_call(
        _adaln_kernel,
        grid=(DEPTH, n6 // ADA_COLS),
        in_specs=[
            pl.BlockSpec((8, D_MODEL), lambda l, j: (0, 0)),
            pl.BlockSpec((None, D_MODEL, ADA_COLS), lambda l, j: (l, 0, j)),
            pl.BlockSpec((None, 1, ADA_COLS), lambda l, j: (l, 0, j)),
        ],
        out_specs=pl.BlockSpec((None, 8, ADA_COLS), lambda l, j: (l, 0, j)),
        out_shape=jax.ShapeDtypeStruct((DEPTH, 8, n6), f32),
        compiler_params=_cparams(("parallel", "parallel")),
        name="adaln",
    )(cond8, w_mod, b_mod.reshape(DEPTH, 1, n6))


def _rope_block(blk, cos, sin_a, sin_b):
    return blk * cos + pltpu.roll(blk, LANES - 16, 1) * sin_a + pltpu.roll(blk, 16, 1) * sin_b


def _inproj_kernel(*refs, chunks, n_x):
    x_refs, kv_refs = refs[:n_x], refs[n_x + 7:]
    mod_ref, g_ref, w_ref, cos_ref, sa_ref, sb_ref, qkv_ref = refs[n_x:n_x + 7]
    i = pl.program_id(0)
    h = _norm_mod(_load_x(i, x_refs), g_ref[...], mod_ref[1:2, :], mod_ref[0:1, :]).astype(bf16)
    is_prompt = i < N_PROMPT_TILES
    for c0, c1, rope_blocks, kv_out in chunks:
        acc = _dot(h, w_ref[:, c0:c1])
        if rope_blocks:
            @pl.when(jnp.logical_not(is_prompt))
            def _():
                cos, sa, sb = cos_ref[...], sa_ref[...], sb_ref[...]
                for b in range((c1 - c0) // LANES):
                    blk = acc[:, b * LANES:(b + 1) * LANES]
                    if b in rope_blocks:
                        blk = _rope_block(blk, cos, sa, sb)
                    qkv_ref[:, c0 + b * LANES:c0 + (b + 1) * LANES] = blk.astype(bf16)

            @pl.when(is_prompt)
            def _():
                qkv_ref[:, c0:c1] = acc.astype(bf16)
        else:
            qkv_ref[:, c0:c1] = acc.astype(bf16)
        if kv_out:
            @pl.when(is_prompt)
            def _():
                for ridx, a0, a1, o0 in kv_out:
                    kv_refs[ridx][:, o0:o0 + (a1 - a0)] = acc[:, a0:a1]


def _inproj(x_parts, mod_l, g, w, rope_tabs, chunks, kv_widths):
    n = w.shape[1]
    cos, sa, sb = rope_tabs

    def rope_idx(i):
        return (jnp.where(i < N_PROMPT_TILES, 0, (i - N_PROMPT_TILES) % TILES_PER_SAMPLE), 0)

    def kv_idx(i):
        return (jnp.minimum(i, N_PROMPT_TILES - 1), 0)

    return pl.pallas_call(
        functools.partial(_inproj_kernel, chunks=chunks, n_x=len(x_parts)),
        grid=(N_TILES,),
        in_specs=_x_specs(x_parts) + [
            pl.BlockSpec((None, 6, D_MODEL), lambda i: (_mod_row(i), 0, 0)),
            pl.BlockSpec((1, D_MODEL), lambda i: (0, 0)),
            pl.BlockSpec((D_MODEL, n), lambda i: (0, 0)),
            pl.BlockSpec((TM, LANES), rope_idx),
            pl.BlockSpec((TM, LANES), rope_idx),
            pl.BlockSpec((TM, LANES), rope_idx),
        ],
        out_specs=[pl.BlockSpec((TM, n), lambda i: (i, 0))]
        + [pl.BlockSpec((TM, wd), kv_idx) for wd in kv_widths],
        out_shape=[jax.ShapeDtypeStruct((N_TOK, n), bf16)]
        + [jax.ShapeDtypeStruct((N_PROMPT, wd), f32) for wd in kv_widths],
        compiler_params=_cparams(("arbitrary",)),
        name="inproj",
    )(*x_parts, mod_l, g, w, cos, sa, sb)


def _rope_tables():
    nq = HEAD_DIM // 4
    t = jnp.arange(SAMPLE_SEQ)
    inv = jnp.power(ROPE_THETA, -jnp.arange(nq, dtype=f32) / nq)
    ang_r = (t // GRID_W).astype(f32)[:, None] * inv
    ang_c = (t % GRID_W).astype(f32)[:, None] * inv
    zero = jnp.zeros_like(ang_r)

    def head(fr, fc):
        return jnp.concatenate([fr[0], fr[1], fc[0], fc[1]], axis=-1)

    cos = head((jnp.cos(ang_r), jnp.cos(ang_r)), (jnp.cos(ang_c), jnp.cos(ang_c)))
    sin_a = head((-jnp.sin(ang_r), zero), (-jnp.sin(ang_c), zero))
    sin_b = head((zero, jnp.sin(ang_r)), (zero, jnp.sin(ang_c)))
    two = lambda a: jnp.concatenate([a, a], axis=-1)
    return two(cos), two(sin_a), two(sin_b)


def _lane_lo(shape):
    return lax.broadcasted_iota(i32, shape, len(shape) - 1) < HEAD_DIM


def _half(q, lo_mask, half):
    keep = lo_mask if half == 0 else jnp.logical_not(lo_mask)
    return jnp.where(keep, q, jnp.zeros_like(q)) * Q_SCALE


def _swap_halves(x):
    return pltpu.roll(x.astype(f32), HEAD_DIM, 1).astype(x.dtype)


def _stack_halves(q, lo_mask):
    return jnp.concatenate([_half(q, lo_mask, 0), _half(q, lo_mask, 1)], axis=0)


def _with_ones(v):
    return jnp.concatenate([v, jnp.ones_like(v)], axis=1)


def _attend(q_rows, n_heads, key_blocks, vx_blocks, fix_scores=None, sinks=None):
    r = q_rows.shape[0] // n_heads
    scores = [_dot_nt(q_rows, k) for k in key_blocks]
    exps = [[] for _ in key_blocks]
    maxes = []
    for h in range(n_heads):
        blocks = [s[h * r:(h + 1) * r] for s in scores]
        if fix_scores is not None:
            blocks = [fix_scores(h, i, s) for i, s in enumerate(blocks)]
        m = functools.reduce(jnp.maximum, [jnp.max(s, axis=-1, keepdims=True) for s in blocks])
        if sinks is not None:
            m = jnp.maximum(m, sinks[h])
        maxes.append(m)
        for i, s in enumerate(blocks):
            exps[i].append(jnp.exp((s - m).astype(bf16)))
    out = functools.reduce(lambda a, b: a + b,
                           [_dot(e[0] if n_heads == 1 else jnp.concatenate(e, axis=0), vx)
                            for e, vx in zip(exps, vx_blocks)])
    outs = []
    for h in range(n_heads):
        den = out[h * r:(h + 1) * r, LANES:]
        if sinks is not None:
            den = den + jnp.exp(sinks[h] - maxes[h])
        outs.append(out[h * r:(h + 1) * r, :LANES] * (1.0 / den))
    return outs


def _gqa_rows(q_blocks, group, lo_mask):
    parts = []
    for q in q_blocks:
        for half in range(2):
            qh = _half(q, lo_mask, half)
            parts.append(qh if half == group else _swap_halves(qh))
    return jnp.concatenate(parts, axis=0)


def _gqa_merge(outs, group, lo_mask):
    fixed = [o if idx % 2 == group else pltpu.roll(o, HEAD_DIM, 1) for idx, o in enumerate(outs)]
    return [jnp.where(lo_mask, fixed[2 * p], fixed[2 * p + 1]) for p in range(len(outs) // 2)]


L0_QA, L0_QB, L0_KB, L0_VB, L0_KA, L0_VA, L0_N = 0, 512, 1024, 1536, 2048, 2176, 2304


def _ctx0_kernel(sink_ref, qkv_ref, o_ref):
    lo = _lane_lo((1, LANES))
    blk = lambda base, j: qkv_ref[:, base + j * LANES:base + (j + 1) * LANES]
    k_a = blk(L0_KA, 0)
    vx_a = _with_ones(blk(L0_VA, 0))
    for g in range(2):
        q_rows = _gqa_rows([blk(L0_QA, 2 * g), blk(L0_QA, 2 * g + 1)], g, lo)
        outs = _attend(q_rows, 4, [k_a], [vx_a], sinks=[sink_ref[4 * g + idx] for idx in range(4)])
        for p, o in enumerate(_gqa_merge(outs, g, lo)):
            j = 2 * g + p
            o_ref[:, j * LANES:(j + 1) * LANES] = o.astype(bf16)
    for j in range(4):
        outs = _attend(_stack_halves(blk(L0_QB, j), lo), 2, [blk(L0_KB, j)], [_with_ones(blk(L0_VB, j))])
        o_ref[:, 512 + j * LANES:512 + (j + 1) * LANES] = jnp.where(lo, outs[0], outs[1]).astype(bf16)


def _ctx0(sink, qkv):
    return pl.pallas_call(
        _ctx0_kernel,
        grid=(N_PROMPT_BATCH,),
        in_specs=[
            pl.BlockSpec(memory_space=pltpu.SMEM),
            pl.BlockSpec((PROMPT_SEQ, L0_N), lambda b: (b, 0)),
        ],
        out_specs=pl.BlockSpec((PROMPT_SEQ, D_MODEL), lambda b: (b, 0)),
        out_shape=jax.ShapeDtypeStruct((N_PROMPT, D_MODEL), bf16),
        compiler_params=_cparams(("parallel",)),
        name="ctx0",
    )(sink, qkv)


WIN_KEYS = 3 * QB
NA_KEY_ROWS = 10
NA_KEYS = NA_KEY_ROWS * GRID_W
N_QB = SAMPLE_SEQ // QB
N_NA_PATTERNS = 5
_PROMPT_QBLOCKS = N_PROMPT // QB


def _na_pattern(n):
    return jnp.where(n < 2, n, jnp.where(n > N_QB - 3, n - (N_QB - 5), 2))


def _lat0_kernel(sink_ref, q_ref, kvb_ref, kva_ref, cak_ref, cav_ref, cbk_ref, cbv_ref, nab_ref, o_ref):
    n = pl.program_id(1)
    lo = _lane_lo((1, LANES))
    kstart = pl.multiple_of(jnp.clip((n - 1) * QB, 0, SAMPLE_SEQ - WIN_KEYS), QB)
    k_a = kva_ref[pl.ds(kstart, WIN_KEYS), 0:LANES]
    v_a = kva_ref[pl.ds(kstart, WIN_KEYS), LANES:2 * LANES]
    c_k = cak_ref[...].astype(bf16)
    keys_a = [c_k, k_a]
    vx_a = [_with_ones(cav_ref[...].astype(bf16)), _with_ones(v_a)]
    qpos = n * QB + lax.broadcasted_iota(i32, (QB, WIN_KEYS), 0)
    kpos = kstart + lax.broadcasted_iota(i32, (QB, WIN_KEYS), 1)
    in_window = jnp.abs(qpos - kpos) <= A_WINDOW
    mask_window = lambda h, i, s: jnp.where(in_window, s, NEG) if i == 1 else s
    for g in range(2):
        q_rows = _gqa_rows([q_ref[:, L0_QA + j * LANES:L0_QA + (j + 1) * LANES] for j in (2 * g, 2 * g + 1)], g, lo)
        outs = _attend(q_rows, 4, keys_a, vx_a, fix_scores=mask_window,
                       sinks=[sink_ref[4 * g + idx] for idx in range(4)])
        for p, o in enumerate(_gqa_merge(outs, g, lo)):
            j = 2 * g + p
            o_ref[:, j * LANES:(j + 1) * LANES] = o.astype(bf16)
    krow = jnp.clip(2 * n - NA_ROWS // 2, 0, SAMPLE_SEQ // GRID_W - NA_KEY_ROWS)
    ktok = pl.multiple_of(krow * GRID_W, QB)
    for j in range(4):
        q_b = q_ref[:, L0_QB + j * LANES:L0_QB + (j + 1) * LANES]
        k_b = kvb_ref[pl.ds(ktok, NA_KEYS), j * LANES:(j + 1) * LANES]
        v_b = kvb_ref[pl.ds(ktok, NA_KEYS), 512 + j * LANES:512 + (j + 1) * LANES]
        cb_k = cbk_ref[:, j * LANES:(j + 1) * LANES].astype(bf16)
        cb_v = cbv_ref[:, j * LANES:(j + 1) * LANES].astype(bf16)
        add_bias = lambda h, i, s, j=j: s + nab_ref[2 * j + h] if i == 1 else s
        outs = _attend(_stack_halves(q_b, lo), 2, [cb_k, k_b], [_with_ones(cb_v), _with_ones(v_b)],
                       fix_scores=add_bias)
        o_ref[:, 512 + j * LANES:512 + (j + 1) * LANES] = jnp.where(lo, outs[0], outs[1]).astype(bf16)


def _lat0(sink, qkv, cak, cav, cbk, cbv, nab):
    sb = N_PROMPT // SAMPLE_SEQ
    return pl.pallas_call(
        _lat0_kernel,
        grid=(N_SAMPLE_BATCH, N_QB),
        in_specs=[
            pl.BlockSpec(memory_space=pltpu.SMEM),
            pl.BlockSpec((QB, 1024), lambda b, n: (_PROMPT_QBLOCKS + b * N_QB + n, 0)),
            pl.BlockSpec((SAMPLE_SEQ, 1024), lambda b, n: (sb + b, 1)),
            pl.BlockSpec((SAMPLE_SEQ, 256), lambda b, n: (sb + b, L0_KA // 256)),
            pl.BlockSpec((None, PAST_LEN, LANES), lambda b, n: (b, 0, 0)),
            pl.BlockSpec((None, PAST_LEN, LANES), lambda b, n: (b, 0, 0)),
            pl.BlockSpec((None, PAST_LEN, 512), lambda b, n: (b, 0, 0)),
            pl.BlockSpec((None, PAST_LEN, 512), lambda b, n: (b, 0, 0)),
            pl.BlockSpec((None, 8, QB, NA_KEYS), lambda b, n: (_na_pattern(n), 0, 0, 0)),
        ],
        out_specs=pl.BlockSpec((QB, D_MODEL), lambda b, n: (b * N_QB + n, 0)),
        out_shape=jax.ShapeDtypeStruct((N_SAMPLE, D_MODEL), bf16),
        compiler_params=_cparams(("parallel", "arbitrary")),
        name="lat0",
    )(sink, qkv, qkv, qkv, cak, cav, cbk, cbv, nab)


def _na_bias_table(rel_bias):
    rows = SAMPLE_SEQ // GRID_W
    n_dr, n_dc = 2 * NA_ROWS - 1, 2 * NA_COLS - 1
    c = np.arange(GRID_W)[:, None]
    kc = np.arange(GRID_W)[None, :]
    cs = np.clip(c - NA_COLS // 2, 0, GRID_W - NA_COLS)
    col_ok = (kc >= cs) & (kc < cs + NA_COLS)
    col_hot = ((kc - c + NA_COLS - 1)[None] == np.arange(n_dc)[:, None, None]) & col_ok[None]
    row_hot = np.zeros((N_NA_PATTERNS, 2, NA_KEY_ROWS, n_dr), np.float32)
    for p, n in enumerate((0, 1, 2, N_QB - 2, N_QB - 1)):
        k0 = int(np.clip(2 * n - NA_ROWS // 2, 0, rows - NA_KEY_ROWS))
        for rq in range(2):
            r = 2 * n + rq
            rs = int(np.clip(r - NA_ROWS // 2, 0, rows - NA_ROWS))
            for kl in range(NA_KEY_ROWS):
                if rs <= k0 + kl < rs + NA_ROWS:
                    row_hot[p, rq, kl, k0 + kl - r + NA_ROWS - 1] = 1.0
    ok = (row_hot.sum(-1) > 0)[:, None, :, None, :, None] & col_ok[None, None, None, :, None, :]
    hp = lax.Precision.HIGHEST
    toeplitz = jnp.einsum("hdx,xck->hdck", rel_bias.astype(f32), col_hot.astype(np.float32), precision=hp)
    tab = jnp.einsum("prkd,hdcx->phrckx", row_hot, toeplitz, precision=hp)
    tab = tab + np.where(ok, 0.0, NEG).astype(np.float32)
    return tab.reshape(N_NA_PATTERNS, 8, QB, NA_KEYS)


def _diff_lambda(lam_ref, lam_init):
    lv = lam_ref[...]
    s1 = jnp.sum(lv[0:1, :] * lv[1:2, :], axis=-1, keepdims=True)
    s2 = jnp.sum(lv[2:3, :] * lv[3:4, :], axis=-1, keepdims=True)
    return jnp.exp(s1) - jnp.exp(s2) + lam_init


def _diff_head(q, key_blocks, value_blocks, lam, subln, lo, lam_init):
    o1, o2 = _attend(_stack_halves(q, lo), 2, key_blocks, [_with_ones(v) for v in value_blocks])
    o = o1 - lam * o2
    o = o * lax.rsqrt(jnp.mean(o * o, axis=-1, keepdims=True) + EPS)
    return (o * subln) * (1.0 - lam_init)


def _ctx1_kernel(lam_ref, subln_ref, qkv_ref, o_ref, *, lam_init):
    lo = _lane_lo((1, LANES))
    lam = _diff_lambda(lam_ref, lam_init)
    subln = subln_ref[...]
    for h in range(8):
        q = qkv_ref[:, h * LANES:(h + 1) * LANES]
        k = qkv_ref[:, 1024 + h * LANES:1024 + (h + 1) * LANES]
        v = qkv_ref[:, 2048 + h * LANES:2048 + (h + 1) * LANES]
        o_ref[:, h * LANES:(h + 1) * LANES] = _diff_head(q, [k], [v], lam, subln, lo, lam_init).astype(bf16)


def _ctx1(lamv, subln, qkv, lam_init):
    return pl.pallas_call(
        functools.partial(_ctx1_kernel, lam_init=lam_init),
        grid=(N_PROMPT_BATCH,),
        in_specs=[
            pl.BlockSpec((8, HEAD_DIM), lambda b: (0, 0)),
            pl.BlockSpec((1, LANES), lambda b: (0, 0)),
            pl.BlockSpec((PROMPT_SEQ, 3 * D_MODEL), lambda b: (b, 0)),
        ],
        out_specs=pl.BlockSpec((PROMPT_SEQ, D_MODEL), lambda b: (b, 0)),
        out_shape=jax.ShapeDtypeStruct((N_PROMPT, D_MODEL), bf16),
        compiler_params=_cparams(("parallel",)),
        name="ctx1",
    )(lamv, subln, qkv)


def _lat1_kernel(lam_ref, subln_ref, q_ref, k_ref, v_ref, ck_ref, cv_ref, o_ref, *, lam_init):
    lo = _lane_lo((1, LANES))
    lam = _diff_lambda(lam_ref, lam_init)
    subln = subln_ref[...]
    for h in range(8):
        sl = slice(h * LANES, (h + 1) * LANES)
        o_ref[:, sl] = _diff_head(q_ref[:, sl], [ck_ref[:, sl].astype(bf16), k_ref[:, sl]],
                                  [cv_ref[:, sl].astype(bf16), v_ref[:, sl]],
                                  lam, subln, lo, lam_init).astype(bf16)


def _lat1(lamv, subln, qkv, ck, cv, lam_init):
    sb = N_PROMPT // SAMPLE_SEQ
    nq = SAMPLE_SEQ // TM
    return pl.pallas_call(
        functools.partial(_lat1_kernel, lam_init=lam_init),
        grid=(N_SAMPLE_BATCH, nq),
        in_specs=[
            pl.BlockSpec((8, HEAD_DIM), lambda b, n: (0, 0)),
            pl.BlockSpec((1, LANES), lambda b, n: (0, 0)),
            pl.BlockSpec((TM, D_MODEL), lambda b, n: (N_PROMPT_TILES + b * nq + n, 0)),
            pl.BlockSpec((SAMPLE_SEQ, D_MODEL), lambda b, n: (sb + b, 1)),
            pl.BlockSpec((SAMPLE_SEQ, D_MODEL), lambda b, n: (sb + b, 2)),
            pl.BlockSpec((None, PAST_LEN, D_MODEL), lambda b, n: (b, 0, 0)),
            pl.BlockSpec((None, PAST_LEN, D_MODEL), lambda b, n: (b, 0, 0)),
        ],
        out_specs=pl.BlockSpec((TM, D_MODEL), lambda b, n: (b * nq + n, 0)),
        out_shape=jax.ShapeDtypeStruct((N_SAMPLE, D_MODEL), bf16),
        compiler_params=_cparams(("parallel", "arbitrary")),
        name="lat1",
    )(lamv, subln, qkv, qkv, qkv, ck, cv)


def _outproj_kernel(*refs, n_x):
    x_refs = refs[:n_x]
    op_ref, os_ref, mod_ref, w_ref, o_ref = refs[n_x:]
    i = pl.program_id(0)
    attn = jnp.where(i < N_PROMPT_TILES, op_ref[...], os_ref[...])
    o_ref[...] = _load_x(i, x_refs) + mod_ref[2:3, :] * _dot(attn, w_ref[...])


def _outproj(x_parts, o_prompt, o_sample, mod_l, w):
    return pl.pallas_call(
        functools.partial(_outproj_kernel, n_x=len(x_parts)),
        grid=(N_TILES,),
        in_specs=_x_specs(x_parts) + [
            pl.BlockSpec((TM, D_MODEL), lambda i: (jnp.minimum(i, N_PROMPT_TILES - 1), 0)),
            pl.BlockSpec((TM, D_MODEL), lambda i: (jnp.maximum(i - N_PROMPT_TILES, 0), 0)),
            pl.BlockSpec((None, 6, D_MODEL), lambda i: (_mod_row(i), 0, 0)),
            pl.BlockSpec((D_MODEL, D_MODEL), lambda i: (0, 0)),
        ],
        out_specs=pl.BlockSpec((TM, D_MODEL), lambda i: (i, 0)),
        out_shape=jax.ShapeDtypeStruct((N_TOK, D_MODEL), f32),
        compiler_params=_cparams(("parallel",)),
        name="outproj",
    )(*x_parts, o_prompt, o_sample, mod_l, w)


def _split_bf16(a):
    hi = a.astype(bf16)
    return hi, (a - hi.astype(f32)).astype(bf16)


def _route_kernel(x_ref, mod_ref, g_ref, rwt_ref, rb_ref, xloc_ref, p_ref, len_ref):
    ng, ge = N_GROUPS, N_EXPERTS // N_GROUPS
    h = _norm_mod(x_ref[...], g_ref[...], mod_ref[4:5, :], mod_ref[3:4, :])
    h_hi, h_lo = _split_bf16(h)
    w_hi, w_lo = _split_bf16(rwt_ref[...])
    logits = _dot_nt(w_hi, h_hi) + (_dot_nt(w_hi, h_lo) + _dot_nt(w_lo, h_hi))
    scores = jax.nn.sigmoid(logits)
    biased = scores + rb_ref[...]
    s3 = scores.reshape(ng, ge, TM)
    b3 = biased.reshape(ng, ge, TM)
    in_group = lax.broadcasted_iota(i32, (ng, ge, TM), 1).astype(f32)
    group_id = lax.broadcasted_iota(i32, (ng, 1, TM), 0).astype(f32)
    expert_id = lax.broadcasted_iota(i32, (ng, ge, TM), 0).astype(f32) * ge + in_group

    def max01(a):
        return jnp.max(jnp.max(a, axis=0, keepdims=True), axis=1, keepdims=True)

    def min01(a):
        return jnp.min(jnp.min(a, axis=0, keepdims=True), axis=1, keepdims=True)

    def sum01(a):
        return jnp.sum(jnp.sum(a, axis=0, keepdims=True), axis=1, keepdims=True)

    m1 = jnp.max(b3, axis=1, keepdims=True)
    first = jnp.min(jnp.where(b3 == m1, in_group, ge), axis=1, keepdims=True)
    m2 = jnp.max(jnp.where(in_group == first, -jnp.inf, b3), axis=1, keepdims=True)
    gscore = m1 + m2
    gsel = jnp.zeros((ng, 1, TM), f32)
    for _ in range(TOPK_GROUPS):
        gm = jnp.max(gscore, axis=0, keepdims=True)
        gi = jnp.min(jnp.where(gscore == gm, group_id, ng), axis=0, keepdims=True)
        hit = group_id == gi
        gsel = jnp.where(hit, 1.0, gsel)
        gscore = jnp.where(hit, -jnp.inf, gscore)
    cand = jnp.where(jnp.broadcast_to(gsel, (ng, ge, TM)) > 0.0, b3, -jnp.inf)
    top_e, top_w = [], []
    for _ in range(TOP_K):
        em = max01(cand)
        ei = min01(jnp.where(cand == em, expert_id, N_EXPERTS))
        hit = expert_id == ei
        top_e.append(ei)
        top_w.append(sum01(jnp.where(hit, s3, 0.0)))
        cand = jnp.where(hit, -jnp.inf, cand)
    wsum = functools.reduce(lambda a, b: a + b, top_w)
    gates3 = jnp.zeros((ng, ge, TM), f32)
    sel3 = jnp.zeros((ng, ge, TM), f32)
    for ei, w in zip(top_e, top_w):
        hit = expert_id == ei
        gates3 = jnp.where(hit, w / wsum * ROUTED_SCALE, gates3)
        sel3 = jnp.where(hit, 1.0, sel3)
    gates = gates3.reshape(N_EXPERTS, TM)
    sel = sel3.reshape(N_EXPERTS, TM)

    cnt = jnp.sum(sel, axis=1, keepdims=True)
    run_len = jnp.ceil(cnt * (1.0 / CHUNK)) * CHUNK
    r_i = lax.broadcasted_iota(i32, (N_EXPERTS, N_EXPERTS), 0)
    c_i = lax.broadcasted_iota(i32, (N_EXPERTS, N_EXPERTS), 1)
    lower = jnp.where(c_i < r_i, 1.0, 0.0).astype(bf16)
    run_off = _dot(lower, jnp.broadcast_to(run_len, (N_EXPERTS, LANES)).astype(bf16))[:, 0:1]
    t_r = lax.broadcasted_iota(i32, (TM, TM), 0)
    t_c = lax.broadcasted_iota(i32, (TM, TM), 1)
    before = jnp.where(t_r < t_c, 1.0, 0.0).astype(bf16)
    rank = _dot(sel.astype(bf16), before)
    slot3 = (run_off + rank).reshape(ng, ge, TM)
    slots = [sum01(jnp.where(expert_id == ei, slot3, 0.0)).reshape(1, TM).astype(i32) for ei in top_e]
    len_ref[...] = jnp.broadcast_to(run_len, (N_EXPERTS, LANES)).astype(i32)

    g_parts = jnp.concatenate(_split_bf16(gates), axis=0)
    rows = ROUTE_ROWS

    def body(c, carry):
        base = pl.multiple_of(c * rows, rows)
        row_id = base + lax.broadcasted_iota(i32, (rows, TM), 0)
        p = jnp.zeros((rows, TM), f32)
        for k in range(TOP_K):
            p = jnp.where(row_id == slots[k], 1.0, p)
        p = p.astype(bf16)
        p_ref[pl.ds(base, rows), :] = p
        xloc_ref[pl.ds(base, rows), 0:D_MODEL] = _dot(p, h_hi).astype(bf16)
        xloc_ref[pl.ds(base, rows), D_MODEL:XW] = _dot_nt(p, g_parts).astype(bf16)
        return carry

    def zero_body(c, carry):
        base = pl.multiple_of(c * rows, rows)
        p_ref[pl.ds(base, rows), :] = jnp.zeros((rows, TM), bf16)
        xloc_ref[pl.ds(base, rows), :] = jnp.zeros((rows, XW), bf16)
        return carry

    n_used = (jnp.sum(run_len).astype(i32) + (rows - 1)) // rows
    lax.fori_loop(0, n_used, body, 0)
    lax.fori_loop(n_used, SLOTS // rows, zero_body, 0)


def _route(x, mod_l, g, rwt, rb):
    return pl.pallas_call(
        _route_kernel,
        grid=(N_TILES,),
        in_specs=[
            pl.BlockSpec((TM, D_MODEL), lambda i: (i, 0)),
            pl.BlockSpec((None, 6, D_MODEL), lambda i: (_mod_row(i), 0, 0)),
            pl.BlockSpec((1, D_MODEL), lambda i: (0, 0)),
            pl.BlockSpec((N_EXPERTS, D_MODEL), lambda i: (0, 0)),
            pl.BlockSpec((N_EXPERTS, 1), lambda i: (0, 0)),
        ],
        out_specs=[
            pl.BlockSpec((SLOTS, XW), lambda i: (i, 0)),
            pl.BlockSpec((SLOTS, TM), lambda i: (i, 0)),
            pl.BlockSpec((None, N_EXPERTS, LANES), lambda i: (i, 0, 0)),
        ],
        out_shape=[
            jax.ShapeDtypeStruct((N_TILES * SLOTS, XW), bf16),
            jax.ShapeDtypeStruct((N_TILES * SLOTS, TM), bf16),
            jax.ShapeDtypeStruct((N_TILES, N_EXPERTS, LANES), i32),
        ],
        compiler_params=_cparams(("parallel",)),
        name="route",
    )(x, mod_l, g, rwt, rb)


def _moe_plan(run_len):
    nt, ne = run_len.shape

    def excl_cumsum(a):
        n = a.shape[-1]
        earlier = np.arange(n)[None, :] < np.arange(n)[:, None]
        return jnp.sum(jnp.where(earlier, a[..., None, :], 0), axis=-1)

    def first_diff(a):
        return a - jnp.concatenate([jnp.zeros_like(a[..., :1]), a[..., :-1]], axis=-1)

    off_loc = excl_cumsum(run_len)
    before = excl_cumsum(run_len.T).T
    n_e = jnp.sum(run_len, axis=0)
    n_pad = -(-n_e // GM) * GM
    g_start = excl_cumsum(n_pad)
    total = jnp.sum(n_pad)
    run_dst = g_start[None, :] + before
    run_src = jnp.arange(nt, dtype=i32)[:, None] * SLOTS + off_loc
    dst_f = run_dst.T.reshape(-1)
    shift_f = first_diff((run_src - run_dst).T.reshape(-1))
    rows = jnp.arange(G_TILES * G_CHUNKS, dtype=i32) * CHUNK
    shift = jnp.sum(jnp.where(dst_f[None, :] <= rows[:, None], shift_f[None, :], 0), axis=1)
    in_run = jnp.any((g_start[None, :] <= rows[:, None]) & (rows[:, None] < (g_start + n_e)[None, :]), axis=1)
    chunk_src = jnp.where(in_run, rows + shift, 0).astype(i32)
    loc_rows = jnp.arange(SLOT_CHUNKS, dtype=i32) * CHUNK
    shift_l = first_diff(run_dst - off_loc)
    shift = jnp.sum(jnp.where(off_loc[:, None, :] <= loc_rows[None, :, None], shift_l[:, None, :], 0), axis=2)
    used = jnp.sum(run_len, axis=1)
    chunk_map = jnp.where(loc_rows[None, :] < used[:, None], (loc_rows[None, :] + shift) // CHUNK, 0).astype(i32)
    return (g_start // GM).astype(i32), (n_pad // GM).astype(i32), chunk_src, chunk_map.reshape(-1)


def _gmm_in_copy(xloc_hbm, xbuf, sem, src_row, slot, c):
    return pltpu.make_async_copy(xloc_hbm.at[pl.ds(src_row, CHUNK)],
                                 xbuf.at[slot, pl.ds(c * CHUNK, CHUNK)], sem.at[slot])


def _gmm_out_copy(ybuf, y_hbm, sem, tile, slot):
    return pltpu.make_async_copy(ybuf.at[slot], y_hbm.at[pl.ds(pl.multiple_of(tile * GM, GM), GM)], sem.at[slot])


def _gmm_kernel(t0_ref, nt_ref, cs_ref, xloc_hbm, wg_ref, wu_ref, wd_ref, y_hbm,
                xbuf, ybuf, zbuf, wg_b, wu_b, wd_b, in_sem, out_sem, zsem):
    e = pl.program_id(0)
    last = pl.num_programs(0) - 1
    n_tiles = nt_ref[e]
    first_tile = t0_ref[e]
    total_tiles = t0_ref[last] + nt_ref[last]

    def start_in(tile):
        for c in range(G_CHUNKS):
            src = pl.multiple_of(cs_ref[tile * G_CHUNKS + c], CHUNK)
            _gmm_in_copy(xloc_hbm, xbuf, in_sem, src, tile % GMM_SLOTS, c).start()

    @pl.when(e == 0)
    def _():
        for tile in range(GMM_SLOTS - 1):
            start_in(tile)
        zbuf[...] = jnp.zeros(zbuf.shape, zbuf.dtype)

    def tail_copies(fn):
        for j in range(_GMM_TAIL_PER_STEP):
            tile = total_tiles + e + j * N_EXPERTS

            @pl.when(tile < G_TILES)
            def _():
                fn(pltpu.make_async_copy(zbuf, y_hbm.at[pl.ds(pl.multiple_of(tile * GM, GM), GM)], zsem.at[0]))

    tail_copies(lambda cp: cp.start())

    @pl.when(n_tiles > 0)
    def _():
        wg_b[...] = wg_ref[...].astype(bf16)
        wu_b[...] = wu_ref[...].astype(bf16)
        wd_b[...] = wd_ref[...].astype(bf16)

    lane = lax.broadcasted_iota(i32, (1, LANES), 1)
    gate_lanes = jnp.logical_or(lane == e, lane == e + N_EXPERTS)

    def body(t, carry):
        tile = first_tile + t
        slot = tile % GMM_SLOTS

        @pl.when(tile + (GMM_SLOTS - 1) < total_tiles)
        def _():
            start_in(tile + (GMM_SLOTS - 1))

        for c in range(G_CHUNKS):
            _gmm_in_copy(xloc_hbm, xbuf, in_sem, 0, slot, c).wait()

        @pl.when(tile >= GMM_SLOTS)
        def _():
            _gmm_out_copy(ybuf, y_hbm, out_sem, tile - GMM_SLOTS, slot).wait()

        x = xbuf[slot]
        gate = jnp.sum(jnp.where(gate_lanes, x[:, D_MODEL:XW].astype(f32), 0.0), axis=1, keepdims=True)
        xa = x[:, 0:D_MODEL]
        act = _silu(_dot(xa, wg_b[...])) * _dot(xa, wu_b[...]) * gate
        ybuf[slot] = _dot(act.astype(bf16), wd_b[...]).astype(bf16)
        _gmm_out_copy(ybuf, y_hbm, out_sem, tile, slot).start()
        return carry

    lax.fori_loop(0, n_tiles, body, 0)
    tail_copies(lambda cp: cp.wait())

    @pl.when(e == last)
    def _():
        for back in range(1, GMM_SLOTS + 1):
            tile = total_tiles - back
            _gmm_out_copy(ybuf, y_hbm, out_sem, tile, tile % GMM_SLOTS).wait()


_GMM_TAIL_PER_STEP = -(-(G_TILES - TM * TOP_K * N_TILES // GM) // N_EXPERTS)


def _gmm(tile_start, n_tiles, chunk_src, xloc, wg, wu, wd, layer):
    grid_spec = pltpu.PrefetchScalarGridSpec(
        num_scalar_prefetch=3,
        grid=(N_EXPERTS,),
        in_specs=[
            pl.BlockSpec(memory_space=pl.ANY),
            pl.BlockSpec((None, None, D_MODEL, EXPERT_DIM), lambda e, t0, nt, cs: (layer, e, 0, 0)),
            pl.BlockSpec((None, None, D_MODEL, EXPERT_DIM), lambda e, t0, nt, cs: (layer, e, 0, 0)),
            pl.BlockSpec((None, None, EXPERT_DIM, D_MODEL), lambda e, t0, nt, cs: (layer, e, 0, 0)),
        ],
        out_specs=pl.BlockSpec(memory_space=pl.ANY),
        scratch_shapes=[pltpu.VMEM((GMM_SLOTS, GM, XW), bf16), pltpu.VMEM((GMM_SLOTS, GM, D_MODEL), bf16),
                        pltpu.VMEM((GM, D_MODEL), bf16),
                        pltpu.VMEM((D_MODEL, EXPERT_DIM), bf16), pltpu.VMEM((D_MODEL, EXPERT_DIM), bf16),
                        pltpu.VMEM((EXPERT_DIM, D_MODEL), bf16),
                        pltpu.SemaphoreType.DMA((GMM_SLOTS,)), pltpu.SemaphoreType.DMA((GMM_SLOTS,)),
                        pltpu.SemaphoreType.DMA((1,))],
    )
    return pl.pallas_call(
        _gmm_kernel,
        grid_spec=grid_spec,
        out_shape=jax.ShapeDtypeStruct((G_TILES * GM, D_MODEL), bf16),
        compiler_params=_cparams(("arbitrary",)),
        name="gmm",
    )(tile_start, n_tiles, chunk_src, xloc, wg, wu, wd)


def _combine_copy(y_hbm, ybuf, sem, sorted_chunk, slot, c):
    return pltpu.make_async_copy(y_hbm.at[pl.ds(pl.multiple_of(sorted_chunk * CHUNK, CHUNK), CHUNK)],
                                 ybuf.at[slot, pl.ds(pl.multiple_of(c * CHUNK, CHUNK), CHUNK)], sem.at[slot])


def _combine_kernel(cm_ref, y_hbm, p_ref, x_ref, mod_ref, g_ref, sg_ref, su_ref, sd_ref, o_ref, ybuf, sem):
    i = pl.program_id(0)
    n = pl.num_programs(0)
    slot = i % 2

    def start(tile, s):
        for c in range(SLOT_CHUNKS):
            _combine_copy(y_hbm, ybuf, sem, cm_ref[tile * SLOT_CHUNKS + c], s, c).start()

    def wait(s):
        for c in range(SLOT_CHUNKS):
            _combine_copy(y_hbm, ybuf, sem, 0, s, c).wait()

    @pl.when(i == 0)
    def _():
        start(0, 0)

    wait(slot)
    start((i + 1) % n, 1 - slot)

    x = x_ref[...]
    hb = _norm_mod(x, g_ref[...], mod_ref[4:5, :], mod_ref[3:4, :]).astype(bf16)
    shared = _dot((_silu(_dot(hb, sg_ref[...])) * _dot(hb, su_ref[...])).astype(bf16), sd_ref[...])
    routed = lax.dot_general(p_ref[...], ybuf[slot], (((0,), (0,)), ((), ())), preferred_element_type=f32)
    o_ref[...] = x + mod_ref[5:6, :] * (routed + shared)

    @pl.when(i == n - 1)
    def _():
        wait(1 - slot)


def _combine(chunk_map, y, onehot, x, mod_l, g, sg, su, sd):
    shd = sg.shape[1]
    grid_spec = pltpu.PrefetchScalarGridSpec(
        num_scalar_prefetch=1,
        grid=(N_TILES,),
        in_specs=[
            pl.BlockSpec(memory_space=pl.ANY),
            pl.BlockSpec((SLOTS, TM), lambda i, cm: (i, 0)),
            pl.BlockSpec((TM, D_MODEL), lambda i, cm: (i, 0)),
            pl.BlockSpec((None, 6, D_MODEL), lambda i, cm: (_mod_row(i), 0, 0)),
            pl.BlockSpec((1, D_MODEL), lambda i, cm: (0, 0)),
            pl.BlockSpec((D_MODEL, shd), lambda i, cm: (0, 0)),
            pl.BlockSpec((D_MODEL, shd), lambda i, cm: (0, 0)),
            pl.BlockSpec((shd, D_MODEL), lambda i, cm: (0, 0)),
        ],
        out_specs=pl.BlockSpec((TM, D_MODEL), lambda i, cm: (i, 0)),
        scratch_shapes=[pltpu.VMEM((2, SLOTS, D_MODEL), bf16), pltpu.SemaphoreType.DMA((2,))],
    )
    return pl.pallas_call(
        _combine_kernel,
        grid_spec=grid_spec,
        out_shape=jax.ShapeDtypeStruct((N_TOK, D_MODEL), f32),
        compiler_params=_cparams(("arbitrary",)),
        name="combine",
    )(chunk_map, y, onehot, x, mod_l, g, sg, su, sd)


def _moe(x, mod_l, g, rwt, rb, wg, wu, wd, layer, sg, su, sd):
    xloc, onehot, run_len = _route(x, mod_l, g, rwt, rb)
    tile_start, n_tiles, chunk_src, chunk_map = _moe_plan(run_len[:, :, 0])
    y = _gmm(tile_start, n_tiles, chunk_src, xloc, wg, wu, wd, layer)
    return _combine(chunk_map, y, onehot, x, mod_l, g, sg, su, sd)


def _final_kernel(x_ref, g_ref, yp_ref, ys_ref):
    i = pl.program_id(0)
    x = x_ref[...]
    y = (x * lax.rsqrt(jnp.mean(x * x, axis=-1, keepdims=True) + EPS)) * g_ref[...]

    @pl.when(i < N_PROMPT_TILES)
    def _():
        yp_ref[...] = y

    @pl.when(i >= N_PROMPT_TILES)
    def _():
        ys_ref[...] = y


def _final(x, g):
    return pl.pallas_call(
        _final_kernel,
        grid=(N_TILES,),
        in_specs=[
            pl.BlockSpec((TM, D_MODEL), lambda i: (i, 0)),
            pl.BlockSpec((1, D_MODEL), lambda i: (0, 0)),
        ],
        out_specs=[
            pl.BlockSpec((TM, D_MODEL), lambda i: (jnp.minimum(i, N_PROMPT_TILES - 1), 0)),
            pl.BlockSpec((TM, D_MODEL), lambda i: (jnp.maximum(i - N_PROMPT_TILES, 0), 0)),
        ],
        out_shape=[jax.ShapeDtypeStruct((N_PROMPT, D_MODEL), f32), jax.ShapeDtypeStruct((N_SAMPLE, D_MODEL), f32)],
        compiler_params=_cparams(("arbitrary",)),
        name="final_norm",
    )(x, g)


def _permute_w_in_ab(w):
    return jnp.concatenate([w[:, 0:512], w[:, 768:1280], w[:, 1280:1792], w[:, 1792:2304],
                            w[:, 512:640], w[:, 640:768]], axis=1)


_L0_CHUNKS = (
    (0, 512, (0, 1, 2, 3), ()),
    (512, 1024, (), ()),
    (1024, 1536, (), ((2, 0, 512, 0),)),
    (1536, 2048, (), ((3, 0, 512, 0),)),
    (2048, 2304, (0,), ((0, 0, 128, 0), (1, 128, 256, 0))),
)
_L1_CHUNKS = (
    (0, 512, (0, 1, 2, 3), ()),
    (512, 1024, (0, 1, 2, 3), ()),
    (1024, 1536, (0, 1, 2, 3), ((0, 0, 512, 0),)),
    (1536, 2048, (0, 1, 2, 3), ((0, 0, 512, 512),)),
    (2048, 2560, (), ((1, 0, 512, 0),)),
    (2560, 3072, (), ((1, 0, 512, 512),)),
)


def kernel(x_prompt, x_sample, cache_a_k, cache_a_v, cache_b_k, cache_b_v, cache_c_k, cache_c_v, c, c_ctx, w_mod, b_mod, norm_mix, norm_ffn, w_in_ab, w_out_ab, sink_a, rel_bias_b, w_in_c, w_out_c, lam_q1, lam_k1, lam_q2, lam_k2, subln_c, router_w, router_bias, exp_w_gate, exp_w_up, exp_w_down, sh_w_gate, sh_w_up, sh_w_down, final_norm):
    x = (x_prompt.reshape(N_PROMPT, D_MODEL), x_sample.reshape(N_SAMPLE, D_MODEL))
    cond8 = jnp.concatenate([c_ctx[None, :], c, jnp.zeros((8 - 1 - N_SAMPLE_BATCH, D_MODEL), f32)], axis=0)
    mod = _adaln(cond8, w_mod, b_mod).reshape(DEPTH, 8, 6, D_MODEL)
    rope_tabs = _rope_tables()
    new_kv = {}
    for layer in range(DEPTH):
        li = layer // 2
        mod_l = mod[layer]
        g_mix = norm_mix[layer][None, :]
        g_ffn = norm_ffn[layer][None, :]
        if layer % 2 == 0:
            w_in = _permute_w_in_ab(w_in_ab[li]).astype(bf16)
            qkv, ak, av, bk, bv = _inproj(x, mod_l, g_mix, w_in, rope_tabs, _L0_CHUNKS, (128, 128, 512, 512))
            new_kv["a_k"], new_kv["a_v"], new_kv["b_k"], new_kv["b_v"] = ak, av, bk, bv
            o_p = _ctx0(sink_a[li], qkv)
            o_s = _lat0(sink_a[li], qkv,
                        cache_a_k[:, li].reshape(N_SAMPLE_BATCH, PAST_LEN, LANES),
                        cache_a_v[:, li].reshape(N_SAMPLE_BATCH, PAST_LEN, LANES),
                        cache_b_k[:, li].reshape(N_SAMPLE_BATCH, PAST_LEN, 512),
                        cache_b_v[:, li].reshape(N_SAMPLE_BATCH, PAST_LEN, 512),
                        _na_bias_table(rel_bias_b[li]))
            w_out = w_out_ab[li].astype(bf16)
        else:
            lam_init = 0.8 - 0.6 * math.exp(-0.3 * layer)
            qkv, ck, cv = _inproj(x, mod_l, g_mix, w_in_c[li].astype(bf16), rope_tabs, _L1_CHUNKS, (1024, 1024))
            new_kv["c_k"], new_kv["c_v"] = ck, cv
            lamv = jnp.concatenate([lam_q1[li][None], lam_k1[li][None], lam_q2[li][None], lam_k2[li][None],
                                    jnp.zeros((4, HEAD_DIM), f32)], axis=0)
            subln = subln_c[li][None, :]
            o_p = _ctx1(lamv, subln, qkv, lam_init)
            o_s = _lat1(lamv, subln, qkv,
                        cache_c_k[:, li].reshape(N_SAMPLE_BATCH, PAST_LEN, D_MODEL),
                        cache_c_v[:, li].reshape(N_SAMPLE_BATCH, PAST_LEN, D_MODEL), lam_init)
            w_out = w_out_c[li].astype(bf16)
        x = _outproj(x, o_p, o_s, mod_l, w_out)
        x = _moe(x, mod_l, g_ffn, router_w[layer].T, router_bias[layer][:, None],
                 exp_w_gate, exp_w_up, exp_w_down, layer,
                 sh_w_gate[layer].astype(bf16), sh_w_up[layer].astype(bf16), sh_w_down[layer].astype(bf16))
        x = (x,)
    y_prompt, y_sample = _final(x[0], final_norm[None, :])
    nb, s = N_PROMPT_BATCH, PROMPT_SEQ
    return (y_prompt.reshape(nb, s, D_MODEL), y_sample.reshape(N_SAMPLE_BATCH, SAMPLE_SEQ, D_MODEL),
            new_kv["a_k"].reshape(nb, 1, s, 2, HEAD_DIM), new_kv["a_v"].reshape(nb, 1, s, 2, HEAD_DIM),
            new_kv["b_k"].reshape(nb, 1, s, 8, HEAD_DIM), new_kv["b_v"].reshape(nb, 1, s, 8, HEAD_DIM),
            new_kv["c_k"].reshape(nb, 1, s, 8, 2, HEAD_DIM), new_kv["c_v"].reshape(nb, 1, s, 8, 2 * HEAD_DIM))
```

```python
import functools
import math

import jax
import jax.numpy as jnp
import numpy as np
from jax import lax
from jax.experimental import pallas as pl
from jax.experimental.pallas import tpu as pltpu

f32 = jnp.float32
bf16 = jnp.bfloat16
i32 = jnp.int32

D_MODEL = 1024
N_PROMPT_BATCH = 16
PROMPT_SEQ = 256
DEPTH = 2
N_SAMPLE_BATCH = 2
SAMPLE_SEQ = 2048
PAST_LEN = 512
GRID_W = 64
HEAD_DIM = 64
ROPE_THETA = 10000.0
EPS = 1e-6
A_WINDOW = 128
NA_ROWS = 8
NA_COLS = 16
N_EXPERTS = 64
TOP_K = 6
N_GROUPS = 8
TOPK_GROUPS = 4
EXPERT_DIM = 256
ROUTED_SCALE = 2.5
Q_SCALE = HEAD_DIM ** -0.5

N_PROMPT = N_PROMPT_BATCH * PROMPT_SEQ
N_SAMPLE = N_SAMPLE_BATCH * SAMPLE_SEQ
N_TOK = N_PROMPT + N_SAMPLE

LANES = 128
TM = 256
TD = 512
N_PROMPT_TILES = N_PROMPT // TM
N_TILES = N_TOK // TM
QB = 128
CHUNK = 16
SLOTS = -(-(TM * TOP_K + N_EXPERTS * (CHUNK - 1)) // 256) * 256
SLOT_CHUNKS = SLOTS // CHUNK
ROUTE_ROWS = 512
XW = D_MODEL + LANES
GM = 256
_MAX_SORTED = TM * TOP_K * N_TILES + N_TILES * N_EXPERTS * (CHUNK - 1) + N_EXPERTS * (GM - CHUNK)
G_TILES = -(-_MAX_SORTED // GM)
G_CHUNKS = GM // CHUNK
GMM_SLOTS = 4
VMEM_LIMIT = 56 * 1024 * 1024

NEG = -1e30


def _cparams(sem):
    return pltpu.CompilerParams(dimension_semantics=sem, vmem_limit_bytes=VMEM_LIMIT)


def _mod_row(i, tm=TM):
    return jnp.where(i < N_PROMPT // tm, 0, 1 + (i - N_PROMPT // tm) // (SAMPLE_SEQ // tm))


def _prompt_block(i, tm=TM):
    return (jnp.minimum(i, N_PROMPT // tm - 1), 0)


def _sample_block(i, tm=TM):
    return (jnp.maximum(i - N_PROMPT // tm, 0), 0)


def _x_specs(parts, tm=TM):
    if len(parts) == 1:
        return [pl.BlockSpec((tm, D_MODEL), lambda i, *_: (i, 0))]
    return [pl.BlockSpec((tm, D_MODEL), lambda i, *_: _prompt_block(i, tm)),
            pl.BlockSpec((tm, D_MODEL), lambda i, *_: _sample_block(i, tm))]


def _load_x(i, x_refs, tm=TM):
    if len(x_refs) == 1:
        return x_refs[0][...]
    return jnp.where(i < N_PROMPT // tm, x_refs[0][...], x_refs[1][...])


def _norm_mod(x, g, scale, shift):
    y = x * lax.rsqrt(jnp.mean(x * x, axis=-1, keepdims=True) + EPS)
    return (y * g) * (1.0 + scale) + shift


def _silu(x):
    return x * jax.nn.sigmoid(x)


def _dot(a, b):
    return jnp.dot(a, b, preferred_element_type=f32)


def _dot_nt(a, b):
    return lax.dot_general(a, b, (((1,), (1,)), ((), ())), preferred_element_type=f32)


ADA_COLS = 1536


def _adaln_kernel(cond_ref, w_ref, b_ref, o_ref):
    s = _silu(cond_ref[...]).astype(bf16)
    o_ref[...] = _dot(s, w_ref[...].astype(bf16)) + b_ref[...]


def _adaln(cond8, w_mod, b_mod):
    n6 = 6 * D_MODEL
    return pl.pallas_call(
        _adaln_kernel,
        grid=(DEPTH, n6 // ADA_COLS),
        in_specs=[
            pl.BlockSpec((8, D_MODEL), lambda l, j: (0, 0)),
            pl.BlockSpec((None, D_MODEL, ADA_COLS), lambda l, j: (l, 0, j)),
            pl.BlockSpec((None, 1, ADA_COLS), lambda l, j: (l, 0, j)),
        ],
        out_specs=pl.BlockSpec((None, 8, ADA_COLS), lambda l, j: (l, 0, j)),
        out_shape=jax.ShapeDtypeStruct((DEPTH, 8, n6), f32),
        compiler_params=_cparams(("parallel", "parallel")),
        name="adaln",
    )(cond8, w_mod, b_mod.reshape(DEPTH, 1, n6))


def _rope_block(blk, cos, sin_a, sin_b):
    return blk * cos + pltpu.roll(blk, LANES - 16, 1) * sin_a + pltpu.roll(blk, 16, 1) * sin_b


def _inproj_kernel(*refs, chunks, n_x):
    x_refs, kv_refs = refs[:n_x], refs[n_x + 7:]
    mod_ref, g_ref, w_ref, cos_ref, sa_ref, sb_ref, qkv_ref = refs[n_x:n_x + 7]
    i = pl.program_id(0)
    h = _norm_mod(_load_x(i, x_refs, TD), g_ref[...], mod_ref[1:2, :], mod_ref[0:1, :]).astype(bf16)
    is_prompt = i < N_PROMPT // TD

    @pl.when(is_prompt)
    def _():
        for c0, c1, _, kv_out in chunks:
            acc = _dot(h, w_ref[:, c0:c1])
            qkv_ref[:, c0:c1] = acc.astype(bf16)
            for ridx, a0, a1, o0 in kv_out:
                kv_refs[ridx][:, o0:o0 + (a1 - a0)] = acc[:, a0:a1]

    @pl.when(jnp.logical_not(is_prompt))
    def _():
        cos, sa, sb = cos_ref[...], sa_ref[...], sb_ref[...]
        for c0, c1, rope_blocks, _ in chunks:
            acc = _dot(h, w_ref[:, c0:c1])
            for b in range((c1 - c0) // LANES):
                blk = acc[:, b * LANES:(b + 1) * LANES]
                if b in rope_blocks:
                    blk = _rope_block(blk, cos, sa, sb)
                qkv_ref[:, c0 + b * LANES:c0 + (b + 1) * LANES] = blk.astype(bf16)


def _inproj(x_parts, mod_l, g, w, rope_tabs, chunks, kv_widths):
    n = w.shape[1]
    cos, sa, sb = rope_tabs

    def rope_idx(i):
        return (jnp.where(i < N_PROMPT // TD, 0, (i - N_PROMPT // TD) % (SAMPLE_SEQ // TD)), 0)

    return pl.pallas_call(
        functools.partial(_inproj_kernel, chunks=chunks, n_x=len(x_parts)),
        grid=(N_TOK // TD,),
        in_specs=_x_specs(x_parts, TD) + [
            pl.BlockSpec((None, 6, D_MODEL), lambda i: (_mod_row(i, TD), 0, 0)),
            pl.BlockSpec((1, D_MODEL), lambda i: (0, 0)),
            pl.BlockSpec((D_MODEL, n), lambda i: (0, 0)),
            pl.BlockSpec((TD, LANES), rope_idx),
            pl.BlockSpec((TD, LANES), rope_idx),
            pl.BlockSpec((TD, LANES), rope_idx),
        ],
        out_specs=[pl.BlockSpec((TD, n), lambda i: (i, 0))]
        + [pl.BlockSpec((TD, wd), lambda i: _prompt_block(i, TD)) for wd in kv_widths],
        out_shape=[jax.ShapeDtypeStruct((N_TOK, n), bf16)]
        + [jax.ShapeDtypeStruct((N_PROMPT, wd), f32) for wd in kv_widths],
        compiler_params=_cparams(("arbitrary",)),
        name="inproj",
    )(*x_parts, mod_l, g, w, cos, sa, sb)


def _rope_tables():
    nq = HEAD_DIM // 4
    t = jnp.arange(SAMPLE_SEQ)
    inv = jnp.power(ROPE_THETA, -jnp.arange(nq, dtype=f32) / nq)
    ang_r = (t // GRID_W).astype(f32)[:, None] * inv
    ang_c = (t % GRID_W).astype(f32)[:, None] * inv
    zero = jnp.zeros_like(ang_r)

    def head(fr, fc):
        return jnp.concatenate([fr[0], fr[1], fc[0], fc[1]], axis=-1)

    cos = head((jnp.cos(ang_r), jnp.cos(ang_r)), (jnp.cos(ang_c), jnp.cos(ang_c)))
    sin_a = head((-jnp.sin(ang_r), zero), (-jnp.sin(ang_c), zero))
    sin_b = head((zero, jnp.sin(ang_r)), (zero, jnp.sin(ang_c)))
    two = lambda a: jnp.concatenate([a, a], axis=-1)
    return two(cos), two(sin_a), two(sin_b)


def _lane_lo(shape):
    return lax.broadcasted_iota(i32, shape, len(shape) - 1) < HEAD_DIM


def _half(q, lo_mask, half):
    keep = lo_mask if half == 0 else jnp.logical_not(lo_mask)
    return jnp.where(keep, q, jnp.zeros_like(q)) * Q_SCALE


def _swap_halves(x):
    return pltpu.roll(x.astype(f32), HEAD_DIM, 1).astype(x.dtype)


def _stack_halves(q, lo_mask):
    return jnp.concatenate([_half(q, lo_mask, 0), _half(q, lo_mask, 1)], axis=0)


def _with_ones(v):
    return jnp.concatenate([v, jnp.ones_like(v)], axis=1)


def _attend(q_rows, n_heads, key_blocks, vx_blocks, fix_scores=None, sinks=None):
    r = q_rows.shape[0] // n_heads
    scores = [_dot_nt(q_rows, k) for k in key_blocks]
    exps = [[] for _ in key_blocks]
    maxes = []
    for h in range(n_heads):
        blocks = [s[h * r:(h + 1) * r] for s in scores]
        if fix_scores is not None:
            blocks = [fix_scores(h, i, s) for i, s in enumerate(blocks)]
        m = functools.reduce(jnp.maximum, [jnp.max(s, axis=-1, keepdims=True) for s in blocks])
        if sinks is not None:
            m = jnp.maximum(m, sinks[h])
        maxes.append(m)
        for i, s in enumerate(blocks):
            exps[i].append(jnp.exp((s - m).astype(bf16)))
    out = functools.reduce(lambda a, b: a + b,
                           [_dot(e[0] if n_heads == 1 else jnp.concatenate(e, axis=0), vx)
                            for e, vx in zip(exps, vx_blocks)])
    outs = []
    for h in range(n_heads):
        den = out[h * r:(h + 1) * r, LANES:]
        if sinks is not None:
            den = den + jnp.exp(sinks[h] - maxes[h])
        outs.append(out[h * r:(h + 1) * r, :LANES] * (1.0 / den))
    return outs


def _gqa_rows(q_blocks, group, lo_mask):
    parts = []
    for q in q_blocks:
        for half in range(2):
            qh = _half(q, lo_mask, half)
            parts.append(qh if half == group else _swap_halves(qh))
    return jnp.concatenate(parts, axis=0)


def _gqa_merge(outs, group, lo_mask):
    fixed = [o if idx % 2 == group else pltpu.roll(o, HEAD_DIM, 1) for idx, o in enumerate(outs)]
    return [jnp.where(lo_mask, fixed[2 * p], fixed[2 * p + 1]) for p in range(len(outs) // 2)]


L0_QA, L0_QB, L0_KB, L0_VB, L0_KA, L0_VA, L0_N = 0, 512, 1024, 1536, 2048, 2176, 2304


def _ctx0_kernel(sink_ref, qkv_ref, o_ref):
    lo = _lane_lo((1, LANES))
    blk = lambda base, j: qkv_ref[:, base + j * LANES:base + (j + 1) * LANES]
    k_a = blk(L0_KA, 0)
    vx_a = _with_ones(blk(L0_VA, 0))
    for g in range(2):
        q_rows = _gqa_rows([blk(L0_QA, 2 * g), blk(L0_QA, 2 * g + 1)], g, lo)
        outs = _attend(q_rows, 4, [k_a], [vx_a], sinks=[sink_ref[4 * g + idx] for idx in range(4)])
        for p, o in enumerate(_gqa_merge(outs, g, lo)):
            j = 2 * g + p
            o_ref[:, j * LANES:(j + 1) * LANES] = o.astype(bf16)
    for j in range(4):
        outs = _attend(_stack_halves(blk(L0_QB, j), lo), 2, [blk(L0_KB, j)], [_with_ones(blk(L0_VB, j))])
        o_ref[:, 512 + j * LANES:512 + (j + 1) * LANES] = jnp.where(lo, outs[0], outs[1]).astype(bf16)


def _ctx0(sink, qkv):
    return pl.pallas_call(
        _ctx0_kernel,
        grid=(N_PROMPT_BATCH,),
        in_specs=[
            pl.BlockSpec(memory_space=pltpu.SMEM),
            pl.BlockSpec((PROMPT_SEQ, L0_N), lambda b: (b, 0)),
        ],
        out_specs=pl.BlockSpec((PROMPT_SEQ, D_MODEL), lambda b: (b, 0)),
        out_shape=jax.ShapeDtypeStruct((N_PROMPT, D_MODEL), bf16),
        compiler_params=_cparams(("parallel",)),
        name="ctx0",
    )(sink, qkv)


WIN_KEYS = 3 * QB
NA_KEY_ROWS = 10
NA_KEYS = NA_KEY_ROWS * GRID_W
N_QB = SAMPLE_SEQ // QB
N_NA_PATTERNS = 5
_PROMPT_QBLOCKS = N_PROMPT // QB


def _na_pattern(n):
    return jnp.where(n < 2, n, jnp.where(n > N_QB - 3, n - (N_QB - 5), 2))


def _lat0_kernel(sink_ref, q_ref, kvb_ref, kva_ref, cak_ref, cav_ref, cbk_ref, cbv_ref, nab_ref, o_ref):
    n = pl.program_id(1)
    lo = _lane_lo((1, LANES))
    kstart = pl.multiple_of(jnp.clip((n - 1) * QB, 0, SAMPLE_SEQ - WIN_KEYS), QB)
    k_a = kva_ref[pl.ds(kstart, WIN_KEYS), 0:LANES]
    v_a = kva_ref[pl.ds(kstart, WIN_KEYS), LANES:2 * LANES]
    c_k = cak_ref[...].astype(bf16)
    keys_a = [c_k, k_a]
    vx_a = [_with_ones(cav_ref[...].astype(bf16)), _with_ones(v_a)]
    qpos = n * QB + lax.broadcasted_iota(i32, (QB, WIN_KEYS), 0)
    kpos = kstart + lax.broadcasted_iota(i32, (QB, WIN_KEYS), 1)
    in_window = jnp.abs(qpos - kpos) <= A_WINDOW
    mask_window = lambda h, i, s: jnp.where(in_window, s, NEG) if i == 1 else s
    for g in range(2):
        q_rows = _gqa_rows([q_ref[:, L0_QA + j * LANES:L0_QA + (j + 1) * LANES] for j in (2 * g, 2 * g + 1)], g, lo)
        outs = _attend(q_rows, 4, keys_a, vx_a, fix_scores=mask_window,
                       sinks=[sink_ref[4 * g + idx] for idx in range(4)])
        for p, o in enumerate(_gqa_merge(outs, g, lo)):
            j = 2 * g + p
            o_ref[:, j * LANES:(j + 1) * LANES] = o.astype(bf16)
    krow = jnp.clip(2 * n - NA_ROWS // 2, 0, SAMPLE_SEQ // GRID_W - NA_KEY_ROWS)
    ktok = pl.multiple_of(krow * GRID_W, QB)
    for j in range(4):
        q_b = q_ref[:, L0_QB + j * LANES:L0_QB + (j + 1) * LANES]
        k_b = kvb_ref[pl.ds(ktok, NA_KEYS), j * LANES:(j + 1) * LANES]
        v_b = kvb_ref[pl.ds(ktok, NA_KEYS), 512 + j * LANES:512 + (j + 1) * LANES]
        cb_k = cbk_ref[:, j * LANES:(j + 1) * LANES].astype(bf16)
        cb_v = cbv_ref[:, j * LANES:(j + 1) * LANES].astype(bf16)
        add_bias = lambda h, i, s, j=j: s + nab_ref[2 * j + h] if i == 1 else s
        outs = _attend(_stack_halves(q_b, lo), 2, [cb_k, k_b], [_with_ones(cb_v), _with_ones(v_b)],
                       fix_scores=add_bias)
        o_ref[:, 512 + j * LANES:512 + (j + 1) * LANES] = jnp.where(lo, outs[0], outs[1]).astype(bf16)


def _lat0(sink, qkv, cak, cav, cbk, cbv, nab):
    sb = N_PROMPT // SAMPLE_SEQ
    return pl.pallas_call(
        _lat0_kernel,
        grid=(N_SAMPLE_BATCH, N_QB),
        in_specs=[
            pl.BlockSpec(memory_space=pltpu.SMEM),
            pl.BlockSpec((QB, 1024), lambda b, n: (_PROMPT_QBLOCKS + b * N_QB + n, 0)),
            pl.BlockSpec((SAMPLE_SEQ, 1024), lambda b, n: (sb + b, 1)),
            pl.BlockSpec((SAMPLE_SEQ, 256), lambda b, n: (sb + b, L0_KA // 256)),
            pl.BlockSpec((None, PAST_LEN, LANES), lambda b, n: (b, 0, 0)),
            pl.BlockSpec((None, PAST_LEN, LANES), lambda b, n: (b, 0, 0)),
            pl.BlockSpec((None, PAST_LEN, 512), lambda b, n: (b, 0, 0)),
            pl.BlockSpec((None, PAST_LEN, 512), lambda b, n: (b, 0, 0)),
            pl.BlockSpec((None, 8, QB, NA_KEYS), lambda b, n: (_na_pattern(n), 0, 0, 0)),
        ],
        out_specs=pl.BlockSpec((QB, D_MODEL), lambda b, n: (b * N_QB + n, 0)),
        out_shape=jax.ShapeDtypeStruct((N_SAMPLE, D_MODEL), bf16),
        compiler_params=_cparams(("parallel", "arbitrary")),
        name="lat0",
    )(sink, qkv, qkv, qkv, cak, cav, cbk, cbv, nab)


def _na_bias_table(rel_bias):
    rows = SAMPLE_SEQ // GRID_W
    n_dr, n_dc = 2 * NA_ROWS - 1, 2 * NA_COLS - 1
    c = np.arange(GRID_W)[:, None]
    kc = np.arange(GRID_W)[None, :]
    cs = np.clip(c - NA_COLS // 2, 0, GRID_W - NA_COLS)
    col_ok = (kc >= cs) & (kc < cs + NA_COLS)
    col_hot = ((kc - c + NA_COLS - 1)[None] == np.arange(n_dc)[:, None, None]) & col_ok[None]
    row_hot = np.zeros((N_NA_PATTERNS, 2, NA_KEY_ROWS, n_dr), np.float32)
    for p, n in enumerate((0, 1, 2, N_QB - 2, N_QB - 1)):
        k0 = int(np.clip(2 * n - NA_ROWS // 2, 0, rows - NA_KEY_ROWS))
        for rq in range(2):
            r = 2 * n + rq
            rs = int(np.clip(r - NA_ROWS // 2, 0, rows - NA_ROWS))
            for kl in range(NA_KEY_ROWS):
                if rs <= k0 + kl < rs + NA_ROWS:
                    row_hot[p, rq, kl, k0 + kl - r + NA_ROWS - 1] = 1.0
    ok = (row_hot.sum(-1) > 0)[:, None, :, None, :, None] & col_ok[None, None, None, :, None, :]
    hp = lax.Precision.HIGHEST
    toeplitz = jnp.einsum("hdx,xck->hdck", rel_bias.astype(f32), col_hot.astype(np.float32), precision=hp)
    tab = jnp.einsum("prkd,hdcx->phrckx", row_hot, toeplitz, precision=hp)
    tab = tab + np.where(ok, 0.0, NEG).astype(np.float32)
    return tab.reshape(N_NA_PATTERNS, 8, QB, NA_KEYS)


def _diff_lambda(lam_ref, lam_init):
    lv = lam_ref[...]
    s1 = jnp.sum(lv[0:1, :] * lv[1:2, :], axis=-1, keepdims=True)
    s2 = jnp.sum(lv[2:3, :] * lv[3:4, :], axis=-1, keepdims=True)
    return jnp.exp(s1) - jnp.exp(s2) + lam_init


def _diff_head(q, key_blocks, value_blocks, lam, subln, lo, lam_init):
    o1, o2 = _attend(_stack_halves(q, lo), 2, key_blocks, [_with_ones(v) for v in value_blocks])
    o = o1 - lam * o2
    o = o * lax.rsqrt(jnp.mean(o * o, axis=-1, keepdims=True) + EPS)
    return (o * subln) * (1.0 - lam_init)


def _ctx1_kernel(lam_ref, subln_ref, qkv_ref, o_ref, *, lam_init):
    lo = _lane_lo((1, LANES))
    lam = _diff_lambda(lam_ref, lam_init)
    subln = subln_ref[...]
    for h in range(8):
        q = qkv_ref[:, h * LANES:(h + 1) * LANES]
        k = qkv_ref[:, 1024 + h * LANES:1024 + (h + 1) * LANES]
        v = qkv_ref[:, 2048 + h * LANES:2048 + (h + 1) * LANES]
        o_ref[:, h * LANES:(h + 1) * LANES] = _diff_head(q, [k], [v], lam, subln, lo, lam_init).astype(bf16)


def _ctx1(lamv, subln, qkv, lam_init):
    return pl.pallas_call(
        functools.partial(_ctx1_kernel, lam_init=lam_init),
        grid=(N_PROMPT_BATCH,),
        in_specs=[
            pl.BlockSpec((8, HEAD_DIM), lambda b: (0, 0)),
            pl.BlockSpec((1, LANES), lambda b: (0, 0)),
            pl.BlockSpec((PROMPT_SEQ, 3 * D_MODEL), lambda b: (b, 0)),
        ],
        out_specs=pl.BlockSpec((PROMPT_SEQ, D_MODEL), lambda b: (b, 0)),
        out_shape=jax.ShapeDtypeStruct((N_PROMPT, D_MODEL), bf16),
        compiler_params=_cparams(("parallel",)),
        name="ctx1",
    )(lamv, subln, qkv)


def _lat1_kernel(lam_ref, subln_ref, q_ref, k_ref, v_ref, ck_ref, cv_ref, o_ref, *, lam_init):
    lo = _lane_lo((1, LANES))
    lam = _diff_lambda(lam_ref, lam_init)
    subln = subln_ref[...]
    for h in range(8):
        sl = slice(h * LANES, (h + 1) * LANES)
        o_ref[:, sl] = _diff_head(q_ref[:, sl], [ck_ref[:, sl].astype(bf16), k_ref[:, sl]],
                                  [cv_ref[:, sl].astype(bf16), v_ref[:, sl]],
                                  lam, subln, lo, lam_init).astype(bf16)


def _lat1(lamv, subln, qkv, ck, cv, lam_init):
    sb = N_PROMPT // SAMPLE_SEQ
    nq = SAMPLE_SEQ // TM
    return pl.pallas_call(
        functools.partial(_lat1_kernel, lam_init=lam_init),
        grid=(N_SAMPLE_BATCH, nq),
        in_specs=[
            pl.BlockSpec((8, HEAD_DIM), lambda b, n: (0, 0)),
            pl.BlockSpec((1, LANES), lambda b, n: (0, 0)),
            pl.BlockSpec((TM, D_MODEL), lambda b, n: (N_PROMPT_TILES + b * nq + n, 0)),
            pl.BlockSpec((SAMPLE_SEQ, D_MODEL), lambda b, n: (sb + b, 1)),
            pl.BlockSpec((SAMPLE_SEQ, D_MODEL), lambda b, n: (sb + b, 2)),
            pl.BlockSpec((None, PAST_LEN, D_MODEL), lambda b, n: (b, 0, 0)),
            pl.BlockSpec((None, PAST_LEN, D_MODEL), lambda b, n: (b, 0, 0)),
        ],
        out_specs=pl.BlockSpec((TM, D_MODEL), lambda b, n: (b * nq + n, 0)),
        out_shape=jax.ShapeDtypeStruct((N_SAMPLE, D_MODEL), bf16),
        compiler_params=_cparams(("parallel", "arbitrary")),
        name="lat1",
    )(lamv, subln, qkv, qkv, qkv, ck, cv)


def _outproj_kernel(*refs, n_x):
    x_refs = refs[:n_x]
    op_ref, os_ref, mod_ref, w_ref, o_ref = refs[n_x:]
    i = pl.program_id(0)
    attn = jnp.where(i < N_PROMPT // TD, op_ref[...], os_ref[...])
    o_ref[...] = _load_x(i, x_refs, TD) + mod_ref[2:3, :] * _dot(attn, w_ref[...])


def _outproj(x_parts, o_prompt, o_sample, mod_l, w):
    return pl.pallas_call(
        functools.partial(_outproj_kernel, n_x=len(x_parts)),
        grid=(N_TOK // TD,),
        in_specs=_x_specs(x_parts, TD) + [
            pl.BlockSpec((TD, D_MODEL), lambda i: _prompt_block(i, TD)),
            pl.BlockSpec((TD, D_MODEL), lambda i: _sample_block(i, TD)),
            pl.BlockSpec((None, 6, D_MODEL), lambda i: (_mod_row(i, TD), 0, 0)),
            pl.BlockSpec((D_MODEL, D_MODEL), lambda i: (0, 0)),
        ],
        out_specs=pl.BlockSpec((TD, D_MODEL), lambda i: (i, 0)),
        out_shape=jax.ShapeDtypeStruct((N_TOK, D_MODEL), f32),
        compiler_params=_cparams(("parallel",)),
        name="outproj",
    )(*x_parts, o_prompt, o_sample, mod_l, w)


def _split_bf16(a):
    hi = a.astype(bf16)
    return hi, (a - hi.astype(f32)).astype(bf16)


def _route_kernel(x_ref, mod_ref, g_ref, rwt_ref, rb_ref, xloc_ref, slots_ref, len_ref):
    ng, ge = N_GROUPS, N_EXPERTS // N_GROUPS
    h = _norm_mod(x_ref[...], g_ref[...], mod_ref[4:5, :], mod_ref[3:4, :])
    h_hi, h_lo = _split_bf16(h)
    w_hi, w_lo = _split_bf16(rwt_ref[...])
    logits = _dot_nt(w_hi, h_hi) + (_dot_nt(w_hi, h_lo) + _dot_nt(w_lo, h_hi))
    scores = jax.nn.sigmoid(logits)
    biased = scores + rb_ref[...]
    s3 = scores.reshape(ng, ge, TM)
    b3 = biased.reshape(ng, ge, TM)
    in_group = lax.broadcasted_iota(i32, (ng, ge, TM), 1).astype(f32)
    group_id = lax.broadcasted_iota(i32, (ng, 1, TM), 0).astype(f32)
    expert_id = lax.broadcasted_iota(i32, (ng, ge, TM), 0).astype(f32) * ge + in_group

    def max01(a):
        return jnp.max(jnp.max(a, axis=0, keepdims=True), axis=1, keepdims=True)

    def min01(a):
        return jnp.min(jnp.min(a, axis=0, keepdims=True), axis=1, keepdims=True)

    def sum01(a):
        return jnp.sum(jnp.sum(a, axis=0, keepdims=True), axis=1, keepdims=True)

    m1 = jnp.max(b3, axis=1, keepdims=True)
    first = jnp.min(jnp.where(b3 == m1, in_group, ge), axis=1, keepdims=True)
    m2 = jnp.max(jnp.where(in_group == first, -jnp.inf, b3), axis=1, keepdims=True)
    gscore = m1 + m2
    gsel = jnp.zeros((ng, 1, TM), f32)
    for _ in range(TOPK_GROUPS):
        gm = jnp.max(gscore, axis=0, keepdims=True)
        gi = jnp.min(jnp.where(gscore == gm, group_id, ng), axis=0, keepdims=True)
        hit = group_id == gi
        gsel = jnp.where(hit, 1.0, gsel)
        gscore = jnp.where(hit, -jnp.inf, gscore)
    cand = jnp.where(jnp.broadcast_to(gsel, (ng, ge, TM)) > 0.0, b3, -jnp.inf)
    top_e, top_w = [], []
    for _ in range(TOP_K):
        em = max01(cand)
        ei = min01(jnp.where(cand == em, expert_id, N_EXPERTS))
        hit = expert_id == ei
        top_e.append(ei)
        top_w.append(sum01(jnp.where(hit, s3, 0.0)))
        cand = jnp.where(hit, -jnp.inf, cand)
    wsum = functools.reduce(lambda a, b: a + b, top_w)
    gates3 = jnp.zeros((ng, ge, TM), f32)
    sel3 = jnp.zeros((ng, ge, TM), f32)
    for ei, w in zip(top_e, top_w):
        hit = expert_id == ei
        gates3 = jnp.where(hit, w / wsum * ROUTED_SCALE, gates3)
        sel3 = jnp.where(hit, 1.0, sel3)
    gates = gates3.reshape(N_EXPERTS, TM)
    sel = sel3.reshape(N_EXPERTS, TM)

    cnt = jnp.sum(sel, axis=1, keepdims=True)
    run_len = jnp.ceil(cnt * (1.0 / CHUNK)) * CHUNK
    r_i = lax.broadcasted_iota(i32, (N_EXPERTS, N_EXPERTS), 0)
    c_i = lax.broadcasted_iota(i32, (N_EXPERTS, N_EXPERTS), 1)
    lower = jnp.where(c_i < r_i, 1.0, 0.0).astype(bf16)
    run_off = _dot(lower, jnp.broadcast_to(run_len, (N_EXPERTS, LANES)).astype(bf16))[:, 0:1]
    t_r = lax.broadcasted_iota(i32, (TM, TM), 0)
    t_c = lax.broadcasted_iota(i32, (TM, TM), 1)
    before = jnp.where(t_r < t_c, 1.0, 0.0).astype(bf16)
    rank = _dot(sel.astype(bf16), before)
    slot3 = (run_off + rank).reshape(ng, ge, TM)
    slots = [sum01(jnp.where(expert_id == ei, slot3, 0.0)).reshape(1, TM).astype(i32) for ei in top_e]
    for k in range(TOP_K):
        slots_ref[k:k + 1, :] = slots[k]
    slots_ref[TOP_K:8, :] = jnp.full((8 - TOP_K, TM), -1, i32)
    len_ref[...] = jnp.broadcast_to(run_len, (N_EXPERTS, LANES)).astype(i32)

    g_parts = jnp.concatenate(_split_bf16(gates), axis=0)
    rows = ROUTE_ROWS

    def body(c, carry):
        base = pl.multiple_of(c * rows, rows)
        row_id = base + lax.broadcasted_iota(i32, (rows, TM), 0)
        p = jnp.zeros((rows, TM), f32)
        for k in range(TOP_K):
            p = jnp.where(row_id == slots[k], 1.0, p)
        p = p.astype(bf16)
        xloc_ref[pl.ds(base, rows), 0:D_MODEL] = _dot(p, h_hi).astype(bf16)
        xloc_ref[pl.ds(base, rows), D_MODEL:XW] = _dot_nt(p, g_parts).astype(bf16)
        return carry

    def zero_body(c, carry):
        base = pl.multiple_of(c * rows, rows)
        xloc_ref[pl.ds(base, rows), :] = jnp.zeros((rows, XW), bf16)
        return carry

    n_used = (jnp.sum(run_len).astype(i32) + (rows - 1)) // rows
    lax.fori_loop(0, n_used, body, 0)
    lax.fori_loop(n_used, SLOTS // rows, zero_body, 0)


def _route(x, mod_l, g, rwt, rb):
    return pl.pallas_call(
        _route_kernel,
        grid=(N_TILES,),
        in_specs=[
            pl.BlockSpec((TM, D_MODEL), lambda i: (i, 0)),
            pl.BlockSpec((None, 6, D_MODEL), lambda i: (_mod_row(i), 0, 0)),
            pl.BlockSpec((1, D_MODEL), lambda i: (0, 0)),
            pl.BlockSpec((N_EXPERTS, D_MODEL), lambda i: (0, 0)),
            pl.BlockSpec((N_EXPERTS, 1), lambda i: (0, 0)),
        ],
        out_specs=[
            pl.BlockSpec((SLOTS, XW), lambda i: (i, 0)),
            pl.BlockSpec((None, 8, TM), lambda i: (i, 0, 0)),
            pl.BlockSpec((None, N_EXPERTS, LANES), lambda i: (i, 0, 0)),
        ],
        out_shape=[
            jax.ShapeDtypeStruct((N_TILES * SLOTS, XW), bf16),
            jax.ShapeDtypeStruct((N_TILES, 8, TM), i32),
            jax.ShapeDtypeStruct((N_TILES, N_EXPERTS, LANES), i32),
        ],
        compiler_params=_cparams(("parallel",)),
        name="route",
    )(x, mod_l, g, rwt, rb)


def _moe_plan(run_len):
    nt, ne = run_len.shape

    def excl_cumsum(a):
        n = a.shape[-1]
        earlier = np.arange(n)[None, :] < np.arange(n)[:, None]
        return jnp.sum(jnp.where(earlier, a[..., None, :], 0), axis=-1)

    def first_diff(a):
        return a - jnp.concatenate([jnp.zeros_like(a[..., :1]), a[..., :-1]], axis=-1)

    off_loc = excl_cumsum(run_len)
    before = excl_cumsum(run_len.T).T
    n_e = jnp.sum(run_len, axis=0)
    n_pad = -(-n_e // GM) * GM
    g_start = excl_cumsum(n_pad)
    total = jnp.sum(n_pad)
    run_dst = g_start[None, :] + before
    run_src = jnp.arange(nt, dtype=i32)[:, None] * SLOTS + off_loc
    dst_f = run_dst.T.reshape(-1)
    shift_f = first_diff((run_src - run_dst).T.reshape(-1))
    rows = jnp.arange(G_TILES * G_CHUNKS, dtype=i32) * CHUNK
    shift = jnp.sum(jnp.where(dst_f[None, :] <= rows[:, None], shift_f[None, :], 0), axis=1)
    in_run = jnp.any((g_start[None, :] <= rows[:, None]) & (rows[:, None] < (g_start + n_e)[None, :]), axis=1)
    chunk_src = jnp.where(in_run, rows + shift, 0).astype(i32)
    loc_rows = jnp.arange(SLOT_CHUNKS, dtype=i32) * CHUNK
    shift_l = first_diff(run_dst - off_loc)
    shift = jnp.sum(jnp.where(off_loc[:, None, :] <= loc_rows[None, :, None], shift_l[:, None, :], 0), axis=2)
    used = jnp.sum(run_len, axis=1)
    chunk_map = jnp.where(loc_rows[None, :] < used[:, None], (loc_rows[None, :] + shift) // CHUNK, 0).astype(i32)
    return (g_start // GM).astype(i32), (n_pad // GM).astype(i32), chunk_src, chunk_map.reshape(-1)


def _gmm_in_copy(xloc_hbm, xbuf, sem, src_row, slot, c):
    return pltpu.make_async_copy(xloc_hbm.at[pl.ds(src_row, CHUNK)],
                                 xbuf.at[slot, pl.ds(c * CHUNK, CHUNK)], sem.at[slot])


def _gmm_out_copy(ybuf, y_hbm, sem, tile, slot):
    return pltpu.make_async_copy(ybuf.at[slot], y_hbm.at[pl.ds(pl.multiple_of(tile * GM, GM), GM)], sem.at[slot])


def _gmm_kernel(t0_ref, nt_ref, cs_ref, xloc_hbm, wg_ref, wu_ref, wd_ref, y_hbm,
                xbuf, ybuf, zbuf, wg_b, wu_b, wd_b, in_sem, out_sem, zsem):
    e = pl.program_id(0)
    last = pl.num_programs(0) - 1
    n_tiles = nt_ref[e]
    first_tile = t0_ref[e]
    total_tiles = t0_ref[last] + nt_ref[last]

    def start_in(tile):
        for c in range(G_CHUNKS):
            src = pl.multiple_of(cs_ref[tile * G_CHUNKS + c], CHUNK)
            _gmm_in_copy(xloc_hbm, xbuf, in_sem, src, tile % GMM_SLOTS, c).start()

    @pl.when(e == 0)
    def _():
        for tile in range(GMM_SLOTS - 1):
            start_in(tile)
        zbuf[...] = jnp.zeros(zbuf.shape, zbuf.dtype)

    def tail_copies(fn):
        for j in range(_GMM_TAIL_PER_STEP):
            tile = total_tiles + e + j * N_EXPERTS

            @pl.when(tile < G_TILES)
            def _():
                fn(pltpu.make_async_copy(zbuf, y_hbm.at[pl.ds(pl.multiple_of(tile * GM, GM), GM)], zsem.at[0]))

    tail_copies(lambda cp: cp.start())

    @pl.when(n_tiles > 0)
    def _():
        wg_b[...] = wg_ref[...].astype(bf16)
        wu_b[...] = wu_ref[...].astype(bf16)
        wd_b[...] = wd_ref[...].astype(bf16)

    lane = lax.broadcasted_iota(i32, (1, LANES), 1)
    gate_lanes = jnp.logical_or(lane == e, lane == e + N_EXPERTS)

    def body(t, carry):
        tile = first_tile + t
        slot = tile % GMM_SLOTS

        @pl.when(tile + (GMM_SLOTS - 1) < total_tiles)
        def _():
            start_in(tile + (GMM_SLOTS - 1))

        for c in range(G_CHUNKS):
            _gmm_in_copy(xloc_hbm, xbuf, in_sem, 0, slot, c).wait()

        @pl.when(tile >= GMM_SLOTS)
        def _():
            _gmm_out_copy(ybuf, y_hbm, out_sem, tile - GMM_SLOTS, slot).wait()

        x = xbuf[slot]
        gate = jnp.sum(jnp.where(gate_lanes, x[:, D_MODEL:XW].astype(f32), 0.0), axis=1, keepdims=True)
        xa = x[:, 0:D_MODEL]
        act = _silu(_dot(xa, wg_b[...])) * _dot(xa, wu_b[...]) * gate
        ybuf[slot] = _dot(act.astype(bf16), wd_b[...]).astype(bf16)
        _gmm_out_copy(ybuf, y_hbm, out_sem, tile, slot).start()
        return carry

    lax.fori_loop(0, n_tiles, body, 0)
    tail_copies(lambda cp: cp.wait())

    @pl.when(e == last)
    def _():
        for back in range(1, GMM_SLOTS + 1):
            tile = total_tiles - back
            _gmm_out_copy(ybuf, y_hbm, out_sem, tile, tile % GMM_SLOTS).wait()


_GMM_TAIL_PER_STEP = -(-(G_TILES - TM * TOP_K * N_TILES // GM) // N_EXPERTS)


def _gmm(tile_start, n_tiles, chunk_src, xloc, wg, wu, wd, layer):
    grid_spec = pltpu.PrefetchScalarGridSpec(
        num_scalar_prefetch=3,
        grid=(N_EXPERTS,),
        in_specs=[
            pl.BlockSpec(memory_space=pl.ANY),
            pl.BlockSpec((None, None, D_MODEL, EXPERT_DIM), lambda e, t0, nt, cs: (layer, e, 0, 0)),
            pl.BlockSpec((None, None, D_MODEL, EXPERT_DIM), lambda e, t0, nt, cs: (layer, e, 0, 0)),
            pl.BlockSpec((None, None, EXPERT_DIM, D_MODEL), lambda e, t0, nt, cs: (layer, e, 0, 0)),
        ],
        out_specs=pl.BlockSpec(memory_space=pl.ANY),
        scratch_shapes=[pltpu.VMEM((GMM_SLOTS, GM, XW), bf16), pltpu.VMEM((GMM_SLOTS, GM, D_MODEL), bf16),
                        pltpu.VMEM((GM, D_MODEL), bf16),
                        pltpu.VMEM((D_MODEL, EXPERT_DIM), bf16), pltpu.VMEM((D_MODEL, EXPERT_DIM), bf16),
                        pltpu.VMEM((EXPERT_DIM, D_MODEL), bf16),
                        pltpu.SemaphoreType.DMA((GMM_SLOTS,)), pltpu.SemaphoreType.DMA((GMM_SLOTS,)),
                        pltpu.SemaphoreType.DMA((1,))],
    )
    return pl.pallas_call(
        _gmm_kernel,
        grid_spec=grid_spec,
        out_shape=jax.ShapeDtypeStruct((G_TILES * GM, D_MODEL), bf16),
        compiler_params=_cparams(("arbitrary",)),
        name="gmm",
    )(tile_start, n_tiles, chunk_src, xloc, wg, wu, wd)


def _combine_copy(y_hbm, ybuf, sem, sorted_chunk, slot, c):
    return pltpu.make_async_copy(y_hbm.at[pl.ds(pl.multiple_of(sorted_chunk * CHUNK, CHUNK), CHUNK)],
                                 ybuf.at[slot, pl.ds(pl.multiple_of(c * CHUNK, CHUNK), CHUNK)], sem.at[slot])


def _combine_kernel(cm_ref, y_hbm, slots_ref, x_ref, mod_ref, g_ref, sg_ref, su_ref, sd_ref, *rest, final):
    if final:
        gf_ref, yp_ref, ys_ref, ybuf, sem = rest
    else:
        o_ref, ybuf, sem = rest
    i = pl.program_id(0)
    n = pl.num_programs(0)
    slot = i % 2

    def start(tile, s):
        for c in range(SLOT_CHUNKS):
            _combine_copy(y_hbm, ybuf, sem, cm_ref[tile * SLOT_CHUNKS + c], s, c).start()

    def wait(s):
        for c in range(SLOT_CHUNKS):
            _combine_copy(y_hbm, ybuf, sem, 0, s, c).wait()

    @pl.when(i == 0)
    def _():
        start(0, 0)

    wait(slot)
    start((i + 1) % n, 1 - slot)

    x = x_ref[...]
    hb = _norm_mod(x, g_ref[...], mod_ref[4:5, :], mod_ref[3:4, :]).astype(bf16)
    shared = _dot((_silu(_dot(hb, sg_ref[...])) * _dot(hb, su_ref[...])).astype(bf16), sd_ref[...])
    row_id = lax.broadcasted_iota(i32, (SLOTS, TM), 0)
    p = jnp.zeros((SLOTS, TM), f32)
    for k in range(TOP_K):
        p = jnp.where(row_id == slots_ref[k:k + 1, :], 1.0, p)
    routed = lax.dot_general(p.astype(bf16), ybuf[slot], (((0,), (0,)), ((), ())), preferred_element_type=f32)
    out = x + mod_ref[5:6, :] * (routed + shared)
    if final:
        y = (out * lax.rsqrt(jnp.mean(out * out, axis=-1, keepdims=True) + EPS)) * gf_ref[...]

        @pl.when(i < N_PROMPT_TILES)
        def _():
            yp_ref[...] = y

        @pl.when(i >= N_PROMPT_TILES)
        def _():
            ys_ref[...] = y
    else:
        o_ref[...] = out

    @pl.when(i == n - 1)
    def _():
        wait(1 - slot)


def _combine(chunk_map, y, slots, x, mod_l, g, sg, su, sd, final_g=None):
    shd = sg.shape[1]
    final = final_g is not None
    row_spec = pl.BlockSpec((TM, D_MODEL), lambda i, cm: (i, 0))
    vec_spec = pl.BlockSpec((1, D_MODEL), lambda i, cm: (0, 0))
    if final:
        out_specs = [pl.BlockSpec((TM, D_MODEL), lambda i, cm: _prompt_block(i)),
                     pl.BlockSpec((TM, D_MODEL), lambda i, cm: _sample_block(i))]
        out_shape = [jax.ShapeDtypeStruct((N_PROMPT, D_MODEL), f32), jax.ShapeDtypeStruct((N_SAMPLE, D_MODEL), f32)]
    else:
        out_specs, out_shape = row_spec, jax.ShapeDtypeStruct((N_TOK, D_MODEL), f32)
    grid_spec = pltpu.PrefetchScalarGridSpec(
        num_scalar_prefetch=1,
        grid=(N_TILES,),
        in_specs=[
            pl.BlockSpec(memory_space=pl.ANY),
            pl.BlockSpec((None, 8, TM), lambda i, cm: (i, 0, 0)),
            row_spec,
            pl.BlockSpec((None, 6, D_MODEL), lambda i, cm: (_mod_row(i), 0, 0)),
            vec_spec,
            pl.BlockSpec((D_MODEL, shd), lambda i, cm: (0, 0)),
            pl.BlockSpec((D_MODEL, shd), lambda i, cm: (0, 0)),
            pl.BlockSpec((shd, D_MODEL), lambda i, cm: (0, 0)),
        ] + ([vec_spec] if final else []),
        out_specs=out_specs,
        scratch_shapes=[pltpu.VMEM((2, SLOTS, D_MODEL), bf16), pltpu.SemaphoreType.DMA((2,))],
    )
    args = (chunk_map, y, slots, x, mod_l, g, sg, su, sd) + ((final_g,) if final else ())
    return pl.pallas_call(
        functools.partial(_combine_kernel, final=final),
        grid_spec=grid_spec,
        out_shape=out_shape,
        compiler_params=_cparams(("arbitrary",)),
        name="combine",
    )(*args)


def _moe(x, mod_l, g, rwt, rb, wg, wu, wd, layer, sg, su, sd, final_g=None):
    xloc, slots, run_len = _route(x, mod_l, g, rwt, rb)
    tile_start, n_tiles, chunk_src, chunk_map = _moe_plan(run_len[:, :, 0])
    y = _gmm(tile_start, n_tiles, chunk_src, xloc, wg, wu, wd, layer)
    return _combine(chunk_map, y, slots, x, mod_l, g, sg, su, sd, final_g)


def _permute_w_in_ab(w):
    return jnp.concatenate([w[:, 0:512], w[:, 768:1280], w[:, 1280:1792], w[:, 1792:2304],
                            w[:, 512:640], w[:, 640:768]], axis=1)


_L0_CHUNKS = (
    (0, 512, (0, 1, 2, 3), ()),
    (512, 1024, (), ()),
    (1024, 1536, (), ((2, 0, 512, 0),)),
    (1536, 2048, (), ((3, 0, 512, 0),)),
    (2048, 2304, (0,), ((0, 0, 128, 0), (1, 128, 256, 0))),
)
_L1_CHUNKS = (
    (0, 512, (0, 1, 2, 3), ()),
    (512, 1024, (0, 1, 2, 3), ()),
    (1024, 1536, (0, 1, 2, 3), ((0, 0, 512, 0),)),
    (1536, 2048, (0, 1, 2, 3), ((0, 0, 512, 512),)),
    (2048, 2560, (), ((1, 0, 512, 0),)),
    (2560, 3072, (), ((1, 0, 512, 512),)),
)


def kernel(x_prompt, x_sample, cache_a_k, cache_a_v, cache_b_k, cache_b_v, cache_c_k, cache_c_v, c, c_ctx, w_mod, b_mod, norm_mix, norm_ffn, w_in_ab, w_out_ab, sink_a, rel_bias_b, w_in_c, w_out_c, lam_q1, lam_k1, lam_q2, lam_k2, subln_c, router_w, router_bias, exp_w_gate, exp_w_up, exp_w_down, sh_w_gate, sh_w_up, sh_w_down, final_norm):
    x = (x_prompt.reshape(N_PROMPT, D_MODEL), x_sample.reshape(N_SAMPLE, D_MODEL))
    cond8 = jnp.concatenate([c_ctx[None, :], c, jnp.zeros((8 - 1 - N_SAMPLE_BATCH, D_MODEL), f32)], axis=0)
    mod = _adaln(cond8, w_mod, b_mod).reshape(DEPTH, 8, 6, D_MODEL)
    rope_tabs = _rope_tables()
    new_kv = {}
    for layer in range(DEPTH):
        li = layer // 2
        mod_l = mod[layer]
        g_mix = norm_mix[layer][None, :]
        g_ffn = norm_ffn[layer][None, :]
        if layer % 2 == 0:
            w_in = _permute_w_in_ab(w_in_ab[li]).astype(bf16)
            qkv, ak, av, bk, bv = _inproj(x, mod_l, g_mix, w_in, rope_tabs, _L0_CHUNKS, (128, 128, 512, 512))
            new_kv["a_k"], new_kv["a_v"], new_kv["b_k"], new_kv["b_v"] = ak, av, bk, bv
            o_p = _ctx0(sink_a[li], qkv)
            o_s = _lat0(sink_a[li], qkv,
                        cache_a_k[:, li].reshape(N_SAMPLE_BATCH, PAST_LEN, LANES),
                        cache_a_v[:, li].reshape(N_SAMPLE_BATCH, PAST_LEN, LANES),
                        cache_b_k[:, li].reshape(N_SAMPLE_BATCH, PAST_LEN, 512),
                        cache_b_v[:, li].reshape(N_SAMPLE_BATCH, PAST_LEN, 512),
                        _na_bias_table(rel_bias_b[li]))
            w_out = w_out_ab[li].astype(bf16)
        else:
            lam_init = 0.8 - 0.6 * math.exp(-0.3 * layer)
            qkv, ck, cv = _inproj(x, mod_l, g_mix, w_in_c[li].astype(bf16), rope_tabs, _L1_CHUNKS, (1024, 1024))
            new_kv["c_k"], new_kv["c_v"] = ck, cv
            lamv = jnp.concatenate([lam_q1[li][None], lam_k1[li][None], lam_q2[li][None], lam_k2[li][None],
                                    jnp.zeros((4, HEAD_DIM), f32)], axis=0)
            subln = subln_c[li][None, :]
            o_p = _ctx1(lamv, subln, qkv, lam_init)
            o_s = _lat1(lamv, subln, qkv,
                        cache_c_k[:, li].reshape(N_SAMPLE_BATCH, PAST_LEN, D_MODEL),
                        cache_c_v[:, li].reshape(N_SAMPLE_BATCH, PAST_LEN, D_MODEL), lam_init)
            w_out = w_out_c[li].astype(bf16)
        x = _outproj(x, o_p, o_s, mod_l, w_out)
        last = layer == DEPTH - 1
        x = _moe(x, mod_l, g_ffn, router_w[layer].T, router_bias[layer][:, None],
                 exp_w_gate, exp_w_up, exp_w_down, layer,
                 sh_w_gate[layer].astype(bf16), sh_w_up[layer].astype(bf16), sh_w_down[layer].astype(bf16),
                 final_norm[None, :] if last else None)
        x = x if last else (x,)
    y_prompt, y_sample = x
    nb, s = N_PROMPT_BATCH, PROMPT_SEQ
    return (y_prompt.reshape(nb, s, D_MODEL), y_sample.reshape(N_SAMPLE_BATCH, SAMPLE_SEQ, D_MODEL),
            new_kv["a_k"].reshape(nb, 1, s, 2, HEAD_DIM), new_kv["a_v"].reshape(nb, 1, s, 2, HEAD_DIM),
            new_kv["b_k"].reshape(nb, 1, s, 8, HEAD_DIM), new_kv["b_v"].reshape(nb, 1, s, 8, HEAD_DIM),
            new_kv["c_k"].reshape(nb, 1, s, 8, 2, HEAD_DIM), new_kv["c_v"].reshape(nb, 1, s, 8, 2 * HEAD_DIM))
```

```python
import functools
import math

import jax
import jax.numpy as jnp
import numpy as np
from jax import lax
from jax.experimental import pallas as pl
from jax.experimental.pallas import tpu as pltpu

f32 = jnp.float32
bf16 = jnp.bfloat16
i32 = jnp.int32

D_MODEL = 1024
N_PROMPT_BATCH = 16
PROMPT_SEQ = 256
DEPTH = 2
N_SAMPLE_BATCH = 2
SAMPLE_SEQ = 2048
PAST_LEN = 512
GRID_W = 64
HEAD_DIM = 64
ROPE_THETA = 10000.0
EPS = 1e-6
A_WINDOW = 128
NA_ROWS = 8
NA_COLS = 16
N_EXPERTS = 64
TOP_K = 6
N_GROUPS = 8
TOPK_GROUPS = 4
EXPERT_DIM = 256
ROUTED_SCALE = 2.5
Q_SCALE = HEAD_DIM ** -0.5

N_PROMPT = N_PROMPT_BATCH * PROMPT_SEQ
N_SAMPLE = N_SAMPLE_BATCH * SAMPLE_SEQ
N_TOK = N_PROMPT + N_SAMPLE

LANES = 128
TM = 256
TD = 512
N_PROMPT_TILES = N_PROMPT // TM
N_TILES = N_TOK // TM
QB = 128
CHUNK = 16
SLOTS = -(-(TM * TOP_K + N_EXPERTS * (CHUNK - 1)) // 256) * 256
SLOT_CHUNKS = SLOTS // CHUNK
ROUTE_ROWS = 512
XW = D_MODEL + LANES
GM = 256
_MAX_SORTED = TM * TOP_K * N_TILES + N_TILES * N_EXPERTS * (CHUNK - 1) + N_EXPERTS * (GM - CHUNK)
G_TILES = -(-_MAX_SORTED // GM)
G_CHUNKS = GM // CHUNK
GMM_SLOTS = 4
VMEM_LIMIT = 56 * 1024 * 1024

NEG = -1e30


def _cparams(sem):
    return pltpu.CompilerParams(dimension_semantics=sem, vmem_limit_bytes=VMEM_LIMIT)


def _mod_row(i, tm=TM):
    return jnp.where(i < N_PROMPT // tm, 0, 1 + (i - N_PROMPT // tm) // (SAMPLE_SEQ // tm))


def _prompt_block(i, tm=TM):
    return (jnp.minimum(i, N_PROMPT // tm - 1), 0)


def _sample_block(i, tm=TM):
    return (jnp.maximum(i - N_PROMPT // tm, 0), 0)


def _x_specs(parts, tm=TM):
    if len(parts) == 1:
        return [pl.BlockSpec((tm, D_MODEL), lambda i, *_: (i, 0))]
    return [pl.BlockSpec((tm, D_MODEL), lambda i, *_: _prompt_block(i, tm)),
            pl.BlockSpec((tm, D_MODEL), lambda i, *_: _sample_block(i, tm))]


def _load_x(i, x_refs, tm=TM):
    if len(x_refs) == 1:
        return x_refs[0][...]
    return jnp.where(i < N_PROMPT // tm, x_refs[0][...], x_refs[1][...])


def _norm_mod(x, g, scale, shift):
    y = x * lax.rsqrt(jnp.mean(x * x, axis=-1, keepdims=True) + EPS)
    return (y * g) * (1.0 + scale) + shift


def _silu(x):
    return x * jax.nn.sigmoid(x)


def _dot(a, b):
    return jnp.dot(a, b, preferred_element_type=f32)


def _dot_nt(a, b):
    return lax.dot_general(a, b, (((1,), (1,)), ((), ())), preferred_element_type=f32)


ADA_COLS = 1536


def _adaln_kernel(cond_ref, w_ref, b_ref, o_ref):
    s = _silu(cond_ref[...]).astype(bf16)
    o_ref[...] = _dot(s, w_ref[...].astype(bf16)) + b_ref[...]


def _adaln(cond8, w_mod, b_mod):
    n6 = 6 * D_MODEL
    return pl.pallas_call(
        _adaln_kernel,
        grid=(DEPTH, n6 // ADA_COLS),
        in_specs=[
            pl.BlockSpec((8, D_MODEL), lambda l, j: (0, 0)),
            pl.BlockSpec((None, D_MODEL, ADA_COLS), lambda l, j: (l, 0, j)),
            pl.BlockSpec((None, 1, ADA_COLS), lambda l, j: (l, 0, j)),
        ],
        out_specs=pl.BlockSpec((None, 8, ADA_COLS), lambda l, j: (l, 0, j)),
        out_shape=jax.ShapeDtypeStruct((DEPTH, 8, n6), f32),
        compiler_params=_cparams(("parallel", "parallel")),
        name="adaln",
    )(cond8, w_mod, b_mod.reshape(DEPTH, 1, n6))


def _rope_block(blk, cos, sin_a, sin_b):
    return blk * cos + pltpu.roll(blk, LANES - 16, 1) * sin_a + pltpu.roll(blk, 16, 1) * sin_b


def _inproj_kernel(*refs, chunks, n_x):
    x_refs, kv_refs = refs[:n_x], refs[n_x + 7:]
    mod_ref, g_ref, w_ref, cos_ref, sa_ref, sb_ref, qkv_ref = refs[n_x:n_x + 7]
    i = pl.program_id(0)
    h = _norm_mod(_load_x(i, x_refs, TD), g_ref[...], mod_ref[1:2, :], mod_ref[0:1, :]).astype(bf16)
    is_prompt = i < N_PROMPT // TD

    @pl.when(is_prompt)
    def _():
        for c0, c1, _, kv_out in chunks:
            acc = _dot(h, w_ref[:, c0:c1])
            qkv_ref[:, c0:c1] = acc.astype(bf16)
            for ridx, a0, a1, o0, per_head in kv_out:
                if per_head:
                    heads = kv_refs[ridx].shape[0] // TD
                    for j in range((a1 - a0) // LANES):
                        kv_refs[ridx][pl.ds(o0 // LANES + j, TD, stride=heads), :] = (
                            acc[:, a0 + j * LANES:a0 + (j + 1) * LANES])
                else:
                    t = acc[:, a0:a1].T
                    for b in range(TD // PROMPT_SEQ):
                        kv_refs[ridx][b, o0:o0 + (a1 - a0), :] = t[:, b * PROMPT_SEQ:(b + 1) * PROMPT_SEQ]

    @pl.when(jnp.logical_not(is_prompt))
    def _():
        cos, sa, sb = cos_ref[...], sa_ref[...], sb_ref[...]
        for c0, c1, rope_blocks, _ in chunks:
            acc = _dot(h, w_ref[:, c0:c1])
            for b in range((c1 - c0) // LANES):
                blk = acc[:, b * LANES:(b + 1) * LANES]
                if b in rope_blocks:
                    blk = _rope_block(blk, cos, sa, sb)
                qkv_ref[:, c0 + b * LANES:c0 + (b + 1) * LANES] = blk.astype(bf16)


def _inproj(x_parts, mod_l, g, w, rope_tabs, chunks, kv_outs):
    n = w.shape[1]
    cos, sa, sb = rope_tabs
    bpt = TD // PROMPT_SEQ

    def rope_idx(i):
        return (jnp.where(i < N_PROMPT // TD, 0, (i - N_PROMPT // TD) % (SAMPLE_SEQ // TD)), 0)

    kv_specs, kv_shapes = [], []
    for kind, size in kv_outs:
        if kind == "T":
            kv_specs.append(pl.BlockSpec((bpt, size, PROMPT_SEQ), lambda i: _prompt_block(i, TD) + (0,)))
            kv_shapes.append(jax.ShapeDtypeStruct((N_PROMPT_BATCH, size, PROMPT_SEQ), f32))
        else:
            kv_specs.append(pl.BlockSpec((TD * size, LANES), lambda i: _prompt_block(i, TD)))
            kv_shapes.append(jax.ShapeDtypeStruct((N_PROMPT * size, LANES), f32))

    return pl.pallas_call(
        functools.partial(_inproj_kernel, chunks=chunks, n_x=len(x_parts)),
        grid=(N_TOK // TD,),
        in_specs=_x_specs(x_parts, TD) + [
            pl.BlockSpec((None, 6, D_MODEL), lambda i: (_mod_row(i, TD), 0, 0)),
            pl.BlockSpec((1, D_MODEL), lambda i: (0, 0)),
            pl.BlockSpec((D_MODEL, n), lambda i: (0, 0)),
            pl.BlockSpec((TD, LANES), rope_idx),
            pl.BlockSpec((TD, LANES), rope_idx),
            pl.BlockSpec((TD, LANES), rope_idx),
        ],
        out_specs=[pl.BlockSpec((TD, n), lambda i: (i, 0))] + kv_specs,
        out_shape=[jax.ShapeDtypeStruct((N_TOK, n), bf16)] + kv_shapes,
        compiler_params=_cparams(("arbitrary",)),
        name="inproj",
    )(*x_parts, mod_l, g, w, cos, sa, sb)


def _rope_tables():
    nq = HEAD_DIM // 4
    t = jnp.arange(SAMPLE_SEQ)
    inv = jnp.power(ROPE_THETA, -jnp.arange(nq, dtype=f32) / nq)
    ang_r = (t // GRID_W).astype(f32)[:, None] * inv
    ang_c = (t % GRID_W).astype(f32)[:, None] * inv
    zero = jnp.zeros_like(ang_r)

    def head(fr, fc):
        return jnp.concatenate([fr[0], fr[1], fc[0], fc[1]], axis=-1)

    cos = head((jnp.cos(ang_r), jnp.cos(ang_r)), (jnp.cos(ang_c), jnp.cos(ang_c)))
    sin_a = head((-jnp.sin(ang_r), zero), (-jnp.sin(ang_c), zero))
    sin_b = head((zero, jnp.sin(ang_r)), (zero, jnp.sin(ang_c)))
    two = lambda a: jnp.concatenate([a, a], axis=-1)
    return two(cos), two(sin_a), two(sin_b)


def _lane_lo(shape):
    return lax.broadcasted_iota(i32, shape, len(shape) - 1) < HEAD_DIM


def _half(q, lo_mask, half):
    keep = lo_mask if half == 0 else jnp.logical_not(lo_mask)
    return jnp.where(keep, q, jnp.zeros_like(q)) * Q_SCALE


def _swap_halves(x):
    return pltpu.roll(x.astype(f32), HEAD_DIM, 1).astype(x.dtype)


def _stack_halves(q, lo_mask):
    return jnp.concatenate([_half(q, lo_mask, 0), _half(q, lo_mask, 1)], axis=0)


def _with_ones(v):
    return jnp.concatenate([v, jnp.ones_like(v)], axis=1)


def _attend(q_rows, n_heads, key_blocks, vx_blocks, fix_scores=None, sinks=None):
    r = q_rows.shape[0] // n_heads
    scores = [_dot_nt(q_rows, k) for k in key_blocks]
    exps = [[] for _ in key_blocks]
    maxes = []
    for h in range(n_heads):
        blocks = [s[h * r:(h + 1) * r] for s in scores]
        if fix_scores is not None:
            blocks = [fix_scores(h, i, s) for i, s in enumerate(blocks)]
        m = functools.reduce(jnp.maximum, [jnp.max(s, axis=-1, keepdims=True) for s in blocks])
        if sinks is not None:
            m = jnp.maximum(m, sinks[h])
        maxes.append(m)
        for i, s in enumerate(blocks):
            exps[i].append(jnp.exp((s - m).astype(bf16)))
    out = functools.reduce(lambda a, b: a + b,
                           [_dot(e[0] if n_heads == 1 else jnp.concatenate(e, axis=0), vx)
                            for e, vx in zip(exps, vx_blocks)])
    outs = []
    for h in range(n_heads):
        den = out[h * r:(h + 1) * r, LANES:]
        if sinks is not None:
            den = den + jnp.exp(sinks[h] - maxes[h])
        outs.append(out[h * r:(h + 1) * r, :LANES] * (1.0 / den))
    return outs


def _gqa_rows(q_blocks, group, lo_mask):
    parts = []
    for q in q_blocks:
        for half in range(2):
            qh = _half(q, lo_mask, half)
            parts.append(qh if half == group else _swap_halves(qh))
    return jnp.concatenate(parts, axis=0)


def _gqa_merge(outs, group, lo_mask):
    fixed = [o if idx % 2 == group else pltpu.roll(o, HEAD_DIM, 1) for idx, o in enumerate(outs)]
    return [jnp.where(lo_mask, fixed[2 * p], fixed[2 * p + 1]) for p in range(len(outs) // 2)]


L0_QA, L0_QB, L0_KB, L0_VB, L0_KA, L0_VA, L0_N = 0, 512, 1024, 1536, 2048, 2176, 2304


def _ctx0_kernel(sink_ref, qkv_ref, o_ref):
    lo = _lane_lo((1, LANES))
    blk = lambda base, j: qkv_ref[:, base + j * LANES:base + (j + 1) * LANES]
    k_a = blk(L0_KA, 0)
    vx_a = _with_ones(blk(L0_VA, 0))
    for g in range(2):
        q_rows = _gqa_rows([blk(L0_QA, 2 * g), blk(L0_QA, 2 * g + 1)], g, lo)
        outs = _attend(q_rows, 4, [k_a], [vx_a], sinks=[sink_ref[4 * g + idx] for idx in range(4)])
        for p, o in enumerate(_gqa_merge(outs, g, lo)):
            j = 2 * g + p
            o_ref[:, j * LANES:(j + 1) * LANES] = o.astype(bf16)
    for j in range(4):
        outs = _attend(_stack_halves(blk(L0_QB, j), lo), 2, [blk(L0_KB, j)], [_with_ones(blk(L0_VB, j))])
        o_ref[:, 512 + j * LANES:512 + (j + 1) * LANES] = jnp.where(lo, outs[0], outs[1]).astype(bf16)


def _ctx0(sink, qkv):
    return pl.pallas_call(
        _ctx0_kernel,
        grid=(N_PROMPT_BATCH,),
        in_specs=[
            pl.BlockSpec(memory_space=pltpu.SMEM),
            pl.BlockSpec((PROMPT_SEQ, L0_N), lambda b: (b, 0)),
        ],
        out_specs=pl.BlockSpec((PROMPT_SEQ, D_MODEL), lambda b: (b, 0)),
        out_shape=jax.ShapeDtypeStruct((N_PROMPT, D_MODEL), bf16),
        compiler_params=_cparams(("parallel",)),
        name="ctx0",
    )(sink, qkv)


WIN_KEYS = 3 * QB
NA_KEY_ROWS = 10
NA_KEYS = NA_KEY_ROWS * GRID_W
N_QB = SAMPLE_SEQ // QB
N_NA_PATTERNS = 5
_PROMPT_QBLOCKS = N_PROMPT // QB


def _na_pattern(n):
    return jnp.where(n < 2, n, jnp.where(n > N_QB - 3, n - (N_QB - 5), 2))


def _lat0_kernel(sink_ref, roff_ref, q_ref, kvb_ref, kva_ref, cak_ref, cav_ref, cbk_ref, cbv_ref, tiles_ref, o_ref):
    n = pl.program_id(1)
    lo = _lane_lo((1, LANES))
    kstart = pl.multiple_of(jnp.clip((n - 1) * QB, 0, SAMPLE_SEQ - WIN_KEYS), QB)
    k_a = kva_ref[pl.ds(kstart, WIN_KEYS), 0:LANES]
    v_a = kva_ref[pl.ds(kstart, WIN_KEYS), LANES:2 * LANES]
    c_k = cak_ref[...].astype(bf16)
    keys_a = [c_k, k_a]
    vx_a = [_with_ones(cav_ref[...].astype(bf16)), _with_ones(v_a)]
    qpos = n * QB + lax.broadcasted_iota(i32, (QB, WIN_KEYS), 0)
    kpos = kstart + lax.broadcasted_iota(i32, (QB, WIN_KEYS), 1)
    in_window = jnp.abs(qpos - kpos) <= A_WINDOW
    mask_window = lambda h, i, s: jnp.where(in_window, s, NEG) if i == 1 else s
    for g in range(2):
        q_rows = _gqa_rows([q_ref[:, L0_QA + j * LANES:L0_QA + (j + 1) * LANES] for j in (2 * g, 2 * g + 1)], g, lo)
        outs = _attend(q_rows, 4, keys_a, vx_a, fix_scores=mask_window,
                       sinks=[sink_ref[4 * g + idx] for idx in range(4)])
        for p, o in enumerate(_gqa_merge(outs, g, lo)):
            j = 2 * g + p
            o_ref[:, j * LANES:(j + 1) * LANES] = o.astype(bf16)
    krow = jnp.clip(2 * n - NA_ROWS // 2, 0, SAMPLE_SEQ // GRID_W - NA_KEY_ROWS)
    ktok = pl.multiple_of(krow * GRID_W, QB)
    pattern = _na_pattern(n)

    def na_bias(head):
        rows = []
        for rq in range(QB // GRID_W):
            blocks = []
            for kb in range(NA_KEY_ROWS // 2):
                d0, d1 = (roff_ref[(pattern * 2 + rq) * NA_KEY_ROWS + 2 * kb + t] for t in range(2))
                blocks.append(jnp.where(lo, tiles_ref[head, d0], tiles_ref[head, d1]))
            rows.append(jnp.concatenate(blocks, axis=1))
        return jnp.concatenate(rows, axis=0)

    for j in range(4):
        q_b = q_ref[:, L0_QB + j * LANES:L0_QB + (j + 1) * LANES]
        k_b = kvb_ref[pl.ds(ktok, NA_KEYS), j * LANES:(j + 1) * LANES]
        v_b = kvb_ref[pl.ds(ktok, NA_KEYS), 512 + j * LANES:512 + (j + 1) * LANES]
        cb_k = cbk_ref[:, j * LANES:(j + 1) * LANES].astype(bf16)
        cb_v = cbv_ref[:, j * LANES:(j + 1) * LANES].astype(bf16)
        add_bias = lambda h, i, s, j=j: s + na_bias(2 * j + h) if i == 1 else s
        outs = _attend(_stack_halves(q_b, lo), 2, [cb_k, k_b], [_with_ones(cb_v), _with_ones(v_b)],
                       fix_scores=add_bias)
        o_ref[:, 512 + j * LANES:512 + (j + 1) * LANES] = jnp.where(lo, outs[0], outs[1]).astype(bf16)


def _lat0(sink, qkv, cak, cav, cbk, cbv, bias_tiles):
    sb = N_PROMPT // SAMPLE_SEQ
    return pl.pallas_call(
        _lat0_kernel,
        grid=(N_SAMPLE_BATCH, N_QB),
        in_specs=[
            pl.BlockSpec(memory_space=pltpu.SMEM),
            pl.BlockSpec(memory_space=pltpu.SMEM),
            pl.BlockSpec((QB, 1024), lambda b, n: (_PROMPT_QBLOCKS + b * N_QB + n, 0)),
            pl.BlockSpec((SAMPLE_SEQ, 1024), lambda b, n: (sb + b, 1)),
            pl.BlockSpec((SAMPLE_SEQ, 256), lambda b, n: (sb + b, L0_KA // 256)),
            pl.BlockSpec((None, PAST_LEN, LANES), lambda b, n: (b, 0, 0)),
            pl.BlockSpec((None, PAST_LEN, LANES), lambda b, n: (b, 0, 0)),
            pl.BlockSpec((None, PAST_LEN, 512), lambda b, n: (b, 0, 0)),
            pl.BlockSpec((None, PAST_LEN, 512), lambda b, n: (b, 0, 0)),
            pl.BlockSpec((8, N_ROW_OFFSETS + 1, GRID_W, LANES), lambda b, n: (0, 0, 0, 0)),
        ],
        out_specs=pl.BlockSpec((QB, D_MODEL), lambda b, n: (b * N_QB + n, 0)),
        out_shape=jax.ShapeDtypeStruct((N_SAMPLE, D_MODEL), bf16),
        compiler_params=_cparams(("parallel", "arbitrary")),
        name="lat0",
    )(sink, jnp.asarray(_na_row_offsets()), qkv, qkv, qkv, cak, cav, cbk, cbv, bias_tiles)


N_ROW_OFFSETS = 2 * NA_ROWS - 1


def _na_row_offsets():
    rows = SAMPLE_SEQ // GRID_W
    idx = np.full((N_NA_PATTERNS, 2, NA_KEY_ROWS), N_ROW_OFFSETS, np.int32)
    for p, n in enumerate((0, 1, 2, N_QB - 2, N_QB - 1)):
        k0 = int(np.clip(2 * n - NA_ROWS // 2, 0, rows - NA_KEY_ROWS))
        for rq in range(2):
            r = 2 * n + rq
            rs = int(np.clip(r - NA_ROWS // 2, 0, rows - NA_ROWS))
            for kl in range(NA_KEY_ROWS):
                if rs <= k0 + kl < rs + NA_ROWS:
                    idx[p, rq, kl] = k0 + kl - r + NA_ROWS - 1
    return idx.reshape(-1)


def _na_bias_tiles(rel_bias):
    n_dc = 2 * NA_COLS - 1
    c = np.arange(GRID_W)[:, None]
    kc = np.arange(GRID_W)[None, :]
    cs = np.clip(c - NA_COLS // 2, 0, GRID_W - NA_COLS)
    col_ok = (kc >= cs) & (kc < cs + NA_COLS)
    col_hot = ((kc - c + NA_COLS - 1)[None] == np.arange(n_dc)[:, None, None]) & col_ok[None]
    hp = lax.Precision.HIGHEST
    tiles = jnp.einsum("hdx,xck->hdck", rel_bias.astype(f32), col_hot.astype(np.float32), precision=hp)
    tiles = tiles + np.where(col_ok, 0.0, NEG).astype(np.float32)
    tiles = jnp.concatenate([tiles, jnp.full((tiles.shape[0], 1, GRID_W, GRID_W), NEG, f32)], axis=1)
    return jnp.concatenate([tiles, tiles], axis=-1)


def _diff_lambda(lam_ref, lam_init):
    lv = lam_ref[...]
    s1 = jnp.sum(lv[0:1, :] * lv[1:2, :], axis=-1, keepdims=True)
    s2 = jnp.sum(lv[2:3, :] * lv[3:4, :], axis=-1, keepdims=True)
    return jnp.exp(s1) - jnp.exp(s2) + lam_init


def _diff_head(q, key_blocks, value_blocks, lam, subln, lo, lam_init):
    o1, o2 = _attend(_stack_halves(q, lo), 2, key_blocks, [_with_ones(v) for v in value_blocks])
    o = o1 - lam * o2
    o = o * lax.rsqrt(jnp.mean(o * o, axis=-1, keepdims=True) + EPS)
    return (o * subln) * (1.0 - lam_init)


def _ctx1_kernel(lam_ref, subln_ref, qkv_ref, o_ref, *, lam_init):
    lo = _lane_lo((1, LANES))
    lam = _diff_lambda(lam_ref, lam_init)
    subln = subln_ref[...]
    for h in range(8):
        q = qkv_ref[:, h * LANES:(h + 1) * LANES]
        k = qkv_ref[:, 1024 + h * LANES:1024 + (h + 1) * LANES]
        v = qkv_ref[:, 2048 + h * LANES:2048 + (h + 1) * LANES]
        o_ref[:, h * LANES:(h + 1) * LANES] = _diff_head(q, [k], [v], lam, subln, lo, lam_init).astype(bf16)


def _ctx1(lamv, subln, qkv, lam_init):
    return pl.pallas_call(
        functools.partial(_ctx1_kernel, lam_init=lam_init),
        grid=(N_PROMPT_BATCH,),
        in_specs=[
            pl.BlockSpec((8, HEAD_DIM), lambda b: (0, 0)),
            pl.BlockSpec((1, LANES), lambda b: (0, 0)),
            pl.BlockSpec((PROMPT_SEQ, 3 * D_MODEL), lambda b: (b, 0)),
        ],
        out_specs=pl.BlockSpec((PROMPT_SEQ, D_MODEL), lambda b: (b, 0)),
        out_shape=jax.ShapeDtypeStruct((N_PROMPT, D_MODEL), bf16),
        compiler_params=_cparams(("parallel",)),
        name="ctx1",
    )(lamv, subln, qkv)


def _lat1_kernel(lam_ref, subln_ref, q_ref, k_ref, v_ref, ck_ref, cv_ref, o_ref, *, lam_init):
    lo = _lane_lo((1, LANES))
    lam = _diff_lambda(lam_ref, lam_init)
    subln = subln_ref[...]
    for h in range(8):
        sl = slice(h * LANES, (h + 1) * LANES)
        o_ref[:, sl] = _diff_head(q_ref[:, sl], [ck_ref[:, sl].astype(bf16), k_ref[:, sl]],
                                  [cv_ref[:, sl].astype(bf16), v_ref[:, sl]],
                                  lam, subln, lo, lam_init).astype(bf16)


def _lat1(lamv, subln, qkv, ck, cv, lam_init):
    sb = N_PROMPT // SAMPLE_SEQ
    nq = SAMPLE_SEQ // TM
    return pl.pallas_call(
        functools.partial(_lat1_kernel, lam_init=lam_init),
        grid=(N_SAMPLE_BATCH, nq),
        in_specs=[
            pl.BlockSpec((8, HEAD_DIM), lambda b, n: (0, 0)),
            pl.BlockSpec((1, LANES), lambda b, n: (0, 0)),
            pl.BlockSpec((TM, D_MODEL), lambda b, n: (N_PROMPT_TILES + b * nq + n, 0)),
            pl.BlockSpec((SAMPLE_SEQ, D_MODEL), lambda b, n: (sb + b, 1)),
            pl.BlockSpec((SAMPLE_SEQ, D_MODEL), lambda b, n: (sb + b, 2)),
            pl.BlockSpec((None, PAST_LEN, D_MODEL), lambda b, n: (b, 0, 0)),
            pl.BlockSpec((None, PAST_LEN, D_MODEL), lambda b, n: (b, 0, 0)),
        ],
        out_specs=pl.BlockSpec((TM, D_MODEL), lambda b, n: (b * nq + n, 0)),
        out_shape=jax.ShapeDtypeStruct((N_SAMPLE, D_MODEL), bf16),
        compiler_params=_cparams(("parallel", "arbitrary")),
        name="lat1",
    )(lamv, subln, qkv, qkv, qkv, ck, cv)


def _outproj_kernel(*refs, n_x):
    x_refs = refs[:n_x]
    op_ref, os_ref, mod_ref, w_ref, o_ref = refs[n_x:]
    i = pl.program_id(0)
    attn = jnp.where(i < N_PROMPT // TD, op_ref[...], os_ref[...])
    o_ref[...] = _load_x(i, x_refs, TD) + mod_ref[2:3, :] * _dot(attn, w_ref[...])


def _outproj(x_parts, o_prompt, o_sample, mod_l, w):
    return pl.pallas_call(
        functools.partial(_outproj_kernel, n_x=len(x_parts)),
        grid=(N_TOK // TD,),
        in_specs=_x_specs(x_parts, TD) + [
            pl.BlockSpec((TD, D_MODEL), lambda i: _prompt_block(i, TD)),
            pl.BlockSpec((TD, D_MODEL), lambda i: _sample_block(i, TD)),
            pl.BlockSpec((None, 6, D_MODEL), lambda i: (_mod_row(i, TD), 0, 0)),
            pl.BlockSpec((D_MODEL, D_MODEL), lambda i: (0, 0)),
        ],
        out_specs=pl.BlockSpec((TD, D_MODEL), lambda i: (i, 0)),
        out_shape=jax.ShapeDtypeStruct((N_TOK, D_MODEL), f32),
        compiler_params=_cparams(("parallel",)),
        name="outproj",
    )(*x_parts, o_prompt, o_sample, mod_l, w)


def _split_bf16(a):
    hi = a.astype(bf16)
    return hi, (a - hi.astype(f32)).astype(bf16)


def _route_kernel(x_ref, mod_ref, g_ref, rwt_ref, rb_ref, xloc_ref, slots_ref, len_ref):
    ng, ge = N_GROUPS, N_EXPERTS // N_GROUPS
    h = _norm_mod(x_ref[...], g_ref[...], mod_ref[4:5, :], mod_ref[3:4, :])
    h_hi, h_lo = _split_bf16(h)
    w_hi, w_lo = _split_bf16(rwt_ref[...])
    logits = _dot_nt(w_hi, h_hi) + (_dot_nt(w_hi, h_lo) + _dot_nt(w_lo, h_hi))
    scores = jax.nn.sigmoid(logits)
    biased = scores + rb_ref[...]
    s3 = scores.reshape(ng, ge, TM)
    b3 = biased.reshape(ng, ge, TM)
    in_group = lax.broadcasted_iota(i32, (ng, ge, TM), 1).astype(f32)
    group_id = lax.broadcasted_iota(i32, (ng, 1, TM), 0).astype(f32)
    expert_id = lax.broadcasted_iota(i32, (ng, ge, TM), 0).astype(f32) * ge + in_group

    def max01(a):
        return jnp.max(jnp.max(a, axis=0, keepdims=True), axis=1, keepdims=True)

    def min01(a):
        return jnp.min(jnp.min(a, axis=0, keepdims=True), axis=1, keepdims=True)

    def sum01(a):
        return jnp.sum(jnp.sum(a, axis=0, keepdims=True), axis=1, keepdims=True)

    m1 = jnp.max(b3, axis=1, keepdims=True)
    first = jnp.min(jnp.where(b3 == m1, in_group, ge), axis=1, keepdims=True)
    m2 = jnp.max(jnp.where(in_group == first, -jnp.inf, b3), axis=1, keepdims=True)
    gscore = m1 + m2
    gsel = jnp.zeros((ng, 1, TM), f32)
    for _ in range(TOPK_GROUPS):
        gm = jnp.max(gscore, axis=0, keepdims=True)
        gi = jnp.min(jnp.where(gscore == gm, group_id, ng), axis=0, keepdims=True)
        hit = group_id == gi
        gsel = jnp.where(hit, 1.0, gsel)
        gscore = jnp.where(hit, -jnp.inf, gscore)
    cand = jnp.where(jnp.broadcast_to(gsel, (ng, ge, TM)) > 0.0, b3, -jnp.inf)
    top_e, top_w = [], []
    for _ in range(TOP_K):
        em = max01(cand)
        ei = min01(jnp.where(cand == em, expert_id, N_EXPERTS))
        hit = expert_id == ei
        top_e.append(ei)
        top_w.append(sum01(jnp.where(hit, s3, 0.0)))
        cand = jnp.where(hit, -jnp.inf, cand)
    wsum = functools.reduce(lambda a, b: a + b, top_w)
    gates3 = jnp.zeros((ng, ge, TM), f32)
    sel3 = jnp.zeros((ng, ge, TM), f32)
    for ei, w in zip(top_e, top_w):
        hit = expert_id == ei
        gates3 = jnp.where(hit, w / wsum * ROUTED_SCALE, gates3)
        sel3 = jnp.where(hit, 1.0, sel3)
    gates = gates3.reshape(N_EXPERTS, TM)
    sel = sel3.reshape(N_EXPERTS, TM)

    cnt = jnp.sum(sel, axis=1, keepdims=True)
    run_len = jnp.ceil(cnt * (1.0 / CHUNK)) * CHUNK
    r_i = lax.broadcasted_iota(i32, (N_EXPERTS, N_EXPERTS), 0)
    c_i = lax.broadcasted_iota(i32, (N_EXPERTS, N_EXPERTS), 1)
    lower = jnp.where(c_i < r_i, 1.0, 0.0).astype(bf16)
    run_off = _dot(lower, jnp.broadcast_to(run_len, (N_EXPERTS, LANES)).astype(bf16))[:, 0:1]
    t_r = lax.broadcasted_iota(i32, (TM, TM), 0)
    t_c = lax.broadcasted_iota(i32, (TM, TM), 1)
    before = jnp.where(t_r < t_c, 1.0, 0.0).astype(bf16)
    rank = _dot(sel.astype(bf16), before)
    slot3 = (run_off + rank).reshape(ng, ge, TM)
    slots = [sum01(jnp.where(expert_id == ei, slot3, 0.0)).reshape(1, TM).astype(i32) for ei in top_e]
    for k in range(TOP_K):
        slots_ref[k:k + 1, :] = slots[k]
    slots_ref[TOP_K:8, :] = jnp.full((8 - TOP_K, TM), -1, i32)
    len_ref[...] = jnp.broadcast_to(run_len, (N_EXPERTS, LANES)).astype(i32)

    g_parts = jnp.concatenate(_split_bf16(gates), axis=0)
    rows = ROUTE_ROWS

    def body(c, carry):
        base = pl.multiple_of(c * rows, rows)
        row_id = base + lax.broadcasted_iota(i32, (rows, TM), 0)
        p = jnp.zeros((rows, TM), f32)
        for k in range(TOP_K):
            p = jnp.where(row_id == slots[k], 1.0, p)
        p = p.astype(bf16)
        xloc_ref[pl.ds(base, rows), 0:D_MODEL] = _dot(p, h_hi).astype(bf16)
        xloc_ref[pl.ds(base, rows), D_MODEL:XW] = _dot_nt(p, g_parts).astype(bf16)
        return carry

    def zero_body(c, carry):
        base = pl.multiple_of(c * rows, rows)
        xloc_ref[pl.ds(base, rows), :] = jnp.zeros((rows, XW), bf16)
        return carry

    n_used = (jnp.sum(run_len).astype(i32) + (rows - 1)) // rows
    lax.fori_loop(0, n_used, body, 0)
    lax.fori_loop(n_used, SLOTS // rows, zero_body, 0)


def _route(x, mod_l, g, rwt, rb):
    return pl.pallas_call(
        _route_kernel,
        grid=(N_TILES,),
        in_specs=[
            pl.BlockSpec((TM, D_MODEL), lambda i: (i, 0)),
            pl.BlockSpec((None, 6, D_MODEL), lambda i: (_mod_row(i), 0, 0)),
            pl.BlockSpec((1, D_MODEL), lambda i: (0, 0)),
            pl.BlockSpec((N_EXPERTS, D_MODEL), lambda i: (0, 0)),
            pl.BlockSpec((N_EXPERTS, 1), lambda i: (0, 0)),
        ],
        out_specs=[
            pl.BlockSpec((SLOTS, XW), lambda i: (i, 0)),
            pl.BlockSpec((None, 8, TM), lambda i: (i, 0, 0)),
            pl.BlockSpec((None, N_EXPERTS, LANES), lambda i: (i, 0, 0)),
        ],
        out_shape=[
            jax.ShapeDtypeStruct((N_TILES * SLOTS, XW), bf16),
            jax.ShapeDtypeStruct((N_TILES, 8, TM), i32),
            jax.ShapeDtypeStruct((N_TILES, N_EXPERTS, LANES), i32),
        ],
        compiler_params=_cparams(("parallel",)),
        name="route",
    )(x, mod_l, g, rwt, rb)


def _moe_plan(run_len):
    nt, ne = run_len.shape

    def excl_cumsum(a):
        n = a.shape[-1]
        earlier = np.arange(n)[None, :] < np.arange(n)[:, None]
        return jnp.sum(jnp.where(earlier, a[..., None, :], 0), axis=-1)

    def first_diff(a):
        return a - jnp.concatenate([jnp.zeros_like(a[..., :1]), a[..., :-1]], axis=-1)

    off_loc = excl_cumsum(run_len)
    before = excl_cumsum(run_len.T).T
    n_e = jnp.sum(run_len, axis=0)
    n_pad = -(-n_e // GM) * GM
    g_start = excl_cumsum(n_pad)
    total = jnp.sum(n_pad)
    run_dst = g_start[None, :] + before
    run_src = jnp.arange(nt, dtype=i32)[:, None] * SLOTS + off_loc
    dst_f = run_dst.T.reshape(-1)
    shift_f = first_diff((run_src - run_dst).T.reshape(-1))
    rows = jnp.arange(G_TILES * G_CHUNKS, dtype=i32) * CHUNK
    shift = jnp.sum(jnp.where(dst_f[None, :] <= rows[:, None], shift_f[None, :], 0), axis=1)
    in_run = jnp.any((g_start[None, :] <= rows[:, None]) & (rows[:, None] < (g_start + n_e)[None, :]), axis=1)
    chunk_src = jnp.where(in_run, rows + shift, 0).astype(i32)
    loc_rows = jnp.arange(SLOT_CHUNKS, dtype=i32) * CHUNK
    shift_l = first_diff(run_dst - off_loc)
    shift = jnp.sum(jnp.where(off_loc[:, None, :] <= loc_rows[None, :, None], shift_l[:, None, :], 0), axis=2)
    used = jnp.sum(run_len, axis=1)
    chunk_map = jnp.where(loc_rows[None, :] < used[:, None], (loc_rows[None, :] + shift) // CHUNK, 0).astype(i32)
    return (g_start // GM).astype(i32), (n_pad // GM).astype(i32), chunk_src, chunk_map.reshape(-1)


def _gmm_in_copy(xloc_hbm, xbuf, sem, src_row, slot, c):
    return pltpu.make_async_copy(xloc_hbm.at[pl.ds(src_row, CHUNK)],
                                 xbuf.at[slot, pl.ds(c * CHUNK, CHUNK)], sem.at[slot])


def _gmm_out_copy(ybuf, y_hbm, sem, tile, slot):
    return pltpu.make_async_copy(ybuf.at[slot], y_hbm.at[pl.ds(pl.multiple_of(tile * GM, GM), GM)], sem.at[slot])


def _gmm_kernel(t0_ref, nt_ref, cs_ref, xloc_hbm, wg_ref, wu_ref, wd_ref, y_hbm,
                xbuf, ybuf, zbuf, wg_b, wu_b, wd_b, in_sem, out_sem, zsem):
    e = pl.program_id(0)
    last = pl.num_programs(0) - 1
    n_tiles = nt_ref[e]
    first_tile = t0_ref[e]
    total_tiles = t0_ref[last] + nt_ref[last]

    def start_in(tile):
        for c in range(G_CHUNKS):
            src = pl.multiple_of(cs_ref[tile * G_CHUNKS + c], CHUNK)
            _gmm_in_copy(xloc_hbm, xbuf, in_sem, src, tile % GMM_SLOTS, c).start()

    @pl.when(e == 0)
    def _():
        for tile in range(GMM_SLOTS - 1):
            start_in(tile)
        zbuf[...] = jnp.zeros(zbuf.shape, zbuf.dtype)

    def tail_copies(fn):
        for j in range(_GMM_TAIL_PER_STEP):
            tile = total_tiles + e + j * N_EXPERTS

            @pl.when(tile < G_TILES)
            def _():
                fn(pltpu.make_async_copy(zbuf, y_hbm.at[pl.ds(pl.multiple_of(tile * GM, GM), GM)], zsem.at[0]))

    tail_copies(lambda cp: cp.start())

    @pl.when(n_tiles > 0)
    def _():
        wg_b[...] = wg_ref[...].astype(bf16)
        wu_b[...] = wu_ref[...].astype(bf16)
        wd_b[...] = wd_ref[...].astype(bf16)

    lane = lax.broadcasted_iota(i32, (1, LANES), 1)
    gate_lanes = jnp.logical_or(lane == e, lane == e + N_EXPERTS)

    def body(t, carry):
        tile = first_tile + t
        slot = tile % GMM_SLOTS

        @pl.when(tile + (GMM_SLOTS - 1) < total_tiles)
        def _():
            start_in(tile + (GMM_SLOTS - 1))

        for c in range(G_CHUNKS):
            _gmm_in_copy(xloc_hbm, xbuf, in_sem, 0, slot, c).wait()

        @pl.when(tile >= GMM_SLOTS)
        def _():
            _gmm_out_copy(ybuf, y_hbm, out_sem, tile - GMM_SLOTS, slot).wait()

        x = xbuf[slot]
        gate = jnp.sum(jnp.where(gate_lanes, x[:, D_MODEL:XW].astype(f32), 0.0), axis=1, keepdims=True)
        xa = x[:, 0:D_MODEL]
        act = _silu(_dot(xa, wg_b[...])) * _dot(xa, wu_b[...]) * gate
        ybuf[slot] = _dot(act.astype(bf16), wd_b[...]).astype(bf16)
        _gmm_out_copy(ybuf, y_hbm, out_sem, tile, slot).start()
        return carry

    lax.fori_loop(0, n_tiles, body, 0)
    tail_copies(lambda cp: cp.wait())

    @pl.when(e == last)
    def _():
        for back in range(1, GMM_SLOTS + 1):
            tile = total_tiles - back
            _gmm_out_copy(ybuf, y_hbm, out_sem, tile, tile % GMM_SLOTS).wait()


_GMM_TAIL_PER_STEP = -(-(G_TILES - TM * TOP_K * N_TILES // GM) // N_EXPERTS)


def _gmm(tile_start, n_tiles, chunk_src, xloc, wg, wu, wd, layer):
    grid_spec = pltpu.PrefetchScalarGridSpec(
        num_scalar_prefetch=3,
        grid=(N_EXPERTS,),
        in_specs=[
            pl.BlockSpec(memory_space=pl.ANY),
            pl.BlockSpec((None, None, D_MODEL, EXPERT_DIM), lambda e, t0, nt, cs: (layer, e, 0, 0)),
            pl.BlockSpec((None, None, D_MODEL, EXPERT_DIM), lambda e, t0, nt, cs: (layer, e, 0, 0)),
            pl.BlockSpec((None, None, EXPERT_DIM, D_MODEL), lambda e, t0, nt, cs: (layer, e, 0, 0)),
        ],
        out_specs=pl.BlockSpec(memory_space=pl.ANY),
        scratch_shapes=[pltpu.VMEM((GMM_SLOTS, GM, XW), bf16), pltpu.VMEM((GMM_SLOTS, GM, D_MODEL), bf16),
                        pltpu.VMEM((GM, D_MODEL), bf16),
                        pltpu.VMEM((D_MODEL, EXPERT_DIM), bf16), pltpu.VMEM((D_MODEL, EXPERT_DIM), bf16),
                        pltpu.VMEM((EXPERT_DIM, D_MODEL), bf16),
                        pltpu.SemaphoreType.DMA((GMM_SLOTS,)), pltpu.SemaphoreType.DMA((GMM_SLOTS,)),
                        pltpu.SemaphoreType.DMA((1,))],
    )
    return pl.pallas_call(
        _gmm_kernel,
        grid_spec=grid_spec,
        out_shape=jax.ShapeDtypeStruct((G_TILES * GM, D_MODEL), bf16),
        compiler_params=_cparams(("arbitrary",)),
        name="gmm",
    )(tile_start, n_tiles, chunk_src, xloc, wg, wu, wd)


def _combine_copy(y_hbm, ybuf, sem, sorted_chunk, slot, c):
    return pltpu.make_async_copy(y_hbm.at[pl.ds(pl.multiple_of(sorted_chunk * CHUNK, CHUNK), CHUNK)],
                                 ybuf.at[slot, pl.ds(pl.multiple_of(c * CHUNK, CHUNK), CHUNK)], sem.at[slot])


def _combine_kernel(cm_ref, y_hbm, slots_ref, x_ref, mod_ref, g_ref, sg_ref, su_ref, sd_ref, *rest, final):
    if final:
        gf_ref, yp_ref, ys_ref, ybuf, sem = rest
    else:
        o_ref, ybuf, sem = rest
    i = pl.program_id(0)
    n = pl.num_programs(0)
    slot = i % 2

    def start(tile, s):
        for c in range(SLOT_CHUNKS):
            _combine_copy(y_hbm, ybuf, sem, cm_ref[tile * SLOT_CHUNKS + c], s, c).start()

    def wait(s):
        for c in range(SLOT_CHUNKS):
            _combine_copy(y_hbm, ybuf, sem, 0, s, c).wait()

    @pl.when(i == 0)
    def _():
        start(0, 0)

    wait(slot)
    start((i + 1) % n, 1 - slot)

    x = x_ref[...]
    hb = _norm_mod(x, g_ref[...], mod_ref[4:5, :], mod_ref[3:4, :]).astype(bf16)
    shared = _dot((_silu(_dot(hb, sg_ref[...])) * _dot(hb, su_ref[...])).astype(bf16), sd_ref[...])
    row_id = lax.broadcasted_iota(i32, (SLOTS, TM), 0)
    p = jnp.zeros((SLOTS, TM), f32)
    for k in range(TOP_K):
        p = jnp.where(row_id == slots_ref[k:k + 1, :], 1.0, p)
    routed = lax.dot_general(p.astype(bf16), ybuf[slot], (((0,), (0,)), ((), ())), preferred_element_type=f32)
    out = x + mod_ref[5:6, :] * (routed + shared)
    if final:
        y = (out * lax.rsqrt(jnp.mean(out * out, axis=-1, keepdims=True) + EPS)) * gf_ref[...]

        @pl.when(i < N_PROMPT_TILES)
        def _():
            yp_ref[...] = y

        @pl.when(i >= N_PROMPT_TILES)
        def _():
            ys_ref[...] = y
    else:
        o_ref[...] = out

    @pl.when(i == n - 1)
    def _():
        wait(1 - slot)


def _combine(chunk_map, y, slots, x, mod_l, g, sg, su, sd, final_g=None):
    shd = sg.shape[1]
    final = final_g is not None
    row_spec = pl.BlockSpec((TM, D_MODEL), lambda i, cm: (i, 0))
    vec_spec = pl.BlockSpec((1, D_MODEL), lambda i, cm: (0, 0))
    if final:
        out_specs = [pl.BlockSpec((TM, D_MODEL), lambda i, cm: _prompt_block(i)),
                     pl.BlockSpec((TM, D_MODEL), lambda i, cm: _sample_block(i))]
        out_shape = [jax.ShapeDtypeStruct((N_PROMPT, D_MODEL), f32), jax.ShapeDtypeStruct((N_SAMPLE, D_MODEL), f32)]
    else:
        out_specs, out_shape = row_spec, jax.ShapeDtypeStruct((N_TOK, D_MODEL), f32)
    grid_spec = pltpu.PrefetchScalarGridSpec(
        num_scalar_prefetch=1,
        grid=(N_TILES,),
        in_specs=[
            pl.BlockSpec(memory_space=pl.ANY),
            pl.BlockSpec((None, 8, TM), lambda i, cm: (i, 0, 0)),
            row_spec,
            pl.BlockSpec((None, 6, D_MODEL), lambda i, cm: (_mod_row(i), 0, 0)),
            vec_spec,
            pl.BlockSpec((D_MODEL, shd), lambda i, cm: (0, 0)),
            pl.BlockSpec((D_MODEL, shd), lambda i, cm: (0, 0)),
            pl.BlockSpec((shd, D_MODEL), lambda i, cm: (0, 0)),
        ] + ([vec_spec] if final else []),
        out_specs=out_specs,
        scratch_shapes=[pltpu.VMEM((2, SLOTS, D_MODEL), bf16), pltpu.SemaphoreType.DMA((2,))],
    )
    args = (chunk_map, y, slots, x, mod_l, g, sg, su, sd) + ((final_g,) if final else ())
    return pl.pallas_call(
        functools.partial(_combine_kernel, final=final),
        grid_spec=grid_spec,
        out_shape=out_shape,
        compiler_params=_cparams(("arbitrary",)),
        name="combine",
    )(*args)


def _moe(x, mod_l, g, rwt, rb, wg, wu, wd, layer, sg, su, sd, final_g=None):
    xloc, slots, run_len = _route(x, mod_l, g, rwt, rb)
    tile_start, n_tiles, chunk_src, chunk_map = _moe_plan(run_len[:, :, 0])
    y = _gmm(tile_start, n_tiles, chunk_src, xloc, wg, wu, wd, layer)
    return _combine(chunk_map, y, slots, x, mod_l, g, sg, su, sd, final_g)


def _permute_w_in_ab(w):
    return jnp.concatenate([w[:, 0:512], w[:, 768:1280], w[:, 1280:1792], w[:, 1792:2304],
                            w[:, 512:640], w[:, 640:768]], axis=1)


_L0_CHUNKS = (
    (0, 512, (0, 1, 2, 3), ()),
    (512, 1024, (), ()),
    (1024, 1536, (), ((2, 0, 512, 0, False),)),
    (1536, 2048, (), ((3, 0, 512, 0, False),)),
    (2048, 2304, (0,), ((0, 0, 128, 0, False), (1, 128, 256, 0, False))),
)
_L0_KV_OUTS = (("T", 128), ("T", 128), ("T", 512), ("T", 512))
_L1_CHUNKS = (
    (0, 512, (0, 1, 2, 3), ()),
    (512, 1024, (0, 1, 2, 3), ()),
    (1024, 1536, (0, 1, 2, 3), ((0, 0, 512, 0, False),)),
    (1536, 2048, (0, 1, 2, 3), ((0, 0, 512, 512, False),)),
    (2048, 2560, (), ((1, 0, 512, 0, True),)),
    (2560, 3072, (), ((1, 0, 512, 512, True),)),
)
_L1_KV_OUTS = (("T", 1024), ("H", 8))


def _from_feature_major(kt, *head_dims):
    nb, _, s = kt.shape
    nd = len(head_dims)
    return kt.reshape(nb, *head_dims, s).transpose(0, nd + 1, *range(1, nd + 1))[:, None]


def kernel(x_prompt, x_sample, cache_a_k, cache_a_v, cache_b_k, cache_b_v, cache_c_k, cache_c_v, c, c_ctx, w_mod, b_mod, norm_mix, norm_ffn, w_in_ab, w_out_ab, sink_a, rel_bias_b, w_in_c, w_out_c, lam_q1, lam_k1, lam_q2, lam_k2, subln_c, router_w, router_bias, exp_w_gate, exp_w_up, exp_w_down, sh_w_gate, sh_w_up, sh_w_down, final_norm):
    x = (x_prompt.reshape(N_PROMPT, D_MODEL), x_sample.reshape(N_SAMPLE, D_MODEL))
    cond8 = jnp.concatenate([c_ctx[None, :], c, jnp.zeros((8 - 1 - N_SAMPLE_BATCH, D_MODEL), f32)], axis=0)
    mod = _adaln(cond8, w_mod, b_mod).reshape(DEPTH, 8, 6, D_MODEL)
    rope_tabs = _rope_tables()
    new_kv = {}
    for layer in range(DEPTH):
        li = layer // 2
        mod_l = mod[layer]
        g_mix = norm_mix[layer][None, :]
        g_ffn = norm_ffn[layer][None, :]
        if layer % 2 == 0:
            w_in = _permute_w_in_ab(w_in_ab[li]).astype(bf16)
            qkv, ak, av, bk, bv = _inproj(x, mod_l, g_mix, w_in, rope_tabs, _L0_CHUNKS, _L0_KV_OUTS)
            new_kv["a_k"], new_kv["a_v"], new_kv["b_k"], new_kv["b_v"] = ak, av, bk, bv
            o_p = _ctx0(sink_a[li], qkv)
            o_s = _lat0(sink_a[li], qkv,
                        cache_a_k[:, li].reshape(N_SAMPLE_BATCH, PAST_LEN, LANES),
                        cache_a_v[:, li].reshape(N_SAMPLE_BATCH, PAST_LEN, LANES),
                        cache_b_k[:, li].reshape(N_SAMPLE_BATCH, PAST_LEN, 512),
                        cache_b_v[:, li].reshape(N_SAMPLE_BATCH, PAST_LEN, 512),
                        _na_bias_tiles(rel_bias_b[li]))
            w_out = w_out_ab[li].astype(bf16)
        else:
            lam_init = 0.8 - 0.6 * math.exp(-0.3 * layer)
            qkv, ck, cv = _inproj(x, mod_l, g_mix, w_in_c[li].astype(bf16), rope_tabs, _L1_CHUNKS, _L1_KV_OUTS)
            new_kv["c_k"], new_kv["c_v"] = ck, cv
            lamv = jnp.concatenate([lam_q1[li][None], lam_k1[li][None], lam_q2[li][None], lam_k2[li][None],
                                    jnp.zeros((4, HEAD_DIM), f32)], axis=0)
            subln = subln_c[li][None, :]
            o_p = _ctx1(lamv, subln, qkv, lam_init)
            o_s = _lat1(lamv, subln, qkv,
                        cache_c_k[:, li].reshape(N_SAMPLE_BATCH, PAST_LEN, D_MODEL),
                        cache_c_v[:, li].reshape(N_SAMPLE_BATCH, PAST_LEN, D_MODEL), lam_init)
            w_out = w_out_c[li].astype(bf16)
        x = _outproj(x, o_p, o_s, mod_l, w_out)
        last = layer == DEPTH - 1
        x = _moe(x, mod_l, g_ffn, router_w[layer].T, router_bias[layer][:, None],
                 exp_w_gate, exp_w_up, exp_w_down, layer,
                 sh_w_gate[layer].astype(bf16), sh_w_up[layer].astype(bf16), sh_w_down[layer].astype(bf16),
                 final_norm[None, :] if last else None)
        x = x if last else (x,)
    y_prompt, y_sample = x
    nb, s = N_PROMPT_BATCH, PROMPT_SEQ
    return (y_prompt.reshape(nb, s, D_MODEL), y_sample.reshape(N_SAMPLE_BATCH, SAMPLE_SEQ, D_MODEL),
            _from_feature_major(new_kv["a_k"], 2, HEAD_DIM), _from_feature_major(new_kv["a_v"], 2, HEAD_DIM),
            _from_feature_major(new_kv["b_k"], 8, HEAD_DIM), _from_feature_major(new_kv["b_v"], 8, HEAD_DIM),
            _from_feature_major(new_kv["c_k"], 8, 2, HEAD_DIM), new_kv["c_v"].reshape(nb, 1, s, 8, 2 * HEAD_DIM))
```

```python
import functools
import math

import jax
import jax.numpy as jnp
import numpy as np
from jax import lax
from jax.experimental import pallas as pl
from jax.experimental.pallas import tpu as pltpu

f32 = jnp.float32
bf16 = jnp.bfloat16
i32 = jnp.int32

D_MODEL = 1024
N_PROMPT_BATCH = 16
PROMPT_SEQ = 256
DEPTH = 2
N_SAMPLE_BATCH = 2
SAMPLE_SEQ = 2048
PAST_LEN = 512
GRID_W = 64
HEAD_DIM = 64
ROPE_THETA = 10000.0
EPS = 1e-6
A_WINDOW = 128
NA_ROWS = 8
NA_COLS = 16
N_EXPERTS = 64
TOP_K = 6
N_GROUPS = 8
TOPK_GROUPS = 4
EXPERT_DIM = 256
ROUTED_SCALE = 2.5
Q_SCALE = HEAD_DIM ** -0.5

N_PROMPT = N_PROMPT_BATCH * PROMPT_SEQ
N_SAMPLE = N_SAMPLE_BATCH * SAMPLE_SEQ
N_TOK = N_PROMPT + N_SAMPLE

LANES = 128
TM = 256
TD = 512
N_PROMPT_TILES = N_PROMPT // TM
N_TILES = N_TOK // TM
QB = 128
CHUNK = 16
SLOTS = -(-(TM * TOP_K + N_EXPERTS * (CHUNK - 1)) // 256) * 256
SLOT_CHUNKS = SLOTS // CHUNK
ROUTE_ROWS = 512
XW = D_MODEL + LANES
GM = 256
_MAX_SORTED = TM * TOP_K * N_TILES + N_TILES * N_EXPERTS * (CHUNK - 1) + N_EXPERTS * (GM - CHUNK)
G_TILES = -(-_MAX_SORTED // GM)
G_CHUNKS = GM // CHUNK
GMM_ITEM_TILES = 2
GMM_SLOTS = 3
GMM_MAX_ITEMS = (G_TILES + N_EXPERTS * (GMM_ITEM_TILES - 1)) // GMM_ITEM_TILES
VMEM_LIMIT = 56 * 1024 * 1024

NEG = -1e30


def _cparams(sem):
    return pltpu.CompilerParams(dimension_semantics=sem, vmem_limit_bytes=VMEM_LIMIT)


def _mod_row(i, tm=TM):
    return jnp.where(i < N_PROMPT // tm, 0, 1 + (i - N_PROMPT // tm) // (SAMPLE_SEQ // tm))


def _prompt_block(i, tm=TM):
    return (jnp.minimum(i, N_PROMPT // tm - 1), 0)


def _sample_block(i, tm=TM):
    return (jnp.maximum(i - N_PROMPT // tm, 0), 0)


def _x_specs(parts, tm=TM):
    if len(parts) == 1:
        return [pl.BlockSpec((tm, D_MODEL), lambda i, *_: (i, 0))]
    return [pl.BlockSpec((tm, D_MODEL), lambda i, *_: _prompt_block(i, tm)),
            pl.BlockSpec((tm, D_MODEL), lambda i, *_: _sample_block(i, tm))]


def _load_x(i, x_refs, tm=TM):
    if len(x_refs) == 1:
        return x_refs[0][...]
    return jnp.where(i < N_PROMPT // tm, x_refs[0][...], x_refs[1][...])


def _norm_mod(x, g, scale, shift):
    y = x * lax.rsqrt(jnp.mean(x * x, axis=-1, keepdims=True) + EPS)
    return (y * g) * (1.0 + scale) + shift


def _silu(x):
    return x * jax.nn.sigmoid(x)


def _dot(a, b):
    return jnp.dot(a, b, preferred_element_type=f32)


def _dot_nt(a, b):
    return lax.dot_general(a, b, (((1,), (1,)), ((), ())), preferred_element_type=f32)


ADA_COLS = 1536


def _adaln_kernel(cond_ref, w_ref, b_ref, o_ref):
    s = _silu(cond_ref[...]).astype(bf16)
    o_ref[...] = _dot(s, w_ref[...].astype(bf16)) + b_ref[...]


def _adaln(cond8, w_mod, b_mod):
    n6 = 6 * D_MODEL
    return pl.pallas_call(
        _adaln_kernel,
        grid=(DEPTH, n6 // ADA_COLS),
        in_specs=[
            pl.BlockSpec((8, D_MODEL), lambda l, j: (0, 0)),
            pl.BlockSpec((None, D_MODEL, ADA_COLS), lambda l, j: (l, 0, j)),
            pl.BlockSpec((None, 1, ADA_COLS), lambda l, j: (l, 0, j)),
        ],
        out_specs=pl.BlockSpec((None, 8, ADA_COLS), lambda l, j: (l, 0, j)),
        out_shape=jax.ShapeDtypeStruct((DEPTH, 8, n6), f32),
        compiler_params=_cparams(("parallel", "parallel")),
        name="adaln",
    )(cond8, w_mod, b_mod.reshape(DEPTH, 1, n6))


def _rope_block(blk, cos, sin_a, sin_b):
    return blk * cos + pltpu.roll(blk, LANES - 16, 1) * sin_a + pltpu.roll(blk, 16, 1) * sin_b


def _inproj_kernel(*refs, chunks, n_x):
    x_refs, kv_refs = refs[:n_x], refs[n_x + 7:]
    mod_ref, g_ref, w_ref, cos_ref, sa_ref, sb_ref, qkv_ref = refs[n_x:n_x + 7]
    i = pl.program_id(0)
    h = _norm_mod(_load_x(i, x_refs, TD), g_ref[...], mod_ref[1:2, :], mod_ref[0:1, :]).astype(bf16)
    is_prompt = i < N_PROMPT // TD

    @pl.when(is_prompt)
    def _():
        for c0, c1, s0, _, kv_out in chunks:
            acc = _dot(h, w_ref[:, s0:s0 + (c1 - c0)])
            qkv_ref[:, c0:c1] = acc.astype(bf16)
            for ridx, a0, a1, o0, per_head in kv_out:
                if per_head:
                    heads = kv_refs[ridx].shape[0] // TD
                    for j in range((a1 - a0) // LANES):
                        kv_refs[ridx][pl.ds(o0 // LANES + j, TD, stride=heads), :] = (
                            acc[:, a0 + j * LANES:a0 + (j + 1) * LANES])
                else:
                    t = acc[:, a0:a1].T
                    for b in range(TD // PROMPT_SEQ):
                        kv_refs[ridx][b, o0:o0 + (a1 - a0), :] = t[:, b * PROMPT_SEQ:(b + 1) * PROMPT_SEQ]

    @pl.when(jnp.logical_not(is_prompt))
    def _():
        cos, sa, sb = cos_ref[...], sa_ref[...], sb_ref[...]
        for c0, c1, s0, rope_blocks, _ in chunks:
            acc = _dot(h, w_ref[:, s0:s0 + (c1 - c0)])
            for b in range((c1 - c0) // LANES):
                blk = acc[:, b * LANES:(b + 1) * LANES]
                if b in rope_blocks:
                    blk = _rope_block(blk, cos, sa, sb)
                qkv_ref[:, c0 + b * LANES:c0 + (b + 1) * LANES] = blk.astype(bf16)


def _inproj(x_parts, mod_l, g, w, rope_tabs, chunks, kv_outs):
    n = w.shape[1]
    cos, sa, sb = rope_tabs
    bpt = TD // PROMPT_SEQ

    def rope_idx(i):
        return (jnp.where(i < N_PROMPT // TD, 0, (i - N_PROMPT // TD) % (SAMPLE_SEQ // TD)), 0)

    kv_specs, kv_shapes = [], []
    for kind, size in kv_outs:
        if kind == "T":
            kv_specs.append(pl.BlockSpec((bpt, size, PROMPT_SEQ), lambda i: _prompt_block(i, TD) + (0,)))
            kv_shapes.append(jax.ShapeDtypeStruct((N_PROMPT_BATCH, size, PROMPT_SEQ), f32))
        else:
            kv_specs.append(pl.BlockSpec((TD * size, LANES), lambda i: _prompt_block(i, TD)))
            kv_shapes.append(jax.ShapeDtypeStruct((N_PROMPT * size, LANES), f32))

    return pl.pallas_call(
        functools.partial(_inproj_kernel, chunks=chunks, n_x=len(x_parts)),
        grid=(N_TOK // TD,),
        in_specs=_x_specs(x_parts, TD) + [
            pl.BlockSpec((None, 6, D_MODEL), lambda i: (_mod_row(i, TD), 0, 0)),
            pl.BlockSpec((1, D_MODEL), lambda i: (0, 0)),
            pl.BlockSpec((D_MODEL, n), lambda i: (0, 0)),
            pl.BlockSpec((TD, LANES), rope_idx),
            pl.BlockSpec((TD, LANES), rope_idx),
            pl.BlockSpec((TD, LANES), rope_idx),
        ],
        out_specs=[pl.BlockSpec((TD, n), lambda i: (i, 0))] + kv_specs,
        out_shape=[jax.ShapeDtypeStruct((N_TOK, n), bf16)] + kv_shapes,
        compiler_params=_cparams(("arbitrary",)),
        name="inproj",
    )(*x_parts, mod_l, g, w, cos, sa, sb)


def _rope_tables():
    nq = HEAD_DIM // 4
    t = np.arange(SAMPLE_SEQ)
    inv = np.power(np.float32(ROPE_THETA), -np.arange(nq, dtype=np.float32) / np.float32(nq))
    ang_r = (t // GRID_W).astype(np.float32)[:, None] * inv
    ang_c = (t % GRID_W).astype(np.float32)[:, None] * inv
    zero = np.zeros_like(ang_r)

    def head(fr, fc):
        return np.concatenate([fr[0], fr[1], fc[0], fc[1]], axis=-1)

    cos = head((np.cos(ang_r), np.cos(ang_r)), (np.cos(ang_c), np.cos(ang_c)))
    sin_a = head((-np.sin(ang_r), zero), (-np.sin(ang_c), zero))
    sin_b = head((zero, np.sin(ang_r)), (zero, np.sin(ang_c)))
    two = lambda a: jnp.asarray(np.concatenate([a, a], axis=-1).astype(np.float32))
    return two(cos), two(sin_a), two(sin_b)


def _lane_lo(shape):
    return lax.broadcasted_iota(i32, shape, len(shape) - 1) < HEAD_DIM


def _half(q, lo_mask, half):
    keep = lo_mask if half == 0 else jnp.logical_not(lo_mask)
    return jnp.where(keep, q, jnp.zeros_like(q)) * Q_SCALE


def _swap_halves(x):
    return pltpu.roll(x.astype(f32), HEAD_DIM, 1).astype(x.dtype)


def _stack_halves(q, lo_mask):
    return jnp.concatenate([_half(q, lo_mask, 0), _half(q, lo_mask, 1)], axis=0)


def _with_ones(v):
    return jnp.concatenate([v, jnp.ones_like(v)], axis=1)


def _attend(q_rows, n_heads, key_blocks, vx_blocks, fix_scores=None, sinks=None):
    r = q_rows.shape[0] // n_heads
    scores = [_dot_nt(q_rows, k) for k in key_blocks]
    exps = [[] for _ in key_blocks]
    maxes = []
    for h in range(n_heads):
        blocks = [s[h * r:(h + 1) * r] for s in scores]
        if fix_scores is not None:
            blocks = [fix_scores(h, i, s) for i, s in enumerate(blocks)]
        m = functools.reduce(jnp.maximum, [jnp.max(s, axis=-1, keepdims=True) for s in blocks])
        if sinks is not None:
            m = jnp.maximum(m, sinks[h])
        maxes.append(m)
        for i, s in enumerate(blocks):
            exps[i].append(jnp.exp((s - m).astype(bf16)))
    out = functools.reduce(lambda a, b: a + b,
                           [_dot(e[0] if n_heads == 1 else jnp.concatenate(e, axis=0), vx)
                            for e, vx in zip(exps, vx_blocks)])
    outs = []
    for h in range(n_heads):
        den = out[h * r:(h + 1) * r, LANES:]
        if sinks is not None:
            den = den + jnp.exp(sinks[h] - maxes[h])
        outs.append(out[h * r:(h + 1) * r, :LANES] * (1.0 / den))
    return outs


def _gqa_rows(q_blocks, group, lo_mask):
    parts = []
    for q in q_blocks:
        for half in range(2):
            qh = _half(q, lo_mask, half)
            parts.append(qh if half == group else _swap_halves(qh))
    return jnp.concatenate(parts, axis=0)


def _gqa_merge(outs, group, lo_mask):
    fixed = [o if idx % 2 == group else pltpu.roll(o, HEAD_DIM, 1) for idx, o in enumerate(outs)]
    return [jnp.where(lo_mask, fixed[2 * p], fixed[2 * p + 1]) for p in range(len(outs) // 2)]


L0_QA, L0_QB, L0_KB, L0_VB, L0_KA, L0_VA, L0_N = 0, 512, 1024, 1536, 2048, 2176, 2304


def _ctx0_kernel(sink_ref, qkv_ref, o_ref):
    lo = _lane_lo((1, LANES))
    blk = lambda base, j: qkv_ref[:, base + j * LANES:base + (j + 1) * LANES]
    k_a = blk(L0_KA, 0)
    vx_a = _with_ones(blk(L0_VA, 0))
    for g in range(2):
        q_rows = _gqa_rows([blk(L0_QA, 2 * g), blk(L0_QA, 2 * g + 1)], g, lo)
        outs = _attend(q_rows, 4, [k_a], [vx_a], sinks=[sink_ref[4 * g + idx] for idx in range(4)])
        for p, o in enumerate(_gqa_merge(outs, g, lo)):
            j = 2 * g + p
            o_ref[:, j * LANES:(j + 1) * LANES] = o.astype(bf16)
    for j in range(4):
        outs = _attend(_stack_halves(blk(L0_QB, j), lo), 2, [blk(L0_KB, j)], [_with_ones(blk(L0_VB, j))])
        o_ref[:, 512 + j * LANES:512 + (j + 1) * LANES] = jnp.where(lo, outs[0], outs[1]).astype(bf16)


def _ctx0(sink, qkv):
    return pl.pallas_call(
        _ctx0_kernel,
        grid=(N_PROMPT_BATCH,),
        in_specs=[
            pl.BlockSpec(memory_space=pltpu.SMEM),
            pl.BlockSpec((PROMPT_SEQ, L0_N), lambda b: (b, 0)),
        ],
        out_specs=pl.BlockSpec((PROMPT_SEQ, D_MODEL), lambda b: (b, 0)),
        out_shape=jax.ShapeDtypeStruct((N_PROMPT, D_MODEL), bf16),
        compiler_params=_cparams(("parallel",)),
        name="ctx0",
    )(sink, qkv)


WIN_KEYS = 3 * QB
NA_KEY_ROWS = 10
NA_KEYS = NA_KEY_ROWS * GRID_W
N_QB = SAMPLE_SEQ // QB
N_NA_PATTERNS = 5
_PROMPT_QBLOCKS = N_PROMPT // QB


def _na_pattern(n):
    return jnp.where(n < 2, n, jnp.where(n > N_QB - 3, n - (N_QB - 5), 2))


def _lat0_kernel(sink_ref, roff_ref, q_ref, kvb_ref, kva_ref, cak_ref, cav_ref, cbk_ref, cbv_ref, tiles_ref, o_ref):
    n = pl.program_id(1)
    lo = _lane_lo((1, LANES))
    kstart = pl.multiple_of(jnp.clip((n - 1) * QB, 0, SAMPLE_SEQ - WIN_KEYS), QB)
    k_a = kva_ref[pl.ds(kstart, WIN_KEYS), 0:LANES]
    v_a = kva_ref[pl.ds(kstart, WIN_KEYS), LANES:2 * LANES]
    c_k = cak_ref[...].astype(bf16)
    keys_a = [c_k, k_a]
    vx_a = [_with_ones(cav_ref[...].astype(bf16)), _with_ones(v_a)]
    qpos = n * QB + lax.broadcasted_iota(i32, (QB, WIN_KEYS), 0)
    kpos = kstart + lax.broadcasted_iota(i32, (QB, WIN_KEYS), 1)
    in_window = jnp.abs(qpos - kpos) <= A_WINDOW
    mask_window = lambda h, i, s: jnp.where(in_window, s, NEG) if i == 1 else s
    for g in range(2):
        q_rows = _gqa_rows([q_ref[:, L0_QA + j * LANES:L0_QA + (j + 1) * LANES] for j in (2 * g, 2 * g + 1)], g, lo)
        outs = _attend(q_rows, 4, keys_a, vx_a, fix_scores=mask_window,
                       sinks=[sink_ref[4 * g + idx] for idx in range(4)])
        for p, o in enumerate(_gqa_merge(outs, g, lo)):
            j = 2 * g + p
            o_ref[:, j * LANES:(j + 1) * LANES] = o.astype(bf16)
    krow = jnp.clip(2 * n - NA_ROWS // 2, 0, SAMPLE_SEQ // GRID_W - NA_KEY_ROWS)
    ktok = pl.multiple_of(krow * GRID_W, QB)
    pattern = _na_pattern(n)

    def na_bias(head):
        rows = []
        for rq in range(QB // GRID_W):
            blocks = []
            for kb in range(NA_KEY_ROWS // 2):
                d0, d1 = (roff_ref[(pattern * 2 + rq) * NA_KEY_ROWS + 2 * kb + t] for t in range(2))
                blocks.append(jnp.where(lo, tiles_ref[head, d0], tiles_ref[head, d1]))
            rows.append(jnp.concatenate(blocks, axis=1))
        return jnp.concatenate(rows, axis=0)

    for j in range(4):
        q_b = q_ref[:, L0_QB + j * LANES:L0_QB + (j + 1) * LANES]
        k_b = kvb_ref[pl.ds(ktok, NA_KEYS), j * LANES:(j + 1) * LANES]
        v_b = kvb_ref[pl.ds(ktok, NA_KEYS), 512 + j * LANES:512 + (j + 1) * LANES]
        cb_k = cbk_ref[:, j * LANES:(j + 1) * LANES].astype(bf16)
        cb_v = cbv_ref[:, j * LANES:(j + 1) * LANES].astype(bf16)
        add_bias = lambda h, i, s, j=j: s + na_bias(2 * j + h) if i == 1 else s
        outs = _attend(_stack_halves(q_b, lo), 2, [cb_k, k_b], [_with_ones(cb_v), _with_ones(v_b)],
                       fix_scores=add_bias)
        o_ref[:, 512 + j * LANES:512 + (j + 1) * LANES] = jnp.where(lo, outs[0], outs[1]).astype(bf16)


def _lat0(sink, qkv, cak, cav, cbk, cbv, bias_tiles):
    sb = N_PROMPT // SAMPLE_SEQ
    return pl.pallas_call(
        _lat0_kernel,
        grid=(N_SAMPLE_BATCH, N_QB),
        in_specs=[
            pl.BlockSpec(memory_space=pltpu.SMEM),
            pl.BlockSpec(memory_space=pltpu.SMEM),
            pl.BlockSpec((QB, 1024), lambda b, n: (_PROMPT_QBLOCKS + b * N_QB + n, 0)),
            pl.BlockSpec((SAMPLE_SEQ, 1024), lambda b, n: (sb + b, 1)),
            pl.BlockSpec((SAMPLE_SEQ, 256), lambda b, n: (sb + b, L0_KA // 256)),
            pl.BlockSpec((None, PAST_LEN, LANES), lambda b, n: (b, 0, 0)),
            pl.BlockSpec((None, PAST_LEN, LANES), lambda b, n: (b, 0, 0)),
            pl.BlockSpec((None, PAST_LEN, 512), lambda b, n: (b, 0, 0)),
            pl.BlockSpec((None, PAST_LEN, 512), lambda b, n: (b, 0, 0)),
            pl.BlockSpec((8, N_ROW_OFFSETS + 1, GRID_W, LANES), lambda b, n: (0, 0, 0, 0)),
        ],
        out_specs=pl.BlockSpec((QB, D_MODEL), lambda b, n: (b * N_QB + n, 0)),
        out_shape=jax.ShapeDtypeStruct((N_SAMPLE, D_MODEL), bf16),
        compiler_params=_cparams(("parallel", "arbitrary")),
        name="lat0",
    )(sink, jnp.asarray(_na_row_offsets()), qkv, qkv, qkv, cak, cav, cbk, cbv, bias_tiles)


N_ROW_OFFSETS = 2 * NA_ROWS - 1


def _na_row_offsets():
    rows = SAMPLE_SEQ // GRID_W
    idx = np.full((N_NA_PATTERNS, 2, NA_KEY_ROWS), N_ROW_OFFSETS, np.int32)
    for p, n in enumerate((0, 1, 2, N_QB - 2, N_QB - 1)):
        k0 = int(np.clip(2 * n - NA_ROWS // 2, 0, rows - NA_KEY_ROWS))
        for rq in range(2):
            r = 2 * n + rq
            rs = int(np.clip(r - NA_ROWS // 2, 0, rows - NA_ROWS))
            for kl in range(NA_KEY_ROWS):
                if rs <= k0 + kl < rs + NA_ROWS:
                    idx[p, rq, kl] = k0 + kl - r + NA_ROWS - 1
    return idx.reshape(-1)


def _na_bias_tiles(rel_bias):
    n_dc = 2 * NA_COLS - 1
    c = np.arange(GRID_W)[:, None]
    kc = np.arange(GRID_W)[None, :]
    cs = np.clip(c - NA_COLS // 2, 0, GRID_W - NA_COLS)
    col_ok = (kc >= cs) & (kc < cs + NA_COLS)
    col_hot = ((kc - c + NA_COLS - 1)[None] == np.arange(n_dc)[:, None, None]) & col_ok[None]
    hp = lax.Precision.HIGHEST
    tiles = jnp.einsum("hdx,xck->hdck", rel_bias.astype(f32), col_hot.astype(np.float32), precision=hp)
    tiles = tiles + np.where(col_ok, 0.0, NEG).astype(np.float32)
    tiles = jnp.concatenate([tiles, jnp.full((tiles.shape[0], 1, GRID_W, GRID_W), NEG, f32)], axis=1)
    return jnp.concatenate([tiles, tiles], axis=-1)


def _diff_lambda(lam_ref, lam_init):
    lv = lam_ref[...]
    s1 = jnp.sum(lv[0:1, :] * lv[1:2, :], axis=-1, keepdims=True)
    s2 = jnp.sum(lv[2:3, :] * lv[3:4, :], axis=-1, keepdims=True)
    return jnp.exp(s1) - jnp.exp(s2) + lam_init


def _diff_head(q, key_blocks, value_blocks, lam, subln, lo, lam_init):
    o1, o2 = _attend(_stack_halves(q, lo), 2, key_blocks, [_with_ones(v) for v in value_blocks])
    o = o1 - lam * o2
    o = o * lax.rsqrt(jnp.mean(o * o, axis=-1, keepdims=True) + EPS)
    return (o * subln) * (1.0 - lam_init)


def _ctx1_kernel(lam_ref, subln_ref, qkv_ref, o_ref, *, lam_init):
    lo = _lane_lo((1, LANES))
    lam = _diff_lambda(lam_ref, lam_init)
    subln = subln_ref[...]
    for h in range(8):
        q = qkv_ref[:, h * LANES:(h + 1) * LANES]
        k = qkv_ref[:, 1024 + h * LANES:1024 + (h + 1) * LANES]
        v = qkv_ref[:, 2048 + h * LANES:2048 + (h + 1) * LANES]
        o_ref[:, h * LANES:(h + 1) * LANES] = _diff_head(q, [k], [v], lam, subln, lo, lam_init).astype(bf16)


def _ctx1(lamv, subln, qkv, lam_init):
    return pl.pallas_call(
        functools.partial(_ctx1_kernel, lam_init=lam_init),
        grid=(N_PROMPT_BATCH,),
        in_specs=[
            pl.BlockSpec((8, HEAD_DIM), lambda b: (0, 0)),
            pl.BlockSpec((1, LANES), lambda b: (0, 0)),
            pl.BlockSpec((PROMPT_SEQ, 3 * D_MODEL), lambda b: (b, 0)),
        ],
        out_specs=pl.BlockSpec((PROMPT_SEQ, D_MODEL), lambda b: (b, 0)),
        out_shape=jax.ShapeDtypeStruct((N_PROMPT, D_MODEL), bf16),
        compiler_params=_cparams(("parallel",)),
        name="ctx1",
    )(lamv, subln, qkv)


def _lat1_kernel(lam_ref, subln_ref, q_ref, k_ref, v_ref, ck_ref, cv_ref, o_ref, *, lam_init):
    lo = _lane_lo((1, LANES))
    lam = _diff_lambda(lam_ref, lam_init)
    subln = subln_ref[...]
    for h in range(8):
        sl = slice(h * LANES, (h + 1) * LANES)
        o_ref[:, sl] = _diff_head(q_ref[:, sl], [ck_ref[:, sl].astype(bf16), k_ref[:, sl]],
                                  [cv_ref[:, sl].astype(bf16), v_ref[:, sl]],
                                  lam, subln, lo, lam_init).astype(bf16)


def _lat1(lamv, subln, qkv, ck, cv, lam_init):
    sb = N_PROMPT // SAMPLE_SEQ
    nq = SAMPLE_SEQ // TM
    return pl.pallas_call(
        functools.partial(_lat1_kernel, lam_init=lam_init),
        grid=(N_SAMPLE_BATCH, nq),
        in_specs=[
            pl.BlockSpec((8, HEAD_DIM), lambda b, n: (0, 0)),
            pl.BlockSpec((1, LANES), lambda b, n: (0, 0)),
            pl.BlockSpec((TM, D_MODEL), lambda b, n: (N_PROMPT_TILES + b * nq + n, 0)),
            pl.BlockSpec((SAMPLE_SEQ, D_MODEL), lambda b, n: (sb + b, 1)),
            pl.BlockSpec((SAMPLE_SEQ, D_MODEL), lambda b, n: (sb + b, 2)),
            pl.BlockSpec((None, PAST_LEN, D_MODEL), lambda b, n: (b, 0, 0)),
            pl.BlockSpec((None, PAST_LEN, D_MODEL), lambda b, n: (b, 0, 0)),
        ],
        out_specs=pl.BlockSpec((TM, D_MODEL), lambda b, n: (b * nq + n, 0)),
        out_shape=jax.ShapeDtypeStruct((N_SAMPLE, D_MODEL), bf16),
        compiler_params=_cparams(("parallel", "arbitrary")),
        name="lat1",
    )(lamv, subln, qkv, qkv, qkv, ck, cv)


def _outproj_kernel(*refs, n_x):
    x_refs = refs[:n_x]
    op_ref, os_ref, mod_ref, w_ref, o_ref = refs[n_x:]
    i = pl.program_id(0)
    attn = jnp.where(i < N_PROMPT // TD, op_ref[...], os_ref[...])
    o_ref[...] = _load_x(i, x_refs, TD) + mod_ref[2:3, :] * _dot(attn, w_ref[...])


def _outproj(x_parts, o_prompt, o_sample, mod_l, w):
    return pl.pallas_call(
        functools.partial(_outproj_kernel, n_x=len(x_parts)),
        grid=(N_TOK // TD,),
        in_specs=_x_specs(x_parts, TD) + [
            pl.BlockSpec((TD, D_MODEL), lambda i: _prompt_block(i, TD)),
            pl.BlockSpec((TD, D_MODEL), lambda i: _sample_block(i, TD)),
            pl.BlockSpec((None, 6, D_MODEL), lambda i: (_mod_row(i, TD), 0, 0)),
            pl.BlockSpec((D_MODEL, D_MODEL), lambda i: (0, 0)),
        ],
        out_specs=pl.BlockSpec((TD, D_MODEL), lambda i: (i, 0)),
        out_shape=jax.ShapeDtypeStruct((N_TOK, D_MODEL), f32),
        compiler_params=_cparams(("parallel",)),
        name="outproj",
    )(*x_parts, o_prompt, o_sample, mod_l, w)


def _split_bf16(a):
    hi = a.astype(bf16)
    return hi, (a - hi.astype(f32)).astype(bf16)


def _route_kernel(x_ref, mod_ref, g_ref, rwt_ref, rb_ref, xloc_ref, slots_ref, len_ref):
    ng, ge = N_GROUPS, N_EXPERTS // N_GROUPS
    h = _norm_mod(x_ref[...], g_ref[...], mod_ref[4:5, :], mod_ref[3:4, :])
    h_hi, h_lo = _split_bf16(h)
    w_hi, w_lo = _split_bf16(rwt_ref[...])
    logits = _dot_nt(w_hi, h_hi) + (_dot_nt(w_hi, h_lo) + _dot_nt(w_lo, h_hi))
    scores = jax.nn.sigmoid(logits)
    biased = scores + rb_ref[...]
    s3 = scores.reshape(ng, ge, TM)
    b3 = biased.reshape(ng, ge, TM)
    in_group = lax.broadcasted_iota(i32, (ng, ge, TM), 1).astype(f32)
    group_id = lax.broadcasted_iota(i32, (ng, 1, TM), 0).astype(f32)
    expert_id = lax.broadcasted_iota(i32, (ng, ge, TM), 0).astype(f32) * ge + in_group

    def max01(a):
        return jnp.max(jnp.max(a, axis=0, keepdims=True), axis=1, keepdims=True)

    def min01(a):
        return jnp.min(jnp.min(a, axis=0, keepdims=True), axis=1, keepdims=True)

    def sum01(a):
        return jnp.sum(jnp.sum(a, axis=0, keepdims=True), axis=1, keepdims=True)

    m1 = jnp.max(b3, axis=1, keepdims=True)
    first = jnp.min(jnp.where(b3 == m1, in_group, ge), axis=1, keepdims=True)
    m2 = jnp.max(jnp.where(in_group == first, -jnp.inf, b3), axis=1, keepdims=True)
    gscore = m1 + m2
    gsel = jnp.zeros((ng, 1, TM), f32)
    for _ in range(TOPK_GROUPS):
        gm = jnp.max(gscore, axis=0, keepdims=True)
        gi = jnp.min(jnp.where(gscore == gm, group_id, ng), axis=0, keepdims=True)
        hit = group_id == gi
        gsel = jnp.where(hit, 1.0, gsel)
        gscore = jnp.where(hit, -jnp.inf, gscore)
    cand = jnp.where(jnp.broadcast_to(gsel, (ng, ge, TM)) > 0.0, b3, -jnp.inf)
    top_e, top_w = [], []
    for _ in range(TOP_K):
        em = max01(cand)
        ei = min01(jnp.where(cand == em, expert_id, N_EXPERTS))
        hit = expert_id == ei
        top_e.append(ei)
        top_w.append(sum01(jnp.where(hit, s3, 0.0)))
        cand = jnp.where(hit, -jnp.inf, cand)
    wsum = functools.reduce(lambda a, b: a + b, top_w)
    gates3 = jnp.zeros((ng, ge, TM), f32)
    sel3 = jnp.zeros((ng, ge, TM), f32)
    for ei, w in zip(top_e, top_w):
        hit = expert_id == ei
        gates3 = jnp.where(hit, w / wsum * ROUTED_SCALE, gates3)
        sel3 = jnp.where(hit, 1.0, sel3)
    gates = gates3.reshape(N_EXPERTS, TM)
    sel = sel3.reshape(N_EXPERTS, TM)

    cnt = jnp.sum(sel, axis=1, keepdims=True)
    run_len = jnp.ceil(cnt * (1.0 / CHUNK)) * CHUNK
    r_i = lax.broadcasted_iota(i32, (N_EXPERTS, N_EXPERTS), 0)
    c_i = lax.broadcasted_iota(i32, (N_EXPERTS, N_EXPERTS), 1)
    lower = jnp.where(c_i < r_i, 1.0, 0.0).astype(bf16)
    run_off = _dot(lower, jnp.broadcast_to(run_len, (N_EXPERTS, LANES)).astype(bf16))[:, 0:1]
    t_r = lax.broadcasted_iota(i32, (TM, TM), 0)
    t_c = lax.broadcasted_iota(i32, (TM, TM), 1)
    before = jnp.where(t_r < t_c, 1.0, 0.0).astype(bf16)
    rank = _dot(sel.astype(bf16), before)
    slot3 = (run_off + rank).reshape(ng, ge, TM)
    slots = [sum01(jnp.where(expert_id == ei, slot3, 0.0)).reshape(1, TM).astype(i32) for ei in top_e]
    for k in range(TOP_K):
        slots_ref[k:k + 1, :] = slots[k]
    slots_ref[TOP_K:8, :] = jnp.full((8 - TOP_K, TM), -1, i32)
    len_ref[...] = jnp.broadcast_to(run_len, (N_EXPERTS, LANES)).astype(i32)

    g_parts = jnp.concatenate(_split_bf16(gates), axis=0)
    rows = ROUTE_ROWS

    def body(c, carry):
        base = pl.multiple_of(c * rows, rows)
        row_id = base + lax.broadcasted_iota(i32, (rows, TM), 0)
        p = jnp.zeros((rows, TM), f32)
        for k in range(TOP_K):
            p = jnp.where(row_id == slots[k], 1.0, p)
        p = p.astype(bf16)
        xloc_ref[pl.ds(base, rows), 0:D_MODEL] = _dot(p, h_hi).astype(bf16)
        xloc_ref[pl.ds(base, rows), D_MODEL:XW] = _dot_nt(p, g_parts).astype(bf16)
        return carry

    def zero_body(c, carry):
        base = pl.multiple_of(c * rows, rows)
        xloc_ref[pl.ds(base, rows), :] = jnp.zeros((rows, XW), bf16)
        return carry

    n_used = (jnp.sum(run_len).astype(i32) + (rows - 1)) // rows
    lax.fori_loop(0, n_used, body, 0)
    lax.fori_loop(n_used, SLOTS // rows, zero_body, 0)


def _route(x, mod_l, g, rwt, rb):
    return pl.pallas_call(
        _route_kernel,
        grid=(N_TILES,),
        in_specs=[
            pl.BlockSpec((TM, D_MODEL), lambda i: (i, 0)),
            pl.BlockSpec((None, 6, D_MODEL), lambda i: (_mod_row(i), 0, 0)),
            pl.BlockSpec((1, D_MODEL), lambda i: (0, 0)),
            pl.BlockSpec((N_EXPERTS, D_MODEL), lambda i: (0, 0)),
            pl.BlockSpec((N_EXPERTS, 1), lambda i: (0, 0)),
        ],
        out_specs=[
            pl.BlockSpec((SLOTS, XW), lambda i: (i, 0)),
            pl.BlockSpec((None, 8, TM), lambda i: (i, 0, 0)),
            pl.BlockSpec((None, N_EXPERTS, LANES), lambda i: (i, 0, 0)),
        ],
        out_shape=[
            jax.ShapeDtypeStruct((N_TILES * SLOTS, XW), bf16),
            jax.ShapeDtypeStruct((N_TILES, 8, TM), i32),
            jax.ShapeDtypeStruct((N_TILES, N_EXPERTS, LANES), i32),
        ],
        compiler_params=_cparams(("parallel",)),
        name="route",
    )(x, mod_l, g, rwt, rb)


def _moe_plan(run_len):
    nt, ne = run_len.shape

    def excl_cumsum(a):
        n = a.shape[-1]
        earlier = np.arange(n)[None, :] < np.arange(n)[:, None]
        return jnp.sum(jnp.where(earlier, a[..., None, :], 0), axis=-1)

    def first_diff(a):
        return a - jnp.concatenate([jnp.zeros_like(a[..., :1]), a[..., :-1]], axis=-1)

    off_loc = excl_cumsum(run_len)
    before = excl_cumsum(run_len.T).T
    n_e = jnp.sum(run_len, axis=0)
    n_pad = -(-n_e // GM) * GM
    g_start = excl_cumsum(n_pad)
    total = jnp.sum(n_pad)
    run_dst = g_start[None, :] + before
    run_src = jnp.arange(nt, dtype=i32)[:, None] * SLOTS + off_loc
    dst_f = run_dst.T.reshape(-1)
    shift_f = first_diff((run_src - run_dst).T.reshape(-1))
    rows = jnp.arange(G_TILES * G_CHUNKS, dtype=i32) * CHUNK
    shift = jnp.sum(jnp.where(dst_f[None, :] <= rows[:, None], shift_f[None, :], 0), axis=1)
    in_run = jnp.any((g_start[None, :] <= rows[:, None]) & (rows[:, None] < (g_start + n_e)[None, :]), axis=1)
    chunk_src = jnp.where(in_run, rows + shift, 0).astype(i32)
    loc_rows = jnp.arange(SLOT_CHUNKS, dtype=i32) * CHUNK
    shift_l = first_diff(run_dst - off_loc)
    shift = jnp.sum(jnp.where(off_loc[:, None, :] <= loc_rows[None, :, None], shift_l[:, None, :], 0), axis=2)
    used = jnp.sum(run_len, axis=1)
    chunk_map = jnp.where(loc_rows[None, :] < used[:, None], (loc_rows[None, :] + shift) // CHUNK, 0).astype(i32)
    tile_start, n_tiles = g_start // GM, n_pad // GM
    n_items = -(-n_tiles // GMM_ITEM_TILES)
    item_start = excl_cumsum(n_items)
    items = jnp.arange(GMM_MAX_ITEMS, dtype=i32)
    owner = items[:, None] >= item_start[None, :]
    e_first = jnp.sum(jnp.where(owner, first_diff(tile_start - GMM_ITEM_TILES * item_start)[None, :], 0), axis=1)
    item_tile = e_first + GMM_ITEM_TILES * items
    e_end = jnp.sum(jnp.where(owner, first_diff(tile_start + n_tiles)[None, :], 0), axis=1)
    item_cnt = jnp.clip(e_end - item_tile, 0, GMM_ITEM_TILES)
    gmm_plan = tuple(a.astype(i32) for a in (item_start, n_items, item_tile, item_cnt, chunk_src))
    return gmm_plan, chunk_map.reshape(-1)


def _gmm_in_copy(xloc_hbm, xbuf, sem, src_row, slot, c):
    return pltpu.make_async_copy(xloc_hbm.at[pl.ds(src_row, CHUNK)],
                                 xbuf.at[slot, pl.ds(c * CHUNK, CHUNK)], sem.at[slot])


def _gmm_out_copy(ybuf, y_hbm, sem, tile, slot, n_tiles):
    rows = n_tiles * GM
    return pltpu.make_async_copy(ybuf.at[slot, pl.ds(0, rows)],
                                 y_hbm.at[pl.ds(pl.multiple_of(tile * GM, GM), rows)], sem.at[slot])


def _gmm_kernel(i0_ref, ni_ref, it_ref, ic_ref, cs_ref, xloc_hbm, wg_ref, wu_ref, wd_ref, y_hbm,
                xbuf, ybuf, zbuf, wg_b, wu_b, wd_b, in_sem, out_sem, zsem):
    e = pl.program_id(0)
    last = pl.num_programs(0) - 1
    n_items = ni_ref[e]
    first_item = i0_ref[e]
    total_items = i0_ref[last] + ni_ref[last]
    last_item = total_items - 1
    total_tiles = it_ref[last_item] + ic_ref[last_item]

    def for_tiles_of(item, fn):
        fn(0)
        for k in range(1, GMM_ITEM_TILES):
            @pl.when(ic_ref[item] > k)
            def _():
                fn(k)

    def start_in(item):
        def one(k):
            for c in range(G_CHUNKS):
                src = pl.multiple_of(cs_ref[(it_ref[item] + k) * G_CHUNKS + c], CHUNK)
                _gmm_in_copy(xloc_hbm, xbuf, in_sem, src, item % GMM_SLOTS, k * G_CHUNKS + c).start()
        for_tiles_of(item, one)

    def wait_in(item):
        def one(k):
            for c in range(G_CHUNKS):
                _gmm_in_copy(xloc_hbm, xbuf, in_sem, 0, item % GMM_SLOTS, k * G_CHUNKS + c).wait()
        for_tiles_of(item, one)

    def out_copy(item, fn):
        for cnt in range(1, GMM_ITEM_TILES + 1):
            @pl.when(ic_ref[item] == cnt)
            def _():
                fn(_gmm_out_copy(ybuf, y_hbm, out_sem, it_ref[item], item % GMM_SLOTS, cnt))

    @pl.when(e == 0)
    def _():
        for item in range(GMM_SLOTS - 1):
            start_in(item)
        zbuf[...] = jnp.zeros(zbuf.shape, zbuf.dtype)

    def tail_copies(fn):
        for j in range(_GMM_TAIL_PER_STEP):
            tile = total_tiles + e + j * N_EXPERTS

            @pl.when(tile < G_TILES)
            def _():
                fn(pltpu.make_async_copy(zbuf, y_hbm.at[pl.ds(pl.multiple_of(tile * GM, GM), GM)], zsem.at[0]))

    tail_copies(lambda cp: cp.start())

    @pl.when(n_items > 0)
    def _():
        wg_b[...] = wg_ref[...].astype(bf16)
        wu_b[...] = wu_ref[...].astype(bf16)
        wd_b[...] = wd_ref[...].astype(bf16)

    lane = lax.broadcasted_iota(i32, (1, LANES), 1)
    gate_lanes = jnp.logical_or(lane == e, lane == e + N_EXPERTS)

    def body(j, carry):
        item = first_item + j
        slot = item % GMM_SLOTS

        @pl.when(item + (GMM_SLOTS - 1) < total_items)
        def _():
            start_in(item + (GMM_SLOTS - 1))

        wait_in(item)

        @pl.when(item >= GMM_SLOTS)
        def _():
            out_copy(item - GMM_SLOTS, lambda cp: cp.wait())

        for cnt in range(1, GMM_ITEM_TILES + 1):
            @pl.when(ic_ref[item] == cnt)
            def _():
                rows = cnt * GM
                x = xbuf[slot, 0:rows, :]
                gate = jnp.sum(jnp.where(gate_lanes, x[:, D_MODEL:XW].astype(f32), 0.0), axis=1, keepdims=True)
                xa = x[:, 0:D_MODEL]
                act = _silu(_dot(xa, wg_b[...])) * _dot(xa, wu_b[...]) * gate
                ybuf[slot, 0:rows, :] = _dot(act.astype(bf16), wd_b[...]).astype(bf16)

        out_copy(item, lambda cp: cp.start())
        return carry

    lax.fori_loop(0, n_items, body, 0)
    tail_copies(lambda cp: cp.wait())

    @pl.when(e == last)
    def _():
        for back in range(1, GMM_SLOTS + 1):
            out_copy(total_items - back, lambda cp: cp.wait())


_GMM_TAIL_PER_STEP = -(-(G_TILES - TM * TOP_K * N_TILES // GM) // N_EXPERTS)


def _gmm(plan, xloc, wg, wu, wd, layer):
    rows = GMM_ITEM_TILES * GM
    w_idx = lambda e, *_: (layer, e, 0, 0)
    grid_spec = pltpu.PrefetchScalarGridSpec(
        num_scalar_prefetch=5,
        grid=(N_EXPERTS,),
        in_specs=[
            pl.BlockSpec(memory_space=pl.ANY),
            pl.BlockSpec((None, None, D_MODEL, EXPERT_DIM), w_idx),
            pl.BlockSpec((None, None, D_MODEL, EXPERT_DIM), w_idx),
            pl.BlockSpec((None, None, EXPERT_DIM, D_MODEL), w_idx),
        ],
        out_specs=pl.BlockSpec(memory_space=pl.ANY),
        scratch_shapes=[pltpu.VMEM((GMM_SLOTS, rows, XW), bf16), pltpu.VMEM((GMM_SLOTS, rows, D_MODEL), bf16),
                        pltpu.VMEM((GM, D_MODEL), bf16),
                        pltpu.VMEM((D_MODEL, EXPERT_DIM), bf16), pltpu.VMEM((D_MODEL, EXPERT_DIM), bf16),
                        pltpu.VMEM((EXPERT_DIM, D_MODEL), bf16),
                        pltpu.SemaphoreType.DMA((GMM_SLOTS,)), pltpu.SemaphoreType.DMA((GMM_SLOTS,)),
                        pltpu.SemaphoreType.DMA((1,))],
    )
    return pl.pallas_call(
        _gmm_kernel,
        grid_spec=grid_spec,
        out_shape=jax.ShapeDtypeStruct((G_TILES * GM, D_MODEL), bf16),
        compiler_params=_cparams(("arbitrary",)),
        name="gmm",
    )(*plan, xloc, wg, wu, wd)


def _combine_copy(y_hbm, ybuf, sem, sorted_chunk, slot, c):
    return pltpu.make_async_copy(y_hbm.at[pl.ds(pl.multiple_of(sorted_chunk * CHUNK, CHUNK), CHUNK)],
                                 ybuf.at[slot, pl.ds(pl.multiple_of(c * CHUNK, CHUNK), CHUNK)], sem.at[slot])


def _combine_kernel(cm_ref, y_hbm, slots_ref, x_ref, mod_ref, g_ref, sg_ref, su_ref, sd_ref, *rest, final):
    if final:
        gf_ref, yp_ref, ys_ref, ybuf, sem = rest
    else:
        o_ref, ybuf, sem = rest
    i = pl.program_id(0)
    n = pl.num_programs(0)
    slot = i % 2

    def start(tile, s):
        for c in range(SLOT_CHUNKS):
            _combine_copy(y_hbm, ybuf, sem, cm_ref[tile * SLOT_CHUNKS + c], s, c).start()

    def wait(s):
        for c in range(SLOT_CHUNKS):
            _combine_copy(y_hbm, ybuf, sem, 0, s, c).wait()

    @pl.when(i == 0)
    def _():
        start(0, 0)

    wait(slot)
    start((i + 1) % n, 1 - slot)

    x = x_ref[...]
    hb = _norm_mod(x, g_ref[...], mod_ref[4:5, :], mod_ref[3:4, :]).astype(bf16)
    shared = _dot((_silu(_dot(hb, sg_ref[...])) * _dot(hb, su_ref[...])).astype(bf16), sd_ref[...])
    row_id = lax.broadcasted_iota(i32, (SLOTS, TM), 0)
    p = jnp.zeros((SLOTS, TM), f32)
    for k in range(TOP_K):
        p = jnp.where(row_id == slots_ref[k:k + 1, :], 1.0, p)
    routed = lax.dot_general(p.astype(bf16), ybuf[slot], (((0,), (0,)), ((), ())), preferred_element_type=f32)
    out = x + mod_ref[5:6, :] * (routed + shared)
    if final:
        y = (out * lax.rsqrt(jnp.mean(out * out, axis=-1, keepdims=True) + EPS)) * gf_ref[...]

        @pl.when(i < N_PROMPT_TILES)
        def _():
            yp_ref[...] = y

        @pl.when(i >= N_PROMPT_TILES)
        def _():
            ys_ref[...] = y
    else:
        o_ref[...] = out

    @pl.when(i == n - 1)
    def _():
        wait(1 - slot)


def _combine(chunk_map, y, slots, x, mod_l, g, sg, su, sd, final_g=None):
    shd = sg.shape[1]
    final = final_g is not None
    row_spec = pl.BlockSpec((TM, D_MODEL), lambda i, cm: (i, 0))
    vec_spec = pl.BlockSpec((1, D_MODEL), lambda i, cm: (0, 0))
    if final:
        out_specs = [pl.BlockSpec((TM, D_MODEL), lambda i, cm: _prompt_block(i)),
                     pl.BlockSpec((TM, D_MODEL), lambda i, cm: _sample_block(i))]
        out_shape = [jax.ShapeDtypeStruct((N_PROMPT, D_MODEL), f32), jax.ShapeDtypeStruct((N_SAMPLE, D_MODEL), f32)]
    else:
        out_specs, out_shape = row_spec, jax.ShapeDtypeStruct((N_TOK, D_MODEL), f32)
    grid_spec = pltpu.PrefetchScalarGridSpec(
        num_scalar_prefetch=1,
        grid=(N_TILES,),
        in_specs=[
            pl.BlockSpec(memory_space=pl.ANY),
            pl.BlockSpec((None, 8, TM), lambda i, cm: (i, 0, 0)),
            row_spec,
            pl.BlockSpec((None, 6, D_MODEL), lambda i, cm: (_mod_row(i), 0, 0)),
            vec_spec,
            pl.BlockSpec((D_MODEL, shd), lambda i, cm: (0, 0)),
            pl.BlockSpec((D_MODEL, shd), lambda i, cm: (0, 0)),
            pl.BlockSpec((shd, D_MODEL), lambda i, cm: (0, 0)),
        ] + ([vec_spec] if final else []),
        out_specs=out_specs,
        scratch_shapes=[pltpu.VMEM((2, SLOTS, D_MODEL), bf16), pltpu.SemaphoreType.DMA((2,))],
    )
    args = (chunk_map, y, slots, x, mod_l, g, sg, su, sd) + ((final_g,) if final else ())
    return pl.pallas_call(
        functools.partial(_combine_kernel, final=final),
        grid_spec=grid_spec,
        out_shape=out_shape,
        compiler_params=_cparams(("arbitrary",)),
        name="combine",
    )(*args)


def _moe(x, mod_l, g, rwt, rb, wg, wu, wd, layer, sg, su, sd, final_g=None):
    xloc, slots, run_len = _route(x, mod_l, g, rwt, rb)
    gmm_plan, chunk_map = _moe_plan(run_len[:, :, 0])
    y = _gmm(gmm_plan, xloc, wg, wu, wd, layer)
    return _combine(chunk_map, y, slots, x, mod_l, g, sg, su, sd, final_g)


_L0_CHUNKS = (
    (0, 512, 0, (0, 1, 2, 3), ()),
    (512, 1024, 768, (), ()),
    (1024, 1536, 1280, (), ((2, 0, 512, 0, False),)),
    (1536, 2048, 1792, (), ((3, 0, 512, 0, False),)),
    (2048, 2304, 512, (0,), ((0, 0, 128, 0, False), (1, 128, 256, 0, False))),
)
_L0_KV_OUTS = (("T", 128), ("T", 128), ("T", 512), ("T", 512))
_L1_CHUNKS = (
    (0, 512, 0, (0, 1, 2, 3), ()),
    (512, 1024, 512, (0, 1, 2, 3), ()),
    (1024, 1536, 1024, (0, 1, 2, 3), ((0, 0, 512, 0, False),)),
    (1536, 2048, 1536, (0, 1, 2, 3), ((0, 0, 512, 512, False),)),
    (2048, 2560, 2048, (), ((1, 0, 512, 0, True),)),
    (2560, 3072, 2560, (), ((1, 0, 512, 512, True),)),
)
_L1_KV_OUTS = (("T", 1024), ("H", 8))


def _from_feature_major(kt, *head_dims):
    nb, _, s = kt.shape
    nd = len(head_dims)
    return kt.reshape(nb, *head_dims, s).transpose(0, nd + 1, *range(1, nd + 1))[:, None]


def kernel(x_prompt, x_sample, cache_a_k, cache_a_v, cache_b_k, cache_b_v, cache_c_k, cache_c_v, c, c_ctx, w_mod, b_mod, norm_mix, norm_ffn, w_in_ab, w_out_ab, sink_a, rel_bias_b, w_in_c, w_out_c, lam_q1, lam_k1, lam_q2, lam_k2, subln_c, router_w, router_bias, exp_w_gate, exp_w_up, exp_w_down, sh_w_gate, sh_w_up, sh_w_down, final_norm):
    x = (x_prompt.reshape(N_PROMPT, D_MODEL), x_sample.reshape(N_SAMPLE, D_MODEL))
    cond8 = jnp.concatenate([c_ctx[None, :], c, jnp.zeros((8 - 1 - N_SAMPLE_BATCH, D_MODEL), f32)], axis=0)
    mod = _adaln(cond8, w_mod, b_mod).reshape(DEPTH, 8, 6, D_MODEL)
    rope_tabs = _rope_tables()
    new_kv = {}
    for layer in range(DEPTH):
        li = layer // 2
        mod_l = mod[layer]
        g_mix = norm_mix[layer][None, :]
        g_ffn = norm_ffn[layer][None, :]
        if layer % 2 == 0:
            w_in = w_in_ab[li].astype(bf16)
            qkv, ak, av, bk, bv = _inproj(x, mod_l, g_mix, w_in, rope_tabs, _L0_CHUNKS, _L0_KV_OUTS)
            new_kv["a_k"], new_kv["a_v"], new_kv["b_k"], new_kv["b_v"] = ak, av, bk, bv
            o_p = _ctx0(sink_a[li], qkv)
            o_s = _lat0(sink_a[li], qkv,
                        cache_a_k[:, li].reshape(N_SAMPLE_BATCH, PAST_LEN, LANES),
                        cache_a_v[:, li].reshape(N_SAMPLE_BATCH, PAST_LEN, LANES),
                        cache_b_k[:, li].reshape(N_SAMPLE_BATCH, PAST_LEN, 512),
                        cache_b_v[:, li].reshape(N_SAMPLE_BATCH, PAST_LEN, 512),
                        _na_bias_tiles(rel_bias_b[li]))
            w_out = w_out_ab[li].astype(bf16)
        else:
            lam_init = 0.8 - 0.6 * math.exp(-0.3 * layer)
            qkv, ck, cv = _inproj(x, mod_l, g_mix, w_in_c[li].astype(bf16), rope_tabs, _L1_CHUNKS, _L1_KV_OUTS)
            new_kv["c_k"], new_kv["c_v"] = ck, cv
            lamv = jnp.concatenate([lam_q1[li][None], lam_k1[li][None], lam_q2[li][None], lam_k2[li][None],
                                    jnp.zeros((4, HEAD_DIM), f32)], axis=0)
            subln = subln_c[li][None, :]
            o_p = _ctx1(lamv, subln, qkv, lam_init)
            o_s = _lat1(lamv, subln, qkv,
                        cache_c_k[:, li].reshape(N_SAMPLE_BATCH, PAST_LEN, D_MODEL),
                        cache_c_v[:, li].reshape(N_SAMPLE_BATCH, PAST_LEN, D_MODEL), lam_init)
            w_out = w_out_c[li].astype(bf16)
        x = _outproj(x, o_p, o_s, mod_l, w_out)
        last = layer == DEPTH - 1
        x = _moe(x, mod_l, g_ffn, router_w[layer].T, router_bias[layer][:, None],
                 exp_w_gate, exp_w_up, exp_w_down, layer,
                 sh_w_gate[layer].astype(bf16), sh_w_up[layer].astype(bf16), sh_w_down[layer].astype(bf16),
                 final_norm[None, :] if last else None)
        x = x if last else (x,)
    y_prompt, y_sample = x
    nb, s = N_PROMPT_BATCH, PROMPT_SEQ
    return (y_prompt.reshape(nb, s, D_MODEL), y_sample.reshape(N_SAMPLE_BATCH, SAMPLE_SEQ, D_MODEL),
            _from_feature_major(new_kv["a_k"], 2, HEAD_DIM), _from_feature_major(new_kv["a_v"], 2, HEAD_DIM),
            _from_feature_major(new_kv["b_k"], 8, HEAD_DIM), _from_feature_major(new_kv["b_v"], 8, HEAD_DIM),
            _from_feature_major(new_kv["c_k"], 8, 2, HEAD_DIM), new_kv["c_v"].reshape(nb, 1, s, 8, 2 * HEAD_DIM))
```

```python
import functools
import math

import jax
import jax.numpy as jnp
import numpy as np
from jax import lax
from jax.experimental import pallas as pl
from jax.experimental.pallas import tpu as pltpu

f32 = jnp.float32
bf16 = jnp.bfloat16
i32 = jnp.int32

D_MODEL = 1024
N_PROMPT_BATCH = 16
PROMPT_SEQ = 256
DEPTH = 2
N_SAMPLE_BATCH = 2
SAMPLE_SEQ = 2048
PAST_LEN = 512
GRID_W = 64
HEAD_DIM = 64
ROPE_THETA = 10000.0
EPS = 1e-6
A_WINDOW = 128
NA_ROWS = 8
NA_COLS = 16
N_EXPERTS = 64
TOP_K = 6
N_GROUPS = 8
TOPK_GROUPS = 4
EXPERT_DIM = 256
ROUTED_SCALE = 2.5
Q_SCALE = HEAD_DIM ** -0.5

N_PROMPT = N_PROMPT_BATCH * PROMPT_SEQ
N_SAMPLE = N_SAMPLE_BATCH * SAMPLE_SEQ
N_TOK = N_PROMPT + N_SAMPLE

LANES = 128
TM = 256
TD = 512
N_PROMPT_TILES = N_PROMPT // TM
N_TILES = N_TOK // TM
QB = 128
CHUNK = 16
SLOTS = -(-(TM * TOP_K + N_EXPERTS * (CHUNK - 1)) // 256) * 256
SLOT_CHUNKS = SLOTS // CHUNK
ROUTE_ROWS = 512
GM = 256
_MAX_SORTED = TM * TOP_K * N_TILES + N_TILES * N_EXPERTS * (CHUNK - 1) + N_EXPERTS * (GM - CHUNK)
G_TILES = -(-_MAX_SORTED // GM)
G_CHUNKS = GM // CHUNK
GMM_ITEM_TILES = 2
GMM_SLOTS = 3
GMM_MAX_ITEMS = (G_TILES + N_EXPERTS * (GMM_ITEM_TILES - 1)) // GMM_ITEM_TILES
VMEM_LIMIT = 56 * 1024 * 1024

NEG = -1e30


def _cparams(sem):
    return pltpu.CompilerParams(dimension_semantics=sem, vmem_limit_bytes=VMEM_LIMIT)


def _mod_row(i, tm=TM):
    return jnp.where(i < N_PROMPT // tm, 0, 1 + (i - N_PROMPT // tm) // (SAMPLE_SEQ // tm))


def _prompt_block(i, tm=TM):
    return (jnp.minimum(i, N_PROMPT // tm - 1), 0)


def _sample_block(i, tm=TM):
    return (jnp.maximum(i - N_PROMPT // tm, 0), 0)


def _x_specs(parts, tm=TM):
    if len(parts) == 1:
        return [pl.BlockSpec((tm, D_MODEL), lambda i, *_: (i, 0))]
    return [pl.BlockSpec((tm, D_MODEL), lambda i, *_: _prompt_block(i, tm)),
            pl.BlockSpec((tm, D_MODEL), lambda i, *_: _sample_block(i, tm))]


def _load_x(i, x_refs, tm=TM):
    if len(x_refs) == 1:
        return x_refs[0][...]
    return jnp.where(i < N_PROMPT // tm, x_refs[0][...], x_refs[1][...])


def _norm_mod(x, g, scale, shift):
    y = x * lax.rsqrt(jnp.mean(x * x, axis=-1, keepdims=True) + EPS)
    return (y * g) * (1.0 + scale) + shift


def _silu(x):
    return x * jax.nn.sigmoid(x)


def _dot(a, b):
    return jnp.dot(a, b, preferred_element_type=f32)


def _dot_nt(a, b):
    return lax.dot_general(a, b, (((1,), (1,)), ((), ())), preferred_element_type=f32)


ADA_COLS = 1536


def _adaln_kernel(cond_ref, w_ref, b_ref, o_ref):
    s = _silu(cond_ref[...]).astype(bf16)
    o_ref[...] = _dot(s, w_ref[...].astype(bf16)) + b_ref[...]


def _adaln(cond8, w_mod, b_mod):
    n6 = 6 * D_MODEL
    return pl.pallas_call(
        _adaln_kernel,
        grid=(DEPTH, n6 // ADA_COLS),
        in_specs=[
            pl.BlockSpec((8, D_MODEL), lambda l, j: (0, 0)),
            pl.BlockSpec((None, D_MODEL, ADA_COLS), lambda l, j: (l, 0, j)),
            pl.BlockSpec((None, 1, ADA_COLS), lambda l, j: (l, 0, j)),
        ],
        out_specs=pl.BlockSpec((None, 8, ADA_COLS), lambda l, j: (l, 0, j)),
        out_shape=jax.ShapeDtypeStruct((DEPTH, 8, n6), f32),
        compiler_params=_cparams(("parallel", "parallel")),
        name="adaln",
    )(cond8, w_mod, b_mod.reshape(DEPTH, 1, n6))


def _rope_block(blk, cos, sin_a, sin_b):
    return blk * cos + pltpu.roll(blk, LANES - 16, 1) * sin_a + pltpu.roll(blk, 16, 1) * sin_b


def _inproj_kernel(*refs, chunks, n_x):
    x_refs, kv_refs = refs[:n_x], refs[n_x + 7:]
    mod_ref, g_ref, w_ref, cos_ref, sa_ref, sb_ref, qkv_ref = refs[n_x:n_x + 7]
    i = pl.program_id(0)
    h = _norm_mod(_load_x(i, x_refs, TD), g_ref[...], mod_ref[1:2, :], mod_ref[0:1, :]).astype(bf16)
    is_prompt = i < N_PROMPT // TD

    @pl.when(is_prompt)
    def _():
        for c0, c1, s0, _, kv_out in chunks:
            acc = _dot(h, w_ref[:, s0:s0 + (c1 - c0)])
            qkv_ref[:, c0:c1] = acc.astype(bf16)
            for ridx, a0, a1, o0, per_head in kv_out:
                if per_head:
                    heads = kv_refs[ridx].shape[0] // TD
                    for j in range((a1 - a0) // LANES):
                        kv_refs[ridx][pl.ds(o0 // LANES + j, TD, stride=heads), :] = (
                            acc[:, a0 + j * LANES:a0 + (j + 1) * LANES])
                else:
                    t = acc[:, a0:a1].T
                    for b in range(TD // PROMPT_SEQ):
                        kv_refs[ridx][b, o0:o0 + (a1 - a0), :] = t[:, b * PROMPT_SEQ:(b + 1) * PROMPT_SEQ]

    @pl.when(jnp.logical_not(is_prompt))
    def _():
        cos, sa, sb = cos_ref[...], sa_ref[...], sb_ref[...]
        for c0, c1, s0, rope_blocks, _ in chunks:
            acc = _dot(h, w_ref[:, s0:s0 + (c1 - c0)])
            for b in range((c1 - c0) // LANES):
                blk = acc[:, b * LANES:(b + 1) * LANES]
                if b in rope_blocks:
                    blk = _rope_block(blk, cos, sa, sb)
                qkv_ref[:, c0 + b * LANES:c0 + (b + 1) * LANES] = blk.astype(bf16)


def _inproj(x_parts, mod_l, g, w, rope_tabs, chunks, kv_outs):
    n = w.shape[1]
    cos, sa, sb = rope_tabs
    bpt = TD // PROMPT_SEQ

    def rope_idx(i):
        return (jnp.where(i < N_PROMPT // TD, 0, (i - N_PROMPT // TD) % (SAMPLE_SEQ // TD)), 0)

    kv_specs, kv_shapes = [], []
    for kind, size in kv_outs:
        if kind == "T":
            kv_specs.append(pl.BlockSpec((bpt, size, PROMPT_SEQ), lambda i: _prompt_block(i, TD) + (0,)))
            kv_shapes.append(jax.ShapeDtypeStruct((N_PROMPT_BATCH, size, PROMPT_SEQ), f32))
        else:
            kv_specs.append(pl.BlockSpec((TD * size, LANES), lambda i: _prompt_block(i, TD)))
            kv_shapes.append(jax.ShapeDtypeStruct((N_PROMPT * size, LANES), f32))

    return pl.pallas_call(
        functools.partial(_inproj_kernel, chunks=chunks, n_x=len(x_parts)),
        grid=(N_TOK // TD,),
        in_specs=_x_specs(x_parts, TD) + [
            pl.BlockSpec((None, 6, D_MODEL), lambda i: (_mod_row(i, TD), 0, 0)),
            pl.BlockSpec((1, D_MODEL), lambda i: (0, 0)),
            pl.BlockSpec((D_MODEL, n), lambda i: (0, 0)),
            pl.BlockSpec((TD, LANES), rope_idx),
            pl.BlockSpec((TD, LANES), rope_idx),
            pl.BlockSpec((TD, LANES), rope_idx),
        ],
        out_specs=[pl.BlockSpec((TD, n), lambda i: (i, 0))] + kv_specs,
        out_shape=[jax.ShapeDtypeStruct((N_TOK, n), bf16)] + kv_shapes,
        compiler_params=_cparams(("arbitrary",)),
        name="inproj",
    )(*x_parts, mod_l, g, w, cos, sa, sb)


def _rope_tables():
    nq = HEAD_DIM // 4
    t = np.arange(SAMPLE_SEQ)
    inv = np.power(np.float32(ROPE_THETA), -np.arange(nq, dtype=np.float32) / np.float32(nq))
    ang_r = (t // GRID_W).astype(np.float32)[:, None] * inv
    ang_c = (t % GRID_W).astype(np.float32)[:, None] * inv
    zero = np.zeros_like(ang_r)

    def head(fr, fc):
        return np.concatenate([fr[0], fr[1], fc[0], fc[1]], axis=-1)

    cos = head((np.cos(ang_r), np.cos(ang_r)), (np.cos(ang_c), np.cos(ang_c)))
    sin_a = head((-np.sin(ang_r), zero), (-np.sin(ang_c), zero))
    sin_b = head((zero, np.sin(ang_r)), (zero, np.sin(ang_c)))
    two = lambda a: jnp.asarray(np.concatenate([a, a], axis=-1).astype(np.float32))
    return two(cos), two(sin_a), two(sin_b)


def _lane_lo(shape):
    return lax.broadcasted_iota(i32, shape, len(shape) - 1) < HEAD_DIM


def _half(q, lo_mask, half):
    keep = lo_mask if half == 0 else jnp.logical_not(lo_mask)
    return jnp.where(keep, q, jnp.zeros_like(q)) * Q_SCALE


def _swap_halves(x):
    return pltpu.roll(x.astype(f32), HEAD_DIM, 1).astype(x.dtype)


def _stack_halves(q, lo_mask):
    return jnp.concatenate([_half(q, lo_mask, 0), _half(q, lo_mask, 1)], axis=0)


def _with_ones(v):
    return jnp.concatenate([v, jnp.ones_like(v)], axis=1)


def _attend(q_rows, n_heads, key_blocks, vx_blocks, fix_scores=None, sinks=None):
    r = q_rows.shape[0] // n_heads
    scores = [_dot_nt(q_rows, k) for k in key_blocks]
    exps = [[] for _ in key_blocks]
    maxes = []
    for h in range(n_heads):
        blocks = [s[h * r:(h + 1) * r] for s in scores]
        if fix_scores is not None:
            blocks = [fix_scores(h, i, s) for i, s in enumerate(blocks)]
        m = functools.reduce(jnp.maximum, [jnp.max(s, axis=-1, keepdims=True) for s in blocks])
        if sinks is not None:
            m = jnp.maximum(m, sinks[h])
        maxes.append(m)
        for i, s in enumerate(blocks):
            exps[i].append(jnp.exp((s - m).astype(bf16)))
    out = functools.reduce(lambda a, b: a + b,
                           [_dot(e[0] if n_heads == 1 else jnp.concatenate(e, axis=0), vx)
                            for e, vx in zip(exps, vx_blocks)])
    outs = []
    for h in range(n_heads):
        den = out[h * r:(h + 1) * r, LANES:]
        if sinks is not None:
            den = den + jnp.exp(sinks[h] - maxes[h])
        outs.append(out[h * r:(h + 1) * r, :LANES] * (1.0 / den))
    return outs


def _gqa_rows(q_blocks, group, lo_mask):
    parts = []
    for q in q_blocks:
        for half in range(2):
            qh = _half(q, lo_mask, half)
            parts.append(qh if half == group else _swap_halves(qh))
    return jnp.concatenate(parts, axis=0)


def _gqa_merge(outs, group, lo_mask):
    fixed = [o if idx % 2 == group else pltpu.roll(o, HEAD_DIM, 1) for idx, o in enumerate(outs)]
    return [jnp.where(lo_mask, fixed[2 * p], fixed[2 * p + 1]) for p in range(len(outs) // 2)]


L0_QA, L0_QB, L0_KB, L0_VB, L0_KA, L0_VA, L0_N = 0, 512, 1024, 1536, 2048, 2176, 2304


def _ctx0_kernel(sink_ref, qkv_ref, o_ref):
    lo = _lane_lo((1, LANES))
    blk = lambda base, j: qkv_ref[:, base + j * LANES:base + (j + 1) * LANES]
    k_a = blk(L0_KA, 0)
    vx_a = _with_ones(blk(L0_VA, 0))
    for g in range(2):
        q_rows = _gqa_rows([blk(L0_QA, 2 * g), blk(L0_QA, 2 * g + 1)], g, lo)
        outs = _attend(q_rows, 4, [k_a], [vx_a], sinks=[sink_ref[4 * g + idx] for idx in range(4)])
        for p, o in enumerate(_gqa_merge(outs, g, lo)):
            j = 2 * g + p
            o_ref[:, j * LANES:(j + 1) * LANES] = o.astype(bf16)
    for j in range(4):
        outs = _attend(_stack_halves(blk(L0_QB, j), lo), 2, [blk(L0_KB, j)], [_with_ones(blk(L0_VB, j))])
        o_ref[:, 512 + j * LANES:512 + (j + 1) * LANES] = jnp.where(lo, outs[0], outs[1]).astype(bf16)


def _ctx0(sink, qkv):
    return pl.pallas_call(
        _ctx0_kernel,
        grid=(N_PROMPT_BATCH,),
        in_specs=[
            pl.BlockSpec(memory_space=pltpu.SMEM),
            pl.BlockSpec((PROMPT_SEQ, L0_N), lambda b: (b, 0)),
        ],
        out_specs=pl.BlockSpec((PROMPT_SEQ, D_MODEL), lambda b: (b, 0)),
        out_shape=jax.ShapeDtypeStruct((N_PROMPT, D_MODEL), bf16),
        compiler_params=_cparams(("parallel",)),
        name="ctx0",
    )(sink, qkv)


WIN_KEYS = 3 * QB
NA_KEY_ROWS = 10
NA_KEYS = NA_KEY_ROWS * GRID_W
N_QB = SAMPLE_SEQ // QB
N_NA_PATTERNS = 5
_PROMPT_QBLOCKS = N_PROMPT // QB


def _na_pattern(n):
    return jnp.where(n < 2, n, jnp.where(n > N_QB - 3, n - (N_QB - 5), 2))


def _lat0_kernel(sink_ref, roff_ref, q_ref, kvb_ref, kva_ref, cak_ref, cav_ref, cbk_ref, cbv_ref, tiles_ref, o_ref):
    n = pl.program_id(1)
    lo = _lane_lo((1, LANES))
    kstart = pl.multiple_of(jnp.clip((n - 1) * QB, 0, SAMPLE_SEQ - WIN_KEYS), QB)
    k_a = kva_ref[pl.ds(kstart, WIN_KEYS), 0:LANES]
    v_a = kva_ref[pl.ds(kstart, WIN_KEYS), LANES:2 * LANES]
    c_k = cak_ref[...].astype(bf16)
    keys_a = [c_k, k_a]
    vx_a = [_with_ones(cav_ref[...].astype(bf16)), _with_ones(v_a)]
    qpos = n * QB + lax.broadcasted_iota(i32, (QB, WIN_KEYS), 0)
    kpos = kstart + lax.broadcasted_iota(i32, (QB, WIN_KEYS), 1)
    in_window = jnp.abs(qpos - kpos) <= A_WINDOW
    mask_window = lambda h, i, s: jnp.where(in_window, s, NEG) if i == 1 else s
    for g in range(2):
        q_rows = _gqa_rows([q_ref[:, L0_QA + j * LANES:L0_QA + (j + 1) * LANES] for j in (2 * g, 2 * g + 1)], g, lo)
        outs = _attend(q_rows, 4, keys_a, vx_a, fix_scores=mask_window,
                       sinks=[sink_ref[4 * g + idx] for idx in range(4)])
        for p, o in enumerate(_gqa_merge(outs, g, lo)):
            j = 2 * g + p
            o_ref[:, j * LANES:(j + 1) * LANES] = o.astype(bf16)
    krow = jnp.clip(2 * n - NA_ROWS // 2, 0, SAMPLE_SEQ // GRID_W - NA_KEY_ROWS)
    ktok = pl.multiple_of(krow * GRID_W, QB)
    pattern = _na_pattern(n)

    def na_bias(head):
        rows = []
        for rq in range(QB // GRID_W):
            blocks = []
            for kb in range(NA_KEY_ROWS // 2):
                d0, d1 = (roff_ref[(pattern * 2 + rq) * NA_KEY_ROWS + 2 * kb + t] for t in range(2))
                blocks.append(jnp.where(lo, tiles_ref[head, d0], tiles_ref[head, d1]))
            rows.append(jnp.concatenate(blocks, axis=1))
        return jnp.concatenate(rows, axis=0)

    for j in range(4):
        q_b = q_ref[:, L0_QB + j * LANES:L0_QB + (j + 1) * LANES]
        k_b = kvb_ref[pl.ds(ktok, NA_KEYS), j * LANES:(j + 1) * LANES]
        v_b = kvb_ref[pl.ds(ktok, NA_KEYS), 512 + j * LANES:512 + (j + 1) * LANES]
        cb_k = cbk_ref[:, j * LANES:(j + 1) * LANES].astype(bf16)
        cb_v = cbv_ref[:, j * LANES:(j + 1) * LANES].astype(bf16)
        add_bias = lambda h, i, s, j=j: s + na_bias(2 * j + h) if i == 1 else s
        outs = _attend(_stack_halves(q_b, lo), 2, [cb_k, k_b], [_with_ones(cb_v), _with_ones(v_b)],
                       fix_scores=add_bias)
        o_ref[:, 512 + j * LANES:512 + (j + 1) * LANES] = jnp.where(lo, outs[0], outs[1]).astype(bf16)


def _lat0(sink, qkv, cak, cav, cbk, cbv, bias_tiles):
    sb = N_PROMPT // SAMPLE_SEQ
    return pl.pallas_call(
        _lat0_kernel,
        grid=(N_SAMPLE_BATCH, N_QB),
        in_specs=[
            pl.BlockSpec(memory_space=pltpu.SMEM),
            pl.BlockSpec(memory_space=pltpu.SMEM),
            pl.BlockSpec((QB, 1024), lambda b, n: (_PROMPT_QBLOCKS + b * N_QB + n, 0)),
            pl.BlockSpec((SAMPLE_SEQ, 1024), lambda b, n: (sb + b, 1)),
            pl.BlockSpec((SAMPLE_SEQ, 256), lambda b, n: (sb + b, L0_KA // 256)),
            pl.BlockSpec((None, PAST_LEN, LANES), lambda b, n: (b, 0, 0)),
            pl.BlockSpec((None, PAST_LEN, LANES), lambda b, n: (b, 0, 0)),
            pl.BlockSpec((None, PAST_LEN, 512), lambda b, n: (b, 0, 0)),
            pl.BlockSpec((None, PAST_LEN, 512), lambda b, n: (b, 0, 0)),
            pl.BlockSpec((8, N_ROW_OFFSETS + 1, GRID_W, LANES), lambda b, n: (0, 0, 0, 0)),
        ],
        out_specs=pl.BlockSpec((QB, D_MODEL), lambda b, n: (b * N_QB + n, 0)),
        out_shape=jax.ShapeDtypeStruct((N_SAMPLE, D_MODEL), bf16),
        compiler_params=_cparams(("parallel", "arbitrary")),
        name="lat0",
    )(sink, jnp.asarray(_na_row_offsets()), qkv, qkv, qkv, cak, cav, cbk, cbv, bias_tiles)


N_ROW_OFFSETS = 2 * NA_ROWS - 1


def _na_row_offsets():
    rows = SAMPLE_SEQ // GRID_W
    idx = np.full((N_NA_PATTERNS, 2, NA_KEY_ROWS), N_ROW_OFFSETS, np.int32)
    for p, n in enumerate((0, 1, 2, N_QB - 2, N_QB - 1)):
        k0 = int(np.clip(2 * n - NA_ROWS // 2, 0, rows - NA_KEY_ROWS))
        for rq in range(2):
            r = 2 * n + rq
            rs = int(np.clip(r - NA_ROWS // 2, 0, rows - NA_ROWS))
            for kl in range(NA_KEY_ROWS):
                if rs <= k0 + kl < rs + NA_ROWS:
                    idx[p, rq, kl] = k0 + kl - r + NA_ROWS - 1
    return idx.reshape(-1)


def _na_bias_tiles(rel_bias):
    n_dc = 2 * NA_COLS - 1
    c = np.arange(GRID_W)[:, None]
    kc = np.arange(GRID_W)[None, :]
    cs = np.clip(c - NA_COLS // 2, 0, GRID_W - NA_COLS)
    col_ok = (kc >= cs) & (kc < cs + NA_COLS)
    col_hot = ((kc - c + NA_COLS - 1)[None] == np.arange(n_dc)[:, None, None]) & col_ok[None]
    hp = lax.Precision.HIGHEST
    tiles = jnp.einsum("hdx,xck->hdck", rel_bias.astype(f32), col_hot.astype(np.float32), precision=hp)
    tiles = tiles + np.where(col_ok, 0.0, NEG).astype(np.float32)
    tiles = jnp.concatenate([tiles, jnp.full((tiles.shape[0], 1, GRID_W, GRID_W), NEG, f32)], axis=1)
    return jnp.concatenate([tiles, tiles], axis=-1)


def _diff_lambda(lam_ref, lam_init):
    lv = lam_ref[...]
    s1 = jnp.sum(lv[0:1, :] * lv[1:2, :], axis=-1, keepdims=True)
    s2 = jnp.sum(lv[2:3, :] * lv[3:4, :], axis=-1, keepdims=True)
    return jnp.exp(s1) - jnp.exp(s2) + lam_init


def _diff_head(q, key_blocks, value_blocks, lam, subln, lo, lam_init):
    o1, o2 = _attend(_stack_halves(q, lo), 2, key_blocks, [_with_ones(v) for v in value_blocks])
    o = o1 - lam * o2
    o = o * lax.rsqrt(jnp.mean(o * o, axis=-1, keepdims=True) + EPS)
    return (o * subln) * (1.0 - lam_init)


def _ctx1_kernel(lam_ref, subln_ref, qkv_ref, o_ref, *, lam_init):
    lo = _lane_lo((1, LANES))
    lam = _diff_lambda(lam_ref, lam_init)
    subln = subln_ref[...]
    for h in range(8):
        q = qkv_ref[:, h * LANES:(h + 1) * LANES]
        k = qkv_ref[:, 1024 + h * LANES:1024 + (h + 1) * LANES]
        v = qkv_ref[:, 2048 + h * LANES:2048 + (h + 1) * LANES]
        o_ref[:, h * LANES:(h + 1) * LANES] = _diff_head(q, [k], [v], lam, subln, lo, lam_init).astype(bf16)


def _ctx1(lamv, subln, qkv, lam_init):
    return pl.pallas_call(
        functools.partial(_ctx1_kernel, lam_init=lam_init),
        grid=(N_PROMPT_BATCH,),
        in_specs=[
            pl.BlockSpec((8, HEAD_DIM), lambda b: (0, 0)),
            pl.BlockSpec((1, LANES), lambda b: (0, 0)),
            pl.BlockSpec((PROMPT_SEQ, 3 * D_MODEL), lambda b: (b, 0)),
        ],
        out_specs=pl.BlockSpec((PROMPT_SEQ, D_MODEL), lambda b: (b, 0)),
        out_shape=jax.ShapeDtypeStruct((N_PROMPT, D_MODEL), bf16),
        compiler_params=_cparams(("parallel",)),
        name="ctx1",
    )(lamv, subln, qkv)


def _lat1_kernel(lam_ref, subln_ref, q_ref, k_ref, v_ref, ck_ref, cv_ref, o_ref, *, lam_init):
    lo = _lane_lo((1, LANES))
    lam = _diff_lambda(lam_ref, lam_init)
    subln = subln_ref[...]
    for h in range(8):
        sl = slice(h * LANES, (h + 1) * LANES)
        o_ref[:, sl] = _diff_head(q_ref[:, sl], [ck_ref[:, sl].astype(bf16), k_ref[:, sl]],
                                  [cv_ref[:, sl].astype(bf16), v_ref[:, sl]],
                                  lam, subln, lo, lam_init).astype(bf16)


def _lat1(lamv, subln, qkv, ck, cv, lam_init):
    sb = N_PROMPT // SAMPLE_SEQ
    nq = SAMPLE_SEQ // TM
    return pl.pallas_call(
        functools.partial(_lat1_kernel, lam_init=lam_init),
        grid=(N_SAMPLE_BATCH, nq),
        in_specs=[
            pl.BlockSpec((8, HEAD_DIM), lambda b, n: (0, 0)),
            pl.BlockSpec((1, LANES), lambda b, n: (0, 0)),
            pl.BlockSpec((TM, D_MODEL), lambda b, n: (N_PROMPT_TILES + b * nq + n, 0)),
            pl.BlockSpec((SAMPLE_SEQ, D_MODEL), lambda b, n: (sb + b, 1)),
            pl.BlockSpec((SAMPLE_SEQ, D_MODEL), lambda b, n: (sb + b, 2)),
            pl.BlockSpec((None, PAST_LEN, D_MODEL), lambda b, n: (b, 0, 0)),
            pl.BlockSpec((None, PAST_LEN, D_MODEL), lambda b, n: (b, 0, 0)),
        ],
        out_specs=pl.BlockSpec((TM, D_MODEL), lambda b, n: (b * nq + n, 0)),
        out_shape=jax.ShapeDtypeStruct((N_SAMPLE, D_MODEL), bf16),
        compiler_params=_cparams(("parallel", "arbitrary")),
        name="lat1",
    )(lamv, subln, qkv, qkv, qkv, ck, cv)


def _split_bf16(a):
    hi = a.astype(bf16)
    return hi, (a - hi.astype(f32)).astype(bf16)


def _route_kernel(*refs, n_x):
    x_refs = refs[:n_x]
    (op_ref, os_ref, mod_ref, g_ref, wo_ref, rwt_ref, rb_ref,
     xnew_ref, xloc_ref, slots_ref, gate_ref, len_ref) = refs[n_x:]
    i = pl.program_id(0)
    ng, ge = N_GROUPS, N_EXPERTS // N_GROUPS
    attn = jnp.where(i < N_PROMPT_TILES, op_ref[...], os_ref[...])
    x = _load_x(i, x_refs) + mod_ref[2:3, :] * _dot(attn, wo_ref[...])
    xnew_ref[...] = x
    h = _norm_mod(x, g_ref[...], mod_ref[4:5, :], mod_ref[3:4, :])
    h_hi, h_lo = _split_bf16(h)
    w_hi, w_lo = _split_bf16(rwt_ref[...])
    logits = _dot_nt(w_hi, h_hi) + (_dot_nt(w_hi, h_lo) + _dot_nt(w_lo, h_hi))
    scores = jax.nn.sigmoid(logits)
    biased = scores + rb_ref[...]
    s3 = scores.reshape(ng, ge, TM)
    b3 = biased.reshape(ng, ge, TM)
    in_group = lax.broadcasted_iota(i32, (ng, ge, TM), 1).astype(f32)
    group_id = lax.broadcasted_iota(i32, (ng, 1, TM), 0).astype(f32)
    expert_id = lax.broadcasted_iota(i32, (ng, ge, TM), 0).astype(f32) * ge + in_group

    def max01(a):
        return jnp.max(jnp.max(a, axis=0, keepdims=True), axis=1, keepdims=True)

    def min01(a):
        return jnp.min(jnp.min(a, axis=0, keepdims=True), axis=1, keepdims=True)

    def sum01(a):
        return jnp.sum(jnp.sum(a, axis=0, keepdims=True), axis=1, keepdims=True)

    m1 = jnp.max(b3, axis=1, keepdims=True)
    first = jnp.min(jnp.where(b3 == m1, in_group, ge), axis=1, keepdims=True)
    m2 = jnp.max(jnp.where(in_group == first, -jnp.inf, b3), axis=1, keepdims=True)
    gscore = m1 + m2
    gsel = jnp.zeros((ng, 1, TM), f32)
    for _ in range(TOPK_GROUPS):
        gm = jnp.max(gscore, axis=0, keepdims=True)
        gi = jnp.min(jnp.where(gscore == gm, group_id, ng), axis=0, keepdims=True)
        hit = group_id == gi
        gsel = jnp.where(hit, 1.0, gsel)
        gscore = jnp.where(hit, -jnp.inf, gscore)
    cand = jnp.where(jnp.broadcast_to(gsel, (ng, ge, TM)) > 0.0, b3, -jnp.inf)
    top_e, top_w = [], []
    for _ in range(TOP_K):
        em = max01(cand)
        ei = min01(jnp.where(cand == em, expert_id, N_EXPERTS))
        hit = expert_id == ei
        top_e.append(ei)
        top_w.append(sum01(jnp.where(hit, s3, 0.0)))
        cand = jnp.where(hit, -jnp.inf, cand)
    wsum = functools.reduce(lambda a, b: a + b, top_w)
    sel3 = jnp.zeros((ng, ge, TM), f32)
    for k, (ei, w) in enumerate(zip(top_e, top_w)):
        gate_ref[k:k + 1, :] = (w / wsum * ROUTED_SCALE).reshape(1, TM)
        sel3 = jnp.where(expert_id == ei, 1.0, sel3)
    gate_ref[TOP_K:8, :] = jnp.zeros((8 - TOP_K, TM), f32)
    sel = sel3.reshape(N_EXPERTS, TM)

    cnt = jnp.sum(sel, axis=1, keepdims=True)
    run_len = jnp.ceil(cnt * (1.0 / CHUNK)) * CHUNK
    r_i = lax.broadcasted_iota(i32, (N_EXPERTS, N_EXPERTS), 0)
    c_i = lax.broadcasted_iota(i32, (N_EXPERTS, N_EXPERTS), 1)
    lower = jnp.where(c_i < r_i, 1.0, 0.0).astype(bf16)
    run_off = _dot(lower, jnp.broadcast_to(run_len, (N_EXPERTS, LANES)).astype(bf16))[:, 0:1]
    t_r = lax.broadcasted_iota(i32, (TM, TM), 0)
    t_c = lax.broadcasted_iota(i32, (TM, TM), 1)
    before = jnp.where(t_r < t_c, 1.0, 0.0).astype(bf16)
    rank = _dot(sel.astype(bf16), before)
    slot3 = (run_off + rank).reshape(ng, ge, TM)
    slots = [sum01(jnp.where(expert_id == ei, slot3, 0.0)).reshape(1, TM).astype(i32) for ei in top_e]
    for k in range(TOP_K):
        slots_ref[k:k + 1, :] = slots[k]
    slots_ref[TOP_K:8, :] = jnp.full((8 - TOP_K, TM), -1, i32)
    len_ref[...] = jnp.broadcast_to(run_len, (N_EXPERTS, LANES)).astype(i32)

    rows = ROUTE_ROWS

    def body(c, carry):
        base = pl.multiple_of(c * rows, rows)
        row_id = base + lax.broadcasted_iota(i32, (rows, TM), 0)
        p = jnp.zeros((rows, TM), f32)
        for k in range(TOP_K):
            p = jnp.where(row_id == slots[k], 1.0, p)
        p = p.astype(bf16)
        xloc_ref[pl.ds(base, rows), :] = _dot(p, h_hi).astype(bf16)
        return carry

    def zero_body(c, carry):
        base = pl.multiple_of(c * rows, rows)
        xloc_ref[pl.ds(base, rows), :] = jnp.zeros((rows, D_MODEL), bf16)
        return carry

    n_used = (jnp.sum(run_len).astype(i32) + (rows - 1)) // rows
    lax.fori_loop(0, n_used, body, 0)
    lax.fori_loop(n_used, SLOTS // rows, zero_body, 0)


def _route(x_parts, o_prompt, o_sample, mod_l, g, w_out, rwt, rb):
    per_tile = lambda i: (i, 0, 0)
    return pl.pallas_call(
        functools.partial(_route_kernel, n_x=len(x_parts)),
        grid=(N_TILES,),
        in_specs=_x_specs(x_parts) + [
            pl.BlockSpec((TM, D_MODEL), _prompt_block),
            pl.BlockSpec((TM, D_MODEL), _sample_block),
            pl.BlockSpec((None, 6, D_MODEL), lambda i: (_mod_row(i), 0, 0)),
            pl.BlockSpec((1, D_MODEL), lambda i: (0, 0)),
            pl.BlockSpec((D_MODEL, D_MODEL), lambda i: (0, 0)),
            pl.BlockSpec((N_EXPERTS, D_MODEL), lambda i: (0, 0)),
            pl.BlockSpec((N_EXPERTS, 1), lambda i: (0, 0)),
        ],
        out_specs=[
            pl.BlockSpec((TM, D_MODEL), lambda i: (i, 0)),
            pl.BlockSpec((SLOTS, D_MODEL), lambda i: (i, 0)),
            pl.BlockSpec((None, 8, TM), per_tile),
            pl.BlockSpec((None, 8, TM), per_tile),
            pl.BlockSpec((None, N_EXPERTS, LANES), per_tile),
        ],
        out_shape=[
            jax.ShapeDtypeStruct((N_TOK, D_MODEL), f32),
            jax.ShapeDtypeStruct((N_TILES * SLOTS, D_MODEL), bf16),
            jax.ShapeDtypeStruct((N_TILES, 8, TM), i32),
            jax.ShapeDtypeStruct((N_TILES, 8, TM), f32),
            jax.ShapeDtypeStruct((N_TILES, N_EXPERTS, LANES), i32),
        ],
        compiler_params=_cparams(("parallel",)),
        name="route",
    )(*x_parts, o_prompt, o_sample, mod_l, g, w_out, rwt, rb)


def _moe_plan(run_len):
    nt, ne = run_len.shape

    def excl_cumsum(a):
        n = a.shape[-1]
        earlier = np.arange(n)[None, :] < np.arange(n)[:, None]
        return jnp.sum(jnp.where(earlier, a[..., None, :], 0), axis=-1)

    def first_diff(a):
        return a - jnp.concatenate([jnp.zeros_like(a[..., :1]), a[..., :-1]], axis=-1)

    off_loc = excl_cumsum(run_len)
    before = excl_cumsum(run_len.T).T
    n_e = jnp.sum(run_len, axis=0)
    n_pad = -(-n_e // GM) * GM
    g_start = excl_cumsum(n_pad)
    total = jnp.sum(n_pad)
    run_dst = g_start[None, :] + before
    run_src = jnp.arange(nt, dtype=i32)[:, None] * SLOTS + off_loc
    dst_f = run_dst.T.reshape(-1)
    shift_f = first_diff((run_src - run_dst).T.reshape(-1))
    rows = jnp.arange(G_TILES * G_CHUNKS, dtype=i32) * CHUNK
    shift = jnp.sum(jnp.where(dst_f[None, :] <= rows[:, None], shift_f[None, :], 0), axis=1)
    in_run = jnp.any((g_start[None, :] <= rows[:, None]) & (rows[:, None] < (g_start + n_e)[None, :]), axis=1)
    chunk_src = jnp.where(in_run, rows + shift, 0).astype(i32)
    loc_rows = jnp.arange(SLOT_CHUNKS, dtype=i32) * CHUNK
    shift_l = first_diff(run_dst - off_loc)
    shift = jnp.sum(jnp.where(off_loc[:, None, :] <= loc_rows[None, :, None], shift_l[:, None, :], 0), axis=2)
    used = jnp.sum(run_len, axis=1)
    chunk_map = jnp.where(loc_rows[None, :] < used[:, None], (loc_rows[None, :] + shift) // CHUNK, 0).astype(i32)
    tile_start, n_tiles = g_start // GM, n_pad // GM
    n_items = -(-n_tiles // GMM_ITEM_TILES)
    item_start = excl_cumsum(n_items)
    items = jnp.arange(GMM_MAX_ITEMS, dtype=i32)
    owner = items[:, None] >= item_start[None, :]
    e_first = jnp.sum(jnp.where(owner, first_diff(tile_start - GMM_ITEM_TILES * item_start)[None, :], 0), axis=1)
    item_tile = e_first + GMM_ITEM_TILES * items
    e_end = jnp.sum(jnp.where(owner, first_diff(tile_start + n_tiles)[None, :], 0), axis=1)
    item_cnt = jnp.clip(e_end - item_tile, 0, GMM_ITEM_TILES)
    gmm_plan = tuple(a.astype(i32) for a in (item_start, n_items, item_tile, item_cnt, chunk_src))
    return gmm_plan, chunk_map.reshape(-1)


def _gmm_in_copy(xloc_hbm, xbuf, sem, src_row, slot, c):
    return pltpu.make_async_copy(xloc_hbm.at[pl.ds(src_row, CHUNK)],
                                 xbuf.at[slot, pl.ds(c * CHUNK, CHUNK)], sem.at[slot])


def _gmm_out_copy(ybuf, y_hbm, sem, tile, slot, n_tiles):
    rows = n_tiles * GM
    return pltpu.make_async_copy(ybuf.at[slot, pl.ds(0, rows)],
                                 y_hbm.at[pl.ds(pl.multiple_of(tile * GM, GM), rows)], sem.at[slot])


def _gmm_kernel(i0_ref, ni_ref, it_ref, ic_ref, cs_ref, xloc_hbm, wg_ref, wu_ref, wd_ref, y_hbm,
                xbuf, ybuf, zbuf, wg_b, wu_b, wd_b, in_sem, out_sem, zsem):
    e = pl.program_id(0)
    last = pl.num_programs(0) - 1
    n_items = ni_ref[e]
    first_item = i0_ref[e]
    total_items = i0_ref[last] + ni_ref[last]
    last_item = total_items - 1
    total_tiles = it_ref[last_item] + ic_ref[last_item]

    def for_tiles_of(item, fn):
        fn(0)
        for k in range(1, GMM_ITEM_TILES):
            @pl.when(ic_ref[item] > k)
            def _():
                fn(k)

    def start_in(item):
        def one(k):
            for c in range(G_CHUNKS):
                src = pl.multiple_of(cs_ref[(it_ref[item] + k) * G_CHUNKS + c], CHUNK)
                _gmm_in_copy(xloc_hbm, xbuf, in_sem, src, item % GMM_SLOTS, k * G_CHUNKS + c).start()
        for_tiles_of(item, one)

    def wait_in(item):
        def one(k):
            for c in range(G_CHUNKS):
                _gmm_in_copy(xloc_hbm, xbuf, in_sem, 0, item % GMM_SLOTS, k * G_CHUNKS + c).wait()
        for_tiles_of(item, one)

    def out_copy(item, fn):
        for cnt in range(1, GMM_ITEM_TILES + 1):
            @pl.when(ic_ref[item] == cnt)
            def _():
                fn(_gmm_out_copy(ybuf, y_hbm, out_sem, it_ref[item], item % GMM_SLOTS, cnt))

    @pl.when(e == 0)
    def _():
        for item in range(GMM_SLOTS - 1):
            start_in(item)
        zbuf[...] = jnp.zeros(zbuf.shape, zbuf.dtype)

    def tail_copies(fn):
        for j in range(_GMM_TAIL_PER_STEP):
            tile = total_tiles + e + j * N_EXPERTS

            @pl.when(tile < G_TILES)
            def _():
                fn(pltpu.make_async_copy(zbuf, y_hbm.at[pl.ds(pl.multiple_of(tile * GM, GM), GM)], zsem.at[0]))

    tail_copies(lambda cp: cp.start())

    @pl.when(n_items > 0)
    def _():
        wg_b[...] = wg_ref[...].astype(bf16)
        wu_b[...] = wu_ref[...].astype(bf16)
        wd_b[...] = wd_ref[...].astype(bf16)

    def body(j, carry):
        item = first_item + j
        slot = item % GMM_SLOTS

        @pl.when(item + (GMM_SLOTS - 1) < total_items)
        def _():
            start_in(item + (GMM_SLOTS - 1))

        wait_in(item)

        @pl.when(item >= GMM_SLOTS)
        def _():
            out_copy(item - GMM_SLOTS, lambda cp: cp.wait())

        for cnt in range(1, GMM_ITEM_TILES + 1):
            @pl.when(ic_ref[item] == cnt)
            def _():
                rows = cnt * GM
                x = xbuf[slot, 0:rows, :]
                act = _silu(_dot(x, wg_b[...])) * _dot(x, wu_b[...])
                ybuf[slot, 0:rows, :] = _dot(act.astype(bf16), wd_b[...]).astype(bf16)

        out_copy(item, lambda cp: cp.start())
        return carry

    lax.fori_loop(0, n_items, body, 0)
    tail_copies(lambda cp: cp.wait())

    @pl.when(e == last)
    def _():
        for back in range(1, GMM_SLOTS + 1):
            out_copy(total_items - back, lambda cp: cp.wait())


_GMM_TAIL_PER_STEP = -(-(G_TILES - TM * TOP_K * N_TILES // GM) // N_EXPERTS)


def _gmm(plan, xloc, wg, wu, wd, layer):
    rows = GMM_ITEM_TILES * GM
    w_idx = lambda e, *_: (layer, e, 0, 0)
    grid_spec = pltpu.PrefetchScalarGridSpec(
        num_scalar_prefetch=5,
        grid=(N_EXPERTS,),
        in_specs=[
            pl.BlockSpec(memory_space=pl.ANY),
            pl.BlockSpec((None, None, D_MODEL, EXPERT_DIM), w_idx),
            pl.BlockSpec((None, None, D_MODEL, EXPERT_DIM), w_idx),
            pl.BlockSpec((None, None, EXPERT_DIM, D_MODEL), w_idx),
        ],
        out_specs=pl.BlockSpec(memory_space=pl.ANY),
        scratch_shapes=[pltpu.VMEM((GMM_SLOTS, rows, D_MODEL), bf16), pltpu.VMEM((GMM_SLOTS, rows, D_MODEL), bf16),
                        pltpu.VMEM((GM, D_MODEL), bf16),
                        pltpu.VMEM((D_MODEL, EXPERT_DIM), bf16), pltpu.VMEM((D_MODEL, EXPERT_DIM), bf16),
                        pltpu.VMEM((EXPERT_DIM, D_MODEL), bf16),
                        pltpu.SemaphoreType.DMA((GMM_SLOTS,)), pltpu.SemaphoreType.DMA((GMM_SLOTS,)),
                        pltpu.SemaphoreType.DMA((1,))],
    )
    return pl.pallas_call(
        _gmm_kernel,
        grid_spec=grid_spec,
        out_shape=jax.ShapeDtypeStruct((G_TILES * GM, D_MODEL), bf16),
        compiler_params=_cparams(("arbitrary",)),
        name="gmm",
    )(*plan, xloc, wg, wu, wd)


def _combine_copy(y_hbm, ybuf, sem, sorted_chunk, slot, c):
    return pltpu.make_async_copy(y_hbm.at[pl.ds(pl.multiple_of(sorted_chunk * CHUNK, CHUNK), CHUNK)],
                                 ybuf.at[slot, pl.ds(pl.multiple_of(c * CHUNK, CHUNK), CHUNK)], sem.at[slot])


def _combine_kernel(cm_ref, y_hbm, slots_ref, gate_ref, x_ref, mod_ref, g_ref, sg_ref, su_ref, sd_ref, *rest, final):
    if final:
        gf_ref, yp_ref, ys_ref, ybuf, sem = rest
    else:
        o_ref, ybuf, sem = rest
    i = pl.program_id(0)
    n = pl.num_programs(0)
    slot = i % 2

    def start(tile, s):
        for c in range(SLOT_CHUNKS):
            _combine_copy(y_hbm, ybuf, sem, cm_ref[tile * SLOT_CHUNKS + c], s, c).start()

    def wait(s):
        for c in range(SLOT_CHUNKS):
            _combine_copy(y_hbm, ybuf, sem, 0, s, c).wait()

    @pl.when(i == 0)
    def _():
        start(0, 0)

    wait(slot)
    start((i + 1) % n, 1 - slot)

    x = x_ref[...]
    hb = _norm_mod(x, g_ref[...], mod_ref[4:5, :], mod_ref[3:4, :]).astype(bf16)
    shared = _dot((_silu(_dot(hb, sg_ref[...])) * _dot(hb, su_ref[...])).astype(bf16), sd_ref[...])
    row_id = lax.broadcasted_iota(i32, (SLOTS, TM), 0)
    p = jnp.zeros((SLOTS, TM), f32)
    for k in range(TOP_K):
        p = jnp.where(row_id == slots_ref[k:k + 1, :], gate_ref[k:k + 1, :], p)
    routed = lax.dot_general(p.astype(bf16), ybuf[slot], (((0,), (0,)), ((), ())), preferred_element_type=f32)
    out = x + mod_ref[5:6, :] * (routed + shared)
    if final:
        y = (out * lax.rsqrt(jnp.mean(out * out, axis=-1, keepdims=True) + EPS)) * gf_ref[...]

        @pl.when(i < N_PROMPT_TILES)
        def _():
            yp_ref[...] = y

        @pl.when(i >= N_PROMPT_TILES)
        def _():
            ys_ref[...] = y
    else:
        o_ref[...] = out

    @pl.when(i == n - 1)
    def _():
        wait(1 - slot)


def _combine(chunk_map, y, slots, gates, x, mod_l, g, sg, su, sd, final_g=None):
    shd = sg.shape[1]
    final = final_g is not None
    row_spec = pl.BlockSpec((TM, D_MODEL), lambda i, cm: (i, 0))
    vec_spec = pl.BlockSpec((1, D_MODEL), lambda i, cm: (0, 0))
    if final:
        out_specs = [pl.BlockSpec((TM, D_MODEL), lambda i, cm: _prompt_block(i)),
                     pl.BlockSpec((TM, D_MODEL), lambda i, cm: _sample_block(i))]
        out_shape = [jax.ShapeDtypeStruct((N_PROMPT, D_MODEL), f32), jax.ShapeDtypeStruct((N_SAMPLE, D_MODEL), f32)]
    else:
        out_specs, out_shape = row_spec, jax.ShapeDtypeStruct((N_TOK, D_MODEL), f32)
    grid_spec = pltpu.PrefetchScalarGridSpec(
        num_scalar_prefetch=1,
        grid=(N_TILES,),
        in_specs=[
            pl.BlockSpec(memory_space=pl.ANY),
            pl.BlockSpec((None, 8, TM), lambda i, cm: (i, 0, 0)),
            pl.BlockSpec((None, 8, TM), lambda i, cm: (i, 0, 0)),
            row_spec,
            pl.BlockSpec((None, 6, D_MODEL), lambda i, cm: (_mod_row(i), 0, 0)),
            vec_spec,
            pl.BlockSpec((D_MODEL, shd), lambda i, cm: (0, 0)),
            pl.BlockSpec((D_MODEL, shd), lambda i, cm: (0, 0)),
            pl.BlockSpec((shd, D_MODEL), lambda i, cm: (0, 0)),
        ] + ([vec_spec] if final else []),
        out_specs=out_specs,
        scratch_shapes=[pltpu.VMEM((2, SLOTS, D_MODEL), bf16), pltpu.SemaphoreType.DMA((2,))],
    )
    args = (chunk_map, y, slots, gates, x, mod_l, g, sg, su, sd) + ((final_g,) if final else ())
    return pl.pallas_call(
        functools.partial(_combine_kernel, final=final),
        grid_spec=grid_spec,
        out_shape=out_shape,
        compiler_params=_cparams(("arbitrary",)),
        name="combine",
    )(*args)


def _moe(x_parts, o_prompt, o_sample, w_out, mod_l, g, rwt, rb, wg, wu, wd, layer, sg, su, sd, final_g=None):
    x, xloc, slots, gates, run_len = _route(x_parts, o_prompt, o_sample, mod_l, g, w_out, rwt, rb)
    gmm_plan, chunk_map = _moe_plan(run_len[:, :, 0])
    y = _gmm(gmm_plan, xloc, wg, wu, wd, layer)
    return _combine(chunk_map, y, slots, gates, x, mod_l, g, sg, su, sd, final_g)


_L0_CHUNKS = (
    (0, 512, 0, (0, 1, 2, 3), ()),
    (512, 1024, 768, (), ()),
    (1024, 1536, 1280, (), ((2, 0, 512, 0, False),)),
    (1536, 2048, 1792, (), ((3, 0, 512, 0, False),)),
    (2048, 2304, 512, (0,), ((0, 0, 128, 0, False), (1, 128, 256, 0, False))),
)
_L0_KV_OUTS = (("T", 128), ("T", 128), ("T", 512), ("T", 512))
_L1_CHUNKS = (
    (0, 512, 0, (0, 1, 2, 3), ()),
    (512, 1024, 512, (0, 1, 2, 3), ()),
    (1024, 1536, 1024, (0, 1, 2, 3), ((0, 0, 512, 0, False),)),
    (1536, 2048, 1536, (0, 1, 2, 3), ((0, 0, 512, 512, False),)),
    (2048, 2560, 2048, (), ((1, 0, 512, 0, True),)),
    (2560, 3072, 2560, (), ((1, 0, 512, 512, True),)),
)
_L1_KV_OUTS = (("T", 1024), ("H", 8))


def _from_feature_major(kt, *head_dims):
    nb, _, s = kt.shape
    nd = len(head_dims)
    return kt.reshape(nb, *head_dims, s).transpose(0, nd + 1, *range(1, nd + 1))[:, None]


def kernel(x_prompt, x_sample, cache_a_k, cache_a_v, cache_b_k, cache_b_v, cache_c_k, cache_c_v, c, c_ctx, w_mod, b_mod, norm_mix, norm_ffn, w_in_ab, w_out_ab, sink_a, rel_bias_b, w_in_c, w_out_c, lam_q1, lam_k1, lam_q2, lam_k2, subln_c, router_w, router_bias, exp_w_gate, exp_w_up, exp_w_down, sh_w_gate, sh_w_up, sh_w_down, final_norm):
    x = (x_prompt.reshape(N_PROMPT, D_MODEL), x_sample.reshape(N_SAMPLE, D_MODEL))
    cond8 = jnp.concatenate([c_ctx[None, :], c, jnp.zeros((8 - 1 - N_SAMPLE_BATCH, D_MODEL), f32)], axis=0)
    mod = _adaln(cond8, w_mod, b_mod).reshape(DEPTH, 8, 6, D_MODEL)
    rope_tabs = _rope_tables()
    new_kv = {}
    for layer in range(DEPTH):
        li = layer // 2
        mod_l = mod[layer]
        g_mix = norm_mix[layer][None, :]
        g_ffn = norm_ffn[layer][None, :]
        if layer % 2 == 0:
            w_in = w_in_ab[li].astype(bf16)
            qkv, ak, av, bk, bv = _inproj(x, mod_l, g_mix, w_in, rope_tabs, _L0_CHUNKS, _L0_KV_OUTS)
            new_kv["a_k"], new_kv["a_v"], new_kv["b_k"], new_kv["b_v"] = ak, av, bk, bv
            o_p = _ctx0(sink_a[li], qkv)
            o_s = _lat0(sink_a[li], qkv,
                        cache_a_k[:, li].reshape(N_SAMPLE_BATCH, PAST_LEN, LANES),
                        cache_a_v[:, li].reshape(N_SAMPLE_BATCH, PAST_LEN, LANES),
                        cache_b_k[:, li].reshape(N_SAMPLE_BATCH, PAST_LEN, 512),
                        cache_b_v[:, li].reshape(N_SAMPLE_BATCH, PAST_LEN, 512),
                        _na_bias_tiles(rel_bias_b[li]))
            w_out = w_out_ab[li].astype(bf16)
        else:
            lam_init = 0.8 - 0.6 * math.exp(-0.3 * layer)
            qkv, ck, cv = _inproj(x, mod_l, g_mix, w_in_c[li].astype(bf16), rope_tabs, _L1_CHUNKS, _L1_KV_OUTS)
            new_kv["c_k"], new_kv["c_v"] = ck, cv
            lamv = jnp.concatenate([lam_q1[li][None], lam_k1[li][None], lam_q2[li][None], lam_k2[li][None],
                                    jnp.zeros((4, HEAD_DIM), f32)], axis=0)
            subln = subln_c[li][None, :]
            o_p = _ctx1(lamv, subln, qkv, lam_init)
            o_s = _lat1(lamv, subln, qkv,
                        cache_c_k[:, li].reshape(N_SAMPLE_BATCH, PAST_LEN, D_MODEL),
                        cache_c_v[:, li].reshape(N_SAMPLE_BATCH, PAST_LEN, D_MODEL), lam_init)
            w_out = w_out_c[li].astype(bf16)
        last = layer == DEPTH - 1
        x = _moe(x, o_p, o_s, w_out, mod_l, g_ffn, router_w[layer].T, router_bias[layer][:, None],
                 exp_w_gate, exp_w_up, exp_w_down, layer,
                 sh_w_gate[layer].astype(bf16), sh_w_up[layer].astype(bf16), sh_w_down[layer].astype(bf16),
                 final_norm[None, :] if last else None)
        x = x if last else (x,)
    y_prompt, y_sample = x
    nb, s = N_PROMPT_BATCH, PROMPT_SEQ
    return (y_prompt.reshape(nb, s, D_MODEL), y_sample.reshape(N_SAMPLE_BATCH, SAMPLE_SEQ, D_MODEL),
            _from_feature_major(new_kv["a_k"], 2, HEAD_DIM), _from_feature_major(new_kv["a_v"], 2, HEAD_DIM),
            _from_feature_major(new_kv["b_k"], 8, HEAD_DIM), _from_feature_major(new_kv["b_v"], 8, HEAD_DIM),
            _from_feature_major(new_kv["c_k"], 8, 2, HEAD_DIM), new_kv["c_v"].reshape(nb, 1, s, 8, 2 * HEAD_DIM))
```

```python
import functools
import math

import jax
import jax.numpy as jnp
import numpy as np
from jax import lax
from jax.experimental import pallas as pl
from jax.experimental.pallas import tpu as pltpu

f32 = jnp.float32
bf16 = jnp.bfloat16
i32 = jnp.int32

D_MODEL = 1024
N_PROMPT_BATCH = 16
PROMPT_SEQ = 256
DEPTH = 2
N_SAMPLE_BATCH = 2
SAMPLE_SEQ = 2048
PAST_LEN = 512
GRID_W = 64
HEAD_DIM = 64
ROPE_THETA = 10000.0
EPS = 1e-6
A_WINDOW = 128
NA_ROWS = 8
NA_COLS = 16
N_EXPERTS = 64
TOP_K = 6
N_GROUPS = 8
TOPK_GROUPS = 4
EXPERT_DIM = 256
ROUTED_SCALE = 2.5
Q_SCALE = HEAD_DIM ** -0.5

N_PROMPT = N_PROMPT_BATCH * PROMPT_SEQ
N_SAMPLE = N_SAMPLE_BATCH * SAMPLE_SEQ
N_TOK = N_PROMPT + N_SAMPLE

LANES = 128
TM = 256
TD = 512
N_PROMPT_TILES = N_PROMPT // TM
N_TILES = N_TOK // TM
QB = 128
CHUNK = 16
SLOTS = -(-(TM * TOP_K + N_EXPERTS * (CHUNK - 1)) // 256) * 256
SLOT_CHUNKS = SLOTS // CHUNK
ROUTE_ROWS = 512
ROUTE_TILES = 2
GM = 256
_MAX_SORTED = TM * TOP_K * N_TILES + N_TILES * N_EXPERTS * (CHUNK - 1) + N_EXPERTS * (GM - CHUNK)
G_TILES = -(-_MAX_SORTED // GM)
G_CHUNKS = GM // CHUNK
GMM_ITEM_TILES = 2
GMM_SLOTS = 4
GMM_MAX_ITEMS = (G_TILES + N_EXPERTS * (GMM_ITEM_TILES - 1)) // GMM_ITEM_TILES
VMEM_LIMIT = 56 * 1024 * 1024

NEG = -1e30


def _cparams(sem):
    return pltpu.CompilerParams(dimension_semantics=sem, vmem_limit_bytes=VMEM_LIMIT)


def _mod_row(i, tm=TM):
    return jnp.where(i < N_PROMPT // tm, 0, 1 + (i - N_PROMPT // tm) // (SAMPLE_SEQ // tm))


def _prompt_block(i, tm=TM):
    return (jnp.minimum(i, N_PROMPT // tm - 1), 0)


def _sample_block(i, tm=TM):
    return (jnp.maximum(i - N_PROMPT // tm, 0), 0)


def _x_specs(parts, tm=TM):
    if len(parts) == 1:
        return [pl.BlockSpec((tm, D_MODEL), lambda i, *_: (i, 0))]
    return [pl.BlockSpec((tm, D_MODEL), lambda i, *_: _prompt_block(i, tm)),
            pl.BlockSpec((tm, D_MODEL), lambda i, *_: _sample_block(i, tm))]


def _load_x(i, x_refs, tm=TM):
    if len(x_refs) == 1:
        return x_refs[0][...]
    return jnp.where(i < N_PROMPT // tm, x_refs[0][...], x_refs[1][...])


def _norm_mod(x, g, scale, shift):
    y = x * lax.rsqrt(jnp.mean(x * x, axis=-1, keepdims=True) + EPS)
    return (y * g) * (1.0 + scale) + shift


def _silu(x):
    return x * jax.nn.sigmoid(x)


def _dot(a, b):
    return jnp.dot(a, b, preferred_element_type=f32)


def _dot_nt(a, b):
    return lax.dot_general(a, b, (((1,), (1,)), ((), ())), preferred_element_type=f32)


ADA_COLS = 1536


def _adaln_kernel(cond_ref, w_ref, b_ref, o_ref):
    s = _silu(cond_ref[...]).astype(bf16)
    o_ref[...] = _dot(s, w_ref[...].astype(bf16)) + b_ref[...]


def _adaln(cond8, w_mod, b_mod):
    n6 = 6 * D_MODEL
    return pl.pallas_call(
        _adaln_kernel,
        grid=(DEPTH, n6 // ADA_COLS),
        in_specs=[
            pl.BlockSpec((8, D_MODEL), lambda l, j: (0, 0)),
            pl.BlockSpec((None, D_MODEL, ADA_COLS), lambda l, j: (l, 0, j)),
            pl.BlockSpec((None, 1, ADA_COLS), lambda l, j: (l, 0, j)),
        ],
        out_specs=pl.BlockSpec((None, 8, ADA_COLS), lambda l, j: (l, 0, j)),
        out_shape=jax.ShapeDtypeStruct((DEPTH, 8, n6), f32),
        compiler_params=_cparams(("parallel", "parallel")),
        name="adaln",
    )(cond8, w_mod, b_mod.reshape(DEPTH, 1, n6))


def _rope_block(blk, cos, sin_a, sin_b):
    return blk * cos + pltpu.roll(blk, LANES - 16, 1) * sin_a + pltpu.roll(blk, 16, 1) * sin_b


def _inproj_kernel(*refs, chunks, n_x):
    x_refs, kv_refs = refs[:n_x], refs[n_x + 7:]
    mod_ref, g_ref, w_ref, cos_ref, sa_ref, sb_ref, qkv_ref = refs[n_x:n_x + 7]
    i = pl.program_id(0)
    h = _norm_mod(_load_x(i, x_refs, TD), g_ref[...], mod_ref[1:2, :], mod_ref[0:1, :]).astype(bf16)
    is_prompt = i < N_PROMPT // TD

    @pl.when(is_prompt)
    def _():
        for c0, c1, s0, _, kv_out in chunks:
            acc = _dot(h, w_ref[:, s0:s0 + (c1 - c0)])
            qkv_ref[:, c0:c1] = acc.astype(bf16)
            for ridx, a0, a1, o0, per_head in kv_out:
                if per_head:
                    heads = kv_refs[ridx].shape[0] // TD
                    for j in range((a1 - a0) // LANES):
                        kv_refs[ridx][pl.ds(o0 // LANES + j, TD, stride=heads), :] = (
                            acc[:, a0 + j * LANES:a0 + (j + 1) * LANES])
                else:
                    t = acc[:, a0:a1].T
                    for b in range(TD // PROMPT_SEQ):
                        kv_refs[ridx][b, o0:o0 + (a1 - a0), :] = t[:, b * PROMPT_SEQ:(b + 1) * PROMPT_SEQ]

    @pl.when(jnp.logical_not(is_prompt))
    def _():
        cos, sa, sb = cos_ref[...], sa_ref[...], sb_ref[...]
        for c0, c1, s0, rope_blocks, _ in chunks:
            acc = _dot(h, w_ref[:, s0:s0 + (c1 - c0)])
            for b in range((c1 - c0) // LANES):
                blk = acc[:, b * LANES:(b + 1) * LANES]
                if b in rope_blocks:
                    blk = _rope_block(blk, cos, sa, sb)
                qkv_ref[:, c0 + b * LANES:c0 + (b + 1) * LANES] = blk.astype(bf16)


def _inproj(x_parts, mod_l, g, w, rope_tabs, chunks, kv_outs):
    n = w.shape[1]
    cos, sa, sb = rope_tabs
    bpt = TD // PROMPT_SEQ

    def rope_idx(i):
        return (jnp.where(i < N_PROMPT // TD, 0, (i - N_PROMPT // TD) % (SAMPLE_SEQ // TD)), 0)

    kv_specs, kv_shapes = [], []
    for kind, size in kv_outs:
        if kind == "T":
            kv_specs.append(pl.BlockSpec((bpt, size, PROMPT_SEQ), lambda i: _prompt_block(i, TD) + (0,)))
            kv_shapes.append(jax.ShapeDtypeStruct((N_PROMPT_BATCH, size, PROMPT_SEQ), f32))
        else:
            kv_specs.append(pl.BlockSpec((TD * size, LANES), lambda i: _prompt_block(i, TD)))
            kv_shapes.append(jax.ShapeDtypeStruct((N_PROMPT * size, LANES), f32))

    return pl.pallas_call(
        functools.partial(_inproj_kernel, chunks=chunks, n_x=len(x_parts)),
        grid=(N_TOK // TD,),
        in_specs=_x_specs(x_parts, TD) + [
            pl.BlockSpec((None, 6, D_MODEL), lambda i: (_mod_row(i, TD), 0, 0)),
            pl.BlockSpec((1, D_MODEL), lambda i: (0, 0)),
            pl.BlockSpec((D_MODEL, n), lambda i: (0, 0)),
            pl.BlockSpec((TD, LANES), rope_idx),
            pl.BlockSpec((TD, LANES), rope_idx),
            pl.BlockSpec((TD, LANES), rope_idx),
        ],
        out_specs=[pl.BlockSpec((TD, n), lambda i: (i, 0))] + kv_specs,
        out_shape=[jax.ShapeDtypeStruct((N_TOK, n), bf16)] + kv_shapes,
        compiler_params=_cparams(("arbitrary",)),
        name="inproj",
    )(*x_parts, mod_l, g, w, cos, sa, sb)


def _rope_tables():
    nq = HEAD_DIM // 4
    t = np.arange(SAMPLE_SEQ)
    inv = np.power(np.float32(ROPE_THETA), -np.arange(nq, dtype=np.float32) / np.float32(nq))
    ang_r = (t // GRID_W).astype(np.float32)[:, None] * inv
    ang_c = (t % GRID_W).astype(np.float32)[:, None] * inv
    zero = np.zeros_like(ang_r)

    def head(fr, fc):
        return np.concatenate([fr[0], fr[1], fc[0], fc[1]], axis=-1)

    cos = head((np.cos(ang_r), np.cos(ang_r)), (np.cos(ang_c), np.cos(ang_c)))
    sin_a = head((-np.sin(ang_r), zero), (-np.sin(ang_c), zero))
    sin_b = head((zero, np.sin(ang_r)), (zero, np.sin(ang_c)))
    two = lambda a: jnp.asarray(np.concatenate([a, a], axis=-1).astype(np.float32))
    return two(cos), two(sin_a), two(sin_b)


def _lane_lo(shape):
    return lax.broadcasted_iota(i32, shape, len(shape) - 1) < HEAD_DIM


def _half(q, lo_mask, half):
    keep = lo_mask if half == 0 else jnp.logical_not(lo_mask)
    return jnp.where(keep, q, jnp.zeros_like(q)) * Q_SCALE


def _swap_halves(x):
    return pltpu.roll(x.astype(f32), HEAD_DIM, 1).astype(x.dtype)


def _stack_halves(q, lo_mask):
    return jnp.concatenate([_half(q, lo_mask, 0), _half(q, lo_mask, 1)], axis=0)


def _with_ones(v):
    return jnp.concatenate([v, jnp.ones_like(v)], axis=1)


def _attend(q_rows, n_heads, key_blocks, vx_blocks, fix_scores=None, sinks=None):
    r = q_rows.shape[0] // n_heads
    scores = [_dot_nt(q_rows, k) for k in key_blocks]
    exps = [[] for _ in key_blocks]
    maxes = []
    for h in range(n_heads):
        blocks = [s[h * r:(h + 1) * r] for s in scores]
        if fix_scores is not None:
            blocks = [fix_scores(h, i, s) for i, s in enumerate(blocks)]
        m = functools.reduce(jnp.maximum, [jnp.max(s, axis=-1, keepdims=True) for s in blocks])
        if sinks is not None:
            m = jnp.maximum(m, sinks[h])
        maxes.append(m)
        for i, s in enumerate(blocks):
            exps[i].append(jnp.exp((s - m).astype(bf16)))
    out = functools.reduce(lambda a, b: a + b,
                           [_dot(e[0] if n_heads == 1 else jnp.concatenate(e, axis=0), vx)
                            for e, vx in zip(exps, vx_blocks)])
    outs = []
    for h in range(n_heads):
        den = out[h * r:(h + 1) * r, LANES:]
        if sinks is not None:
            den = den + jnp.exp(sinks[h] - maxes[h])
        outs.append(out[h * r:(h + 1) * r, :LANES] * (1.0 / den))
    return outs


def _gqa_rows(q_blocks, group, lo_mask):
    parts = []
    for q in q_blocks:
        for half in range(2):
            qh = _half(q, lo_mask, half)
            parts.append(qh if half == group else _swap_halves(qh))
    return jnp.concatenate(parts, axis=0)


def _gqa_merge(outs, group, lo_mask):
    fixed = [o if idx % 2 == group else pltpu.roll(o, HEAD_DIM, 1) for idx, o in enumerate(outs)]
    return [jnp.where(lo_mask, fixed[2 * p], fixed[2 * p + 1]) for p in range(len(outs) // 2)]


L0_QA, L0_QB, L0_KB, L0_VB, L0_KA, L0_VA, L0_N = 0, 512, 1024, 1536, 2048, 2176, 2304


def _ctx0_kernel(sink_ref, qkv_ref, o_ref):
    lo = _lane_lo((1, LANES))
    blk = lambda base, j: qkv_ref[:, base + j * LANES:base + (j + 1) * LANES]
    k_a = blk(L0_KA, 0)
    vx_a = _with_ones(blk(L0_VA, 0))
    for g in range(2):
        q_rows = _gqa_rows([blk(L0_QA, 2 * g), blk(L0_QA, 2 * g + 1)], g, lo)
        outs = _attend(q_rows, 4, [k_a], [vx_a], sinks=[sink_ref[4 * g + idx] for idx in range(4)])
        for p, o in enumerate(_gqa_merge(outs, g, lo)):
            j = 2 * g + p
            o_ref[:, j * LANES:(j + 1) * LANES] = o.astype(bf16)
    for j in range(4):
        outs = _attend(_stack_halves(blk(L0_QB, j), lo), 2, [blk(L0_KB, j)], [_with_ones(blk(L0_VB, j))])
        o_ref[:, 512 + j * LANES:512 + (j + 1) * LANES] = jnp.where(lo, outs[0], outs[1]).astype(bf16)


def _ctx0(sink, qkv):
    return pl.pallas_call(
        _ctx0_kernel,
        grid=(N_PROMPT_BATCH,),
        in_specs=[
            pl.BlockSpec(memory_space=pltpu.SMEM),
            pl.BlockSpec((PROMPT_SEQ, L0_N), lambda b: (b, 0)),
        ],
        out_specs=pl.BlockSpec((PROMPT_SEQ, D_MODEL), lambda b: (b, 0)),
        out_shape=jax.ShapeDtypeStruct((N_PROMPT, D_MODEL), bf16),
        compiler_params=_cparams(("parallel",)),
        name="ctx0",
    )(sink, qkv)


WIN_KEYS = 3 * QB
NA_KEY_ROWS = 10
NA_KEYS = NA_KEY_ROWS * GRID_W
N_QB = SAMPLE_SEQ // QB
N_NA_PATTERNS = 5
_PROMPT_QBLOCKS = N_PROMPT // QB


def _na_pattern(n):
    return jnp.where(n < 2, n, jnp.where(n > N_QB - 3, n - (N_QB - 5), 2))


def _lat0_kernel(sink_ref, roff_ref, q_ref, kvb_ref, kva_ref, cak_ref, cav_ref, cbk_ref, cbv_ref, tiles_ref, o_ref):
    n = pl.program_id(1)
    lo = _lane_lo((1, LANES))
    kstart = pl.multiple_of(jnp.clip((n - 1) * QB, 0, SAMPLE_SEQ - WIN_KEYS), QB)
    k_a = kva_ref[pl.ds(kstart, WIN_KEYS), 0:LANES]
    v_a = kva_ref[pl.ds(kstart, WIN_KEYS), LANES:2 * LANES]
    c_k = cak_ref[...].astype(bf16)
    keys_a = [c_k, k_a]
    vx_a = [_with_ones(cav_ref[...].astype(bf16)), _with_ones(v_a)]
    qpos = n * QB + lax.broadcasted_iota(i32, (QB, WIN_KEYS), 0)
    kpos = kstart + lax.broadcasted_iota(i32, (QB, WIN_KEYS), 1)
    in_window = jnp.abs(qpos - kpos) <= A_WINDOW
    mask_window = lambda h, i, s: jnp.where(in_window, s, NEG) if i == 1 else s
    for g in range(2):
        q_rows = _gqa_rows([q_ref[:, L0_QA + j * LANES:L0_QA + (j + 1) * LANES] for j in (2 * g, 2 * g + 1)], g, lo)
        outs = _attend(q_rows, 4, keys_a, vx_a, fix_scores=mask_window,
                       sinks=[sink_ref[4 * g + idx] for idx in range(4)])
        for p, o in enumerate(_gqa_merge(outs, g, lo)):
            j = 2 * g + p
            o_ref[:, j * LANES:(j + 1) * LANES] = o.astype(bf16)
    krow = jnp.clip(2 * n - NA_ROWS // 2, 0, SAMPLE_SEQ // GRID_W - NA_KEY_ROWS)
    ktok = pl.multiple_of(krow * GRID_W, QB)
    pattern = _na_pattern(n)

    def na_bias(head):
        rows = []
        for rq in range(QB // GRID_W):
            blocks = []
            for kb in range(NA_KEY_ROWS // 2):
                d0, d1 = (roff_ref[(pattern * 2 + rq) * NA_KEY_ROWS + 2 * kb + t] for t in range(2))
                blocks.append(jnp.where(lo, tiles_ref[head, d0], tiles_ref[head, d1]))
            rows.append(jnp.concatenate(blocks, axis=1))
        return jnp.concatenate(rows, axis=0)

    for j in range(4):
        q_b = q_ref[:, L0_QB + j * LANES:L0_QB + (j + 1) * LANES]
        k_b = kvb_ref[pl.ds(ktok, NA_KEYS), j * LANES:(j + 1) * LANES]
        v_b = kvb_ref[pl.ds(ktok, NA_KEYS), 512 + j * LANES:512 + (j + 1) * LANES]
        cb_k = cbk_ref[:, j * LANES:(j + 1) * LANES].astype(bf16)
        cb_v = cbv_ref[:, j * LANES:(j + 1) * LANES].astype(bf16)
        add_bias = lambda h, i, s, j=j: s + na_bias(2 * j + h) if i == 1 else s
        outs = _attend(_stack_halves(q_b, lo), 2, [cb_k, k_b], [_with_ones(cb_v), _with_ones(v_b)],
                       fix_scores=add_bias)
        o_ref[:, 512 + j * LANES:512 + (j + 1) * LANES] = jnp.where(lo, outs[0], outs[1]).astype(bf16)


def _lat0(sink, qkv, cak, cav, cbk, cbv, bias_tiles):
    sb = N_PROMPT // SAMPLE_SEQ
    return pl.pallas_call(
        _lat0_kernel,
        grid=(N_SAMPLE_BATCH, N_QB),
        in_specs=[
            pl.BlockSpec(memory_space=pltpu.SMEM),
            pl.BlockSpec(memory_space=pltpu.SMEM),
            pl.BlockSpec((QB, 1024), lambda b, n: (_PROMPT_QBLOCKS + b * N_QB + n, 0)),
            pl.BlockSpec((SAMPLE_SEQ, 1024), lambda b, n: (sb + b, 1)),
            pl.BlockSpec((SAMPLE_SEQ, 256), lambda b, n: (sb + b, L0_KA // 256)),
            pl.BlockSpec((None, PAST_LEN, LANES), lambda b, n: (b, 0, 0)),
            pl.BlockSpec((None, PAST_LEN, LANES), lambda b, n: (b, 0, 0)),
            pl.BlockSpec((None, PAST_LEN, 512), lambda b, n: (b, 0, 0)),
            pl.BlockSpec((None, PAST_LEN, 512), lambda b, n: (b, 0, 0)),
            pl.BlockSpec((8, N_ROW_OFFSETS + 1, GRID_W, LANES), lambda b, n: (0, 0, 0, 0)),
        ],
        out_specs=pl.BlockSpec((QB, D_MODEL), lambda b, n: (b * N_QB + n, 0)),
        out_shape=jax.ShapeDtypeStruct((N_SAMPLE, D_MODEL), bf16),
        compiler_params=_cparams(("parallel", "arbitrary")),
        name="lat0",
    )(sink, jnp.asarray(_na_row_offsets()), qkv, qkv, qkv, cak, cav, cbk, cbv, bias_tiles)


N_ROW_OFFSETS = 2 * NA_ROWS - 1


def _na_row_offsets():
    rows = SAMPLE_SEQ // GRID_W
    idx = np.full((N_NA_PATTERNS, 2, NA_KEY_ROWS), N_ROW_OFFSETS, np.int32)
    for p, n in enumerate((0, 1, 2, N_QB - 2, N_QB - 1)):
        k0 = int(np.clip(2 * n - NA_ROWS // 2, 0, rows - NA_KEY_ROWS))
        for rq in range(2):
            r = 2 * n + rq
            rs = int(np.clip(r - NA_ROWS // 2, 0, rows - NA_ROWS))
            for kl in range(NA_KEY_ROWS):
                if rs <= k0 + kl < rs + NA_ROWS:
                    idx[p, rq, kl] = k0 + kl - r + NA_ROWS - 1
    return idx.reshape(-1)


def _na_bias_tiles(rel_bias):
    n_dc = 2 * NA_COLS - 1
    c = np.arange(GRID_W)[:, None]
    kc = np.arange(GRID_W)[None, :]
    cs = np.clip(c - NA_COLS // 2, 0, GRID_W - NA_COLS)
    col_ok = (kc >= cs) & (kc < cs + NA_COLS)
    col_hot = ((kc - c + NA_COLS - 1)[None] == np.arange(n_dc)[:, None, None]) & col_ok[None]
    hp = lax.Precision.HIGHEST
    tiles = jnp.einsum("hdx,xck->hdck", rel_bias.astype(f32), col_hot.astype(np.float32), precision=hp)
    tiles = tiles + np.where(col_ok, 0.0, NEG).astype(np.float32)
    tiles = jnp.concatenate([tiles, jnp.full((tiles.shape[0], 1, GRID_W, GRID_W), NEG, f32)], axis=1)
    return jnp.concatenate([tiles, tiles], axis=-1)


def _diff_lambda(lam_ref, lam_init):
    lv = lam_ref[...]
    s1 = jnp.sum(lv[0:1, :] * lv[1:2, :], axis=-1, keepdims=True)
    s2 = jnp.sum(lv[2:3, :] * lv[3:4, :], axis=-1, keepdims=True)
    return jnp.exp(s1) - jnp.exp(s2) + lam_init


def _diff_head(q, key_blocks, value_blocks, lam, subln, lo, lam_init):
    o1, o2 = _attend(_stack_halves(q, lo), 2, key_blocks, [_with_ones(v) for v in value_blocks])
    o = o1 - lam * o2
    o = o * lax.rsqrt(jnp.mean(o * o, axis=-1, keepdims=True) + EPS)
    return (o * subln) * (1.0 - lam_init)


def _ctx1_kernel(lam_ref, subln_ref, qkv_ref, o_ref, *, lam_init):
    lo = _lane_lo((1, LANES))
    lam = _diff_lambda(lam_ref, lam_init)
    subln = subln_ref[...]
    for h in range(8):
        q = qkv_ref[:, h * LANES:(h + 1) * LANES]
        k = qkv_ref[:, 1024 + h * LANES:1024 + (h + 1) * LANES]
        v = qkv_ref[:, 2048 + h * LANES:2048 + (h + 1) * LANES]
        o_ref[:, h * LANES:(h + 1) * LANES] = _diff_head(q, [k], [v], lam, subln, lo, lam_init).astype(bf16)


def _ctx1(lamv, subln, qkv, lam_init):
    return pl.pallas_call(
        functools.partial(_ctx1_kernel, lam_init=lam_init),
        grid=(N_PROMPT_BATCH,),
        in_specs=[
            pl.BlockSpec((8, HEAD_DIM), lambda b: (0, 0)),
            pl.BlockSpec((1, LANES), lambda b: (0, 0)),
            pl.BlockSpec((PROMPT_SEQ, 3 * D_MODEL), lambda b: (b, 0)),
        ],
        out_specs=pl.BlockSpec((PROMPT_SEQ, D_MODEL), lambda b: (b, 0)),
        out_shape=jax.ShapeDtypeStruct((N_PROMPT, D_MODEL), bf16),
        compiler_params=_cparams(("parallel",)),
        name="ctx1",
    )(lamv, subln, qkv)


def _lat1_kernel(lam_ref, subln_ref, q_ref, k_ref, v_ref, ck_ref, cv_ref, o_ref, *, lam_init):
    lo = _lane_lo((1, LANES))
    lam = _diff_lambda(lam_ref, lam_init)
    subln = subln_ref[...]
    for h in range(8):
        sl = slice(h * LANES, (h + 1) * LANES)
        o_ref[:, sl] = _diff_head(q_ref[:, sl], [ck_ref[:, sl].astype(bf16), k_ref[:, sl]],
                                  [cv_ref[:, sl].astype(bf16), v_ref[:, sl]],
                                  lam, subln, lo, lam_init).astype(bf16)


def _lat1(lamv, subln, qkv, ck, cv, lam_init):
    sb = N_PROMPT // SAMPLE_SEQ
    nq = SAMPLE_SEQ // TM
    return pl.pallas_call(
        functools.partial(_lat1_kernel, lam_init=lam_init),
        grid=(N_SAMPLE_BATCH, nq),
        in_specs=[
            pl.BlockSpec((8, HEAD_DIM), lambda b, n: (0, 0)),
            pl.BlockSpec((1, LANES), lambda b, n: (0, 0)),
            pl.BlockSpec((TM, D_MODEL), lambda b, n: (N_PROMPT_TILES + b * nq + n, 0)),
            pl.BlockSpec((SAMPLE_SEQ, D_MODEL), lambda b, n: (sb + b, 1)),
            pl.BlockSpec((SAMPLE_SEQ, D_MODEL), lambda b, n: (sb + b, 2)),
            pl.BlockSpec((None, PAST_LEN, D_MODEL), lambda b, n: (b, 0, 0)),
            pl.BlockSpec((None, PAST_LEN, D_MODEL), lambda b, n: (b, 0, 0)),
        ],
        out_specs=pl.BlockSpec((TM, D_MODEL), lambda b, n: (b * nq + n, 0)),
        out_shape=jax.ShapeDtypeStruct((N_SAMPLE, D_MODEL), bf16),
        compiler_params=_cparams(("parallel", "arbitrary")),
        name="lat1",
    )(lamv, subln, qkv, qkv, qkv, ck, cv)


def _split_bf16(a):
    hi = a.astype(bf16)
    return hi, (a - hi.astype(f32)).astype(bf16)


def _route_kernel(*refs, n_x):
    x_refs = refs[:n_x]
    (op_ref, os_ref, mod_ref, g_ref, wo_ref, rwt_ref, rb_ref,
     xnew_ref, xloc_ref, slots_ref, gate_ref, len_ref) = refs[n_x:]
    tiles = [_route_tile(t, x_refs, op_ref, os_ref, mod_ref, g_ref, wo_ref, rwt_ref, rb_ref,
                         xnew_ref, slots_ref, gate_ref, len_ref) for t in range(ROUTE_TILES)]
    for t, (slots, h_hi, run_len) in enumerate(tiles):
        _route_dispatch(t, slots, h_hi, run_len, xloc_ref)


def _route_tile(t, x_refs, op_ref, os_ref, mod_ref, g_ref, wo_ref, rwt_ref, rb_ref,
                xnew_ref, slots_ref, gate_ref, len_ref):
    is_prompt = pl.program_id(0) < N_PROMPT_TILES // ROUTE_TILES
    rows = slice(t * TM, (t + 1) * TM)
    ng, ge = N_GROUPS, N_EXPERTS // N_GROUPS
    attn = jnp.where(is_prompt, op_ref[rows, :], os_ref[rows, :])
    x_in = x_refs[0][rows, :] if len(x_refs) == 1 else jnp.where(is_prompt, x_refs[0][rows, :], x_refs[1][rows, :])
    x = x_in + mod_ref[2:3, :] * _dot(attn, wo_ref[...])
    xnew_ref[rows, :] = x
    h = _norm_mod(x, g_ref[...], mod_ref[4:5, :], mod_ref[3:4, :])
    h_hi, h_lo = _split_bf16(h)
    w_hi, w_lo = _split_bf16(rwt_ref[...])
    logits = _dot_nt(w_hi, h_hi) + (_dot_nt(w_hi, h_lo) + _dot_nt(w_lo, h_hi))
    scores = jax.nn.sigmoid(logits)
    biased = scores + rb_ref[...]
    s3 = scores.reshape(ng, ge, TM)
    b3 = biased.reshape(ng, ge, TM)
    in_group = lax.broadcasted_iota(i32, (ng, ge, TM), 1).astype(f32)
    group_id = lax.broadcasted_iota(i32, (ng, 1, TM), 0).astype(f32)
    expert_id = lax.broadcasted_iota(i32, (ng, ge, TM), 0).astype(f32) * ge + in_group

    def max01(a):
        return jnp.max(jnp.max(a, axis=0, keepdims=True), axis=1, keepdims=True)

    def min01(a):
        return jnp.min(jnp.min(a, axis=0, keepdims=True), axis=1, keepdims=True)

    def sum01(a):
        return jnp.sum(jnp.sum(a, axis=0, keepdims=True), axis=1, keepdims=True)

    m1 = jnp.max(b3, axis=1, keepdims=True)
    first = jnp.min(jnp.where(b3 == m1, in_group, ge), axis=1, keepdims=True)
    m2 = jnp.max(jnp.where(in_group == first, -jnp.inf, b3), axis=1, keepdims=True)
    gscore = m1 + m2
    gsel = jnp.zeros((ng, 1, TM), f32)
    for _ in range(TOPK_GROUPS):
        gm = jnp.max(gscore, axis=0, keepdims=True)
        gi = jnp.min(jnp.where(gscore == gm, group_id, ng), axis=0, keepdims=True)
        hit = group_id == gi
        gsel = jnp.where(hit, 1.0, gsel)
        gscore = jnp.where(hit, -jnp.inf, gscore)
    cand = jnp.where(jnp.broadcast_to(gsel, (ng, ge, TM)) > 0.0, b3, -jnp.inf)
    top_e, top_w = [], []
    for _ in range(TOP_K):
        em = max01(cand)
        ei = min01(jnp.where(cand == em, expert_id, N_EXPERTS))
        hit = expert_id == ei
        top_e.append(ei)
        top_w.append(sum01(jnp.where(hit, s3, 0.0)))
        cand = jnp.where(hit, -jnp.inf, cand)
    wsum = functools.reduce(lambda a, b: a + b, top_w)
    sel3 = jnp.zeros((ng, ge, TM), f32)
    for k, (ei, w) in enumerate(zip(top_e, top_w)):
        gate_ref[t, k:k + 1, :] = (w / wsum * ROUTED_SCALE).reshape(1, TM)
        sel3 = jnp.where(expert_id == ei, 1.0, sel3)
    gate_ref[t, TOP_K:8, :] = jnp.zeros((8 - TOP_K, TM), f32)
    sel = sel3.reshape(N_EXPERTS, TM)

    cnt = jnp.sum(sel, axis=1, keepdims=True)
    run_len = jnp.ceil(cnt * (1.0 / CHUNK)) * CHUNK
    r_i = lax.broadcasted_iota(i32, (N_EXPERTS, N_EXPERTS), 0)
    c_i = lax.broadcasted_iota(i32, (N_EXPERTS, N_EXPERTS), 1)
    lower = jnp.where(c_i < r_i, 1.0, 0.0).astype(bf16)
    run_off = _dot(lower, jnp.broadcast_to(run_len, (N_EXPERTS, LANES)).astype(bf16))[:, 0:1]
    t_r = lax.broadcasted_iota(i32, (TM, TM), 0)
    t_c = lax.broadcasted_iota(i32, (TM, TM), 1)
    before = jnp.where(t_r < t_c, 1.0, 0.0).astype(bf16)
    rank = _dot(sel.astype(bf16), before)
    slot3 = (run_off + rank).reshape(ng, ge, TM)
    slots = [sum01(jnp.where(expert_id == ei, slot3, 0.0)).reshape(1, TM).astype(i32) for ei in top_e]
    for k in range(TOP_K):
        slots_ref[t, k:k + 1, :] = slots[k]
    slots_ref[t, TOP_K:8, :] = jnp.full((8 - TOP_K, TM), -1, i32)
    len_ref[t] = jnp.broadcast_to(run_len, (N_EXPERTS, LANES)).astype(i32)
    return slots, h_hi, run_len


def _route_dispatch(t, slots, h_hi, run_len, xloc_ref):
    rows = ROUTE_ROWS

    def body(c, carry):
        base = pl.multiple_of(c * rows, rows)
        row_id = base + lax.broadcasted_iota(i32, (rows, TM), 0)
        p = jnp.zeros((rows, TM), f32)
        for k in range(TOP_K):
            p = jnp.where(row_id == slots[k], 1.0, p)
        p = p.astype(bf16)
        xloc_ref[pl.ds(t * SLOTS + base, rows), :] = _dot(p, h_hi).astype(bf16)
        return carry

    def zero_body(c, carry):
        base = pl.multiple_of(c * rows, rows)
        xloc_ref[pl.ds(t * SLOTS + base, rows), :] = jnp.zeros((rows, D_MODEL), bf16)
        return carry

    n_used = (jnp.sum(run_len).astype(i32) + (rows - 1)) // rows
    lax.fori_loop(0, n_used, body, 0)
    lax.fori_loop(n_used, SLOTS // rows, zero_body, 0)


def _route(x_parts, o_prompt, o_sample, mod_l, g, w_out, rwt, rb):
    per_tile = lambda i: (i, 0, 0)
    rt = ROUTE_TILES
    tm = rt * TM
    return pl.pallas_call(
        functools.partial(_route_kernel, n_x=len(x_parts)),
        grid=(N_TILES // rt,),
        in_specs=_x_specs(x_parts, tm) + [
            pl.BlockSpec((tm, D_MODEL), lambda i: _prompt_block(i, tm)),
            pl.BlockSpec((tm, D_MODEL), lambda i: _sample_block(i, tm)),
            pl.BlockSpec((None, 6, D_MODEL), lambda i: (_mod_row(i, tm), 0, 0)),
            pl.BlockSpec((1, D_MODEL), lambda i: (0, 0)),
            pl.BlockSpec((D_MODEL, D_MODEL), lambda i: (0, 0)),
            pl.BlockSpec((N_EXPERTS, D_MODEL), lambda i: (0, 0)),
            pl.BlockSpec((N_EXPERTS, 1), lambda i: (0, 0)),
        ],
        out_specs=[
            pl.BlockSpec((tm, D_MODEL), lambda i: (i, 0)),
            pl.BlockSpec((rt * SLOTS, D_MODEL), lambda i: (i, 0)),
            pl.BlockSpec((rt, 8, TM), per_tile),
            pl.BlockSpec((rt, 8, TM), per_tile),
            pl.BlockSpec((rt, N_EXPERTS, LANES), per_tile),
        ],
        out_shape=[
            jax.ShapeDtypeStruct((N_TOK, D_MODEL), f32),
            jax.ShapeDtypeStruct((N_TILES * SLOTS, D_MODEL), bf16),
            jax.ShapeDtypeStruct((N_TILES, 8, TM), i32),
            jax.ShapeDtypeStruct((N_TILES, 8, TM), f32),
            jax.ShapeDtypeStruct((N_TILES, N_EXPERTS, LANES), i32),
        ],
        compiler_params=_cparams(("parallel",)),
        name="route",
    )(*x_parts, o_prompt, o_sample, mod_l, g, w_out, rwt, rb)


def _moe_plan(run_len):
    nt, ne = run_len.shape

    def excl_cumsum(a):
        n = a.shape[-1]
        earlier = np.arange(n)[None, :] < np.arange(n)[:, None]
        return jnp.sum(jnp.where(earlier, a[..., None, :], 0), axis=-1)

    def first_diff(a):
        return a - jnp.concatenate([jnp.zeros_like(a[..., :1]), a[..., :-1]], axis=-1)

    off_loc = excl_cumsum(run_len)
    before = excl_cumsum(run_len.T).T
    n_e = jnp.sum(run_len, axis=0)
    n_pad = -(-n_e // GM) * GM
    g_start = excl_cumsum(n_pad)
    total = jnp.sum(n_pad)
    run_dst = g_start[None, :] + before
    run_src = jnp.arange(nt, dtype=i32)[:, None] * SLOTS + off_loc
    dst_f = run_dst.T.reshape(-1)
    shift_f = first_diff((run_src - run_dst).T.reshape(-1))
    rows = jnp.arange((G_TILES + GMM_ITEM_TILES - 1) * G_CHUNKS, dtype=i32) * CHUNK
    shift = jnp.sum(jnp.where(dst_f[None, :] <= rows[:, None], shift_f[None, :], 0), axis=1)
    in_run = jnp.any((g_start[None, :] <= rows[:, None]) & (rows[:, None] < (g_start + n_e)[None, :]), axis=1)
    chunk_src = (jnp.where(in_run, rows + shift, 0) // CHUNK).astype(i32)
    loc_rows = jnp.arange(SLOT_CHUNKS, dtype=i32) * CHUNK
    shift_l = first_diff(run_dst - off_loc)
    shift = jnp.sum(jnp.where(off_loc[:, None, :] <= loc_rows[None, :, None], shift_l[:, None, :], 0), axis=2)
    used = jnp.sum(run_len, axis=1)
    chunk_map = jnp.where(loc_rows[None, :] < used[:, None], (loc_rows[None, :] + shift) // CHUNK, 0).astype(i32)
    tile_start, n_tiles = g_start // GM, n_pad // GM
    n_items = -(-n_tiles // GMM_ITEM_TILES)
    item_start = excl_cumsum(n_items)
    items = jnp.arange(GMM_MAX_ITEMS, dtype=i32)
    owner = items[:, None] >= item_start[None, :]
    e_first = jnp.sum(jnp.where(owner, first_diff(tile_start - GMM_ITEM_TILES * item_start)[None, :], 0), axis=1)
    item_tile = e_first + GMM_ITEM_TILES * items
    e_end = jnp.sum(jnp.where(owner, first_diff(tile_start + n_tiles)[None, :], 0), axis=1)
    item_cnt = jnp.clip(e_end - item_tile, 0, GMM_ITEM_TILES)
    gmm_plan = tuple(a.astype(i32) for a in (item_start, n_items, item_tile, item_cnt, chunk_src))
    return gmm_plan, chunk_map.reshape(-1)


def _gmm_in_copy(xloc_hbm, xbuf, sem, src_chunk, slot, c):
    return pltpu.make_async_copy(xloc_hbm.at[src_chunk], xbuf.at[slot, c], sem.at[slot])


def _gmm_out_copy(ybuf, y_hbm, sem, tile, slot, n_tiles):
    chunks = n_tiles * G_CHUNKS
    return pltpu.make_async_copy(ybuf.at[slot, pl.ds(0, chunks)],
                                 y_hbm.at[pl.ds(tile * G_CHUNKS, chunks)], sem.at[slot])


def _gmm_kernel(i0_ref, ni_ref, it_ref, ic_ref, cs_ref, xloc_hbm, wg_ref, wu_ref, wd_ref, y_hbm,
                xbuf, ybuf, zbuf, wg_b, wu_b, wd_b, in_sem, out_sem, zsem):
    e = pl.program_id(0)
    last = pl.num_programs(0) - 1
    n_items = ni_ref[e]
    first_item = i0_ref[e]
    total_items = i0_ref[last] + ni_ref[last]
    last_item = total_items - 1
    total_tiles = it_ref[last_item] + ic_ref[last_item]

    def start_in(item):
        first = it_ref[item] * G_CHUNKS
        for c in range(GMM_ITEM_TILES * G_CHUNKS):
            _gmm_in_copy(xloc_hbm, xbuf, in_sem, cs_ref[first + c], item % GMM_SLOTS, c).start()

    def wait_in(item):
        for c in range(GMM_ITEM_TILES * G_CHUNKS):
            _gmm_in_copy(xloc_hbm, xbuf, in_sem, 0, item % GMM_SLOTS, c).wait()

    def out_copy(item, fn):
        for cnt in range(1, GMM_ITEM_TILES + 1):
            @pl.when(ic_ref[item] == cnt)
            def _():
                fn(_gmm_out_copy(ybuf, y_hbm, out_sem, it_ref[item], item % GMM_SLOTS, cnt))

    @pl.when(e == 0)
    def _():
        for item in range(GMM_SLOTS - 1):
            start_in(item)
        zbuf[...] = jnp.zeros(zbuf.shape, zbuf.dtype)

    def tail_copies(fn):
        for j in range(_GMM_TAIL_PER_STEP):
            tile = total_tiles + e + j * N_EXPERTS

            @pl.when(tile < G_TILES)
            def _():
                fn(pltpu.make_async_copy(zbuf, y_hbm.at[pl.ds(tile * G_CHUNKS, G_CHUNKS)], zsem.at[0]))

    tail_copies(lambda cp: cp.start())

    @pl.when(n_items > 0)
    def _():
        wg_b[...] = wg_ref[...].astype(bf16)
        wu_b[...] = wu_ref[...].astype(bf16)
        wd_b[...] = wd_ref[...].astype(bf16)

    def body(j, carry):
        item = first_item + j
        slot = item % GMM_SLOTS

        @pl.when(item + (GMM_SLOTS - 1) < total_items)
        def _():
            start_in(item + (GMM_SLOTS - 1))

        wait_in(item)

        @pl.when(item >= GMM_SLOTS)
        def _():
            out_copy(item - GMM_SLOTS, lambda cp: cp.wait())

        for cnt in range(1, GMM_ITEM_TILES + 1):
            @pl.when(ic_ref[item] == cnt)
            def _():
                rows = cnt * GM
                chunks = cnt * G_CHUNKS
                x = xbuf[slot, 0:chunks].reshape(rows, D_MODEL)
                act = _silu(_dot(x, wg_b[...])) * _dot(x, wu_b[...])
                y = _dot(act.astype(bf16), wd_b[...]).astype(bf16)
                ybuf[slot, 0:chunks] = y.reshape(chunks, CHUNK, D_MODEL)

        out_copy(item, lambda cp: cp.start())
        return carry

    lax.fori_loop(0, n_items, body, 0)
    tail_copies(lambda cp: cp.wait())

    @pl.when(e == last)
    def _():
        for back in range(1, GMM_SLOTS + 1):
            out_copy(total_items - back, lambda cp: cp.wait())


_GMM_TAIL_PER_STEP = -(-(G_TILES - TM * TOP_K * N_TILES // GM) // N_EXPERTS)


def _gmm(plan, xloc, wg, wu, wd, layer):
    rows = GMM_ITEM_TILES * GM
    w_idx = lambda e, *_: (layer, e, 0, 0)
    grid_spec = pltpu.PrefetchScalarGridSpec(
        num_scalar_prefetch=5,
        grid=(N_EXPERTS,),
        in_specs=[
            pl.BlockSpec(memory_space=pl.ANY),
            pl.BlockSpec((None, None, D_MODEL, EXPERT_DIM), w_idx),
            pl.BlockSpec((None, None, D_MODEL, EXPERT_DIM), w_idx),
            pl.BlockSpec((None, None, EXPERT_DIM, D_MODEL), w_idx),
        ],
        out_specs=pl.BlockSpec(memory_space=pl.ANY),
        scratch_shapes=[pltpu.VMEM((GMM_SLOTS, rows // CHUNK, CHUNK, D_MODEL), bf16),
                        pltpu.VMEM((GMM_SLOTS, rows // CHUNK, CHUNK, D_MODEL), bf16),
                        pltpu.VMEM((G_CHUNKS, CHUNK, D_MODEL), bf16),
                        pltpu.VMEM((D_MODEL, EXPERT_DIM), bf16), pltpu.VMEM((D_MODEL, EXPERT_DIM), bf16),
                        pltpu.VMEM((EXPERT_DIM, D_MODEL), bf16),
                        pltpu.SemaphoreType.DMA((GMM_SLOTS,)), pltpu.SemaphoreType.DMA((GMM_SLOTS,)),
                        pltpu.SemaphoreType.DMA((1,))],
    )
    return pl.pallas_call(
        _gmm_kernel,
        grid_spec=grid_spec,
        out_shape=jax.ShapeDtypeStruct((G_TILES * G_CHUNKS, CHUNK, D_MODEL), bf16),
        compiler_params=_cparams(("arbitrary",)),
        name="gmm",
    )(*plan, xloc, wg, wu, wd)


def _combine_copy(y_hbm, ybuf, sem, sorted_chunk, slot, c):
    return pltpu.make_async_copy(y_hbm.at[sorted_chunk], ybuf.at[slot, c], sem.at[slot])


def _combine_kernel(cm_ref, y_hbm, slots_ref, gate_ref, x_ref, mod_ref, g_ref, sg_ref, su_ref, sd_ref, *rest, final):
    if final:
        gf_ref, yp_ref, ys_ref, ybuf, sem = rest
    else:
        o_ref, ybuf, sem = rest
    i = pl.program_id(0)
    n = pl.num_programs(0)
    slot = i % 2

    def start(tile, s):
        for c in range(SLOT_CHUNKS):
            _combine_copy(y_hbm, ybuf, sem, cm_ref[tile * SLOT_CHUNKS + c], s, c).start()

    def wait(s):
        for c in range(SLOT_CHUNKS):
            _combine_copy(y_hbm, ybuf, sem, 0, s, c).wait()

    @pl.when(i == 0)
    def _():
        start(0, 0)

    wait(slot)
    start((i + 1) % n, 1 - slot)

    x = x_ref[...]
    hb = _norm_mod(x, g_ref[...], mod_ref[4:5, :], mod_ref[3:4, :]).astype(bf16)
    shared = _dot((_silu(_dot(hb, sg_ref[...])) * _dot(hb, su_ref[...])).astype(bf16), sd_ref[...])
    row_id = lax.broadcasted_iota(i32, (SLOTS, TM), 0)
    p = jnp.zeros((SLOTS, TM), f32)
    for k in range(TOP_K):
        p = jnp.where(row_id == slots_ref[k:k + 1, :], gate_ref[k:k + 1, :], p)
    routed = lax.dot_general(p.astype(bf16), ybuf[slot].reshape(SLOTS, D_MODEL), (((0,), (0,)), ((), ())),
                             preferred_element_type=f32)
    out = x + mod_ref[5:6, :] * (routed + shared)
    if final:
        y = (out * lax.rsqrt(jnp.mean(out * out, axis=-1, keepdims=True) + EPS)) * gf_ref[...]

        @pl.when(i < N_PROMPT_TILES)
        def _():
            yp_ref[...] = y

        @pl.when(i >= N_PROMPT_TILES)
        def _():
            ys_ref[...] = y
    else:
        o_ref[...] = out

    @pl.when(i == n - 1)
    def _():
        wait(1 - slot)


def _combine(chunk_map, y, slots, gates, x, mod_l, g, sg, su, sd, final_g=None):
    shd = sg.shape[1]
    final = final_g is not None
    row_spec = pl.BlockSpec((TM, D_MODEL), lambda i, cm: (i, 0))
    vec_spec = pl.BlockSpec((1, D_MODEL), lambda i, cm: (0, 0))
    if final:
        out_specs = [pl.BlockSpec((TM, D_MODEL), lambda i, cm: _prompt_block(i)),
                     pl.BlockSpec((TM, D_MODEL), lambda i, cm: _sample_block(i))]
        out_shape = [jax.ShapeDtypeStruct((N_PROMPT, D_MODEL), f32), jax.ShapeDtypeStruct((N_SAMPLE, D_MODEL), f32)]
    else:
        out_specs, out_shape = row_spec, jax.ShapeDtypeStruct((N_TOK, D_MODEL), f32)
    grid_spec = pltpu.PrefetchScalarGridSpec(
        num_scalar_prefetch=1,
        grid=(N_TILES,),
        in_specs=[
            pl.BlockSpec(memory_space=pl.ANY),
            pl.BlockSpec((None, 8, TM), lambda i, cm: (i, 0, 0)),
            pl.BlockSpec((None, 8, TM), lambda i, cm: (i, 0, 0)),
            row_spec,
            pl.BlockSpec((None, 6, D_MODEL), lambda i, cm: (_mod_row(i), 0, 0)),
            vec_spec,
            pl.BlockSpec((D_MODEL, shd), lambda i, cm: (0, 0)),
            pl.BlockSpec((D_MODEL, shd), lambda i, cm: (0, 0)),
            pl.BlockSpec((shd, D_MODEL), lambda i, cm: (0, 0)),
        ] + ([vec_spec] if final else []),
        out_specs=out_specs,
        scratch_shapes=[pltpu.VMEM((2, SLOT_CHUNKS, CHUNK, D_MODEL), bf16), pltpu.SemaphoreType.DMA((2,))],
    )
    args = (chunk_map, y, slots, gates, x, mod_l, g, sg, su, sd) + ((final_g,) if final else ())
    return pl.pallas_call(
        functools.partial(_combine_kernel, final=final),
        grid_spec=grid_spec,
        out_shape=out_shape,
        compiler_params=_cparams(("arbitrary",)),
        name="combine",
    )(*args)


def _moe(x_parts, o_prompt, o_sample, w_out, mod_l, g, rwt, rb, wg, wu, wd, layer, sg, su, sd, final_g=None):
    x, xloc, slots, gates, run_len = _route(x_parts, o_prompt, o_sample, mod_l, g, w_out, rwt, rb)
    gmm_plan, chunk_map = _moe_plan(run_len[:, :, 0])
    y = _gmm(gmm_plan, xloc.reshape(N_TILES * SLOT_CHUNKS, CHUNK, D_MODEL), wg, wu, wd, layer)
    return _combine(chunk_map, y, slots, gates, x, mod_l, g, sg, su, sd, final_g)


_L0_CHUNKS = (
    (0, 512, 0, (0, 1, 2, 3), ()),
    (512, 1024, 768, (), ()),
    (1024, 1536, 1280, (), ((2, 0, 512, 0, False),)),
    (1536, 2048, 1792, (), ((3, 0, 512, 0, False),)),
    (2048, 2304, 512, (0,), ((0, 0, 128, 0, False), (1, 128, 256, 0, False))),
)
_L0_KV_OUTS = (("T", 128), ("T", 128), ("T", 512), ("T", 512))
_L1_CHUNKS = (
    (0, 512, 0, (0, 1, 2, 3), ()),
    (512, 1024, 512, (0, 1, 2, 3), ()),
    (1024, 1536, 1024, (0, 1, 2, 3), ((0, 0, 512, 0, False),)),
    (1536, 2048, 1536, (0, 1, 2, 3), ((0, 0, 512, 512, False),)),
    (2048, 2560, 2048, (), ((1, 0, 512, 0, True),)),
    (2560, 3072, 2560, (), ((1, 0, 512, 512, True),)),
)
_L1_KV_OUTS = (("T", 1024), ("H", 8))


def _from_feature_major(kt, *head_dims):
    nb, _, s = kt.shape
    nd = len(head_dims)
    return kt.reshape(nb, *head_dims, s).transpose(0, nd + 1, *range(1, nd + 1))[:, None]


def kernel(x_prompt, x_sample, cache_a_k, cache_a_v, cache_b_k, cache_b_v, cache_c_k, cache_c_v, c, c_ctx, w_mod, b_mod, norm_mix, norm_ffn, w_in_ab, w_out_ab, sink_a, rel_bias_b, w_in_c, w_out_c, lam_q1, lam_k1, lam_q2, lam_k2, subln_c, router_w, router_bias, exp_w_gate, exp_w_up, exp_w_down, sh_w_gate, sh_w_up, sh_w_down, final_norm):
    x = (x_prompt.reshape(N_PROMPT, D_MODEL), x_sample.reshape(N_SAMPLE, D_MODEL))
    cond8 = jnp.concatenate([c_ctx[None, :], c, jnp.zeros((8 - 1 - N_SAMPLE_BATCH, D_MODEL), f32)], axis=0)
    mod = _adaln(cond8, w_mod, b_mod).reshape(DEPTH, 8, 6, D_MODEL)
    rope_tabs = _rope_tables()
    new_kv = {}
    for layer in range(DEPTH):
        li = layer // 2
        mod_l = mod[layer]
        g_mix = norm_mix[layer][None, :]
        g_ffn = norm_ffn[layer][None, :]
        if layer % 2 == 0:
            w_in = w_in_ab[li].astype(bf16)
            qkv, ak, av, bk, bv = _inproj(x, mod_l, g_mix, w_in, rope_tabs, _L0_CHUNKS, _L0_KV_OUTS)
            new_kv["a_k"], new_kv["a_v"], new_kv["b_k"], new_kv["b_v"] = ak, av, bk, bv
            o_p = _ctx0(sink_a[li], qkv)
            o_s = _lat0(sink_a[li], qkv,
                        cache_a_k[:, li].reshape(N_SAMPLE_BATCH, PAST_LEN, LANES),
                        cache_a_v[:, li].reshape(N_SAMPLE_BATCH, PAST_LEN, LANES),
                        cache_b_k[:, li].reshape(N_SAMPLE_BATCH, PAST_LEN, 512),
                        cache_b_v[:, li].reshape(N_SAMPLE_BATCH, PAST_LEN, 512),
                        _na_bias_tiles(rel_bias_b[li]))
            w_out = w_out_ab[li].astype(bf16)
        else:
            lam_init = 0.8 - 0.6 * math.exp(-0.3 * layer)
            qkv, ck, cv = _inproj(x, mod_l, g_mix, w_in_c[li].astype(bf16), rope_tabs, _L1_CHUNKS, _L1_KV_OUTS)
            new_kv["c_k"], new_kv["c_v"] = ck, cv
            lamv = jnp.concatenate([lam_q1[li][None], lam_k1[li][None], lam_q2[li][None], lam_k2[li][None],
                                    jnp.zeros((4, HEAD_DIM), f32)], axis=0)
            subln = subln_c[li][None, :]
            o_p = _ctx1(lamv, subln, qkv, lam_init)
            o_s = _lat1(lamv, subln, qkv,
                        cache_c_k[:, li].reshape(N_SAMPLE_BATCH, PAST_LEN, D_MODEL),
                        cache_c_v[:, li].reshape(N_SAMPLE_BATCH, PAST_LEN, D_MODEL), lam_init)
            w_out = w_out_c[li].astype(bf16)
        last = layer == DEPTH - 1
        x = _moe(x, o_p, o_s, w_out, mod_l, g_ffn, router_w[layer].T, router_bias[layer][:, None],
                 exp_w_gate, exp_w_up, exp_w_down, layer,
                 sh_w_gate[layer].astype(bf16), sh_w_up[layer].astype(bf16), sh_w_down[layer].astype(bf16),
                 final_norm[None, :] if last else None)
        x = x if last else (x,)
    y_prompt, y_sample = x
    nb, s = N_PROMPT_BATCH, PROMPT_SEQ
    return (y_prompt.reshape(nb, s, D_MODEL), y_sample.reshape(N_SAMPLE_BATCH, SAMPLE_SEQ, D_MODEL),
            _from_feature_major(new_kv["a_k"], 2, HEAD_DIM), _from_feature_major(new_kv["a_v"], 2, HEAD_DIM),
            _from_feature_major(new_kv["b_k"], 8, HEAD_DIM), _from_feature_major(new_kv["b_v"], 8, HEAD_DIM),
            _from_feature_major(new_kv["c_k"], 8, 2, HEAD_DIM), new_kv["c_v"].reshape(nb, 1, s, 8, 2 * HEAD_DIM))
```

```python
import functools
import math

import jax
import jax.numpy as jnp
import numpy as np
from jax import lax
from jax.experimental import pallas as pl
from jax.experimental.pallas import tpu as pltpu

f32 = jnp.float32
bf16 = jnp.bfloat16
i32 = jnp.int32

D_MODEL = 1024
N_PROMPT_BATCH = 16
PROMPT_SEQ = 256
DEPTH = 2
N_SAMPLE_BATCH = 2
SAMPLE_SEQ = 2048
PAST_LEN = 512
GRID_W = 64
HEAD_DIM = 64
ROPE_THETA = 10000.0
EPS = 1e-6
A_WINDOW = 128
NA_ROWS = 8
NA_COLS = 16
N_EXPERTS = 64
TOP_K = 6
N_GROUPS = 8
TOPK_GROUPS = 4
EXPERT_DIM = 256
ROUTED_SCALE = 2.5
Q_SCALE = HEAD_DIM ** -0.5

N_PROMPT = N_PROMPT_BATCH * PROMPT_SEQ
N_SAMPLE = N_SAMPLE_BATCH * SAMPLE_SEQ
N_TOK = N_PROMPT + N_SAMPLE

LANES = 128
TM = 256
TD = 512
N_PROMPT_TILES = N_PROMPT // TM
N_TILES = N_TOK // TM
QB = 128
CHUNK = 16
SLOTS = -(-(TM * TOP_K + N_EXPERTS * (CHUNK - 1)) // 256) * 256
SLOT_CHUNKS = SLOTS // CHUNK
ROUTE_ROWS = 512
ROUTE_TILES = 2
GM = 256
_MAX_SORTED = TM * TOP_K * N_TILES + N_TILES * N_EXPERTS * (CHUNK - 1) + N_EXPERTS * (GM - CHUNK)
G_TILES = -(-_MAX_SORTED // GM)
G_CHUNKS = GM // CHUNK
GMM_ITEM_TILES = 2
GMM_SLOTS = 4
GMM_MAX_ITEMS = (G_TILES + N_EXPERTS * (GMM_ITEM_TILES - 1)) // GMM_ITEM_TILES
VMEM_LIMIT = 56 * 1024 * 1024

NEG = -1e30


def _cparams(sem):
    return pltpu.CompilerParams(dimension_semantics=sem, vmem_limit_bytes=VMEM_LIMIT)


def _mod_row(i, tm=TM):
    return jnp.where(i < N_PROMPT // tm, 0, 1 + (i - N_PROMPT // tm) // (SAMPLE_SEQ // tm))


def _prompt_block(i, tm=TM):
    return (jnp.minimum(i, N_PROMPT // tm - 1), 0)


def _sample_block(i, tm=TM):
    return (jnp.maximum(i - N_PROMPT // tm, 0), 0)


def _x_specs(parts, tm=TM):
    if len(parts) == 1:
        return [pl.BlockSpec((tm, D_MODEL), lambda i, *_: (i, 0))]
    return [pl.BlockSpec((tm, D_MODEL), lambda i, *_: _prompt_block(i, tm)),
            pl.BlockSpec((tm, D_MODEL), lambda i, *_: _sample_block(i, tm))]


def _load_x(i, x_refs, tm=TM):
    if len(x_refs) == 1:
        return x_refs[0][...]
    return jnp.where(i < N_PROMPT // tm, x_refs[0][...], x_refs[1][...])


def _norm_mod(x, g, scale, shift):
    y = x * lax.rsqrt(jnp.mean(x * x, axis=-1, keepdims=True) + EPS)
    return (y * g) * (1.0 + scale) + shift


def _silu(x):
    return x * jax.nn.sigmoid(x)


def _dot(a, b):
    return jnp.dot(a, b, preferred_element_type=f32)


def _dot_nt(a, b):
    return lax.dot_general(a, b, (((1,), (1,)), ((), ())), preferred_element_type=f32)


ADA_COLS = 1536


def _adaln_kernel(cond_ref, w_ref, b_ref, o_ref):
    s = _silu(cond_ref[...]).astype(bf16)
    o_ref[...] = _dot(s, w_ref[...].astype(bf16)) + b_ref[...]


def _adaln(cond8, w_mod, b_mod):
    n6 = 6 * D_MODEL
    return pl.pallas_call(
        _adaln_kernel,
        grid=(DEPTH, n6 // ADA_COLS),
        in_specs=[
            pl.BlockSpec((8, D_MODEL), lambda l, j: (0, 0)),
            pl.BlockSpec((None, D_MODEL, ADA_COLS), lambda l, j: (l, 0, j)),
            pl.BlockSpec((None, 1, ADA_COLS), lambda l, j: (l, 0, j)),
        ],
        out_specs=pl.BlockSpec((None, 8, ADA_COLS), lambda l, j: (l, 0, j)),
        out_shape=jax.ShapeDtypeStruct((DEPTH, 8, n6), f32),
        compiler_params=_cparams(("parallel", "parallel")),
        name="adaln",
    )(cond8, w_mod, b_mod.reshape(DEPTH, 1, n6))


def _rope_block(blk, cos, sin_a, sin_b):
    return blk * cos + pltpu.roll(blk, LANES - 16, 1) * sin_a + pltpu.roll(blk, 16, 1) * sin_b


def _inproj_kernel(*refs, chunks, n_x):
    x_refs, kv_refs = refs[:n_x], refs[n_x + 7:]
    mod_ref, g_ref, w_ref, cos_ref, sa_ref, sb_ref, qkv_ref = refs[n_x:n_x + 7]
    i = pl.program_id(0)
    h = _norm_mod(_load_x(i, x_refs, TD), g_ref[...], mod_ref[1:2, :], mod_ref[0:1, :]).astype(bf16)
    is_prompt = i < N_PROMPT // TD

    @pl.when(is_prompt)
    def _():
        for c0, c1, s0, _, kv_out in chunks:
            acc = _dot(h, w_ref[:, s0:s0 + (c1 - c0)])
            qkv_ref[:, c0:c1] = acc.astype(bf16)
            for ridx, a0, a1, o0, per_head in kv_out:
                if per_head:
                    heads = kv_refs[ridx].shape[0] // TD
                    for j in range((a1 - a0) // LANES):
                        kv_refs[ridx][pl.ds(o0 // LANES + j, TD, stride=heads), :] = (
                            acc[:, a0 + j * LANES:a0 + (j + 1) * LANES])
                else:
                    t = acc[:, a0:a1].T
                    for b in range(TD // PROMPT_SEQ):
                        kv_refs[ridx][b, o0:o0 + (a1 - a0), :] = t[:, b * PROMPT_SEQ:(b + 1) * PROMPT_SEQ]

    @pl.when(jnp.logical_not(is_prompt))
    def _():
        cos, sa, sb = cos_ref[...], sa_ref[...], sb_ref[...]
        for c0, c1, s0, rope_blocks, _ in chunks:
            acc = _dot(h, w_ref[:, s0:s0 + (c1 - c0)])
            for b in range((c1 - c0) // LANES):
                blk = acc[:, b * LANES:(b + 1) * LANES]
                if b in rope_blocks:
                    blk = _rope_block(blk, cos, sa, sb)
                qkv_ref[:, c0 + b * LANES:c0 + (b + 1) * LANES] = blk.astype(bf16)


def _inproj(x_parts, mod_l, g, w, rope_tabs, chunks, kv_outs):
    n = w.shape[1]
    cos, sa, sb = rope_tabs
    bpt = TD // PROMPT_SEQ

    def rope_idx(i):
        return (jnp.where(i < N_PROMPT // TD, 0, (i - N_PROMPT // TD) % (SAMPLE_SEQ // TD)), 0)

    kv_specs, kv_shapes = [], []
    for kind, size in kv_outs:
        if kind == "T":
            kv_specs.append(pl.BlockSpec((bpt, size, PROMPT_SEQ), lambda i: _prompt_block(i, TD) + (0,)))
            kv_shapes.append(jax.ShapeDtypeStruct((N_PROMPT_BATCH, size, PROMPT_SEQ), f32))
        else:
            kv_specs.append(pl.BlockSpec((TD * size, LANES), lambda i: _prompt_block(i, TD)))
            kv_shapes.append(jax.ShapeDtypeStruct((N_PROMPT * size, LANES), f32))

    return pl.pallas_call(
        functools.partial(_inproj_kernel, chunks=chunks, n_x=len(x_parts)),
        grid=(N_TOK // TD,),
        in_specs=_x_specs(x_parts, TD) + [
            pl.BlockSpec((None, 6, D_MODEL), lambda i: (_mod_row(i, TD), 0, 0)),
            pl.BlockSpec((1, D_MODEL), lambda i: (0, 0)),
            pl.BlockSpec((D_MODEL, n), lambda i: (0, 0)),
            pl.BlockSpec((TD, LANES), rope_idx),
            pl.BlockSpec((TD, LANES), rope_idx),
            pl.BlockSpec((TD, LANES), rope_idx),
        ],
        out_specs=[pl.BlockSpec((TD, n), lambda i: (i, 0))] + kv_specs,
        out_shape=[jax.ShapeDtypeStruct((N_TOK, n), bf16)] + kv_shapes,
        compiler_params=_cparams(("arbitrary",)),
        name="inproj",
    )(*x_parts, mod_l, g, w, cos, sa, sb)


def _rope_tables():
    nq = HEAD_DIM // 4
    t = np.arange(SAMPLE_SEQ)
    inv = np.power(np.float32(ROPE_THETA), -np.arange(nq, dtype=np.float32) / np.float32(nq))
    ang_r = (t // GRID_W).astype(np.float32)[:, None] * inv
    ang_c = (t % GRID_W).astype(np.float32)[:, None] * inv
    zero = np.zeros_like(ang_r)

    def head(fr, fc):
        return np.concatenate([fr[0], fr[1], fc[0], fc[1]], axis=-1)

    cos = head((np.cos(ang_r), np.cos(ang_r)), (np.cos(ang_c), np.cos(ang_c)))
    sin_a = head((-np.sin(ang_r), zero), (-np.sin(ang_c), zero))
    sin_b = head((zero, np.sin(ang_r)), (zero, np.sin(ang_c)))
    two = lambda a: jnp.asarray(np.concatenate([a, a], axis=-1).astype(np.float32))
    return two(cos), two(sin_a), two(sin_b)


def _lane_lo(shape):
    return lax.broadcasted_iota(i32, shape, len(shape) - 1) < HEAD_DIM


def _half(q, lo_mask, half):
    keep = lo_mask if half == 0 else jnp.logical_not(lo_mask)
    return jnp.where(keep, q, jnp.zeros_like(q)) * Q_SCALE


def _swap_halves(x):
    return pltpu.roll(x.astype(f32), HEAD_DIM, 1).astype(x.dtype)


def _stack_halves(q, lo_mask):
    return jnp.concatenate([_half(q, lo_mask, 0), _half(q, lo_mask, 1)], axis=0)


def _with_ones(v):
    return jnp.concatenate([v, jnp.ones_like(v)], axis=1)


def _attend(q_rows, n_heads, key_blocks, vx_blocks, fix_scores=None, sinks=None):
    return _attend_many([(q_rows, n_heads, key_blocks, vx_blocks, fix_scores, sinks)])[0]


def _attend_many(problems):
    scores = [[_dot_nt(q_rows, k) for k in key_blocks] for q_rows, _, key_blocks, _, _, _ in problems]
    exps, maxes = [], []
    for (q_rows, n_heads, key_blocks, _, fix_scores, sinks), sc in zip(problems, scores):
        r = q_rows.shape[0] // n_heads
        e_p, m_p = [[] for _ in key_blocks], []
        for h in range(n_heads):
            blocks = [s[h * r:(h + 1) * r] for s in sc]
            if fix_scores is not None:
                blocks = [fix_scores(h, i, s) for i, s in enumerate(blocks)]
            m = functools.reduce(jnp.maximum, [jnp.max(s, axis=-1, keepdims=True) for s in blocks])
            if sinks is not None:
                m = jnp.maximum(m, sinks[h])
            m_p.append(m)
            for i, s in enumerate(blocks):
                e_p[i].append(jnp.exp((s - m).astype(bf16)))
        exps.append(e_p)
        maxes.append(m_p)
    outs = [functools.reduce(lambda a, b: a + b,
                             [_dot(e[0] if n_heads == 1 else jnp.concatenate(e, axis=0), vx)
                              for e, vx in zip(e_p, vx_blocks)])
            for (_, n_heads, _, vx_blocks, _, _), e_p in zip(problems, exps)]
    results = []
    for (q_rows, n_heads, _, _, _, sinks), out, m_p in zip(problems, outs, maxes):
        r = q_rows.shape[0] // n_heads
        res = []
        for h in range(n_heads):
            den = out[h * r:(h + 1) * r, LANES:]
            if sinks is not None:
                den = den + jnp.exp(sinks[h] - m_p[h])
            res.append(out[h * r:(h + 1) * r, :LANES] * (1.0 / den))
        results.append(res)
    return results


def _gqa_rows(q_blocks, group, lo_mask):
    parts = []
    for q in q_blocks:
        for half in range(2):
            qh = _half(q, lo_mask, half)
            parts.append(qh if half == group else _swap_halves(qh))
    return jnp.concatenate(parts, axis=0)


def _gqa_merge(outs, group, lo_mask):
    fixed = [o if idx % 2 == group else pltpu.roll(o, HEAD_DIM, 1) for idx, o in enumerate(outs)]
    return [jnp.where(lo_mask, fixed[2 * p], fixed[2 * p + 1]) for p in range(len(outs) // 2)]


L0_QA, L0_QB, L0_KB, L0_VB, L0_KA, L0_VA, L0_N = 0, 512, 1024, 1536, 2048, 2176, 2304


def _ctx0_kernel(sink_ref, qkv_ref, o_ref):
    lo = _lane_lo((1, LANES))
    blk = lambda base, j: qkv_ref[:, base + j * LANES:base + (j + 1) * LANES]
    k_a = blk(L0_KA, 0)
    vx_a = _with_ones(blk(L0_VA, 0))
    for g in range(2):
        q_rows = _gqa_rows([blk(L0_QA, 2 * g), blk(L0_QA, 2 * g + 1)], g, lo)
        outs = _attend(q_rows, 4, [k_a], [vx_a], sinks=[sink_ref[4 * g + idx] for idx in range(4)])
        for p, o in enumerate(_gqa_merge(outs, g, lo)):
            j = 2 * g + p
            o_ref[:, j * LANES:(j + 1) * LANES] = o.astype(bf16)
    for j in range(4):
        outs = _attend(_stack_halves(blk(L0_QB, j), lo), 2, [blk(L0_KB, j)], [_with_ones(blk(L0_VB, j))])
        o_ref[:, 512 + j * LANES:512 + (j + 1) * LANES] = jnp.where(lo, outs[0], outs[1]).astype(bf16)


def _ctx0(sink, qkv):
    return pl.pallas_call(
        _ctx0_kernel,
        grid=(N_PROMPT_BATCH,),
        in_specs=[
            pl.BlockSpec(memory_space=pltpu.SMEM),
            pl.BlockSpec((PROMPT_SEQ, L0_N), lambda b: (b, 0)),
        ],
        out_specs=pl.BlockSpec((PROMPT_SEQ, D_MODEL), lambda b: (b, 0)),
        out_shape=jax.ShapeDtypeStruct((N_PROMPT, D_MODEL), bf16),
        compiler_params=_cparams(("parallel",)),
        name="ctx0",
    )(sink, qkv)


WIN_KEYS = 3 * QB
NA_KEY_ROWS = 10
NA_KEYS = NA_KEY_ROWS * GRID_W
N_QB = SAMPLE_SEQ // QB
N_NA_PATTERNS = 5
_PROMPT_QBLOCKS = N_PROMPT // QB


def _na_pattern(n):
    return jnp.where(n < 2, n, jnp.where(n > N_QB - 3, n - (N_QB - 5), 2))


def _lat0_kernel(sink_ref, roff_ref, q_ref, kvb_ref, kva_ref, cak_ref, cav_ref, cbk_ref, cbv_ref, tiles_ref, o_ref):
    n = pl.program_id(1)
    lo = _lane_lo((1, LANES))
    kstart = pl.multiple_of(jnp.clip((n - 1) * QB, 0, SAMPLE_SEQ - WIN_KEYS), QB)
    k_a = kva_ref[pl.ds(kstart, WIN_KEYS), 0:LANES]
    v_a = kva_ref[pl.ds(kstart, WIN_KEYS), LANES:2 * LANES]
    c_k = cak_ref[...].astype(bf16)
    keys_a = [c_k, k_a]
    vx_a = [_with_ones(cav_ref[...].astype(bf16)), _with_ones(v_a)]
    qpos = n * QB + lax.broadcasted_iota(i32, (QB, WIN_KEYS), 0)
    kpos = kstart + lax.broadcasted_iota(i32, (QB, WIN_KEYS), 1)
    in_window = jnp.abs(qpos - kpos) <= A_WINDOW
    mask_window = lambda h, i, s: jnp.where(in_window, s, NEG) if i == 1 else s
    problems = []
    for g in range(2):
        q_rows = _gqa_rows([q_ref[:, L0_QA + j * LANES:L0_QA + (j + 1) * LANES] for j in (2 * g, 2 * g + 1)], g, lo)
        problems.append((q_rows, 4, keys_a, vx_a, mask_window, [sink_ref[4 * g + idx] for idx in range(4)]))
    krow = jnp.clip(2 * n - NA_ROWS // 2, 0, SAMPLE_SEQ // GRID_W - NA_KEY_ROWS)
    ktok = pl.multiple_of(krow * GRID_W, QB)
    pattern = _na_pattern(n)

    def na_bias(head):
        rows = []
        for rq in range(QB // GRID_W):
            blocks = []
            for kb in range(NA_KEY_ROWS // 2):
                d0, d1 = (roff_ref[(pattern * 2 + rq) * NA_KEY_ROWS + 2 * kb + t] for t in range(2))
                blocks.append(jnp.where(lo, tiles_ref[head, d0], tiles_ref[head, d1]))
            rows.append(jnp.concatenate(blocks, axis=1))
        return jnp.concatenate(rows, axis=0)

    for j in range(4):
        q_b = q_ref[:, L0_QB + j * LANES:L0_QB + (j + 1) * LANES]
        k_b = kvb_ref[pl.ds(ktok, NA_KEYS), j * LANES:(j + 1) * LANES]
        v_b = kvb_ref[pl.ds(ktok, NA_KEYS), 512 + j * LANES:512 + (j + 1) * LANES]
        cb_k = cbk_ref[:, j * LANES:(j + 1) * LANES].astype(bf16)
        cb_v = cbv_ref[:, j * LANES:(j + 1) * LANES].astype(bf16)
        add_bias = lambda h, i, s, j=j: s + na_bias(2 * j + h) if i == 1 else s
        problems.append((_stack_halves(q_b, lo), 2, [cb_k, k_b], [_with_ones(cb_v), _with_ones(v_b)], add_bias, None))

    results = _attend_many(problems)
    for g in range(2):
        for p, o in enumerate(_gqa_merge(results[g], g, lo)):
            j = 2 * g + p
            o_ref[:, j * LANES:(j + 1) * LANES] = o.astype(bf16)
    for j in range(4):
        outs = results[2 + j]
        o_ref[:, 512 + j * LANES:512 + (j + 1) * LANES] = jnp.where(lo, outs[0], outs[1]).astype(bf16)


def _lat0(sink, qkv, cak, cav, cbk, cbv, bias_tiles):
    sb = N_PROMPT // SAMPLE_SEQ
    return pl.pallas_call(
        _lat0_kernel,
        grid=(N_SAMPLE_BATCH, N_QB),
        in_specs=[
            pl.BlockSpec(memory_space=pltpu.SMEM),
            pl.BlockSpec(memory_space=pltpu.SMEM),
            pl.BlockSpec((QB, 1024), lambda b, n: (_PROMPT_QBLOCKS + b * N_QB + n, 0)),
            pl.BlockSpec((SAMPLE_SEQ, 1024), lambda b, n: (sb + b, 1)),
            pl.BlockSpec((SAMPLE_SEQ, 256), lambda b, n: (sb + b, L0_KA // 256)),
            pl.BlockSpec((None, PAST_LEN, LANES), lambda b, n: (b, 0, 0)),
            pl.BlockSpec((None, PAST_LEN, LANES), lambda b, n: (b, 0, 0)),
            pl.BlockSpec((None, PAST_LEN, 512), lambda b, n: (b, 0, 0)),
            pl.BlockSpec((None, PAST_LEN, 512), lambda b, n: (b, 0, 0)),
            pl.BlockSpec((8, N_ROW_OFFSETS + 1, GRID_W, LANES), lambda b, n: (0, 0, 0, 0)),
        ],
        out_specs=pl.BlockSpec((QB, D_MODEL), lambda b, n: (b * N_QB + n, 0)),
        out_shape=jax.ShapeDtypeStruct((N_SAMPLE, D_MODEL), bf16),
        compiler_params=_cparams(("parallel", "arbitrary")),
        name="lat0",
    )(sink, jnp.asarray(_na_row_offsets()), qkv, qkv, qkv, cak, cav, cbk, cbv, bias_tiles)


N_ROW_OFFSETS = 2 * NA_ROWS - 1


def _na_row_offsets():
    rows = SAMPLE_SEQ // GRID_W
    idx = np.full((N_NA_PATTERNS, 2, NA_KEY_ROWS), N_ROW_OFFSETS, np.int32)
    for p, n in enumerate((0, 1, 2, N_QB - 2, N_QB - 1)):
        k0 = int(np.clip(2 * n - NA_ROWS // 2, 0, rows - NA_KEY_ROWS))
        for rq in range(2):
            r = 2 * n + rq
            rs = int(np.clip(r - NA_ROWS // 2, 0, rows - NA_ROWS))
            for kl in range(NA_KEY_ROWS):
                if rs <= k0 + kl < rs + NA_ROWS:
                    idx[p, rq, kl] = k0 + kl - r + NA_ROWS - 1
    return idx.reshape(-1)


def _na_bias_tiles(rel_bias):
    n_dc = 2 * NA_COLS - 1
    c = np.arange(GRID_W)[:, None]
    kc = np.arange(GRID_W)[None, :]
    cs = np.clip(c - NA_COLS // 2, 0, GRID_W - NA_COLS)
    col_ok = (kc >= cs) & (kc < cs + NA_COLS)
    col_hot = ((kc - c + NA_COLS - 1)[None] == np.arange(n_dc)[:, None, None]) & col_ok[None]
    hp = lax.Precision.HIGHEST
    tiles = jnp.einsum("hdx,xck->hdck", rel_bias.astype(f32), col_hot.astype(np.float32), precision=hp)
    tiles = tiles + np.where(col_ok, 0.0, NEG).astype(np.float32)
    tiles = jnp.concatenate([tiles, jnp.full((tiles.shape[0], 1, GRID_W, GRID_W), NEG, f32)], axis=1)
    return jnp.concatenate([tiles, tiles], axis=-1)


def _diff_lambda(lam_ref, lam_init):
    lv = lam_ref[...]
    s1 = jnp.sum(lv[0:1, :] * lv[1:2, :], axis=-1, keepdims=True)
    s2 = jnp.sum(lv[2:3, :] * lv[3:4, :], axis=-1, keepdims=True)
    return jnp.exp(s1) - jnp.exp(s2) + lam_init


DIFF_HEADS_PER_PASS = 2


def _diff_heads(heads, o_ref, lam, subln, lo, lam_init):
    for p0 in range(0, len(heads), DIFF_HEADS_PER_PASS):
        group = heads[p0:p0 + DIFF_HEADS_PER_PASS]
        results = _attend_many([(_stack_halves(q, lo), 2, ks, [_with_ones(v) for v in vs], None, None)
                                for _, q, ks, vs in group])
        for (h, _, _, _), (o1, o2) in zip(group, results):
            o = o1 - lam * o2
            o = o * lax.rsqrt(jnp.mean(o * o, axis=-1, keepdims=True) + EPS)
            o_ref[:, h * LANES:(h + 1) * LANES] = ((o * subln) * (1.0 - lam_init)).astype(bf16)


def _ctx1_kernel(lam_ref, subln_ref, qkv_ref, o_ref, *, lam_init):
    lo = _lane_lo((1, LANES))
    lam = _diff_lambda(lam_ref, lam_init)
    blk = lambda base, h: qkv_ref[:, base + h * LANES:base + (h + 1) * LANES]
    heads = [(h, blk(0, h), [blk(D_MODEL, h)], [blk(2 * D_MODEL, h)]) for h in range(8)]
    _diff_heads(heads, o_ref, lam, subln_ref[...], lo, lam_init)


def _ctx1(lamv, subln, qkv, lam_init):
    return pl.pallas_call(
        functools.partial(_ctx1_kernel, lam_init=lam_init),
        grid=(N_PROMPT_BATCH,),
        in_specs=[
            pl.BlockSpec((8, HEAD_DIM), lambda b: (0, 0)),
            pl.BlockSpec((1, LANES), lambda b: (0, 0)),
            pl.BlockSpec((PROMPT_SEQ, 3 * D_MODEL), lambda b: (b, 0)),
        ],
        out_specs=pl.BlockSpec((PROMPT_SEQ, D_MODEL), lambda b: (b, 0)),
        out_shape=jax.ShapeDtypeStruct((N_PROMPT, D_MODEL), bf16),
        compiler_params=_cparams(("parallel",)),
        name="ctx1",
    )(lamv, subln, qkv)


def _lat1_kernel(lam_ref, subln_ref, q_ref, k_ref, v_ref, ck_ref, cv_ref, o_ref, *, lam_init):
    lo = _lane_lo((1, LANES))
    lam = _diff_lambda(lam_ref, lam_init)
    heads = []
    for h in range(8):
        sl = slice(h * LANES, (h + 1) * LANES)
        heads.append((h, q_ref[:, sl], [ck_ref[:, sl].astype(bf16), k_ref[:, sl]],
                      [cv_ref[:, sl].astype(bf16), v_ref[:, sl]]))
    _diff_heads(heads, o_ref, lam, subln_ref[...], lo, lam_init)


def _lat1(lamv, subln, qkv, ck, cv, lam_init):
    sb = N_PROMPT // SAMPLE_SEQ
    nq = SAMPLE_SEQ // TM
    return pl.pallas_call(
        functools.partial(_lat1_kernel, lam_init=lam_init),
        grid=(N_SAMPLE_BATCH, nq),
        in_specs=[
            pl.BlockSpec((8, HEAD_DIM), lambda b, n: (0, 0)),
            pl.BlockSpec((1, LANES), lambda b, n: (0, 0)),
            pl.BlockSpec((TM, D_MODEL), lambda b, n: (N_PROMPT_TILES + b * nq + n, 0)),
            pl.BlockSpec((SAMPLE_SEQ, D_MODEL), lambda b, n: (sb + b, 1)),
            pl.BlockSpec((SAMPLE_SEQ, D_MODEL), lambda b, n: (sb + b, 2)),
            pl.BlockSpec((None, PAST_LEN, D_MODEL), lambda b, n: (b, 0, 0)),
            pl.BlockSpec((None, PAST_LEN, D_MODEL), lambda b, n: (b, 0, 0)),
        ],
        out_specs=pl.BlockSpec((TM, D_MODEL), lambda b, n: (b * nq + n, 0)),
        out_shape=jax.ShapeDtypeStruct((N_SAMPLE, D_MODEL), bf16),
        compiler_params=_cparams(("parallel", "arbitrary")),
        name="lat1",
    )(lamv, subln, qkv, qkv, qkv, ck, cv)


def _split_bf16(a):
    hi = a.astype(bf16)
    return hi, (a - hi.astype(f32)).astype(bf16)


def _route_kernel(*refs, n_x):
    x_refs = refs[:n_x]
    (op_ref, os_ref, mod_ref, g_ref, wo_ref, rwt_ref, rb_ref,
     xnew_ref, xloc_ref, slots_ref, gate_ref, len_ref) = refs[n_x:]
    logits = [_route_logits(t, x_refs, op_ref, os_ref, mod_ref, g_ref, wo_ref, rwt_ref, xnew_ref)
              for t in range(ROUTE_TILES)]
    tiles = _route_tiles(jnp.concatenate([lg for _, lg in logits], axis=1), rb_ref, slots_ref, gate_ref, len_ref)
    for t, (slots, run_len) in enumerate(tiles):
        _route_dispatch(t, slots, logits[t][0], run_len, xloc_ref)


def _route_logits(t, x_refs, op_ref, os_ref, mod_ref, g_ref, wo_ref, rwt_ref, xnew_ref):
    is_prompt = pl.program_id(0) < N_PROMPT_TILES // ROUTE_TILES
    rows = slice(t * TM, (t + 1) * TM)
    attn = jnp.where(is_prompt, op_ref[rows, :], os_ref[rows, :])
    x_in = x_refs[0][rows, :] if len(x_refs) == 1 else jnp.where(is_prompt, x_refs[0][rows, :], x_refs[1][rows, :])
    x = x_in + mod_ref[2:3, :] * _dot(attn, wo_ref[...])
    xnew_ref[rows, :] = x
    h = _norm_mod(x, g_ref[...], mod_ref[4:5, :], mod_ref[3:4, :])
    h_hi, h_lo = _split_bf16(h)
    w_hi, w_lo = _split_bf16(rwt_ref[...])
    return h_hi, _dot_nt(w_hi, h_hi) + (_dot_nt(w_hi, h_lo) + _dot_nt(w_lo, h_hi))


def _route_tiles(logits, rb_ref, slots_ref, gate_ref, len_ref):
    ng, ge = N_GROUPS, N_EXPERTS // N_GROUPS
    n = ROUTE_TILES * TM
    tile = lambda a, t: a[..., t * TM:(t + 1) * TM]
    scores = jax.nn.sigmoid(logits)
    biased = scores + rb_ref[...]
    s3 = scores.reshape(ng, ge, n)
    b3 = biased.reshape(ng, ge, n)
    in_group = lax.broadcasted_iota(i32, (ng, ge, n), 1).astype(f32)
    group_id = lax.broadcasted_iota(i32, (ng, 1, n), 0).astype(f32)
    expert_id = lax.broadcasted_iota(i32, (ng, ge, n), 0).astype(f32) * ge + in_group

    def max01(a):
        return jnp.max(jnp.max(a, axis=0, keepdims=True), axis=1, keepdims=True)

    def min01(a):
        return jnp.min(jnp.min(a, axis=0, keepdims=True), axis=1, keepdims=True)

    def sum01(a):
        return jnp.sum(jnp.sum(a, axis=0, keepdims=True), axis=1, keepdims=True)

    m1 = jnp.max(b3, axis=1, keepdims=True)
    first = jnp.min(jnp.where(b3 == m1, in_group, ge), axis=1, keepdims=True)
    m2 = jnp.max(jnp.where(in_group == first, -jnp.inf, b3), axis=1, keepdims=True)
    gscore = m1 + m2
    gsel = jnp.zeros((ng, 1, n), f32)
    for _ in range(TOPK_GROUPS):
        gm = jnp.max(gscore, axis=0, keepdims=True)
        gi = jnp.min(jnp.where(gscore == gm, group_id, ng), axis=0, keepdims=True)
        hit = group_id == gi
        gsel = jnp.where(hit, 1.0, gsel)
        gscore = jnp.where(hit, -jnp.inf, gscore)
    cand = jnp.where(jnp.broadcast_to(gsel, (ng, ge, n)) > 0.0, b3, -jnp.inf)
    top_e, top_w = [], []
    for _ in range(TOP_K):
        em = max01(cand)
        ei = min01(jnp.where(cand == em, expert_id, N_EXPERTS))
        hit = expert_id == ei
        top_e.append(ei)
        top_w.append(sum01(jnp.where(hit, s3, 0.0)))
        cand = jnp.where(hit, -jnp.inf, cand)
    wsum = functools.reduce(lambda a, b: a + b, top_w)
    sel3 = jnp.zeros((ng, ge, n), f32)
    for k, (ei, w) in enumerate(zip(top_e, top_w)):
        gate = (w / wsum * ROUTED_SCALE).reshape(1, n)
        for t in range(ROUTE_TILES):
            gate_ref[t, k:k + 1, :] = tile(gate, t)
        sel3 = jnp.where(expert_id == ei, 1.0, sel3)
    sel = sel3.reshape(N_EXPERTS, n)

    r_i = lax.broadcasted_iota(i32, (N_EXPERTS, N_EXPERTS), 0)
    c_i = lax.broadcasted_iota(i32, (N_EXPERTS, N_EXPERTS), 1)
    lower = jnp.where(c_i < r_i, 1.0, 0.0).astype(bf16)
    run_lens, run_offs = [], []
    for t in range(ROUTE_TILES):
        cnt = jnp.sum(tile(sel, t), axis=1, keepdims=True)
        run_len = jnp.ceil(cnt * (1.0 / CHUNK)) * CHUNK
        run_off = _dot(lower, jnp.broadcast_to(run_len, (N_EXPERTS, LANES)).astype(bf16))[:, 0:1]
        run_lens.append(run_len)
        run_offs.append(jnp.broadcast_to(run_off, (N_EXPERTS, TM)))
    t_r = lax.broadcasted_iota(i32, (n, n), 0)
    t_c = lax.broadcasted_iota(i32, (n, n), 1)
    before = jnp.where(jnp.logical_and(t_r < t_c, t_r // TM == t_c // TM), 1.0, 0.0).astype(bf16)
    rank = _dot(sel.astype(bf16), before)
    slot3 = (jnp.concatenate(run_offs, axis=1) + rank).reshape(ng, ge, n)
    slots = [sum01(jnp.where(expert_id == ei, slot3, 0.0)).reshape(1, n).astype(i32) for ei in top_e]
    out = []
    for t in range(ROUTE_TILES):
        for k in range(TOP_K):
            slots_ref[t, k:k + 1, :] = tile(slots[k], t)
        slots_ref[t, TOP_K:8, :] = jnp.full((8 - TOP_K, TM), -1, i32)
        gate_ref[t, TOP_K:8, :] = jnp.zeros((8 - TOP_K, TM), f32)
        len_ref[t] = jnp.broadcast_to(run_lens[t], (N_EXPERTS, LANES)).astype(i32)
        out.append(([tile(sl, t) for sl in slots], run_lens[t]))
    return out


def _route_dispatch(t, slots, h_hi, run_len, xloc_ref):
    rows = ROUTE_ROWS

    def body(c, carry):
        base = pl.multiple_of(c * rows, rows)
        row_id = base + lax.broadcasted_iota(i32, (rows, TM), 0)
        p = jnp.zeros((rows, TM), f32)
        for k in range(TOP_K):
            p = jnp.where(row_id == slots[k], 1.0, p)
        p = p.astype(bf16)
        xloc_ref[pl.ds(t * SLOTS + base, rows), :] = _dot(p, h_hi).astype(bf16)
        return carry

    def zero_body(c, carry):
        base = pl.multiple_of(c * rows, rows)
        xloc_ref[pl.ds(t * SLOTS + base, rows), :] = jnp.zeros((rows, D_MODEL), bf16)
        return carry

    n_used = (jnp.sum(run_len).astype(i32) + (rows - 1)) // rows
    lax.fori_loop(0, n_used, body, 0)
    lax.fori_loop(n_used, SLOTS // rows, zero_body, 0)


def _route(x_parts, o_prompt, o_sample, mod_l, g, w_out, rwt, rb):
    per_tile = lambda i: (i, 0, 0)
    rt = ROUTE_TILES
    tm = rt * TM
    return pl.pallas_call(
        functools.partial(_route_kernel, n_x=len(x_parts)),
        grid=(N_TILES // rt,),
        in_specs=_x_specs(x_parts, tm) + [
            pl.BlockSpec((tm, D_MODEL), lambda i: _prompt_block(i, tm)),
            pl.BlockSpec((tm, D_MODEL), lambda i: _sample_block(i, tm)),
            pl.BlockSpec((None, 6, D_MODEL), lambda i: (_mod_row(i, tm), 0, 0)),
            pl.BlockSpec((1, D_MODEL), lambda i: (0, 0)),
            pl.BlockSpec((D_MODEL, D_MODEL), lambda i: (0, 0)),
            pl.BlockSpec((N_EXPERTS, D_MODEL), lambda i: (0, 0)),
            pl.BlockSpec((N_EXPERTS, 1), lambda i: (0, 0)),
        ],
        out_specs=[
            pl.BlockSpec((tm, D_MODEL), lambda i: (i, 0)),
            pl.BlockSpec((rt * SLOTS, D_MODEL), lambda i: (i, 0)),
            pl.BlockSpec((rt, 8, TM), per_tile),
            pl.BlockSpec((rt, 8, TM), per_tile),
            pl.BlockSpec((rt, N_EXPERTS, LANES), per_tile),
        ],
        out_shape=[
            jax.ShapeDtypeStruct((N_TOK, D_MODEL), f32),
            jax.ShapeDtypeStruct((N_TILES * SLOTS, D_MODEL), bf16),
            jax.ShapeDtypeStruct((N_TILES, 8, TM), i32),
            jax.ShapeDtypeStruct((N_TILES, 8, TM), f32),
            jax.ShapeDtypeStruct((N_TILES, N_EXPERTS, LANES), i32),
        ],
        compiler_params=_cparams(("parallel",)),
        name="route",
    )(*x_parts, o_prompt, o_sample, mod_l, g, w_out, rwt, rb)


def _moe_plan(run_len):
    nt, ne = run_len.shape

    def excl_cumsum(a):
        n = a.shape[-1]
        earlier = np.arange(n)[None, :] < np.arange(n)[:, None]
        return jnp.sum(jnp.where(earlier, a[..., None, :], 0), axis=-1)

    def first_diff(a):
        return a - jnp.concatenate([jnp.zeros_like(a[..., :1]), a[..., :-1]], axis=-1)

    off_loc = excl_cumsum(run_len)
    before = excl_cumsum(run_len.T).T
    n_e = jnp.sum(run_len, axis=0)
    n_pad = -(-n_e // GM) * GM
    g_start = excl_cumsum(n_pad)
    total = jnp.sum(n_pad)
    run_dst = g_start[None, :] + before
    run_src = jnp.arange(nt, dtype=i32)[:, None] * SLOTS + off_loc
    dst_f = run_dst.T.reshape(-1)
    shift_f = first_diff((run_src - run_dst).T.reshape(-1))
    rows = jnp.arange((G_TILES + GMM_ITEM_TILES - 1) * G_CHUNKS, dtype=i32) * CHUNK
    shift = jnp.sum(jnp.where(dst_f[None, :] <= rows[:, None], shift_f[None, :], 0), axis=1)
    in_run = jnp.any((g_start[None, :] <= rows[:, None]) & (rows[:, None] < (g_start + n_e)[None, :]), axis=1)
    chunk_src = (jnp.where(in_run, rows + shift, 0) // CHUNK).astype(i32)
    loc_rows = jnp.arange(SLOT_CHUNKS, dtype=i32) * CHUNK
    shift_l = first_diff(run_dst - off_loc)
    shift = jnp.sum(jnp.where(off_loc[:, None, :] <= loc_rows[None, :, None], shift_l[:, None, :], 0), axis=2)
    used = jnp.sum(run_len, axis=1)
    chunk_map = jnp.where(loc_rows[None, :] < used[:, None], (loc_rows[None, :] + shift) // CHUNK, 0).astype(i32)
    tile_start, n_tiles = g_start // GM, n_pad // GM
    n_items = -(-n_tiles // GMM_ITEM_TILES)
    item_start = excl_cumsum(n_items)
    items = jnp.arange(GMM_MAX_ITEMS, dtype=i32)
    owner = items[:, None] >= item_start[None, :]
    e_first = jnp.sum(jnp.where(owner, first_diff(tile_start - GMM_ITEM_TILES * item_start)[None, :], 0), axis=1)
    item_tile = e_first + GMM_ITEM_TILES * items
    e_end = jnp.sum(jnp.where(owner, first_diff(tile_start + n_tiles)[None, :], 0), axis=1)
    item_cnt = jnp.clip(e_end - item_tile, 0, GMM_ITEM_TILES)
    gmm_plan = tuple(a.astype(i32) for a in (item_start, n_items, item_tile, item_cnt, chunk_src))
    return gmm_plan, chunk_map.reshape(-1)


def _gmm_in_copy(xloc_hbm, xbuf, sem, src_chunk, slot, c):
    return pltpu.make_async_copy(xloc_hbm.at[src_chunk], xbuf.at[slot, c], sem.at[slot])


def _gmm_out_copy(ybuf, y_hbm, sem, tile, slot, n_tiles):
    chunks = n_tiles * G_CHUNKS
    return pltpu.make_async_copy(ybuf.at[slot, pl.ds(0, chunks)],
                                 y_hbm.at[pl.ds(tile * G_CHUNKS, chunks)], sem.at[slot])


def _gmm_kernel(i0_ref, ni_ref, it_ref, ic_ref, cs_ref, xloc_hbm, wg_ref, wu_ref, wd_ref, y_hbm,
                xbuf, ybuf, zbuf, wg_b, wu_b, wd_b, in_sem, out_sem, zsem):
    e = pl.program_id(0)
    last = pl.num_programs(0) - 1
    n_items = ni_ref[e]
    first_item = i0_ref[e]
    total_items = i0_ref[last] + ni_ref[last]
    last_item = total_items - 1
    total_tiles = it_ref[last_item] + ic_ref[last_item]

    def start_in(item):
        first = it_ref[item] * G_CHUNKS
        for c in range(GMM_ITEM_TILES * G_CHUNKS):
            _gmm_in_copy(xloc_hbm, xbuf, in_sem, cs_ref[first + c], item % GMM_SLOTS, c).start()

    def wait_in(item):
        for c in range(GMM_ITEM_TILES * G_CHUNKS):
            _gmm_in_copy(xloc_hbm, xbuf, in_sem, 0, item % GMM_SLOTS, c).wait()

    def out_copy(item, fn):
        for cnt in range(1, GMM_ITEM_TILES + 1):
            @pl.when(ic_ref[item] == cnt)
            def _():
                fn(_gmm_out_copy(ybuf, y_hbm, out_sem, it_ref[item], item % GMM_SLOTS, cnt))

    @pl.when(e == 0)
    def _():
        for item in range(GMM_SLOTS - 1):
            start_in(item)
        zbuf[...] = jnp.zeros(zbuf.shape, zbuf.dtype)

    def tail_copies(fn):
        for j in range(_GMM_TAIL_PER_STEP):
            tile = total_tiles + e + j * N_EXPERTS

            @pl.when(tile < G_TILES)
            def _():
                fn(pltpu.make_async_copy(zbuf, y_hbm.at[pl.ds(tile * G_CHUNKS, G_CHUNKS)], zsem.at[0]))

    tail_copies(lambda cp: cp.start())

    @pl.when(n_items > 0)
    def _():
        wg_b[...] = wg_ref[...].astype(bf16)
        wu_b[...] = wu_ref[...].astype(bf16)
        wd_b[...] = wd_ref[...].astype(bf16)

    def body(j, carry):
        item = first_item + j
        slot = item % GMM_SLOTS

        @pl.when(item + (GMM_SLOTS - 1) < total_items)
        def _():
            start_in(item + (GMM_SLOTS - 1))

        wait_in(item)

        @pl.when(item >= GMM_SLOTS)
        def _():
            out_copy(item - GMM_SLOTS, lambda cp: cp.wait())

        for cnt in range(1, GMM_ITEM_TILES + 1):
            @pl.when(ic_ref[item] == cnt)
            def _():
                rows = cnt * GM
                chunks = cnt * G_CHUNKS
                x = xbuf[slot, 0:chunks].reshape(rows, D_MODEL)
                act = _silu(_dot(x, wg_b[...])) * _dot(x, wu_b[...])
                y = _dot(act.astype(bf16), wd_b[...]).astype(bf16)
                ybuf[slot, 0:chunks] = y.reshape(chunks, CHUNK, D_MODEL)

        out_copy(item, lambda cp: cp.start())
        return carry

    lax.fori_loop(0, n_items, body, 0)
    tail_copies(lambda cp: cp.wait())

    @pl.when(e == last)
    def _():
        for back in range(1, GMM_SLOTS + 1):
            out_copy(total_items - back, lambda cp: cp.wait())


_GMM_TAIL_PER_STEP = -(-(G_TILES - TM * TOP_K * N_TILES // GM) // N_EXPERTS)


def _gmm(plan, xloc, wg, wu, wd, layer):
    rows = GMM_ITEM_TILES * GM
    w_idx = lambda e, *_: (layer, e, 0, 0)
    grid_spec = pltpu.PrefetchScalarGridSpec(
        num_scalar_prefetch=5,
        grid=(N_EXPERTS,),
        in_specs=[
            pl.BlockSpec(memory_space=pl.ANY),
            pl.BlockSpec((None, None, D_MODEL, EXPERT_DIM), w_idx),
            pl.BlockSpec((None, None, D_MODEL, EXPERT_DIM), w_idx),
            pl.BlockSpec((None, None, EXPERT_DIM, D_MODEL), w_idx),
        ],
        out_specs=pl.BlockSpec(memory_space=pl.ANY),
        scratch_shapes=[pltpu.VMEM((GMM_SLOTS, rows // CHUNK, CHUNK, D_MODEL), bf16),
                        pltpu.VMEM((GMM_SLOTS, rows // CHUNK, CHUNK, D_MODEL), bf16),
                        pltpu.VMEM((G_CHUNKS, CHUNK, D_MODEL), bf16),
                        pltpu.VMEM((D_MODEL, EXPERT_DIM), bf16), pltpu.VMEM((D_MODEL, EXPERT_DIM), bf16),
                        pltpu.VMEM((EXPERT_DIM, D_MODEL), bf16),
                        pltpu.SemaphoreType.DMA((GMM_SLOTS,)), pltpu.SemaphoreType.DMA((GMM_SLOTS,)),
                        pltpu.SemaphoreType.DMA((1,))],
    )
    return pl.pallas_call(
        _gmm_kernel,
        grid_spec=grid_spec,
        out_shape=jax.ShapeDtypeStruct((G_TILES * G_CHUNKS, CHUNK, D_MODEL), bf16),
        compiler_params=_cparams(("arbitrary",)),
        name="gmm",
    )(*plan, xloc, wg, wu, wd)


def _combine_copy(y_hbm, ybuf, sem, sorted_chunk, slot, c):
    return pltpu.make_async_copy(y_hbm.at[sorted_chunk], ybuf.at[slot, c], sem.at[slot])


def _combine_kernel(cm_ref, y_hbm, slots_ref, gate_ref, x_ref, mod_ref, g_ref, sg_ref, su_ref, sd_ref, *rest, final):
    if final:
        gf_ref, yp_ref, ys_ref, ybuf, sem = rest
    else:
        o_ref, ybuf, sem = rest
    i = pl.program_id(0)
    n = pl.num_programs(0)
    slot = i % 2

    def start(tile, s):
        for c in range(SLOT_CHUNKS):
            _combine_copy(y_hbm, ybuf, sem, cm_ref[tile * SLOT_CHUNKS + c], s, c).start()

    def wait(s):
        for c in range(SLOT_CHUNKS):
            _combine_copy(y_hbm, ybuf, sem, 0, s, c).wait()

    @pl.when(i == 0)
    def _():
        start(0, 0)

    wait(slot)
    start((i + 1) % n, 1 - slot)

    x = x_ref[...]
    hb = _norm_mod(x, g_ref[...], mod_ref[4:5, :], mod_ref[3:4, :]).astype(bf16)
    shared = _dot((_silu(_dot(hb, sg_ref[...])) * _dot(hb, su_ref[...])).astype(bf16), sd_ref[...])
    row_id = lax.broadcasted_iota(i32, (SLOTS, TM), 0)
    p = jnp.zeros((SLOTS, TM), f32)
    for k in range(TOP_K):
        p = jnp.where(row_id == slots_ref[k:k + 1, :], gate_ref[k:k + 1, :], p)
    routed = lax.dot_general(p.astype(bf16), ybuf[slot].reshape(SLOTS, D_MODEL), (((0,), (0,)), ((), ())),
                             preferred_element_type=f32)
    out = x + mod_ref[5:6, :] * (routed + shared)
    if final:
        y = (out * lax.rsqrt(jnp.mean(out * out, axis=-1, keepdims=True) + EPS)) * gf_ref[...]

        @pl.when(i < N_PROMPT_TILES)
        def _():
            yp_ref[...] = y

        @pl.when(i >= N_PROMPT_TILES)
        def _():
            ys_ref[...] = y
    else:
        o_ref[...] = out

    @pl.when(i == n - 1)
    def _():
        wait(1 - slot)


def _combine(chunk_map, y, slots, gates, x, mod_l, g, sg, su, sd, final_g=None):
    shd = sg.shape[1]
    final = final_g is not None
    row_spec = pl.BlockSpec((TM, D_MODEL), lambda i, cm: (i, 0))
    vec_spec = pl.BlockSpec((1, D_MODEL), lambda i, cm: (0, 0))
    if final:
        out_specs = [pl.BlockSpec((TM, D_MODEL), lambda i, cm: _prompt_block(i)),
                     pl.BlockSpec((TM, D_MODEL), lambda i, cm: _sample_block(i))]
        out_shape = [jax.ShapeDtypeStruct((N_PROMPT, D_MODEL), f32), jax.ShapeDtypeStruct((N_SAMPLE, D_MODEL), f32)]
    else:
        out_specs, out_shape = row_spec, jax.ShapeDtypeStruct((N_TOK, D_MODEL), f32)
    grid_spec = pltpu.PrefetchScalarGridSpec(
        num_scalar_prefetch=1,
        grid=(N_TILES,),
        in_specs=[
            pl.BlockSpec(memory_space=pl.ANY),
            pl.BlockSpec((None, 8, TM), lambda i, cm: (i, 0, 0)),
            pl.BlockSpec((None, 8, TM), lambda i, cm: (i, 0, 0)),
            row_spec,
            pl.BlockSpec((None, 6, D_MODEL), lambda i, cm: (_mod_row(i), 0, 0)),
            vec_spec,
            pl.BlockSpec((D_MODEL, shd), lambda i, cm: (0, 0)),
            pl.BlockSpec((D_MODEL, shd), lambda i, cm: (0, 0)),
            pl.BlockSpec((shd, D_MODEL), lambda i, cm: (0, 0)),
        ] + ([vec_spec] if final else []),
        out_specs=out_specs,
        scratch_shapes=[pltpu.VMEM((2, SLOT_CHUNKS, CHUNK, D_MODEL), bf16), pltpu.SemaphoreType.DMA((2,))],
    )
    args = (chunk_map, y, slots, gates, x, mod_l, g, sg, su, sd) + ((final_g,) if final else ())
    return pl.pallas_call(
        functools.partial(_combine_kernel, final=final),
        grid_spec=grid_spec,
        out_shape=out_shape,
        compiler_params=_cparams(("arbitrary",)),
        name="combine",
    )(*args)


def _moe(x_parts, o_prompt, o_sample, w_out, mod_l, g, rwt, rb, wg, wu, wd, layer, sg, su, sd, final_g=None):
    x, xloc, slots, gates, run_len = _route(x_parts, o_prompt, o_sample, mod_l, g, w_out, rwt, rb)
    gmm_plan, chunk_map = _moe_plan(run_len[:, :, 0])
    y = _gmm(gmm_plan, xloc.reshape(N_TILES * SLOT_CHUNKS, CHUNK, D_MODEL), wg, wu, wd, layer)
    return _combine(chunk_map, y, slots, gates, x, mod_l, g, sg, su, sd, final_g)


_L0_CHUNKS = (
    (0, 512, 0, (0, 1, 2, 3), ()),
    (512, 1024, 768, (), ()),
    (1024, 1536, 1280, (), ((2, 0, 512, 0, False),)),
    (1536, 2048, 1792, (), ((3, 0, 512, 0, False),)),
    (2048, 2304, 512, (0,), ((0, 0, 128, 0, False), (1, 128, 256, 0, False))),
)
_L0_KV_OUTS = (("T", 128), ("T", 128), ("T", 512), ("T", 512))
_L1_CHUNKS = (
    (0, 512, 0, (0, 1, 2, 3), ()),
    (512, 1024, 512, (0, 1, 2, 3), ()),
    (1024, 1536, 1024, (0, 1, 2, 3), ((0, 0, 512, 0, False),)),
    (1536, 2048, 1536, (0, 1, 2, 3), ((0, 0, 512, 512, False),)),
    (2048, 2560, 2048, (), ((1, 0, 512, 0, True),)),
    (2560, 3072, 2560, (), ((1, 0, 512, 512, True),)),
)
_L1_KV_OUTS = (("T", 1024), ("H", 8))


def _from_feature_major(kt, *head_dims):
    nb, _, s = kt.shape
    nd = len(head_dims)
    return kt.reshape(nb, *head_dims, s).transpose(0, nd + 1, *range(1, nd + 1))[:, None]


def kernel(x_prompt, x_sample, cache_a_k, cache_a_v, cache_b_k, cache_b_v, cache_c_k, cache_c_v, c, c_ctx, w_mod, b_mod, norm_mix, norm_ffn, w_in_ab, w_out_ab, sink_a, rel_bias_b, w_in_c, w_out_c, lam_q1, lam_k1, lam_q2, lam_k2, subln_c, router_w, router_bias, exp_w_gate, exp_w_up, exp_w_down, sh_w_gate, sh_w_up, sh_w_down, final_norm):
    x = (x_prompt.reshape(N_PROMPT, D_MODEL), x_sample.reshape(N_SAMPLE, D_MODEL))
    cond8 = jnp.concatenate([c_ctx[None, :], c, jnp.zeros((8 - 1 - N_SAMPLE_BATCH, D_MODEL), f32)], axis=0)
    mod = _adaln(cond8, w_mod, b_mod).reshape(DEPTH, 8, 6, D_MODEL)
    rope_tabs = _rope_tables()
    new_kv = {}
    for layer in range(DEPTH):
        li = layer // 2
        mod_l = mod[layer]
        g_mix = norm_mix[layer][None, :]
        g_ffn = norm_ffn[layer][None, :]
        if layer % 2 == 0:
            w_in = w_in_ab[li].astype(bf16)
            qkv, ak, av, bk, bv = _inproj(x, mod_l, g_mix, w_in, rope_tabs, _L0_CHUNKS, _L0_KV_OUTS)
            new_kv["a_k"], new_kv["a_v"], new_kv["b_k"], new_kv["b_v"] = ak, av, bk, bv
            o_p = _ctx0(sink_a[li], qkv)
            o_s = _lat0(sink_a[li], qkv,
                        cache_a_k[:, li].reshape(N_SAMPLE_BATCH, PAST_LEN, LANES),
                        cache_a_v[:, li].reshape(N_SAMPLE_BATCH, PAST_LEN, LANES),
                        cache_b_k[:, li].reshape(N_SAMPLE_BATCH, PAST_LEN, 512),
                        cache_b_v[:, li].reshape(N_SAMPLE_BATCH, PAST_LEN, 512),
                        _na_bias_tiles(rel_bias_b[li]))
            w_out = w_out_ab[li].astype(bf16)
        else:
            lam_init = 0.8 - 0.6 * math.exp(-0.3 * layer)
            qkv, ck, cv = _inproj(x, mod_l, g_mix, w_in_c[li].astype(bf16), rope_tabs, _L1_CHUNKS, _L1_KV_OUTS)
            new_kv["c_k"], new_kv["c_v"] = ck, cv
            lamv = jnp.concatenate([lam_q1[li][None], lam_k1[li][None], lam_q2[li][None], lam_k2[li][None],
                                    jnp.zeros((4, HEAD_DIM), f32)], axis=0)
            subln = subln_c[li][None, :]
            o_p = _ctx1(lamv, subln, qkv, lam_init)
            o_s = _lat1(lamv, subln, qkv,
                        cache_c_k[:, li].reshape(N_SAMPLE_BATCH, PAST_LEN, D_MODEL),
                        cache_c_v[:, li].reshape(N_SAMPLE_BATCH, PAST_LEN, D_MODEL), lam_init)
            w_out = w_out_c[li].astype(bf16)
        last = layer == DEPTH - 1
        x = _moe(x, o_p, o_s, w_out, mod_l, g_ffn, router_w[layer].T, router_bias[layer][:, None],
                 exp_w_gate, exp_w_up, exp_w_down, layer,
                 sh_w_gate[layer].astype(bf16), sh_w_up[layer].astype(bf16), sh_w_down[layer].astype(bf16),
                 final_norm[None, :] if last else None)
        x = x if last else (x,)
    y_prompt, y_sample = x
    nb, s = N_PROMPT_BATCH, PROMPT_SEQ
    return (y_prompt.reshape(nb, s, D_MODEL), y_sample.reshape(N_SAMPLE_BATCH, SAMPLE_SEQ, D_MODEL),
            _from_feature_major(new_kv["a_k"], 2, HEAD_DIM), _from_feature_major(new_kv["a_v"], 2, HEAD_DIM),
            _from_feature_major(new_kv["b_k"], 8, HEAD_DIM), _from_feature_major(new_kv["b_v"], 8, HEAD_DIM),
            _from_feature_major(new_kv["c_k"], 8, 2, HEAD_DIM), new_kv["c_v"].reshape(nb, 1, s, 8, 2 * HEAD_DIM))
```

```python
import functools
import math

import jax
import jax.numpy as jnp
import numpy as np
from jax import lax
from jax.experimental import pallas as pl
from jax.experimental.pallas import tpu as pltpu

f32 = jnp.float32
bf16 = jnp.bfloat16
i32 = jnp.int32

D_MODEL = 1024
N_PROMPT_BATCH = 16
PROMPT_SEQ = 256
DEPTH = 2
N_SAMPLE_BATCH = 2
SAMPLE_SEQ = 2048
PAST_LEN = 512
GRID_W = 64
HEAD_DIM = 64
ROPE_THETA = 10000.0
EPS = 1e-6
A_WINDOW = 128
NA_ROWS = 8
NA_COLS = 16
N_EXPERTS = 64
TOP_K = 6
N_GROUPS = 8
TOPK_GROUPS = 4
EXPERT_DIM = 256
ROUTED_SCALE = 2.5
Q_SCALE = HEAD_DIM ** -0.5

N_PROMPT = N_PROMPT_BATCH * PROMPT_SEQ
N_SAMPLE = N_SAMPLE_BATCH * SAMPLE_SEQ
N_TOK = N_PROMPT + N_SAMPLE

LANES = 128
TM = 256
TD = 512
N_PROMPT_TILES = N_PROMPT // TM
N_TILES = N_TOK // TM
QB = 128
CHUNK = 16
SLOTS = -(-(TM * TOP_K + N_EXPERTS * (CHUNK - 1)) // 256) * 256
SLOT_CHUNKS = SLOTS // CHUNK
ROUTE_ROWS = 512
ROUTE_TILES = 2
GM = 256
_MAX_SORTED = TM * TOP_K * N_TILES + N_TILES * N_EXPERTS * (CHUNK - 1) + N_EXPERTS * (GM - CHUNK)
G_TILES = -(-_MAX_SORTED // GM)
G_CHUNKS = GM // CHUNK
GMM_ITEM_TILES = 2
GMM_SLOTS = 4
GMM_MAX_ITEMS = (G_TILES + N_EXPERTS * (GMM_ITEM_TILES - 1)) // GMM_ITEM_TILES
VMEM_LIMIT = 56 * 1024 * 1024

NEG = -1e30


def _cparams(sem):
    return pltpu.CompilerParams(dimension_semantics=sem, vmem_limit_bytes=VMEM_LIMIT)


def _mod_row(i, tm=TM):
    return jnp.where(i < N_PROMPT // tm, 0, 1 + (i - N_PROMPT // tm) // (SAMPLE_SEQ // tm))


def _prompt_block(i, tm=TM):
    return (jnp.minimum(i, N_PROMPT // tm - 1), 0)


def _sample_block(i, tm=TM):
    return (jnp.maximum(i - N_PROMPT // tm, 0), 0)


def _x_specs(parts, tm=TM):
    if len(parts) == 1:
        return [pl.BlockSpec((tm, D_MODEL), lambda i, *_: (i, 0))]
    return [pl.BlockSpec((tm, D_MODEL), lambda i, *_: _prompt_block(i, tm)),
            pl.BlockSpec((tm, D_MODEL), lambda i, *_: _sample_block(i, tm))]


def _load_x(i, x_refs, tm=TM):
    if len(x_refs) == 1:
        return x_refs[0][...]
    return jnp.where(i < N_PROMPT // tm, x_refs[0][...], x_refs[1][...])


def _norm_mod(x, g, scale, shift):
    y = x * lax.rsqrt(jnp.mean(x * x, axis=-1, keepdims=True) + EPS)
    return (y * g) * (1.0 + scale) + shift


def _silu(x):
    return x * jax.nn.sigmoid(x)


def _dot(a, b):
    return jnp.dot(a, b, preferred_element_type=f32)


def _dot_nt(a, b):
    return lax.dot_general(a, b, (((1,), (1,)), ((), ())), preferred_element_type=f32)


ADA_COLS = 1536


def _adaln_kernel(cond_ref, w_ref, b_ref, o_ref):
    s = _silu(cond_ref[...]).astype(bf16)
    o_ref[...] = _dot(s, w_ref[...].astype(bf16)) + b_ref[...]


def _adaln(cond8, w_mod, b_mod):
    n6 = 6 * D_MODEL
    return pl.pallas_call(
        _adaln_kernel,
        grid=(DEPTH, n6 // ADA_COLS),
        in_specs=[
            pl.BlockSpec((8, D_MODEL), lambda l, j: (0, 0)),
            pl.BlockSpec((None, D_MODEL, ADA_COLS), lambda l, j: (l, 0, j)),
            pl.BlockSpec((None, 1, ADA_COLS), lambda l, j: (l, 0, j)),
        ],
        out_specs=pl.BlockSpec((None, 8, ADA_COLS), lambda l, j: (l, 0, j)),
        out_shape=jax.ShapeDtypeStruct((DEPTH, 8, n6), f32),
        compiler_params=_cparams(("parallel", "parallel")),
        name="adaln",
    )(cond8, w_mod, b_mod.reshape(DEPTH, 1, n6))


def _rope_block(blk, cos, sin_a, sin_b):
    return blk * cos + pltpu.roll(blk, LANES - 16, 1) * sin_a + pltpu.roll(blk, 16, 1) * sin_b


def _inproj_kernel(*refs, chunks, n_x):
    x_refs, kv_refs = refs[:n_x], refs[n_x + 7:]
    mod_ref, g_ref, w_ref, cos_ref, sa_ref, sb_ref, qkv_ref = refs[n_x:n_x + 7]
    i = pl.program_id(0)
    h = _norm_mod(_load_x(i, x_refs, TD), g_ref[...], mod_ref[1:2, :], mod_ref[0:1, :]).astype(bf16)
    is_prompt = i < N_PROMPT // TD

    @pl.when(is_prompt)
    def _():
        for c0, c1, s0, _, kv_out in chunks:
            acc = _dot(h, w_ref[:, s0:s0 + (c1 - c0)])
            qkv_ref[:, c0:c1] = acc.astype(bf16)
            for ridx, a0, a1, o0, per_head in kv_out:
                if per_head:
                    heads = kv_refs[ridx].shape[0] // TD
                    for j in range((a1 - a0) // LANES):
                        kv_refs[ridx][pl.ds(o0 // LANES + j, TD, stride=heads), :] = (
                            acc[:, a0 + j * LANES:a0 + (j + 1) * LANES])
                else:
                    t = acc[:, a0:a1].T
                    for b in range(TD // PROMPT_SEQ):
                        kv_refs[ridx][b, o0:o0 + (a1 - a0), :] = t[:, b * PROMPT_SEQ:(b + 1) * PROMPT_SEQ]

    @pl.when(jnp.logical_not(is_prompt))
    def _():
        cos, sa, sb = cos_ref[...], sa_ref[...], sb_ref[...]
        for c0, c1, s0, rope_blocks, _ in chunks:
            acc = _dot(h, w_ref[:, s0:s0 + (c1 - c0)])
            for b in range((c1 - c0) // LANES):
                blk = acc[:, b * LANES:(b + 1) * LANES]
                if b in rope_blocks:
                    blk = _rope_block(blk, cos, sa, sb)
                qkv_ref[:, c0 + b * LANES:c0 + (b + 1) * LANES] = blk.astype(bf16)


def _inproj(x_parts, mod_l, g, w, rope_tabs, chunks, kv_outs):
    n = w.shape[1]
    cos, sa, sb = rope_tabs
    bpt = TD // PROMPT_SEQ

    def rope_idx(i):
        return (jnp.where(i < N_PROMPT // TD, 0, (i - N_PROMPT // TD) % (SAMPLE_SEQ // TD)), 0)

    kv_specs, kv_shapes = [], []
    for kind, size in kv_outs:
        if kind == "T":
            kv_specs.append(pl.BlockSpec((bpt, size, PROMPT_SEQ), lambda i: _prompt_block(i, TD) + (0,)))
            kv_shapes.append(jax.ShapeDtypeStruct((N_PROMPT_BATCH, size, PROMPT_SEQ), f32))
        else:
            kv_specs.append(pl.BlockSpec((TD * size, LANES), lambda i: _prompt_block(i, TD)))
            kv_shapes.append(jax.ShapeDtypeStruct((N_PROMPT * size, LANES), f32))

    return pl.pallas_call(
        functools.partial(_inproj_kernel, chunks=chunks, n_x=len(x_parts)),
        grid=(N_TOK // TD,),
        in_specs=_x_specs(x_parts, TD) + [
            pl.BlockSpec((None, 6, D_MODEL), lambda i: (_mod_row(i, TD), 0, 0)),
            pl.BlockSpec((1, D_MODEL), lambda i: (0, 0)),
            pl.BlockSpec((D_MODEL, n), lambda i: (0, 0)),
            pl.BlockSpec((TD, LANES), rope_idx),
            pl.BlockSpec((TD, LANES), rope_idx),
            pl.BlockSpec((TD, LANES), rope_idx),
        ],
        out_specs=[pl.BlockSpec((TD, n), lambda i: (i, 0))] + kv_specs,
        out_shape=[jax.ShapeDtypeStruct((N_TOK, n), bf16)] + kv_shapes,
        compiler_params=_cparams(("arbitrary",)),
        name="inproj",
    )(*x_parts, mod_l, g, w, cos, sa, sb)


def _rope_tables():
    nq = HEAD_DIM // 4
    t = np.arange(SAMPLE_SEQ)
    inv = np.power(np.float32(ROPE_THETA), -np.arange(nq, dtype=np.float32) / np.float32(nq))
    ang_r = (t // GRID_W).astype(np.float32)[:, None] * inv
    ang_c = (t % GRID_W).astype(np.float32)[:, None] * inv
    zero = np.zeros_like(ang_r)

    def head(fr, fc):
        return np.concatenate([fr[0], fr[1], fc[0], fc[1]], axis=-1)

    cos = head((np.cos(ang_r), np.cos(ang_r)), (np.cos(ang_c), np.cos(ang_c)))
    sin_a = head((-np.sin(ang_r), zero), (-np.sin(ang_c), zero))
    sin_b = head((zero, np.sin(ang_r)), (zero, np.sin(ang_c)))
    two = lambda a: jnp.asarray(np.concatenate([a, a], axis=-1).astype(np.float32))
    return two(cos), two(sin_a), two(sin_b)


def _lane_lo(shape):
    return lax.broadcasted_iota(i32, shape, len(shape) - 1) < HEAD_DIM


def _half(q, lo_mask, half):
    keep = lo_mask if half == 0 else jnp.logical_not(lo_mask)
    return jnp.where(keep, q, jnp.zeros_like(q)) * Q_SCALE


def _swap_halves(x):
    return pltpu.roll(x.astype(f32), HEAD_DIM, 1).astype(x.dtype)


def _stack_halves(q, lo_mask):
    return jnp.concatenate([_half(q, lo_mask, 0), _half(q, lo_mask, 1)], axis=0)


def _with_ones(v):
    return jnp.concatenate([v, jnp.ones_like(v)], axis=1)


def _attend(q_rows, n_heads, key_blocks, vx_blocks, fix_scores=None, sinks=None):
    return _attend_many([(q_rows, n_heads, key_blocks, vx_blocks, fix_scores, sinks)])[0]


def _attend_many(problems):
    scores = [[_dot_nt(q_rows, k) for k in key_blocks] for q_rows, _, key_blocks, _, _, _ in problems]
    exps, maxes = [], []
    for (q_rows, n_heads, key_blocks, _, fix_scores, sinks), sc in zip(problems, scores):
        r = q_rows.shape[0] // n_heads
        e_p, m_p = [[] for _ in key_blocks], []
        for h in range(n_heads):
            blocks = [s[h * r:(h + 1) * r] for s in sc]
            if fix_scores is not None:
                blocks = [fix_scores(h, i, s) for i, s in enumerate(blocks)]
            m = functools.reduce(jnp.maximum, [jnp.max(s, axis=-1, keepdims=True) for s in blocks])
            if sinks is not None:
                m = jnp.maximum(m, sinks[h])
            m_p.append(m)
            for i, s in enumerate(blocks):
                e_p[i].append(jnp.exp((s - m).astype(bf16)))
        exps.append(e_p)
        maxes.append(m_p)
    outs = [functools.reduce(lambda a, b: a + b,
                             [_dot(e[0] if n_heads == 1 else jnp.concatenate(e, axis=0), vx)
                              for e, vx in zip(e_p, vx_blocks)])
            for (_, n_heads, _, vx_blocks, _, _), e_p in zip(problems, exps)]
    results = []
    for (q_rows, n_heads, _, _, _, sinks), out, m_p in zip(problems, outs, maxes):
        r = q_rows.shape[0] // n_heads
        res = []
        for h in range(n_heads):
            den = out[h * r:(h + 1) * r, LANES:]
            if sinks is not None:
                den = den + jnp.exp(sinks[h] - m_p[h])
            res.append(out[h * r:(h + 1) * r, :LANES] * (1.0 / den))
        results.append(res)
    return results


def _gqa_rows(q_blocks, group, lo_mask):
    parts = []
    for q in q_blocks:
        for half in range(2):
            qh = _half(q, lo_mask, half)
            parts.append(qh if half == group else _swap_halves(qh))
    return jnp.concatenate(parts, axis=0)


def _gqa_merge(outs, group, lo_mask):
    fixed = [o if idx % 2 == group else pltpu.roll(o, HEAD_DIM, 1) for idx, o in enumerate(outs)]
    return [jnp.where(lo_mask, fixed[2 * p], fixed[2 * p + 1]) for p in range(len(outs) // 2)]


L0_QA, L0_QB, L0_KB, L0_VB, L0_KA, L0_VA, L0_N = 0, 512, 1024, 1536, 2048, 2176, 2304


def _ctx0_kernel(sink_ref, qkv_ref, o_ref):
    lo = _lane_lo((1, LANES))
    blk = lambda base, j: qkv_ref[:, base + j * LANES:base + (j + 1) * LANES]
    k_a = blk(L0_KA, 0)
    vx_a = _with_ones(blk(L0_VA, 0))
    for g in range(2):
        q_rows = _gqa_rows([blk(L0_QA, 2 * g), blk(L0_QA, 2 * g + 1)], g, lo)
        outs = _attend(q_rows, 4, [k_a], [vx_a], sinks=[sink_ref[4 * g + idx] for idx in range(4)])
        for p, o in enumerate(_gqa_merge(outs, g, lo)):
            j = 2 * g + p
            o_ref[:, j * LANES:(j + 1) * LANES] = o.astype(bf16)
    for j in range(4):
        outs = _attend(_stack_halves(blk(L0_QB, j), lo), 2, [blk(L0_KB, j)], [_with_ones(blk(L0_VB, j))])
        o_ref[:, 512 + j * LANES:512 + (j + 1) * LANES] = jnp.where(lo, outs[0], outs[1]).astype(bf16)


def _ctx0(sink, qkv):
    return pl.pallas_call(
        _ctx0_kernel,
        grid=(N_PROMPT_BATCH,),
        in_specs=[
            pl.BlockSpec(memory_space=pltpu.SMEM),
            pl.BlockSpec((PROMPT_SEQ, L0_N), lambda b: (b, 0)),
        ],
        out_specs=pl.BlockSpec((PROMPT_SEQ, D_MODEL), lambda b: (b, 0)),
        out_shape=jax.ShapeDtypeStruct((N_PROMPT, D_MODEL), bf16),
        compiler_params=_cparams(("parallel",)),
        name="ctx0",
    )(sink, qkv)


WIN_KEYS = 3 * QB
NA_KEY_ROWS = 10
NA_KEYS = NA_KEY_ROWS * GRID_W
N_QB = SAMPLE_SEQ // QB
N_NA_PATTERNS = 5
_PROMPT_QBLOCKS = N_PROMPT // QB


def _na_pattern(n):
    return jnp.where(n < 2, n, jnp.where(n > N_QB - 3, n - (N_QB - 5), 2))


def _lat0_kernel(sink_ref, roff_ref, q_ref, kvb_ref, kva_ref, cak_ref, cav_ref, cbk_ref, cbv_ref, tiles_ref, o_ref):
    n = pl.program_id(1)
    lo = _lane_lo((1, LANES))
    kstart = pl.multiple_of(jnp.clip((n - 1) * QB, 0, SAMPLE_SEQ - WIN_KEYS), QB)
    k_a = kva_ref[pl.ds(kstart, WIN_KEYS), 0:LANES]
    v_a = kva_ref[pl.ds(kstart, WIN_KEYS), LANES:2 * LANES]
    c_k = cak_ref[...].astype(bf16)
    keys_a = [c_k, k_a]
    vx_a = [_with_ones(cav_ref[...].astype(bf16)), _with_ones(v_a)]
    qpos = n * QB + lax.broadcasted_iota(i32, (QB, WIN_KEYS), 0)
    kpos = kstart + lax.broadcasted_iota(i32, (QB, WIN_KEYS), 1)
    in_window = jnp.abs(qpos - kpos) <= A_WINDOW
    mask_window = lambda h, i, s: jnp.where(in_window, s, NEG) if i == 1 else s
    problems = []
    for g in range(2):
        q_rows = _gqa_rows([q_ref[:, L0_QA + j * LANES:L0_QA + (j + 1) * LANES] for j in (2 * g, 2 * g + 1)], g, lo)
        problems.append((q_rows, 4, keys_a, vx_a, mask_window, [sink_ref[4 * g + idx] for idx in range(4)]))
    krow = jnp.clip(2 * n - NA_ROWS // 2, 0, SAMPLE_SEQ // GRID_W - NA_KEY_ROWS)
    ktok = pl.multiple_of(krow * GRID_W, QB)
    pattern = _na_pattern(n)

    def na_bias(head):
        rows = []
        for rq in range(QB // GRID_W):
            blocks = []
            for kb in range(NA_KEY_ROWS // 2):
                d0, d1 = (roff_ref[(pattern * 2 + rq) * NA_KEY_ROWS + 2 * kb + t] for t in range(2))
                blocks.append(jnp.where(lo, tiles_ref[head, d0], tiles_ref[head, d1]))
            rows.append(jnp.concatenate(blocks, axis=1))
        return jnp.concatenate(rows, axis=0)

    for j in range(4):
        q_b = q_ref[:, L0_QB + j * LANES:L0_QB + (j + 1) * LANES]
        k_b = kvb_ref[pl.ds(ktok, NA_KEYS), j * LANES:(j + 1) * LANES]
        v_b = kvb_ref[pl.ds(ktok, NA_KEYS), 512 + j * LANES:512 + (j + 1) * LANES]
        cb_k = cbk_ref[:, j * LANES:(j + 1) * LANES].astype(bf16)
        cb_v = cbv_ref[:, j * LANES:(j + 1) * LANES].astype(bf16)
        add_bias = lambda h, i, s, j=j: s + na_bias(2 * j + h) if i == 1 else s
        problems.append((_stack_halves(q_b, lo), 2, [cb_k, k_b], [_with_ones(cb_v), _with_ones(v_b)], add_bias, None))

    results = _attend_many(problems)
    for g in range(2):
        for p, o in enumerate(_gqa_merge(results[g], g, lo)):
            j = 2 * g + p
            o_ref[:, j * LANES:(j + 1) * LANES] = o.astype(bf16)
    for j in range(4):
        outs = results[2 + j]
        o_ref[:, 512 + j * LANES:512 + (j + 1) * LANES] = jnp.where(lo, outs[0], outs[1]).astype(bf16)


def _lat0(sink, qkv, cak, cav, cbk, cbv, bias_tiles):
    sb = N_PROMPT // SAMPLE_SEQ
    return pl.pallas_call(
        _lat0_kernel,
        grid=(N_SAMPLE_BATCH, N_QB),
        in_specs=[
            pl.BlockSpec(memory_space=pltpu.SMEM),
            pl.BlockSpec(memory_space=pltpu.SMEM),
            pl.BlockSpec((QB, 1024), lambda b, n: (_PROMPT_QBLOCKS + b * N_QB + n, 0)),
            pl.BlockSpec((SAMPLE_SEQ, 1024), lambda b, n: (sb + b, 1)),
            pl.BlockSpec((SAMPLE_SEQ, 256), lambda b, n: (sb + b, L0_KA // 256)),
            pl.BlockSpec((None, PAST_LEN, LANES), lambda b, n: (b, 0, 0)),
            pl.BlockSpec((None, PAST_LEN, LANES), lambda b, n: (b, 0, 0)),
            pl.BlockSpec((None, PAST_LEN, 512), lambda b, n: (b, 0, 0)),
            pl.BlockSpec((None, PAST_LEN, 512), lambda b, n: (b, 0, 0)),
            pl.BlockSpec((8, N_ROW_OFFSETS + 1, GRID_W, LANES), lambda b, n: (0, 0, 0, 0)),
        ],
        out_specs=pl.BlockSpec((QB, D_MODEL), lambda b, n: (b * N_QB + n, 0)),
        out_shape=jax.ShapeDtypeStruct((N_SAMPLE, D_MODEL), bf16),
        compiler_params=_cparams(("parallel", "arbitrary")),
        name="lat0",
    )(sink, jnp.asarray(_na_row_offsets()), qkv, qkv, qkv, cak, cav, cbk, cbv, bias_tiles)


N_ROW_OFFSETS = 2 * NA_ROWS - 1


def _na_row_offsets():
    rows = SAMPLE_SEQ // GRID_W
    idx = np.full((N_NA_PATTERNS, 2, NA_KEY_ROWS), N_ROW_OFFSETS, np.int32)
    for p, n in enumerate((0, 1, 2, N_QB - 2, N_QB - 1)):
        k0 = int(np.clip(2 * n - NA_ROWS // 2, 0, rows - NA_KEY_ROWS))
        for rq in range(2):
            r = 2 * n + rq
            rs = int(np.clip(r - NA_ROWS // 2, 0, rows - NA_ROWS))
            for kl in range(NA_KEY_ROWS):
                if rs <= k0 + kl < rs + NA_ROWS:
                    idx[p, rq, kl] = k0 + kl - r + NA_ROWS - 1
    return idx.reshape(-1)


def _na_bias_tiles(rel_bias):
    n_dc = 2 * NA_COLS - 1
    c = np.arange(GRID_W)[:, None]
    kc = np.arange(GRID_W)[None, :]
    cs = np.clip(c - NA_COLS // 2, 0, GRID_W - NA_COLS)
    col_ok = (kc >= cs) & (kc < cs + NA_COLS)
    col_hot = ((kc - c + NA_COLS - 1)[None] == np.arange(n_dc)[:, None, None]) & col_ok[None]
    hp = lax.Precision.HIGHEST
    tiles = jnp.einsum("hdx,xck->hdck", rel_bias.astype(f32), col_hot.astype(np.float32), precision=hp)
    tiles = tiles + np.where(col_ok, 0.0, NEG).astype(np.float32)
    tiles = jnp.concatenate([tiles, jnp.full((tiles.shape[0], 1, GRID_W, GRID_W), NEG, f32)], axis=1)
    return jnp.concatenate([tiles, tiles], axis=-1)


def _diff_lambda(lam_ref, lam_init):
    lv = lam_ref[...]
    s1 = jnp.sum(lv[0:1, :] * lv[1:2, :], axis=-1, keepdims=True)
    s2 = jnp.sum(lv[2:3, :] * lv[3:4, :], axis=-1, keepdims=True)
    return jnp.exp(s1) - jnp.exp(s2) + lam_init


DIFF_HEADS_PER_PASS = 2


def _diff_heads(heads, o_ref, lam, subln, lo, lam_init):
    for p0 in range(0, len(heads), DIFF_HEADS_PER_PASS):
        group = heads[p0:p0 + DIFF_HEADS_PER_PASS]
        results = _attend_many([(_stack_halves(q, lo), 2, ks, [_with_ones(v) for v in vs], None, None)
                                for _, q, ks, vs in group])
        for (h, _, _, _), (o1, o2) in zip(group, results):
            o = o1 - lam * o2
            o = o * lax.rsqrt(jnp.mean(o * o, axis=-1, keepdims=True) + EPS)
            o_ref[:, h * LANES:(h + 1) * LANES] = ((o * subln) * (1.0 - lam_init)).astype(bf16)


def _ctx1_kernel(lam_ref, subln_ref, qkv_ref, o_ref, *, lam_init):
    lo = _lane_lo((1, LANES))
    lam = _diff_lambda(lam_ref, lam_init)
    blk = lambda base, h: qkv_ref[:, base + h * LANES:base + (h + 1) * LANES]
    heads = [(h, blk(0, h), [blk(D_MODEL, h)], [blk(2 * D_MODEL, h)]) for h in range(8)]
    _diff_heads(heads, o_ref, lam, subln_ref[...], lo, lam_init)


def _ctx1(lamv, subln, qkv, lam_init):
    return pl.pallas_call(
        functools.partial(_ctx1_kernel, lam_init=lam_init),
        grid=(N_PROMPT_BATCH,),
        in_specs=[
            pl.BlockSpec((8, HEAD_DIM), lambda b: (0, 0)),
            pl.BlockSpec((1, LANES), lambda b: (0, 0)),
            pl.BlockSpec((PROMPT_SEQ, 3 * D_MODEL), lambda b: (b, 0)),
        ],
        out_specs=pl.BlockSpec((PROMPT_SEQ, D_MODEL), lambda b: (b, 0)),
        out_shape=jax.ShapeDtypeStruct((N_PROMPT, D_MODEL), bf16),
        compiler_params=_cparams(("parallel",)),
        name="ctx1",
    )(lamv, subln, qkv)


def _lat1_kernel(lam_ref, subln_ref, q_ref, k_ref, v_ref, ck_ref, cv_ref, o_ref, *, lam_init):
    lo = _lane_lo((1, LANES))
    lam = _diff_lambda(lam_ref, lam_init)
    heads = []
    for h in range(8):
        sl = slice(h * LANES, (h + 1) * LANES)
        heads.append((h, q_ref[:, sl], [ck_ref[:, sl].astype(bf16), k_ref[:, sl]],
                      [cv_ref[:, sl].astype(bf16), v_ref[:, sl]]))
    _diff_heads(heads, o_ref, lam, subln_ref[...], lo, lam_init)


def _lat1(lamv, subln, qkv, ck, cv, lam_init):
    sb = N_PROMPT // SAMPLE_SEQ
    nq = SAMPLE_SEQ // TM
    return pl.pallas_call(
        functools.partial(_lat1_kernel, lam_init=lam_init),
        grid=(N_SAMPLE_BATCH, nq),
        in_specs=[
            pl.BlockSpec((8, HEAD_DIM), lambda b, n: (0, 0)),
            pl.BlockSpec((1, LANES), lambda b, n: (0, 0)),
            pl.BlockSpec((TM, D_MODEL), lambda b, n: (N_PROMPT_TILES + b * nq + n, 0)),
            pl.BlockSpec((SAMPLE_SEQ, D_MODEL), lambda b, n: (sb + b, 1)),
            pl.BlockSpec((SAMPLE_SEQ, D_MODEL), lambda b, n: (sb + b, 2)),
            pl.BlockSpec((None, PAST_LEN, D_MODEL), lambda b, n: (b, 0, 0)),
            pl.BlockSpec((None, PAST_LEN, D_MODEL), lambda b, n: (b, 0, 0)),
        ],
        out_specs=pl.BlockSpec((TM, D_MODEL), lambda b, n: (b * nq + n, 0)),
        out_shape=jax.ShapeDtypeStruct((N_SAMPLE, D_MODEL), bf16),
        compiler_params=_cparams(("parallel", "arbitrary")),
        name="lat1",
    )(lamv, subln, qkv, qkv, qkv, ck, cv)


def _split_bf16(a):
    hi = a.astype(bf16)
    return hi, (a - hi.astype(f32)).astype(bf16)


def _route_kernel(*refs, n_x):
    x_refs = refs[:n_x]
    (op_ref, os_ref, mod_ref, g_ref, wo_ref, rwt_ref, rb_ref,
     xnew_ref, xloc_ref, slots_ref, gate_ref, len_ref) = refs[n_x:]
    logits = [_route_logits(t, x_refs, op_ref, os_ref, mod_ref, g_ref, wo_ref, rwt_ref, xnew_ref)
              for t in range(ROUTE_TILES)]
    tiles = _route_tiles(jnp.concatenate([lg for _, lg in logits], axis=1), rb_ref, slots_ref, gate_ref, len_ref)
    for t, (slots, run_len) in enumerate(tiles):
        _route_dispatch(t, slots, logits[t][0], run_len, xloc_ref)


def _route_logits(t, x_refs, op_ref, os_ref, mod_ref, g_ref, wo_ref, rwt_ref, xnew_ref):
    is_prompt = pl.program_id(0) < N_PROMPT_TILES // ROUTE_TILES
    rows = slice(t * TM, (t + 1) * TM)
    attn = jnp.where(is_prompt, op_ref[rows, :], os_ref[rows, :])
    x_in = x_refs[0][rows, :] if len(x_refs) == 1 else jnp.where(is_prompt, x_refs[0][rows, :], x_refs[1][rows, :])
    x = x_in + mod_ref[2:3, :] * _dot(attn, wo_ref[...])
    xnew_ref[rows, :] = x
    h = _norm_mod(x, g_ref[...], mod_ref[4:5, :], mod_ref[3:4, :])
    h_hi, h_lo = _split_bf16(h)
    w_hi, w_lo = _split_bf16(rwt_ref[...])
    return h_hi, _dot_nt(w_hi, h_hi) + (_dot_nt(w_hi, h_lo) + _dot_nt(w_lo, h_hi))


def _route_tiles(logits, rb_ref, slots_ref, gate_ref, len_ref):
    ng, ge = N_GROUPS, N_EXPERTS // N_GROUPS
    n = ROUTE_TILES * TM
    tile = lambda a, t: a[..., t * TM:(t + 1) * TM]
    scores = jax.nn.sigmoid(logits)
    biased = scores + rb_ref[...]
    s3 = scores.reshape(ng, ge, n)
    b3 = biased.reshape(ng, ge, n)
    in_group = lax.broadcasted_iota(i32, (ng, ge, n), 1).astype(f32)
    group_id = lax.broadcasted_iota(i32, (ng, 1, n), 0).astype(f32)
    expert_id = lax.broadcasted_iota(i32, (ng, ge, n), 0).astype(f32) * ge + in_group

    def max01(a):
        return jnp.max(jnp.max(a, axis=0, keepdims=True), axis=1, keepdims=True)

    def min01(a):
        return jnp.min(jnp.min(a, axis=0, keepdims=True), axis=1, keepdims=True)

    def sum01(a):
        return jnp.sum(jnp.sum(a, axis=0, keepdims=True), axis=1, keepdims=True)

    m1 = jnp.max(b3, axis=1, keepdims=True)
    first = jnp.min(jnp.where(b3 == m1, in_group, ge), axis=1, keepdims=True)
    m2 = jnp.max(jnp.where(in_group == first, -jnp.inf, b3), axis=1, keepdims=True)
    gscore = m1 + m2
    gsel = jnp.zeros((ng, 1, n), f32)
    for _ in range(TOPK_GROUPS):
        gm = jnp.max(gscore, axis=0, keepdims=True)
        gi = jnp.min(jnp.where(gscore == gm, group_id, ng), axis=0, keepdims=True)
        hit = group_id == gi
        gsel = jnp.where(hit, 1.0, gsel)
        gscore = jnp.where(hit, -jnp.inf, gscore)
    cand = jnp.where(jnp.broadcast_to(gsel, (ng, ge, n)) > 0.0, b3, -jnp.inf)
    top_e, top_w = [], []
    for _ in range(TOP_K):
        em = max01(cand)
        ei = min01(jnp.where(cand == em, expert_id, N_EXPERTS))
        hit = expert_id == ei
        top_e.append(ei)
        top_w.append(sum01(jnp.where(hit, s3, 0.0)))
        cand = jnp.where(hit, -jnp.inf, cand)
    wsum = functools.reduce(lambda a, b: a + b, top_w)
    sel3 = jnp.zeros((ng, ge, n), f32)
    for k, (ei, w) in enumerate(zip(top_e, top_w)):
        gate = (w / wsum * ROUTED_SCALE).reshape(1, n)
        for t in range(ROUTE_TILES):
            gate_ref[t, k:k + 1, :] = tile(gate, t)
        sel3 = jnp.where(expert_id == ei, 1.0, sel3)
    sel = sel3.reshape(N_EXPERTS, n)

    r_i = lax.broadcasted_iota(i32, (N_EXPERTS, N_EXPERTS), 0)
    c_i = lax.broadcasted_iota(i32, (N_EXPERTS, N_EXPERTS), 1)
    lower = jnp.where(c_i < r_i, 1.0, 0.0).astype(bf16)
    run_lens, run_offs = [], []
    for t in range(ROUTE_TILES):
        cnt = jnp.sum(tile(sel, t), axis=1, keepdims=True)
        run_len = jnp.ceil(cnt * (1.0 / CHUNK)) * CHUNK
        run_off = _dot(lower, jnp.broadcast_to(run_len, (N_EXPERTS, LANES)).astype(bf16))[:, 0:1]
        run_lens.append(run_len)
        run_offs.append(jnp.broadcast_to(run_off, (N_EXPERTS, TM)))
    t_r = lax.broadcasted_iota(i32, (n, n), 0)
    t_c = lax.broadcasted_iota(i32, (n, n), 1)
    before = jnp.where(jnp.logical_and(t_r < t_c, t_r // TM == t_c // TM), 1.0, 0.0).astype(bf16)
    rank = _dot(sel.astype(bf16), before)
    slot3 = (jnp.concatenate(run_offs, axis=1) + rank).reshape(ng, ge, n)
    slots = [sum01(jnp.where(expert_id == ei, slot3, 0.0)).reshape(1, n).astype(i32) for ei in top_e]
    out = []
    for t in range(ROUTE_TILES):
        for k in range(TOP_K):
            slots_ref[t, k:k + 1, :] = tile(slots[k], t)
        slots_ref[t, TOP_K:8, :] = jnp.full((8 - TOP_K, TM), -1, i32)
        gate_ref[t, TOP_K:8, :] = jnp.zeros((8 - TOP_K, TM), f32)
        len_ref[t] = jnp.broadcast_to(run_lens[t], (N_EXPERTS, LANES)).astype(i32)
        out.append(([tile(sl, t) for sl in slots], run_lens[t]))
    return out


def _route_dispatch(t, slots, h_hi, run_len, xloc_ref):
    rows = ROUTE_ROWS

    def body(c, carry):
        base = pl.multiple_of(c * rows, rows)
        row_id = base.astype(jnp.int16) + lax.broadcasted_iota(jnp.int16, (rows, TM), 0)
        p = jnp.zeros((rows, TM), bf16)
        for k in range(TOP_K):
            p = jnp.where(row_id == slots[k].astype(jnp.int16), jnp.ones((), bf16), p)
        xloc_ref[pl.ds(t * SLOTS + base, rows), :] = _dot(p, h_hi).astype(bf16)
        return carry

    def zero_body(c, carry):
        base = pl.multiple_of(c * rows, rows)
        xloc_ref[pl.ds(t * SLOTS + base, rows), :] = jnp.zeros((rows, D_MODEL), bf16)
        return carry

    n_used = (jnp.sum(run_len).astype(i32) + (rows - 1)) // rows
    lax.fori_loop(0, n_used, body, 0)
    lax.fori_loop(n_used, SLOTS // rows, zero_body, 0)


def _route(x_parts, o_prompt, o_sample, mod_l, g, w_out, rwt, rb):
    per_tile = lambda i: (i, 0, 0)
    rt = ROUTE_TILES
    tm = rt * TM
    return pl.pallas_call(
        functools.partial(_route_kernel, n_x=len(x_parts)),
        grid=(N_TILES // rt,),
        in_specs=_x_specs(x_parts, tm) + [
            pl.BlockSpec((tm, D_MODEL), lambda i: _prompt_block(i, tm)),
            pl.BlockSpec((tm, D_MODEL), lambda i: _sample_block(i, tm)),
            pl.BlockSpec((None, 6, D_MODEL), lambda i: (_mod_row(i, tm), 0, 0)),
            pl.BlockSpec((1, D_MODEL), lambda i: (0, 0)),
            pl.BlockSpec((D_MODEL, D_MODEL), lambda i: (0, 0)),
            pl.BlockSpec((N_EXPERTS, D_MODEL), lambda i: (0, 0)),
            pl.BlockSpec((N_EXPERTS, 1), lambda i: (0, 0)),
        ],
        out_specs=[
            pl.BlockSpec((tm, D_MODEL), lambda i: (i, 0)),
            pl.BlockSpec((rt * SLOTS, D_MODEL), lambda i: (i, 0)),
            pl.BlockSpec((rt, 8, TM), per_tile),
            pl.BlockSpec((rt, 8, TM), per_tile),
            pl.BlockSpec((rt, N_EXPERTS, LANES), per_tile),
        ],
        out_shape=[
            jax.ShapeDtypeStruct((N_TOK, D_MODEL), f32),
            jax.ShapeDtypeStruct((N_TILES * SLOTS, D_MODEL), bf16),
            jax.ShapeDtypeStruct((N_TILES, 8, TM), i32),
            jax.ShapeDtypeStruct((N_TILES, 8, TM), f32),
            jax.ShapeDtypeStruct((N_TILES, N_EXPERTS, LANES), i32),
        ],
        compiler_params=_cparams(("parallel",)),
        name="route",
    )(*x_parts, o_prompt, o_sample, mod_l, g, w_out, rwt, rb)


def _moe_plan(run_len):
    nt, ne = run_len.shape

    def excl_cumsum(a):
        n = a.shape[-1]
        earlier = np.arange(n)[None, :] < np.arange(n)[:, None]
        return jnp.sum(jnp.where(earlier, a[..., None, :], 0), axis=-1)

    def first_diff(a):
        return a - jnp.concatenate([jnp.zeros_like(a[..., :1]), a[..., :-1]], axis=-1)

    off_loc = excl_cumsum(run_len)
    before = excl_cumsum(run_len.T).T
    n_e = jnp.sum(run_len, axis=0)
    n_pad = -(-n_e // GM) * GM
    g_start = excl_cumsum(n_pad)
    total = jnp.sum(n_pad)
    run_dst = g_start[None, :] + before
    run_src = jnp.arange(nt, dtype=i32)[:, None] * SLOTS + off_loc
    dst_f = run_dst.T.reshape(-1)
    shift_f = first_diff((run_src - run_dst).T.reshape(-1))
    rows = jnp.arange((G_TILES + GMM_ITEM_TILES - 1) * G_CHUNKS, dtype=i32) * CHUNK
    shift = jnp.sum(jnp.where(dst_f[None, :] <= rows[:, None], shift_f[None, :], 0), axis=1)
    in_run = jnp.any((g_start[None, :] <= rows[:, None]) & (rows[:, None] < (g_start + n_e)[None, :]), axis=1)
    chunk_src = (jnp.where(in_run, rows + shift, 0) // CHUNK).astype(i32)
    loc_rows = jnp.arange(SLOT_CHUNKS, dtype=i32) * CHUNK
    shift_l = first_diff(run_dst - off_loc)
    shift = jnp.sum(jnp.where(off_loc[:, None, :] <= loc_rows[None, :, None], shift_l[:, None, :], 0), axis=2)
    used = jnp.sum(run_len, axis=1)
    chunk_map = jnp.where(loc_rows[None, :] < used[:, None], (loc_rows[None, :] + shift) // CHUNK, 0).astype(i32)
    tile_start, n_tiles = g_start // GM, n_pad // GM
    n_items = -(-n_tiles // GMM_ITEM_TILES)
    item_start = excl_cumsum(n_items)
    items = jnp.arange(GMM_MAX_ITEMS, dtype=i32)
    owner = items[:, None] >= item_start[None, :]
    e_first = jnp.sum(jnp.where(owner, first_diff(tile_start - GMM_ITEM_TILES * item_start)[None, :], 0), axis=1)
    item_tile = e_first + GMM_ITEM_TILES * items
    e_end = jnp.sum(jnp.where(owner, first_diff(tile_start + n_tiles)[None, :], 0), axis=1)
    item_cnt = jnp.clip(e_end - item_tile, 0, GMM_ITEM_TILES)
    gmm_plan = tuple(a.astype(i32) for a in (item_start, n_items, item_tile, item_cnt, chunk_src))
    return gmm_plan, chunk_map.reshape(-1)


def _gmm_in_copy(xloc_hbm, xbuf, sem, src_chunk, slot, c):
    return pltpu.make_async_copy(xloc_hbm.at[src_chunk], xbuf.at[slot, c], sem.at[slot])


def _gmm_out_copy(ybuf, y_hbm, sem, tile, slot, n_tiles):
    chunks = n_tiles * G_CHUNKS
    return pltpu.make_async_copy(ybuf.at[slot, pl.ds(0, chunks)],
                                 y_hbm.at[pl.ds(tile * G_CHUNKS, chunks)], sem.at[slot])


def _gmm_kernel(i0_ref, ni_ref, it_ref, ic_ref, cs_ref, xloc_hbm, wg_ref, wu_ref, wd_ref, y_hbm,
                xbuf, ybuf, zbuf, wg_b, wu_b, wd_b, in_sem, out_sem, zsem):
    e = pl.program_id(0)
    last = pl.num_programs(0) - 1
    n_items = ni_ref[e]
    first_item = i0_ref[e]
    total_items = i0_ref[last] + ni_ref[last]
    last_item = total_items - 1
    total_tiles = it_ref[last_item] + ic_ref[last_item]

    def start_in(item):
        first = it_ref[item] * G_CHUNKS
        for c in range(GMM_ITEM_TILES * G_CHUNKS):
            _gmm_in_copy(xloc_hbm, xbuf, in_sem, cs_ref[first + c], item % GMM_SLOTS, c).start()

    def wait_in(item):
        for c in range(GMM_ITEM_TILES * G_CHUNKS):
            _gmm_in_copy(xloc_hbm, xbuf, in_sem, 0, item % GMM_SLOTS, c).wait()

    def out_copy(item, fn):
        for cnt in range(1, GMM_ITEM_TILES + 1):
            @pl.when(ic_ref[item] == cnt)
            def _():
                fn(_gmm_out_copy(ybuf, y_hbm, out_sem, it_ref[item], item % GMM_SLOTS, cnt))

    @pl.when(e == 0)
    def _():
        for item in range(GMM_SLOTS - 1):
            start_in(item)
        zbuf[...] = jnp.zeros(zbuf.shape, zbuf.dtype)

    def tail_copies(fn):
        for j in range(_GMM_TAIL_PER_STEP):
            tile = total_tiles + e + j * N_EXPERTS

            @pl.when(tile < G_TILES)
            def _():
                fn(pltpu.make_async_copy(zbuf, y_hbm.at[pl.ds(tile * G_CHUNKS, G_CHUNKS)], zsem.at[0]))

    tail_copies(lambda cp: cp.start())

    @pl.when(n_items > 0)
    def _():
        wg_b[...] = wg_ref[...].astype(bf16)
        wu_b[...] = wu_ref[...].astype(bf16)
        wd_b[...] = wd_ref[...].astype(bf16)

    def body(j, carry):
        item = first_item + j
        slot = item % GMM_SLOTS

        @pl.when(item + (GMM_SLOTS - 1) < total_items)
        def _():
            start_in(item + (GMM_SLOTS - 1))

        wait_in(item)

        @pl.when(item >= GMM_SLOTS)
        def _():
            out_copy(item - GMM_SLOTS, lambda cp: cp.wait())

        for cnt in range(1, GMM_ITEM_TILES + 1):
            @pl.when(ic_ref[item] == cnt)
            def _():
                rows = cnt * GM
                chunks = cnt * G_CHUNKS
                x = xbuf[slot, 0:chunks].reshape(rows, D_MODEL)
                act = _silu(_dot(x, wg_b[...])) * _dot(x, wu_b[...])
                y = _dot(act.astype(bf16), wd_b[...]).astype(bf16)
                ybuf[slot, 0:chunks] = y.reshape(chunks, CHUNK, D_MODEL)

        out_copy(item, lambda cp: cp.start())
        return carry

    lax.fori_loop(0, n_items, body, 0)
    tail_copies(lambda cp: cp.wait())

    @pl.when(e == last)
    def _():
        for back in range(1, GMM_SLOTS + 1):
            out_copy(total_items - back, lambda cp: cp.wait())


_GMM_TAIL_PER_STEP = -(-(G_TILES - TM * TOP_K * N_TILES // GM) // N_EXPERTS)


def _gmm(plan, xloc, wg, wu, wd, layer):
    rows = GMM_ITEM_TILES * GM
    w_idx = lambda e, *_: (layer, e, 0, 0)
    grid_spec = pltpu.PrefetchScalarGridSpec(
        num_scalar_prefetch=5,
        grid=(N_EXPERTS,),
        in_specs=[
            pl.BlockSpec(memory_space=pl.ANY),
            pl.BlockSpec((None, None, D_MODEL, EXPERT_DIM), w_idx),
            pl.BlockSpec((None, None, D_MODEL, EXPERT_DIM), w_idx),
            pl.BlockSpec((None, None, EXPERT_DIM, D_MODEL), w_idx),
        ],
        out_specs=pl.BlockSpec(memory_space=pl.ANY),
        scratch_shapes=[pltpu.VMEM((GMM_SLOTS, rows // CHUNK, CHUNK, D_MODEL), bf16),
                        pltpu.VMEM((GMM_SLOTS, rows // CHUNK, CHUNK, D_MODEL), bf16),
                        pltpu.VMEM((G_CHUNKS, CHUNK, D_MODEL), bf16),
                        pltpu.VMEM((D_MODEL, EXPERT_DIM), bf16), pltpu.VMEM((D_MODEL, EXPERT_DIM), bf16),
                        pltpu.VMEM((EXPERT_DIM, D_MODEL), bf16),
                        pltpu.SemaphoreType.DMA((GMM_SLOTS,)), pltpu.SemaphoreType.DMA((GMM_SLOTS,)),
                        pltpu.SemaphoreType.DMA((1,))],
    )
    return pl.pallas_call(
        _gmm_kernel,
        grid_spec=grid_spec,
        out_shape=jax.ShapeDtypeStruct((G_TILES * G_CHUNKS, CHUNK, D_MODEL), bf16),
        compiler_params=_cparams(("arbitrary",)),
        name="gmm",
    )(*plan, xloc, wg, wu, wd)


def _combine_copy(y_hbm, ybuf, sem, sorted_chunk, slot, c):
    return pltpu.make_async_copy(y_hbm.at[sorted_chunk], ybuf.at[slot, c], sem.at[slot])


def _combine_kernel(cm_ref, y_hbm, slots_ref, gate_ref, x_ref, mod_ref, g_ref, sg_ref, su_ref, sd_ref, *rest, final):
    if final:
        gf_ref, yp_ref, ys_ref, ybuf, sem = rest
    else:
        o_ref, ybuf, sem = rest
    i = pl.program_id(0)
    n = pl.num_programs(0)
    slot = i % 2

    def start(tile, s):
        for c in range(SLOT_CHUNKS):
            _combine_copy(y_hbm, ybuf, sem, cm_ref[tile * SLOT_CHUNKS + c], s, c).start()

    def wait(s):
        for c in range(SLOT_CHUNKS):
            _combine_copy(y_hbm, ybuf, sem, 0, s, c).wait()

    @pl.when(i == 0)
    def _():
        start(0, 0)

    wait(slot)
    start((i + 1) % n, 1 - slot)

    x = x_ref[...]
    hb = _norm_mod(x, g_ref[...], mod_ref[4:5, :], mod_ref[3:4, :]).astype(bf16)
    shared = _dot((_silu(_dot(hb, sg_ref[...])) * _dot(hb, su_ref[...])).astype(bf16), sd_ref[...])
    row_id = lax.broadcasted_iota(jnp.int16, (SLOTS, TM), 0)
    p = jnp.zeros((SLOTS, TM), bf16)
    for k in range(TOP_K):
        p = jnp.where(row_id == slots_ref[k:k + 1, :].astype(jnp.int16), gate_ref[k:k + 1, :].astype(bf16), p)
    routed = lax.dot_general(p, ybuf[slot].reshape(SLOTS, D_MODEL), (((0,), (0,)), ((), ())),
                             preferred_element_type=f32)
    out = x + mod_ref[5:6, :] * (routed + shared)
    if final:
        y = (out * lax.rsqrt(jnp.mean(out * out, axis=-1, keepdims=True) + EPS)) * gf_ref[...]

        @pl.when(i < N_PROMPT_TILES)
        def _():
            yp_ref[...] = y

        @pl.when(i >= N_PROMPT_TILES)
        def _():
            ys_ref[...] = y
    else:
        o_ref[...] = out

    @pl.when(i == n - 1)
    def _():
        wait(1 - slot)


def _combine(chunk_map, y, slots, gates, x, mod_l, g, sg, su, sd, final_g=None):
    shd = sg.shape[1]
    final = final_g is not None
    row_spec = pl.BlockSpec((TM, D_MODEL), lambda i, cm: (i, 0))
    vec_spec = pl.BlockSpec((1, D_MODEL), lambda i, cm: (0, 0))
    if final:
        out_specs = [pl.BlockSpec((TM, D_MODEL), lambda i, cm: _prompt_block(i)),
                     pl.BlockSpec((TM, D_MODEL), lambda i, cm: _sample_block(i))]
        out_shape = [jax.ShapeDtypeStruct((N_PROMPT, D_MODEL), f32), jax.ShapeDtypeStruct((N_SAMPLE, D_MODEL), f32)]
    else:
        out_specs, out_shape = row_spec, jax.ShapeDtypeStruct((N_TOK, D_MODEL), f32)
    grid_spec = pltpu.PrefetchScalarGridSpec(
        num_scalar_prefetch=1,
        grid=(N_TILES,),
        in_specs=[
            pl.BlockSpec(memory_space=pl.ANY),
            pl.BlockSpec((None, 8, TM), lambda i, cm: (i, 0, 0)),
            pl.BlockSpec((None, 8, TM), lambda i, cm: (i, 0, 0)),
            row_spec,
            pl.BlockSpec((None, 6, D_MODEL), lambda i, cm: (_mod_row(i), 0, 0)),
            vec_spec,
            pl.BlockSpec((D_MODEL, shd), lambda i, cm: (0, 0)),
            pl.BlockSpec((D_MODEL, shd), lambda i, cm: (0, 0)),
            pl.BlockSpec((shd, D_MODEL), lambda i, cm: (0, 0)),
        ] + ([vec_spec] if final else []),
        out_specs=out_specs,
        scratch_shapes=[pltpu.VMEM((2, SLOT_CHUNKS, CHUNK, D_MODEL), bf16), pltpu.SemaphoreType.DMA((2,))],
    )
    args = (chunk_map, y, slots, gates, x, mod_l, g, sg, su, sd) + ((final_g,) if final else ())
    return pl.pallas_call(
        functools.partial(_combine_kernel, final=final),
        grid_spec=grid_spec,
        out_shape=out_shape,
        compiler_params=_cparams(("arbitrary",)),
        name="combine",
    )(*args)


def _moe(x_parts, o_prompt, o_sample, w_out, mod_l, g, rwt, rb, wg, wu, wd, layer, sg, su, sd, final_g=None):
    x, xloc, slots, gates, run_len = _route(x_parts, o_prompt, o_sample, mod_l, g, w_out, rwt, rb)
    gmm_plan, chunk_map = _moe_plan(run_len[:, :, 0])
    y = _gmm(gmm_plan, xloc.reshape(N_TILES * SLOT_CHUNKS, CHUNK, D_MODEL), wg, wu, wd, layer)
    return _combine(chunk_map, y, slots, gates, x, mod_l, g, sg, su, sd, final_g)


_L0_CHUNKS = (
    (0, 512, 0, (0, 1, 2, 3), ()),
    (512, 1024, 768, (), ()),
    (1024, 1536, 1280, (), ((2, 0, 512, 0, False),)),
    (1536, 2048, 1792, (), ((3, 0, 512, 0, False),)),
    (2048, 2304, 512, (0,), ((0, 0, 128, 0, False), (1, 128, 256, 0, False))),
)
_L0_KV_OUTS = (("T", 128), ("T", 128), ("T", 512), ("T", 512))
_L1_CHUNKS = (
    (0, 512, 0, (0, 1, 2, 3), ()),
    (512, 1024, 512, (0, 1, 2, 3), ()),
    (1024, 1536, 1024, (0, 1, 2, 3), ((0, 0, 512, 0, False),)),
    (1536, 2048, 1536, (0, 1, 2, 3), ((0, 0, 512, 512, False),)),
    (2048, 2560, 2048, (), ((1, 0, 512, 0, True),)),
    (2560, 3072, 2560, (), ((1, 0, 512, 512, True),)),
)
_L1_KV_OUTS = (("T", 1024), ("H", 8))


def _from_feature_major(kt, *head_dims):
    nb, _, s = kt.shape
    nd = len(head_dims)
    return kt.reshape(nb, *head_dims, s).transpose(0, nd + 1, *range(1, nd + 1))[:, None]


def kernel(x_prompt, x_sample, cache_a_k, cache_a_v, cache_b_k, cache_b_v, cache_c_k, cache_c_v, c, c_ctx, w_mod, b_mod, norm_mix, norm_ffn, w_in_ab, w_out_ab, sink_a, rel_bias_b, w_in_c, w_out_c, lam_q1, lam_k1, lam_q2, lam_k2, subln_c, router_w, router_bias, exp_w_gate, exp_w_up, exp_w_down, sh_w_gate, sh_w_up, sh_w_down, final_norm):
    x = (x_prompt.reshape(N_PROMPT, D_MODEL), x_sample.reshape(N_SAMPLE, D_MODEL))
    cond8 = jnp.concatenate([c_ctx[None, :], c, jnp.zeros((8 - 1 - N_SAMPLE_BATCH, D_MODEL), f32)], axis=0)
    mod = _adaln(cond8, w_mod, b_mod).reshape(DEPTH, 8, 6, D_MODEL)
    rope_tabs = _rope_tables()
    new_kv = {}
    for layer in range(DEPTH):
        li = layer // 2
        mod_l = mod[layer]
        g_mix = norm_mix[layer][None, :]
        g_ffn = norm_ffn[layer][None, :]
        if layer % 2 == 0:
            w_in = w_in_ab[li].astype(bf16)
            qkv, ak, av, bk, bv = _inproj(x, mod_l, g_mix, w_in, rope_tabs, _L0_CHUNKS, _L0_KV_OUTS)
            new_kv["a_k"], new_kv["a_v"], new_kv["b_k"], new_kv["b_v"] = ak, av, bk, bv
            o_p = _ctx0(sink_a[li], qkv)
            o_s = _lat0(sink_a[li], qkv,
                        cache_a_k[:, li].reshape(N_SAMPLE_BATCH, PAST_LEN, LANES),
                        cache_a_v[:, li].reshape(N_SAMPLE_BATCH, PAST_LEN, LANES),
                        cache_b_k[:, li].reshape(N_SAMPLE_BATCH, PAST_LEN, 512),
                        cache_b_v[:, li].reshape(N_SAMPLE_BATCH, PAST_LEN, 512),
                        _na_bias_tiles(rel_bias_b[li]))
            w_out = w_out_ab[li].astype(bf16)
        else:
            lam_init = 0.8 - 0.6 * math.exp(-0.3 * layer)
            qkv, ck, cv = _inproj(x, mod_l, g_mix, w_in_c[li].astype(bf16), rope_tabs, _L1_CHUNKS, _L1_KV_OUTS)
            new_kv["c_k"], new_kv["c_v"] = ck, cv
            lamv = jnp.concatenate([lam_q1[li][None], lam_k1[li][None], lam_q2[li][None], lam_k2[li][None],
                                    jnp.zeros((4, HEAD_DIM), f32)], axis=0)
            subln = subln_c[li][None, :]
            o_p = _ctx1(lamv, subln, qkv, lam_init)
            o_s = _lat1(lamv, subln, qkv,
                        cache_c_k[:, li].reshape(N_SAMPLE_BATCH, PAST_LEN, D_MODEL),
                        cache_c_v[:, li].reshape(N_SAMPLE_BATCH, PAST_LEN, D_MODEL), lam_init)
            w_out = w_out_c[li].astype(bf16)
        last = layer == DEPTH - 1
        x = _moe(x, o_p, o_s, w_out, mod_l, g_ffn, router_w[layer].T, router_bias[layer][:, None],
                 exp_w_gate, exp_w_up, exp_w_down, layer,
                 sh_w_gate[layer].astype(bf16), sh_w_up[layer].astype(bf16), sh_w_down[layer].astype(bf16),
                 final_norm[None, :] if last else None)
        x = x if last else (x,)
    y_prompt, y_sample = x
    nb, s = N_PROMPT_BATCH, PROMPT_SEQ
    return (y_prompt.reshape(nb, s, D_MODEL), y_sample.reshape(N_SAMPLE_BATCH, SAMPLE_SEQ, D_MODEL),
            _from_feature_major(new_kv["a_k"], 2, HEAD_DIM), _from_feature_major(new_kv["a_v"], 2, HEAD_DIM),
            _from_feature_major(new_kv["b_k"], 8, HEAD_DIM), _from_feature_major(new_kv["b_v"], 8, HEAD_DIM),
            _from_feature_major(new_kv["c_k"], 8, 2, HEAD_DIM), new_kv["c_v"].reshape(nb, 1, s, 8, 2 * HEAD_DIM))
```

```python
import functools
import math

import jax
import jax.numpy as jnp
import numpy as np
from jax import lax
from jax.experimental import pallas as pl
from jax.experimental.pallas import tpu as pltpu

f32 = jnp.float32
bf16 = jnp.bfloat16
i32 = jnp.int32

D_MODEL = 1024
N_PROMPT_BATCH = 16
PROMPT_SEQ = 256
DEPTH = 2
N_SAMPLE_BATCH = 2
SAMPLE_SEQ = 2048
PAST_LEN = 512
GRID_W = 64
HEAD_DIM = 64
ROPE_THETA = 10000.0
EPS = 1e-6
A_WINDOW = 128
NA_ROWS = 8
NA_COLS = 16
N_EXPERTS = 64
TOP_K = 6
N_GROUPS = 8
TOPK_GROUPS = 4
EXPERT_DIM = 256
ROUTED_SCALE = 2.5
Q_SCALE = HEAD_DIM ** -0.5

N_PROMPT = N_PROMPT_BATCH * PROMPT_SEQ
N_SAMPLE = N_SAMPLE_BATCH * SAMPLE_SEQ
N_TOK = N_PROMPT + N_SAMPLE

LANES = 128
TM = 256
TD = 512
N_PROMPT_TILES = N_PROMPT // TM
N_TILES = N_TOK // TM
QB = 128
CHUNK = 16
SLOTS = -(-(TM * TOP_K + N_EXPERTS * (CHUNK - 1)) // 256) * 256
SLOT_CHUNKS = SLOTS // CHUNK
ROUTE_ROWS = 512
ROUTE_TILES = 2
COMBINE_SLOTS = 3
GM = 256
_MAX_SORTED = TM * TOP_K * N_TILES + N_TILES * N_EXPERTS * (CHUNK - 1) + N_EXPERTS * (GM - CHUNK)
G_TILES = -(-_MAX_SORTED // GM)
G_CHUNKS = GM // CHUNK
GMM_ITEM_TILES = 2
GMM_SLOTS = 4
GMM_MAX_ITEMS = (G_TILES + N_EXPERTS * (GMM_ITEM_TILES - 1)) // GMM_ITEM_TILES
VMEM_LIMIT = 56 * 1024 * 1024

NEG = -1e30


def _cparams(sem):
    return pltpu.CompilerParams(dimension_semantics=sem, vmem_limit_bytes=VMEM_LIMIT)


def _mod_row(i, tm=TM):
    return jnp.where(i < N_PROMPT // tm, 0, 1 + (i - N_PROMPT // tm) // (SAMPLE_SEQ // tm))


def _prompt_block(i, tm=TM):
    return (jnp.minimum(i, N_PROMPT // tm - 1), 0)


def _sample_block(i, tm=TM):
    return (jnp.maximum(i - N_PROMPT // tm, 0), 0)


def _x_specs(parts, tm=TM):
    if len(parts) == 1:
        return [pl.BlockSpec((tm, D_MODEL), lambda i, *_: (i, 0))]
    return [pl.BlockSpec((tm, D_MODEL), lambda i, *_: _prompt_block(i, tm)),
            pl.BlockSpec((tm, D_MODEL), lambda i, *_: _sample_block(i, tm))]


def _load_x(i, x_refs, tm=TM):
    if len(x_refs) == 1:
        return x_refs[0][...]
    return jnp.where(i < N_PROMPT // tm, x_refs[0][...], x_refs[1][...])


def _norm_mod(x, g, scale, shift):
    y = x * lax.rsqrt(jnp.mean(x * x, axis=-1, keepdims=True) + EPS)
    return (y * g) * (1.0 + scale) + shift


def _silu(x):
    return x * jax.nn.sigmoid(x)


def _dot(a, b):
    return jnp.dot(a, b, preferred_element_type=f32)


def _dot_nt(a, b):
    return lax.dot_general(a, b, (((1,), (1,)), ((), ())), preferred_element_type=f32)


ADA_COLS = 1536


def _adaln_kernel(cond_ref, w_ref, b_ref, o_ref):
    s = _silu(cond_ref[...]).astype(bf16)
    o_ref[...] = _dot(s, w_ref[...].astype(bf16)) + b_ref[...]


def _adaln(cond8, w_mod, b_mod):
    n6 = 6 * D_MODEL
    return pl.pallas_call(
        _adaln_kernel,
        grid=(DEPTH, n6 // ADA_COLS),
        in_specs=[
            pl.BlockSpec((8, D_MODEL), lambda l, j: (0, 0)),
            pl.BlockSpec((None, D_MODEL, ADA_COLS), lambda l, j: (l, 0, j)),
            pl.BlockSpec((None, 1, ADA_COLS), lambda l, j: (l, 0, j)),
        ],
        out_specs=pl.BlockSpec((None, 8, ADA_COLS), lambda l, j: (l, 0, j)),
        out_shape=jax.ShapeDtypeStruct((DEPTH, 8, n6), f32),
        compiler_params=_cparams(("parallel", "parallel")),
        name="adaln",
    )(cond8, w_mod, b_mod.reshape(DEPTH, 1, n6))


def _rope_block(blk, cos, sin_a, sin_b):
    return blk * cos + pltpu.roll(blk, LANES - 16, 1) * sin_a + pltpu.roll(blk, 16, 1) * sin_b


def _inproj_kernel(*refs, chunks, n_x):
    x_refs, kv_refs = refs[:n_x], refs[n_x + 7:]
    mod_ref, g_ref, w_ref, cos_ref, sa_ref, sb_ref, qkv_ref = refs[n_x:n_x + 7]
    i = pl.program_id(0)
    h = _norm_mod(_load_x(i, x_refs, TD), g_ref[...], mod_ref[1:2, :], mod_ref[0:1, :]).astype(bf16)
    is_prompt = i < N_PROMPT // TD

    @pl.when(is_prompt)
    def _():
        for c0, c1, s0, _, kv_out in chunks:
            acc = _dot(h, w_ref[:, s0:s0 + (c1 - c0)])
            qkv_ref[:, c0:c1] = acc.astype(bf16)
            for ridx, a0, a1, o0, per_head in kv_out:
                if per_head:
                    heads = kv_refs[ridx].shape[0] // TD
                    for j in range((a1 - a0) // LANES):
                        kv_refs[ridx][pl.ds(o0 // LANES + j, TD, stride=heads), :] = (
                            acc[:, a0 + j * LANES:a0 + (j + 1) * LANES])
                else:
                    t = acc[:, a0:a1].T
                    for b in range(TD // PROMPT_SEQ):
                        kv_refs[ridx][b, o0:o0 + (a1 - a0), :] = t[:, b * PROMPT_SEQ:(b + 1) * PROMPT_SEQ]

    @pl.when(jnp.logical_not(is_prompt))
    def _():
        cos, sa, sb = cos_ref[...], sa_ref[...], sb_ref[...]
        for c0, c1, s0, rope_blocks, _ in chunks:
            acc = _dot(h, w_ref[:, s0:s0 + (c1 - c0)])
            for b in range((c1 - c0) // LANES):
                blk = acc[:, b * LANES:(b + 1) * LANES]
                if b in rope_blocks:
                    blk = _rope_block(blk, cos, sa, sb)
                qkv_ref[:, c0 + b * LANES:c0 + (b + 1) * LANES] = blk.astype(bf16)


def _inproj(x_parts, mod_l, g, w, rope_tabs, chunks, kv_outs):
    n = w.shape[1]
    cos, sa, sb = rope_tabs
    bpt = TD // PROMPT_SEQ

    def rope_idx(i):
        return (jnp.where(i < N_PROMPT // TD, 0, (i - N_PROMPT // TD) % (SAMPLE_SEQ // TD)), 0)

    kv_specs, kv_shapes = [], []
    for kind, size in kv_outs:
        if kind == "T":
            kv_specs.append(pl.BlockSpec((bpt, size, PROMPT_SEQ), lambda i: _prompt_block(i, TD) + (0,)))
            kv_shapes.append(jax.ShapeDtypeStruct((N_PROMPT_BATCH, size, PROMPT_SEQ), f32))
        else:
            kv_specs.append(pl.BlockSpec((TD * size, LANES), lambda i: _prompt_block(i, TD)))
            kv_shapes.append(jax.ShapeDtypeStruct((N_PROMPT * size, LANES), f32))

    return pl.pallas_call(
        functools.partial(_inproj_kernel, chunks=chunks, n_x=len(x_parts)),
        grid=(N_TOK // TD,),
        in_specs=_x_specs(x_parts, TD) + [
            pl.BlockSpec((None, 6, D_MODEL), lambda i: (_mod_row(i, TD), 0, 0)),
            pl.BlockSpec((1, D_MODEL), lambda i: (0, 0)),
            pl.BlockSpec((D_MODEL, n), lambda i: (0, 0)),
            pl.BlockSpec((TD, LANES), rope_idx),
            pl.BlockSpec((TD, LANES), rope_idx),
            pl.BlockSpec((TD, LANES), rope_idx),
        ],
        out_specs=[pl.BlockSpec((TD, n), lambda i: (i, 0))] + kv_specs,
        out_shape=[jax.ShapeDtypeStruct((N_TOK, n), bf16)] + kv_shapes,
        compiler_params=_cparams(("arbitrary",)),
        name="inproj",
    )(*x_parts, mod_l, g, w, cos, sa, sb)


def _rope_tables():
    nq = HEAD_DIM // 4
    t = np.arange(SAMPLE_SEQ)
    inv = np.power(np.float32(ROPE_THETA), -np.arange(nq, dtype=np.float32) / np.float32(nq))
    ang_r = (t // GRID_W).astype(np.float32)[:, None] * inv
    ang_c = (t % GRID_W).astype(np.float32)[:, None] * inv
    zero = np.zeros_like(ang_r)

    def head(fr, fc):
        return np.concatenate([fr[0], fr[1], fc[0], fc[1]], axis=-1)

    cos = head((np.cos(ang_r), np.cos(ang_r)), (np.cos(ang_c), np.cos(ang_c)))
    sin_a = head((-np.sin(ang_r), zero), (-np.sin(ang_c), zero))
    sin_b = head((zero, np.sin(ang_r)), (zero, np.sin(ang_c)))
    two = lambda a: jnp.asarray(np.concatenate([a, a], axis=-1).astype(np.float32))
    return two(cos), two(sin_a), two(sin_b)


def _lane_lo(shape):
    return lax.broadcasted_iota(i32, shape, len(shape) - 1) < HEAD_DIM


def _half(q, lo_mask, half):
    keep = lo_mask if half == 0 else jnp.logical_not(lo_mask)
    return jnp.where(keep, q, jnp.zeros_like(q)) * Q_SCALE


def _swap_halves(x):
    return pltpu.roll(x.astype(f32), HEAD_DIM, 1).astype(x.dtype)


def _stack_halves(q, lo_mask):
    return jnp.concatenate([_half(q, lo_mask, 0), _half(q, lo_mask, 1)], axis=0)


def _with_ones(v):
    return jnp.concatenate([v, jnp.ones_like(v)], axis=1)


def _attend(q_rows, n_heads, key_blocks, vx_blocks, fix_scores=None, sinks=None):
    return _attend_many([(q_rows, n_heads, key_blocks, vx_blocks, fix_scores, sinks)])[0]


def _attend_many(problems):
    scores = [[_dot_nt(q_rows, k) for k in key_blocks] for q_rows, _, key_blocks, _, _, _ in problems]
    exps, maxes = [], []
    for (q_rows, n_heads, key_blocks, _, fix_scores, sinks), sc in zip(problems, scores):
        r = q_rows.shape[0] // n_heads
        e_p, m_p = [[] for _ in key_blocks], []
        for h in range(n_heads):
            blocks = [s[h * r:(h + 1) * r] for s in sc]
            if fix_scores is not None:
                blocks = [fix_scores(h, i, s) for i, s in enumerate(blocks)]
            m = functools.reduce(jnp.maximum, [jnp.max(s, axis=-1, keepdims=True) for s in blocks])
            if sinks is not None:
                m = jnp.maximum(m, sinks[h])
            m_p.append(m)
            for i, s in enumerate(blocks):
                e_p[i].append(jnp.exp((s - m).astype(bf16)))
        exps.append(e_p)
        maxes.append(m_p)
    outs = [functools.reduce(lambda a, b: a + b,
                             [_dot(e[0] if n_heads == 1 else jnp.concatenate(e, axis=0), vx)
                              for e, vx in zip(e_p, vx_blocks)])
            for (_, n_heads, _, vx_blocks, _, _), e_p in zip(problems, exps)]
    results = []
    for (q_rows, n_heads, _, _, _, sinks), out, m_p in zip(problems, outs, maxes):
        r = q_rows.shape[0] // n_heads
        res = []
        for h in range(n_heads):
            den = out[h * r:(h + 1) * r, LANES:]
            if sinks is not None:
                den = den + jnp.exp(sinks[h] - m_p[h])
            res.append(out[h * r:(h + 1) * r, :LANES] * (1.0 / den))
        results.append(res)
    return results


def _gqa_rows(q_blocks, group, lo_mask):
    parts = []
    for q in q_blocks:
        for half in range(2):
            qh = _half(q, lo_mask, half)
            parts.append(qh if half == group else _swap_halves(qh))
    return jnp.concatenate(parts, axis=0)


def _gqa_merge(outs, group, lo_mask):
    fixed = [o if idx % 2 == group else pltpu.roll(o, HEAD_DIM, 1) for idx, o in enumerate(outs)]
    return [jnp.where(lo_mask, fixed[2 * p], fixed[2 * p + 1]) for p in range(len(outs) // 2)]


L0_QA, L0_QB, L0_KB, L0_VB, L0_KA, L0_VA, L0_N = 0, 512, 1024, 1536, 2048, 2176, 2304


def _ctx0_kernel(sink_ref, qkv_ref, o_ref):
    lo = _lane_lo((1, LANES))
    blk = lambda base, j: qkv_ref[:, base + j * LANES:base + (j + 1) * LANES]
    k_a = blk(L0_KA, 0)
    vx_a = _with_ones(blk(L0_VA, 0))
    for g in range(2):
        q_rows = _gqa_rows([blk(L0_QA, 2 * g), blk(L0_QA, 2 * g + 1)], g, lo)
        outs = _attend(q_rows, 4, [k_a], [vx_a], sinks=[sink_ref[4 * g + idx] for idx in range(4)])
        for p, o in enumerate(_gqa_merge(outs, g, lo)):
            j = 2 * g + p
            o_ref[:, j * LANES:(j + 1) * LANES] = o.astype(bf16)
    for j in range(4):
        outs = _attend(_stack_halves(blk(L0_QB, j), lo), 2, [blk(L0_KB, j)], [_with_ones(blk(L0_VB, j))])
        o_ref[:, 512 + j * LANES:512 + (j + 1) * LANES] = jnp.where(lo, outs[0], outs[1]).astype(bf16)


def _ctx0(sink, qkv):
    return pl.pallas_call(
        _ctx0_kernel,
        grid=(N_PROMPT_BATCH,),
        in_specs=[
            pl.BlockSpec(memory_space=pltpu.SMEM),
            pl.BlockSpec((PROMPT_SEQ, L0_N), lambda b: (b, 0)),
        ],
        out_specs=pl.BlockSpec((PROMPT_SEQ, D_MODEL), lambda b: (b, 0)),
        out_shape=jax.ShapeDtypeStruct((N_PROMPT, D_MODEL), bf16),
        compiler_params=_cparams(("parallel",)),
        name="ctx0",
    )(sink, qkv)


WIN_KEYS = 3 * QB
NA_KEY_ROWS = 10
NA_KEYS = NA_KEY_ROWS * GRID_W
N_QB = SAMPLE_SEQ // QB
N_NA_PATTERNS = 5
_PROMPT_QBLOCKS = N_PROMPT // QB


def _na_pattern(n):
    return jnp.where(n < 2, n, jnp.where(n > N_QB - 3, n - (N_QB - 5), 2))


def _lat0_kernel(sink_ref, roff_ref, q_ref, kvb_ref, kva_ref, cak_ref, cav_ref, cbk_ref, cbv_ref, tiles_ref, o_ref):
    n = pl.program_id(1)
    lo = _lane_lo((1, LANES))
    kstart = pl.multiple_of(jnp.clip((n - 1) * QB, 0, SAMPLE_SEQ - WIN_KEYS), QB)
    k_a = kva_ref[pl.ds(kstart, WIN_KEYS), 0:LANES]
    v_a = kva_ref[pl.ds(kstart, WIN_KEYS), LANES:2 * LANES]
    c_k = cak_ref[...].astype(bf16)
    keys_a = [c_k, k_a]
    vx_a = [_with_ones(cav_ref[...].astype(bf16)), _with_ones(v_a)]
    qpos = n * QB + lax.broadcasted_iota(i32, (QB, WIN_KEYS), 0)
    kpos = kstart + lax.broadcasted_iota(i32, (QB, WIN_KEYS), 1)
    in_window = jnp.abs(qpos - kpos) <= A_WINDOW
    mask_window = lambda h, i, s: jnp.where(in_window, s, NEG) if i == 1 else s
    problems = []
    for g in range(2):
        q_rows = _gqa_rows([q_ref[:, L0_QA + j * LANES:L0_QA + (j + 1) * LANES] for j in (2 * g, 2 * g + 1)], g, lo)
        problems.append((q_rows, 4, keys_a, vx_a, mask_window, [sink_ref[4 * g + idx] for idx in range(4)]))
    krow = jnp.clip(2 * n - NA_ROWS // 2, 0, SAMPLE_SEQ // GRID_W - NA_KEY_ROWS)
    ktok = pl.multiple_of(krow * GRID_W, QB)
    pattern = _na_pattern(n)

    def na_bias(head):
        rows = []
        for rq in range(QB // GRID_W):
            blocks = []
            for kb in range(NA_KEY_ROWS // 2):
                d0, d1 = (roff_ref[(pattern * 2 + rq) * NA_KEY_ROWS + 2 * kb + t] for t in range(2))
                blocks.append(jnp.where(lo, tiles_ref[head, d0], tiles_ref[head, d1]))
            rows.append(jnp.concatenate(blocks, axis=1))
        return jnp.concatenate(rows, axis=0)

    for j in range(4):
        q_b = q_ref[:, L0_QB + j * LANES:L0_QB + (j + 1) * LANES]
        k_b = kvb_ref[pl.ds(ktok, NA_KEYS), j * LANES:(j + 1) * LANES]
        v_b = kvb_ref[pl.ds(ktok, NA_KEYS), 512 + j * LANES:512 + (j + 1) * LANES]
        cb_k = cbk_ref[:, j * LANES:(j + 1) * LANES].astype(bf16)
        cb_v = cbv_ref[:, j * LANES:(j + 1) * LANES].astype(bf16)
        add_bias = lambda h, i, s, j=j: s + na_bias(2 * j + h) if i == 1 else s
        problems.append((_stack_halves(q_b, lo), 2, [cb_k, k_b], [_with_ones(cb_v), _with_ones(v_b)], add_bias, None))

    results = _attend_many(problems)
    for g in range(2):
        for p, o in enumerate(_gqa_merge(results[g], g, lo)):
            j = 2 * g + p
            o_ref[:, j * LANES:(j + 1) * LANES] = o.astype(bf16)
    for j in range(4):
        outs = results[2 + j]
        o_ref[:, 512 + j * LANES:512 + (j + 1) * LANES] = jnp.where(lo, outs[0], outs[1]).astype(bf16)


def _lat0(sink, qkv, cak, cav, cbk, cbv, bias_tiles):
    sb = N_PROMPT // SAMPLE_SEQ
    return pl.pallas_call(
        _lat0_kernel,
        grid=(N_SAMPLE_BATCH, N_QB),
        in_specs=[
            pl.BlockSpec(memory_space=pltpu.SMEM),
            pl.BlockSpec(memory_space=pltpu.SMEM),
            pl.BlockSpec((QB, 1024), lambda b, n: (_PROMPT_QBLOCKS + b * N_QB + n, 0)),
            pl.BlockSpec((SAMPLE_SEQ, 1024), lambda b, n: (sb + b, 1)),
            pl.BlockSpec((SAMPLE_SEQ, 256), lambda b, n: (sb + b, L0_KA // 256)),
            pl.BlockSpec((None, PAST_LEN, LANES), lambda b, n: (b, 0, 0)),
            pl.BlockSpec((None, PAST_LEN, LANES), lambda b, n: (b, 0, 0)),
            pl.BlockSpec((None, PAST_LEN, 512), lambda b, n: (b, 0, 0)),
            pl.BlockSpec((None, PAST_LEN, 512), lambda b, n: (b, 0, 0)),
            pl.BlockSpec((8, N_ROW_OFFSETS + 1, GRID_W, LANES), lambda b, n: (0, 0, 0, 0)),
        ],
        out_specs=pl.BlockSpec((QB, D_MODEL), lambda b, n: (b * N_QB + n, 0)),
        out_shape=jax.ShapeDtypeStruct((N_SAMPLE, D_MODEL), bf16),
        compiler_params=_cparams(("parallel", "arbitrary")),
        name="lat0",
    )(sink, jnp.asarray(_na_row_offsets()), qkv, qkv, qkv, cak, cav, cbk, cbv, bias_tiles)


N_ROW_OFFSETS = 2 * NA_ROWS - 1


def _na_row_offsets():
    rows = SAMPLE_SEQ // GRID_W
    idx = np.full((N_NA_PATTERNS, 2, NA_KEY_ROWS), N_ROW_OFFSETS, np.int32)
    for p, n in enumerate((0, 1, 2, N_QB - 2, N_QB - 1)):
        k0 = int(np.clip(2 * n - NA_ROWS // 2, 0, rows - NA_KEY_ROWS))
        for rq in range(2):
            r = 2 * n + rq
            rs = int(np.clip(r - NA_ROWS // 2, 0, rows - NA_ROWS))
            for kl in range(NA_KEY_ROWS):
                if rs <= k0 + kl < rs + NA_ROWS:
                    idx[p, rq, kl] = k0 + kl - r + NA_ROWS - 1
    return idx.reshape(-1)


def _na_bias_tiles(rel_bias):
    n_dc = 2 * NA_COLS - 1
    c = np.arange(GRID_W)[:, None]
    kc = np.arange(GRID_W)[None, :]
    cs = np.clip(c - NA_COLS // 2, 0, GRID_W - NA_COLS)
    col_ok = (kc >= cs) & (kc < cs + NA_COLS)
    col_hot = ((kc - c + NA_COLS - 1)[None] == np.arange(n_dc)[:, None, None]) & col_ok[None]
    hp = lax.Precision.HIGHEST
    tiles = jnp.einsum("hdx,xck->hdck", rel_bias.astype(f32), col_hot.astype(np.float32), precision=hp)
    tiles = tiles + np.where(col_ok, 0.0, NEG).astype(np.float32)
    tiles = jnp.concatenate([tiles, jnp.full((tiles.shape[0], 1, GRID_W, GRID_W), NEG, f32)], axis=1)
    return jnp.concatenate([tiles, tiles], axis=-1)


def _diff_lambda(lam_ref, lam_init):
    lv = lam_ref[...]
    s1 = jnp.sum(lv[0:1, :] * lv[1:2, :], axis=-1, keepdims=True)
    s2 = jnp.sum(lv[2:3, :] * lv[3:4, :], axis=-1, keepdims=True)
    return jnp.exp(s1) - jnp.exp(s2) + lam_init


DIFF_HEADS_PER_PASS = 2


def _diff_heads(heads, o_ref, lam, subln, lo, lam_init):
    for p0 in range(0, len(heads), DIFF_HEADS_PER_PASS):
        group = heads[p0:p0 + DIFF_HEADS_PER_PASS]
        results = _attend_many([(_stack_halves(q, lo), 2, ks, [_with_ones(v) for v in vs], None, None)
                                for _, q, ks, vs in group])
        for (h, _, _, _), (o1, o2) in zip(group, results):
            o = o1 - lam * o2
            o = o * lax.rsqrt(jnp.mean(o * o, axis=-1, keepdims=True) + EPS)
            o_ref[:, h * LANES:(h + 1) * LANES] = ((o * subln) * (1.0 - lam_init)).astype(bf16)


def _ctx1_kernel(lam_ref, subln_ref, qkv_ref, o_ref, *, lam_init):
    lo = _lane_lo((1, LANES))
    lam = _diff_lambda(lam_ref, lam_init)
    blk = lambda base, h: qkv_ref[:, base + h * LANES:base + (h + 1) * LANES]
    heads = [(h, blk(0, h), [blk(D_MODEL, h)], [blk(2 * D_MODEL, h)]) for h in range(8)]
    _diff_heads(heads, o_ref, lam, subln_ref[...], lo, lam_init)


def _ctx1(lamv, subln, qkv, lam_init):
    return pl.pallas_call(
        functools.partial(_ctx1_kernel, lam_init=lam_init),
        grid=(N_PROMPT_BATCH,),
        in_specs=[
            pl.BlockSpec((8, HEAD_DIM), lambda b: (0, 0)),
            pl.BlockSpec((1, LANES), lambda b: (0, 0)),
            pl.BlockSpec((PROMPT_SEQ, 3 * D_MODEL), lambda b: (b, 0)),
        ],
        out_specs=pl.BlockSpec((PROMPT_SEQ, D_MODEL), lambda b: (b, 0)),
        out_shape=jax.ShapeDtypeStruct((N_PROMPT, D_MODEL), bf16),
        compiler_params=_cparams(("parallel",)),
        name="ctx1",
    )(lamv, subln, qkv)


def _lat1_kernel(lam_ref, subln_ref, q_ref, k_ref, v_ref, ck_ref, cv_ref, o_ref, *, lam_init):
    lo = _lane_lo((1, LANES))
    lam = _diff_lambda(lam_ref, lam_init)
    heads = []
    for h in range(8):
        sl = slice(h * LANES, (h + 1) * LANES)
        heads.append((h, q_ref[:, sl], [ck_ref[:, sl].astype(bf16), k_ref[:, sl]],
                      [cv_ref[:, sl].astype(bf16), v_ref[:, sl]]))
    _diff_heads(heads, o_ref, lam, subln_ref[...], lo, lam_init)


def _lat1(lamv, subln, qkv, ck, cv, lam_init):
    sb = N_PROMPT // SAMPLE_SEQ
    nq = SAMPLE_SEQ // TM
    return pl.pallas_call(
        functools.partial(_lat1_kernel, lam_init=lam_init),
        grid=(N_SAMPLE_BATCH, nq),
        in_specs=[
            pl.BlockSpec((8, HEAD_DIM), lambda b, n: (0, 0)),
            pl.BlockSpec((1, LANES), lambda b, n: (0, 0)),
            pl.BlockSpec((TM, D_MODEL), lambda b, n: (N_PROMPT_TILES + b * nq + n, 0)),
            pl.BlockSpec((SAMPLE_SEQ, D_MODEL), lambda b, n: (sb + b, 1)),
            pl.BlockSpec((SAMPLE_SEQ, D_MODEL), lambda b, n: (sb + b, 2)),
            pl.BlockSpec((None, PAST_LEN, D_MODEL), lambda b, n: (b, 0, 0)),
            pl.BlockSpec((None, PAST_LEN, D_MODEL), lambda b, n: (b, 0, 0)),
        ],
        out_specs=pl.BlockSpec((TM, D_MODEL), lambda b, n: (b * nq + n, 0)),
        out_shape=jax.ShapeDtypeStruct((N_SAMPLE, D_MODEL), bf16),
        compiler_params=_cparams(("parallel", "arbitrary")),
        name="lat1",
    )(lamv, subln, qkv, qkv, qkv, ck, cv)


def _split_bf16(a):
    hi = a.astype(bf16)
    return hi, (a - hi.astype(f32)).astype(bf16)


def _route_kernel(*refs, n_x):
    x_refs = refs[:n_x]
    (op_ref, os_ref, mod_ref, g_ref, wo_ref, rwt_ref, rb_ref,
     xnew_ref, xloc_ref, slots_ref, gate_ref, len_ref) = refs[n_x:]
    logits = [_route_logits(t, x_refs, op_ref, os_ref, mod_ref, g_ref, wo_ref, rwt_ref, xnew_ref)
              for t in range(ROUTE_TILES)]
    tiles = _route_tiles(jnp.concatenate([lg for _, lg in logits], axis=1), rb_ref, slots_ref, gate_ref, len_ref)
    for t, (slots, run_len) in enumerate(tiles):
        _route_dispatch(t, slots, logits[t][0], run_len, xloc_ref)


def _route_logits(t, x_refs, op_ref, os_ref, mod_ref, g_ref, wo_ref, rwt_ref, xnew_ref):
    is_prompt = pl.program_id(0) < N_PROMPT_TILES // ROUTE_TILES
    rows = slice(t * TM, (t + 1) * TM)
    attn = jnp.where(is_prompt, op_ref[rows, :], os_ref[rows, :])
    x_in = x_refs[0][rows, :] if len(x_refs) == 1 else jnp.where(is_prompt, x_refs[0][rows, :], x_refs[1][rows, :])
    x = x_in + mod_ref[2:3, :] * _dot(attn, wo_ref[...])
    xnew_ref[rows, :] = x
    h = _norm_mod(x, g_ref[...], mod_ref[4:5, :], mod_ref[3:4, :])
    h_hi, h_lo = _split_bf16(h)
    w_hi, w_lo = _split_bf16(rwt_ref[...])
    return h_hi, _dot_nt(w_hi, h_hi) + (_dot_nt(w_hi, h_lo) + _dot_nt(w_lo, h_hi))


def _route_tiles(logits, rb_ref, slots_ref, gate_ref, len_ref):
    ng, ge = N_GROUPS, N_EXPERTS // N_GROUPS
    n = ROUTE_TILES * TM
    tile = lambda a, t: a[..., t * TM:(t + 1) * TM]
    scores = jax.nn.sigmoid(logits)
    biased = scores + rb_ref[...]
    s3 = scores.reshape(ng, ge, n)
    b3 = biased.reshape(ng, ge, n)
    in_group = lax.broadcasted_iota(i32, (ng, ge, n), 1).astype(f32)
    group_id = lax.broadcasted_iota(i32, (ng, 1, n), 0).astype(f32)
    expert_id = lax.broadcasted_iota(i32, (ng, ge, n), 0).astype(f32) * ge + in_group

    def max01(a):
        return jnp.max(jnp.max(a, axis=0, keepdims=True), axis=1, keepdims=True)

    def min01(a):
        return jnp.min(jnp.min(a, axis=0, keepdims=True), axis=1, keepdims=True)

    def sum01(a):
        return jnp.sum(jnp.sum(a, axis=0, keepdims=True), axis=1, keepdims=True)

    m1 = jnp.max(b3, axis=1, keepdims=True)
    first = jnp.min(jnp.where(b3 == m1, in_group, ge), axis=1, keepdims=True)
    m2 = jnp.max(jnp.where(in_group == first, -jnp.inf, b3), axis=1, keepdims=True)
    gscore = m1 + m2
    gsel = jnp.zeros((ng, 1, n), f32)
    for _ in range(TOPK_GROUPS):
        gm = jnp.max(gscore, axis=0, keepdims=True)
        gi = jnp.min(jnp.where(gscore == gm, group_id, ng), axis=0, keepdims=True)
        hit = group_id == gi
        gsel = jnp.where(hit, 1.0, gsel)
        gscore = jnp.where(hit, -jnp.inf, gscore)
    cand = jnp.where(jnp.broadcast_to(gsel, (ng, ge, n)) > 0.0, b3, -jnp.inf)
    top_e, top_w = [], []
    for _ in range(TOP_K):
        em = max01(cand)
        ei = min01(jnp.where(cand == em, expert_id, N_EXPERTS))
        hit = expert_id == ei
        top_e.append(ei)
        top_w.append(sum01(jnp.where(hit, s3, 0.0)))
        cand = jnp.where(hit, -jnp.inf, cand)
    wsum = functools.reduce(lambda a, b: a + b, top_w)
    sel3 = jnp.zeros((ng, ge, n), f32)
    for k, (ei, w) in enumerate(zip(top_e, top_w)):
        gate = (w / wsum * ROUTED_SCALE).reshape(1, n)
        for t in range(ROUTE_TILES):
            gate_ref[t, k:k + 1, :] = tile(gate, t)
        sel3 = jnp.where(expert_id == ei, 1.0, sel3)
    sel = sel3.reshape(N_EXPERTS, n)

    r_i = lax.broadcasted_iota(i32, (N_EXPERTS, N_EXPERTS), 0)
    c_i = lax.broadcasted_iota(i32, (N_EXPERTS, N_EXPERTS), 1)
    lower = jnp.where(c_i < r_i, 1.0, 0.0).astype(bf16)
    run_lens, run_offs = [], []
    for t in range(ROUTE_TILES):
        cnt = jnp.sum(tile(sel, t), axis=1, keepdims=True)
        run_len = jnp.ceil(cnt * (1.0 / CHUNK)) * CHUNK
        run_off = _dot(lower, jnp.broadcast_to(run_len, (N_EXPERTS, LANES)).astype(bf16))[:, 0:1]
        run_lens.append(run_len)
        run_offs.append(jnp.broadcast_to(run_off, (N_EXPERTS, TM)))
    t_r = lax.broadcasted_iota(i32, (n, n), 0)
    t_c = lax.broadcasted_iota(i32, (n, n), 1)
    before = jnp.where(jnp.logical_and(t_r < t_c, t_r // TM == t_c // TM), 1.0, 0.0).astype(bf16)
    rank = _dot(sel.astype(bf16), before)
    slot3 = (jnp.concatenate(run_offs, axis=1) + rank).reshape(ng, ge, n)
    slots = [sum01(jnp.where(expert_id == ei, slot3, 0.0)).reshape(1, n).astype(i32) for ei in top_e]
    out = []
    for t in range(ROUTE_TILES):
        for k in range(TOP_K):
            slots_ref[t, k:k + 1, :] = tile(slots[k], t)
        slots_ref[t, TOP_K:8, :] = jnp.full((8 - TOP_K, TM), -1, i32)
        gate_ref[t, TOP_K:8, :] = jnp.zeros((8 - TOP_K, TM), f32)
        len_ref[t] = jnp.broadcast_to(run_lens[t], (N_EXPERTS, LANES)).astype(i32)
        out.append(([tile(sl, t) for sl in slots], run_lens[t]))
    return out


def _route_dispatch(t, slots, h_hi, run_len, xloc_ref):
    rows = ROUTE_ROWS

    def body(c, carry):
        base = pl.multiple_of(c * rows, rows)
        row_id = base.astype(jnp.int16) + lax.broadcasted_iota(jnp.int16, (rows, TM), 0)
        p = jnp.zeros((rows, TM), bf16)
        for k in range(TOP_K):
            p = jnp.where(row_id == slots[k].astype(jnp.int16), jnp.ones((), bf16), p)
        xloc_ref[pl.ds(t * SLOTS + base, rows), :] = _dot(p, h_hi).astype(bf16)
        return carry

    def zero_body(c, carry):
        base = pl.multiple_of(c * rows, rows)
        xloc_ref[pl.ds(t * SLOTS + base, rows), :] = jnp.zeros((rows, D_MODEL), bf16)
        return carry

    n_used = (jnp.sum(run_len).astype(i32) + (rows - 1)) // rows
    lax.fori_loop(0, n_used, body, 0)
    lax.fori_loop(n_used, SLOTS // rows, zero_body, 0)


def _route(x_parts, o_prompt, o_sample, mod_l, g, w_out, rwt, rb):
    per_tile = lambda i: (i, 0, 0)
    rt = ROUTE_TILES
    tm = rt * TM
    return pl.pallas_call(
        functools.partial(_route_kernel, n_x=len(x_parts)),
        grid=(N_TILES // rt,),
        in_specs=_x_specs(x_parts, tm) + [
            pl.BlockSpec((tm, D_MODEL), lambda i: _prompt_block(i, tm)),
            pl.BlockSpec((tm, D_MODEL), lambda i: _sample_block(i, tm)),
            pl.BlockSpec((None, 6, D_MODEL), lambda i: (_mod_row(i, tm), 0, 0)),
            pl.BlockSpec((1, D_MODEL), lambda i: (0, 0)),
            pl.BlockSpec((D_MODEL, D_MODEL), lambda i: (0, 0)),
            pl.BlockSpec((N_EXPERTS, D_MODEL), lambda i: (0, 0)),
            pl.BlockSpec((N_EXPERTS, 1), lambda i: (0, 0)),
        ],
        out_specs=[
            pl.BlockSpec((tm, D_MODEL), lambda i: (i, 0)),
            pl.BlockSpec((rt * SLOTS, D_MODEL), lambda i: (i, 0)),
            pl.BlockSpec((rt, 8, TM), per_tile),
            pl.BlockSpec((rt, 8, TM), per_tile),
            pl.BlockSpec((rt, N_EXPERTS, LANES), per_tile),
        ],
        out_shape=[
            jax.ShapeDtypeStruct((N_TOK, D_MODEL), f32),
            jax.ShapeDtypeStruct((N_TILES * SLOTS, D_MODEL), bf16),
            jax.ShapeDtypeStruct((N_TILES, 8, TM), i32),
            jax.ShapeDtypeStruct((N_TILES, 8, TM), f32),
            jax.ShapeDtypeStruct((N_TILES, N_EXPERTS, LANES), i32),
        ],
        compiler_params=_cparams(("parallel",)),
        name="route",
    )(*x_parts, o_prompt, o_sample, mod_l, g, w_out, rwt, rb)


def _moe_plan(run_len):
    nt, ne = run_len.shape

    def excl_cumsum(a):
        n = a.shape[-1]
        earlier = np.arange(n)[None, :] < np.arange(n)[:, None]
        return jnp.sum(jnp.where(earlier, a[..., None, :], 0), axis=-1)

    def first_diff(a):
        return a - jnp.concatenate([jnp.zeros_like(a[..., :1]), a[..., :-1]], axis=-1)

    off_loc = excl_cumsum(run_len)
    before = excl_cumsum(run_len.T).T
    n_e = jnp.sum(run_len, axis=0)
    n_pad = -(-n_e // GM) * GM
    g_start = excl_cumsum(n_pad)
    total = jnp.sum(n_pad)
    run_dst = g_start[None, :] + before
    run_src = jnp.arange(nt, dtype=i32)[:, None] * SLOTS + off_loc
    dst_f = run_dst.T.reshape(-1)
    shift_f = first_diff((run_src - run_dst).T.reshape(-1))
    rows = jnp.arange((G_TILES + GMM_ITEM_TILES - 1) * G_CHUNKS, dtype=i32) * CHUNK
    shift = jnp.sum(jnp.where(dst_f[None, :] <= rows[:, None], shift_f[None, :], 0), axis=1)
    in_run = jnp.any((g_start[None, :] <= rows[:, None]) & (rows[:, None] < (g_start + n_e)[None, :]), axis=1)
    chunk_src = (jnp.where(in_run, rows + shift, 0) // CHUNK).astype(i32)
    loc_rows = jnp.arange(SLOT_CHUNKS, dtype=i32) * CHUNK
    shift_l = first_diff(run_dst - off_loc)
    shift = jnp.sum(jnp.where(off_loc[:, None, :] <= loc_rows[None, :, None], shift_l[:, None, :], 0), axis=2)
    used = jnp.sum(run_len, axis=1)
    chunk_map = jnp.where(loc_rows[None, :] < used[:, None], (loc_rows[None, :] + shift) // CHUNK, 0).astype(i32)
    tile_start, n_tiles = g_start // GM, n_pad // GM
    n_items = -(-n_tiles // GMM_ITEM_TILES)
    item_start = excl_cumsum(n_items)
    items = jnp.arange(GMM_MAX_ITEMS, dtype=i32)
    owner = items[:, None] >= item_start[None, :]
    e_first = jnp.sum(jnp.where(owner, first_diff(tile_start - GMM_ITEM_TILES * item_start)[None, :], 0), axis=1)
    item_tile = e_first + GMM_ITEM_TILES * items
    e_end = jnp.sum(jnp.where(owner, first_diff(tile_start + n_tiles)[None, :], 0), axis=1)
    item_cnt = jnp.clip(e_end - item_tile, 0, GMM_ITEM_TILES)
    gmm_plan = tuple(a.astype(i32) for a in (item_start, n_items, item_tile, item_cnt, chunk_src))
    return gmm_plan, chunk_map.reshape(-1)


def _gmm_in_copy(xloc_hbm, xbuf, sem, src_chunk, slot, c):
    return pltpu.make_async_copy(xloc_hbm.at[src_chunk], xbuf.at[slot, c], sem.at[slot])


def _gmm_out_copy(ybuf, y_hbm, sem, tile, slot, n_tiles):
    chunks = n_tiles * G_CHUNKS
    return pltpu.make_async_copy(ybuf.at[slot, pl.ds(0, chunks)],
                                 y_hbm.at[pl.ds(tile * G_CHUNKS, chunks)], sem.at[slot])


def _gmm_kernel(i0_ref, ni_ref, it_ref, ic_ref, cs_ref, xloc_hbm, wg_ref, wu_ref, wd_ref, y_hbm,
                xbuf, ybuf, zbuf, wg_b, wu_b, wd_b, in_sem, out_sem, zsem):
    e = pl.program_id(0)
    last = pl.num_programs(0) - 1
    n_items = ni_ref[e]
    first_item = i0_ref[e]
    total_items = i0_ref[last] + ni_ref[last]
    last_item = total_items - 1
    total_tiles = it_ref[last_item] + ic_ref[last_item]

    def start_in(item):
        first = it_ref[item] * G_CHUNKS
        for c in range(GMM_ITEM_TILES * G_CHUNKS):
            _gmm_in_copy(xloc_hbm, xbuf, in_sem, cs_ref[first + c], item % GMM_SLOTS, c).start()

    def wait_in(item):
        for c in range(GMM_ITEM_TILES * G_CHUNKS):
            _gmm_in_copy(xloc_hbm, xbuf, in_sem, 0, item % GMM_SLOTS, c).wait()

    def out_copy(item, fn):
        for cnt in range(1, GMM_ITEM_TILES + 1):
            @pl.when(ic_ref[item] == cnt)
            def _():
                fn(_gmm_out_copy(ybuf, y_hbm, out_sem, it_ref[item], item % GMM_SLOTS, cnt))

    @pl.when(e == 0)
    def _():
        for item in range(GMM_SLOTS - 1):
            start_in(item)
        zbuf[...] = jnp.zeros(zbuf.shape, zbuf.dtype)

    def tail_copies(fn):
        for j in range(_GMM_TAIL_PER_STEP):
            tile = total_tiles + e + j * N_EXPERTS

            @pl.when(tile < G_TILES)
            def _():
                fn(pltpu.make_async_copy(zbuf, y_hbm.at[pl.ds(tile * G_CHUNKS, G_CHUNKS)], zsem.at[0]))

    tail_copies(lambda cp: cp.start())

    @pl.when(n_items > 0)
    def _():
        wg_b[...] = wg_ref[...].astype(bf16)
        wu_b[...] = wu_ref[...].astype(bf16)
        wd_b[...] = wd_ref[...].astype(bf16)

    def body(j, carry):
        item = first_item + j
        slot = item % GMM_SLOTS

        @pl.when(item + (GMM_SLOTS - 1) < total_items)
        def _():
            start_in(item + (GMM_SLOTS - 1))

        wait_in(item)

        @pl.when(item >= GMM_SLOTS)
        def _():
            out_copy(item - GMM_SLOTS, lambda cp: cp.wait())

        for cnt in range(1, GMM_ITEM_TILES + 1):
            @pl.when(ic_ref[item] == cnt)
            def _():
                rows = cnt * GM
                chunks = cnt * G_CHUNKS
                x = xbuf[slot, 0:chunks].reshape(rows, D_MODEL)
                act = _silu(_dot(x, wg_b[...])) * _dot(x, wu_b[...])
                y = _dot(act.astype(bf16), wd_b[...]).astype(bf16)
                ybuf[slot, 0:chunks] = y.reshape(chunks, CHUNK, D_MODEL)

        out_copy(item, lambda cp: cp.start())
        return carry

    lax.fori_loop(0, n_items, body, 0)
    tail_copies(lambda cp: cp.wait())

    @pl.when(e == last)
    def _():
        for back in range(1, GMM_SLOTS + 1):
            out_copy(total_items - back, lambda cp: cp.wait())


_GMM_TAIL_PER_STEP = -(-(G_TILES - TM * TOP_K * N_TILES // GM) // N_EXPERTS)


def _gmm(plan, xloc, wg, wu, wd, layer):
    rows = GMM_ITEM_TILES * GM
    w_idx = lambda e, *_: (layer, e, 0, 0)
    grid_spec = pltpu.PrefetchScalarGridSpec(
        num_scalar_prefetch=5,
        grid=(N_EXPERTS,),
        in_specs=[
            pl.BlockSpec(memory_space=pl.ANY),
            pl.BlockSpec((None, None, D_MODEL, EXPERT_DIM), w_idx),
            pl.BlockSpec((None, None, D_MODEL, EXPERT_DIM), w_idx),
            pl.BlockSpec((None, None, EXPERT_DIM, D_MODEL), w_idx),
        ],
        out_specs=pl.BlockSpec(memory_space=pl.ANY),
        scratch_shapes=[pltpu.VMEM((GMM_SLOTS, rows // CHUNK, CHUNK, D_MODEL), bf16),
                        pltpu.VMEM((GMM_SLOTS, rows // CHUNK, CHUNK, D_MODEL), bf16),
                        pltpu.VMEM((G_CHUNKS, CHUNK, D_MODEL), bf16),
                        pltpu.VMEM((D_MODEL, EXPERT_DIM), bf16), pltpu.VMEM((D_MODEL, EXPERT_DIM), bf16),
                        pltpu.VMEM((EXPERT_DIM, D_MODEL), bf16),
                        pltpu.SemaphoreType.DMA((GMM_SLOTS,)), pltpu.SemaphoreType.DMA((GMM_SLOTS,)),
                        pltpu.SemaphoreType.DMA((1,))],
    )
    return pl.pallas_call(
        _gmm_kernel,
        grid_spec=grid_spec,
        out_shape=jax.ShapeDtypeStruct((G_TILES * G_CHUNKS, CHUNK, D_MODEL), bf16),
        compiler_params=_cparams(("arbitrary",)),
        name="gmm",
    )(*plan, xloc, wg, wu, wd)


def _combine_copy(y_hbm, ybuf, sem, sorted_chunk, slot, c):
    return pltpu.make_async_copy(y_hbm.at[sorted_chunk], ybuf.at[slot, c], sem.at[slot])


def _combine_kernel(cm_ref, y_hbm, slots_ref, gate_ref, x_ref, mod_ref, g_ref, sg_ref, su_ref, sd_ref, *rest, final):
    if final:
        gf_ref, yp_ref, ys_ref, ybuf, sem = rest
    else:
        o_ref, ybuf, sem = rest
    i = pl.program_id(0)
    n = pl.num_programs(0)
    ahead = COMBINE_SLOTS - 1
    slot = i % COMBINE_SLOTS

    def start(tile, s):
        for c in range(SLOT_CHUNKS):
            _combine_copy(y_hbm, ybuf, sem, cm_ref[tile * SLOT_CHUNKS + c], s, c).start()

    def wait(s):
        for c in range(SLOT_CHUNKS):
            _combine_copy(y_hbm, ybuf, sem, 0, s, c).wait()

    @pl.when(i == 0)
    def _():
        for tile in range(ahead):
            start(tile, tile)

    wait(slot)
    start((i + ahead) % n, (i + ahead) % COMBINE_SLOTS)

    x = x_ref[...]
    hb = _norm_mod(x, g_ref[...], mod_ref[4:5, :], mod_ref[3:4, :]).astype(bf16)
    shared = _dot((_silu(_dot(hb, sg_ref[...])) * _dot(hb, su_ref[...])).astype(bf16), sd_ref[...])
    row_id = lax.broadcasted_iota(jnp.int16, (SLOTS, TM), 0)
    p = jnp.zeros((SLOTS, TM), bf16)
    for k in range(TOP_K):
        p = jnp.where(row_id == slots_ref[k:k + 1, :].astype(jnp.int16), gate_ref[k:k + 1, :].astype(bf16), p)
    routed = lax.dot_general(p, ybuf[slot].reshape(SLOTS, D_MODEL), (((0,), (0,)), ((), ())),
                             preferred_element_type=f32)
    out = x + mod_ref[5:6, :] * (routed + shared)
    if final:
        y = (out * lax.rsqrt(jnp.mean(out * out, axis=-1, keepdims=True) + EPS)) * gf_ref[...]

        @pl.when(i < N_PROMPT_TILES)
        def _():
            yp_ref[...] = y

        @pl.when(i >= N_PROMPT_TILES)
        def _():
            ys_ref[...] = y
    else:
        o_ref[...] = out

    @pl.when(i == n - 1)
    def _():
        for k in range(1, ahead + 1):
            wait((i + k) % COMBINE_SLOTS)


def _combine(chunk_map, y, slots, gates, x, mod_l, g, sg, su, sd, final_g=None):
    shd = sg.shape[1]
    final = final_g is not None
    row_spec = pl.BlockSpec((TM, D_MODEL), lambda i, cm: (i, 0))
    vec_spec = pl.BlockSpec((1, D_MODEL), lambda i, cm: (0, 0))
    if final:
        out_specs = [pl.BlockSpec((TM, D_MODEL), lambda i, cm: _prompt_block(i)),
                     pl.BlockSpec((TM, D_MODEL), lambda i, cm: _sample_block(i))]
        out_shape = [jax.ShapeDtypeStruct((N_PROMPT, D_MODEL), f32), jax.ShapeDtypeStruct((N_SAMPLE, D_MODEL), f32)]
    else:
        out_specs, out_shape = row_spec, jax.ShapeDtypeStruct((N_TOK, D_MODEL), f32)
    grid_spec = pltpu.PrefetchScalarGridSpec(
        num_scalar_prefetch=1,
        grid=(N_TILES,),
        in_specs=[
            pl.BlockSpec(memory_space=pl.ANY),
            pl.BlockSpec((None, 8, TM), lambda i, cm: (i, 0, 0)),
            pl.BlockSpec((None, 8, TM), lambda i, cm: (i, 0, 0)),
            row_spec,
            pl.BlockSpec((None, 6, D_MODEL), lambda i, cm: (_mod_row(i), 0, 0)),
            vec_spec,
            pl.BlockSpec((D_MODEL, shd), lambda i, cm: (0, 0)),
            pl.BlockSpec((D_MODEL, shd), lambda i, cm: (0, 0)),
            pl.BlockSpec((shd, D_MODEL), lambda i, cm: (0, 0)),
        ] + ([vec_spec] if final else []),
        out_specs=out_specs,
        scratch_shapes=[pltpu.VMEM((COMBINE_SLOTS, SLOT_CHUNKS, CHUNK, D_MODEL), bf16),
                        pltpu.SemaphoreType.DMA((COMBINE_SLOTS,))],
    )
    args = (chunk_map, y, slots, gates, x, mod_l, g, sg, su, sd) + ((final_g,) if final else ())
    return pl.pallas_call(
        functools.partial(_combine_kernel, final=final),
        grid_spec=grid_spec,
        out_shape=out_shape,
        compiler_params=_cparams(("arbitrary",)),
        name="combine",
    )(*args)


def _moe(x_parts, o_prompt, o_sample, w_out, mod_l, g, rwt, rb, wg, wu, wd, layer, sg, su, sd, final_g=None):
    x, xloc, slots, gates, run_len = _route(x_parts, o_prompt, o_sample, mod_l, g, w_out, rwt, rb)
    gmm_plan, chunk_map = _moe_plan(run_len[:, :, 0])
    y = _gmm(gmm_plan, xloc.reshape(N_TILES * SLOT_CHUNKS, CHUNK, D_MODEL), wg, wu, wd, layer)
    return _combine(chunk_map, y, slots, gates, x, mod_l, g, sg, su, sd, final_g)


_L0_CHUNKS = (
    (0, 512, 0, (0, 1, 2, 3), ()),
    (512, 1024, 768, (), ()),
    (1024, 1536, 1280, (), ((2, 0, 512, 0, False),)),
    (1536, 2048, 1792, (), ((3, 0, 512, 0, False),)),
    (2048, 2304, 512, (0,), ((0, 0, 128, 0, False), (1, 128, 256, 0, False))),
)
_L0_KV_OUTS = (("T", 128), ("T", 128), ("T", 512), ("T", 512))
_L1_CHUNKS = (
    (0, 512, 0, (0, 1, 2, 3), ()),
    (512, 1024, 512, (0, 1, 2, 3), ()),
    (1024, 1536, 1024, (0, 1, 2, 3), ((0, 0, 512, 0, False),)),
    (1536, 2048, 1536, (0, 1, 2, 3), ((0, 0, 512, 512, False),)),
    (2048, 2560, 2048, (), ((1, 0, 512, 0, True),)),
    (2560, 3072, 2560, (), ((1, 0, 512, 512, True),)),
)
_L1_KV_OUTS = (("T", 1024), ("H", 8))


def _from_feature_major(kt, *head_dims):
    nb, _, s = kt.shape
    nd = len(head_dims)
    return kt.reshape(nb, *head_dims, s).transpose(0, nd + 1, *range(1, nd + 1))[:, None]


def kernel(x_prompt, x_sample, cache_a_k, cache_a_v, cache_b_k, cache_b_v, cache_c_k, cache_c_v, c, c_ctx, w_mod, b_mod, norm_mix, norm_ffn, w_in_ab, w_out_ab, sink_a, rel_bias_b, w_in_c, w_out_c, lam_q1, lam_k1, lam_q2, lam_k2, subln_c, router_w, router_bias, exp_w_gate, exp_w_up, exp_w_down, sh_w_gate, sh_w_up, sh_w_down, final_norm):
    x = (x_prompt.reshape(N_PROMPT, D_MODEL), x_sample.reshape(N_SAMPLE, D_MODEL))
    cond8 = jnp.concatenate([c_ctx[None, :], c, jnp.zeros((8 - 1 - N_SAMPLE_BATCH, D_MODEL), f32)], axis=0)
    mod = _adaln(cond8, w_mod, b_mod).reshape(DEPTH, 8, 6, D_MODEL)
    rope_tabs = _rope_tables()
    new_kv = {}
    for layer in range(DEPTH):
        li = layer // 2
        mod_l = mod[layer]
        g_mix = norm_mix[layer][None, :]
        g_ffn = norm_ffn[layer][None, :]
        if layer % 2 == 0:
            w_in = w_in_ab[li].astype(bf16)
            qkv, ak, av, bk, bv = _inproj(x, mod_l, g_mix, w_in, rope_tabs, _L0_CHUNKS, _L0_KV_OUTS)
            new_kv["a_k"], new_kv["a_v"], new_kv["b_k"], new_kv["b_v"] = ak, av, bk, bv
            o_p = _ctx0(sink_a[li], qkv)
            o_s = _lat0(sink_a[li], qkv,
                        cache_a_k[:, li].reshape(N_SAMPLE_BATCH, PAST_LEN, LANES),
                        cache_a_v[:, li].reshape(N_SAMPLE_BATCH, PAST_LEN, LANES),
                        cache_b_k[:, li].reshape(N_SAMPLE_BATCH, PAST_LEN, 512),
                        cache_b_v[:, li].reshape(N_SAMPLE_BATCH, PAST_LEN, 512),
                        _na_bias_tiles(rel_bias_b[li]))
            w_out = w_out_ab[li].astype(bf16)
        else:
            lam_init = 0.8 - 0.6 * math.exp(-0.3 * layer)
            qkv, ck, cv = _inproj(x, mod_l, g_mix, w_in_c[li].astype(bf16), rope_tabs, _L1_CHUNKS, _L1_KV_OUTS)
            new_kv["c_k"], new_kv["c_v"] = ck, cv
            lamv = jnp.concatenate([lam_q1[li][None], lam_k1[li][None], lam_q2[li][None], lam_k2[li][None],
                                    jnp.zeros((4, HEAD_DIM), f32)], axis=0)
            subln = subln_c[li][None, :]
            o_p = _ctx1(lamv, subln, qkv, lam_init)
            o_s = _lat1(lamv, subln, qkv,
                        cache_c_k[:, li].reshape(N_SAMPLE_BATCH, PAST_LEN, D_MODEL),
                        cache_c_v[:, li].reshape(N_SAMPLE_BATCH, PAST_LEN, D_MODEL), lam_init)
            w_out = w_out_c[li].astype(bf16)
        last = layer == DEPTH - 1
        x = _moe(x, o_p, o_s, w_out, mod_l, g_ffn, router_w[layer].T, router_bias[layer][:, None],
                 exp_w_gate, exp_w_up, exp_w_down, layer,
                 sh_w_gate[layer].astype(bf16), sh_w_up[layer].astype(bf16), sh_w_down[layer].astype(bf16),
                 final_norm[None, :] if last else None)
        x = x if last else (x,)
    y_prompt, y_sample = x
    nb, s = N_PROMPT_BATCH, PROMPT_SEQ
    return (y_prompt.reshape(nb, s, D_MODEL), y_sample.reshape(N_SAMPLE_BATCH, SAMPLE_SEQ, D_MODEL),
            _from_feature_major(new_kv["a_k"], 2, HEAD_DIM), _from_feature_major(new_kv["a_v"], 2, HEAD_DIM),
            _from_feature_major(new_kv["b_k"], 8, HEAD_DIM), _from_feature_major(new_kv["b_v"], 8, HEAD_DIM),
            _from_feature_major(new_kv["c_k"], 8, 2, HEAD_DIM), new_kv["c_v"].reshape(nb, 1, s, 8, 2 * HEAD_DIM))
```

```python
import functools
import math

import jax
import jax.numpy as jnp
import numpy as np
from jax import lax
from jax.experimental import pallas as pl
from jax.experimental.pallas import tpu as pltpu

f32 = jnp.float32
bf16 = jnp.bfloat16
i32 = jnp.int32

D_MODEL = 1024
N_PROMPT_BATCH = 16
PROMPT_SEQ = 256
DEPTH = 2
N_SAMPLE_BATCH = 2
SAMPLE_SEQ = 2048
PAST_LEN = 512
GRID_W = 64
HEAD_DIM = 64
ROPE_THETA = 10000.0
EPS = 1e-6
A_WINDOW = 128
NA_ROWS = 8
NA_COLS = 16
N_EXPERTS = 64
TOP_K = 6
N_GROUPS = 8
TOPK_GROUPS = 4
EXPERT_DIM = 256
ROUTED_SCALE = 2.5
Q_SCALE = HEAD_DIM ** -0.5

N_PROMPT = N_PROMPT_BATCH * PROMPT_SEQ
N_SAMPLE = N_SAMPLE_BATCH * SAMPLE_SEQ
N_TOK = N_PROMPT + N_SAMPLE

LANES = 128
TM = 256
TD = 512
N_PROMPT_TILES = N_PROMPT // TM
N_TILES = N_TOK // TM
QB = 128
CHUNK = 16
SLOTS = -(-(TM * TOP_K + N_EXPERTS * (CHUNK - 1)) // 256) * 256
SLOT_CHUNKS = SLOTS // CHUNK
ROUTE_ROWS = 512
ROUTE_TILES = 2
COMBINE_SLOTS = 4
GM = 256
_MAX_SORTED = TM * TOP_K * N_TILES + N_TILES * N_EXPERTS * (CHUNK - 1) + N_EXPERTS * (GM - CHUNK)
G_TILES = -(-_MAX_SORTED // GM)
G_CHUNKS = GM // CHUNK
GMM_ITEM_TILES = 2
GMM_SLOTS = 4
GMM_MAX_ITEMS = (G_TILES + N_EXPERTS * (GMM_ITEM_TILES - 1)) // GMM_ITEM_TILES
_GMM_TAIL_PER_STEP = -(-(G_TILES - TM * TOP_K * N_TILES // GM) // N_EXPERTS)
ADA_COLS = 1536
DIFF_HEADS_PER_PASS = 2
VMEM_LIMIT = 56 * 1024 * 1024

NEG = -1e30


def _cparams(sem):
    return pltpu.CompilerParams(dimension_semantics=sem, vmem_limit_bytes=VMEM_LIMIT)


def _mod_row(i, tm=TM):
    return jnp.where(i < N_PROMPT // tm, 0, 1 + (i - N_PROMPT // tm) // (SAMPLE_SEQ // tm))


def _prompt_block(i, tm=TM):
    return (jnp.minimum(i, N_PROMPT // tm - 1), 0)


def _sample_block(i, tm=TM):
    return (jnp.maximum(i - N_PROMPT // tm, 0), 0)


def _x_specs(parts, tm=TM):
    if len(parts) == 1:
        return [pl.BlockSpec((tm, D_MODEL), lambda i, *_: (i, 0))]
    return [pl.BlockSpec((tm, D_MODEL), lambda i, *_: _prompt_block(i, tm)),
            pl.BlockSpec((tm, D_MODEL), lambda i, *_: _sample_block(i, tm))]


def _load_x(i, x_refs, tm=TM):
    if len(x_refs) == 1:
        return x_refs[0][...]
    return jnp.where(i < N_PROMPT // tm, x_refs[0][...], x_refs[1][...])


def _norm_mod(x, g, scale, shift):
    y = x * lax.rsqrt(jnp.mean(x * x, axis=-1, keepdims=True) + EPS)
    return (y * g) * (1.0 + scale) + shift


def _silu(x):
    return x * jax.nn.sigmoid(x)


def _dot(a, b):
    return jnp.dot(a, b, preferred_element_type=f32)


def _dot_nt(a, b):
    return lax.dot_general(a, b, (((1,), (1,)), ((), ())), preferred_element_type=f32)


def _adaln_kernel(cond_ref, w_ref, b_ref, o_ref):
    s = _silu(cond_ref[...]).astype(bf16)
    o_ref[...] = _dot(s, w_ref[...].astype(bf16)) + b_ref[...]


def _adaln(cond8, w_mod, b_mod):
    n6 = 6 * D_MODEL
    return pl.pallas_call(
        _adaln_kernel,
        grid=(DEPTH, n6 // ADA_COLS),
        in_specs=[
            pl.BlockSpec((8, D_MODEL), lambda l, j: (0, 0)),
            pl.BlockSpec((None, D_MODEL, ADA_COLS), lambda l, j: (l, 0, j)),
            pl.BlockSpec((None, 1, ADA_COLS), lambda l, j: (l, 0, j)),
        ],
        out_specs=pl.BlockSpec((None, 8, ADA_COLS), lambda l, j: (l, 0, j)),
        out_shape=jax.ShapeDtypeStruct((DEPTH, 8, n6), f32),
        compiler_params=_cparams(("parallel", "parallel")),
        name="adaln",
    )(cond8, w_mod, b_mod.reshape(DEPTH, 1, n6))


def _rope_block(blk, cos, sin_a, sin_b):
    return blk * cos + pltpu.roll(blk, LANES - 16, 1) * sin_a + pltpu.roll(blk, 16, 1) * sin_b


def _inproj_kernel(*refs, chunks, n_x):
    x_refs, kv_refs = refs[:n_x], refs[n_x + 7:]
    mod_ref, g_ref, w_ref, cos_ref, sa_ref, sb_ref, qkv_ref = refs[n_x:n_x + 7]
    i = pl.program_id(0)
    h = _norm_mod(_load_x(i, x_refs, TD), g_ref[...], mod_ref[1:2, :], mod_ref[0:1, :]).astype(bf16)
    is_prompt = i < N_PROMPT // TD

    @pl.when(is_prompt)
    def _():
        for c0, c1, s0, _, kv_out in chunks:
            acc = _dot(h, w_ref[:, s0:s0 + (c1 - c0)])
            qkv_ref[:, c0:c1] = acc.astype(bf16)
            for ridx, a0, a1, o0, per_head in kv_out:
                if per_head:
                    heads = kv_refs[ridx].shape[0] // TD
                    for j in range((a1 - a0) // LANES):
                        kv_refs[ridx][pl.ds(o0 // LANES + j, TD, stride=heads), :] = (
                            acc[:, a0 + j * LANES:a0 + (j + 1) * LANES])
                else:
                    t = acc[:, a0:a1].T
                    for b in range(TD // PROMPT_SEQ):
                        kv_refs[ridx][b, o0:o0 + (a1 - a0), :] = t[:, b * PROMPT_SEQ:(b + 1) * PROMPT_SEQ]

    @pl.when(jnp.logical_not(is_prompt))
    def _():
        cos, sa, sb = cos_ref[...], sa_ref[...], sb_ref[...]
        for c0, c1, s0, rope_blocks, _ in chunks:
            acc = _dot(h, w_ref[:, s0:s0 + (c1 - c0)])
            for b in range((c1 - c0) // LANES):
                blk = acc[:, b * LANES:(b + 1) * LANES]
                if b in rope_blocks:
                    blk = _rope_block(blk, cos, sa, sb)
                qkv_ref[:, c0 + b * LANES:c0 + (b + 1) * LANES] = blk.astype(bf16)


def _inproj(x_parts, mod_l, g, w, rope_tabs, chunks, kv_outs):
    n = w.shape[1]
    cos, sa, sb = rope_tabs
    bpt = TD // PROMPT_SEQ

    def rope_idx(i):
        return (jnp.where(i < N_PROMPT // TD, 0, (i - N_PROMPT // TD) % (SAMPLE_SEQ // TD)), 0)

    kv_specs, kv_shapes = [], []
    for kind, size in kv_outs:
        if kind == "T":
            kv_specs.append(pl.BlockSpec((bpt, size, PROMPT_SEQ), lambda i: _prompt_block(i, TD) + (0,)))
            kv_shapes.append(jax.ShapeDtypeStruct((N_PROMPT_BATCH, size, PROMPT_SEQ), f32))
        else:
            kv_specs.append(pl.BlockSpec((TD * size, LANES), lambda i: _prompt_block(i, TD)))
            kv_shapes.append(jax.ShapeDtypeStruct((N_PROMPT * size, LANES), f32))

    return pl.pallas_call(
        functools.partial(_inproj_kernel, chunks=chunks, n_x=len(x_parts)),
        grid=(N_TOK // TD,),
        in_specs=_x_specs(x_parts, TD) + [
            pl.BlockSpec((None, 6, D_MODEL), lambda i: (_mod_row(i, TD), 0, 0)),
            pl.BlockSpec((1, D_MODEL), lambda i: (0, 0)),
            pl.BlockSpec((D_MODEL, n), lambda i: (0, 0)),
            pl.BlockSpec((TD, LANES), rope_idx),
            pl.BlockSpec((TD, LANES), rope_idx),
            pl.BlockSpec((TD, LANES), rope_idx),
        ],
        out_specs=[pl.BlockSpec((TD, n), lambda i: (i, 0))] + kv_specs,
        out_shape=[jax.ShapeDtypeStruct((N_TOK, n), bf16)] + kv_shapes,
        compiler_params=_cparams(("arbitrary",)),
        name="inproj",
    )(*x_parts, mod_l, g, w, cos, sa, sb)


def _rope_tables():
    nq = HEAD_DIM // 4
    t = np.arange(SAMPLE_SEQ)
    inv = np.power(np.float32(ROPE_THETA), -np.arange(nq, dtype=np.float32) / np.float32(nq))
    ang_r = (t // GRID_W).astype(np.float32)[:, None] * inv
    ang_c = (t % GRID_W).astype(np.float32)[:, None] * inv
    zero = np.zeros_like(ang_r)

    def head(fr, fc):
        return np.concatenate([fr[0], fr[1], fc[0], fc[1]], axis=-1)

    cos = head((np.cos(ang_r), np.cos(ang_r)), (np.cos(ang_c), np.cos(ang_c)))
    sin_a = head((-np.sin(ang_r), zero), (-np.sin(ang_c), zero))
    sin_b = head((zero, np.sin(ang_r)), (zero, np.sin(ang_c)))
    two = lambda a: jnp.asarray(np.concatenate([a, a], axis=-1).astype(np.float32))
    return two(cos), two(sin_a), two(sin_b)


def _lane_lo(shape):
    return lax.broadcasted_iota(i32, shape, len(shape) - 1) < HEAD_DIM


def _half(q, lo_mask, half):
    keep = lo_mask if half == 0 else jnp.logical_not(lo_mask)
    return jnp.where(keep, q, jnp.zeros_like(q)) * Q_SCALE


def _swap_halves(x):
    return pltpu.roll(x.astype(f32), HEAD_DIM, 1).astype(x.dtype)


def _stack_halves(q, lo_mask):
    return jnp.concatenate([_half(q, lo_mask, 0), _half(q, lo_mask, 1)], axis=0)


def _with_ones(v):
    return jnp.concatenate([v, jnp.ones_like(v)], axis=1)


def _attend_many(problems):
    scores = [[_dot_nt(q_rows, k) for k in key_blocks] for q_rows, _, key_blocks, _, _, _ in problems]
    exps, maxes = [], []
    for (q_rows, n_heads, key_blocks, _, fix_scores, sinks), sc in zip(problems, scores):
        r = q_rows.shape[0] // n_heads
        e_p, m_p = [[] for _ in key_blocks], []
        for h in range(n_heads):
            blocks = [s[h * r:(h + 1) * r] for s in sc]
            if fix_scores is not None:
                blocks = [fix_scores(h, i, s) for i, s in enumerate(blocks)]
            m = functools.reduce(jnp.maximum, [jnp.max(s, axis=-1, keepdims=True) for s in blocks])
            if sinks is not None:
                m = jnp.maximum(m, sinks[h])
            m_p.append(m)
            for i, s in enumerate(blocks):
                e_p[i].append(jnp.exp((s - m).astype(bf16)))
        exps.append(e_p)
        maxes.append(m_p)
    outs = [functools.reduce(lambda a, b: a + b,
                             [_dot(e[0] if n_heads == 1 else jnp.concatenate(e, axis=0), vx)
                              for e, vx in zip(e_p, vx_blocks)])
            for (_, n_heads, _, vx_blocks, _, _), e_p in zip(problems, exps)]
    results = []
    for (q_rows, n_heads, _, _, _, sinks), out, m_p in zip(problems, outs, maxes):
        r = q_rows.shape[0] // n_heads
        res = []
        for h in range(n_heads):
            den = out[h * r:(h + 1) * r, LANES:]
            if sinks is not None:
                den = den + jnp.exp(sinks[h] - m_p[h])
            res.append(out[h * r:(h + 1) * r, :LANES] * (1.0 / den))
        results.append(res)
    return results


def _gqa_rows(q_blocks, group, lo_mask):
    parts = []
    for q in q_blocks:
        for half in range(2):
            qh = _half(q, lo_mask, half)
            parts.append(qh if half == group else _swap_halves(qh))
    return jnp.concatenate(parts, axis=0)


def _gqa_merge(outs, group, lo_mask):
    fixed = [o if idx % 2 == group else pltpu.roll(o, HEAD_DIM, 1) for idx, o in enumerate(outs)]
    return [jnp.where(lo_mask, fixed[2 * p], fixed[2 * p + 1]) for p in range(len(outs) // 2)]


L0_QA, L0_QB, L0_KB, L0_VB, L0_KA, L0_VA, L0_N = 0, 512, 1024, 1536, 2048, 2176, 2304


def _ctx0_kernel(sink_ref, qkv_ref, o_ref):
    lo = _lane_lo((1, LANES))
    blk = lambda base, j: qkv_ref[:, base + j * LANES:base + (j + 1) * LANES]
    k_a = blk(L0_KA, 0)
    vx_a = _with_ones(blk(L0_VA, 0))
    problems = []
    for g in range(2):
        q_rows = _gqa_rows([blk(L0_QA, 2 * g), blk(L0_QA, 2 * g + 1)], g, lo)
        problems.append((q_rows, 4, [k_a], [vx_a], None, [sink_ref[4 * g + idx] for idx in range(4)]))
    for j in range(4):
        problems.append((_stack_halves(blk(L0_QB, j), lo), 2, [blk(L0_KB, j)], [_with_ones(blk(L0_VB, j))],
                         None, None))
    results = _attend_many(problems)
    for g in range(2):
        for p, o in enumerate(_gqa_merge(results[g], g, lo)):
            j = 2 * g + p
            o_ref[:, j * LANES:(j + 1) * LANES] = o.astype(bf16)
    for j in range(4):
        outs = results[2 + j]
        o_ref[:, 512 + j * LANES:512 + (j + 1) * LANES] = jnp.where(lo, outs[0], outs[1]).astype(bf16)


def _ctx0(sink, qkv):
    return pl.pallas_call(
        _ctx0_kernel,
        grid=(N_PROMPT_BATCH,),
        in_specs=[
            pl.BlockSpec(memory_space=pltpu.SMEM),
            pl.BlockSpec((PROMPT_SEQ, L0_N), lambda b: (b, 0)),
        ],
        out_specs=pl.BlockSpec((PROMPT_SEQ, D_MODEL), lambda b: (b, 0)),
        out_shape=jax.ShapeDtypeStruct((N_PROMPT, D_MODEL), bf16),
        compiler_params=_cparams(("parallel",)),
        name="ctx0",
    )(sink, qkv)


WIN_KEYS = 3 * QB
NA_KEY_ROWS = 10
NA_KEYS = NA_KEY_ROWS * GRID_W
N_QB = SAMPLE_SEQ // QB
N_NA_PATTERNS = 5
_PROMPT_QBLOCKS = N_PROMPT // QB


def _na_pattern(n):
    return jnp.where(n < 2, n, jnp.where(n > N_QB - 3, n - (N_QB - 5), 2))


def _lat0_kernel(sink_ref, roff_ref, q_ref, kvb_ref, kva_ref, cak_ref, cav_ref, cbk_ref, cbv_ref, tiles_ref, o_ref):
    n = pl.program_id(1)
    lo = _lane_lo((1, LANES))
    kstart = pl.multiple_of(jnp.clip((n - 1) * QB, 0, SAMPLE_SEQ - WIN_KEYS), QB)
    k_a = kva_ref[pl.ds(kstart, WIN_KEYS), 0:LANES]
    v_a = kva_ref[pl.ds(kstart, WIN_KEYS), LANES:2 * LANES]
    c_k = cak_ref[...].astype(bf16)
    keys_a = [c_k, k_a]
    vx_a = [_with_ones(cav_ref[...].astype(bf16)), _with_ones(v_a)]
    qpos = n * QB + lax.broadcasted_iota(i32, (QB, WIN_KEYS), 0)
    kpos = kstart + lax.broadcasted_iota(i32, (QB, WIN_KEYS), 1)
    in_window = jnp.abs(qpos - kpos) <= A_WINDOW
    mask_window = lambda h, i, s: jnp.where(in_window, s, NEG) if i == 1 else s
    problems = []
    for g in range(2):
        q_rows = _gqa_rows([q_ref[:, L0_QA + j * LANES:L0_QA + (j + 1) * LANES] for j in (2 * g, 2 * g + 1)], g, lo)
        problems.append((q_rows, 4, keys_a, vx_a, mask_window, [sink_ref[4 * g + idx] for idx in range(4)]))
    krow = jnp.clip(2 * n - NA_ROWS // 2, 0, SAMPLE_SEQ // GRID_W - NA_KEY_ROWS)
    ktok = pl.multiple_of(krow * GRID_W, QB)
    pattern = _na_pattern(n)

    def na_bias(head):
        rows = []
        for rq in range(QB // GRID_W):
            blocks = []
            for kb in range(NA_KEY_ROWS // 2):
                d0, d1 = (roff_ref[(pattern * 2 + rq) * NA_KEY_ROWS + 2 * kb + t] for t in range(2))
                blocks.append(jnp.where(lo, tiles_ref[head, d0], tiles_ref[head, d1]))
            rows.append(jnp.concatenate(blocks, axis=1))
        return jnp.concatenate(rows, axis=0)

    for j in range(4):
        q_b = q_ref[:, L0_QB + j * LANES:L0_QB + (j + 1) * LANES]
        k_b = kvb_ref[pl.ds(ktok, NA_KEYS), j * LANES:(j + 1) * LANES]
        v_b = kvb_ref[pl.ds(ktok, NA_KEYS), 512 + j * LANES:512 + (j + 1) * LANES]
        cb_k = cbk_ref[:, j * LANES:(j + 1) * LANES].astype(bf16)
        cb_v = cbv_ref[:, j * LANES:(j + 1) * LANES].astype(bf16)
        add_bias = lambda h, i, s, j=j: s + na_bias(2 * j + h) if i == 1 else s
        problems.append((_stack_halves(q_b, lo), 2, [cb_k, k_b], [_with_ones(cb_v), _with_ones(v_b)], add_bias, None))

    results = _attend_many(problems)
    for g in range(2):
        for p, o in enumerate(_gqa_merge(results[g], g, lo)):
            j = 2 * g + p
            o_ref[:, j * LANES:(j + 1) * LANES] = o.astype(bf16)
    for j in range(4):
        outs = results[2 + j]
        o_ref[:, 512 + j * LANES:512 + (j + 1) * LANES] = jnp.where(lo, outs[0], outs[1]).astype(bf16)


def _lat0(sink, qkv, cak, cav, cbk, cbv, bias_tiles):
    sb = N_PROMPT // SAMPLE_SEQ
    return pl.pallas_call(
        _lat0_kernel,
        grid=(N_SAMPLE_BATCH, N_QB),
        in_specs=[
            pl.BlockSpec(memory_space=pltpu.SMEM),
            pl.BlockSpec(memory_space=pltpu.SMEM),
            pl.BlockSpec((QB, 1024), lambda b, n: (_PROMPT_QBLOCKS + b * N_QB + n, 0)),
            pl.BlockSpec((SAMPLE_SEQ, 1024), lambda b, n: (sb + b, 1)),
            pl.BlockSpec((SAMPLE_SEQ, 256), lambda b, n: (sb + b, L0_KA // 256)),
            pl.BlockSpec((None, PAST_LEN, LANES), lambda b, n: (b, 0, 0)),
            pl.BlockSpec((None, PAST_LEN, LANES), lambda b, n: (b, 0, 0)),
            pl.BlockSpec((None, PAST_LEN, 512), lambda b, n: (b, 0, 0)),
            pl.BlockSpec((None, PAST_LEN, 512), lambda b, n: (b, 0, 0)),
            pl.BlockSpec((8, N_ROW_OFFSETS + 1, GRID_W, LANES), lambda b, n: (0, 0, 0, 0)),
        ],
        out_specs=pl.BlockSpec((QB, D_MODEL), lambda b, n: (b * N_QB + n, 0)),
        out_shape=jax.ShapeDtypeStruct((N_SAMPLE, D_MODEL), bf16),
        compiler_params=_cparams(("parallel", "arbitrary")),
        name="lat0",
    )(sink, jnp.asarray(_na_row_offsets()), qkv, qkv, qkv, cak, cav, cbk, cbv, bias_tiles)


N_ROW_OFFSETS = 2 * NA_ROWS - 1


def _na_row_offsets():
    rows = SAMPLE_SEQ // GRID_W
    idx = np.full((N_NA_PATTERNS, 2, NA_KEY_ROWS), N_ROW_OFFSETS, np.int32)
    for p, n in enumerate((0, 1, 2, N_QB - 2, N_QB - 1)):
        k0 = int(np.clip(2 * n - NA_ROWS // 2, 0, rows - NA_KEY_ROWS))
        for rq in range(2):
            r = 2 * n + rq
            rs = int(np.clip(r - NA_ROWS // 2, 0, rows - NA_ROWS))
            for kl in range(NA_KEY_ROWS):
                if rs <= k0 + kl < rs + NA_ROWS:
                    idx[p, rq, kl] = k0 + kl - r + NA_ROWS - 1
    return idx.reshape(-1)


def _na_bias_tiles(rel_bias):
    n_dc = 2 * NA_COLS - 1
    c = np.arange(GRID_W)[:, None]
    kc = np.arange(GRID_W)[None, :]
    cs = np.clip(c - NA_COLS // 2, 0, GRID_W - NA_COLS)
    col_ok = (kc >= cs) & (kc < cs + NA_COLS)
    col_hot = ((kc - c + NA_COLS - 1)[None] == np.arange(n_dc)[:, None, None]) & col_ok[None]
    hp = lax.Precision.HIGHEST
    tiles = jnp.einsum("hdx,xck->hdck", rel_bias.astype(f32), col_hot.astype(np.float32), precision=hp)
    tiles = tiles + np.where(col_ok, 0.0, NEG).astype(np.float32)
    tiles = jnp.concatenate([tiles, jnp.full((tiles.shape[0], 1, GRID_W, GRID_W), NEG, f32)], axis=1)
    return jnp.concatenate([tiles, tiles], axis=-1)


def _diff_lambda(lam_ref, lam_init):
    lv = lam_ref[...]
    s1 = jnp.sum(lv[0:1, :] * lv[1:2, :], axis=-1, keepdims=True)
    s2 = jnp.sum(lv[2:3, :] * lv[3:4, :], axis=-1, keepdims=True)
    return jnp.exp(s1) - jnp.exp(s2) + lam_init


def _diff_heads(heads, o_ref, lam, subln, lo, lam_init):
    for p0 in range(0, len(heads), DIFF_HEADS_PER_PASS):
        group = heads[p0:p0 + DIFF_HEADS_PER_PASS]
        results = _attend_many([(_stack_halves(q, lo), 2, ks, [_with_ones(v) for v in vs], None, None)
                                for _, q, ks, vs in group])
        for (h, _, _, _), (o1, o2) in zip(group, results):
            o = o1 - lam * o2
            o = o * lax.rsqrt(jnp.mean(o * o, axis=-1, keepdims=True) + EPS)
            o_ref[:, h * LANES:(h + 1) * LANES] = ((o * subln) * (1.0 - lam_init)).astype(bf16)


def _ctx1_kernel(lam_ref, subln_ref, qkv_ref, o_ref, *, lam_init):
    lo = _lane_lo((1, LANES))
    lam = _diff_lambda(lam_ref, lam_init)
    blk = lambda base, h: qkv_ref[:, base + h * LANES:base + (h + 1) * LANES]
    heads = [(h, blk(0, h), [blk(D_MODEL, h)], [blk(2 * D_MODEL, h)]) for h in range(8)]
    _diff_heads(heads, o_ref, lam, subln_ref[...], lo, lam_init)


def _ctx1(lamv, subln, qkv, lam_init):
    return pl.pallas_call(
        functools.partial(_ctx1_kernel, lam_init=lam_init),
        grid=(N_PROMPT_BATCH,),
        in_specs=[
            pl.BlockSpec((8, HEAD_DIM), lambda b: (0, 0)),
            pl.BlockSpec((1, LANES), lambda b: (0, 0)),
            pl.BlockSpec((PROMPT_SEQ, 3 * D_MODEL), lambda b: (b, 0)),
        ],
        out_specs=pl.BlockSpec((PROMPT_SEQ, D_MODEL), lambda b: (b, 0)),
        out_shape=jax.ShapeDtypeStruct((N_PROMPT, D_MODEL), bf16),
        compiler_params=_cparams(("parallel",)),
        name="ctx1",
    )(lamv, subln, qkv)


def _lat1_kernel(lam_ref, subln_ref, q_ref, k_ref, v_ref, ck_ref, cv_ref, o_ref, *, lam_init):
    lo = _lane_lo((1, LANES))
    lam = _diff_lambda(lam_ref, lam_init)
    heads = []
    for h in range(8):
        sl = slice(h * LANES, (h + 1) * LANES)
        heads.append((h, q_ref[:, sl], [ck_ref[:, sl].astype(bf16), k_ref[:, sl]],
                      [cv_ref[:, sl].astype(bf16), v_ref[:, sl]]))
    _diff_heads(heads, o_ref, lam, subln_ref[...], lo, lam_init)


def _lat1(lamv, subln, qkv, ck, cv, lam_init):
    sb = N_PROMPT // SAMPLE_SEQ
    nq = SAMPLE_SEQ // TM
    return pl.pallas_call(
        functools.partial(_lat1_kernel, lam_init=lam_init),
        grid=(N_SAMPLE_BATCH, nq),
        in_specs=[
            pl.BlockSpec((8, HEAD_DIM), lambda b, n: (0, 0)),
            pl.BlockSpec((1, LANES), lambda b, n: (0, 0)),
            pl.BlockSpec((TM, D_MODEL), lambda b, n: (N_PROMPT_TILES + b * nq + n, 0)),
            pl.BlockSpec((SAMPLE_SEQ, D_MODEL), lambda b, n: (sb + b, 1)),
            pl.BlockSpec((SAMPLE_SEQ, D_MODEL), lambda b, n: (sb + b, 2)),
            pl.BlockSpec((None, PAST_LEN, D_MODEL), lambda b, n: (b, 0, 0)),
            pl.BlockSpec((None, PAST_LEN, D_MODEL), lambda b, n: (b, 0, 0)),
        ],
        out_specs=pl.BlockSpec((TM, D_MODEL), lambda b, n: (b * nq + n, 0)),
        out_shape=jax.ShapeDtypeStruct((N_SAMPLE, D_MODEL), bf16),
        compiler_params=_cparams(("parallel", "arbitrary")),
        name="lat1",
    )(lamv, subln, qkv, qkv, qkv, ck, cv)


def _split_bf16(a):
    hi = a.astype(bf16)
    return hi, (a - hi.astype(f32)).astype(bf16)


def _route_kernel(*refs, n_x):
    x_refs = refs[:n_x]
    (op_ref, os_ref, mod_ref, g_ref, wo_ref, rwt_ref, rb_ref,
     xnew_ref, xloc_ref, slots_ref, gate_ref, len_ref) = refs[n_x:]
    logits = [_route_logits(t, x_refs, op_ref, os_ref, mod_ref, g_ref, wo_ref, rwt_ref, xnew_ref)
              for t in range(ROUTE_TILES)]
    tiles = _route_tiles(jnp.concatenate([lg for _, lg in logits], axis=1), rb_ref, slots_ref, gate_ref, len_ref)
    for t, (slots, run_len) in enumerate(tiles):
        _route_dispatch(t, slots, logits[t][0], run_len, xloc_ref)


def _route_logits(t, x_refs, op_ref, os_ref, mod_ref, g_ref, wo_ref, rwt_ref, xnew_ref):
    is_prompt = pl.program_id(0) < N_PROMPT_TILES // ROUTE_TILES
    rows = slice(t * TM, (t + 1) * TM)
    attn = jnp.where(is_prompt, op_ref[rows, :], os_ref[rows, :])
    x_in = x_refs[0][rows, :] if len(x_refs) == 1 else jnp.where(is_prompt, x_refs[0][rows, :], x_refs[1][rows, :])
    x = x_in + mod_ref[2:3, :] * _dot(attn, wo_ref[...])
    xnew_ref[rows, :] = x
    h = _norm_mod(x, g_ref[...], mod_ref[4:5, :], mod_ref[3:4, :])
    h_hi, h_lo = _split_bf16(h)
    w_hi, w_lo = _split_bf16(rwt_ref[...])
    return h_hi, _dot_nt(w_hi, h_hi) + (_dot_nt(w_hi, h_lo) + _dot_nt(w_lo, h_hi))


def _route_tiles(logits, rb_ref, slots_ref, gate_ref, len_ref):
    ng, ge = N_GROUPS, N_EXPERTS // N_GROUPS
    n = ROUTE_TILES * TM
    tile = lambda a, t: a[..., t * TM:(t + 1) * TM]
    scores = jax.nn.sigmoid(logits)
    biased = scores + rb_ref[...]
    s3 = scores.reshape(ng, ge, n)
    b3 = biased.reshape(ng, ge, n)
    in_group = lax.broadcasted_iota(i32, (ng, ge, n), 1).astype(f32)
    group_id = lax.broadcasted_iota(i32, (ng, 1, n), 0).astype(f32)
    expert_id = lax.broadcasted_iota(i32, (ng, ge, n), 0).astype(f32) * ge + in_group

    def max01(a):
        return jnp.max(jnp.max(a, axis=0, keepdims=True), axis=1, keepdims=True)

    def min01(a):
        return jnp.min(jnp.min(a, axis=0, keepdims=True), axis=1, keepdims=True)

    def sum01(a):
        return jnp.sum(jnp.sum(a, axis=0, keepdims=True), axis=1, keepdims=True)

    m1 = jnp.max(b3, axis=1, keepdims=True)
    first = jnp.min(jnp.where(b3 == m1, in_group, ge), axis=1, keepdims=True)
    m2 = jnp.max(jnp.where(in_group == first, -jnp.inf, b3), axis=1, keepdims=True)
    gscore = m1 + m2
    gsel = jnp.zeros((ng, 1, n), f32)
    for _ in range(TOPK_GROUPS):
        gm = jnp.max(gscore, axis=0, keepdims=True)
        gi = jnp.min(jnp.where(gscore == gm, group_id, ng), axis=0, keepdims=True)
        hit = group_id == gi
        gsel = jnp.where(hit, 1.0, gsel)
        gscore = jnp.where(hit, -jnp.inf, gscore)
    cand = jnp.where(jnp.broadcast_to(gsel, (ng, ge, n)) > 0.0, b3, -jnp.inf)
    top_e, top_w = [], []
    for _ in range(TOP_K):
        em = max01(cand)
        ei = min01(jnp.where(cand == em, expert_id, N_EXPERTS))
        hit = expert_id == ei
        top_e.append(ei)
        top_w.append(sum01(jnp.where(hit, s3, 0.0)))
        cand = jnp.where(hit, -jnp.inf, cand)
    wsum = functools.reduce(lambda a, b: a + b, top_w)
    sel3 = jnp.zeros((ng, ge, n), f32)
    for k, (ei, w) in enumerate(zip(top_e, top_w)):
        gate = (w / wsum * ROUTED_SCALE).reshape(1, n)
        for t in range(ROUTE_TILES):
            gate_ref[t, k:k + 1, :] = tile(gate, t)
        sel3 = jnp.where(expert_id == ei, 1.0, sel3)
    sel = sel3.reshape(N_EXPERTS, n)

    r_i = lax.broadcasted_iota(i32, (N_EXPERTS, N_EXPERTS), 0)
    c_i = lax.broadcasted_iota(i32, (N_EXPERTS, N_EXPERTS), 1)
    lower = jnp.where(c_i < r_i, 1.0, 0.0).astype(bf16)
    run_lens, run_offs = [], []
    for t in range(ROUTE_TILES):
        cnt = jnp.sum(tile(sel, t), axis=1, keepdims=True)
        run_len = jnp.ceil(cnt * (1.0 / CHUNK)) * CHUNK
        run_off = _dot(lower, jnp.broadcast_to(run_len, (N_EXPERTS, LANES)).astype(bf16))[:, 0:1]
        run_lens.append(run_len)
        run_offs.append(jnp.broadcast_to(run_off, (N_EXPERTS, TM)))
    t_r = lax.broadcasted_iota(i32, (n, n), 0)
    t_c = lax.broadcasted_iota(i32, (n, n), 1)
    before = jnp.where(jnp.logical_and(t_r < t_c, t_r // TM == t_c // TM), 1.0, 0.0).astype(bf16)
    rank = _dot(sel.astype(bf16), before)
    slot3 = (jnp.concatenate(run_offs, axis=1) + rank).reshape(ng, ge, n)
    slots = [sum01(jnp.where(expert_id == ei, slot3, 0.0)).reshape(1, n).astype(i32) for ei in top_e]
    out = []
    for t in range(ROUTE_TILES):
        for k in range(TOP_K):
            slots_ref[t, k:k + 1, :] = tile(slots[k], t)
        slots_ref[t, TOP_K:8, :] = jnp.full((8 - TOP_K, TM), -1, i32)
        gate_ref[t, TOP_K:8, :] = jnp.zeros((8 - TOP_K, TM), f32)
        len_ref[t] = jnp.broadcast_to(run_lens[t], (N_EXPERTS, LANES)).astype(i32)
        out.append(([tile(sl, t) for sl in slots], run_lens[t]))
    return out


def _route_dispatch(t, slots, h_hi, run_len, xloc_ref):
    rows = ROUTE_ROWS

    def body(c, carry):
        base = pl.multiple_of(c * rows, rows)
        row_id = base.astype(jnp.int16) + lax.broadcasted_iota(jnp.int16, (rows, TM), 0)
        p = jnp.zeros((rows, TM), bf16)
        for k in range(TOP_K):
            p = jnp.where(row_id == slots[k].astype(jnp.int16), jnp.ones((), bf16), p)
        xloc_ref[pl.ds(t * SLOTS + base, rows), :] = _dot(p, h_hi).astype(bf16)
        return carry

    def zero_body(c, carry):
        base = pl.multiple_of(c * rows, rows)
        xloc_ref[pl.ds(t * SLOTS + base, rows), :] = jnp.zeros((rows, D_MODEL), bf16)
        return carry

    n_used = (jnp.sum(run_len).astype(i32) + (rows - 1)) // rows
    lax.fori_loop(0, n_used, body, 0)
    lax.fori_loop(n_used, SLOTS // rows, zero_body, 0)


def _route(x_parts, o_prompt, o_sample, mod_l, g, w_out, rwt, rb):
    per_tile = lambda i: (i, 0, 0)
    rt = ROUTE_TILES
    tm = rt * TM
    return pl.pallas_call(
        functools.partial(_route_kernel, n_x=len(x_parts)),
        grid=(N_TILES // rt,),
        in_specs=_x_specs(x_parts, tm) + [
            pl.BlockSpec((tm, D_MODEL), lambda i: _prompt_block(i, tm)),
            pl.BlockSpec((tm, D_MODEL), lambda i: _sample_block(i, tm)),
            pl.BlockSpec((None, 6, D_MODEL), lambda i: (_mod_row(i, tm), 0, 0)),
            pl.BlockSpec((1, D_MODEL), lambda i: (0, 0)),
            pl.BlockSpec((D_MODEL, D_MODEL), lambda i: (0, 0)),
            pl.BlockSpec((N_EXPERTS, D_MODEL), lambda i: (0, 0)),
            pl.BlockSpec((N_EXPERTS, 1), lambda i: (0, 0)),
        ],
        out_specs=[
            pl.BlockSpec((tm, D_MODEL), lambda i: (i, 0)),
            pl.BlockSpec((rt * SLOTS, D_MODEL), lambda i: (i, 0)),
            pl.BlockSpec((rt, 8, TM), per_tile),
            pl.BlockSpec((rt, 8, TM), per_tile),
            pl.BlockSpec((rt, N_EXPERTS, LANES), per_tile),
        ],
        out_shape=[
            jax.ShapeDtypeStruct((N_TOK, D_MODEL), f32),
            jax.ShapeDtypeStruct((N_TILES * SLOTS, D_MODEL), bf16),
            jax.ShapeDtypeStruct((N_TILES, 8, TM), i32),
            jax.ShapeDtypeStruct((N_TILES, 8, TM), f32),
            jax.ShapeDtypeStruct((N_TILES, N_EXPERTS, LANES), i32),
        ],
        compiler_params=_cparams(("parallel",)),
        name="route",
    )(*x_parts, o_prompt, o_sample, mod_l, g, w_out, rwt, rb)


def _moe_plan(run_len):
    nt, ne = run_len.shape

    def excl_cumsum(a):
        n = a.shape[-1]
        earlier = np.arange(n)[None, :] < np.arange(n)[:, None]
        return jnp.sum(jnp.where(earlier, a[..., None, :], 0), axis=-1)

    def first_diff(a):
        return a - jnp.concatenate([jnp.zeros_like(a[..., :1]), a[..., :-1]], axis=-1)

    off_loc = excl_cumsum(run_len)
    before = excl_cumsum(run_len.T).T
    n_e = jnp.sum(run_len, axis=0)
    n_pad = -(-n_e // GM) * GM
    g_start = excl_cumsum(n_pad)
    total = jnp.sum(n_pad)
    run_dst = g_start[None, :] + before
    run_src = jnp.arange(nt, dtype=i32)[:, None] * SLOTS + off_loc
    dst_f = run_dst.T.reshape(-1)
    shift_f = first_diff((run_src - run_dst).T.reshape(-1))
    rows = jnp.arange((G_TILES + GMM_ITEM_TILES - 1) * G_CHUNKS, dtype=i32) * CHUNK
    shift = jnp.sum(jnp.where(dst_f[None, :] <= rows[:, None], shift_f[None, :], 0), axis=1)
    in_run = jnp.any((g_start[None, :] <= rows[:, None]) & (rows[:, None] < (g_start + n_e)[None, :]), axis=1)
    chunk_src = (jnp.where(in_run, rows + shift, 0) // CHUNK).astype(i32)
    loc_rows = jnp.arange(SLOT_CHUNKS, dtype=i32) * CHUNK
    shift_l = first_diff(run_dst - off_loc)
    shift = jnp.sum(jnp.where(off_loc[:, None, :] <= loc_rows[None, :, None], shift_l[:, None, :], 0), axis=2)
    used = jnp.sum(run_len, axis=1)
    chunk_map = jnp.where(loc_rows[None, :] < used[:, None], (loc_rows[None, :] + shift) // CHUNK, 0).astype(i32)
    tile_start, n_tiles = g_start // GM, n_pad // GM
    n_items = -(-n_tiles // GMM_ITEM_TILES)
    item_start = excl_cumsum(n_items)
    items = jnp.arange(GMM_MAX_ITEMS, dtype=i32)
    owner = items[:, None] >= item_start[None, :]
    e_first = jnp.sum(jnp.where(owner, first_diff(tile_start - GMM_ITEM_TILES * item_start)[None, :], 0), axis=1)
    item_tile = e_first + GMM_ITEM_TILES * items
    e_end = jnp.sum(jnp.where(owner, first_diff(tile_start + n_tiles)[None, :], 0), axis=1)
    item_cnt = jnp.clip(e_end - item_tile, 0, GMM_ITEM_TILES)
    gmm_plan = tuple(a.astype(i32) for a in (item_start, n_items, item_tile, item_cnt, chunk_src))
    return gmm_plan, chunk_map.reshape(-1)


def _gmm_in_copy(xloc_hbm, xbuf, sem, src_chunk, slot, c):
    return pltpu.make_async_copy(xloc_hbm.at[src_chunk], xbuf.at[slot, c], sem.at[slot])


def _gmm_out_copy(ybuf, y_hbm, sem, tile, slot, n_tiles):
    chunks = n_tiles * G_CHUNKS
    return pltpu.make_async_copy(ybuf.at[slot, pl.ds(0, chunks)],
                                 y_hbm.at[pl.ds(tile * G_CHUNKS, chunks)], sem.at[slot])


def _gmm_kernel(i0_ref, ni_ref, it_ref, ic_ref, cs_ref, xloc_hbm, wg_ref, wu_ref, wd_ref, y_hbm,
                xbuf, ybuf, zbuf, wg_b, wu_b, wd_b, in_sem, out_sem, zsem):
    e = pl.program_id(0)
    last = pl.num_programs(0) - 1
    n_items = ni_ref[e]
    first_item = i0_ref[e]
    total_items = i0_ref[last] + ni_ref[last]
    last_item = total_items - 1
    total_tiles = it_ref[last_item] + ic_ref[last_item]

    def start_in(item):
        first = it_ref[item] * G_CHUNKS
        for c in range(GMM_ITEM_TILES * G_CHUNKS):
            _gmm_in_copy(xloc_hbm, xbuf, in_sem, cs_ref[first + c], item % GMM_SLOTS, c).start()

    def wait_in(item):
        for c in range(GMM_ITEM_TILES * G_CHUNKS):
            _gmm_in_copy(xloc_hbm, xbuf, in_sem, 0, item % GMM_SLOTS, c).wait()

    def out_copy(item, fn):
        for cnt in range(1, GMM_ITEM_TILES + 1):
            @pl.when(ic_ref[item] == cnt)
            def _():
                fn(_gmm_out_copy(ybuf, y_hbm, out_sem, it_ref[item], item % GMM_SLOTS, cnt))

    @pl.when(e == 0)
    def _():
        for item in range(GMM_SLOTS - 1):
            start_in(item)
        zbuf[...] = jnp.zeros(zbuf.shape, zbuf.dtype)

    def tail_copies(fn):
        for j in range(_GMM_TAIL_PER_STEP):
            tile = total_tiles + e + j * N_EXPERTS

            @pl.when(tile < G_TILES)
            def _():
                fn(pltpu.make_async_copy(zbuf, y_hbm.at[pl.ds(tile * G_CHUNKS, G_CHUNKS)], zsem.at[0]))

    tail_copies(lambda cp: cp.start())

    @pl.when(n_items > 0)
    def _():
        wg_b[...] = wg_ref[...].astype(bf16)
        wu_b[...] = wu_ref[...].astype(bf16)
        wd_b[...] = wd_ref[...].astype(bf16)

    def body(j, carry):
        item = first_item + j
        slot = item % GMM_SLOTS

        @pl.when(item + (GMM_SLOTS - 1) < total_items)
        def _():
            start_in(item + (GMM_SLOTS - 1))

        wait_in(item)

        @pl.when(item >= GMM_SLOTS)
        def _():
            out_copy(item - GMM_SLOTS, lambda cp: cp.wait())

        for cnt in range(1, GMM_ITEM_TILES + 1):
            @pl.when(ic_ref[item] == cnt)
            def _():
                rows = cnt * GM
                chunks = cnt * G_CHUNKS
                x = xbuf[slot, 0:chunks].reshape(rows, D_MODEL)
                act = _silu(_dot(x, wg_b[...])) * _dot(x, wu_b[...])
                y = _dot(act.astype(bf16), wd_b[...]).astype(bf16)
                ybuf[slot, 0:chunks] = y.reshape(chunks, CHUNK, D_MODEL)

        out_copy(item, lambda cp: cp.start())
        return carry

    lax.fori_loop(0, n_items, body, 0)
    tail_copies(lambda cp: cp.wait())

    @pl.when(e == last)
    def _():
        for back in range(1, GMM_SLOTS + 1):
            out_copy(total_items - back, lambda cp: cp.wait())


def _gmm(plan, xloc, wg, wu, wd, layer):
    rows = GMM_ITEM_TILES * GM
    w_idx = lambda e, *_: (layer, e, 0, 0)
    grid_spec = pltpu.PrefetchScalarGridSpec(
        num_scalar_prefetch=5,
        grid=(N_EXPERTS,),
        in_specs=[
            pl.BlockSpec(memory_space=pl.ANY),
            pl.BlockSpec((None, None, D_MODEL, EXPERT_DIM), w_idx),
            pl.BlockSpec((None, None, D_MODEL, EXPERT_DIM), w_idx),
            pl.BlockSpec((None, None, EXPERT_DIM, D_MODEL), w_idx),
        ],
        out_specs=pl.BlockSpec(memory_space=pl.ANY),
        scratch_shapes=[pltpu.VMEM((GMM_SLOTS, rows // CHUNK, CHUNK, D_MODEL), bf16),
                        pltpu.VMEM((GMM_SLOTS, rows // CHUNK, CHUNK, D_MODEL), bf16),
                        pltpu.VMEM((G_CHUNKS, CHUNK, D_MODEL), bf16),
                        pltpu.VMEM((D_MODEL, EXPERT_DIM), bf16), pltpu.VMEM((D_MODEL, EXPERT_DIM), bf16),
                        pltpu.VMEM((EXPERT_DIM, D_MODEL), bf16),
                        pltpu.SemaphoreType.DMA((GMM_SLOTS,)), pltpu.SemaphoreType.DMA((GMM_SLOTS,)),
                        pltpu.SemaphoreType.DMA((1,))],
    )
    return pl.pallas_call(
        _gmm_kernel,
        grid_spec=grid_spec,
        out_shape=jax.ShapeDtypeStruct((G_TILES * G_CHUNKS, CHUNK, D_MODEL), bf16),
        compiler_params=_cparams(("arbitrary",)),
        name="gmm",
    )(*plan, xloc, wg, wu, wd)


def _combine_copy(y_hbm, ybuf, sem, sorted_chunk, slot, c):
    return pltpu.make_async_copy(y_hbm.at[sorted_chunk], ybuf.at[slot, c], sem.at[slot])


def _combine_kernel(cm_ref, y_hbm, slots_ref, gate_ref, x_ref, mod_ref, g_ref, sg_ref, su_ref, sd_ref, *rest, final):
    if final:
        gf_ref, yp_ref, ys_ref, ybuf, sem = rest
    else:
        o_ref, ybuf, sem = rest
    i = pl.program_id(0)
    n = pl.num_programs(0)
    ahead = COMBINE_SLOTS - 1
    slot = i % COMBINE_SLOTS

    def start(tile, s):
        for c in range(SLOT_CHUNKS):
            _combine_copy(y_hbm, ybuf, sem, cm_ref[tile * SLOT_CHUNKS + c], s, c).start()

    def wait(s):
        for c in range(SLOT_CHUNKS):
            _combine_copy(y_hbm, ybuf, sem, 0, s, c).wait()

    @pl.when(i == 0)
    def _():
        for tile in range(ahead):
            start(tile, tile)

    wait(slot)
    start((i + ahead) % n, (i + ahead) % COMBINE_SLOTS)

    x = x_ref[...]
    hb = _norm_mod(x, g_ref[...], mod_ref[4:5, :], mod_ref[3:4, :]).astype(bf16)
    shared = _dot((_silu(_dot(hb, sg_ref[...])) * _dot(hb, su_ref[...])).astype(bf16), sd_ref[...])
    row_id = lax.broadcasted_iota(jnp.int16, (SLOTS, TM), 0)
    p = jnp.zeros((SLOTS, TM), bf16)
    for k in range(TOP_K):
        p = jnp.where(row_id == slots_ref[k:k + 1, :].astype(jnp.int16), gate_ref[k:k + 1, :].astype(bf16), p)
    routed = lax.dot_general(p, ybuf[slot].reshape(SLOTS, D_MODEL), (((0,), (0,)), ((), ())),
                             preferred_element_type=f32)
    out = x + mod_ref[5:6, :] * (routed + shared)
    if final:
        y = (out * lax.rsqrt(jnp.mean(out * out, axis=-1, keepdims=True) + EPS)) * gf_ref[...]

        @pl.when(i < N_PROMPT_TILES)
        def _():
            yp_ref[...] = y

        @pl.when(i >= N_PROMPT_TILES)
        def _():
            ys_ref[...] = y
    else:
        o_ref[...] = out

    @pl.when(i == n - 1)
    def _():
        for k in range(1, ahead + 1):
            wait((i + k) % COMBINE_SLOTS)


def _combine(chunk_map, y, slots, gates, x, mod_l, g, sg, su, sd, final_g=None):
    shd = sg.shape[1]
    final = final_g is not None
    row_spec = pl.BlockSpec((TM, D_MODEL), lambda i, cm: (i, 0))
    vec_spec = pl.BlockSpec((1, D_MODEL), lambda i, cm: (0, 0))
    if final:
        out_specs = [pl.BlockSpec((TM, D_MODEL), lambda i, cm: _prompt_block(i)),
                     pl.BlockSpec((TM, D_MODEL), lambda i, cm: _sample_block(i))]
        out_shape = [jax.ShapeDtypeStruct((N_PROMPT, D_MODEL), f32), jax.ShapeDtypeStruct((N_SAMPLE, D_MODEL), f32)]
    else:
        out_specs, out_shape = row_spec, jax.ShapeDtypeStruct((N_TOK, D_MODEL), f32)
    grid_spec = pltpu.PrefetchScalarGridSpec(
        num_scalar_prefetch=1,
        grid=(N_TILES,),
        in_specs=[
            pl.BlockSpec(memory_space=pl.ANY),
            pl.BlockSpec((None, 8, TM), lambda i, cm: (i, 0, 0)),
            pl.BlockSpec((None, 8, TM), lambda i, cm: (i, 0, 0)),
            row_spec,
            pl.BlockSpec((None, 6, D_MODEL), lambda i, cm: (_mod_row(i), 0, 0)),
            vec_spec,
            pl.BlockSpec((D_MODEL, shd), lambda i, cm: (0, 0)),
            pl.BlockSpec((D_MODEL, shd), lambda i, cm: (0, 0)),
            pl.BlockSpec((shd, D_MODEL), lambda i, cm: (0, 0)),
        ] + ([vec_spec] if final else []),
        out_specs=out_specs,
        scratch_shapes=[pltpu.VMEM((COMBINE_SLOTS, SLOT_CHUNKS, CHUNK, D_MODEL), bf16),
                        pltpu.SemaphoreType.DMA((COMBINE_SLOTS,))],
    )
    args = (chunk_map, y, slots, gates, x, mod_l, g, sg, su, sd) + ((final_g,) if final else ())
    return pl.pallas_call(
        functools.partial(_combine_kernel, final=final),
        grid_spec=grid_spec,
        out_shape=out_shape,
        compiler_params=_cparams(("arbitrary",)),
        name="combine",
    )(*args)


def _moe(x_parts, o_prompt, o_sample, w_out, mod_l, g, rwt, rb, wg, wu, wd, layer, sg, su, sd, final_g=None):
    x, xloc, slots, gates, run_len = _route(x_parts, o_prompt, o_sample, mod_l, g, w_out, rwt, rb)
    gmm_plan, chunk_map = _moe_plan(run_len[:, :, 0])
    y = _gmm(gmm_plan, xloc.reshape(N_TILES * SLOT_CHUNKS, CHUNK, D_MODEL), wg, wu, wd, layer)
    return _combine(chunk_map, y, slots, gates, x, mod_l, g, sg, su, sd, final_g)


_L0_CHUNKS = (
    (0, 512, 0, (0, 1, 2, 3), ()),
    (512, 1024, 768, (), ()),
    (1024, 1536, 1280, (), ((2, 0, 512, 0, False),)),
    (1536, 2048, 1792, (), ((3, 0, 512, 0, False),)),
    (2048, 2304, 512, (0,), ((0, 0, 128, 0, False), (1, 128, 256, 0, False))),
)
_L0_KV_OUTS = (("T", 128), ("T", 128), ("T", 512), ("T", 512))
_L1_CHUNKS = (
    (0, 512, 0, (0, 1, 2, 3), ()),
    (512, 1024, 512, (0, 1, 2, 3), ()),
    (1024, 1536, 1024, (0, 1, 2, 3), ((0, 0, 512, 0, False),)),
    (1536, 2048, 1536, (0, 1, 2, 3), ((0, 0, 512, 512, False),)),
    (2048, 2560, 2048, (), ((1, 0, 512, 0, True),)),
    (2560, 3072, 2560, (), ((1, 0, 512, 512, True),)),
)
_L1_KV_OUTS = (("T", 1024), ("H", 8))


def _from_feature_major(kt, *head_dims):
    nb, _, s = kt.shape
    nd = len(head_dims)
    return kt.reshape(nb, *head_dims, s).transpose(0, nd + 1, *range(1, nd + 1))[:, None]


def kernel(x_prompt, x_sample, cache_a_k, cache_a_v, cache_b_k, cache_b_v, cache_c_k, cache_c_v, c, c_ctx, w_mod, b_mod, norm_mix, norm_ffn, w_in_ab, w_out_ab, sink_a, rel_bias_b, w_in_c, w_out_c, lam_q1, lam_k1, lam_q2, lam_k2, subln_c, router_w, router_bias, exp_w_gate, exp_w_up, exp_w_down, sh_w_gate, sh_w_up, sh_w_down, final_norm):
    x = (x_prompt.reshape(N_PROMPT, D_MODEL), x_sample.reshape(N_SAMPLE, D_MODEL))
    cond8 = jnp.concatenate([c_ctx[None, :], c, jnp.zeros((8 - 1 - N_SAMPLE_BATCH, D_MODEL), f32)], axis=0)
    mod = _adaln(cond8, w_mod, b_mod).reshape(DEPTH, 8, 6, D_MODEL)
    rope_tabs = _rope_tables()
    new_kv = {}
    for layer in range(DEPTH):
        li = layer // 2
        mod_l = mod[layer]
        g_mix = norm_mix[layer][None, :]
        g_ffn = norm_ffn[layer][None, :]
        if layer % 2 == 0:
            w_in = w_in_ab[li].astype(bf16)
            qkv, ak, av, bk, bv = _inproj(x, mod_l, g_mix, w_in, rope_tabs, _L0_CHUNKS, _L0_KV_OUTS)
            new_kv["a_k"], new_kv["a_v"], new_kv["b_k"], new_kv["b_v"] = ak, av, bk, bv
            o_p = _ctx0(sink_a[li], qkv)
            o_s = _lat0(sink_a[li], qkv,
                        cache_a_k[:, li].reshape(N_SAMPLE_BATCH, PAST_LEN, LANES),
                        cache_a_v[:, li].reshape(N_SAMPLE_BATCH, PAST_LEN, LANES),
                        cache_b_k[:, li].reshape(N_SAMPLE_BATCH, PAST_LEN, 512),
                        cache_b_v[:, li].reshape(N_SAMPLE_BATCH, PAST_LEN, 512),
                        _na_bias_tiles(rel_bias_b[li]))
            w_out = w_out_ab[li].astype(bf16)
        else:
            lam_init = 0.8 - 0.6 * math.exp(-0.3 * layer)
            qkv, ck, cv = _inproj(x, mod_l, g_mix, w_in_c[li].astype(bf16), rope_tabs, _L1_CHUNKS, _L1_KV_OUTS)
            new_kv["c_k"], new_kv["c_v"] = ck, cv
            lamv = jnp.concatenate([lam_q1[li][None], lam_k1[li][None], lam_q2[li][None], lam_k2[li][None],
                                    jnp.zeros((4, HEAD_DIM), f32)], axis=0)
            subln = subln_c[li][None, :]
            o_p = _ctx1(lamv, subln, qkv, lam_init)
            o_s = _lat1(lamv, subln, qkv,
                        cache_c_k[:, li].reshape(N_SAMPLE_BATCH, PAST_LEN, D_MODEL),
                        cache_c_v[:, li].reshape(N_SAMPLE_BATCH, PAST_LEN, D_MODEL), lam_init)
            w_out = w_out_c[li].astype(bf16)
        last = layer == DEPTH - 1
        x = _moe(x, o_p, o_s, w_out, mod_l, g_ffn, router_w[layer].T, router_bias[layer][:, None],
                 exp_w_gate, exp_w_up, exp_w_down, layer,
                 sh_w_gate[layer].astype(bf16), sh_w_up[layer].astype(bf16), sh_w_down[layer].astype(bf16),
                 final_norm[None, :] if last else None)
        x = x if last else (x,)
    y_prompt, y_sample = x
    nb, s = N_PROMPT_BATCH, PROMPT_SEQ
    return (y_prompt.reshape(nb, s, D_MODEL), y_sample.reshape(N_SAMPLE_BATCH, SAMPLE_SEQ, D_MODEL),
            _from_feature_major(new_kv["a_k"], 2, HEAD_DIM), _from_feature_major(new_kv["a_v"], 2, HEAD_DIM),
            _from_feature_major(new_kv["b_k"], 8, HEAD_DIM), _from_feature_major(new_kv["b_v"], 8, HEAD_DIM),
            _from_feature_major(new_kv["c_k"], 8, 2, HEAD_DIM), new_kv["c_v"].reshape(nb, 1, s, 8, 2 * HEAD_DIM))
```

```python
import functools
import math

import jax
import jax.numpy as jnp
import numpy as np
from jax import lax
from jax.experimental import pallas as pl
from jax.experimental.pallas import tpu as pltpu

f32 = jnp.float32
bf16 = jnp.bfloat16
i32 = jnp.int32

D_MODEL = 1024
N_PROMPT_BATCH = 16
PROMPT_SEQ = 256
DEPTH = 2
N_SAMPLE_BATCH = 2
SAMPLE_SEQ = 2048
PAST_LEN = 512
GRID_W = 64
HEAD_DIM = 64
ROPE_THETA = 10000.0
EPS = 1e-6
A_WINDOW = 128
NA_ROWS = 8
NA_COLS = 16
N_EXPERTS = 64
TOP_K = 6
N_GROUPS = 8
TOPK_GROUPS = 4
EXPERT_DIM = 256
ROUTED_SCALE = 2.5
Q_SCALE = HEAD_DIM ** -0.5

N_PROMPT = N_PROMPT_BATCH * PROMPT_SEQ
N_SAMPLE = N_SAMPLE_BATCH * SAMPLE_SEQ
N_TOK = N_PROMPT + N_SAMPLE

LANES = 128
TM = 256
TD = 512
N_PROMPT_TILES = N_PROMPT // TM
N_TILES = N_TOK // TM
QB = 128
CHUNK = 16
SLOTS = -(-(TM * TOP_K + N_EXPERTS * (CHUNK - 1)) // 256) * 256
SLOT_CHUNKS = SLOTS // CHUNK
ROUTE_ROWS = 512
ROUTE_TILES = 2
COMBINE_SLOTS = 3
COMBINE_SURE_CHUNKS = 128
GM = 256
_MAX_SORTED = TM * TOP_K * N_TILES + N_TILES * N_EXPERTS * (CHUNK - 1) + N_EXPERTS * (GM - CHUNK)
G_TILES = -(-_MAX_SORTED // GM)
G_CHUNKS = GM // CHUNK
GMM_ITEM_TILES = 2
GMM_SLOTS = 4
GMM_MAX_ITEMS = (G_TILES + N_EXPERTS * (GMM_ITEM_TILES - 1)) // GMM_ITEM_TILES
_GMM_TAIL_PER_STEP = -(-(G_TILES - TM * TOP_K * N_TILES // GM) // N_EXPERTS)
ADA_COLS = 1536
DIFF_HEADS_PER_PASS = 2
VMEM_LIMIT = 56 * 1024 * 1024

NEG = -1e30


def _cparams(sem):
    return pltpu.CompilerParams(dimension_semantics=sem, vmem_limit_bytes=VMEM_LIMIT)


def _mod_row(i, tm=TM):
    return jnp.where(i < N_PROMPT // tm, 0, 1 + (i - N_PROMPT // tm) // (SAMPLE_SEQ // tm))


def _prompt_block(i, tm=TM):
    return (jnp.minimum(i, N_PROMPT // tm - 1), 0)


def _sample_block(i, tm=TM):
    return (jnp.maximum(i - N_PROMPT // tm, 0), 0)


def _x_specs(parts, tm=TM):
    if len(parts) == 1:
        return [pl.BlockSpec((tm, D_MODEL), lambda i, *_: (i, 0))]
    return [pl.BlockSpec((tm, D_MODEL), lambda i, *_: _prompt_block(i, tm)),
            pl.BlockSpec((tm, D_MODEL), lambda i, *_: _sample_block(i, tm))]


def _load_x(i, x_refs, tm=TM):
    if len(x_refs) == 1:
        return x_refs[0][...]
    return jnp.where(i < N_PROMPT // tm, x_refs[0][...], x_refs[1][...])


def _norm_mod(x, g, scale, shift):
    y = x * lax.rsqrt(jnp.mean(x * x, axis=-1, keepdims=True) + EPS)
    return (y * g) * (1.0 + scale) + shift


def _silu(x):
    return x * jax.nn.sigmoid(x)


def _dot(a, b):
    return jnp.dot(a, b, preferred_element_type=f32)


def _dot_nt(a, b):
    return lax.dot_general(a, b, (((1,), (1,)), ((), ())), preferred_element_type=f32)


def _adaln_kernel(cond_ref, w_ref, b_ref, o_ref):
    s = _silu(cond_ref[...]).astype(bf16)
    o_ref[...] = _dot(s, w_ref[...].astype(bf16)) + b_ref[...]


def _adaln(cond8, w_mod, b_mod):
    n6 = 6 * D_MODEL
    return pl.pallas_call(
        _adaln_kernel,
        grid=(DEPTH, n6 // ADA_COLS),
        in_specs=[
            pl.BlockSpec((8, D_MODEL), lambda l, j: (0, 0)),
            pl.BlockSpec((None, D_MODEL, ADA_COLS), lambda l, j: (l, 0, j)),
            pl.BlockSpec((None, 1, ADA_COLS), lambda l, j: (l, 0, j)),
        ],
        out_specs=pl.BlockSpec((None, 8, ADA_COLS), lambda l, j: (l, 0, j)),
        out_shape=jax.ShapeDtypeStruct((DEPTH, 8, n6), f32),
        compiler_params=_cparams(("parallel", "parallel")),
        name="adaln",
    )(cond8, w_mod, b_mod.reshape(DEPTH, 1, n6))


def _rope_block(blk, cos, sin_a, sin_b):
    return blk * cos + pltpu.roll(blk, LANES - 16, 1) * sin_a + pltpu.roll(blk, 16, 1) * sin_b


def _inproj_kernel(*refs, chunks, n_x):
    x_refs, kv_refs = refs[:n_x], refs[n_x + 7:]
    mod_ref, g_ref, w_ref, cos_ref, sa_ref, sb_ref, qkv_ref = refs[n_x:n_x + 7]
    i = pl.program_id(0)
    h = _norm_mod(_load_x(i, x_refs, TD), g_ref[...], mod_ref[1:2, :], mod_ref[0:1, :]).astype(bf16)
    is_prompt = i < N_PROMPT // TD

    @pl.when(is_prompt)
    def _():
        for c0, c1, s0, _, kv_out in chunks:
            acc = _dot(h, w_ref[:, s0:s0 + (c1 - c0)])
            qkv_ref[:, c0:c1] = acc.astype(bf16)
            for ridx, a0, a1, o0, per_head in kv_out:
                if per_head:
                    heads = kv_refs[ridx].shape[0] // TD
                    for j in range((a1 - a0) // LANES):
                        kv_refs[ridx][pl.ds(o0 // LANES + j, TD, stride=heads), :] = (
                            acc[:, a0 + j * LANES:a0 + (j + 1) * LANES])
                else:
                    t = acc[:, a0:a1].T
                    for b in range(TD // PROMPT_SEQ):
                        kv_refs[ridx][b, o0:o0 + (a1 - a0), :] = t[:, b * PROMPT_SEQ:(b + 1) * PROMPT_SEQ]

    @pl.when(jnp.logical_not(is_prompt))
    def _():
        cos, sa, sb = cos_ref[...], sa_ref[...], sb_ref[...]
        for c0, c1, s0, rope_blocks, _ in chunks:
            acc = _dot(h, w_ref[:, s0:s0 + (c1 - c0)])
            for b in range((c1 - c0) // LANES):
                blk = acc[:, b * LANES:(b + 1) * LANES]
                if b in rope_blocks:
                    blk = _rope_block(blk, cos, sa, sb)
                qkv_ref[:, c0 + b * LANES:c0 + (b + 1) * LANES] = blk.astype(bf16)


def _inproj(x_parts, mod_l, g, w, rope_tabs, chunks, kv_outs):
    n = w.shape[1]
    cos, sa, sb = rope_tabs
    bpt = TD // PROMPT_SEQ

    def rope_idx(i):
        return (jnp.where(i < N_PROMPT // TD, 0, (i - N_PROMPT // TD) % (SAMPLE_SEQ // TD)), 0)

    kv_specs, kv_shapes = [], []
    for kind, size in kv_outs:
        if kind == "T":
            kv_specs.append(pl.BlockSpec((bpt, size, PROMPT_SEQ), lambda i: _prompt_block(i, TD) + (0,)))
            kv_shapes.append(jax.ShapeDtypeStruct((N_PROMPT_BATCH, size, PROMPT_SEQ), f32))
        else:
            kv_specs.append(pl.BlockSpec((TD * size, LANES), lambda i: _prompt_block(i, TD)))
            kv_shapes.append(jax.ShapeDtypeStruct((N_PROMPT * size, LANES), f32))

    return pl.pallas_call(
        functools.partial(_inproj_kernel, chunks=chunks, n_x=len(x_parts)),
        grid=(N_TOK // TD,),
        in_specs=_x_specs(x_parts, TD) + [
            pl.BlockSpec((None, 6, D_MODEL), lambda i: (_mod_row(i, TD), 0, 0)),
            pl.BlockSpec((1, D_MODEL), lambda i: (0, 0)),
            pl.BlockSpec((D_MODEL, n), lambda i: (0, 0)),
            pl.BlockSpec((TD, LANES), rope_idx),
            pl.BlockSpec((TD, LANES), rope_idx),
            pl.BlockSpec((TD, LANES), rope_idx),
        ],
        out_specs=[pl.BlockSpec((TD, n), lambda i: (i, 0))] + kv_specs,
        out_shape=[jax.ShapeDtypeStruct((N_TOK, n), bf16)] + kv_shapes,
        compiler_params=_cparams(("arbitrary",)),
        name="inproj",
    )(*x_parts, mod_l, g, w, cos, sa, sb)


def _rope_tables():
    nq = HEAD_DIM // 4
    t = np.arange(SAMPLE_SEQ)
    inv = np.power(np.float32(ROPE_THETA), -np.arange(nq, dtype=np.float32) / np.float32(nq))
    ang_r = (t // GRID_W).astype(np.float32)[:, None] * inv
    ang_c = (t % GRID_W).astype(np.float32)[:, None] * inv
    zero = np.zeros_like(ang_r)

    def head(fr, fc):
        return np.concatenate([fr[0], fr[1], fc[0], fc[1]], axis=-1)

    cos = head((np.cos(ang_r), np.cos(ang_r)), (np.cos(ang_c), np.cos(ang_c)))
    sin_a = head((-np.sin(ang_r), zero), (-np.sin(ang_c), zero))
    sin_b = head((zero, np.sin(ang_r)), (zero, np.sin(ang_c)))
    two = lambda a: jnp.asarray(np.concatenate([a, a], axis=-1).astype(np.float32))
    return two(cos), two(sin_a), two(sin_b)


def _lane_lo(shape):
    return lax.broadcasted_iota(i32, shape, len(shape) - 1) < HEAD_DIM


def _half(q, lo_mask, half):
    keep = lo_mask if half == 0 else jnp.logical_not(lo_mask)
    return jnp.where(keep, q, jnp.zeros_like(q)) * Q_SCALE


def _swap_halves(x):
    return pltpu.roll(x.astype(f32), HEAD_DIM, 1).astype(x.dtype)


def _stack_halves(q, lo_mask):
    return jnp.concatenate([_half(q, lo_mask, 0), _half(q, lo_mask, 1)], axis=0)


def _with_ones(v):
    return jnp.concatenate([v, jnp.ones_like(v)], axis=1)


def _attend_many(problems):
    scores = [[_dot_nt(q_rows, k) for k in key_blocks] for q_rows, _, key_blocks, _, _, _ in problems]
    exps, maxes = [], []
    for (q_rows, n_heads, key_blocks, _, fix_scores, sinks), sc in zip(problems, scores):
        r = q_rows.shape[0] // n_heads
        e_p, m_p = [[] for _ in key_blocks], []
        for h in range(n_heads):
            blocks = [s[h * r:(h + 1) * r] for s in sc]
            if fix_scores is not None:
                blocks = [fix_scores(h, i, s) for i, s in enumerate(blocks)]
            m = functools.reduce(jnp.maximum, [jnp.max(s, axis=-1, keepdims=True) for s in blocks])
            if sinks is not None:
                m = jnp.maximum(m, sinks[h])
            m_p.append(m)
            for i, s in enumerate(blocks):
                e_p[i].append(jnp.exp((s - m).astype(bf16)))
        exps.append(e_p)
        maxes.append(m_p)
    outs = [functools.reduce(lambda a, b: a + b,
                             [_dot(e[0] if n_heads == 1 else jnp.concatenate(e, axis=0), vx)
                              for e, vx in zip(e_p, vx_blocks)])
            for (_, n_heads, _, vx_blocks, _, _), e_p in zip(problems, exps)]
    results = []
    for (q_rows, n_heads, _, _, _, sinks), out, m_p in zip(problems, outs, maxes):
        r = q_rows.shape[0] // n_heads
        res = []
        for h in range(n_heads):
            den = out[h * r:(h + 1) * r, LANES:]
            if sinks is not None:
                den = den + jnp.exp(sinks[h] - m_p[h])
            res.append(out[h * r:(h + 1) * r, :LANES] * (1.0 / den))
        results.append(res)
    return results


def _gqa_rows(q_blocks, group, lo_mask):
    parts = []
    for q in q_blocks:
        for half in range(2):
            qh = _half(q, lo_mask, half)
            parts.append(qh if half == group else _swap_halves(qh))
    return jnp.concatenate(parts, axis=0)


def _gqa_merge(outs, group, lo_mask):
    fixed = [o if idx % 2 == group else pltpu.roll(o, HEAD_DIM, 1) for idx, o in enumerate(outs)]
    return [jnp.where(lo_mask, fixed[2 * p], fixed[2 * p + 1]) for p in range(len(outs) // 2)]


L0_QA, L0_QB, L0_KB, L0_VB, L0_KA, L0_VA, L0_N = 0, 512, 1024, 1536, 2048, 2176, 2304


def _ctx0_kernel(sink_ref, qkv_ref, o_ref):
    lo = _lane_lo((1, LANES))
    blk = lambda base, j: qkv_ref[:, base + j * LANES:base + (j + 1) * LANES]
    k_a = blk(L0_KA, 0)
    vx_a = _with_ones(blk(L0_VA, 0))
    problems = []
    for g in range(2):
        q_rows = _gqa_rows([blk(L0_QA, 2 * g), blk(L0_QA, 2 * g + 1)], g, lo)
        problems.append((q_rows, 4, [k_a], [vx_a], None, [sink_ref[4 * g + idx] for idx in range(4)]))
    for j in range(4):
        problems.append((_stack_halves(blk(L0_QB, j), lo), 2, [blk(L0_KB, j)], [_with_ones(blk(L0_VB, j))],
                         None, None))
    results = _attend_many(problems)
    for g in range(2):
        for p, o in enumerate(_gqa_merge(results[g], g, lo)):
            j = 2 * g + p
            o_ref[:, j * LANES:(j + 1) * LANES] = o.astype(bf16)
    for j in range(4):
        outs = results[2 + j]
        o_ref[:, 512 + j * LANES:512 + (j + 1) * LANES] = jnp.where(lo, outs[0], outs[1]).astype(bf16)


def _ctx0(sink, qkv):
    return pl.pallas_call(
        _ctx0_kernel,
        grid=(N_PROMPT_BATCH,),
        in_specs=[
            pl.BlockSpec(memory_space=pltpu.SMEM),
            pl.BlockSpec((PROMPT_SEQ, L0_N), lambda b: (b, 0)),
        ],
        out_specs=pl.BlockSpec((PROMPT_SEQ, D_MODEL), lambda b: (b, 0)),
        out_shape=jax.ShapeDtypeStruct((N_PROMPT, D_MODEL), bf16),
        compiler_params=_cparams(("parallel",)),
        name="ctx0",
    )(sink, qkv)


WIN_KEYS = 3 * QB
NA_KEY_ROWS = 10
NA_KEYS = NA_KEY_ROWS * GRID_W
N_QB = SAMPLE_SEQ // QB
N_NA_PATTERNS = 5
_PROMPT_QBLOCKS = N_PROMPT // QB


def _na_pattern(n):
    return jnp.where(n < 2, n, jnp.where(n > N_QB - 3, n - (N_QB - 5), 2))


def _lat0_kernel(sink_ref, roff_ref, q_ref, kvb_ref, kva_ref, cak_ref, cav_ref, cbk_ref, cbv_ref, tiles_ref, o_ref):
    n = pl.program_id(1)
    lo = _lane_lo((1, LANES))
    kstart = pl.multiple_of(jnp.clip((n - 1) * QB, 0, SAMPLE_SEQ - WIN_KEYS), QB)
    k_a = kva_ref[pl.ds(kstart, WIN_KEYS), 0:LANES]
    v_a = kva_ref[pl.ds(kstart, WIN_KEYS), LANES:2 * LANES]
    c_k = cak_ref[...].astype(bf16)
    keys_a = [c_k, k_a]
    vx_a = [_with_ones(cav_ref[...].astype(bf16)), _with_ones(v_a)]
    qpos = n * QB + lax.broadcasted_iota(i32, (QB, WIN_KEYS), 0)
    kpos = kstart + lax.broadcasted_iota(i32, (QB, WIN_KEYS), 1)
    in_window = jnp.abs(qpos - kpos) <= A_WINDOW
    mask_window = lambda h, i, s: jnp.where(in_window, s, NEG) if i == 1 else s
    problems = []
    for g in range(2):
        q_rows = _gqa_rows([q_ref[:, L0_QA + j * LANES:L0_QA + (j + 1) * LANES] for j in (2 * g, 2 * g + 1)], g, lo)
        problems.append((q_rows, 4, keys_a, vx_a, mask_window, [sink_ref[4 * g + idx] for idx in range(4)]))
    krow = jnp.clip(2 * n - NA_ROWS // 2, 0, SAMPLE_SEQ // GRID_W - NA_KEY_ROWS)
    ktok = pl.multiple_of(krow * GRID_W, QB)
    pattern = _na_pattern(n)

    def na_bias(head):
        rows = []
        for rq in range(QB // GRID_W):
            blocks = []
            for kb in range(NA_KEY_ROWS // 2):
                d0, d1 = (roff_ref[(pattern * 2 + rq) * NA_KEY_ROWS + 2 * kb + t] for t in range(2))
                blocks.append(jnp.where(lo, tiles_ref[head, d0], tiles_ref[head, d1]))
            rows.append(jnp.concatenate(blocks, axis=1))
        return jnp.concatenate(rows, axis=0)

    for j in range(4):
        q_b = q_ref[:, L0_QB + j * LANES:L0_QB + (j + 1) * LANES]
        k_b = kvb_ref[pl.ds(ktok, NA_KEYS), j * LANES:(j + 1) * LANES]
        v_b = kvb_ref[pl.ds(ktok, NA_KEYS), 512 + j * LANES:512 + (j + 1) * LANES]
        cb_k = cbk_ref[:, j * LANES:(j + 1) * LANES].astype(bf16)
        cb_v = cbv_ref[:, j * LANES:(j + 1) * LANES].astype(bf16)
        add_bias = lambda h, i, s, j=j: s + na_bias(2 * j + h) if i == 1 else s
        problems.append((_stack_halves(q_b, lo), 2, [cb_k, k_b], [_with_ones(cb_v), _with_ones(v_b)], add_bias, None))

    results = _attend_many(problems)
    for g in range(2):
        for p, o in enumerate(_gqa_merge(results[g], g, lo)):
            j = 2 * g + p
            o_ref[:, j * LANES:(j + 1) * LANES] = o.astype(bf16)
    for j in range(4):
        outs = results[2 + j]
        o_ref[:, 512 + j * LANES:512 + (j + 1) * LANES] = jnp.where(lo, outs[0], outs[1]).astype(bf16)


def _lat0(sink, qkv, cak, cav, cbk, cbv, bias_tiles):
    sb = N_PROMPT // SAMPLE_SEQ
    return pl.pallas_call(
        _lat0_kernel,
        grid=(N_SAMPLE_BATCH, N_QB),
        in_specs=[
            pl.BlockSpec(memory_space=pltpu.SMEM),
            pl.BlockSpec(memory_space=pltpu.SMEM),
            pl.BlockSpec((QB, 1024), lambda b, n: (_PROMPT_QBLOCKS + b * N_QB + n, 0)),
            pl.BlockSpec((SAMPLE_SEQ, 1024), lambda b, n: (sb + b, 1)),
            pl.BlockSpec((SAMPLE_SEQ, 256), lambda b, n: (sb + b, L0_KA // 256)),
            pl.BlockSpec((None, PAST_LEN, LANES), lambda b, n: (b, 0, 0)),
            pl.BlockSpec((None, PAST_LEN, LANES), lambda b, n: (b, 0, 0)),
            pl.BlockSpec((None, PAST_LEN, 512), lambda b, n: (b, 0, 0)),
            pl.BlockSpec((None, PAST_LEN, 512), lambda b, n: (b, 0, 0)),
            pl.BlockSpec((8, N_ROW_OFFSETS + 1, GRID_W, LANES), lambda b, n: (0, 0, 0, 0)),
        ],
        out_specs=pl.BlockSpec((QB, D_MODEL), lambda b, n: (b * N_QB + n, 0)),
        out_shape=jax.ShapeDtypeStruct((N_SAMPLE, D_MODEL), bf16),
        compiler_params=_cparams(("parallel", "arbitrary")),
        name="lat0",
    )(sink, jnp.asarray(_na_row_offsets()), qkv, qkv, qkv, cak, cav, cbk, cbv, bias_tiles)


N_ROW_OFFSETS = 2 * NA_ROWS - 1


def _na_row_offsets():
    rows = SAMPLE_SEQ // GRID_W
    idx = np.full((N_NA_PATTERNS, 2, NA_KEY_ROWS), N_ROW_OFFSETS, np.int32)
    for p, n in enumerate((0, 1, 2, N_QB - 2, N_QB - 1)):
        k0 = int(np.clip(2 * n - NA_ROWS // 2, 0, rows - NA_KEY_ROWS))
        for rq in range(2):
            r = 2 * n + rq
            rs = int(np.clip(r - NA_ROWS // 2, 0, rows - NA_ROWS))
            for kl in range(NA_KEY_ROWS):
                if rs <= k0 + kl < rs + NA_ROWS:
                    idx[p, rq, kl] = k0 + kl - r + NA_ROWS - 1
    return idx.reshape(-1)


def _na_bias_tiles(rel_bias):
    n_dc = 2 * NA_COLS - 1
    c = np.arange(GRID_W)[:, None]
    kc = np.arange(GRID_W)[None, :]
    cs = np.clip(c - NA_COLS // 2, 0, GRID_W - NA_COLS)
    col_ok = (kc >= cs) & (kc < cs + NA_COLS)
    col_hot = ((kc - c + NA_COLS - 1)[None] == np.arange(n_dc)[:, None, None]) & col_ok[None]
    hp = lax.Precision.HIGHEST
    tiles = jnp.einsum("hdx,xck->hdck", rel_bias.astype(f32), col_hot.astype(np.float32), precision=hp)
    tiles = tiles + np.where(col_ok, 0.0, NEG).astype(np.float32)
    tiles = jnp.concatenate([tiles, jnp.full((tiles.shape[0], 1, GRID_W, GRID_W), NEG, f32)], axis=1)
    return jnp.concatenate([tiles, tiles], axis=-1)


def _diff_lambda(lam_ref, lam_init):
    lv = lam_ref[...]
    s1 = jnp.sum(lv[0:1, :] * lv[1:2, :], axis=-1, keepdims=True)
    s2 = jnp.sum(lv[2:3, :] * lv[3:4, :], axis=-1, keepdims=True)
    return jnp.exp(s1) - jnp.exp(s2) + lam_init


def _diff_heads(heads, o_ref, lam, subln, lo, lam_init):
    for p0 in range(0, len(heads), DIFF_HEADS_PER_PASS):
        group = heads[p0:p0 + DIFF_HEADS_PER_PASS]
        results = _attend_many([(_stack_halves(q, lo), 2, ks, [_with_ones(v) for v in vs], None, None)
                                for _, q, ks, vs in group])
        for (h, _, _, _), (o1, o2) in zip(group, results):
            o = o1 - lam * o2
            o = o * lax.rsqrt(jnp.mean(o * o, axis=-1, keepdims=True) + EPS)
            o_ref[:, h * LANES:(h + 1) * LANES] = ((o * subln) * (1.0 - lam_init)).astype(bf16)


def _ctx1_kernel(lam_ref, subln_ref, qkv_ref, o_ref, *, lam_init):
    lo = _lane_lo((1, LANES))
    lam = _diff_lambda(lam_ref, lam_init)
    blk = lambda base, h: qkv_ref[:, base + h * LANES:base + (h + 1) * LANES]
    heads = [(h, blk(0, h), [blk(D_MODEL, h)], [blk(2 * D_MODEL, h)]) for h in range(8)]
    _diff_heads(heads, o_ref, lam, subln_ref[...], lo, lam_init)


def _ctx1(lamv, subln, qkv, lam_init):
    return pl.pallas_call(
        functools.partial(_ctx1_kernel, lam_init=lam_init),
        grid=(N_PROMPT_BATCH,),
        in_specs=[
            pl.BlockSpec((8, HEAD_DIM), lambda b: (0, 0)),
            pl.BlockSpec((1, LANES), lambda b: (0, 0)),
            pl.BlockSpec((PROMPT_SEQ, 3 * D_MODEL), lambda b: (b, 0)),
        ],
        out_specs=pl.BlockSpec((PROMPT_SEQ, D_MODEL), lambda b: (b, 0)),
        out_shape=jax.ShapeDtypeStruct((N_PROMPT, D_MODEL), bf16),
        compiler_params=_cparams(("parallel",)),
        name="ctx1",
    )(lamv, subln, qkv)


def _lat1_kernel(lam_ref, subln_ref, q_ref, k_ref, v_ref, ck_ref, cv_ref, o_ref, *, lam_init):
    lo = _lane_lo((1, LANES))
    lam = _diff_lambda(lam_ref, lam_init)
    heads = []
    for h in range(8):
        sl = slice(h * LANES, (h + 1) * LANES)
        heads.append((h, q_ref[:, sl], [ck_ref[:, sl].astype(bf16), k_ref[:, sl]],
                      [cv_ref[:, sl].astype(bf16), v_ref[:, sl]]))
    _diff_heads(heads, o_ref, lam, subln_ref[...], lo, lam_init)


def _lat1(lamv, subln, qkv, ck, cv, lam_init):
    sb = N_PROMPT // SAMPLE_SEQ
    nq = SAMPLE_SEQ // TM
    return pl.pallas_call(
        functools.partial(_lat1_kernel, lam_init=lam_init),
        grid=(N_SAMPLE_BATCH, nq),
        in_specs=[
            pl.BlockSpec((8, HEAD_DIM), lambda b, n: (0, 0)),
            pl.BlockSpec((1, LANES), lambda b, n: (0, 0)),
            pl.BlockSpec((TM, D_MODEL), lambda b, n: (N_PROMPT_TILES + b * nq + n, 0)),
            pl.BlockSpec((SAMPLE_SEQ, D_MODEL), lambda b, n: (sb + b, 1)),
            pl.BlockSpec((SAMPLE_SEQ, D_MODEL), lambda b, n: (sb + b, 2)),
            pl.BlockSpec((None, PAST_LEN, D_MODEL), lambda b, n: (b, 0, 0)),
            pl.BlockSpec((None, PAST_LEN, D_MODEL), lambda b, n: (b, 0, 0)),
        ],
        out_specs=pl.BlockSpec((TM, D_MODEL), lambda b, n: (b * nq + n, 0)),
        out_shape=jax.ShapeDtypeStruct((N_SAMPLE, D_MODEL), bf16),
        compiler_params=_cparams(("parallel", "arbitrary")),
        name="lat1",
    )(lamv, subln, qkv, qkv, qkv, ck, cv)


def _split_bf16(a):
    hi = a.astype(bf16)
    return hi, (a - hi.astype(f32)).astype(bf16)


def _route_kernel(*refs, n_x):
    x_refs = refs[:n_x]
    (op_ref, os_ref, mod_ref, g_ref, wo_ref, rwt_ref, rb_ref,
     xnew_ref, xloc_ref, slots_ref, gate_ref, len_ref) = refs[n_x:]
    logits = [_route_logits(t, x_refs, op_ref, os_ref, mod_ref, g_ref, wo_ref, rwt_ref, xnew_ref)
              for t in range(ROUTE_TILES)]
    tiles = _route_tiles(jnp.concatenate([lg for _, lg in logits], axis=1), rb_ref, slots_ref, gate_ref, len_ref)
    for t, (slots, run_len) in enumerate(tiles):
        _route_dispatch(t, slots, logits[t][0], run_len, xloc_ref)


def _route_logits(t, x_refs, op_ref, os_ref, mod_ref, g_ref, wo_ref, rwt_ref, xnew_ref):
    is_prompt = pl.program_id(0) < N_PROMPT_TILES // ROUTE_TILES
    rows = slice(t * TM, (t + 1) * TM)
    attn = jnp.where(is_prompt, op_ref[rows, :], os_ref[rows, :])
    x_in = x_refs[0][rows, :] if len(x_refs) == 1 else jnp.where(is_prompt, x_refs[0][rows, :], x_refs[1][rows, :])
    x = x_in + mod_ref[2:3, :] * _dot(attn, wo_ref[...])
    xnew_ref[rows, :] = x
    h = _norm_mod(x, g_ref[...], mod_ref[4:5, :], mod_ref[3:4, :])
    h_hi, h_lo = _split_bf16(h)
    w_hi, w_lo = _split_bf16(rwt_ref[...])
    return h_hi, _dot_nt(w_hi, h_hi) + (_dot_nt(w_hi, h_lo) + _dot_nt(w_lo, h_hi))


def _route_tiles(logits, rb_ref, slots_ref, gate_ref, len_ref):
    ng, ge = N_GROUPS, N_EXPERTS // N_GROUPS
    n = ROUTE_TILES * TM
    tile = lambda a, t: a[..., t * TM:(t + 1) * TM]
    scores = jax.nn.sigmoid(logits)
    biased = scores + rb_ref[...]
    s3 = scores.reshape(ng, ge, n)
    b3 = biased.reshape(ng, ge, n)
    in_group = lax.broadcasted_iota(i32, (ng, ge, n), 1).astype(f32)
    group_id = lax.broadcasted_iota(i32, (ng, 1, n), 0).astype(f32)
    expert_id = lax.broadcasted_iota(i32, (ng, ge, n), 0).astype(f32) * ge + in_group

    def max01(a):
        return jnp.max(jnp.max(a, axis=0, keepdims=True), axis=1, keepdims=True)

    def min01(a):
        return jnp.min(jnp.min(a, axis=0, keepdims=True), axis=1, keepdims=True)

    def sum01(a):
        return jnp.sum(jnp.sum(a, axis=0, keepdims=True), axis=1, keepdims=True)

    m1 = jnp.max(b3, axis=1, keepdims=True)
    first = jnp.min(jnp.where(b3 == m1, in_group, ge), axis=1, keepdims=True)
    m2 = jnp.max(jnp.where(in_group == first, -jnp.inf, b3), axis=1, keepdims=True)
    gscore = m1 + m2
    gsel = jnp.zeros((ng, 1, n), f32)
    for _ in range(TOPK_GROUPS):
        gm = jnp.max(gscore, axis=0, keepdims=True)
        gi = jnp.min(jnp.where(gscore == gm, group_id, ng), axis=0, keepdims=True)
        hit = group_id == gi
        gsel = jnp.where(hit, 1.0, gsel)
        gscore = jnp.where(hit, -jnp.inf, gscore)
    cand = jnp.where(jnp.broadcast_to(gsel, (ng, ge, n)) > 0.0, b3, -jnp.inf)
    top_e, top_w = [], []
    for _ in range(TOP_K):
        em = max01(cand)
        ei = min01(jnp.where(cand == em, expert_id, N_EXPERTS))
        hit = expert_id == ei
        top_e.append(ei)
        top_w.append(sum01(jnp.where(hit, s3, 0.0)))
        cand = jnp.where(hit, -jnp.inf, cand)
    wsum = functools.reduce(lambda a, b: a + b, top_w)
    sel3 = jnp.zeros((ng, ge, n), f32)
    for k, (ei, w) in enumerate(zip(top_e, top_w)):
        gate = (w / wsum * ROUTED_SCALE).reshape(1, n)
        for t in range(ROUTE_TILES):
            gate_ref[t, k:k + 1, :] = tile(gate, t)
        sel3 = jnp.where(expert_id == ei, 1.0, sel3)
    sel = sel3.reshape(N_EXPERTS, n)

    r_i = lax.broadcasted_iota(i32, (N_EXPERTS, N_EXPERTS), 0)
    c_i = lax.broadcasted_iota(i32, (N_EXPERTS, N_EXPERTS), 1)
    lower = jnp.where(c_i < r_i, 1.0, 0.0).astype(bf16)
    run_lens, run_offs = [], []
    for t in range(ROUTE_TILES):
        cnt = jnp.sum(tile(sel, t), axis=1, keepdims=True)
        run_len = jnp.ceil(cnt * (1.0 / CHUNK)) * CHUNK
        run_off = _dot(lower, jnp.broadcast_to(run_len, (N_EXPERTS, LANES)).astype(bf16))[:, 0:1]
        run_lens.append(run_len)
        run_offs.append(jnp.broadcast_to(run_off, (N_EXPERTS, TM)))
    t_r = lax.broadcasted_iota(i32, (n, n), 0)
    t_c = lax.broadcasted_iota(i32, (n, n), 1)
    before = jnp.where(jnp.logical_and(t_r < t_c, t_r // TM == t_c // TM), 1.0, 0.0).astype(bf16)
    rank = _dot(sel.astype(bf16), before)
    slot3 = (jnp.concatenate(run_offs, axis=1) + rank).reshape(ng, ge, n)
    slots = [sum01(jnp.where(expert_id == ei, slot3, 0.0)).reshape(1, n).astype(i32) for ei in top_e]
    out = []
    for t in range(ROUTE_TILES):
        for k in range(TOP_K):
            slots_ref[t, k:k + 1, :] = tile(slots[k], t)
        slots_ref[t, TOP_K:8, :] = jnp.full((8 - TOP_K, TM), -1, i32)
        gate_ref[t, TOP_K:8, :] = jnp.zeros((8 - TOP_K, TM), f32)
        len_ref[t] = jnp.broadcast_to(run_lens[t], (N_EXPERTS, LANES)).astype(i32)
        out.append(([tile(sl, t) for sl in slots], run_lens[t]))
    return out


def _route_dispatch(t, slots, h_hi, run_len, xloc_ref):
    rows = ROUTE_ROWS

    def body(c, carry):
        base = pl.multiple_of(c * rows, rows)
        row_id = base.astype(jnp.int16) + lax.broadcasted_iota(jnp.int16, (rows, TM), 0)
        p = jnp.zeros((rows, TM), bf16)
        for k in range(TOP_K):
            p = jnp.where(row_id == slots[k].astype(jnp.int16), jnp.ones((), bf16), p)
        xloc_ref[pl.ds(t * SLOTS + base, rows), :] = _dot(p, h_hi).astype(bf16)
        return carry

    def zero_body(c, carry):
        base = pl.multiple_of(c * rows, rows)
        xloc_ref[pl.ds(t * SLOTS + base, rows), :] = jnp.zeros((rows, D_MODEL), bf16)
        return carry

    n_used = (jnp.sum(run_len).astype(i32) + (rows - 1)) // rows
    lax.fori_loop(0, n_used, body, 0)
    lax.fori_loop(n_used, SLOTS // rows, zero_body, 0)


def _route(x_parts, o_prompt, o_sample, mod_l, g, w_out, rwt, rb):
    per_tile = lambda i: (i, 0, 0)
    rt = ROUTE_TILES
    tm = rt * TM
    return pl.pallas_call(
        functools.partial(_route_kernel, n_x=len(x_parts)),
        grid=(N_TILES // rt,),
        in_specs=_x_specs(x_parts, tm) + [
            pl.BlockSpec((tm, D_MODEL), lambda i: _prompt_block(i, tm)),
            pl.BlockSpec((tm, D_MODEL), lambda i: _sample_block(i, tm)),
            pl.BlockSpec((None, 6, D_MODEL), lambda i: (_mod_row(i, tm), 0, 0)),
            pl.BlockSpec((1, D_MODEL), lambda i: (0, 0)),
            pl.BlockSpec((D_MODEL, D_MODEL), lambda i: (0, 0)),
            pl.BlockSpec((N_EXPERTS, D_MODEL), lambda i: (0, 0)),
            pl.BlockSpec((N_EXPERTS, 1), lambda i: (0, 0)),
        ],
        out_specs=[
            pl.BlockSpec((tm, D_MODEL), lambda i: (i, 0)),
            pl.BlockSpec((rt * SLOTS, D_MODEL), lambda i: (i, 0)),
            pl.BlockSpec((rt, 8, TM), per_tile),
            pl.BlockSpec((rt, 8, TM), per_tile),
            pl.BlockSpec((rt, N_EXPERTS, LANES), per_tile),
        ],
        out_shape=[
            jax.ShapeDtypeStruct((N_TOK, D_MODEL), f32),
            jax.ShapeDtypeStruct((N_TILES * SLOTS, D_MODEL), bf16),
            jax.ShapeDtypeStruct((N_TILES, 8, TM), i32),
            jax.ShapeDtypeStruct((N_TILES, 8, TM), f32),
            jax.ShapeDtypeStruct((N_TILES, N_EXPERTS, LANES), i32),
        ],
        compiler_params=_cparams(("parallel",)),
        name="route",
    )(*x_parts, o_prompt, o_sample, mod_l, g, w_out, rwt, rb)


def _moe_plan(run_len):
    nt, ne = run_len.shape

    def excl_cumsum(a):
        n = a.shape[-1]
        earlier = np.arange(n)[None, :] < np.arange(n)[:, None]
        return jnp.sum(jnp.where(earlier, a[..., None, :], 0), axis=-1)

    def first_diff(a):
        return a - jnp.concatenate([jnp.zeros_like(a[..., :1]), a[..., :-1]], axis=-1)

    off_loc = excl_cumsum(run_len)
    before = excl_cumsum(run_len.T).T
    n_e = jnp.sum(run_len, axis=0)
    n_pad = -(-n_e // GM) * GM
    g_start = excl_cumsum(n_pad)
    total = jnp.sum(n_pad)
    run_dst = g_start[None, :] + before
    run_src = jnp.arange(nt, dtype=i32)[:, None] * SLOTS + off_loc
    dst_f = run_dst.T.reshape(-1)
    shift_f = first_diff((run_src - run_dst).T.reshape(-1))
    rows = jnp.arange((G_TILES + GMM_ITEM_TILES - 1) * G_CHUNKS, dtype=i32) * CHUNK
    shift = jnp.sum(jnp.where(dst_f[None, :] <= rows[:, None], shift_f[None, :], 0), axis=1)
    in_run = jnp.any((g_start[None, :] <= rows[:, None]) & (rows[:, None] < (g_start + n_e)[None, :]), axis=1)
    chunk_src = (jnp.where(in_run, rows + shift, 0) // CHUNK).astype(i32)
    loc_rows = jnp.arange(SLOT_CHUNKS, dtype=i32) * CHUNK
    shift_l = first_diff(run_dst - off_loc)
    shift = jnp.sum(jnp.where(off_loc[:, None, :] <= loc_rows[None, :, None], shift_l[:, None, :], 0), axis=2)
    used = jnp.sum(run_len, axis=1)
    chunk_map = jnp.where(loc_rows[None, :] < used[:, None], (loc_rows[None, :] + shift) // CHUNK, 0).astype(i32)
    tile_start, n_tiles = g_start // GM, n_pad // GM
    n_items = -(-n_tiles // GMM_ITEM_TILES)
    item_start = excl_cumsum(n_items)
    items = jnp.arange(GMM_MAX_ITEMS, dtype=i32)
    owner = items[:, None] >= item_start[None, :]
    e_first = jnp.sum(jnp.where(owner, first_diff(tile_start - GMM_ITEM_TILES * item_start)[None, :], 0), axis=1)
    item_tile = e_first + GMM_ITEM_TILES * items
    e_end = jnp.sum(jnp.where(owner, first_diff(tile_start + n_tiles)[None, :], 0), axis=1)
    item_cnt = jnp.clip(e_end - item_tile, 0, GMM_ITEM_TILES)
    gmm_plan = tuple(a.astype(i32) for a in (item_start, n_items, item_tile, item_cnt, chunk_src))
    long_tiles = (used > COMBINE_SURE_CHUNKS * CHUNK).astype(i32)
    return gmm_plan, chunk_map.reshape(-1), long_tiles


def _gmm_in_copy(xloc_hbm, xbuf, sem, src_chunk, slot, c):
    return pltpu.make_async_copy(xloc_hbm.at[src_chunk], xbuf.at[slot, c], sem.at[slot])


def _gmm_out_copy(ybuf, y_hbm, sem, tile, slot, n_tiles):
    chunks = n_tiles * G_CHUNKS
    return pltpu.make_async_copy(ybuf.at[slot, pl.ds(0, chunks)],
                                 y_hbm.at[pl.ds(tile * G_CHUNKS, chunks)], sem.at[slot])


def _gmm_kernel(i0_ref, ni_ref, it_ref, ic_ref, cs_ref, xloc_hbm, wg_ref, wu_ref, wd_ref, y_hbm,
                xbuf, ybuf, zbuf, wg_b, wu_b, wd_b, in_sem, out_sem, zsem):
    e = pl.program_id(0)
    last = pl.num_programs(0) - 1
    n_items = ni_ref[e]
    first_item = i0_ref[e]
    total_items = i0_ref[last] + ni_ref[last]
    last_item = total_items - 1
    total_tiles = it_ref[last_item] + ic_ref[last_item]

    def start_in(item):
        first = it_ref[item] * G_CHUNKS
        for c in range(GMM_ITEM_TILES * G_CHUNKS):
            _gmm_in_copy(xloc_hbm, xbuf, in_sem, cs_ref[first + c], item % GMM_SLOTS, c).start()

    def wait_in(item):
        for c in range(GMM_ITEM_TILES * G_CHUNKS):
            _gmm_in_copy(xloc_hbm, xbuf, in_sem, 0, item % GMM_SLOTS, c).wait()

    def out_copy(item, fn):
        for cnt in range(1, GMM_ITEM_TILES + 1):
            @pl.when(ic_ref[item] == cnt)
            def _():
                fn(_gmm_out_copy(ybuf, y_hbm, out_sem, it_ref[item], item % GMM_SLOTS, cnt))

    @pl.when(e == 0)
    def _():
        for item in range(GMM_SLOTS - 1):
            start_in(item)
        zbuf[...] = jnp.zeros(zbuf.shape, zbuf.dtype)

    def tail_copies(fn):
        for j in range(_GMM_TAIL_PER_STEP):
            tile = total_tiles + e + j * N_EXPERTS

            @pl.when(tile < G_TILES)
            def _():
                fn(pltpu.make_async_copy(zbuf, y_hbm.at[pl.ds(tile * G_CHUNKS, G_CHUNKS)], zsem.at[0]))

    tail_copies(lambda cp: cp.start())

    @pl.when(n_items > 0)
    def _():
        wg_b[...] = wg_ref[...].astype(bf16)
        wu_b[...] = wu_ref[...].astype(bf16)
        wd_b[...] = wd_ref[...].astype(bf16)

    def body(j, carry):
        item = first_item + j
        slot = item % GMM_SLOTS

        @pl.when(item + (GMM_SLOTS - 1) < total_items)
        def _():
            start_in(item + (GMM_SLOTS - 1))

        wait_in(item)

        @pl.when(item >= GMM_SLOTS)
        def _():
            out_copy(item - GMM_SLOTS, lambda cp: cp.wait())

        for cnt in range(1, GMM_ITEM_TILES + 1):
            @pl.when(ic_ref[item] == cnt)
            def _():
                rows = cnt * GM
                chunks = cnt * G_CHUNKS
                x = xbuf[slot, 0:chunks].reshape(rows, D_MODEL)
                act = _silu(_dot(x, wg_b[...])) * _dot(x, wu_b[...])
                y = _dot(act.astype(bf16), wd_b[...]).astype(bf16)
                ybuf[slot, 0:chunks] = y.reshape(chunks, CHUNK, D_MODEL)

        out_copy(item, lambda cp: cp.start())
        return carry

    lax.fori_loop(0, n_items, body, 0)
    tail_copies(lambda cp: cp.wait())

    @pl.when(e == last)
    def _():
        for back in range(1, GMM_SLOTS + 1):
            out_copy(total_items - back, lambda cp: cp.wait())


def _gmm(plan, xloc, wg, wu, wd, layer):
    rows = GMM_ITEM_TILES * GM
    w_idx = lambda e, *_: (layer, e, 0, 0)
    grid_spec = pltpu.PrefetchScalarGridSpec(
        num_scalar_prefetch=5,
        grid=(N_EXPERTS,),
        in_specs=[
            pl.BlockSpec(memory_space=pl.ANY),
            pl.BlockSpec((None, None, D_MODEL, EXPERT_DIM), w_idx),
            pl.BlockSpec((None, None, D_MODEL, EXPERT_DIM), w_idx),
            pl.BlockSpec((None, None, EXPERT_DIM, D_MODEL), w_idx),
        ],
        out_specs=pl.BlockSpec(memory_space=pl.ANY),
        scratch_shapes=[pltpu.VMEM((GMM_SLOTS, rows // CHUNK, CHUNK, D_MODEL), bf16),
                        pltpu.VMEM((GMM_SLOTS, rows // CHUNK, CHUNK, D_MODEL), bf16),
                        pltpu.VMEM((G_CHUNKS, CHUNK, D_MODEL), bf16),
                        pltpu.VMEM((D_MODEL, EXPERT_DIM), bf16), pltpu.VMEM((D_MODEL, EXPERT_DIM), bf16),
                        pltpu.VMEM((EXPERT_DIM, D_MODEL), bf16),
                        pltpu.SemaphoreType.DMA((GMM_SLOTS,)), pltpu.SemaphoreType.DMA((GMM_SLOTS,)),
                        pltpu.SemaphoreType.DMA((1,))],
    )
    return pl.pallas_call(
        _gmm_kernel,
        grid_spec=grid_spec,
        out_shape=jax.ShapeDtypeStruct((G_TILES * G_CHUNKS, CHUNK, D_MODEL), bf16),
        compiler_params=_cparams(("arbitrary",)),
        name="gmm",
    )(*plan, xloc, wg, wu, wd)


def _combine_copy(y_hbm, ybuf, sem, sorted_chunk, slot, c):
    return pltpu.make_async_copy(y_hbm.at[sorted_chunk], ybuf.at[slot, c], sem.at[slot])


def _combine_kernel(cm_ref, long_ref, y_hbm, slots_ref, gate_ref, x_ref, mod_ref, g_ref, sg_ref, su_ref, sd_ref,
                    *rest, final):
    if final:
        gf_ref, yp_ref, ys_ref, ybuf, sem = rest
    else:
        o_ref, ybuf, sem = rest
    i = pl.program_id(0)
    n = pl.num_programs(0)
    ahead = COMBINE_SLOTS - 1
    slot = i % COMBINE_SLOTS

    def for_chunks(tile, fn):
        for c in range(COMBINE_SURE_CHUNKS):
            fn(c)

        @pl.when(long_ref[tile] == 1)
        def _():
            for c in range(COMBINE_SURE_CHUNKS, SLOT_CHUNKS):
                fn(c)

    def start(tile, s):
        for_chunks(tile, lambda c: _combine_copy(y_hbm, ybuf, sem, cm_ref[tile * SLOT_CHUNKS + c], s, c).start())

    def wait(tile, s):
        for_chunks(tile, lambda c: _combine_copy(y_hbm, ybuf, sem, 0, s, c).wait())

    @pl.when(i == 0)
    def _():
        ybuf[:, COMBINE_SURE_CHUNKS:] = jnp.zeros((COMBINE_SLOTS, SLOT_CHUNKS - COMBINE_SURE_CHUNKS, CHUNK, D_MODEL), bf16)
        for tile in range(ahead):
            start(tile, tile)

    wait(i, slot)
    start((i + ahead) % n, (i + ahead) % COMBINE_SLOTS)

    x = x_ref[...]
    hb = _norm_mod(x, g_ref[...], mod_ref[4:5, :], mod_ref[3:4, :]).astype(bf16)
    shared = _dot((_silu(_dot(hb, sg_ref[...])) * _dot(hb, su_ref[...])).astype(bf16), sd_ref[...])
    row_id = lax.broadcasted_iota(jnp.int16, (SLOTS, TM), 0)
    p = jnp.zeros((SLOTS, TM), bf16)
    for k in range(TOP_K):
        p = jnp.where(row_id == slots_ref[k:k + 1, :].astype(jnp.int16), gate_ref[k:k + 1, :].astype(bf16), p)
    routed = lax.dot_general(p, ybuf[slot].reshape(SLOTS, D_MODEL), (((0,), (0,)), ((), ())),
                             preferred_element_type=f32)
    out = x + mod_ref[5:6, :] * (routed + shared)
    if final:
        y = (out * lax.rsqrt(jnp.mean(out * out, axis=-1, keepdims=True) + EPS)) * gf_ref[...]

        @pl.when(i < N_PROMPT_TILES)
        def _():
            yp_ref[...] = y

        @pl.when(i >= N_PROMPT_TILES)
        def _():
            ys_ref[...] = y
    else:
        o_ref[...] = out

    @pl.when(i == n - 1)
    def _():
        for k in range(1, ahead + 1):
            wait((i + k) % n, (i + k) % COMBINE_SLOTS)


def _combine(chunk_map, long_tiles, y, slots, gates, x, mod_l, g, sg, su, sd, final_g=None):
    shd = sg.shape[1]
    final = final_g is not None
    row_spec = pl.BlockSpec((TM, D_MODEL), lambda i, *_: (i, 0))
    vec_spec = pl.BlockSpec((1, D_MODEL), lambda i, *_: (0, 0))
    if final:
        out_specs = [pl.BlockSpec((TM, D_MODEL), lambda i, *_: _prompt_block(i)),
                     pl.BlockSpec((TM, D_MODEL), lambda i, *_: _sample_block(i))]
        out_shape = [jax.ShapeDtypeStruct((N_PROMPT, D_MODEL), f32), jax.ShapeDtypeStruct((N_SAMPLE, D_MODEL), f32)]
    else:
        out_specs, out_shape = row_spec, jax.ShapeDtypeStruct((N_TOK, D_MODEL), f32)
    grid_spec = pltpu.PrefetchScalarGridSpec(
        num_scalar_prefetch=2,
        grid=(N_TILES,),
        in_specs=[
            pl.BlockSpec(memory_space=pl.ANY),
            pl.BlockSpec((None, 8, TM), lambda i, *_: (i, 0, 0)),
            pl.BlockSpec((None, 8, TM), lambda i, *_: (i, 0, 0)),
            row_spec,
            pl.BlockSpec((None, 6, D_MODEL), lambda i, *_: (_mod_row(i), 0, 0)),
            vec_spec,
            pl.BlockSpec((D_MODEL, shd), lambda i, *_: (0, 0)),
            pl.BlockSpec((D_MODEL, shd), lambda i, *_: (0, 0)),
            pl.BlockSpec((shd, D_MODEL), lambda i, *_: (0, 0)),
        ] + ([vec_spec] if final else []),
        out_specs=out_specs,
        scratch_shapes=[pltpu.VMEM((COMBINE_SLOTS, SLOT_CHUNKS, CHUNK, D_MODEL), bf16),
                        pltpu.SemaphoreType.DMA((COMBINE_SLOTS,))],
    )
    args = (chunk_map, long_tiles, y, slots, gates, x, mod_l, g, sg, su, sd) + ((final_g,) if final else ())
    return pl.pallas_call(
        functools.partial(_combine_kernel, final=final),
        grid_spec=grid_spec,
        out_shape=out_shape,
        compiler_params=_cparams(("arbitrary",)),
        name="combine",
    )(*args)


def _moe(x_parts, o_prompt, o_sample, w_out, mod_l, g, rwt, rb, wg, wu, wd, layer, sg, su, sd, final_g=None):
    x, xloc, slots, gates, run_len = _route(x_parts, o_prompt, o_sample, mod_l, g, w_out, rwt, rb)
    gmm_plan, chunk_map, long_tiles = _moe_plan(run_len[:, :, 0])
    y = _gmm(gmm_plan, xloc.reshape(N_TILES * SLOT_CHUNKS, CHUNK, D_MODEL), wg, wu, wd, layer)
    return _combine(chunk_map, long_tiles, y, slots, gates, x, mod_l, g, sg, su, sd, final_g)


_L0_CHUNKS = (
    (0, 512, 0, (0, 1, 2, 3), ()),
    (512, 1024, 768, (), ()),
    (1024, 1536, 1280, (), ((2, 0, 512, 0, False),)),
    (1536, 2048, 1792, (), ((3, 0, 512, 0, False),)),
    (2048, 2304, 512, (0,), ((0, 0, 128, 0, False), (1, 128, 256, 0, False))),
)
_L0_KV_OUTS = (("T", 128), ("T", 128), ("T", 512), ("T", 512))
_L1_CHUNKS = (
    (0, 512, 0, (0, 1, 2, 3), ()),
    (512, 1024, 512, (0, 1, 2, 3), ()),
    (1024, 1536, 1024, (0, 1, 2, 3), ((0, 0, 512, 0, False),)),
    (1536, 2048, 1536, (0, 1, 2, 3), ((0, 0, 512, 512, False),)),
    (2048, 2560, 2048, (), ((1, 0, 512, 0, True),)),
    (2560, 3072, 2560, (), ((1, 0, 512, 512, True),)),
)
_L1_KV_OUTS = (("T", 1024), ("H", 8))


def _from_feature_major(kt, *head_dims):
    nb, _, s = kt.shape
    nd = len(head_dims)
    return kt.reshape(nb, *head_dims, s).transpose(0, nd + 1, *range(1, nd + 1))[:, None]


def kernel(x_prompt, x_sample, cache_a_k, cache_a_v, cache_b_k, cache_b_v, cache_c_k, cache_c_v, c, c_ctx, w_mod, b_mod, norm_mix, norm_ffn, w_in_ab, w_out_ab, sink_a, rel_bias_b, w_in_c, w_out_c, lam_q1, lam_k1, lam_q2, lam_k2, subln_c, router_w, router_bias, exp_w_gate, exp_w_up, exp_w_down, sh_w_gate, sh_w_up, sh_w_down, final_norm):
    x = (x_prompt.reshape(N_PROMPT, D_MODEL), x_sample.reshape(N_SAMPLE, D_MODEL))
    cond8 = jnp.concatenate([c_ctx[None, :], c, jnp.zeros((8 - 1 - N_SAMPLE_BATCH, D_MODEL), f32)], axis=0)
    mod = _adaln(cond8, w_mod, b_mod).reshape(DEPTH, 8, 6, D_MODEL)
    rope_tabs = _rope_tables()
    new_kv = {}
    for layer in range(DEPTH):
        li = layer // 2
        mod_l = mod[layer]
        g_mix = norm_mix[layer][None, :]
        g_ffn = norm_ffn[layer][None, :]
        if layer % 2 == 0:
            w_in = w_in_ab[li].astype(bf16)
            qkv, ak, av, bk, bv = _inproj(x, mod_l, g_mix, w_in, rope_tabs, _L0_CHUNKS, _L0_KV_OUTS)
            new_kv["a_k"], new_kv["a_v"], new_kv["b_k"], new_kv["b_v"] = ak, av, bk, bv
            o_p = _ctx0(sink_a[li], qkv)
            o_s = _lat0(sink_a[li], qkv,
                        cache_a_k[:, li].reshape(N_SAMPLE_BATCH, PAST_LEN, LANES),
                        cache_a_v[:, li].reshape(N_SAMPLE_BATCH, PAST_LEN, LANES),
                        cache_b_k[:, li].reshape(N_SAMPLE_BATCH, PAST_LEN, 512),
                        cache_b_v[:, li].reshape(N_SAMPLE_BATCH, PAST_LEN, 512),
                        _na_bias_tiles(rel_bias_b[li]))
            w_out = w_out_ab[li].astype(bf16)
        else:
            lam_init = 0.8 - 0.6 * math.exp(-0.3 * layer)
            qkv, ck, cv = _inproj(x, mod_l, g_mix, w_in_c[li].astype(bf16), rope_tabs, _L1_CHUNKS, _L1_KV_OUTS)
            new_kv["c_k"], new_kv["c_v"] = ck, cv
            lamv = jnp.concatenate([lam_q1[li][None], lam_k1[li][None], lam_q2[li][None], lam_k2[li][None],
                                    jnp.zeros((4, HEAD_DIM), f32)], axis=0)
            subln = subln_c[li][None, :]
            o_p = _ctx1(lamv, subln, qkv, lam_init)
            o_s = _lat1(lamv, subln, qkv,
                        cache_c_k[:, li].reshape(N_SAMPLE_BATCH, PAST_LEN, D_MODEL),
                        cache_c_v[:, li].reshape(N_SAMPLE_BATCH, PAST_LEN, D_MODEL), lam_init)
            w_out = w_out_c[li].astype(bf16)
        last = layer == DEPTH - 1
        x = _moe(x, o_p, o_s, w_out, mod_l, g_ffn, router_w[layer].T, router_bias[layer][:, None],
                 exp_w_gate, exp_w_up, exp_w_down, layer,
                 sh_w_gate[layer].astype(bf16), sh_w_up[layer].astype(bf16), sh_w_down[layer].astype(bf16),
                 final_norm[None, :] if last else None)
        x = x if last else (x,)
    y_prompt, y_sample = x
    nb, s = N_PROMPT_BATCH, PROMPT_SEQ
    return (y_prompt.reshape(nb, s, D_MODEL), y_sample.reshape(N_SAMPLE_BATCH, SAMPLE_SEQ, D_MODEL),
            _from_feature_major(new_kv["a_k"], 2, HEAD_DIM), _from_feature_major(new_kv["a_v"], 2, HEAD_DIM),
            _from_feature_major(new_kv["b_k"], 8, HEAD_DIM), _from_feature_major(new_kv["b_v"], 8, HEAD_DIM),
            _from_feature_major(new_kv["c_k"], 8, 2, HEAD_DIM), new_kv["c_v"].reshape(nb, 1, s, 8, 2 * HEAD_DIM))
```

```python
import functools
import math

import jax
import jax.numpy as jnp
import numpy as np
from jax import lax
from jax.experimental import pallas as pl
from jax.experimental.pallas import tpu as pltpu

f32 = jnp.float32
bf16 = jnp.bfloat16
i32 = jnp.int32

D_MODEL = 1024
N_PROMPT_BATCH = 16
PROMPT_SEQ = 256
DEPTH = 2
N_SAMPLE_BATCH = 2
SAMPLE_SEQ = 2048
PAST_LEN = 512
GRID_W = 64
HEAD_DIM = 64
ROPE_THETA = 10000.0
EPS = 1e-6
A_WINDOW = 128
NA_ROWS = 8
NA_COLS = 16
N_EXPERTS = 64
TOP_K = 6
N_GROUPS = 8
TOPK_GROUPS = 4
EXPERT_DIM = 256
ROUTED_SCALE = 2.5
Q_SCALE = HEAD_DIM ** -0.5

N_PROMPT = N_PROMPT_BATCH * PROMPT_SEQ
N_SAMPLE = N_SAMPLE_BATCH * SAMPLE_SEQ
N_TOK = N_PROMPT + N_SAMPLE

LANES = 128
TM = 256
TD = 512
N_PROMPT_TILES = N_PROMPT // TM
N_TILES = N_TOK // TM
QB = 128
CHUNK = 16
SLOTS = -(-(TM * TOP_K + N_EXPERTS * (CHUNK - 1)) // 256) * 256
SLOT_CHUNKS = SLOTS // CHUNK
ROUTE_ROWS = 512
ROUTE_TILES = 2
COMBINE_SLOTS = 3
COMBINE_SURE_CHUNKS = 128
GM = 256
_MAX_SORTED = TM * TOP_K * N_TILES + N_TILES * N_EXPERTS * (CHUNK - 1) + N_EXPERTS * (GM - CHUNK)
G_TILES = -(-_MAX_SORTED // GM)
G_CHUNKS = GM // CHUNK
GMM_ITEM_TILES = 2
GMM_SLOTS = 4
GMM_MAX_ITEMS = (G_TILES + N_EXPERTS * (GMM_ITEM_TILES - 1)) // GMM_ITEM_TILES
_GMM_TAIL_PER_STEP = -(-(G_TILES - TM * TOP_K * N_TILES // GM) // N_EXPERTS)
ADA_COLS = 1536
DIFF_HEADS_PER_PASS = 2
VMEM_LIMIT = 56 * 1024 * 1024

NEG = -1e30


def _cparams(sem):
    return pltpu.CompilerParams(dimension_semantics=sem, vmem_limit_bytes=VMEM_LIMIT)


def _mod_row(i, tm=TM):
    return jnp.where(i < N_PROMPT // tm, 0, 1 + (i - N_PROMPT // tm) // (SAMPLE_SEQ // tm))


def _prompt_block(i, tm=TM):
    return (jnp.minimum(i, N_PROMPT // tm - 1), 0)


def _sample_block(i, tm=TM):
    return (jnp.maximum(i - N_PROMPT // tm, 0), 0)


def _x_specs(parts, tm=TM):
    if len(parts) == 1:
        return [pl.BlockSpec((tm, D_MODEL), lambda i, *_: (i, 0))]
    return [pl.BlockSpec((tm, D_MODEL), lambda i, *_: _prompt_block(i, tm)),
            pl.BlockSpec((tm, D_MODEL), lambda i, *_: _sample_block(i, tm))]


def _load_x(i, x_refs, tm=TM):
    if len(x_refs) == 1:
        return x_refs[0][...]
    return jnp.where(i < N_PROMPT // tm, x_refs[0][...], x_refs[1][...])


def _norm_mod(x, g, scale, shift):
    y = x * lax.rsqrt(jnp.mean(x * x, axis=-1, keepdims=True) + EPS)
    return (y * g) * (1.0 + scale) + shift


def _silu(x):
    return x * jax.nn.sigmoid(x)


def _dot(a, b):
    return jnp.dot(a, b, preferred_element_type=f32)


def _dot_nt(a, b):
    return lax.dot_general(a, b, (((1,), (1,)), ((), ())), preferred_element_type=f32)


def _adaln_kernel(cond_ref, w_ref, b_ref, o_ref):
    s = _silu(cond_ref[...]).astype(bf16)
    o_ref[...] = _dot(s, w_ref[...].astype(bf16)) + b_ref[...]


def _adaln(cond8, w_mod, b_mod):
    n6 = 6 * D_MODEL
    return pl.pallas_call(
        _adaln_kernel,
        grid=(DEPTH, n6 // ADA_COLS),
        in_specs=[
            pl.BlockSpec((8, D_MODEL), lambda l, j: (0, 0)),
            pl.BlockSpec((None, D_MODEL, ADA_COLS), lambda l, j: (l, 0, j)),
            pl.BlockSpec((None, 1, ADA_COLS), lambda l, j: (l, 0, j)),
        ],
        out_specs=pl.BlockSpec((None, 8, ADA_COLS), lambda l, j: (l, 0, j)),
        out_shape=jax.ShapeDtypeStruct((DEPTH, 8, n6), f32),
        compiler_params=_cparams(("parallel", "parallel")),
        name="adaln",
    )(cond8, w_mod, b_mod.reshape(DEPTH, 1, n6))


def _rope_block(blk, cos, sin_a, sin_b):
    return blk * cos + pltpu.roll(blk, LANES - 16, 1) * sin_a + pltpu.roll(blk, 16, 1) * sin_b


def _inproj_kernel(*refs, chunks, n_x):
    x_refs, kv_refs = refs[:n_x], refs[n_x + 7:]
    mod_ref, g_ref, w_ref, cos_ref, sa_ref, sb_ref, qkv_ref = refs[n_x:n_x + 7]
    i = pl.program_id(0)
    h = _norm_mod(_load_x(i, x_refs, TD), g_ref[...], mod_ref[1:2, :], mod_ref[0:1, :]).astype(bf16)
    is_prompt = i < N_PROMPT // TD

    @pl.when(is_prompt)
    def _():
        for c0, c1, s0, _, kv_out in chunks:
            acc = _dot(h, w_ref[:, s0:s0 + (c1 - c0)])
            qkv_ref[:, c0:c1] = acc.astype(bf16)
            for ridx, a0, a1, o0, per_head in kv_out:
                if per_head:
                    heads = kv_refs[ridx].shape[0] // TD
                    for j in range((a1 - a0) // LANES):
                        kv_refs[ridx][pl.ds(o0 // LANES + j, TD, stride=heads), :] = (
                            acc[:, a0 + j * LANES:a0 + (j + 1) * LANES])
                else:
                    t = acc[:, a0:a1].T
                    for b in range(TD // PROMPT_SEQ):
                        kv_refs[ridx][b, o0:o0 + (a1 - a0), :] = t[:, b * PROMPT_SEQ:(b + 1) * PROMPT_SEQ]

    @pl.when(jnp.logical_not(is_prompt))
    def _():
        cos, sa, sb = cos_ref[...], sa_ref[...], sb_ref[...]
        for c0, c1, s0, rope_blocks, _ in chunks:
            acc = _dot(h, w_ref[:, s0:s0 + (c1 - c0)])
            for b in range((c1 - c0) // LANES):
                blk = acc[:, b * LANES:(b + 1) * LANES]
                if b in rope_blocks:
                    blk = _rope_block(blk, cos, sa, sb)
                qkv_ref[:, c0 + b * LANES:c0 + (b + 1) * LANES] = blk.astype(bf16)


def _inproj(x_parts, mod_l, g, w, rope_tabs, chunks, kv_outs):
    n = w.shape[1]
    cos, sa, sb = rope_tabs
    bpt = TD // PROMPT_SEQ

    def rope_idx(i):
        return (jnp.where(i < N_PROMPT // TD, 0, (i - N_PROMPT // TD) % (SAMPLE_SEQ // TD)), 0)

    kv_specs, kv_shapes = [], []
    for kind, size in kv_outs:
        if kind == "T":
            kv_specs.append(pl.BlockSpec((bpt, size, PROMPT_SEQ), lambda i: _prompt_block(i, TD) + (0,)))
            kv_shapes.append(jax.ShapeDtypeStruct((N_PROMPT_BATCH, size, PROMPT_SEQ), f32))
        else:
            kv_specs.append(pl.BlockSpec((TD * size, LANES), lambda i: _prompt_block(i, TD)))
            kv_shapes.append(jax.ShapeDtypeStruct((N_PROMPT * size, LANES), f32))

    return pl.pallas_call(
        functools.partial(_inproj_kernel, chunks=chunks, n_x=len(x_parts)),
        grid=(N_TOK // TD,),
        in_specs=_x_specs(x_parts, TD) + [
            pl.BlockSpec((None, 6, D_MODEL), lambda i: (_mod_row(i, TD), 0, 0)),
            pl.BlockSpec((1, D_MODEL), lambda i: (0, 0)),
            pl.BlockSpec((D_MODEL, n), lambda i: (0, 0)),
            pl.BlockSpec((TD, LANES), rope_idx),
            pl.BlockSpec((TD, LANES), rope_idx),
            pl.BlockSpec((TD, LANES), rope_idx),
        ],
        out_specs=[pl.BlockSpec((TD, n), lambda i: (i, 0))] + kv_specs,
        out_shape=[jax.ShapeDtypeStruct((N_TOK, n), bf16)] + kv_shapes,
        compiler_params=_cparams(("arbitrary",)),
        name="inproj",
    )(*x_parts, mod_l, g, w, cos, sa, sb)


def _rope_tables():
    nq = HEAD_DIM // 4
    t = np.arange(SAMPLE_SEQ)
    inv = np.power(np.float32(ROPE_THETA), -np.arange(nq, dtype=np.float32) / np.float32(nq))
    ang_r = (t // GRID_W).astype(np.float32)[:, None] * inv
    ang_c = (t % GRID_W).astype(np.float32)[:, None] * inv
    zero = np.zeros_like(ang_r)

    def head(fr, fc):
        return np.concatenate([fr[0], fr[1], fc[0], fc[1]], axis=-1)

    cos = head((np.cos(ang_r), np.cos(ang_r)), (np.cos(ang_c), np.cos(ang_c)))
    sin_a = head((-np.sin(ang_r), zero), (-np.sin(ang_c), zero))
    sin_b = head((zero, np.sin(ang_r)), (zero, np.sin(ang_c)))
    two = lambda a: jnp.asarray(np.concatenate([a, a], axis=-1).astype(np.float32))
    return two(cos), two(sin_a), two(sin_b)


def _lane_lo(shape):
    return lax.broadcasted_iota(i32, shape, len(shape) - 1) < HEAD_DIM


def _half(q, lo_mask, half):
    keep = lo_mask if half == 0 else jnp.logical_not(lo_mask)
    return jnp.where(keep, q, jnp.zeros_like(q)) * Q_SCALE


def _swap_halves(x):
    return pltpu.roll(x.astype(f32), HEAD_DIM, 1).astype(x.dtype)


def _stack_halves(q, lo_mask):
    return jnp.concatenate([_half(q, lo_mask, 0), _half(q, lo_mask, 1)], axis=0)


def _with_ones(v):
    return jnp.concatenate([v, jnp.ones_like(v)], axis=1)


def _attend_many(problems):
    scores = [[_dot_nt(q_rows, k) for k in key_blocks] for q_rows, _, key_blocks, _, _, _ in problems]
    exps, maxes = [], []
    for (q_rows, n_heads, key_blocks, _, fix_scores, sinks), sc in zip(problems, scores):
        r = q_rows.shape[0] // n_heads
        e_p, m_p = [[] for _ in key_blocks], []
        for h in range(n_heads):
            blocks = [s[h * r:(h + 1) * r] for s in sc]
            if fix_scores is not None:
                blocks = [fix_scores(h, i, s) for i, s in enumerate(blocks)]
            m = functools.reduce(jnp.maximum, [jnp.max(s, axis=-1, keepdims=True) for s in blocks])
            if sinks is not None:
                m = jnp.maximum(m, sinks[h])
            m_p.append(m)
            for i, s in enumerate(blocks):
                e_p[i].append(jnp.exp((s - m).astype(bf16)))
        exps.append(e_p)
        maxes.append(m_p)
    outs = [functools.reduce(lambda a, b: a + b,
                             [_dot(e[0] if n_heads == 1 else jnp.concatenate(e, axis=0), vx)
                              for e, vx in zip(e_p, vx_blocks)])
            for (_, n_heads, _, vx_blocks, _, _), e_p in zip(problems, exps)]
    results = []
    for (q_rows, n_heads, _, _, _, sinks), out, m_p in zip(problems, outs, maxes):
        r = q_rows.shape[0] // n_heads
        res = []
        for h in range(n_heads):
            den = out[h * r:(h + 1) * r, LANES:]
            if sinks is not None:
                den = den + jnp.exp(sinks[h] - m_p[h])
            res.append(out[h * r:(h + 1) * r, :LANES] * (1.0 / den))
        results.append(res)
    return results


def _gqa_rows(q_blocks, group, lo_mask):
    parts = []
    for q in q_blocks:
        for half in range(2):
            qh = _half(q, lo_mask, half)
            parts.append(qh if half == group else _swap_halves(qh))
    return jnp.concatenate(parts, axis=0)


def _gqa_merge(outs, group, lo_mask):
    fixed = [o if idx % 2 == group else pltpu.roll(o, HEAD_DIM, 1) for idx, o in enumerate(outs)]
    return [jnp.where(lo_mask, fixed[2 * p], fixed[2 * p + 1]) for p in range(len(outs) // 2)]


L0_QA, L0_QB, L0_KB, L0_VB, L0_KA, L0_VA, L0_N = 0, 512, 1024, 1536, 2048, 2176, 2304


def _ctx0_kernel(sink_ref, qkv_ref, o_ref):
    lo = _lane_lo((1, LANES))
    blk = lambda base, j: qkv_ref[:, base + j * LANES:base + (j + 1) * LANES]
    k_a = blk(L0_KA, 0)
    vx_a = _with_ones(blk(L0_VA, 0))
    problems = []
    for g in range(2):
        q_rows = _gqa_rows([blk(L0_QA, 2 * g), blk(L0_QA, 2 * g + 1)], g, lo)
        problems.append((q_rows, 4, [k_a], [vx_a], None, [sink_ref[4 * g + idx] for idx in range(4)]))
    for j in range(4):
        problems.append((_stack_halves(blk(L0_QB, j), lo), 2, [blk(L0_KB, j)], [_with_ones(blk(L0_VB, j))],
                         None, None))
    results = _attend_many(problems)
    for g in range(2):
        for p, o in enumerate(_gqa_merge(results[g], g, lo)):
            j = 2 * g + p
            o_ref[:, j * LANES:(j + 1) * LANES] = o.astype(bf16)
    for j in range(4):
        outs = results[2 + j]
        o_ref[:, 512 + j * LANES:512 + (j + 1) * LANES] = jnp.where(lo, outs[0], outs[1]).astype(bf16)


def _ctx0(sink, qkv):
    return pl.pallas_call(
        _ctx0_kernel,
        grid=(N_PROMPT_BATCH,),
        in_specs=[
            pl.BlockSpec(memory_space=pltpu.SMEM),
            pl.BlockSpec((PROMPT_SEQ, L0_N), lambda b: (b, 0)),
        ],
        out_specs=pl.BlockSpec((PROMPT_SEQ, D_MODEL), lambda b: (b, 0)),
        out_shape=jax.ShapeDtypeStruct((N_PROMPT, D_MODEL), bf16),
        compiler_params=_cparams(("parallel",)),
        name="ctx0",
    )(sink, qkv)


WIN_KEYS = 3 * QB
NA_KEY_ROWS = 10
NA_KEYS = NA_KEY_ROWS * GRID_W
N_QB = SAMPLE_SEQ // QB
N_NA_PATTERNS = 5
_PROMPT_QBLOCKS = N_PROMPT // QB


def _na_pattern(n):
    return jnp.where(n < 2, n, jnp.where(n > N_QB - 3, n - (N_QB - 5), 2))


def _lat0_kernel(sink_ref, roff_ref, q_ref, kvb_ref, kva_ref, cak_ref, cav_ref, cbk_ref, cbv_ref, tiles_ref, o_ref):
    n = pl.program_id(1)
    lo = _lane_lo((1, LANES))
    kstart = pl.multiple_of(jnp.clip((n - 1) * QB, 0, SAMPLE_SEQ - WIN_KEYS), QB)
    k_a = kva_ref[pl.ds(kstart, WIN_KEYS), 0:LANES]
    v_a = kva_ref[pl.ds(kstart, WIN_KEYS), LANES:2 * LANES]
    c_k = cak_ref[...].astype(bf16)
    keys_a = [c_k, k_a]
    vx_a = [_with_ones(cav_ref[...].astype(bf16)), _with_ones(v_a)]
    qpos = n * QB + lax.broadcasted_iota(i32, (QB, WIN_KEYS), 0)
    kpos = kstart + lax.broadcasted_iota(i32, (QB, WIN_KEYS), 1)
    in_window = jnp.abs(qpos - kpos) <= A_WINDOW
    mask_window = lambda h, i, s: jnp.where(in_window, s, NEG) if i == 1 else s
    problems = []
    for g in range(2):
        q_rows = _gqa_rows([q_ref[:, L0_QA + j * LANES:L0_QA + (j + 1) * LANES] for j in (2 * g, 2 * g + 1)], g, lo)
        problems.append((q_rows, 4, keys_a, vx_a, mask_window, [sink_ref[4 * g + idx] for idx in range(4)]))
    krow = jnp.clip(2 * n - NA_ROWS // 2, 0, SAMPLE_SEQ // GRID_W - NA_KEY_ROWS)
    ktok = pl.multiple_of(krow * GRID_W, QB)
    pattern = _na_pattern(n)

    def na_bias(head):
        rows = []
        for rq in range(QB // GRID_W):
            blocks = []
            for kb in range(NA_KEY_ROWS // 2):
                d0, d1 = (roff_ref[(pattern * 2 + rq) * NA_KEY_ROWS + 2 * kb + t] for t in range(2))
                blocks.append(jnp.where(lo, tiles_ref[head, d0], tiles_ref[head, d1]))
            rows.append(jnp.concatenate(blocks, axis=1))
        return jnp.concatenate(rows, axis=0)

    for j in range(4):
        q_b = q_ref[:, L0_QB + j * LANES:L0_QB + (j + 1) * LANES]
        k_b = kvb_ref[pl.ds(ktok, NA_KEYS), j * LANES:(j + 1) * LANES]
        v_b = kvb_ref[pl.ds(ktok, NA_KEYS), 512 + j * LANES:512 + (j + 1) * LANES]
        cb_k = cbk_ref[:, j * LANES:(j + 1) * LANES].astype(bf16)
        cb_v = cbv_ref[:, j * LANES:(j + 1) * LANES].astype(bf16)
        add_bias = lambda h, i, s, j=j: s + na_bias(2 * j + h) if i == 1 else s
        problems.append((_stack_halves(q_b, lo), 2, [cb_k, k_b], [_with_ones(cb_v), _with_ones(v_b)], add_bias, None))

    results = _attend_many(problems)
    for g in range(2):
        for p, o in enumerate(_gqa_merge(results[g], g, lo)):
            j = 2 * g + p
            o_ref[:, j * LANES:(j + 1) * LANES] = o.astype(bf16)
    for j in range(4):
        outs = results[2 + j]
        o_ref[:, 512 + j * LANES:512 + (j + 1) * LANES] = jnp.where(lo, outs[0], outs[1]).astype(bf16)


def _lat0(sink, qkv, cak, cav, cbk, cbv, bias_tiles):
    sb = N_PROMPT // SAMPLE_SEQ
    return pl.pallas_call(
        _lat0_kernel,
        grid=(N_SAMPLE_BATCH, N_QB),
        in_specs=[
            pl.BlockSpec(memory_space=pltpu.SMEM),
            pl.BlockSpec(memory_space=pltpu.SMEM),
            pl.BlockSpec((QB, 1024), lambda b, n: (_PROMPT_QBLOCKS + b * N_QB + n, 0)),
            pl.BlockSpec((SAMPLE_SEQ, 1024), lambda b, n: (sb + b, 1)),
            pl.BlockSpec((SAMPLE_SEQ, 256), lambda b, n: (sb + b, L0_KA // 256)),
            pl.BlockSpec((None, PAST_LEN, LANES), lambda b, n: (b, 0, 0)),
            pl.BlockSpec((None, PAST_LEN, LANES), lambda b, n: (b, 0, 0)),
            pl.BlockSpec((None, PAST_LEN, 512), lambda b, n: (b, 0, 0)),
            pl.BlockSpec((None, PAST_LEN, 512), lambda b, n: (b, 0, 0)),
            pl.BlockSpec((8, N_ROW_OFFSETS + 1, GRID_W, LANES), lambda b, n: (0, 0, 0, 0)),
        ],
        out_specs=pl.BlockSpec((QB, D_MODEL), lambda b, n: (b * N_QB + n, 0)),
        out_shape=jax.ShapeDtypeStruct((N_SAMPLE, D_MODEL), bf16),
        compiler_params=_cparams(("parallel", "arbitrary")),
        name="lat0",
    )(sink, jnp.asarray(_na_row_offsets()), qkv, qkv, qkv, cak, cav, cbk, cbv, bias_tiles)


N_ROW_OFFSETS = 2 * NA_ROWS - 1


def _na_row_offsets():
    rows = SAMPLE_SEQ // GRID_W
    idx = np.full((N_NA_PATTERNS, 2, NA_KEY_ROWS), N_ROW_OFFSETS, np.int32)
    for p, n in enumerate((0, 1, 2, N_QB - 2, N_QB - 1)):
        k0 = int(np.clip(2 * n - NA_ROWS // 2, 0, rows - NA_KEY_ROWS))
        for rq in range(2):
            r = 2 * n + rq
            rs = int(np.clip(r - NA_ROWS // 2, 0, rows - NA_ROWS))
            for kl in range(NA_KEY_ROWS):
                if rs <= k0 + kl < rs + NA_ROWS:
                    idx[p, rq, kl] = k0 + kl - r + NA_ROWS - 1
    return idx.reshape(-1)


def _na_bias_tiles(rel_bias):
    n_dc = 2 * NA_COLS - 1
    c = np.arange(GRID_W)[:, None]
    kc = np.arange(GRID_W)[None, :]
    cs = np.clip(c - NA_COLS // 2, 0, GRID_W - NA_COLS)
    col_ok = (kc >= cs) & (kc < cs + NA_COLS)
    col_hot = ((kc - c + NA_COLS - 1)[None] == np.arange(n_dc)[:, None, None]) & col_ok[None]
    hp = lax.Precision.HIGHEST
    tiles = jnp.einsum("hdx,xck->hdck", rel_bias.astype(f32), col_hot.astype(np.float32), precision=hp)
    tiles = tiles + np.where(col_ok, 0.0, NEG).astype(np.float32)
    tiles = jnp.concatenate([tiles, jnp.full((tiles.shape[0], 1, GRID_W, GRID_W), NEG, f32)], axis=1)
    return jnp.concatenate([tiles, tiles], axis=-1)


def _diff_lambda(lam_ref, lam_init):
    lv = lam_ref[...]
    s1 = jnp.sum(lv[0:1, :] * lv[1:2, :], axis=-1, keepdims=True)
    s2 = jnp.sum(lv[2:3, :] * lv[3:4, :], axis=-1, keepdims=True)
    return jnp.exp(s1) - jnp.exp(s2) + lam_init


def _diff_heads(heads, o_ref, lam, subln, lo, lam_init):
    for p0 in range(0, len(heads), DIFF_HEADS_PER_PASS):
        group = heads[p0:p0 + DIFF_HEADS_PER_PASS]
        results = _attend_many([(_stack_halves(q, lo), 2, ks, [_with_ones(v) for v in vs], None, None)
                                for _, q, ks, vs in group])
        for (h, _, _, _), (o1, o2) in zip(group, results):
            o = o1 - lam * o2
            o = o * lax.rsqrt(jnp.mean(o * o, axis=-1, keepdims=True) + EPS)
            o_ref[:, h * LANES:(h + 1) * LANES] = ((o * subln) * (1.0 - lam_init)).astype(bf16)


def _ctx1_kernel(lam_ref, subln_ref, qkv_ref, o_ref, *, lam_init):
    lo = _lane_lo((1, LANES))
    lam = _diff_lambda(lam_ref, lam_init)
    blk = lambda base, h: qkv_ref[:, base + h * LANES:base + (h + 1) * LANES]
    heads = [(h, blk(0, h), [blk(D_MODEL, h)], [blk(2 * D_MODEL, h)]) for h in range(8)]
    _diff_heads(heads, o_ref, lam, subln_ref[...], lo, lam_init)


def _ctx1(lamv, subln, qkv, lam_init):
    return pl.pallas_call(
        functools.partial(_ctx1_kernel, lam_init=lam_init),
        grid=(N_PROMPT_BATCH,),
        in_specs=[
            pl.BlockSpec((8, HEAD_DIM), lambda b: (0, 0)),
            pl.BlockSpec((1, LANES), lambda b: (0, 0)),
            pl.BlockSpec((PROMPT_SEQ, 3 * D_MODEL), lambda b: (b, 0)),
        ],
        out_specs=pl.BlockSpec((PROMPT_SEQ, D_MODEL), lambda b: (b, 0)),
        out_shape=jax.ShapeDtypeStruct((N_PROMPT, D_MODEL), bf16),
        compiler_params=_cparams(("parallel",)),
        name="ctx1",
    )(lamv, subln, qkv)


def _lat1_kernel(lam_ref, subln_ref, q_ref, k_ref, v_ref, ck_ref, cv_ref, o_ref, *, lam_init):
    lo = _lane_lo((1, LANES))
    lam = _diff_lambda(lam_ref, lam_init)
    heads = []
    for h in range(8):
        sl = slice(h * LANES, (h + 1) * LANES)
        heads.append((h, q_ref[:, sl], [ck_ref[:, sl].astype(bf16), k_ref[:, sl]],
                      [cv_ref[:, sl].astype(bf16), v_ref[:, sl]]))
    _diff_heads(heads, o_ref, lam, subln_ref[...], lo, lam_init)


def _lat1(lamv, subln, qkv, ck, cv, lam_init):
    sb = N_PROMPT // SAMPLE_SEQ
    nq = SAMPLE_SEQ // TM
    return pl.pallas_call(
        functools.partial(_lat1_kernel, lam_init=lam_init),
        grid=(N_SAMPLE_BATCH, nq),
        in_specs=[
            pl.BlockSpec((8, HEAD_DIM), lambda b, n: (0, 0)),
            pl.BlockSpec((1, LANES), lambda b, n: (0, 0)),
            pl.BlockSpec((TM, D_MODEL), lambda b, n: (N_PROMPT_TILES + b * nq + n, 0)),
            pl.BlockSpec((SAMPLE_SEQ, D_MODEL), lambda b, n: (sb + b, 1)),
            pl.BlockSpec((SAMPLE_SEQ, D_MODEL), lambda b, n: (sb + b, 2)),
            pl.BlockSpec((None, PAST_LEN, D_MODEL), lambda b, n: (b, 0, 0)),
            pl.BlockSpec((None, PAST_LEN, D_MODEL), lambda b, n: (b, 0, 0)),
        ],
        out_specs=pl.BlockSpec((TM, D_MODEL), lambda b, n: (b * nq + n, 0)),
        out_shape=jax.ShapeDtypeStruct((N_SAMPLE, D_MODEL), bf16),
        compiler_params=_cparams(("parallel", "arbitrary")),
        name="lat1",
    )(lamv, subln, qkv, qkv, qkv, ck, cv)


def _split_bf16(a):
    hi = a.astype(bf16)
    return hi, (a - hi.astype(f32)).astype(bf16)


def _route_kernel(*refs, n_x):
    x_refs = refs[:n_x]
    (op_ref, os_ref, mod_ref, g_ref, wo_ref, rwt_ref, rb_ref,
     xnew_ref, xloc_ref, slots_ref, gate_ref, len_ref) = refs[n_x:]
    logits = [_route_logits(t, x_refs, op_ref, os_ref, mod_ref, g_ref, wo_ref, rwt_ref, xnew_ref)
              for t in range(ROUTE_TILES)]
    tiles = _route_tiles(jnp.concatenate([lg for _, lg in logits], axis=1), rb_ref, slots_ref, gate_ref, len_ref)
    for t, (slots, run_len) in enumerate(tiles):
        _route_dispatch(t, slots, logits[t][0], run_len, xloc_ref)


def _route_logits(t, x_refs, op_ref, os_ref, mod_ref, g_ref, wo_ref, rwt_ref, xnew_ref):
    is_prompt = pl.program_id(0) < N_PROMPT_TILES // ROUTE_TILES
    rows = slice(t * TM, (t + 1) * TM)
    attn = jnp.where(is_prompt, op_ref[rows, :], os_ref[rows, :])
    x_in = x_refs[0][rows, :] if len(x_refs) == 1 else jnp.where(is_prompt, x_refs[0][rows, :], x_refs[1][rows, :])
    x = x_in + mod_ref[2:3, :] * _dot(attn, wo_ref[...])
    xnew_ref[rows, :] = x
    h = _norm_mod(x, g_ref[...], mod_ref[4:5, :], mod_ref[3:4, :])
    h_hi, h_lo = _split_bf16(h)
    w_hi, w_lo = _split_bf16(rwt_ref[...])
    return h_hi, _dot_nt(w_hi, h_hi) + (_dot_nt(w_hi, h_lo) + _dot_nt(w_lo, h_hi))


def _route_tiles(logits, rb_ref, slots_ref, gate_ref, len_ref):
    ng, ge = N_GROUPS, N_EXPERTS // N_GROUPS
    n = ROUTE_TILES * TM
    tile = lambda a, t: a[..., t * TM:(t + 1) * TM]
    scores = jax.nn.sigmoid(logits)
    biased = scores + rb_ref[...]
    s3 = scores.reshape(ng, ge, n)
    b3 = biased.reshape(ng, ge, n)
    in_group = lax.broadcasted_iota(i32, (ng, ge, n), 1).astype(f32)
    group_id = lax.broadcasted_iota(i32, (ng, 1, n), 0).astype(f32)
    expert_id = lax.broadcasted_iota(i32, (ng, ge, n), 0).astype(f32) * ge + in_group

    def max01(a):
        return jnp.max(jnp.max(a, axis=0, keepdims=True), axis=1, keepdims=True)

    def min01(a):
        return jnp.min(jnp.min(a, axis=0, keepdims=True), axis=1, keepdims=True)

    def sum01(a):
        return jnp.sum(jnp.sum(a, axis=0, keepdims=True), axis=1, keepdims=True)

    m1 = jnp.max(b3, axis=1, keepdims=True)
    first = jnp.min(jnp.where(b3 == m1, in_group, ge), axis=1, keepdims=True)
    m2 = jnp.max(jnp.where(in_group == first, -jnp.inf, b3), axis=1, keepdims=True)
    gscore = m1 + m2
    gsel = jnp.zeros((ng, 1, n), f32)
    for _ in range(TOPK_GROUPS):
        gm = jnp.max(gscore, axis=0, keepdims=True)
        gi = jnp.min(jnp.where(gscore == gm, group_id, ng), axis=0, keepdims=True)
        hit = group_id == gi
        gsel = jnp.where(hit, 1.0, gsel)
        gscore = jnp.where(hit, -jnp.inf, gscore)
    cand = jnp.where(jnp.broadcast_to(gsel, (ng, ge, n)) > 0.0, b3, -jnp.inf)
    top_e, top_w = [], []
    for _ in range(TOP_K):
        em = max01(cand)
        ei = min01(jnp.where(cand == em, expert_id, N_EXPERTS))
        hit = expert_id == ei
        top_e.append(ei)
        top_w.append(sum01(jnp.where(hit, s3, 0.0)))
        cand = jnp.where(hit, -jnp.inf, cand)
    wsum = functools.reduce(lambda a, b: a + b, top_w)
    sel3 = jnp.zeros((ng, ge, n), f32)
    for k, (ei, w) in enumerate(zip(top_e, top_w)):
        gate = (w / wsum * ROUTED_SCALE).reshape(1, n)
        for t in range(ROUTE_TILES):
            gate_ref[t, k:k + 1, :] = tile(gate, t)
        sel3 = jnp.where(expert_id == ei, 1.0, sel3)
    sel = sel3.reshape(N_EXPERTS, n)

    r_i = lax.broadcasted_iota(i32, (N_EXPERTS, N_EXPERTS), 0)
    c_i = lax.broadcasted_iota(i32, (N_EXPERTS, N_EXPERTS), 1)
    lower = jnp.where(c_i < r_i, 1.0, 0.0).astype(bf16)
    run_lens, run_offs = [], []
    for t in range(ROUTE_TILES):
        cnt = jnp.sum(tile(sel, t), axis=1, keepdims=True)
        run_len = jnp.ceil(cnt * (1.0 / CHUNK)) * CHUNK
        run_off = _dot(lower, jnp.broadcast_to(run_len, (N_EXPERTS, LANES)).astype(bf16))[:, 0:1]
        run_lens.append(run_len)
        run_offs.append(jnp.broadcast_to(run_off, (N_EXPERTS, TM)))
    t_r = lax.broadcasted_iota(i32, (n, n), 0)
    t_c = lax.broadcasted_iota(i32, (n, n), 1)
    before = jnp.where(jnp.logical_and(t_r < t_c, t_r // TM == t_c // TM), 1.0, 0.0).astype(bf16)
    rank = _dot(sel.astype(bf16), before)
    slot3 = (jnp.concatenate(run_offs, axis=1) + rank).reshape(ng, ge, n)
    slots = [sum01(jnp.where(expert_id == ei, slot3, 0.0)).reshape(1, n).astype(i32) for ei in top_e]
    out = []
    for t in range(ROUTE_TILES):
        for k in range(TOP_K):
            slots_ref[t, k:k + 1, :] = tile(slots[k], t)
        slots_ref[t, TOP_K:8, :] = jnp.full((8 - TOP_K, TM), -1, i32)
        gate_ref[t, TOP_K:8, :] = jnp.zeros((8 - TOP_K, TM), f32)
        cnt_row = _dot_nt(jnp.ones((8, TM), bf16), tile(sel, t).astype(bf16))
        len_ref[t] = (jnp.ceil(cnt_row * (1.0 / CHUNK)) * CHUNK).astype(i32)
        out.append(([tile(sl, t) for sl in slots], run_lens[t]))
    return out


def _route_dispatch(t, slots, h_hi, run_len, xloc_ref):
    rows = ROUTE_ROWS

    def body(c, carry):
        base = pl.multiple_of(c * rows, rows)
        row_id = base.astype(jnp.int16) + lax.broadcasted_iota(jnp.int16, (rows, TM), 0)
        p = jnp.zeros((rows, TM), bf16)
        for k in range(TOP_K):
            p = jnp.where(row_id == slots[k].astype(jnp.int16), jnp.ones((), bf16), p)
        xloc_ref[pl.ds(t * SLOTS + base, rows), :] = _dot(p, h_hi).astype(bf16)
        return carry

    def zero_body(c, carry):
        base = pl.multiple_of(c * rows, rows)
        xloc_ref[pl.ds(t * SLOTS + base, rows), :] = jnp.zeros((rows, D_MODEL), bf16)
        return carry

    n_used = (jnp.sum(run_len).astype(i32) + (rows - 1)) // rows
    lax.fori_loop(0, n_used, body, 0)
    lax.fori_loop(n_used, SLOTS // rows, zero_body, 0)


def _route(x_parts, o_prompt, o_sample, mod_l, g, w_out, rwt, rb):
    per_tile = lambda i: (i, 0, 0)
    rt = ROUTE_TILES
    tm = rt * TM
    return pl.pallas_call(
        functools.partial(_route_kernel, n_x=len(x_parts)),
        grid=(N_TILES // rt,),
        in_specs=_x_specs(x_parts, tm) + [
            pl.BlockSpec((tm, D_MODEL), lambda i: _prompt_block(i, tm)),
            pl.BlockSpec((tm, D_MODEL), lambda i: _sample_block(i, tm)),
            pl.BlockSpec((None, 6, D_MODEL), lambda i: (_mod_row(i, tm), 0, 0)),
            pl.BlockSpec((1, D_MODEL), lambda i: (0, 0)),
            pl.BlockSpec((D_MODEL, D_MODEL), lambda i: (0, 0)),
            pl.BlockSpec((N_EXPERTS, D_MODEL), lambda i: (0, 0)),
            pl.BlockSpec((N_EXPERTS, 1), lambda i: (0, 0)),
        ],
        out_specs=[
            pl.BlockSpec((tm, D_MODEL), lambda i: (i, 0)),
            pl.BlockSpec((rt * SLOTS, D_MODEL), lambda i: (i, 0)),
            pl.BlockSpec((rt, 8, TM), per_tile),
            pl.BlockSpec((rt, 8, TM), per_tile),
            pl.BlockSpec((rt, 8, N_EXPERTS), per_tile),
        ],
        out_shape=[
            jax.ShapeDtypeStruct((N_TOK, D_MODEL), f32),
            jax.ShapeDtypeStruct((N_TILES * SLOTS, D_MODEL), bf16),
            jax.ShapeDtypeStruct((N_TILES, 8, TM), i32),
            jax.ShapeDtypeStruct((N_TILES, 8, TM), f32),
            jax.ShapeDtypeStruct((N_TILES, 8, N_EXPERTS), i32),
        ],
        compiler_params=_cparams(("parallel",)),
        name="route",
    )(*x_parts, o_prompt, o_sample, mod_l, g, w_out, rwt, rb)


def _moe_plan(run_len):
    nt, ne = run_len.shape

    def excl_cumsum(a):
        n = a.shape[-1]
        earlier = np.arange(n)[None, :] < np.arange(n)[:, None]
        return jnp.sum(jnp.where(earlier, a[..., None, :], 0), axis=-1)

    def first_diff(a):
        return a - jnp.concatenate([jnp.zeros_like(a[..., :1]), a[..., :-1]], axis=-1)

    off_loc = excl_cumsum(run_len)
    before = excl_cumsum(run_len.T).T
    n_e = jnp.sum(run_len, axis=0)
    n_pad = -(-n_e // GM) * GM
    g_start = excl_cumsum(n_pad)
    total = jnp.sum(n_pad)
    run_dst = g_start[None, :] + before
    run_src = jnp.arange(nt, dtype=i32)[:, None] * SLOTS + off_loc
    dst_f = run_dst.T.reshape(-1)
    shift_f = first_diff((run_src - run_dst).T.reshape(-1))
    rows = jnp.arange((G_TILES + GMM_ITEM_TILES - 1) * G_CHUNKS, dtype=i32) * CHUNK
    shift = jnp.sum(jnp.where(dst_f[None, :] <= rows[:, None], shift_f[None, :], 0), axis=1)
    in_run = jnp.any((g_start[None, :] <= rows[:, None]) & (rows[:, None] < (g_start + n_e)[None, :]), axis=1)
    chunk_src = (jnp.where(in_run, rows + shift, 0) // CHUNK).astype(i32)
    loc_rows = jnp.arange(SLOT_CHUNKS, dtype=i32) * CHUNK
    shift_l = first_diff(run_dst - off_loc)
    shift = jnp.sum(jnp.where(off_loc[:, None, :] <= loc_rows[None, :, None], shift_l[:, None, :], 0), axis=2)
    used = jnp.sum(run_len, axis=1)
    chunk_map = jnp.where(loc_rows[None, :] < used[:, None], (loc_rows[None, :] + shift) // CHUNK, 0).astype(i32)
    tile_start, n_tiles = g_start // GM, n_pad // GM
    n_items = -(-n_tiles // GMM_ITEM_TILES)
    item_start = excl_cumsum(n_items)
    items = jnp.arange(GMM_MAX_ITEMS, dtype=i32)
    owner = items[:, None] >= item_start[None, :]
    e_first = jnp.sum(jnp.where(owner, first_diff(tile_start - GMM_ITEM_TILES * item_start)[None, :], 0), axis=1)
    item_tile = e_first + GMM_ITEM_TILES * items
    e_end = jnp.sum(jnp.where(owner, first_diff(tile_start + n_tiles)[None, :], 0), axis=1)
    item_cnt = jnp.clip(e_end - item_tile, 0, GMM_ITEM_TILES)
    gmm_plan = tuple(a.astype(i32) for a in (item_start, n_items, item_tile, item_cnt, chunk_src))
    long_tiles = (used > COMBINE_SURE_CHUNKS * CHUNK).astype(i32)
    return gmm_plan, chunk_map.reshape(-1), long_tiles


def _gmm_in_copy(xloc_hbm, xbuf, sem, src_chunk, slot, c):
    return pltpu.make_async_copy(xloc_hbm.at[src_chunk], xbuf.at[slot, c], sem.at[slot])


def _gmm_out_copy(ybuf, y_hbm, sem, tile, slot, n_tiles):
    chunks = n_tiles * G_CHUNKS
    return pltpu.make_async_copy(ybuf.at[slot, pl.ds(0, chunks)],
                                 y_hbm.at[pl.ds(tile * G_CHUNKS, chunks)], sem.at[slot])


def _gmm_kernel(i0_ref, ni_ref, it_ref, ic_ref, cs_ref, xloc_hbm, wg_ref, wu_ref, wd_ref, y_hbm,
                xbuf, ybuf, zbuf, wg_b, wu_b, wd_b, in_sem, out_sem, zsem):
    e = pl.program_id(0)
    last = pl.num_programs(0) - 1
    n_items = ni_ref[e]
    first_item = i0_ref[e]
    total_items = i0_ref[last] + ni_ref[last]
    last_item = total_items - 1
    total_tiles = it_ref[last_item] + ic_ref[last_item]

    def start_in(item):
        first = it_ref[item] * G_CHUNKS
        for c in range(GMM_ITEM_TILES * G_CHUNKS):
            _gmm_in_copy(xloc_hbm, xbuf, in_sem, cs_ref[first + c], item % GMM_SLOTS, c).start()

    def wait_in(item):
        for c in range(GMM_ITEM_TILES * G_CHUNKS):
            _gmm_in_copy(xloc_hbm, xbuf, in_sem, 0, item % GMM_SLOTS, c).wait()

    def out_copy(item, fn):
        for cnt in range(1, GMM_ITEM_TILES + 1):
            @pl.when(ic_ref[item] == cnt)
            def _():
                fn(_gmm_out_copy(ybuf, y_hbm, out_sem, it_ref[item], item % GMM_SLOTS, cnt))

    @pl.when(e == 0)
    def _():
        for item in range(GMM_SLOTS - 1):
            start_in(item)
        zbuf[...] = jnp.zeros(zbuf.shape, zbuf.dtype)

    def tail_copies(fn):
        for j in range(_GMM_TAIL_PER_STEP):
            tile = total_tiles + e + j * N_EXPERTS

            @pl.when(tile < G_TILES)
            def _():
                fn(pltpu.make_async_copy(zbuf, y_hbm.at[pl.ds(tile * G_CHUNKS, G_CHUNKS)], zsem.at[0]))

    tail_copies(lambda cp: cp.start())

    @pl.when(n_items > 0)
    def _():
        wg_b[...] = wg_ref[...].astype(bf16)
        wu_b[...] = wu_ref[...].astype(bf16)
        wd_b[...] = wd_ref[...].astype(bf16)

    def body(j, carry):
        item = first_item + j
        slot = item % GMM_SLOTS

        @pl.when(item + (GMM_SLOTS - 1) < total_items)
        def _():
            start_in(item + (GMM_SLOTS - 1))

        wait_in(item)

        @pl.when(item >= GMM_SLOTS)
        def _():
            out_copy(item - GMM_SLOTS, lambda cp: cp.wait())

        for cnt in range(1, GMM_ITEM_TILES + 1):
            @pl.when(ic_ref[item] == cnt)
            def _():
                rows = cnt * GM
                chunks = cnt * G_CHUNKS
                x = xbuf[slot, 0:chunks].reshape(rows, D_MODEL)
                act = _silu(_dot(x, wg_b[...])) * _dot(x, wu_b[...])
                y = _dot(act.astype(bf16), wd_b[...]).astype(bf16)
                ybuf[slot, 0:chunks] = y.reshape(chunks, CHUNK, D_MODEL)

        out_copy(item, lambda cp: cp.start())
        return carry

    lax.fori_loop(0, n_items, body, 0)
    tail_copies(lambda cp: cp.wait())

    @pl.when(e == last)
    def _():
        for back in range(1, GMM_SLOTS + 1):
            out_copy(total_items - back, lambda cp: cp.wait())


def _gmm(plan, xloc, wg, wu, wd, layer):
    rows = GMM_ITEM_TILES * GM
    w_idx = lambda e, *_: (layer, e, 0, 0)
    grid_spec = pltpu.PrefetchScalarGridSpec(
        num_scalar_prefetch=5,
        grid=(N_EXPERTS,),
        in_specs=[
            pl.BlockSpec(memory_space=pl.ANY),
            pl.BlockSpec((None, None, D_MODEL, EXPERT_DIM), w_idx),
            pl.BlockSpec((None, None, D_MODEL, EXPERT_DIM), w_idx),
            pl.BlockSpec((None, None, EXPERT_DIM, D_MODEL), w_idx),
        ],
        out_specs=pl.BlockSpec(memory_space=pl.ANY),
        scratch_shapes=[pltpu.VMEM((GMM_SLOTS, rows // CHUNK, CHUNK, D_MODEL), bf16),
                        pltpu.VMEM((GMM_SLOTS, rows // CHUNK, CHUNK, D_MODEL), bf16),
                        pltpu.VMEM((G_CHUNKS, CHUNK, D_MODEL), bf16),
                        pltpu.VMEM((D_MODEL, EXPERT_DIM), bf16), pltpu.VMEM((D_MODEL, EXPERT_DIM), bf16),
                        pltpu.VMEM((EXPERT_DIM, D_MODEL), bf16),
                        pltpu.SemaphoreType.DMA((GMM_SLOTS,)), pltpu.SemaphoreType.DMA((GMM_SLOTS,)),
                        pltpu.SemaphoreType.DMA((1,))],
    )
    return pl.pallas_call(
        _gmm_kernel,
        grid_spec=grid_spec,
        out_shape=jax.ShapeDtypeStruct((G_TILES * G_CHUNKS, CHUNK, D_MODEL), bf16),
        compiler_params=_cparams(("arbitrary",)),
        name="gmm",
    )(*plan, xloc, wg, wu, wd)


def _combine_copy(y_hbm, ybuf, sem, sorted_chunk, slot, c):
    return pltpu.make_async_copy(y_hbm.at[sorted_chunk], ybuf.at[slot, c], sem.at[slot])


def _combine_kernel(cm_ref, long_ref, y_hbm, slots_ref, gate_ref, x_ref, mod_ref, g_ref, sg_ref, su_ref, sd_ref,
                    *rest, final):
    if final:
        gf_ref, yp_ref, ys_ref, ybuf, sem = rest
    else:
        o_ref, ybuf, sem = rest
    i = pl.program_id(0)
    n = pl.num_programs(0)
    ahead = COMBINE_SLOTS - 1
    slot = i % COMBINE_SLOTS

    def for_chunks(tile, fn):
        for c in range(COMBINE_SURE_CHUNKS):
            fn(c)

        @pl.when(long_ref[tile] == 1)
        def _():
            for c in range(COMBINE_SURE_CHUNKS, SLOT_CHUNKS):
                fn(c)

    def start(tile, s):
        for_chunks(tile, lambda c: _combine_copy(y_hbm, ybuf, sem, cm_ref[tile * SLOT_CHUNKS + c], s, c).start())

    def wait(tile, s):
        for_chunks(tile, lambda c: _combine_copy(y_hbm, ybuf, sem, 0, s, c).wait())

    @pl.when(i == 0)
    def _():
        ybuf[:, COMBINE_SURE_CHUNKS:] = jnp.zeros((COMBINE_SLOTS, SLOT_CHUNKS - COMBINE_SURE_CHUNKS, CHUNK, D_MODEL), bf16)
        for tile in range(ahead):
            start(tile, tile)

    wait(i, slot)
    start((i + ahead) % n, (i + ahead) % COMBINE_SLOTS)

    x = x_ref[...]
    hb = _norm_mod(x, g_ref[...], mod_ref[4:5, :], mod_ref[3:4, :]).astype(bf16)
    shared = _dot((_silu(_dot(hb, sg_ref[...])) * _dot(hb, su_ref[...])).astype(bf16), sd_ref[...])
    row_id = lax.broadcasted_iota(jnp.int16, (SLOTS, TM), 0)
    p = jnp.zeros((SLOTS, TM), bf16)
    for k in range(TOP_K):
        p = jnp.where(row_id == slots_ref[k:k + 1, :].astype(jnp.int16), gate_ref[k:k + 1, :].astype(bf16), p)
    routed = lax.dot_general(p, ybuf[slot].reshape(SLOTS, D_MODEL), (((0,), (0,)), ((), ())),
                             preferred_element_type=f32)
    out = x + mod_ref[5:6, :] * (routed + shared)
    if final:
        y = (out * lax.rsqrt(jnp.mean(out * out, axis=-1, keepdims=True) + EPS)) * gf_ref[...]

        @pl.when(i < N_PROMPT_TILES)
        def _():
            yp_ref[...] = y

        @pl.when(i >= N_PROMPT_TILES)
        def _():
            ys_ref[...] = y
    else:
        o_ref[...] = out

    @pl.when(i == n - 1)
    def _():
        for k in range(1, ahead + 1):
            wait((i + k) % n, (i + k) % COMBINE_SLOTS)


def _combine(chunk_map, long_tiles, y, slots, gates, x, mod_l, g, sg, su, sd, final_g=None):
    shd = sg.shape[1]
    final = final_g is not None
    row_spec = pl.BlockSpec((TM, D_MODEL), lambda i, *_: (i, 0))
    vec_spec = pl.BlockSpec((1, D_MODEL), lambda i, *_: (0, 0))
    if final:
        out_specs = [pl.BlockSpec((TM, D_MODEL), lambda i, *_: _prompt_block(i)),
                     pl.BlockSpec((TM, D_MODEL), lambda i, *_: _sample_block(i))]
        out_shape = [jax.ShapeDtypeStruct((N_PROMPT, D_MODEL), f32), jax.ShapeDtypeStruct((N_SAMPLE, D_MODEL), f32)]
    else:
        out_specs, out_shape = row_spec, jax.ShapeDtypeStruct((N_TOK, D_MODEL), f32)
    grid_spec = pltpu.PrefetchScalarGridSpec(
        num_scalar_prefetch=2,
        grid=(N_TILES,),
        in_specs=[
            pl.BlockSpec(memory_space=pl.ANY),
            pl.BlockSpec((None, 8, TM), lambda i, *_: (i, 0, 0)),
            pl.BlockSpec((None, 8, TM), lambda i, *_: (i, 0, 0)),
            row_spec,
            pl.BlockSpec((None, 6, D_MODEL), lambda i, *_: (_mod_row(i), 0, 0)),
            vec_spec,
            pl.BlockSpec((D_MODEL, shd), lambda i, *_: (0, 0)),
            pl.BlockSpec((D_MODEL, shd), lambda i, *_: (0, 0)),
            pl.BlockSpec((shd, D_MODEL), lambda i, *_: (0, 0)),
        ] + ([vec_spec] if final else []),
        out_specs=out_specs,
        scratch_shapes=[pltpu.VMEM((COMBINE_SLOTS, SLOT_CHUNKS, CHUNK, D_MODEL), bf16),
                        pltpu.SemaphoreType.DMA((COMBINE_SLOTS,))],
    )
    args = (chunk_map, long_tiles, y, slots, gates, x, mod_l, g, sg, su, sd) + ((final_g,) if final else ())
    return pl.pallas_call(
        functools.partial(_combine_kernel, final=final),
        grid_spec=grid_spec,
        out_shape=out_shape,
        compiler_params=_cparams(("arbitrary",)),
        name="combine",
    )(*args)


def _moe(x_parts, o_prompt, o_sample, w_out, mod_l, g, rwt, rb, wg, wu, wd, layer, sg, su, sd, final_g=None):
    x, xloc, slots, gates, run_len = _route(x_parts, o_prompt, o_sample, mod_l, g, w_out, rwt, rb)
    gmm_plan, chunk_map, long_tiles = _moe_plan(run_len[:, 0, :])
    y = _gmm(gmm_plan, xloc.reshape(N_TILES * SLOT_CHUNKS, CHUNK, D_MODEL), wg, wu, wd, layer)
    return _combine(chunk_map, long_tiles, y, slots, gates, x, mod_l, g, sg, su, sd, final_g)


_L0_CHUNKS = (
    (0, 512, 0, (0, 1, 2, 3), ()),
    (512, 1024, 768, (), ()),
    (1024, 1536, 1280, (), ((2, 0, 512, 0, False),)),
    (1536, 2048, 1792, (), ((3, 0, 512, 0, False),)),
    (2048, 2304, 512, (0,), ((0, 0, 128, 0, False), (1, 128, 256, 0, False))),
)
_L0_KV_OUTS = (("T", 128), ("T", 128), ("T", 512), ("T", 512))
_L1_CHUNKS = (
    (0, 512, 0, (0, 1, 2, 3), ()),
    (512, 1024, 512, (0, 1, 2, 3), ()),
    (1024, 1536, 1024, (0, 1, 2, 3), ((0, 0, 512, 0, False),)),
    (1536, 2048, 1536, (0, 1, 2, 3), ((0, 0, 512, 512, False),)),
    (2048, 2560, 2048, (), ((1, 0, 512, 0, True),)),
    (2560, 3072, 2560, (), ((1, 0, 512, 512, True),)),
)
_L1_KV_OUTS = (("T", 1024), ("H", 8))


def _from_feature_major(kt, *head_dims):
    nb, _, s = kt.shape
    nd = len(head_dims)
    return kt.reshape(nb, *head_dims, s).transpose(0, nd + 1, *range(1, nd + 1))[:, None]


def kernel(x_prompt, x_sample, cache_a_k, cache_a_v, cache_b_k, cache_b_v, cache_c_k, cache_c_v, c, c_ctx, w_mod, b_mod, norm_mix, norm_ffn, w_in_ab, w_out_ab, sink_a, rel_bias_b, w_in_c, w_out_c, lam_q1, lam_k1, lam_q2, lam_k2, subln_c, router_w, router_bias, exp_w_gate, exp_w_up, exp_w_down, sh_w_gate, sh_w_up, sh_w_down, final_norm):
    x = (x_prompt.reshape(N_PROMPT, D_MODEL), x_sample.reshape(N_SAMPLE, D_MODEL))
    cond8 = jnp.concatenate([c_ctx[None, :], c, jnp.zeros((8 - 1 - N_SAMPLE_BATCH, D_MODEL), f32)], axis=0)
    mod = _adaln(cond8, w_mod, b_mod).reshape(DEPTH, 8, 6, D_MODEL)
    rope_tabs = _rope_tables()
    new_kv = {}
    for layer in range(DEPTH):
        li = layer // 2
        mod_l = mod[layer]
        g_mix = norm_mix[layer][None, :]
        g_ffn = norm_ffn[layer][None, :]
        if layer % 2 == 0:
            w_in = w_in_ab[li].astype(bf16)
            qkv, ak, av, bk, bv = _inproj(x, mod_l, g_mix, w_in, rope_tabs, _L0_CHUNKS, _L0_KV_OUTS)
            new_kv["a_k"], new_kv["a_v"], new_kv["b_k"], new_kv["b_v"] = ak, av, bk, bv
            o_p = _ctx0(sink_a[li], qkv)
            o_s = _lat0(sink_a[li], qkv,
                        cache_a_k[:, li].reshape(N_SAMPLE_BATCH, PAST_LEN, LANES),
                        cache_a_v[:, li].reshape(N_SAMPLE_BATCH, PAST_LEN, LANES),
                        cache_b_k[:, li].reshape(N_SAMPLE_BATCH, PAST_LEN, 512),
                        cache_b_v[:, li].reshape(N_SAMPLE_BATCH, PAST_LEN, 512),
                        _na_bias_tiles(rel_bias_b[li]))
            w_out = w_out_ab[li].astype(bf16)
        else:
            lam_init = 0.8 - 0.6 * math.exp(-0.3 * layer)
            qkv, ck, cv = _inproj(x, mod_l, g_mix, w_in_c[li].astype(bf16), rope_tabs, _L1_CHUNKS, _L1_KV_OUTS)
            new_kv["c_k"], new_kv["c_v"] = ck, cv
            lamv = jnp.concatenate([lam_q1[li][None], lam_k1[li][None], lam_q2[li][None], lam_k2[li][None],
                                    jnp.zeros((4, HEAD_DIM), f32)], axis=0)
            subln = subln_c[li][None, :]
            o_p = _ctx1(lamv, subln, qkv, lam_init)
            o_s = _lat1(lamv, subln, qkv,
                        cache_c_k[:, li].reshape(N_SAMPLE_BATCH, PAST_LEN, D_MODEL),
                        cache_c_v[:, li].reshape(N_SAMPLE_BATCH, PAST_LEN, D_MODEL), lam_init)
            w_out = w_out_c[li].astype(bf16)
        last = layer == DEPTH - 1
        x = _moe(x, o_p, o_s, w_out, mod_l, g_ffn, router_w[layer].T, router_bias[layer][:, None],
                 exp_w_gate, exp_w_up, exp_w_down, layer,
                 sh_w_gate[layer].astype(bf16), sh_w_up[layer].astype(bf16), sh_w_down[layer].astype(bf16),
                 final_norm[None, :] if last else None)
        x = x if last else (x,)
    y_prompt, y_sample = x
    nb, s = N_PROMPT_BATCH, PROMPT_SEQ
    return (y_prompt.reshape(nb, s, D_MODEL), y_sample.reshape(N_SAMPLE_BATCH, SAMPLE_SEQ, D_MODEL),
            _from_feature_major(new_kv["a_k"], 2, HEAD_DIM), _from_feature_major(new_kv["a_v"], 2, HEAD_DIM),
            _from_feature_major(new_kv["b_k"], 8, HEAD_DIM), _from_feature_major(new_kv["b_v"], 8, HEAD_DIM),
            _from_feature_major(new_kv["c_k"], 8, 2, HEAD_DIM), new_kv["c_v"].reshape(nb, 1, s, 8, 2 * HEAD_DIM))
```

```python
import functools
import math

import jax
import jax.numpy as jnp
import numpy as np
from jax import lax
from jax.experimental import pallas as pl
from jax.experimental.pallas import tpu as pltpu

f32 = jnp.float32
bf16 = jnp.bfloat16
i32 = jnp.int32

D_MODEL = 1024
N_PROMPT_BATCH = 16
PROMPT_SEQ = 256
DEPTH = 2
N_SAMPLE_BATCH = 2
SAMPLE_SEQ = 2048
PAST_LEN = 512
GRID_W = 64
HEAD_DIM = 64
ROPE_THETA = 10000.0
EPS = 1e-6
A_WINDOW = 128
NA_ROWS = 8
NA_COLS = 16
N_EXPERTS = 64
TOP_K = 6
N_GROUPS = 8
TOPK_GROUPS = 4
EXPERT_DIM = 256
ROUTED_SCALE = 2.5
Q_SCALE = HEAD_DIM ** -0.5

N_PROMPT = N_PROMPT_BATCH * PROMPT_SEQ
N_SAMPLE = N_SAMPLE_BATCH * SAMPLE_SEQ
N_TOK = N_PROMPT + N_SAMPLE

LANES = 128
TM = 256
TD = 512
N_PROMPT_TILES = N_PROMPT // TM
N_TILES = N_TOK // TM
QB = 128
CHUNK = 16
SLOTS = -(-(TM * TOP_K + N_EXPERTS * (CHUNK - 1)) // 256) * 256
SLOT_CHUNKS = SLOTS // CHUNK
ROUTE_ROWS = 512
ROUTE_TILES = 2
COMBINE_SLOTS = 3
COMBINE_SURE_CHUNKS = 136
GM = 256
_MAX_SORTED = TM * TOP_K * N_TILES + N_TILES * N_EXPERTS * (CHUNK - 1) + N_EXPERTS * (GM - CHUNK)
G_TILES = -(-_MAX_SORTED // GM)
G_CHUNKS = GM // CHUNK
GMM_ITEM_TILES = 2
GMM_SLOTS = 4
GMM_MAX_ITEMS = (G_TILES + N_EXPERTS * (GMM_ITEM_TILES - 1)) // GMM_ITEM_TILES
_GMM_TAIL_PER_STEP = -(-(G_TILES - TM * TOP_K * N_TILES // GM) // N_EXPERTS)
ADA_COLS = 1536
DIFF_HEADS_PER_PASS = 4
VMEM_LIMIT = 56 * 1024 * 1024

NEG = -1e30


def _cparams(sem):
    return pltpu.CompilerParams(dimension_semantics=sem, vmem_limit_bytes=VMEM_LIMIT)


def _mod_row(i, tm=TM):
    return jnp.where(i < N_PROMPT // tm, 0, 1 + (i - N_PROMPT // tm) // (SAMPLE_SEQ // tm))


def _prompt_block(i, tm=TM):
    return (jnp.minimum(i, N_PROMPT // tm - 1), 0)


def _sample_block(i, tm=TM):
    return (jnp.maximum(i - N_PROMPT // tm, 0), 0)


def _x_specs(parts, tm=TM):
    if len(parts) == 1:
        return [pl.BlockSpec((tm, D_MODEL), lambda i, *_: (i, 0))]
    return [pl.BlockSpec((tm, D_MODEL), lambda i, *_: _prompt_block(i, tm)),
            pl.BlockSpec((tm, D_MODEL), lambda i, *_: _sample_block(i, tm))]


def _load_x(i, x_refs, tm=TM):
    if len(x_refs) == 1:
        return x_refs[0][...]
    return jnp.where(i < N_PROMPT // tm, x_refs[0][...], x_refs[1][...])


def _norm_mod(x, g, scale, shift):
    y = x * lax.rsqrt(jnp.mean(x * x, axis=-1, keepdims=True) + EPS)
    return (y * g) * (1.0 + scale) + shift


def _silu(x):
    return x * jax.nn.sigmoid(x)


def _dot(a, b):
    return jnp.dot(a, b, preferred_element_type=f32)


def _dot_nt(a, b):
    return lax.dot_general(a, b, (((1,), (1,)), ((), ())), preferred_element_type=f32)


def _adaln_kernel(cond_ref, w_ref, b_ref, o_ref):
    s = _silu(cond_ref[...]).astype(bf16)
    o_ref[...] = _dot(s, w_ref[...].astype(bf16)) + b_ref[...]


def _adaln(cond8, w_mod, b_mod):
    n6 = 6 * D_MODEL
    return pl.pallas_call(
        _adaln_kernel,
        grid=(DEPTH, n6 // ADA_COLS),
        in_specs=[
            pl.BlockSpec((8, D_MODEL), lambda l, j: (0, 0)),
            pl.BlockSpec((None, D_MODEL, ADA_COLS), lambda l, j: (l, 0, j)),
            pl.BlockSpec((None, 1, ADA_COLS), lambda l, j: (l, 0, j)),
        ],
        out_specs=pl.BlockSpec((None, 8, ADA_COLS), lambda l, j: (l, 0, j)),
        out_shape=jax.ShapeDtypeStruct((DEPTH, 8, n6), f32),
        compiler_params=_cparams(("parallel", "parallel")),
        name="adaln",
    )(cond8, w_mod, b_mod.reshape(DEPTH, 1, n6))


def _rope_block(blk, cos, sin_a, sin_b):
    return blk * cos + pltpu.roll(blk, LANES - 16, 1) * sin_a + pltpu.roll(blk, 16, 1) * sin_b


def _inproj_kernel(*refs, chunks, n_x):
    x_refs, kv_refs = refs[:n_x], refs[n_x + 7:]
    mod_ref, g_ref, w_ref, cos_ref, sa_ref, sb_ref, qkv_ref = refs[n_x:n_x + 7]
    i = pl.program_id(0)
    h = _norm_mod(_load_x(i, x_refs, TD), g_ref[...], mod_ref[1:2, :], mod_ref[0:1, :]).astype(bf16)
    is_prompt = i < N_PROMPT // TD

    @pl.when(is_prompt)
    def _():
        for c0, c1, s0, _, kv_out in chunks:
            acc = _dot(h, w_ref[:, s0:s0 + (c1 - c0)])
            qkv_ref[:, c0:c1] = acc.astype(bf16)
            for ridx, a0, a1, o0, per_head in kv_out:
                if per_head:
                    heads = kv_refs[ridx].shape[0] // TD
                    for j in range((a1 - a0) // LANES):
                        kv_refs[ridx][pl.ds(o0 // LANES + j, TD, stride=heads), :] = (
                            acc[:, a0 + j * LANES:a0 + (j + 1) * LANES])
                else:
                    t = acc[:, a0:a1].T
                    for b in range(TD // PROMPT_SEQ):
                        kv_refs[ridx][b, o0:o0 + (a1 - a0), :] = t[:, b * PROMPT_SEQ:(b + 1) * PROMPT_SEQ]

    @pl.when(jnp.logical_not(is_prompt))
    def _():
        cos, sa, sb = cos_ref[...], sa_ref[...], sb_ref[...]
        for c0, c1, s0, rope_blocks, _ in chunks:
            acc = _dot(h, w_ref[:, s0:s0 + (c1 - c0)])
            for b in range((c1 - c0) // LANES):
                blk = acc[:, b * LANES:(b + 1) * LANES]
                if b in rope_blocks:
                    blk = _rope_block(blk, cos, sa, sb)
                qkv_ref[:, c0 + b * LANES:c0 + (b + 1) * LANES] = blk.astype(bf16)


def _inproj(x_parts, mod_l, g, w, rope_tabs, chunks, kv_outs):
    n = w.shape[1]
    cos, sa, sb = rope_tabs
    bpt = TD // PROMPT_SEQ

    def rope_idx(i):
        return (jnp.where(i < N_PROMPT // TD, 0, (i - N_PROMPT // TD) % (SAMPLE_SEQ // TD)), 0)

    kv_specs, kv_shapes = [], []
    for kind, size in kv_outs:
        if kind == "T":
            kv_specs.append(pl.BlockSpec((bpt, size, PROMPT_SEQ), lambda i: _prompt_block(i, TD) + (0,)))
            kv_shapes.append(jax.ShapeDtypeStruct((N_PROMPT_BATCH, size, PROMPT_SEQ), f32))
        else:
            kv_specs.append(pl.BlockSpec((TD * size, LANES), lambda i: _prompt_block(i, TD)))
            kv_shapes.append(jax.ShapeDtypeStruct((N_PROMPT * size, LANES), f32))

    return pl.pallas_call(
        functools.partial(_inproj_kernel, chunks=chunks, n_x=len(x_parts)),
        grid=(N_TOK // TD,),
        in_specs=_x_specs(x_parts, TD) + [
            pl.BlockSpec((None, 6, D_MODEL), lambda i: (_mod_row(i, TD), 0, 0)),
            pl.BlockSpec((1, D_MODEL), lambda i: (0, 0)),
            pl.BlockSpec((D_MODEL, n), lambda i: (0, 0)),
            pl.BlockSpec((TD, LANES), rope_idx),
            pl.BlockSpec((TD, LANES), rope_idx),
            pl.BlockSpec((TD, LANES), rope_idx),
        ],
        out_specs=[pl.BlockSpec((TD, n), lambda i: (i, 0))] + kv_specs,
        out_shape=[jax.ShapeDtypeStruct((N_TOK, n), bf16)] + kv_shapes,
        compiler_params=_cparams(("arbitrary",)),
        name="inproj",
    )(*x_parts, mod_l, g, w, cos, sa, sb)


def _rope_tables():
    nq = HEAD_DIM // 4
    t = np.arange(SAMPLE_SEQ)
    inv = np.power(np.float32(ROPE_THETA), -np.arange(nq, dtype=np.float32) / np.float32(nq))
    ang_r = (t // GRID_W).astype(np.float32)[:, None] * inv
    ang_c = (t % GRID_W).astype(np.float32)[:, None] * inv
    zero = np.zeros_like(ang_r)

    def head(fr, fc):
        return np.concatenate([fr[0], fr[1], fc[0], fc[1]], axis=-1)

    cos = head((np.cos(ang_r), np.cos(ang_r)), (np.cos(ang_c), np.cos(ang_c)))
    sin_a = head((-np.sin(ang_r), zero), (-np.sin(ang_c), zero))
    sin_b = head((zero, np.sin(ang_r)), (zero, np.sin(ang_c)))
    two = lambda a: jnp.asarray(np.concatenate([a, a], axis=-1).astype(np.float32))
    return two(cos), two(sin_a), two(sin_b)


def _lane_lo(shape):
    return lax.broadcasted_iota(i32, shape, len(shape) - 1) < HEAD_DIM


def _half(q, lo_mask, half):
    keep = lo_mask if half == 0 else jnp.logical_not(lo_mask)
    return jnp.where(keep, q, jnp.zeros_like(q)) * Q_SCALE


def _swap_halves(x):
    return pltpu.roll(x.astype(f32), HEAD_DIM, 1).astype(x.dtype)


def _stack_halves(q, lo_mask):
    return jnp.concatenate([_half(q, lo_mask, 0), _half(q, lo_mask, 1)], axis=0)


def _with_ones(v):
    return jnp.concatenate([v, jnp.ones_like(v)], axis=1)


def _attend_many(problems):
    scores = [[_dot_nt(q_rows, k) for k in key_blocks] for q_rows, _, key_blocks, _, _, _ in problems]
    exps, maxes = [], []
    for (q_rows, n_heads, key_blocks, _, fix_scores, sinks), sc in zip(problems, scores):
        r = q_rows.shape[0] // n_heads
        e_p, m_p = [[] for _ in key_blocks], []
        for h in range(n_heads):
            blocks = [s[h * r:(h + 1) * r] for s in sc]
            if fix_scores is not None:
                blocks = [fix_scores(h, i, s) for i, s in enumerate(blocks)]
            m = functools.reduce(jnp.maximum, [jnp.max(s, axis=-1, keepdims=True) for s in blocks])
            if sinks is not None:
                m = jnp.maximum(m, sinks[h])
            m_p.append(m)
            for i, s in enumerate(blocks):
                e_p[i].append(jnp.exp((s - m).astype(bf16)))
        exps.append(e_p)
        maxes.append(m_p)
    outs = [functools.reduce(lambda a, b: a + b,
                             [_dot(e[0] if n_heads == 1 else jnp.concatenate(e, axis=0), vx)
                              for e, vx in zip(e_p, vx_blocks)])
            for (_, n_heads, _, vx_blocks, _, _), e_p in zip(problems, exps)]
    results = []
    for (q_rows, n_heads, _, _, _, sinks), out, m_p in zip(problems, outs, maxes):
        r = q_rows.shape[0] // n_heads
        res = []
        for h in range(n_heads):
            den = out[h * r:(h + 1) * r, LANES:]
            if sinks is not None:
                den = den + jnp.exp(sinks[h] - m_p[h])
            res.append(out[h * r:(h + 1) * r, :LANES] * (1.0 / den))
        results.append(res)
    return results


def _gqa_rows(q_blocks, group, lo_mask):
    parts = []
    for q in q_blocks:
        for half in range(2):
            qh = _half(q, lo_mask, half)
            parts.append(qh if half == group else _swap_halves(qh))
    return jnp.concatenate(parts, axis=0)


def _gqa_merge(outs, group, lo_mask):
    fixed = [o if idx % 2 == group else pltpu.roll(o, HEAD_DIM, 1) for idx, o in enumerate(outs)]
    return [jnp.where(lo_mask, fixed[2 * p], fixed[2 * p + 1]) for p in range(len(outs) // 2)]


L0_QA, L0_QB, L0_KB, L0_VB, L0_KA, L0_VA, L0_N = 0, 512, 1024, 1536, 2048, 2176, 2304


def _ctx0_kernel(sink_ref, qkv_ref, o_ref):
    lo = _lane_lo((1, LANES))
    blk = lambda base, j: qkv_ref[:, base + j * LANES:base + (j + 1) * LANES]
    k_a = blk(L0_KA, 0)
    vx_a = _with_ones(blk(L0_VA, 0))
    problems = []
    for g in range(2):
        q_rows = _gqa_rows([blk(L0_QA, 2 * g), blk(L0_QA, 2 * g + 1)], g, lo)
        problems.append((q_rows, 4, [k_a], [vx_a], None, [sink_ref[4 * g + idx] for idx in range(4)]))
    for j in range(4):
        problems.append((_stack_halves(blk(L0_QB, j), lo), 2, [blk(L0_KB, j)], [_with_ones(blk(L0_VB, j))],
                         None, None))
    results = _attend_many(problems)
    for g in range(2):
        for p, o in enumerate(_gqa_merge(results[g], g, lo)):
            j = 2 * g + p
            o_ref[:, j * LANES:(j + 1) * LANES] = o.astype(bf16)
    for j in range(4):
        outs = results[2 + j]
        o_ref[:, 512 + j * LANES:512 + (j + 1) * LANES] = jnp.where(lo, outs[0], outs[1]).astype(bf16)


def _ctx0(sink, qkv):
    return pl.pallas_call(
        _ctx0_kernel,
        grid=(N_PROMPT_BATCH,),
        in_specs=[
            pl.BlockSpec(memory_space=pltpu.SMEM),
            pl.BlockSpec((PROMPT_SEQ, L0_N), lambda b: (b, 0)),
        ],
        out_specs=pl.BlockSpec((PROMPT_SEQ, D_MODEL), lambda b: (b, 0)),
        out_shape=jax.ShapeDtypeStruct((N_PROMPT, D_MODEL), bf16),
        compiler_params=_cparams(("parallel",)),
        name="ctx0",
    )(sink, qkv)


WIN_KEYS = 3 * QB
NA_KEY_ROWS = 10
NA_KEYS = NA_KEY_ROWS * GRID_W
N_QB = SAMPLE_SEQ // QB
N_NA_PATTERNS = 5
_PROMPT_QBLOCKS = N_PROMPT // QB


def _na_pattern(n):
    return jnp.where(n < 2, n, jnp.where(n > N_QB - 3, n - (N_QB - 5), 2))


def _lat0_kernel(sink_ref, roff_ref, q_ref, kvb_ref, kva_ref, cak_ref, cav_ref, cbk_ref, cbv_ref, tiles_ref, o_ref):
    n = pl.program_id(1)
    lo = _lane_lo((1, LANES))
    kstart = pl.multiple_of(jnp.clip((n - 1) * QB, 0, SAMPLE_SEQ - WIN_KEYS), QB)
    k_a = kva_ref[pl.ds(kstart, WIN_KEYS), 0:LANES]
    v_a = kva_ref[pl.ds(kstart, WIN_KEYS), LANES:2 * LANES]
    c_k = cak_ref[...].astype(bf16)
    keys_a = [c_k, k_a]
    vx_a = [_with_ones(cav_ref[...].astype(bf16)), _with_ones(v_a)]
    qpos = n * QB + lax.broadcasted_iota(i32, (QB, WIN_KEYS), 0)
    kpos = kstart + lax.broadcasted_iota(i32, (QB, WIN_KEYS), 1)
    in_window = jnp.abs(qpos - kpos) <= A_WINDOW
    mask_window = lambda h, i, s: jnp.where(in_window, s, NEG) if i == 1 else s
    problems = []
    for g in range(2):
        q_rows = _gqa_rows([q_ref[:, L0_QA + j * LANES:L0_QA + (j + 1) * LANES] for j in (2 * g, 2 * g + 1)], g, lo)
        problems.append((q_rows, 4, keys_a, vx_a, mask_window, [sink_ref[4 * g + idx] for idx in range(4)]))
    krow = jnp.clip(2 * n - NA_ROWS // 2, 0, SAMPLE_SEQ // GRID_W - NA_KEY_ROWS)
    ktok = pl.multiple_of(krow * GRID_W, QB)
    pattern = _na_pattern(n)

    def na_bias(head):
        rows = []
        for rq in range(QB // GRID_W):
            blocks = []
            for kb in range(NA_KEY_ROWS // 2):
                d0, d1 = (roff_ref[(pattern * 2 + rq) * NA_KEY_ROWS + 2 * kb + t] for t in range(2))
                blocks.append(jnp.where(lo, tiles_ref[head, d0], tiles_ref[head, d1]))
            rows.append(jnp.concatenate(blocks, axis=1))
        return jnp.concatenate(rows, axis=0)

    for j in range(4):
        q_b = q_ref[:, L0_QB + j * LANES:L0_QB + (j + 1) * LANES]
        k_b = kvb_ref[pl.ds(ktok, NA_KEYS), j * LANES:(j + 1) * LANES]
        v_b = kvb_ref[pl.ds(ktok, NA_KEYS), 512 + j * LANES:512 + (j + 1) * LANES]
        cb_k = cbk_ref[:, j * LANES:(j + 1) * LANES].astype(bf16)
        cb_v = cbv_ref[:, j * LANES:(j + 1) * LANES].astype(bf16)
        add_bias = lambda h, i, s, j=j: s + na_bias(2 * j + h) if i == 1 else s
        problems.append((_stack_halves(q_b, lo), 2, [cb_k, k_b], [_with_ones(cb_v), _with_ones(v_b)], add_bias, None))

    results = _attend_many(problems)
    for g in range(2):
        for p, o in enumerate(_gqa_merge(results[g], g, lo)):
            j = 2 * g + p
            o_ref[:, j * LANES:(j + 1) * LANES] = o.astype(bf16)
    for j in range(4):
        outs = results[2 + j]
        o_ref[:, 512 + j * LANES:512 + (j + 1) * LANES] = jnp.where(lo, outs[0], outs[1]).astype(bf16)


def _lat0(sink, qkv, cak, cav, cbk, cbv, bias_tiles):
    sb = N_PROMPT // SAMPLE_SEQ
    return pl.pallas_call(
        _lat0_kernel,
        grid=(N_SAMPLE_BATCH, N_QB),
        in_specs=[
            pl.BlockSpec(memory_space=pltpu.SMEM),
            pl.BlockSpec(memory_space=pltpu.SMEM),
            pl.BlockSpec((QB, 1024), lambda b, n: (_PROMPT_QBLOCKS + b * N_QB + n, 0)),
            pl.BlockSpec((SAMPLE_SEQ, 1024), lambda b, n: (sb + b, 1)),
            pl.BlockSpec((SAMPLE_SEQ, 256), lambda b, n: (sb + b, L0_KA // 256)),
            pl.BlockSpec((None, PAST_LEN, LANES), lambda b, n: (b, 0, 0)),
            pl.BlockSpec((None, PAST_LEN, LANES), lambda b, n: (b, 0, 0)),
            pl.BlockSpec((None, PAST_LEN, 512), lambda b, n: (b, 0, 0)),
            pl.BlockSpec((None, PAST_LEN, 512), lambda b, n: (b, 0, 0)),
            pl.BlockSpec((8, N_ROW_OFFSETS + 1, GRID_W, LANES), lambda b, n: (0, 0, 0, 0)),
        ],
        out_specs=pl.BlockSpec((QB, D_MODEL), lambda b, n: (b * N_QB + n, 0)),
        out_shape=jax.ShapeDtypeStruct((N_SAMPLE, D_MODEL), bf16),
        compiler_params=_cparams(("parallel", "arbitrary")),
        name="lat0",
    )(sink, jnp.asarray(_na_row_offsets()), qkv, qkv, qkv, cak, cav, cbk, cbv, bias_tiles)


N_ROW_OFFSETS = 2 * NA_ROWS - 1


def _na_row_offsets():
    rows = SAMPLE_SEQ // GRID_W
    idx = np.full((N_NA_PATTERNS, 2, NA_KEY_ROWS), N_ROW_OFFSETS, np.int32)
    for p, n in enumerate((0, 1, 2, N_QB - 2, N_QB - 1)):
        k0 = int(np.clip(2 * n - NA_ROWS // 2, 0, rows - NA_KEY_ROWS))
        for rq in range(2):
            r = 2 * n + rq
            rs = int(np.clip(r - NA_ROWS // 2, 0, rows - NA_ROWS))
            for kl in range(NA_KEY_ROWS):
                if rs <= k0 + kl < rs + NA_ROWS:
                    idx[p, rq, kl] = k0 + kl - r + NA_ROWS - 1
    return idx.reshape(-1)


def _na_bias_tiles(rel_bias):
    n_dc = 2 * NA_COLS - 1
    c = np.arange(GRID_W)[:, None]
    kc = np.arange(GRID_W)[None, :]
    cs = np.clip(c - NA_COLS // 2, 0, GRID_W - NA_COLS)
    col_ok = (kc >= cs) & (kc < cs + NA_COLS)
    col_hot = ((kc - c + NA_COLS - 1)[None] == np.arange(n_dc)[:, None, None]) & col_ok[None]
    hp = lax.Precision.HIGHEST
    tiles = jnp.einsum("hdx,xck->hdck", rel_bias.astype(f32), col_hot.astype(np.float32), precision=hp)
    tiles = tiles + np.where(col_ok, 0.0, NEG).astype(np.float32)
    tiles = jnp.concatenate([tiles, jnp.full((tiles.shape[0], 1, GRID_W, GRID_W), NEG, f32)], axis=1)
    return jnp.concatenate([tiles, tiles], axis=-1)


def _diff_lambda(lam_ref, lam_init):
    lv = lam_ref[...]
    s1 = jnp.sum(lv[0:1, :] * lv[1:2, :], axis=-1, keepdims=True)
    s2 = jnp.sum(lv[2:3, :] * lv[3:4, :], axis=-1, keepdims=True)
    return jnp.exp(s1) - jnp.exp(s2) + lam_init


def _diff_heads(heads, o_ref, lam, subln, lo, lam_init):
    for p0 in range(0, len(heads), DIFF_HEADS_PER_PASS):
        group = heads[p0:p0 + DIFF_HEADS_PER_PASS]
        results = _attend_many([(_stack_halves(q, lo), 2, ks, [_with_ones(v) for v in vs], None, None)
                                for _, q, ks, vs in group])
        for (h, _, _, _), (o1, o2) in zip(group, results):
            o = o1 - lam * o2
            o = o * lax.rsqrt(jnp.mean(o * o, axis=-1, keepdims=True) + EPS)
            o_ref[:, h * LANES:(h + 1) * LANES] = ((o * subln) * (1.0 - lam_init)).astype(bf16)


def _ctx1_kernel(lam_ref, subln_ref, qkv_ref, o_ref, *, lam_init):
    lo = _lane_lo((1, LANES))
    lam = _diff_lambda(lam_ref, lam_init)
    blk = lambda base, h: qkv_ref[:, base + h * LANES:base + (h + 1) * LANES]
    heads = [(h, blk(0, h), [blk(D_MODEL, h)], [blk(2 * D_MODEL, h)]) for h in range(8)]
    _diff_heads(heads, o_ref, lam, subln_ref[...], lo, lam_init)


def _ctx1(lamv, subln, qkv, lam_init):
    return pl.pallas_call(
        functools.partial(_ctx1_kernel, lam_init=lam_init),
        grid=(N_PROMPT_BATCH,),
        in_specs=[
            pl.BlockSpec((8, HEAD_DIM), lambda b: (0, 0)),
            pl.BlockSpec((1, LANES), lambda b: (0, 0)),
            pl.BlockSpec((PROMPT_SEQ, 3 * D_MODEL), lambda b: (b, 0)),
        ],
        out_specs=pl.BlockSpec((PROMPT_SEQ, D_MODEL), lambda b: (b, 0)),
        out_shape=jax.ShapeDtypeStruct((N_PROMPT, D_MODEL), bf16),
        compiler_params=_cparams(("parallel",)),
        name="ctx1",
    )(lamv, subln, qkv)


def _lat1_kernel(lam_ref, subln_ref, q_ref, k_ref, v_ref, ck_ref, cv_ref, o_ref, *, lam_init):
    lo = _lane_lo((1, LANES))
    lam = _diff_lambda(lam_ref, lam_init)
    heads = []
    for h in range(8):
        sl = slice(h * LANES, (h + 1) * LANES)
        heads.append((h, q_ref[:, sl], [ck_ref[:, sl].astype(bf16), k_ref[:, sl]],
                      [cv_ref[:, sl].astype(bf16), v_ref[:, sl]]))
    _diff_heads(heads, o_ref, lam, subln_ref[...], lo, lam_init)


def _lat1(lamv, subln, qkv, ck, cv, lam_init):
    sb = N_PROMPT // SAMPLE_SEQ
    nq = SAMPLE_SEQ // TM
    return pl.pallas_call(
        functools.partial(_lat1_kernel, lam_init=lam_init),
        grid=(N_SAMPLE_BATCH, nq),
        in_specs=[
            pl.BlockSpec((8, HEAD_DIM), lambda b, n: (0, 0)),
            pl.BlockSpec((1, LANES), lambda b, n: (0, 0)),
            pl.BlockSpec((TM, D_MODEL), lambda b, n: (N_PROMPT_TILES + b * nq + n, 0)),
            pl.BlockSpec((SAMPLE_SEQ, D_MODEL), lambda b, n: (sb + b, 1)),
            pl.BlockSpec((SAMPLE_SEQ, D_MODEL), lambda b, n: (sb + b, 2)),
            pl.BlockSpec((None, PAST_LEN, D_MODEL), lambda b, n: (b, 0, 0)),
            pl.BlockSpec((None, PAST_LEN, D_MODEL), lambda b, n: (b, 0, 0)),
        ],
        out_specs=pl.BlockSpec((TM, D_MODEL), lambda b, n: (b * nq + n, 0)),
        out_shape=jax.ShapeDtypeStruct((N_SAMPLE, D_MODEL), bf16),
        compiler_params=_cparams(("parallel", "arbitrary")),
        name="lat1",
    )(lamv, subln, qkv, qkv, qkv, ck, cv)


def _split_bf16(a):
    hi = a.astype(bf16)
    return hi, (a - hi.astype(f32)).astype(bf16)


def _route_kernel(*refs, n_x):
    x_refs = refs[:n_x]
    (op_ref, os_ref, mod_ref, g_ref, wo_ref, rwt_ref, rb_ref,
     xnew_ref, xloc_ref, slots_ref, gate_ref, len_ref) = refs[n_x:]
    logits = [_route_logits(t, x_refs, op_ref, os_ref, mod_ref, g_ref, wo_ref, rwt_ref, xnew_ref)
              for t in range(ROUTE_TILES)]
    tiles = _route_tiles(jnp.concatenate([lg for _, lg in logits], axis=1), rb_ref, slots_ref, gate_ref, len_ref)
    for t, (slots, run_len) in enumerate(tiles):
        _route_dispatch(t, slots, logits[t][0], run_len, xloc_ref)


def _route_logits(t, x_refs, op_ref, os_ref, mod_ref, g_ref, wo_ref, rwt_ref, xnew_ref):
    is_prompt = pl.program_id(0) < N_PROMPT_TILES // ROUTE_TILES
    rows = slice(t * TM, (t + 1) * TM)
    attn = jnp.where(is_prompt, op_ref[rows, :], os_ref[rows, :])
    x_in = x_refs[0][rows, :] if len(x_refs) == 1 else jnp.where(is_prompt, x_refs[0][rows, :], x_refs[1][rows, :])
    x = x_in + mod_ref[2:3, :] * _dot(attn, wo_ref[...])
    xnew_ref[rows, :] = x
    h = _norm_mod(x, g_ref[...], mod_ref[4:5, :], mod_ref[3:4, :])
    h_hi, h_lo = _split_bf16(h)
    w_hi, w_lo = _split_bf16(rwt_ref[...])
    return h_hi, _dot_nt(w_hi, h_hi) + (_dot_nt(w_hi, h_lo) + _dot_nt(w_lo, h_hi))


def _route_tiles(logits, rb_ref, slots_ref, gate_ref, len_ref):
    ng, ge = N_GROUPS, N_EXPERTS // N_GROUPS
    n = ROUTE_TILES * TM
    tile = lambda a, t: a[..., t * TM:(t + 1) * TM]
    scores = jax.nn.sigmoid(logits)
    biased = scores + rb_ref[...]
    s3 = scores.reshape(ng, ge, n)
    b3 = biased.reshape(ng, ge, n)
    in_group = lax.broadcasted_iota(i32, (ng, ge, n), 1).astype(f32)
    group_id = lax.broadcasted_iota(i32, (ng, 1, n), 0).astype(f32)
    expert_id = lax.broadcasted_iota(i32, (ng, ge, n), 0).astype(f32) * ge + in_group

    def max01(a):
        return jnp.max(jnp.max(a, axis=0, keepdims=True), axis=1, keepdims=True)

    def min01(a):
        return jnp.min(jnp.min(a, axis=0, keepdims=True), axis=1, keepdims=True)

    def sum01(a):
        return jnp.sum(jnp.sum(a, axis=0, keepdims=True), axis=1, keepdims=True)

    m1 = jnp.max(b3, axis=1, keepdims=True)
    first = jnp.min(jnp.where(b3 == m1, in_group, ge), axis=1, keepdims=True)
    m2 = jnp.max(jnp.where(in_group == first, -jnp.inf, b3), axis=1, keepdims=True)
    gscore = m1 + m2
    gsel = jnp.zeros((ng, 1, n), f32)
    for _ in range(TOPK_GROUPS):
        gm = jnp.max(gscore, axis=0, keepdims=True)
        gi = jnp.min(jnp.where(gscore == gm, group_id, ng), axis=0, keepdims=True)
        hit = group_id == gi
        gsel = jnp.where(hit, 1.0, gsel)
        gscore = jnp.where(hit, -jnp.inf, gscore)
    cand = jnp.where(jnp.broadcast_to(gsel, (ng, ge, n)) > 0.0, b3, -jnp.inf)
    top_e, top_w = [], []
    for _ in range(TOP_K):
        em = max01(cand)
        ei = min01(jnp.where(cand == em, expert_id, N_EXPERTS))
        hit = expert_id == ei
        top_e.append(ei)
        top_w.append(sum01(jnp.where(hit, s3, 0.0)))
        cand = jnp.where(hit, -jnp.inf, cand)
    wsum = functools.reduce(lambda a, b: a + b, top_w)
    sel3 = jnp.zeros((ng, ge, n), f32)
    for k, (ei, w) in enumerate(zip(top_e, top_w)):
        gate = (w / wsum * ROUTED_SCALE).reshape(1, n)
        for t in range(ROUTE_TILES):
            gate_ref[t, k:k + 1, :] = tile(gate, t)
        sel3 = jnp.where(expert_id == ei, 1.0, sel3)
    sel = sel3.reshape(N_EXPERTS, n)

    r_i = lax.broadcasted_iota(i32, (N_EXPERTS, N_EXPERTS), 0)
    c_i = lax.broadcasted_iota(i32, (N_EXPERTS, N_EXPERTS), 1)
    lower = jnp.where(c_i < r_i, 1.0, 0.0).astype(bf16)
    run_lens, run_offs = [], []
    for t in range(ROUTE_TILES):
        cnt = jnp.sum(tile(sel, t), axis=1, keepdims=True)
        run_len = jnp.ceil(cnt * (1.0 / CHUNK)) * CHUNK
        run_off = _dot(lower, jnp.broadcast_to(run_len, (N_EXPERTS, LANES)).astype(bf16))[:, 0:1]
        run_lens.append(run_len)
        run_offs.append(jnp.broadcast_to(run_off, (N_EXPERTS, TM)))
    t_r = lax.broadcasted_iota(i32, (n, n), 0)
    t_c = lax.broadcasted_iota(i32, (n, n), 1)
    before = jnp.where(jnp.logical_and(t_r < t_c, t_r // TM == t_c // TM), 1.0, 0.0).astype(bf16)
    rank = _dot(sel.astype(bf16), before)
    slot3 = (jnp.concatenate(run_offs, axis=1) + rank).reshape(ng, ge, n)
    slots = [sum01(jnp.where(expert_id == ei, slot3, 0.0)).reshape(1, n).astype(i32) for ei in top_e]
    out = []
    for t in range(ROUTE_TILES):
        for k in range(TOP_K):
            slots_ref[t, k:k + 1, :] = tile(slots[k], t)
        slots_ref[t, TOP_K:8, :] = jnp.full((8 - TOP_K, TM), -1, i32)
        gate_ref[t, TOP_K:8, :] = jnp.zeros((8 - TOP_K, TM), f32)
        cnt_row = _dot_nt(jnp.ones((8, TM), bf16), tile(sel, t).astype(bf16))
        len_ref[t] = (jnp.ceil(cnt_row * (1.0 / CHUNK)) * CHUNK).astype(i32)
        out.append(([tile(sl, t) for sl in slots], run_lens[t]))
    return out


def _route_dispatch(t, slots, h_hi, run_len, xloc_ref):
    rows = ROUTE_ROWS

    def body(c, carry):
        base = pl.multiple_of(c * rows, rows)
        row_id = base.astype(jnp.int16) + lax.broadcasted_iota(jnp.int16, (rows, TM), 0)
        p = jnp.zeros((rows, TM), bf16)
        for k in range(TOP_K):
            p = jnp.where(row_id == slots[k].astype(jnp.int16), jnp.ones((), bf16), p)
        xloc_ref[pl.ds(t * SLOTS + base, rows), :] = _dot(p, h_hi).astype(bf16)
        return carry

    def zero_body(c, carry):
        base = pl.multiple_of(c * rows, rows)
        xloc_ref[pl.ds(t * SLOTS + base, rows), :] = jnp.zeros((rows, D_MODEL), bf16)
        return carry

    n_used = (jnp.sum(run_len).astype(i32) + (rows - 1)) // rows
    lax.fori_loop(0, n_used, body, 0)
    lax.fori_loop(n_used, SLOTS // rows, zero_body, 0)


def _route(x_parts, o_prompt, o_sample, mod_l, g, w_out, rwt, rb):
    per_tile = lambda i: (i, 0, 0)
    rt = ROUTE_TILES
    tm = rt * TM
    return pl.pallas_call(
        functools.partial(_route_kernel, n_x=len(x_parts)),
        grid=(N_TILES // rt,),
        in_specs=_x_specs(x_parts, tm) + [
            pl.BlockSpec((tm, D_MODEL), lambda i: _prompt_block(i, tm)),
            pl.BlockSpec((tm, D_MODEL), lambda i: _sample_block(i, tm)),
            pl.BlockSpec((None, 6, D_MODEL), lambda i: (_mod_row(i, tm), 0, 0)),
            pl.BlockSpec((1, D_MODEL), lambda i: (0, 0)),
            pl.BlockSpec((D_MODEL, D_MODEL), lambda i: (0, 0)),
            pl.BlockSpec((N_EXPERTS, D_MODEL), lambda i: (0, 0)),
            pl.BlockSpec((N_EXPERTS, 1), lambda i: (0, 0)),
        ],
        out_specs=[
            pl.BlockSpec((tm, D_MODEL), lambda i: (i, 0)),
            pl.BlockSpec((rt * SLOTS, D_MODEL), lambda i: (i, 0)),
            pl.BlockSpec((rt, 8, TM), per_tile),
            pl.BlockSpec((rt, 8, TM), per_tile),
            pl.BlockSpec((rt, 8, N_EXPERTS), per_tile),
        ],
        out_shape=[
            jax.ShapeDtypeStruct((N_TOK, D_MODEL), f32),
            jax.ShapeDtypeStruct((N_TILES * SLOTS, D_MODEL), bf16),
            jax.ShapeDtypeStruct((N_TILES, 8, TM), i32),
            jax.ShapeDtypeStruct((N_TILES, 8, TM), f32),
            jax.ShapeDtypeStruct((N_TILES, 8, N_EXPERTS), i32),
        ],
        compiler_params=_cparams(("parallel",)),
        name="route",
    )(*x_parts, o_prompt, o_sample, mod_l, g, w_out, rwt, rb)


def _moe_plan(run_len):
    nt, ne = run_len.shape

    def excl_cumsum(a):
        n = a.shape[-1]
        earlier = np.arange(n)[None, :] < np.arange(n)[:, None]
        return jnp.sum(jnp.where(earlier, a[..., None, :], 0), axis=-1)

    def first_diff(a):
        return a - jnp.concatenate([jnp.zeros_like(a[..., :1]), a[..., :-1]], axis=-1)

    off_loc = excl_cumsum(run_len)
    before = excl_cumsum(run_len.T).T
    n_e = jnp.sum(run_len, axis=0)
    n_pad = -(-n_e // GM) * GM
    g_start = excl_cumsum(n_pad)
    total = jnp.sum(n_pad)
    run_dst = g_start[None, :] + before
    run_src = jnp.arange(nt, dtype=i32)[:, None] * SLOTS + off_loc
    dst_f = run_dst.T.reshape(-1)
    shift_f = first_diff((run_src - run_dst).T.reshape(-1))
    rows = jnp.arange((G_TILES + GMM_ITEM_TILES - 1) * G_CHUNKS, dtype=i32) * CHUNK
    shift = jnp.sum(jnp.where(dst_f[None, :] <= rows[:, None], shift_f[None, :], 0), axis=1)
    in_run = jnp.any((g_start[None, :] <= rows[:, None]) & (rows[:, None] < (g_start + n_e)[None, :]), axis=1)
    chunk_src = (jnp.where(in_run, rows + shift, 0) // CHUNK).astype(i32)
    loc_rows = jnp.arange(SLOT_CHUNKS, dtype=i32) * CHUNK
    shift_l = first_diff(run_dst - off_loc)
    shift = jnp.sum(jnp.where(off_loc[:, None, :] <= loc_rows[None, :, None], shift_l[:, None, :], 0), axis=2)
    used = jnp.sum(run_len, axis=1)
    chunk_map = jnp.where(loc_rows[None, :] < used[:, None], (loc_rows[None, :] + shift) // CHUNK, 0).astype(i32)
    tile_start, n_tiles = g_start // GM, n_pad // GM
    n_items = -(-n_tiles // GMM_ITEM_TILES)
    item_start = excl_cumsum(n_items)
    items = jnp.arange(GMM_MAX_ITEMS, dtype=i32)
    owner = items[:, None] >= item_start[None, :]
    e_first = jnp.sum(jnp.where(owner, first_diff(tile_start - GMM_ITEM_TILES * item_start)[None, :], 0), axis=1)
    item_tile = e_first + GMM_ITEM_TILES * items
    e_end = jnp.sum(jnp.where(owner, first_diff(tile_start + n_tiles)[None, :], 0), axis=1)
    item_cnt = jnp.clip(e_end - item_tile, 0, GMM_ITEM_TILES)
    gmm_plan = tuple(a.astype(i32) for a in (item_start, n_items, item_tile, item_cnt, chunk_src))
    long_tiles = (used > COMBINE_SURE_CHUNKS * CHUNK).astype(i32)
    return gmm_plan, chunk_map.reshape(-1), long_tiles


def _gmm_in_copy(xloc_hbm, xbuf, sem, src_chunk, slot, c):
    return pltpu.make_async_copy(xloc_hbm.at[src_chunk], xbuf.at[slot, c], sem.at[slot])


def _gmm_out_copy(ybuf, y_hbm, sem, tile, slot, n_tiles):
    chunks = n_tiles * G_CHUNKS
    return pltpu.make_async_copy(ybuf.at[slot, pl.ds(0, chunks)],
                                 y_hbm.at[pl.ds(tile * G_CHUNKS, chunks)], sem.at[slot])


def _gmm_kernel(i0_ref, ni_ref, it_ref, ic_ref, cs_ref, xloc_hbm, wg_ref, wu_ref, wd_ref, y_hbm,
                xbuf, ybuf, zbuf, wg_b, wu_b, wd_b, in_sem, out_sem, zsem):
    e = pl.program_id(0)
    last = pl.num_programs(0) - 1
    n_items = ni_ref[e]
    first_item = i0_ref[e]
    total_items = i0_ref[last] + ni_ref[last]
    last_item = total_items - 1
    total_tiles = it_ref[last_item] + ic_ref[last_item]

    def start_in(item):
        first = it_ref[item] * G_CHUNKS
        for c in range(GMM_ITEM_TILES * G_CHUNKS):
            _gmm_in_copy(xloc_hbm, xbuf, in_sem, cs_ref[first + c], item % GMM_SLOTS, c).start()

    def wait_in(item):
        for c in range(GMM_ITEM_TILES * G_CHUNKS):
            _gmm_in_copy(xloc_hbm, xbuf, in_sem, 0, item % GMM_SLOTS, c).wait()

    def out_copy(item, fn):
        for cnt in range(1, GMM_ITEM_TILES + 1):
            @pl.when(ic_ref[item] == cnt)
            def _():
                fn(_gmm_out_copy(ybuf, y_hbm, out_sem, it_ref[item], item % GMM_SLOTS, cnt))

    @pl.when(e == 0)
    def _():
        for item in range(GMM_SLOTS - 1):
            start_in(item)
        zbuf[...] = jnp.zeros(zbuf.shape, zbuf.dtype)

    def tail_copies(fn):
        for j in range(_GMM_TAIL_PER_STEP):
            tile = total_tiles + e + j * N_EXPERTS

            @pl.when(tile < G_TILES)
            def _():
                fn(pltpu.make_async_copy(zbuf, y_hbm.at[pl.ds(tile * G_CHUNKS, G_CHUNKS)], zsem.at[0]))

    tail_copies(lambda cp: cp.start())

    @pl.when(n_items > 0)
    def _():
        wg_b[...] = wg_ref[...].astype(bf16)
        wu_b[...] = wu_ref[...].astype(bf16)
        wd_b[...] = wd_ref[...].astype(bf16)

    def body(j, carry):
        item = first_item + j
        slot = item % GMM_SLOTS

        @pl.when(item + (GMM_SLOTS - 1) < total_items)
        def _():
            start_in(item + (GMM_SLOTS - 1))

        wait_in(item)

        @pl.when(item >= GMM_SLOTS)
        def _():
            out_copy(item - GMM_SLOTS, lambda cp: cp.wait())

        for cnt in range(1, GMM_ITEM_TILES + 1):
            @pl.when(ic_ref[item] == cnt)
            def _():
                rows = cnt * GM
                chunks = cnt * G_CHUNKS
                x = xbuf[slot, 0:chunks].reshape(rows, D_MODEL)
                act = _silu(_dot(x, wg_b[...])) * _dot(x, wu_b[...])
                y = _dot(act.astype(bf16), wd_b[...]).astype(bf16)
                ybuf[slot, 0:chunks] = y.reshape(chunks, CHUNK, D_MODEL)

        out_copy(item, lambda cp: cp.start())
        return carry

    lax.fori_loop(0, n_items, body, 0)
    tail_copies(lambda cp: cp.wait())

    @pl.when(e == last)
    def _():
        for back in range(1, GMM_SLOTS + 1):
            out_copy(total_items - back, lambda cp: cp.wait())


def _gmm(plan, xloc, wg, wu, wd, layer):
    rows = GMM_ITEM_TILES * GM
    w_idx = lambda e, *_: (layer, e, 0, 0)
    grid_spec = pltpu.PrefetchScalarGridSpec(
        num_scalar_prefetch=5,
        grid=(N_EXPERTS,),
        in_specs=[
            pl.BlockSpec(memory_space=pl.ANY),
            pl.BlockSpec((None, None, D_MODEL, EXPERT_DIM), w_idx),
            pl.BlockSpec((None, None, D_MODEL, EXPERT_DIM), w_idx),
            pl.BlockSpec((None, None, EXPERT_DIM, D_MODEL), w_idx),
        ],
        out_specs=pl.BlockSpec(memory_space=pl.ANY),
        scratch_shapes=[pltpu.VMEM((GMM_SLOTS, rows // CHUNK, CHUNK, D_MODEL), bf16),
                        pltpu.VMEM((GMM_SLOTS, rows // CHUNK, CHUNK, D_MODEL), bf16),
                        pltpu.VMEM((G_CHUNKS, CHUNK, D_MODEL), bf16),
                        pltpu.VMEM((D_MODEL, EXPERT_DIM), bf16), pltpu.VMEM((D_MODEL, EXPERT_DIM), bf16),
                        pltpu.VMEM((EXPERT_DIM, D_MODEL), bf16),
                        pltpu.SemaphoreType.DMA((GMM_SLOTS,)), pltpu.SemaphoreType.DMA((GMM_SLOTS,)),
                        pltpu.SemaphoreType.DMA((1,))],
    )
    return pl.pallas_call(
        _gmm_kernel,
        grid_spec=grid_spec,
        out_shape=jax.ShapeDtypeStruct((G_TILES * G_CHUNKS, CHUNK, D_MODEL), bf16),
        compiler_params=_cparams(("arbitrary",)),
        name="gmm",
    )(*plan, xloc, wg, wu, wd)


def _combine_copy(y_hbm, ybuf, sem, sorted_chunk, slot, c):
    return pltpu.make_async_copy(y_hbm.at[sorted_chunk], ybuf.at[slot, c], sem.at[slot])


def _combine_kernel(cm_ref, long_ref, y_hbm, slots_ref, gate_ref, x_ref, mod_ref, g_ref, sg_ref, su_ref, sd_ref,
                    *rest, final):
    if final:
        gf_ref, yp_ref, ys_ref, ybuf, sem = rest
    else:
        o_ref, ybuf, sem = rest
    i = pl.program_id(0)
    n = pl.num_programs(0)
    ahead = COMBINE_SLOTS - 1
    slot = i % COMBINE_SLOTS

    def for_chunks(tile, fn):
        for c in range(COMBINE_SURE_CHUNKS):
            fn(c)

        @pl.when(long_ref[tile] == 1)
        def _():
            for c in range(COMBINE_SURE_CHUNKS, SLOT_CHUNKS):
                fn(c)

    def start(tile, s):
        for_chunks(tile, lambda c: _combine_copy(y_hbm, ybuf, sem, cm_ref[tile * SLOT_CHUNKS + c], s, c).start())

    def wait(tile, s):
        for_chunks(tile, lambda c: _combine_copy(y_hbm, ybuf, sem, 0, s, c).wait())

    @pl.when(i == 0)
    def _():
        ybuf[:, COMBINE_SURE_CHUNKS:] = jnp.zeros((COMBINE_SLOTS, SLOT_CHUNKS - COMBINE_SURE_CHUNKS, CHUNK, D_MODEL), bf16)
        for tile in range(ahead):
            start(tile, tile)

    wait(i, slot)
    start((i + ahead) % n, (i + ahead) % COMBINE_SLOTS)

    x = x_ref[...]
    hb = _norm_mod(x, g_ref[...], mod_ref[4:5, :], mod_ref[3:4, :]).astype(bf16)
    shared = _dot((_silu(_dot(hb, sg_ref[...])) * _dot(hb, su_ref[...])).astype(bf16), sd_ref[...])
    row_id = lax.broadcasted_iota(jnp.int16, (SLOTS, TM), 0)
    p = jnp.zeros((SLOTS, TM), bf16)
    for k in range(TOP_K):
        p = jnp.where(row_id == slots_ref[k:k + 1, :].astype(jnp.int16), gate_ref[k:k + 1, :].astype(bf16), p)
    routed = lax.dot_general(p, ybuf[slot].reshape(SLOTS, D_MODEL), (((0,), (0,)), ((), ())),
                             preferred_element_type=f32)
    out = x + mod_ref[5:6, :] * (routed + shared)
    if final:
        y = (out * lax.rsqrt(jnp.mean(out * out, axis=-1, keepdims=True) + EPS)) * gf_ref[...]

        @pl.when(i < N_PROMPT_TILES)
        def _():
            yp_ref[...] = y

        @pl.when(i >= N_PROMPT_TILES)
        def _():
            ys_ref[...] = y
    else:
        o_ref[...] = out

    @pl.when(i == n - 1)
    def _():
        for k in range(1, ahead + 1):
            wait((i + k) % n, (i + k) % COMBINE_SLOTS)


def _combine(chunk_map, long_tiles, y, slots, gates, x, mod_l, g, sg, su, sd, final_g=None):
    shd = sg.shape[1]
    final = final_g is not None
    row_spec = pl.BlockSpec((TM, D_MODEL), lambda i, *_: (i, 0))
    vec_spec = pl.BlockSpec((1, D_MODEL), lambda i, *_: (0, 0))
    if final:
        out_specs = [pl.BlockSpec((TM, D_MODEL), lambda i, *_: _prompt_block(i)),
                     pl.BlockSpec((TM, D_MODEL), lambda i, *_: _sample_block(i))]
        out_shape = [jax.ShapeDtypeStruct((N_PROMPT, D_MODEL), f32), jax.ShapeDtypeStruct((N_SAMPLE, D_MODEL), f32)]
    else:
        out_specs, out_shape = row_spec, jax.ShapeDtypeStruct((N_TOK, D_MODEL), f32)
    grid_spec = pltpu.PrefetchScalarGridSpec(
        num_scalar_prefetch=2,
        grid=(N_TILES,),
        in_specs=[
            pl.BlockSpec(memory_space=pl.ANY),
            pl.BlockSpec((None, 8, TM), lambda i, *_: (i, 0, 0)),
            pl.BlockSpec((None, 8, TM), lambda i, *_: (i, 0, 0)),
            row_spec,
            pl.BlockSpec((None, 6, D_MODEL), lambda i, *_: (_mod_row(i), 0, 0)),
            vec_spec,
            pl.BlockSpec((D_MODEL, shd), lambda i, *_: (0, 0)),
            pl.BlockSpec((D_MODEL, shd), lambda i, *_: (0, 0)),
            pl.BlockSpec((shd, D_MODEL), lambda i, *_: (0, 0)),
        ] + ([vec_spec] if final else []),
        out_specs=out_specs,
        scratch_shapes=[pltpu.VMEM((COMBINE_SLOTS, SLOT_CHUNKS, CHUNK, D_MODEL), bf16),
                        pltpu.SemaphoreType.DMA((COMBINE_SLOTS,))],
    )
    args = (chunk_map, long_tiles, y, slots, gates, x, mod_l, g, sg, su, sd) + ((final_g,) if final else ())
    return pl.pallas_call(
        functools.partial(_combine_kernel, final=final),
        grid_spec=grid_spec,
        out_shape=out_shape,
        compiler_params=_cparams(("arbitrary",)),
        name="combine",
    )(*args)


def _moe(x_parts, o_prompt, o_sample, w_out, mod_l, g, rwt, rb, wg, wu, wd, layer, sg, su, sd, final_g=None):
    x, xloc, slots, gates, run_len = _route(x_parts, o_prompt, o_sample, mod_l, g, w_out, rwt, rb)
    gmm_plan, chunk_map, long_tiles = _moe_plan(run_len[:, 0, :])
    y = _gmm(gmm_plan, xloc.reshape(N_TILES * SLOT_CHUNKS, CHUNK, D_MODEL), wg, wu, wd, layer)
    return _combine(chunk_map, long_tiles, y, slots, gates, x, mod_l, g, sg, su, sd, final_g)


_L0_CHUNKS = (
    (0, 512, 0, (0, 1, 2, 3), ()),
    (512, 1024, 768, (), ()),
    (1024, 1536, 1280, (), ((2, 0, 512, 0, False),)),
    (1536, 2048, 1792, (), ((3, 0, 512, 0, False),)),
    (2048, 2304, 512, (0,), ((0, 0, 128, 0, False), (1, 128, 256, 0, False))),
)
_L0_KV_OUTS = (("T", 128), ("T", 128), ("T", 512), ("T", 512))
_L1_CHUNKS = (
    (0, 512, 0, (0, 1, 2, 3), ()),
    (512, 1024, 512, (0, 1, 2, 3), ()),
    (1024, 1536, 1024, (0, 1, 2, 3), ((0, 0, 512, 0, False),)),
    (1536, 2048, 1536, (0, 1, 2, 3), ((0, 0, 512, 512, False),)),
    (2048, 2560, 2048, (), ((1, 0, 512, 0, True),)),
    (2560, 3072, 2560, (), ((1, 0, 512, 512, True),)),
)
_L1_KV_OUTS = (("T", 1024), ("H", 8))


def _from_feature_major(kt, *head_dims):
    nb, _, s = kt.shape
    nd = len(head_dims)
    return kt.reshape(nb, *head_dims, s).transpose(0, nd + 1, *range(1, nd + 1))[:, None]


def kernel(x_prompt, x_sample, cache_a_k, cache_a_v, cache_b_k, cache_b_v, cache_c_k, cache_c_v, c, c_ctx, w_mod, b_mod, norm_mix, norm_ffn, w_in_ab, w_out_ab, sink_a, rel_bias_b, w_in_c, w_out_c, lam_q1, lam_k1, lam_q2, lam_k2, subln_c, router_w, router_bias, exp_w_gate, exp_w_up, exp_w_down, sh_w_gate, sh_w_up, sh_w_down, final_norm):
    x = (x_prompt.reshape(N_PROMPT, D_MODEL), x_sample.reshape(N_SAMPLE, D_MODEL))
    cond8 = jnp.concatenate([c_ctx[None, :], c, jnp.zeros((8 - 1 - N_SAMPLE_BATCH, D_MODEL), f32)], axis=0)
    mod = _adaln(cond8, w_mod, b_mod).reshape(DEPTH, 8, 6, D_MODEL)
    rope_tabs = _rope_tables()
    new_kv = {}
    for layer in range(DEPTH):
        li = layer // 2
        mod_l = mod[layer]
        g_mix = norm_mix[layer][None, :]
        g_ffn = norm_ffn[layer][None, :]
        if layer % 2 == 0:
            w_in = w_in_ab[li].astype(bf16)
            qkv, ak, av, bk, bv = _inproj(x, mod_l, g_mix, w_in, rope_tabs, _L0_CHUNKS, _L0_KV_OUTS)
            new_kv["a_k"], new_kv["a_v"], new_kv["b_k"], new_kv["b_v"] = ak, av, bk, bv
            o_p = _ctx0(sink_a[li], qkv)
            o_s = _lat0(sink_a[li], qkv,
                        cache_a_k[:, li].reshape(N_SAMPLE_BATCH, PAST_LEN, LANES),
                        cache_a_v[:, li].reshape(N_SAMPLE_BATCH, PAST_LEN, LANES),
                        cache_b_k[:, li].reshape(N_SAMPLE_BATCH, PAST_LEN, 512),
                        cache_b_v[:, li].reshape(N_SAMPLE_BATCH, PAST_LEN, 512),
                        _na_bias_tiles(rel_bias_b[li]))
            w_out = w_out_ab[li].astype(bf16)
        else:
            lam_init = 0.8 - 0.6 * math.exp(-0.3 * layer)
            qkv, ck, cv = _inproj(x, mod_l, g_mix, w_in_c[li].astype(bf16), rope_tabs, _L1_CHUNKS, _L1_KV_OUTS)
            new_kv["c_k"], new_kv["c_v"] = ck, cv
            lamv = jnp.concatenate([lam_q1[li][None], lam_k1[li][None], lam_q2[li][None], lam_k2[li][None],
                                    jnp.zeros((4, HEAD_DIM), f32)], axis=0)
            subln = subln_c[li][None, :]
            o_p = _ctx1(lamv, subln, qkv, lam_init)
            o_s = _lat1(lamv, subln, qkv,
                        cache_c_k[:, li].reshape(N_SAMPLE_BATCH, PAST_LEN, D_MODEL),
                        cache_c_v[:, li].reshape(N_SAMPLE_BATCH, PAST_LEN, D_MODEL), lam_init)
            w_out = w_out_c[li].astype(bf16)
        last = layer == DEPTH - 1
        x = _moe(x, o_p, o_s, w_out, mod_l, g_ffn, router_w[layer].T, router_bias[layer][:, None],
                 exp_w_gate, exp_w_up, exp_w_down, layer,
                 sh_w_gate[layer].astype(bf16), sh_w_up[layer].astype(bf16), sh_w_down[layer].astype(bf16),
                 final_norm[None, :] if last else None)
        x = x if last else (x,)
    y_prompt, y_sample = x
    nb, s = N_PROMPT_BATCH, PROMPT_SEQ
    return (y_prompt.reshape(nb, s, D_MODEL), y_sample.reshape(N_SAMPLE_BATCH, SAMPLE_SEQ, D_MODEL),
            _from_feature_major(new_kv["a_k"], 2, HEAD_DIM), _from_feature_major(new_kv["a_v"], 2, HEAD_DIM),
            _from_feature_major(new_kv["b_k"], 8, HEAD_DIM), _from_feature_major(new_kv["b_v"], 8, HEAD_DIM),
            _from_feature_major(new_kv["c_k"], 8, 2, HEAD_DIM), new_kv["c_v"].reshape(nb, 1, s, 8, 2 * HEAD_DIM))
```

```python
import functools
import math

import jax
import jax.numpy as jnp
import numpy as np
from jax import lax
from jax.experimental import pallas as pl
from jax.experimental.pallas import tpu as pltpu

f32 = jnp.float32
bf16 = jnp.bfloat16
i32 = jnp.int32

D_MODEL = 1024
N_PROMPT_BATCH = 16
PROMPT_SEQ = 256
DEPTH = 2
N_SAMPLE_BATCH = 2
SAMPLE_SEQ = 2048
PAST_LEN = 512
GRID_W = 64
HEAD_DIM = 64
ROPE_THETA = 10000.0
EPS = 1e-6
A_WINDOW = 128
NA_ROWS = 8
NA_COLS = 16
N_EXPERTS = 64
TOP_K = 6
N_GROUPS = 8
TOPK_GROUPS = 4
EXPERT_DIM = 256
ROUTED_SCALE = 2.5
Q_SCALE = HEAD_DIM ** -0.5

N_PROMPT = N_PROMPT_BATCH * PROMPT_SEQ
N_SAMPLE = N_SAMPLE_BATCH * SAMPLE_SEQ
N_TOK = N_PROMPT + N_SAMPLE

LANES = 128
TM = 256
TD = 512
N_PROMPT_TILES = N_PROMPT // TM
N_TILES = N_TOK // TM
QB = 128
CHUNK = 16
SLOTS = -(-(TM * TOP_K + N_EXPERTS * (CHUNK - 1)) // 256) * 256
SLOT_CHUNKS = SLOTS // CHUNK
ROUTE_ROWS = 512
ROUTE_TILES = 2
COMBINE_SLOTS = 3
COMBINE_SURE_CHUNKS = 136
GM = 256
_MAX_SORTED = TM * TOP_K * N_TILES + N_TILES * N_EXPERTS * (CHUNK - 1) + N_EXPERTS * (GM - CHUNK)
G_TILES = -(-_MAX_SORTED // GM)
G_CHUNKS = GM // CHUNK
GMM_ITEM_TILES = 2
GMM_SLOTS = 4
GMM_MAX_ITEMS = (G_TILES + N_EXPERTS * (GMM_ITEM_TILES - 1)) // GMM_ITEM_TILES
_GMM_TAIL_PER_STEP = -(-(G_TILES - TM * TOP_K * N_TILES // GM) // N_EXPERTS)
ADA_COLS = 1536
DIFF_HEADS_PER_PASS = 4
VMEM_LIMIT = 56 * 1024 * 1024

NEG = -1e30


def _cparams(sem):
    return pltpu.CompilerParams(dimension_semantics=sem, vmem_limit_bytes=VMEM_LIMIT)


def _mod_row(i, tm=TM):
    return jnp.where(i < N_PROMPT // tm, 0, 1 + (i - N_PROMPT // tm) // (SAMPLE_SEQ // tm))


def _prompt_block(i, tm=TM):
    return (jnp.minimum(i, N_PROMPT // tm - 1), 0)


def _sample_block(i, tm=TM):
    return (jnp.maximum(i - N_PROMPT // tm, 0), 0)


def _x_specs(parts, tm=TM):
    if len(parts) == 1:
        return [pl.BlockSpec((tm, D_MODEL), lambda i, *_: (i, 0))]
    return [pl.BlockSpec((tm, D_MODEL), lambda i, *_: _prompt_block(i, tm)),
            pl.BlockSpec((tm, D_MODEL), lambda i, *_: _sample_block(i, tm))]


def _load_x(i, x_refs, tm=TM):
    if len(x_refs) == 1:
        return x_refs[0][...]
    return jnp.where(i < N_PROMPT // tm, x_refs[0][...], x_refs[1][...])


def _norm_mod(x, g, scale, shift):
    y = x * lax.rsqrt(jnp.mean(x * x, axis=-1, keepdims=True) + EPS)
    return (y * g) * (1.0 + scale) + shift


def _silu(x):
    return x * jax.nn.sigmoid(x)


def _dot(a, b):
    return jnp.dot(a, b, preferred_element_type=f32)


def _dot_nt(a, b):
    return lax.dot_general(a, b, (((1,), (1,)), ((), ())), preferred_element_type=f32)


def _adaln_kernel(cond_ref, w_ref, b_ref, o_ref):
    s = _silu(cond_ref[...]).astype(bf16)
    o_ref[...] = _dot(s, w_ref[...].astype(bf16)) + b_ref[...]


def _adaln(cond8, w_mod, b_mod):
    n6 = 6 * D_MODEL
    return pl.pallas_call(
        _adaln_kernel,
        grid=(DEPTH, n6 // ADA_COLS),
        in_specs=[
            pl.BlockSpec((8, D_MODEL), lambda l, j: (0, 0)),
            pl.BlockSpec((None, D_MODEL, ADA_COLS), lambda l, j: (l, 0, j)),
            pl.BlockSpec((None, 1, ADA_COLS), lambda l, j: (l, 0, j)),
        ],
        out_specs=pl.BlockSpec((None, 8, ADA_COLS), lambda l, j: (l, 0, j)),
        out_shape=jax.ShapeDtypeStruct((DEPTH, 8, n6), f32),
        compiler_params=_cparams(("parallel", "parallel")),
        name="adaln",
    )(cond8, w_mod, b_mod.reshape(DEPTH, 1, n6))


def _rope_block(blk, cos, sin_a, sin_b):
    return blk * cos + pltpu.roll(blk, LANES - 16, 1) * sin_a + pltpu.roll(blk, 16, 1) * sin_b


def _inproj_kernel(*refs, chunks, n_x):
    x_refs, kv_refs = refs[:n_x], refs[n_x + 7:]
    mod_ref, g_ref, w_ref, cos_ref, sa_ref, sb_ref, qkv_ref = refs[n_x:n_x + 7]
    i = pl.program_id(0)
    is_prompt = i < N_PROMPT // TD
    kb = 256

    def normed_input():
        x = _load_x(i, x_refs, TD)
        rinv = lax.rsqrt(jnp.mean(x * x, axis=-1, keepdims=True) + EPS)
        c0, c1, s0 = chunks[0][:3]
        blocks, acc = [], None
        for k in range(D_MODEL // kb):
            sl = slice(k * kb, (k + 1) * kb)
            hk = ((((x[:, sl] * rinv) * g_ref[:, sl]) * (1.0 + mod_ref[1:2, sl])) + mod_ref[0:1, sl]).astype(bf16)
            part = _dot(hk, w_ref[sl, s0:s0 + (c1 - c0)])
            acc = part if acc is None else acc + part
            blocks.append(hk)
        return jnp.concatenate(blocks, axis=1), acc

    @pl.when(is_prompt)
    def _():
        h, acc0 = normed_input()
        for n, (c0, c1, s0, _, kv_out) in enumerate(chunks):
            acc = acc0 if n == 0 else _dot(h, w_ref[:, s0:s0 + (c1 - c0)])
            qkv_ref[:, c0:c1] = acc.astype(bf16)
            for ridx, a0, a1, o0, per_head in kv_out:
                if per_head:
                    heads = kv_refs[ridx].shape[0] // TD
                    for j in range((a1 - a0) // LANES):
                        kv_refs[ridx][pl.ds(o0 // LANES + j, TD, stride=heads), :] = (
                            acc[:, a0 + j * LANES:a0 + (j + 1) * LANES])
                else:
                    t = acc[:, a0:a1].T
                    for b in range(TD // PROMPT_SEQ):
                        kv_refs[ridx][b, o0:o0 + (a1 - a0), :] = t[:, b * PROMPT_SEQ:(b + 1) * PROMPT_SEQ]

    @pl.when(jnp.logical_not(is_prompt))
    def _():
        cos, sa, sb = cos_ref[...], sa_ref[...], sb_ref[...]
        h, acc0 = normed_input()
        for n, (c0, c1, s0, rope_blocks, _) in enumerate(chunks):
            acc = acc0 if n == 0 else _dot(h, w_ref[:, s0:s0 + (c1 - c0)])
            for b in range((c1 - c0) // LANES):
                blk = acc[:, b * LANES:(b + 1) * LANES]
                if b in rope_blocks:
                    blk = _rope_block(blk, cos, sa, sb)
                qkv_ref[:, c0 + b * LANES:c0 + (b + 1) * LANES] = blk.astype(bf16)


def _inproj(x_parts, mod_l, g, w, rope_tabs, chunks, kv_outs):
    n = w.shape[1]
    cos, sa, sb = rope_tabs
    bpt = TD // PROMPT_SEQ

    def rope_idx(i):
        return (jnp.where(i < N_PROMPT // TD, 0, (i - N_PROMPT // TD) % (SAMPLE_SEQ // TD)), 0)

    kv_specs, kv_shapes = [], []
    for kind, size in kv_outs:
        if kind == "T":
            kv_specs.append(pl.BlockSpec((bpt, size, PROMPT_SEQ), lambda i: _prompt_block(i, TD) + (0,)))
            kv_shapes.append(jax.ShapeDtypeStruct((N_PROMPT_BATCH, size, PROMPT_SEQ), f32))
        else:
            kv_specs.append(pl.BlockSpec((TD * size, LANES), lambda i: _prompt_block(i, TD)))
            kv_shapes.append(jax.ShapeDtypeStruct((N_PROMPT * size, LANES), f32))

    return pl.pallas_call(
        functools.partial(_inproj_kernel, chunks=chunks, n_x=len(x_parts)),
        grid=(N_TOK // TD,),
        in_specs=_x_specs(x_parts, TD) + [
            pl.BlockSpec((None, 6, D_MODEL), lambda i: (_mod_row(i, TD), 0, 0)),
            pl.BlockSpec((1, D_MODEL), lambda i: (0, 0)),
            pl.BlockSpec((D_MODEL, n), lambda i: (0, 0)),
            pl.BlockSpec((TD, LANES), rope_idx),
            pl.BlockSpec((TD, LANES), rope_idx),
            pl.BlockSpec((TD, LANES), rope_idx),
        ],
        out_specs=[pl.BlockSpec((TD, n), lambda i: (i, 0))] + kv_specs,
        out_shape=[jax.ShapeDtypeStruct((N_TOK, n), bf16)] + kv_shapes,
        compiler_params=_cparams(("arbitrary",)),
        name="inproj",
    )(*x_parts, mod_l, g, w, cos, sa, sb)


def _rope_tables():
    nq = HEAD_DIM // 4
    t = np.arange(SAMPLE_SEQ)
    inv = np.power(np.float32(ROPE_THETA), -np.arange(nq, dtype=np.float32) / np.float32(nq))
    ang_r = (t // GRID_W).astype(np.float32)[:, None] * inv
    ang_c = (t % GRID_W).astype(np.float32)[:, None] * inv
    zero = np.zeros_like(ang_r)

    def head(fr, fc):
        return np.concatenate([fr[0], fr[1], fc[0], fc[1]], axis=-1)

    cos = head((np.cos(ang_r), np.cos(ang_r)), (np.cos(ang_c), np.cos(ang_c)))
    sin_a = head((-np.sin(ang_r), zero), (-np.sin(ang_c), zero))
    sin_b = head((zero, np.sin(ang_r)), (zero, np.sin(ang_c)))
    two = lambda a: jnp.asarray(np.concatenate([a, a], axis=-1).astype(np.float32))
    return two(cos), two(sin_a), two(sin_b)


def _lane_lo(shape):
    return lax.broadcasted_iota(i32, shape, len(shape) - 1) < HEAD_DIM


def _half(q, lo_mask, half):
    keep = lo_mask if half == 0 else jnp.logical_not(lo_mask)
    return jnp.where(keep, q, jnp.zeros_like(q)) * Q_SCALE


def _swap_halves(x):
    return pltpu.roll(x.astype(f32), HEAD_DIM, 1).astype(x.dtype)


def _stack_halves(q, lo_mask):
    return jnp.concatenate([_half(q, lo_mask, 0), _half(q, lo_mask, 1)], axis=0)


def _with_ones(v):
    return jnp.concatenate([v, jnp.ones_like(v)], axis=1)


def _attend_many(problems):
    scores = [[_dot_nt(q_rows, k) for k in key_blocks] for q_rows, _, key_blocks, _, _, _ in problems]
    exps, maxes = [], []
    for (q_rows, n_heads, key_blocks, _, fix_scores, sinks), sc in zip(problems, scores):
        r = q_rows.shape[0] // n_heads
        e_p, m_p = [[] for _ in key_blocks], []
        for h in range(n_heads):
            blocks = [s[h * r:(h + 1) * r] for s in sc]
            if fix_scores is not None:
                blocks = [fix_scores(h, i, s) for i, s in enumerate(blocks)]
            m = functools.reduce(jnp.maximum, [jnp.max(s, axis=-1, keepdims=True) for s in blocks])
            if sinks is not None:
                m = jnp.maximum(m, sinks[h])
            m_p.append(m)
            for i, s in enumerate(blocks):
                e_p[i].append(jnp.exp((s - m).astype(bf16)))
        exps.append(e_p)
        maxes.append(m_p)
    outs = [functools.reduce(lambda a, b: a + b,
                             [_dot(e[0] if n_heads == 1 else jnp.concatenate(e, axis=0), vx)
                              for e, vx in zip(e_p, vx_blocks)])
            for (_, n_heads, _, vx_blocks, _, _), e_p in zip(problems, exps)]
    results = []
    for (q_rows, n_heads, _, _, _, sinks), out, m_p in zip(problems, outs, maxes):
        r = q_rows.shape[0] // n_heads
        res = []
        for h in range(n_heads):
            den = out[h * r:(h + 1) * r, LANES:]
            if sinks is not None:
                den = den + jnp.exp(sinks[h] - m_p[h])
            res.append(out[h * r:(h + 1) * r, :LANES] * (1.0 / den))
        results.append(res)
    return results


def _gqa_rows(q_blocks, group, lo_mask):
    parts = []
    for q in q_blocks:
        for half in range(2):
            qh = _half(q, lo_mask, half)
            parts.append(qh if half == group else _swap_halves(qh))
    return jnp.concatenate(parts, axis=0)


def _gqa_merge(outs, group, lo_mask):
    fixed = [o if idx % 2 == group else pltpu.roll(o, HEAD_DIM, 1) for idx, o in enumerate(outs)]
    return [jnp.where(lo_mask, fixed[2 * p], fixed[2 * p + 1]) for p in range(len(outs) // 2)]


L0_QA, L0_QB, L0_KB, L0_VB, L0_KA, L0_VA, L0_N = 0, 512, 1024, 1536, 2048, 2176, 2304


def _ctx0_kernel(sink_ref, qkv_ref, o_ref):
    lo = _lane_lo((1, LANES))
    blk = lambda base, j: qkv_ref[:, base + j * LANES:base + (j + 1) * LANES]
    k_a = blk(L0_KA, 0)
    vx_a = _with_ones(blk(L0_VA, 0))
    problems = []
    for g in range(2):
        q_rows = _gqa_rows([blk(L0_QA, 2 * g), blk(L0_QA, 2 * g + 1)], g, lo)
        problems.append((q_rows, 4, [k_a], [vx_a], None, [sink_ref[4 * g + idx] for idx in range(4)]))
    for j in range(4):
        problems.append((_stack_halves(blk(L0_QB, j), lo), 2, [blk(L0_KB, j)], [_with_ones(blk(L0_VB, j))],
                         None, None))
    results = _attend_many(problems)
    for g in range(2):
        for p, o in enumerate(_gqa_merge(results[g], g, lo)):
            j = 2 * g + p
            o_ref[:, j * LANES:(j + 1) * LANES] = o.astype(bf16)
    for j in range(4):
        outs = results[2 + j]
        o_ref[:, 512 + j * LANES:512 + (j + 1) * LANES] = jnp.where(lo, outs[0], outs[1]).astype(bf16)


def _ctx0(sink, qkv):
    return pl.pallas_call(
        _ctx0_kernel,
        grid=(N_PROMPT_BATCH,),
        in_specs=[
            pl.BlockSpec(memory_space=pltpu.SMEM),
            pl.BlockSpec((PROMPT_SEQ, L0_N), lambda b: (b, 0)),
        ],
        out_specs=pl.BlockSpec((PROMPT_SEQ, D_MODEL), lambda b: (b, 0)),
        out_shape=jax.ShapeDtypeStruct((N_PROMPT, D_MODEL), bf16),
        compiler_params=_cparams(("parallel",)),
        name="ctx0",
    )(sink, qkv)


WIN_KEYS = 3 * QB
NA_KEY_ROWS = 10
NA_KEYS = NA_KEY_ROWS * GRID_W
N_QB = SAMPLE_SEQ // QB
N_NA_PATTERNS = 5
_PROMPT_QBLOCKS = N_PROMPT // QB


def _na_pattern(n):
    return jnp.where(n < 2, n, jnp.where(n > N_QB - 3, n - (N_QB - 5), 2))


def _lat0_kernel(sink_ref, roff_ref, q_ref, kvb_ref, kva_ref, cak_ref, cav_ref, cbk_ref, cbv_ref, tiles_ref, o_ref):
    n = pl.program_id(1)
    lo = _lane_lo((1, LANES))
    kstart = pl.multiple_of(jnp.clip((n - 1) * QB, 0, SAMPLE_SEQ - WIN_KEYS), QB)
    k_a = kva_ref[pl.ds(kstart, WIN_KEYS), 0:LANES]
    v_a = kva_ref[pl.ds(kstart, WIN_KEYS), LANES:2 * LANES]
    c_k = cak_ref[...].astype(bf16)
    keys_a = [c_k, k_a]
    vx_a = [_with_ones(cav_ref[...].astype(bf16)), _with_ones(v_a)]
    qpos = n * QB + lax.broadcasted_iota(i32, (QB, WIN_KEYS), 0)
    kpos = kstart + lax.broadcasted_iota(i32, (QB, WIN_KEYS), 1)
    in_window = jnp.abs(qpos - kpos) <= A_WINDOW
    mask_window = lambda h, i, s: jnp.where(in_window, s, NEG) if i == 1 else s
    problems = []
    for g in range(2):
        q_rows = _gqa_rows([q_ref[:, L0_QA + j * LANES:L0_QA + (j + 1) * LANES] for j in (2 * g, 2 * g + 1)], g, lo)
        problems.append((q_rows, 4, keys_a, vx_a, mask_window, [sink_ref[4 * g + idx] for idx in range(4)]))
    krow = jnp.clip(2 * n - NA_ROWS // 2, 0, SAMPLE_SEQ // GRID_W - NA_KEY_ROWS)
    ktok = pl.multiple_of(krow * GRID_W, QB)
    pattern = _na_pattern(n)

    def na_bias(head):
        rows = []
        for rq in range(QB // GRID_W):
            blocks = []
            for kb in range(NA_KEY_ROWS // 2):
                d0, d1 = (roff_ref[(pattern * 2 + rq) * NA_KEY_ROWS + 2 * kb + t] for t in range(2))
                blocks.append(jnp.where(lo, tiles_ref[head, d0], tiles_ref[head, d1]))
            rows.append(jnp.concatenate(blocks, axis=1))
        return jnp.concatenate(rows, axis=0)

    for j in range(4):
        q_b = q_ref[:, L0_QB + j * LANES:L0_QB + (j + 1) * LANES]
        k_b = kvb_ref[pl.ds(ktok, NA_KEYS), j * LANES:(j + 1) * LANES]
        v_b = kvb_ref[pl.ds(ktok, NA_KEYS), 512 + j * LANES:512 + (j + 1) * LANES]
        cb_k = cbk_ref[:, j * LANES:(j + 1) * LANES].astype(bf16)
        cb_v = cbv_ref[:, j * LANES:(j + 1) * LANES].astype(bf16)
        add_bias = lambda h, i, s, j=j: s + na_bias(2 * j + h) if i == 1 else s
        problems.append((_stack_halves(q_b, lo), 2, [cb_k, k_b], [_with_ones(cb_v), _with_ones(v_b)], add_bias, None))

    results = _attend_many(problems)
    for g in range(2):
        for p, o in enumerate(_gqa_merge(results[g], g, lo)):
            j = 2 * g + p
            o_ref[:, j * LANES:(j + 1) * LANES] = o.astype(bf16)
    for j in range(4):
        outs = results[2 + j]
        o_ref[:, 512 + j * LANES:512 + (j + 1) * LANES] = jnp.where(lo, outs[0], outs[1]).astype(bf16)


def _lat0(sink, qkv, cak, cav, cbk, cbv, bias_tiles):
    sb = N_PROMPT // SAMPLE_SEQ
    return pl.pallas_call(
        _lat0_kernel,
        grid=(N_SAMPLE_BATCH, N_QB),
        in_specs=[
            pl.BlockSpec(memory_space=pltpu.SMEM),
            pl.BlockSpec(memory_space=pltpu.SMEM),
            pl.BlockSpec((QB, 1024), lambda b, n: (_PROMPT_QBLOCKS + b * N_QB + n, 0)),
            pl.BlockSpec((SAMPLE_SEQ, 1024), lambda b, n: (sb + b, 1)),
            pl.BlockSpec((SAMPLE_SEQ, 256), lambda b, n: (sb + b, L0_KA // 256)),
            pl.BlockSpec((None, PAST_LEN, LANES), lambda b, n: (b, 0, 0)),
            pl.BlockSpec((None, PAST_LEN, LANES), lambda b, n: (b, 0, 0)),
            pl.BlockSpec((None, PAST_LEN, 512), lambda b, n: (b, 0, 0)),
            pl.BlockSpec((None, PAST_LEN, 512), lambda b, n: (b, 0, 0)),
            pl.BlockSpec((8, N_ROW_OFFSETS + 1, GRID_W, LANES), lambda b, n: (0, 0, 0, 0)),
        ],
        out_specs=pl.BlockSpec((QB, D_MODEL), lambda b, n: (b * N_QB + n, 0)),
        out_shape=jax.ShapeDtypeStruct((N_SAMPLE, D_MODEL), bf16),
        compiler_params=_cparams(("parallel", "arbitrary")),
        name="lat0",
    )(sink, jnp.asarray(_na_row_offsets()), qkv, qkv, qkv, cak, cav, cbk, cbv, bias_tiles)


N_ROW_OFFSETS = 2 * NA_ROWS - 1


def _na_row_offsets():
    rows = SAMPLE_SEQ // GRID_W
    idx = np.full((N_NA_PATTERNS, 2, NA_KEY_ROWS), N_ROW_OFFSETS, np.int32)
    for p, n in enumerate((0, 1, 2, N_QB - 2, N_QB - 1)):
        k0 = int(np.clip(2 * n - NA_ROWS // 2, 0, rows - NA_KEY_ROWS))
        for rq in range(2):
            r = 2 * n + rq
            rs = int(np.clip(r - NA_ROWS // 2, 0, rows - NA_ROWS))
            for kl in range(NA_KEY_ROWS):
                if rs <= k0 + kl < rs + NA_ROWS:
                    idx[p, rq, kl] = k0 + kl - r + NA_ROWS - 1
    return idx.reshape(-1)


def _na_bias_tiles(rel_bias):
    n_dc = 2 * NA_COLS - 1
    c = np.arange(GRID_W)[:, None]
    kc = np.arange(GRID_W)[None, :]
    cs = np.clip(c - NA_COLS // 2, 0, GRID_W - NA_COLS)
    col_ok = (kc >= cs) & (kc < cs + NA_COLS)
    col_hot = ((kc - c + NA_COLS - 1)[None] == np.arange(n_dc)[:, None, None]) & col_ok[None]
    hp = lax.Precision.HIGHEST
    tiles = jnp.einsum("hdx,xck->hdck", rel_bias.astype(f32), col_hot.astype(np.float32), precision=hp)
    tiles = tiles + np.where(col_ok, 0.0, NEG).astype(np.float32)
    tiles = jnp.concatenate([tiles, jnp.full((tiles.shape[0], 1, GRID_W, GRID_W), NEG, f32)], axis=1)
    return jnp.concatenate([tiles, tiles], axis=-1)


def _diff_lambda(lam_ref, lam_init):
    lv = lam_ref[...]
    s1 = jnp.sum(lv[0:1, :] * lv[1:2, :], axis=-1, keepdims=True)
    s2 = jnp.sum(lv[2:3, :] * lv[3:4, :], axis=-1, keepdims=True)
    return jnp.exp(s1) - jnp.exp(s2) + lam_init


def _diff_heads(heads, o_ref, lam, subln, lo, lam_init):
    for p0 in range(0, len(heads), DIFF_HEADS_PER_PASS):
        group = heads[p0:p0 + DIFF_HEADS_PER_PASS]
        results = _attend_many([(_stack_halves(q, lo), 2, ks, [_with_ones(v) for v in vs], None, None)
                                for _, q, ks, vs in group])
        for (h, _, _, _), (o1, o2) in zip(group, results):
            o = o1 - lam * o2
            o = o * lax.rsqrt(jnp.mean(o * o, axis=-1, keepdims=True) + EPS)
            o_ref[:, h * LANES:(h + 1) * LANES] = ((o * subln) * (1.0 - lam_init)).astype(bf16)


def _ctx1_kernel(lam_ref, subln_ref, qkv_ref, o_ref, *, lam_init):
    lo = _lane_lo((1, LANES))
    lam = _diff_lambda(lam_ref, lam_init)
    blk = lambda base, h: qkv_ref[:, base + h * LANES:base + (h + 1) * LANES]
    heads = [(h, blk(0, h), [blk(D_MODEL, h)], [blk(2 * D_MODEL, h)]) for h in range(8)]
    _diff_heads(heads, o_ref, lam, subln_ref[...], lo, lam_init)


def _ctx1(lamv, subln, qkv, lam_init):
    return pl.pallas_call(
        functools.partial(_ctx1_kernel, lam_init=lam_init),
        grid=(N_PROMPT_BATCH,),
        in_specs=[
            pl.BlockSpec((8, HEAD_DIM), lambda b: (0, 0)),
            pl.BlockSpec((1, LANES), lambda b: (0, 0)),
            pl.BlockSpec((PROMPT_SEQ, 3 * D_MODEL), lambda b: (b, 0)),
        ],
        out_specs=pl.BlockSpec((PROMPT_SEQ, D_MODEL), lambda b: (b, 0)),
        out_shape=jax.ShapeDtypeStruct((N_PROMPT, D_MODEL), bf16),
        compiler_params=_cparams(("parallel",)),
        name="ctx1",
    )(lamv, subln, qkv)


def _lat1_kernel(lam_ref, subln_ref, q_ref, k_ref, v_ref, ck_ref, cv_ref, o_ref, *, lam_init):
    lo = _lane_lo((1, LANES))
    lam = _diff_lambda(lam_ref, lam_init)
    heads = []
    for h in range(8):
        sl = slice(h * LANES, (h + 1) * LANES)
        heads.append((h, q_ref[:, sl], [ck_ref[:, sl].astype(bf16), k_ref[:, sl]],
                      [cv_ref[:, sl].astype(bf16), v_ref[:, sl]]))
    _diff_heads(heads, o_ref, lam, subln_ref[...], lo, lam_init)


def _lat1(lamv, subln, qkv, ck, cv, lam_init):
    sb = N_PROMPT // SAMPLE_SEQ
    nq = SAMPLE_SEQ // TM
    return pl.pallas_call(
        functools.partial(_lat1_kernel, lam_init=lam_init),
        grid=(N_SAMPLE_BATCH, nq),
        in_specs=[
            pl.BlockSpec((8, HEAD_DIM), lambda b, n: (0, 0)),
            pl.BlockSpec((1, LANES), lambda b, n: (0, 0)),
            pl.BlockSpec((TM, D_MODEL), lambda b, n: (N_PROMPT_TILES + b * nq + n, 0)),
            pl.BlockSpec((SAMPLE_SEQ, D_MODEL), lambda b, n: (sb + b, 1)),
            pl.BlockSpec((SAMPLE_SEQ, D_MODEL), lambda b, n: (sb + b, 2)),
            pl.BlockSpec((None, PAST_LEN, D_MODEL), lambda b, n: (b, 0, 0)),
            pl.BlockSpec((None, PAST_LEN, D_MODEL), lambda b, n: (b, 0, 0)),
        ],
        out_specs=pl.BlockSpec((TM, D_MODEL), lambda b, n: (b * nq + n, 0)),
        out_shape=jax.ShapeDtypeStruct((N_SAMPLE, D_MODEL), bf16),
        compiler_params=_cparams(("parallel", "arbitrary")),
        name="lat1",
    )(lamv, subln, qkv, qkv, qkv, ck, cv)


def _split_bf16(a):
    hi = a.astype(bf16)
    return hi, (a - hi.astype(f32)).astype(bf16)


def _route_kernel(*refs, n_x):
    x_refs = refs[:n_x]
    (op_ref, os_ref, mod_ref, g_ref, wo_ref, rwt_ref, rb_ref,
     xnew_ref, xloc_ref, slots_ref, gate_ref, len_ref) = refs[n_x:]
    logits = [_route_logits(t, x_refs, op_ref, os_ref, mod_ref, g_ref, wo_ref, rwt_ref, xnew_ref)
              for t in range(ROUTE_TILES)]
    tiles = _route_tiles(jnp.concatenate([lg for _, lg in logits], axis=1), rb_ref, slots_ref, gate_ref, len_ref)
    for t, (slots, run_len) in enumerate(tiles):
        _route_dispatch(t, slots, logits[t][0], run_len, xloc_ref)


def _route_logits(t, x_refs, op_ref, os_ref, mod_ref, g_ref, wo_ref, rwt_ref, xnew_ref):
    is_prompt = pl.program_id(0) < N_PROMPT_TILES // ROUTE_TILES
    rows = slice(t * TM, (t + 1) * TM)
    attn = jnp.where(is_prompt, op_ref[rows, :], os_ref[rows, :])
    x_in = x_refs[0][rows, :] if len(x_refs) == 1 else jnp.where(is_prompt, x_refs[0][rows, :], x_refs[1][rows, :])
    x = x_in + mod_ref[2:3, :] * _dot(attn, wo_ref[...])
    xnew_ref[rows, :] = x
    h = _norm_mod(x, g_ref[...], mod_ref[4:5, :], mod_ref[3:4, :])
    h_hi, h_lo = _split_bf16(h)
    w_hi, w_lo = _split_bf16(rwt_ref[...])
    return h_hi, _dot_nt(w_hi, h_hi) + (_dot_nt(w_hi, h_lo) + _dot_nt(w_lo, h_hi))


def _route_tiles(logits, rb_ref, slots_ref, gate_ref, len_ref):
    ng, ge = N_GROUPS, N_EXPERTS // N_GROUPS
    n = ROUTE_TILES * TM
    tile = lambda a, t: a[..., t * TM:(t + 1) * TM]
    scores = jax.nn.sigmoid(logits)
    biased = scores + rb_ref[...]
    s3 = scores.reshape(ng, ge, n)
    b3 = biased.reshape(ng, ge, n)
    in_group = lax.broadcasted_iota(i32, (ng, ge, n), 1).astype(f32)
    group_id = lax.broadcasted_iota(i32, (ng, 1, n), 0).astype(f32)
    expert_id = lax.broadcasted_iota(i32, (ng, ge, n), 0).astype(f32) * ge + in_group

    def max01(a):
        return jnp.max(jnp.max(a, axis=0, keepdims=True), axis=1, keepdims=True)

    def min01(a):
        return jnp.min(jnp.min(a, axis=0, keepdims=True), axis=1, keepdims=True)

    def sum01(a):
        return jnp.sum(jnp.sum(a, axis=0, keepdims=True), axis=1, keepdims=True)

    m1 = jnp.max(b3, axis=1, keepdims=True)
    first = jnp.min(jnp.where(b3 == m1, in_group, ge), axis=1, keepdims=True)
    m2 = jnp.max(jnp.where(in_group == first, -jnp.inf, b3), axis=1, keepdims=True)
    gscore = m1 + m2
    gsel = jnp.zeros((ng, 1, n), f32)
    for _ in range(TOPK_GROUPS):
        gm = jnp.max(gscore, axis=0, keepdims=True)
        gi = jnp.min(jnp.where(gscore == gm, group_id, ng), axis=0, keepdims=True)
        hit = group_id == gi
        gsel = jnp.where(hit, 1.0, gsel)
        gscore = jnp.where(hit, -jnp.inf, gscore)
    cand = jnp.where(jnp.broadcast_to(gsel, (ng, ge, n)) > 0.0, b3, -jnp.inf)
    top_e, top_w = [], []
    for _ in range(TOP_K):
        em = max01(cand)
        ei = min01(jnp.where(cand == em, expert_id, N_EXPERTS))
        hit = expert_id == ei
        top_e.append(ei)
        top_w.append(sum01(jnp.where(hit, s3, 0.0)))
        cand = jnp.where(hit, -jnp.inf, cand)
    wsum = functools.reduce(lambda a, b: a + b, top_w)
    sel3 = jnp.zeros((ng, ge, n), f32)
    for k, (ei, w) in enumerate(zip(top_e, top_w)):
        gate = (w / wsum * ROUTED_SCALE).reshape(1, n)
        for t in range(ROUTE_TILES):
            gate_ref[t, k:k + 1, :] = tile(gate, t)
        sel3 = jnp.where(expert_id == ei, 1.0, sel3)
    sel = sel3.reshape(N_EXPERTS, n)

    r_i = lax.broadcasted_iota(i32, (N_EXPERTS, N_EXPERTS), 0)
    c_i = lax.broadcasted_iota(i32, (N_EXPERTS, N_EXPERTS), 1)
    lower = jnp.where(c_i < r_i, 1.0, 0.0).astype(bf16)
    run_lens, run_offs = [], []
    for t in range(ROUTE_TILES):
        cnt = jnp.sum(tile(sel, t), axis=1, keepdims=True)
        run_len = jnp.ceil(cnt * (1.0 / CHUNK)) * CHUNK
        run_off = _dot(lower, jnp.broadcast_to(run_len, (N_EXPERTS, LANES)).astype(bf16))[:, 0:1]
        run_lens.append(run_len)
        run_offs.append(jnp.broadcast_to(run_off, (N_EXPERTS, TM)))
    t_r = lax.broadcasted_iota(i32, (n, n), 0)
    t_c = lax.broadcasted_iota(i32, (n, n), 1)
    before = jnp.where(jnp.logical_and(t_r < t_c, t_r // TM == t_c // TM), 1.0, 0.0).astype(bf16)
    rank = _dot(sel.astype(bf16), before)
    slot3 = (jnp.concatenate(run_offs, axis=1) + rank).reshape(ng, ge, n)
    slots = [sum01(jnp.where(expert_id == ei, slot3, 0.0)).reshape(1, n).astype(i32) for ei in top_e]
    out = []
    for t in range(ROUTE_TILES):
        for k in range(TOP_K):
            slots_ref[t, k:k + 1, :] = tile(slots[k], t)
        slots_ref[t, TOP_K:8, :] = jnp.full((8 - TOP_K, TM), -1, i32)
        gate_ref[t, TOP_K:8, :] = jnp.zeros((8 - TOP_K, TM), f32)
        cnt_row = _dot_nt(jnp.ones((8, TM), bf16), tile(sel, t).astype(bf16))
        len_ref[t] = (jnp.ceil(cnt_row * (1.0 / CHUNK)) * CHUNK).astype(i32)
        out.append(([tile(sl, t) for sl in slots], run_lens[t]))
    return out


def _route_dispatch(t, slots, h_hi, run_len, xloc_ref):
    rows = ROUTE_ROWS

    def body(c, carry):
        base = pl.multiple_of(c * rows, rows)
        row_id = base.astype(jnp.int16) + lax.broadcasted_iota(jnp.int16, (rows, TM), 0)
        p = jnp.zeros((rows, TM), bf16)
        for k in range(TOP_K):
            p = jnp.where(row_id == slots[k].astype(jnp.int16), jnp.ones((), bf16), p)
        xloc_ref[pl.ds(t * SLOTS + base, rows), :] = _dot(p, h_hi).astype(bf16)
        return carry

    def zero_body(c, carry):
        base = pl.multiple_of(c * rows, rows)
        xloc_ref[pl.ds(t * SLOTS + base, rows), :] = jnp.zeros((rows, D_MODEL), bf16)
        return carry

    n_used = (jnp.sum(run_len).astype(i32) + (rows - 1)) // rows
    lax.fori_loop(0, n_used, body, 0)
    lax.fori_loop(n_used, SLOTS // rows, zero_body, 0)


def _route(x_parts, o_prompt, o_sample, mod_l, g, w_out, rwt, rb):
    per_tile = lambda i: (i, 0, 0)
    rt = ROUTE_TILES
    tm = rt * TM
    return pl.pallas_call(
        functools.partial(_route_kernel, n_x=len(x_parts)),
        grid=(N_TILES // rt,),
        in_specs=_x_specs(x_parts, tm) + [
            pl.BlockSpec((tm, D_MODEL), lambda i: _prompt_block(i, tm)),
            pl.BlockSpec((tm, D_MODEL), lambda i: _sample_block(i, tm)),
            pl.BlockSpec((None, 6, D_MODEL), lambda i: (_mod_row(i, tm), 0, 0)),
            pl.BlockSpec((1, D_MODEL), lambda i: (0, 0)),
            pl.BlockSpec((D_MODEL, D_MODEL), lambda i: (0, 0)),
            pl.BlockSpec((N_EXPERTS, D_MODEL), lambda i: (0, 0)),
            pl.BlockSpec((N_EXPERTS, 1), lambda i: (0, 0)),
        ],
        out_specs=[
            pl.BlockSpec((tm, D_MODEL), lambda i: (i, 0)),
            pl.BlockSpec((rt * SLOTS, D_MODEL), lambda i: (i, 0)),
            pl.BlockSpec((rt, 8, TM), per_tile),
            pl.BlockSpec((rt, 8, TM), per_tile),
            pl.BlockSpec((rt, 8, N_EXPERTS), per_tile),
        ],
        out_shape=[
            jax.ShapeDtypeStruct((N_TOK, D_MODEL), f32),
            jax.ShapeDtypeStruct((N_TILES * SLOTS, D_MODEL), bf16),
            jax.ShapeDtypeStruct((N_TILES, 8, TM), i32),
            jax.ShapeDtypeStruct((N_TILES, 8, TM), f32),
            jax.ShapeDtypeStruct((N_TILES, 8, N_EXPERTS), i32),
        ],
        compiler_params=_cparams(("parallel",)),
        name="route",
    )(*x_parts, o_prompt, o_sample, mod_l, g, w_out, rwt, rb)


def _moe_plan(run_len):
    nt, ne = run_len.shape

    def excl_cumsum(a):
        n = a.shape[-1]
        earlier = np.arange(n)[None, :] < np.arange(n)[:, None]
        return jnp.sum(jnp.where(earlier, a[..., None, :], 0), axis=-1)

    def first_diff(a):
        return a - jnp.concatenate([jnp.zeros_like(a[..., :1]), a[..., :-1]], axis=-1)

    off_loc = excl_cumsum(run_len)
    before = excl_cumsum(run_len.T).T
    n_e = jnp.sum(run_len, axis=0)
    n_pad = -(-n_e // GM) * GM
    g_start = excl_cumsum(n_pad)
    total = jnp.sum(n_pad)
    run_dst = g_start[None, :] + before
    run_src = jnp.arange(nt, dtype=i32)[:, None] * SLOTS + off_loc
    dst_f = run_dst.T.reshape(-1)
    shift_f = first_diff((run_src - run_dst).T.reshape(-1))
    rows = jnp.arange((G_TILES + GMM_ITEM_TILES - 1) * G_CHUNKS, dtype=i32) * CHUNK
    shift = jnp.sum(jnp.where(dst_f[None, :] <= rows[:, None], shift_f[None, :], 0), axis=1)
    in_run = jnp.any((g_start[None, :] <= rows[:, None]) & (rows[:, None] < (g_start + n_e)[None, :]), axis=1)
    chunk_src = (jnp.where(in_run, rows + shift, 0) // CHUNK).astype(i32)
    loc_rows = jnp.arange(SLOT_CHUNKS, dtype=i32) * CHUNK
    shift_l = first_diff(run_dst - off_loc)
    shift = jnp.sum(jnp.where(off_loc[:, None, :] <= loc_rows[None, :, None], shift_l[:, None, :], 0), axis=2)
    used = jnp.sum(run_len, axis=1)
    chunk_map = jnp.where(loc_rows[None, :] < used[:, None], (loc_rows[None, :] + shift) // CHUNK, 0).astype(i32)
    tile_start, n_tiles = g_start // GM, n_pad // GM
    n_items = -(-n_tiles // GMM_ITEM_TILES)
    item_start = excl_cumsum(n_items)
    items = jnp.arange(GMM_MAX_ITEMS, dtype=i32)
    owner = items[:, None] >= item_start[None, :]
    e_first = jnp.sum(jnp.where(owner, first_diff(tile_start - GMM_ITEM_TILES * item_start)[None, :], 0), axis=1)
    item_tile = e_first + GMM_ITEM_TILES * items
    e_end = jnp.sum(jnp.where(owner, first_diff(tile_start + n_tiles)[None, :], 0), axis=1)
    item_cnt = jnp.clip(e_end - item_tile, 0, GMM_ITEM_TILES)
    gmm_plan = tuple(a.astype(i32) for a in (item_start, n_items, item_tile, item_cnt, chunk_src))
    long_tiles = (used > COMBINE_SURE_CHUNKS * CHUNK).astype(i32)
    return gmm_plan, chunk_map.reshape(-1), long_tiles


def _gmm_in_copy(xloc_hbm, xbuf, sem, src_chunk, slot, c):
    return pltpu.make_async_copy(xloc_hbm.at[src_chunk], xbuf.at[slot, c], sem.at[slot])


def _gmm_out_copy(ybuf, y_hbm, sem, tile, slot, n_tiles):
    chunks = n_tiles * G_CHUNKS
    return pltpu.make_async_copy(ybuf.at[slot, pl.ds(0, chunks)],
                                 y_hbm.at[pl.ds(tile * G_CHUNKS, chunks)], sem.at[slot])


def _gmm_kernel(i0_ref, ni_ref, it_ref, ic_ref, cs_ref, xloc_hbm, wg_ref, wu_ref, wd_ref, y_hbm,
                xbuf, ybuf, zbuf, wg_b, wu_b, wd_b, in_sem, out_sem, zsem):
    e = pl.program_id(0)
    last = pl.num_programs(0) - 1
    n_items = ni_ref[e]
    first_item = i0_ref[e]
    total_items = i0_ref[last] + ni_ref[last]
    last_item = total_items - 1
    total_tiles = it_ref[last_item] + ic_ref[last_item]

    def start_in(item):
        first = it_ref[item] * G_CHUNKS
        for c in range(GMM_ITEM_TILES * G_CHUNKS):
            _gmm_in_copy(xloc_hbm, xbuf, in_sem, cs_ref[first + c], item % GMM_SLOTS, c).start()

    def wait_in(item):
        for c in range(GMM_ITEM_TILES * G_CHUNKS):
            _gmm_in_copy(xloc_hbm, xbuf, in_sem, 0, item % GMM_SLOTS, c).wait()

    def out_copy(item, fn):
        for cnt in range(1, GMM_ITEM_TILES + 1):
            @pl.when(ic_ref[item] == cnt)
            def _():
                fn(_gmm_out_copy(ybuf, y_hbm, out_sem, it_ref[item], item % GMM_SLOTS, cnt))

    @pl.when(e == 0)
    def _():
        for item in range(GMM_SLOTS - 1):
            start_in(item)
        zbuf[...] = jnp.zeros(zbuf.shape, zbuf.dtype)

    def tail_copies(fn):
        for j in range(_GMM_TAIL_PER_STEP):
            tile = total_tiles + e + j * N_EXPERTS

            @pl.when(tile < G_TILES)
            def _():
                fn(pltpu.make_async_copy(zbuf, y_hbm.at[pl.ds(tile * G_CHUNKS, G_CHUNKS)], zsem.at[0]))

    tail_copies(lambda cp: cp.start())

    @pl.when(n_items > 0)
    def _():
        wg_b[...] = wg_ref[...].astype(bf16)
        wu_b[...] = wu_ref[...].astype(bf16)
        wd_b[...] = wd_ref[...].astype(bf16)

    def body(j, carry):
        item = first_item + j
        slot = item % GMM_SLOTS

        @pl.when(item + (GMM_SLOTS - 1) < total_items)
        def _():
            start_in(item + (GMM_SLOTS - 1))

        wait_in(item)

        @pl.when(item >= GMM_SLOTS)
        def _():
            out_copy(item - GMM_SLOTS, lambda cp: cp.wait())

        for cnt in range(1, GMM_ITEM_TILES + 1):
            @pl.when(ic_ref[item] == cnt)
            def _():
                rows = cnt * GM
                chunks = cnt * G_CHUNKS
                x = xbuf[slot, 0:chunks].reshape(rows, D_MODEL)
                act = _silu(_dot(x, wg_b[...])) * _dot(x, wu_b[...])
                y = _dot(act.astype(bf16), wd_b[...]).astype(bf16)
                ybuf[slot, 0:chunks] = y.reshape(chunks, CHUNK, D_MODEL)

        out_copy(item, lambda cp: cp.start())
        return carry

    lax.fori_loop(0, n_items, body, 0)
    tail_copies(lambda cp: cp.wait())

    @pl.when(e == last)
    def _():
        for back in range(1, GMM_SLOTS + 1):
            out_copy(total_items - back, lambda cp: cp.wait())


def _gmm(plan, xloc, wg, wu, wd, layer):
    rows = GMM_ITEM_TILES * GM
    w_idx = lambda e, *_: (layer, e, 0, 0)
    grid_spec = pltpu.PrefetchScalarGridSpec(
        num_scalar_prefetch=5,
        grid=(N_EXPERTS,),
        in_specs=[
            pl.BlockSpec(memory_space=pl.ANY),
            pl.BlockSpec((None, None, D_MODEL, EXPERT_DIM), w_idx),
            pl.BlockSpec((None, None, D_MODEL, EXPERT_DIM), w_idx),
            pl.BlockSpec((None, None, EXPERT_DIM, D_MODEL), w_idx),
        ],
        out_specs=pl.BlockSpec(memory_space=pl.ANY),
        scratch_shapes=[pltpu.VMEM((GMM_SLOTS, rows // CHUNK, CHUNK, D_MODEL), bf16),
                        pltpu.VMEM((GMM_SLOTS, rows // CHUNK, CHUNK, D_MODEL), bf16),
                        pltpu.VMEM((G_CHUNKS, CHUNK, D_MODEL), bf16),
                        pltpu.VMEM((D_MODEL, EXPERT_DIM), bf16), pltpu.VMEM((D_MODEL, EXPERT_DIM), bf16),
                        pltpu.VMEM((EXPERT_DIM, D_MODEL), bf16),
                        pltpu.SemaphoreType.DMA((GMM_SLOTS,)), pltpu.SemaphoreType.DMA((GMM_SLOTS,)),
                        pltpu.SemaphoreType.DMA((1,))],
    )
    return pl.pallas_call(
        _gmm_kernel,
        grid_spec=grid_spec,
        out_shape=jax.ShapeDtypeStruct((G_TILES * G_CHUNKS, CHUNK, D_MODEL), bf16),
        compiler_params=_cparams(("arbitrary",)),
        name="gmm",
    )(*plan, xloc, wg, wu, wd)


def _combine_copy(y_hbm, ybuf, sem, sorted_chunk, slot, c):
    return pltpu.make_async_copy(y_hbm.at[sorted_chunk], ybuf.at[slot, c], sem.at[slot])


def _combine_kernel(cm_ref, long_ref, y_hbm, slots_ref, gate_ref, x_ref, mod_ref, g_ref, sg_ref, su_ref, sd_ref,
                    *rest, final):
    if final:
        gf_ref, yp_ref, ys_ref, ybuf, sem = rest
    else:
        o_ref, ybuf, sem = rest
    i = pl.program_id(0)
    n = pl.num_programs(0)
    ahead = COMBINE_SLOTS - 1
    slot = i % COMBINE_SLOTS

    def for_chunks(tile, fn):
        for c in range(COMBINE_SURE_CHUNKS):
            fn(c)

        @pl.when(long_ref[tile] == 1)
        def _():
            for c in range(COMBINE_SURE_CHUNKS, SLOT_CHUNKS):
                fn(c)

    def start(tile, s):
        for_chunks(tile, lambda c: _combine_copy(y_hbm, ybuf, sem, cm_ref[tile * SLOT_CHUNKS + c], s, c).start())

    def wait(tile, s):
        for_chunks(tile, lambda c: _combine_copy(y_hbm, ybuf, sem, 0, s, c).wait())

    @pl.when(i == 0)
    def _():
        ybuf[:, COMBINE_SURE_CHUNKS:] = jnp.zeros((COMBINE_SLOTS, SLOT_CHUNKS - COMBINE_SURE_CHUNKS, CHUNK, D_MODEL), bf16)
        for tile in range(ahead):
            start(tile, tile)

    wait(i, slot)
    start((i + ahead) % n, (i + ahead) % COMBINE_SLOTS)

    x = x_ref[...]
    hb = _norm_mod(x, g_ref[...], mod_ref[4:5, :], mod_ref[3:4, :]).astype(bf16)
    shared = _dot((_silu(_dot(hb, sg_ref[...])) * _dot(hb, su_ref[...])).astype(bf16), sd_ref[...])
    row_id = lax.broadcasted_iota(jnp.int16, (SLOTS, TM), 0)
    p = jnp.zeros((SLOTS, TM), bf16)
    for k in range(TOP_K):
        p = jnp.where(row_id == slots_ref[k:k + 1, :].astype(jnp.int16), gate_ref[k:k + 1, :].astype(bf16), p)
    routed = lax.dot_general(p, ybuf[slot].reshape(SLOTS, D_MODEL), (((0,), (0,)), ((), ())),
                             preferred_element_type=f32)
    out = x + mod_ref[5:6, :] * (routed + shared)
    if final:
        y = (out * lax.rsqrt(jnp.mean(out * out, axis=-1, keepdims=True) + EPS)) * gf_ref[...]

        @pl.when(i < N_PROMPT_TILES)
        def _():
            yp_ref[...] = y

        @pl.when(i >= N_PROMPT_TILES)
        def _():
            ys_ref[...] = y
    else:
        o_ref[...] = out

    @pl.when(i == n - 1)
    def _():
        for k in range(1, ahead + 1):
            wait((i + k) % n, (i + k) % COMBINE_SLOTS)


def _combine(chunk_map, long_tiles, y, slots, gates, x, mod_l, g, sg, su, sd, final_g=None):
    shd = sg.shape[1]
    final = final_g is not None
    row_spec = pl.BlockSpec((TM, D_MODEL), lambda i, *_: (i, 0))
    vec_spec = pl.BlockSpec((1, D_MODEL), lambda i, *_: (0, 0))
    if final:
        out_specs = [pl.BlockSpec((TM, D_MODEL), lambda i, *_: _prompt_block(i)),
                     pl.BlockSpec((TM, D_MODEL), lambda i, *_: _sample_block(i))]
        out_shape = [jax.ShapeDtypeStruct((N_PROMPT, D_MODEL), f32), jax.ShapeDtypeStruct((N_SAMPLE, D_MODEL), f32)]
    else:
        out_specs, out_shape = row_spec, jax.ShapeDtypeStruct((N_TOK, D_MODEL), f32)
    grid_spec = pltpu.PrefetchScalarGridSpec(
        num_scalar_prefetch=2,
        grid=(N_TILES,),
        in_specs=[
            pl.BlockSpec(memory_space=pl.ANY),
            pl.BlockSpec((None, 8, TM), lambda i, *_: (i, 0, 0)),
            pl.BlockSpec((None, 8, TM), lambda i, *_: (i, 0, 0)),
            row_spec,
            pl.BlockSpec((None, 6, D_MODEL), lambda i, *_: (_mod_row(i), 0, 0)),
            vec_spec,
            pl.BlockSpec((D_MODEL, shd), lambda i, *_: (0, 0)),
            pl.BlockSpec((D_MODEL, shd), lambda i, *_: (0, 0)),
            pl.BlockSpec((shd, D_MODEL), lambda i, *_: (0, 0)),
        ] + ([vec_spec] if final else []),
        out_specs=out_specs,
        scratch_shapes=[pltpu.VMEM((COMBINE_SLOTS, SLOT_CHUNKS, CHUNK, D_MODEL), bf16),
                        pltpu.SemaphoreType.DMA((COMBINE_SLOTS,))],
    )
    args = (chunk_map, long_tiles, y, slots, gates, x, mod_l, g, sg, su, sd) + ((final_g,) if final else ())
    return pl.pallas_call(
        functools.partial(_combine_kernel, final=final),
        grid_spec=grid_spec,
        out_shape=out_shape,
        compiler_params=_cparams(("arbitrary",)),
        name="combine",
    )(*args)


def _moe(x_parts, o_prompt, o_sample, w_out, mod_l, g, rwt, rb, wg, wu, wd, layer, sg, su, sd, final_g=None):
    x, xloc, slots, gates, run_len = _route(x_parts, o_prompt, o_sample, mod_l, g, w_out, rwt, rb)
    gmm_plan, chunk_map, long_tiles = _moe_plan(run_len[:, 0, :])
    y = _gmm(gmm_plan, xloc.reshape(N_TILES * SLOT_CHUNKS, CHUNK, D_MODEL), wg, wu, wd, layer)
    return _combine(chunk_map, long_tiles, y, slots, gates, x, mod_l, g, sg, su, sd, final_g)


_L0_CHUNKS = (
    (0, 512, 0, (0, 1, 2, 3), ()),
    (512, 1024, 768, (), ()),
    (1024, 1536, 1280, (), ((2, 0, 512, 0, False),)),
    (1536, 2048, 1792, (), ((3, 0, 512, 0, False),)),
    (2048, 2304, 512, (0,), ((0, 0, 128, 0, False), (1, 128, 256, 0, False))),
)
_L0_KV_OUTS = (("T", 128), ("T", 128), ("T", 512), ("T", 512))
_L1_CHUNKS = (
    (0, 512, 0, (0, 1, 2, 3), ()),
    (512, 1024, 512, (0, 1, 2, 3), ()),
    (1024, 1536, 1024, (0, 1, 2, 3), ((0, 0, 512, 0, False),)),
    (1536, 2048, 1536, (0, 1, 2, 3), ((0, 0, 512, 512, False),)),
    (2048, 2560, 2048, (), ((1, 0, 512, 0, True),)),
    (2560, 3072, 2560, (), ((1, 0, 512, 512, True),)),
)
_L1_KV_OUTS = (("T", 1024), ("H", 8))


def _from_feature_major(kt, *head_dims):
    nb, _, s = kt.shape
    nd = len(head_dims)
    return kt.reshape(nb, *head_dims, s).transpose(0, nd + 1, *range(1, nd + 1))[:, None]


def kernel(x_prompt, x_sample, cache_a_k, cache_a_v, cache_b_k, cache_b_v, cache_c_k, cache_c_v, c, c_ctx, w_mod, b_mod, norm_mix, norm_ffn, w_in_ab, w_out_ab, sink_a, rel_bias_b, w_in_c, w_out_c, lam_q1, lam_k1, lam_q2, lam_k2, subln_c, router_w, router_bias, exp_w_gate, exp_w_up, exp_w_down, sh_w_gate, sh_w_up, sh_w_down, final_norm):
    x = (x_prompt.reshape(N_PROMPT, D_MODEL), x_sample.reshape(N_SAMPLE, D_MODEL))
    cond8 = jnp.concatenate([c_ctx[None, :], c, jnp.zeros((8 - 1 - N_SAMPLE_BATCH, D_MODEL), f32)], axis=0)
    mod = _adaln(cond8, w_mod, b_mod).reshape(DEPTH, 8, 6, D_MODEL)
    rope_tabs = _rope_tables()
    new_kv = {}
    for layer in range(DEPTH):
        li = layer // 2
        mod_l = mod[layer]
        g_mix = norm_mix[layer][None, :]
        g_ffn = norm_ffn[layer][None, :]
        if layer % 2 == 0:
            w_in = w_in_ab[li].astype(bf16)
            qkv, ak, av, bk, bv = _inproj(x, mod_l, g_mix, w_in, rope_tabs, _L0_CHUNKS, _L0_KV_OUTS)
            new_kv["a_k"], new_kv["a_v"], new_kv["b_k"], new_kv["b_v"] = ak, av, bk, bv
            o_p = _ctx0(sink_a[li], qkv)
            o_s = _lat0(sink_a[li], qkv,
                        cache_a_k[:, li].reshape(N_SAMPLE_BATCH, PAST_LEN, LANES),
                        cache_a_v[:, li].reshape(N_SAMPLE_BATCH, PAST_LEN, LANES),
                        cache_b_k[:, li].reshape(N_SAMPLE_BATCH, PAST_LEN, 512),
                        cache_b_v[:, li].reshape(N_SAMPLE_BATCH, PAST_LEN, 512),
                        _na_bias_tiles(rel_bias_b[li]))
            w_out = w_out_ab[li].astype(bf16)
        else:
            lam_init = 0.8 - 0.6 * math.exp(-0.3 * layer)
            qkv, ck, cv = _inproj(x, mod_l, g_mix, w_in_c[li].astype(bf16), rope_tabs, _L1_CHUNKS, _L1_KV_OUTS)
            new_kv["c_k"], new_kv["c_v"] = ck, cv
            lamv = jnp.concatenate([lam_q1[li][None], lam_k1[li][None], lam_q2[li][None], lam_k2[li][None],
                                    jnp.zeros((4, HEAD_DIM), f32)], axis=0)
            subln = subln_c[li][None, :]
            o_p = _ctx1(lamv, subln, qkv, lam_init)
            o_s = _lat1(lamv, subln, qkv,
                        cache_c_k[:, li].reshape(N_SAMPLE_BATCH, PAST_LEN, D_MODEL),
                        cache_c_v[:, li].reshape(N_SAMPLE_BATCH, PAST_LEN, D_MODEL), lam_init)
            w_out = w_out_c[li].astype(bf16)
        last = layer == DEPTH - 1
        x = _moe(x, o_p, o_s, w_out, mod_l, g_ffn, router_w[layer].T, router_bias[layer][:, None],
                 exp_w_gate, exp_w_up, exp_w_down, layer,
                 sh_w_gate[layer].astype(bf16), sh_w_up[layer].astype(bf16), sh_w_down[layer].astype(bf16),
                 final_norm[None, :] if last else None)
        x = x if last else (x,)
    y_prompt, y_sample = x
    nb, s = N_PROMPT_BATCH, PROMPT_SEQ
    return (y_prompt.reshape(nb, s, D_MODEL), y_sample.reshape(N_SAMPLE_BATCH, SAMPLE_SEQ, D_MODEL),
            _from_feature_major(new_kv["a_k"], 2, HEAD_DIM), _from_feature_major(new_kv["a_v"], 2, HEAD_DIM),
            _from_feature_major(new_kv["b_k"], 8, HEAD_DIM), _from_feature_major(new_kv["b_v"], 8, HEAD_DIM),
            _from_feature_major(new_kv["c_k"], 8, 2, HEAD_DIM), new_kv["c_v"].reshape(nb, 1, s, 8, 2 * HEAD_DIM))
```

```python
import functools
import math

import jax
import jax.numpy as jnp
import numpy as np
from jax import lax
from jax.experimental import pallas as pl
from jax.experimental.pallas import tpu as pltpu

f32 = jnp.float32
bf16 = jnp.bfloat16
i32 = jnp.int32

D_MODEL = 1024
N_PROMPT_BATCH = 16
PROMPT_SEQ = 256
DEPTH = 2
N_SAMPLE_BATCH = 2
SAMPLE_SEQ = 2048
PAST_LEN = 512
GRID_W = 64
HEAD_DIM = 64
ROPE_THETA = 10000.0
EPS = 1e-6
A_WINDOW = 128
NA_ROWS = 8
NA_COLS = 16
N_EXPERTS = 64
TOP_K = 6
N_GROUPS = 8
TOPK_GROUPS = 4
EXPERT_DIM = 256
ROUTED_SCALE = 2.5
Q_SCALE = HEAD_DIM ** -0.5

N_PROMPT = N_PROMPT_BATCH * PROMPT_SEQ
N_SAMPLE = N_SAMPLE_BATCH * SAMPLE_SEQ
N_TOK = N_PROMPT + N_SAMPLE

LANES = 128
TM = 256
TD = 512
N_PROMPT_TILES = N_PROMPT // TM
N_TILES = N_TOK // TM
QB = 128
CHUNK = 16
SLOTS = -(-(TM * TOP_K + N_EXPERTS * (CHUNK - 1)) // 256) * 256
SLOT_CHUNKS = SLOTS // CHUNK
ROUTE_ROWS = 512
ROUTE_TILES = 2
COMBINE_SLOTS = 3
COMBINE_SURE_CHUNKS = 136
GM = 256
_MAX_SORTED = TM * TOP_K * N_TILES + N_TILES * N_EXPERTS * (CHUNK - 1) + N_EXPERTS * (GM - CHUNK)
G_TILES = -(-_MAX_SORTED // GM)
G_CHUNKS = GM // CHUNK
GMM_ITEM_TILES = 2
GMM_SLOTS = 4
GMM_MAX_ITEMS = (G_TILES + N_EXPERTS * (GMM_ITEM_TILES - 1)) // GMM_ITEM_TILES
_GMM_TAIL_PER_STEP = -(-(G_TILES - TM * TOP_K * N_TILES // GM) // N_EXPERTS)
ADA_COLS = 1536
DIFF_HEADS_PER_PASS = 4
VMEM_LIMIT = 56 * 1024 * 1024

NEG = -1e30


def _cparams(sem):
    return pltpu.CompilerParams(dimension_semantics=sem, vmem_limit_bytes=VMEM_LIMIT)


def _mod_row(i, tm=TM):
    return jnp.where(i < N_PROMPT // tm, 0, 1 + (i - N_PROMPT // tm) // (SAMPLE_SEQ // tm))


def _prompt_block(i, tm=TM):
    return (jnp.minimum(i, N_PROMPT // tm - 1), 0)


def _sample_block(i, tm=TM):
    return (jnp.maximum(i - N_PROMPT // tm, 0), 0)


def _x_specs(parts, tm=TM):
    if len(parts) == 1:
        return [pl.BlockSpec((tm, D_MODEL), lambda i, *_: (i, 0))]
    return [pl.BlockSpec((tm, D_MODEL), lambda i, *_: _prompt_block(i, tm)),
            pl.BlockSpec((tm, D_MODEL), lambda i, *_: _sample_block(i, tm))]


def _load_x(i, x_refs, tm=TM):
    if len(x_refs) == 1:
        return x_refs[0][...]
    return jnp.where(i < N_PROMPT // tm, x_refs[0][...], x_refs[1][...])


def _norm_mod(x, g, scale, shift):
    y = x * lax.rsqrt(jnp.mean(x * x, axis=-1, keepdims=True) + EPS)
    return (y * g) * (1.0 + scale) + shift


def _silu(x):
    return x * jax.nn.sigmoid(x)


def _dot(a, b):
    return jnp.dot(a, b, preferred_element_type=f32)


def _dot_nt(a, b):
    return lax.dot_general(a, b, (((1,), (1,)), ((), ())), preferred_element_type=f32)


def _adaln_kernel(cond_ref, w_ref, b_ref, o_ref):
    s = _silu(cond_ref[...]).astype(bf16)
    o_ref[...] = _dot(s, w_ref[...].astype(bf16)) + b_ref[...]


def _adaln(cond8, w_mod, b_mod):
    n6 = 6 * D_MODEL
    return pl.pallas_call(
        _adaln_kernel,
        grid=(DEPTH, n6 // ADA_COLS),
        in_specs=[
            pl.BlockSpec((8, D_MODEL), lambda l, j: (0, 0)),
            pl.BlockSpec((None, D_MODEL, ADA_COLS), lambda l, j: (l, 0, j)),
            pl.BlockSpec((None, 1, ADA_COLS), lambda l, j: (l, 0, j)),
        ],
        out_specs=pl.BlockSpec((None, 8, ADA_COLS), lambda l, j: (l, 0, j)),
        out_shape=jax.ShapeDtypeStruct((DEPTH, 8, n6), f32),
        compiler_params=_cparams(("parallel", "parallel")),
        name="adaln",
    )(cond8, w_mod, b_mod.reshape(DEPTH, 1, n6))


def _rope_block(blk, cos, sin_a, sin_b):
    return blk * cos + pltpu.roll(blk, LANES - 16, 1) * sin_a + pltpu.roll(blk, 16, 1) * sin_b


def _inproj_kernel(*refs, chunks, n_x):
    x_refs, kv_refs = refs[:n_x], refs[n_x + 7:]
    mod_ref, g_ref, w_ref, cos_ref, sa_ref, sb_ref, qkv_ref = refs[n_x:n_x + 7]
    i = pl.program_id(0)
    is_prompt = i < N_PROMPT // TD
    kb = 256

    def normed_input():
        x = _load_x(i, x_refs, TD)
        rinv = lax.rsqrt(jnp.mean(x * x, axis=-1, keepdims=True) + EPS)
        c0, c1, s0 = chunks[0][:3]
        blocks, acc = [], None
        for k in range(D_MODEL // kb):
            sl = slice(k * kb, (k + 1) * kb)
            hk = ((((x[:, sl] * rinv) * g_ref[:, sl]) * (1.0 + mod_ref[1:2, sl])) + mod_ref[0:1, sl]).astype(bf16)
            part = _dot(hk, w_ref[sl, s0:s0 + (c1 - c0)])
            acc = part if acc is None else acc + part
            blocks.append(hk)
        return jnp.concatenate(blocks, axis=1), acc

    @pl.when(is_prompt)
    def _():
        h, acc0 = normed_input()
        for n, (c0, c1, s0, _, kv_out) in enumerate(chunks):
            acc = acc0 if n == 0 else _dot(h, w_ref[:, s0:s0 + (c1 - c0)])
            qkv_ref[:, c0:c1] = acc.astype(bf16)
            for ridx, a0, a1, o0, per_head in kv_out:
                if per_head:
                    heads = kv_refs[ridx].shape[0] // TD
                    for j in range((a1 - a0) // LANES):
                        kv_refs[ridx][pl.ds(o0 // LANES + j, TD, stride=heads), :] = (
                            acc[:, a0 + j * LANES:a0 + (j + 1) * LANES])
                else:
                    t = acc[:, a0:a1].T
                    for b in range(TD // PROMPT_SEQ):
                        kv_refs[ridx][b, o0:o0 + (a1 - a0), :] = t[:, b * PROMPT_SEQ:(b + 1) * PROMPT_SEQ]

    @pl.when(jnp.logical_not(is_prompt))
    def _():
        cos, sa, sb = cos_ref[...], sa_ref[...], sb_ref[...]
        h, acc0 = normed_input()
        for n, (c0, c1, s0, rope_blocks, _) in enumerate(chunks):
            acc = acc0 if n == 0 else _dot(h, w_ref[:, s0:s0 + (c1 - c0)])
            for b in range((c1 - c0) // LANES):
                blk = acc[:, b * LANES:(b + 1) * LANES]
                if b in rope_blocks:
                    blk = _rope_block(blk, cos, sa, sb)
                qkv_ref[:, c0 + b * LANES:c0 + (b + 1) * LANES] = blk.astype(bf16)


def _inproj(x_parts, mod_l, g, w, rope_tabs, chunks, kv_outs):
    n = w.shape[1]
    cos, sa, sb = rope_tabs
    bpt = TD // PROMPT_SEQ

    def rope_idx(i):
        return (jnp.where(i < N_PROMPT // TD, 0, (i - N_PROMPT // TD) % (SAMPLE_SEQ // TD)), 0)

    kv_specs, kv_shapes = [], []
    for kind, size in kv_outs:
        if kind == "T":
            kv_specs.append(pl.BlockSpec((bpt, size, PROMPT_SEQ), lambda i: _prompt_block(i, TD) + (0,)))
            kv_shapes.append(jax.ShapeDtypeStruct((N_PROMPT_BATCH, size, PROMPT_SEQ), f32))
        else:
            kv_specs.append(pl.BlockSpec((TD * size, LANES), lambda i: _prompt_block(i, TD)))
            kv_shapes.append(jax.ShapeDtypeStruct((N_PROMPT * size, LANES), f32))

    return pl.pallas_call(
        functools.partial(_inproj_kernel, chunks=chunks, n_x=len(x_parts)),
        grid=(N_TOK // TD,),
        in_specs=_x_specs(x_parts, TD) + [
            pl.BlockSpec((None, 6, D_MODEL), lambda i: (_mod_row(i, TD), 0, 0)),
            pl.BlockSpec((1, D_MODEL), lambda i: (0, 0)),
            pl.BlockSpec((D_MODEL, n), lambda i: (0, 0)),
            pl.BlockSpec((TD, LANES), rope_idx),
            pl.BlockSpec((TD, LANES), rope_idx),
            pl.BlockSpec((TD, LANES), rope_idx),
        ],
        out_specs=[pl.BlockSpec((TD, n), lambda i: (i, 0))] + kv_specs,
        out_shape=[jax.ShapeDtypeStruct((N_TOK, n), bf16)] + kv_shapes,
        compiler_params=_cparams(("arbitrary",)),
        name="inproj",
    )(*x_parts, mod_l, g, w, cos, sa, sb)


def _rope_tables():
    nq = HEAD_DIM // 4
    t = np.arange(SAMPLE_SEQ)
    inv = np.power(np.float32(ROPE_THETA), -np.arange(nq, dtype=np.float32) / np.float32(nq))
    ang_r = (t // GRID_W).astype(np.float32)[:, None] * inv
    ang_c = (t % GRID_W).astype(np.float32)[:, None] * inv
    zero = np.zeros_like(ang_r)

    def head(fr, fc):
        return np.concatenate([fr[0], fr[1], fc[0], fc[1]], axis=-1)

    cos = head((np.cos(ang_r), np.cos(ang_r)), (np.cos(ang_c), np.cos(ang_c)))
    sin_a = head((-np.sin(ang_r), zero), (-np.sin(ang_c), zero))
    sin_b = head((zero, np.sin(ang_r)), (zero, np.sin(ang_c)))
    two = lambda a: jnp.asarray(np.concatenate([a, a], axis=-1).astype(np.float32))
    return two(cos), two(sin_a), two(sin_b)


def _lane_lo(shape):
    return lax.broadcasted_iota(i32, shape, len(shape) - 1) < HEAD_DIM


def _half(q, lo_mask, half):
    keep = lo_mask if half == 0 else jnp.logical_not(lo_mask)
    return jnp.where(keep, q, jnp.zeros_like(q)) * Q_SCALE


def _swap_halves(x):
    return pltpu.roll(x.astype(f32), HEAD_DIM, 1).astype(x.dtype)


def _stack_halves(q, lo_mask):
    return jnp.concatenate([_half(q, lo_mask, 0), _half(q, lo_mask, 1)], axis=0)


def _with_ones(v):
    return jnp.concatenate([v, jnp.ones_like(v)], axis=1)


def _attend_many(problems):
    scores = [[_dot_nt(q_rows, k) for k in key_blocks] for q_rows, _, key_blocks, _, _, _ in problems]
    exps, maxes = [], []
    for (q_rows, n_heads, key_blocks, _, fix_scores, sinks), sc in zip(problems, scores):
        r = q_rows.shape[0] // n_heads
        e_p, m_p = [[] for _ in key_blocks], []
        for h in range(n_heads):
            blocks = [s[h * r:(h + 1) * r] for s in sc]
            if fix_scores is not None:
                blocks = [fix_scores(h, i, s) for i, s in enumerate(blocks)]
            m = functools.reduce(jnp.maximum, [jnp.max(s, axis=-1, keepdims=True) for s in blocks])
            if sinks is not None:
                m = jnp.maximum(m, sinks[h])
            m_p.append(m)
            for i, s in enumerate(blocks):
                e_p[i].append(jnp.exp((s - m).astype(bf16)))
        exps.append(e_p)
        maxes.append(m_p)
    outs = [functools.reduce(lambda a, b: a + b,
                             [_dot(e[0] if n_heads == 1 else jnp.concatenate(e, axis=0), vx)
                              for e, vx in zip(e_p, vx_blocks)])
            for (_, n_heads, _, vx_blocks, _, _), e_p in zip(problems, exps)]
    results = []
    for (q_rows, n_heads, _, _, _, sinks), out, m_p in zip(problems, outs, maxes):
        r = q_rows.shape[0] // n_heads
        res = []
        for h in range(n_heads):
            den = out[h * r:(h + 1) * r, LANES:]
            if sinks is not None:
                den = den + jnp.exp(sinks[h] - m_p[h])
            res.append(out[h * r:(h + 1) * r, :LANES] * (1.0 / den))
        results.append(res)
    return results


def _gqa_rows(q_blocks, group, lo_mask):
    parts = []
    for q in q_blocks:
        for half in range(2):
            qh = _half(q, lo_mask, half)
            parts.append(qh if half == group else _swap_halves(qh))
    return jnp.concatenate(parts, axis=0)


def _gqa_merge(outs, group, lo_mask):
    fixed = [o if idx % 2 == group else pltpu.roll(o, HEAD_DIM, 1) for idx, o in enumerate(outs)]
    return [jnp.where(lo_mask, fixed[2 * p], fixed[2 * p + 1]) for p in range(len(outs) // 2)]


L0_QA, L0_QB, L0_KB, L0_VB, L0_KA, L0_VA, L0_N = 0, 512, 1024, 1536, 2048, 2176, 2304


def _ctx0_kernel(sink_ref, qkv_ref, o_ref):
    lo = _lane_lo((1, LANES))
    blk = lambda base, j: qkv_ref[:, base + j * LANES:base + (j + 1) * LANES]
    k_a = blk(L0_KA, 0)
    vx_a = _with_ones(blk(L0_VA, 0))
    problems = []
    for g in range(2):
        q_rows = _gqa_rows([blk(L0_QA, 2 * g), blk(L0_QA, 2 * g + 1)], g, lo)
        problems.append((q_rows, 4, [k_a], [vx_a], None, [sink_ref[4 * g + idx] for idx in range(4)]))
    for j in range(4):
        problems.append((_stack_halves(blk(L0_QB, j), lo), 2, [blk(L0_KB, j)], [_with_ones(blk(L0_VB, j))],
                         None, None))
    results = _attend_many(problems)
    for g in range(2):
        for p, o in enumerate(_gqa_merge(results[g], g, lo)):
            j = 2 * g + p
            o_ref[:, j * LANES:(j + 1) * LANES] = o.astype(bf16)
    for j in range(4):
        outs = results[2 + j]
        o_ref[:, 512 + j * LANES:512 + (j + 1) * LANES] = jnp.where(lo, outs[0], outs[1]).astype(bf16)


def _ctx0(sink, qkv):
    return pl.pallas_call(
        _ctx0_kernel,
        grid=(N_PROMPT_BATCH,),
        in_specs=[
            pl.BlockSpec(memory_space=pltpu.SMEM),
            pl.BlockSpec((PROMPT_SEQ, L0_N), lambda b: (b, 0)),
        ],
        out_specs=pl.BlockSpec((PROMPT_SEQ, D_MODEL), lambda b: (b, 0)),
        out_shape=jax.ShapeDtypeStruct((N_PROMPT, D_MODEL), bf16),
        compiler_params=_cparams(("parallel",)),
        name="ctx0",
    )(sink, qkv)


WIN_KEYS = 3 * QB
NA_KEY_ROWS = 10
NA_KEYS = NA_KEY_ROWS * GRID_W
N_QB = SAMPLE_SEQ // QB
N_NA_PATTERNS = 5
_PROMPT_QBLOCKS = N_PROMPT // QB


def _na_pattern(n):
    return jnp.where(n < 2, n, jnp.where(n > N_QB - 3, n - (N_QB - 5), 2))


def _lat0_kernel(sink_ref, roff_ref, q_ref, kvb_ref, kva_ref, cak_ref, cav_ref, cbk_ref, cbv_ref, tiles_ref, o_ref):
    n = pl.program_id(1)
    lo = _lane_lo((1, LANES))
    kstart = pl.multiple_of(jnp.clip((n - 1) * QB, 0, SAMPLE_SEQ - WIN_KEYS), QB)
    k_a = kva_ref[pl.ds(kstart, WIN_KEYS), 0:LANES]
    v_a = kva_ref[pl.ds(kstart, WIN_KEYS), LANES:2 * LANES]
    c_k = cak_ref[...].astype(bf16)
    keys_a = [c_k, k_a]
    vx_a = [_with_ones(cav_ref[...].astype(bf16)), _with_ones(v_a)]
    qpos = n * QB + lax.broadcasted_iota(i32, (QB, WIN_KEYS), 0)
    kpos = kstart + lax.broadcasted_iota(i32, (QB, WIN_KEYS), 1)
    in_window = jnp.abs(qpos - kpos) <= A_WINDOW
    mask_window = lambda h, i, s: jnp.where(in_window, s, NEG) if i == 1 else s
    problems = []
    for g in range(2):
        q_rows = _gqa_rows([q_ref[:, L0_QA + j * LANES:L0_QA + (j + 1) * LANES] for j in (2 * g, 2 * g + 1)], g, lo)
        problems.append((q_rows, 4, keys_a, vx_a, mask_window, [sink_ref[4 * g + idx] for idx in range(4)]))
    krow = jnp.clip(2 * n - NA_ROWS // 2, 0, SAMPLE_SEQ // GRID_W - NA_KEY_ROWS)
    ktok = pl.multiple_of(krow * GRID_W, QB)
    pattern = _na_pattern(n)

    def na_bias(head):
        rows = []
        for rq in range(QB // GRID_W):
            blocks = []
            for kb in range(NA_KEY_ROWS // 2):
                d0, d1 = (roff_ref[(pattern * 2 + rq) * NA_KEY_ROWS + 2 * kb + t] for t in range(2))
                blocks.append(jnp.where(lo, tiles_ref[head, d0], tiles_ref[head, d1]))
            rows.append(jnp.concatenate(blocks, axis=1))
        return jnp.concatenate(rows, axis=0)

    for j in range(4):
        q_b = q_ref[:, L0_QB + j * LANES:L0_QB + (j + 1) * LANES]
        k_b = kvb_ref[pl.ds(ktok, NA_KEYS), j * LANES:(j + 1) * LANES]
        v_b = kvb_ref[pl.ds(ktok, NA_KEYS), 512 + j * LANES:512 + (j + 1) * LANES]
        cb_k = cbk_ref[:, j * LANES:(j + 1) * LANES].astype(bf16)
        cb_v = cbv_ref[:, j * LANES:(j + 1) * LANES].astype(bf16)
        add_bias = lambda h, i, s, j=j: s + na_bias(2 * j + h) if i == 1 else s
        problems.append((_stack_halves(q_b, lo), 2, [cb_k, k_b], [_with_ones(cb_v), _with_ones(v_b)], add_bias, None))

    results = _attend_many(problems)
    for g in range(2):
        for p, o in enumerate(_gqa_merge(results[g], g, lo)):
            j = 2 * g + p
            o_ref[:, j * LANES:(j + 1) * LANES] = o.astype(bf16)
    for j in range(4):
        outs = results[2 + j]
        o_ref[:, 512 + j * LANES:512 + (j + 1) * LANES] = jnp.where(lo, outs[0], outs[1]).astype(bf16)


def _lat0(sink, qkv, cak, cav, cbk, cbv, bias_tiles):
    sb = N_PROMPT // SAMPLE_SEQ
    return pl.pallas_call(
        _lat0_kernel,
        grid=(N_SAMPLE_BATCH, N_QB),
        in_specs=[
            pl.BlockSpec(memory_space=pltpu.SMEM),
            pl.BlockSpec(memory_space=pltpu.SMEM),
            pl.BlockSpec((QB, 1024), lambda b, n: (_PROMPT_QBLOCKS + b * N_QB + n, 0)),
            pl.BlockSpec((SAMPLE_SEQ, 1024), lambda b, n: (sb + b, 1)),
            pl.BlockSpec((SAMPLE_SEQ, 256), lambda b, n: (sb + b, L0_KA // 256)),
            pl.BlockSpec((None, PAST_LEN, LANES), lambda b, n: (b, 0, 0)),
            pl.BlockSpec((None, PAST_LEN, LANES), lambda b, n: (b, 0, 0)),
            pl.BlockSpec((None, PAST_LEN, 512), lambda b, n: (b, 0, 0)),
            pl.BlockSpec((None, PAST_LEN, 512), lambda b, n: (b, 0, 0)),
            pl.BlockSpec((8, N_ROW_OFFSETS + 1, GRID_W, LANES), lambda b, n: (0, 0, 0, 0)),
        ],
        out_specs=pl.BlockSpec((QB, D_MODEL), lambda b, n: (b * N_QB + n, 0)),
        out_shape=jax.ShapeDtypeStruct((N_SAMPLE, D_MODEL), bf16),
        compiler_params=_cparams(("parallel", "arbitrary")),
        name="lat0",
    )(sink, jnp.asarray(_na_row_offsets()), qkv, qkv, qkv, cak, cav, cbk, cbv, bias_tiles)


N_ROW_OFFSETS = 2 * NA_ROWS - 1


def _na_row_offsets():
    rows = SAMPLE_SEQ // GRID_W
    idx = np.full((N_NA_PATTERNS, 2, NA_KEY_ROWS), N_ROW_OFFSETS, np.int32)
    for p, n in enumerate((0, 1, 2, N_QB - 2, N_QB - 1)):
        k0 = int(np.clip(2 * n - NA_ROWS // 2, 0, rows - NA_KEY_ROWS))
        for rq in range(2):
            r = 2 * n + rq
            rs = int(np.clip(r - NA_ROWS // 2, 0, rows - NA_ROWS))
            for kl in range(NA_KEY_ROWS):
                if rs <= k0 + kl < rs + NA_ROWS:
                    idx[p, rq, kl] = k0 + kl - r + NA_ROWS - 1
    return idx.reshape(-1)


def _na_bias_tiles(rel_bias):
    n_dc = 2 * NA_COLS - 1
    c = np.arange(GRID_W)[:, None]
    kc = np.arange(GRID_W)[None, :]
    cs = np.clip(c - NA_COLS // 2, 0, GRID_W - NA_COLS)
    col_ok = (kc >= cs) & (kc < cs + NA_COLS)
    col_hot = ((kc - c + NA_COLS - 1)[None] == np.arange(n_dc)[:, None, None]) & col_ok[None]
    hp = lax.Precision.HIGHEST
    tiles = jnp.einsum("hdx,xck->hdck", rel_bias.astype(f32), col_hot.astype(np.float32), precision=hp)
    tiles = tiles + np.where(col_ok, 0.0, NEG).astype(np.float32)
    tiles = jnp.concatenate([tiles, jnp.full((tiles.shape[0], 1, GRID_W, GRID_W), NEG, f32)], axis=1)
    return jnp.concatenate([tiles, tiles], axis=-1)


def _diff_lambda(lam_ref, lam_init):
    lv = lam_ref[...]
    s1 = jnp.sum(lv[0:1, :] * lv[1:2, :], axis=-1, keepdims=True)
    s2 = jnp.sum(lv[2:3, :] * lv[3:4, :], axis=-1, keepdims=True)
    return jnp.exp(s1) - jnp.exp(s2) + lam_init


def _diff_heads(heads, o_ref, lam, subln, lo, lam_init):
    for p0 in range(0, len(heads), DIFF_HEADS_PER_PASS):
        group = heads[p0:p0 + DIFF_HEADS_PER_PASS]
        results = _attend_many([(_stack_halves(q, lo), 2, ks, [_with_ones(v) for v in vs], None, None)
                                for _, q, ks, vs in group])
        for (h, _, _, _), (o1, o2) in zip(group, results):
            o = o1 - lam * o2
            o = o * lax.rsqrt(jnp.mean(o * o, axis=-1, keepdims=True) + EPS)
            o_ref[:, h * LANES:(h + 1) * LANES] = ((o * subln) * (1.0 - lam_init)).astype(bf16)


def _ctx1_kernel(lam_ref, subln_ref, qkv_ref, o_ref, *, lam_init):
    lo = _lane_lo((1, LANES))
    lam = _diff_lambda(lam_ref, lam_init)
    blk = lambda base, h: qkv_ref[:, base + h * LANES:base + (h + 1) * LANES]
    heads = [(h, blk(0, h), [blk(D_MODEL, h)], [blk(2 * D_MODEL, h)]) for h in range(8)]
    _diff_heads(heads, o_ref, lam, subln_ref[...], lo, lam_init)


def _ctx1(lamv, subln, qkv, lam_init):
    return pl.pallas_call(
        functools.partial(_ctx1_kernel, lam_init=lam_init),
        grid=(N_PROMPT_BATCH,),
        in_specs=[
            pl.BlockSpec((8, HEAD_DIM), lambda b: (0, 0)),
            pl.BlockSpec((1, LANES), lambda b: (0, 0)),
            pl.BlockSpec((PROMPT_SEQ, 3 * D_MODEL), lambda b: (b, 0)),
        ],
        out_specs=pl.BlockSpec((PROMPT_SEQ, D_MODEL), lambda b: (b, 0)),
        out_shape=jax.ShapeDtypeStruct((N_PROMPT, D_MODEL), bf16),
        compiler_params=_cparams(("parallel",)),
        name="ctx1",
    )(lamv, subln, qkv)


def _lat1_kernel(lam_ref, subln_ref, q_ref, k_ref, v_ref, ck_ref, cv_ref, o_ref, *, lam_init):
    lo = _lane_lo((1, LANES))
    lam = _diff_lambda(lam_ref, lam_init)
    heads = []
    for h in range(8):
        sl = slice(h * LANES, (h + 1) * LANES)
        heads.append((h, q_ref[:, sl], [ck_ref[:, sl].astype(bf16), k_ref[:, sl]],
                      [cv_ref[:, sl].astype(bf16), v_ref[:, sl]]))
    _diff_heads(heads, o_ref, lam, subln_ref[...], lo, lam_init)


def _lat1(lamv, subln, qkv, ck, cv, lam_init):
    sb = N_PROMPT // SAMPLE_SEQ
    nq = SAMPLE_SEQ // TM
    return pl.pallas_call(
        functools.partial(_lat1_kernel, lam_init=lam_init),
        grid=(N_SAMPLE_BATCH, nq),
        in_specs=[
            pl.BlockSpec((8, HEAD_DIM), lambda b, n: (0, 0)),
            pl.BlockSpec((1, LANES), lambda b, n: (0, 0)),
            pl.BlockSpec((TM, D_MODEL), lambda b, n: (N_PROMPT_TILES + b * nq + n, 0)),
            pl.BlockSpec((SAMPLE_SEQ, D_MODEL), lambda b, n: (sb + b, 1)),
            pl.BlockSpec((SAMPLE_SEQ, D_MODEL), lambda b, n: (sb + b, 2)),
            pl.BlockSpec((None, PAST_LEN, D_MODEL), lambda b, n: (b, 0, 0)),
            pl.BlockSpec((None, PAST_LEN, D_MODEL), lambda b, n: (b, 0, 0)),
        ],
        out_specs=pl.BlockSpec((TM, D_MODEL), lambda b, n: (b * nq + n, 0)),
        out_shape=jax.ShapeDtypeStruct((N_SAMPLE, D_MODEL), bf16),
        compiler_params=_cparams(("parallel", "arbitrary")),
        name="lat1",
    )(lamv, subln, qkv, qkv, qkv, ck, cv)


def _split_bf16(a):
    hi = a.astype(bf16)
    return hi, (a - hi.astype(f32)).astype(bf16)


def _route_kernel(*refs, n_x):
    x_refs = refs[:n_x]
    (op_ref, os_ref, mod_ref, g_ref, wo_ref, rwt_ref, rb_ref,
     xnew_ref, xloc_ref, slots_ref, gate_ref, len_ref) = refs[n_x:]
    logits = _route_logits(x_refs, op_ref, os_ref, mod_ref, g_ref, wo_ref, rwt_ref, xnew_ref)
    tiles = _route_tiles(jnp.concatenate([lg for _, lg in logits], axis=1), rb_ref, slots_ref, gate_ref, len_ref)
    for t, (slots, run_len) in enumerate(tiles):
        _route_dispatch(t, slots, logits[t][0], run_len, xloc_ref)


def _route_logits(x_refs, op_ref, os_ref, mod_ref, g_ref, wo_ref, rwt_ref, xnew_ref):
    is_prompt = pl.program_id(0) < N_PROMPT_TILES // ROUTE_TILES
    tiles = [slice(t * TM, (t + 1) * TM) for t in range(ROUTE_TILES)]
    proj = [_dot(jnp.where(is_prompt, op_ref[rows, :], os_ref[rows, :]), wo_ref[...]) for rows in tiles]
    w_hi, w_lo = _split_bf16(rwt_ref[...])
    splits = []
    for rows, y in zip(tiles, proj):
        x_in = x_refs[0][rows, :] if len(x_refs) == 1 else jnp.where(is_prompt, x_refs[0][rows, :], x_refs[1][rows, :])
        x = x_in + mod_ref[2:3, :] * y
        xnew_ref[rows, :] = x
        splits.append(_split_bf16(_norm_mod(x, g_ref[...], mod_ref[4:5, :], mod_ref[3:4, :])))
    return [(h_hi, _dot_nt(w_hi, h_hi) + (_dot_nt(w_hi, h_lo) + _dot_nt(w_lo, h_hi)))
            for h_hi, h_lo in splits]


def _route_tiles(logits, rb_ref, slots_ref, gate_ref, len_ref):
    ng, ge = N_GROUPS, N_EXPERTS // N_GROUPS
    n = ROUTE_TILES * TM
    tile = lambda a, t: a[..., t * TM:(t + 1) * TM]
    scores = jax.nn.sigmoid(logits)
    biased = scores + rb_ref[...]
    s3 = scores.reshape(ng, ge, n)
    b3 = biased.reshape(ng, ge, n)
    in_group = lax.broadcasted_iota(i32, (ng, ge, n), 1).astype(f32)
    group_id = lax.broadcasted_iota(i32, (ng, 1, n), 0).astype(f32)
    expert_id = lax.broadcasted_iota(i32, (ng, ge, n), 0).astype(f32) * ge + in_group

    def max01(a):
        return jnp.max(jnp.max(a, axis=0, keepdims=True), axis=1, keepdims=True)

    def min01(a):
        return jnp.min(jnp.min(a, axis=0, keepdims=True), axis=1, keepdims=True)

    def sum01(a):
        return jnp.sum(jnp.sum(a, axis=0, keepdims=True), axis=1, keepdims=True)

    m1 = jnp.max(b3, axis=1, keepdims=True)
    first = jnp.min(jnp.where(b3 == m1, in_group, ge), axis=1, keepdims=True)
    m2 = jnp.max(jnp.where(in_group == first, -jnp.inf, b3), axis=1, keepdims=True)
    gscore = m1 + m2
    gsel = jnp.zeros((ng, 1, n), f32)
    for _ in range(TOPK_GROUPS):
        gm = jnp.max(gscore, axis=0, keepdims=True)
        gi = jnp.min(jnp.where(gscore == gm, group_id, ng), axis=0, keepdims=True)
        hit = group_id == gi
        gsel = jnp.where(hit, 1.0, gsel)
        gscore = jnp.where(hit, -jnp.inf, gscore)
    cand = jnp.where(jnp.broadcast_to(gsel, (ng, ge, n)) > 0.0, b3, -jnp.inf)
    top_e, top_w = [], []
    for _ in range(TOP_K):
        em = max01(cand)
        ei = min01(jnp.where(cand == em, expert_id, N_EXPERTS))
        hit = expert_id == ei
        top_e.append(ei)
        top_w.append(sum01(jnp.where(hit, s3, 0.0)))
        cand = jnp.where(hit, -jnp.inf, cand)
    wsum = functools.reduce(lambda a, b: a + b, top_w)
    sel3 = jnp.zeros((ng, ge, n), f32)
    for k, (ei, w) in enumerate(zip(top_e, top_w)):
        gate = (w / wsum * ROUTED_SCALE).reshape(1, n)
        for t in range(ROUTE_TILES):
            gate_ref[t, k:k + 1, :] = tile(gate, t)
        sel3 = jnp.where(expert_id == ei, 1.0, sel3)
    sel = sel3.reshape(N_EXPERTS, n)

    r_i = lax.broadcasted_iota(i32, (N_EXPERTS, N_EXPERTS), 0)
    c_i = lax.broadcasted_iota(i32, (N_EXPERTS, N_EXPERTS), 1)
    lower = jnp.where(c_i < r_i, 1.0, 0.0).astype(bf16)
    run_lens, run_offs = [], []
    for t in range(ROUTE_TILES):
        cnt = jnp.sum(tile(sel, t), axis=1, keepdims=True)
        run_len = jnp.ceil(cnt * (1.0 / CHUNK)) * CHUNK
        run_off = _dot(lower, jnp.broadcast_to(run_len, (N_EXPERTS, LANES)).astype(bf16))[:, 0:1]
        run_lens.append(run_len)
        run_offs.append(jnp.broadcast_to(run_off, (N_EXPERTS, TM)))
    t_r = lax.broadcasted_iota(i32, (n, n), 0)
    t_c = lax.broadcasted_iota(i32, (n, n), 1)
    before = jnp.where(jnp.logical_and(t_r < t_c, t_r // TM == t_c // TM), 1.0, 0.0).astype(bf16)
    rank = _dot(sel.astype(bf16), before)
    slot3 = (jnp.concatenate(run_offs, axis=1) + rank).reshape(ng, ge, n)
    slots = [sum01(jnp.where(expert_id == ei, slot3, 0.0)).reshape(1, n).astype(i32) for ei in top_e]
    out = []
    for t in range(ROUTE_TILES):
        for k in range(TOP_K):
            slots_ref[t, k:k + 1, :] = tile(slots[k], t)
        slots_ref[t, TOP_K:8, :] = jnp.full((8 - TOP_K, TM), -1, i32)
        gate_ref[t, TOP_K:8, :] = jnp.zeros((8 - TOP_K, TM), f32)
        cnt_row = _dot_nt(jnp.ones((8, TM), bf16), tile(sel, t).astype(bf16))
        len_ref[t] = (jnp.ceil(cnt_row * (1.0 / CHUNK)) * CHUNK).astype(i32)
        out.append(([tile(sl, t) for sl in slots], run_lens[t]))
    return out


def _route_dispatch(t, slots, h_hi, run_len, xloc_ref):
    rows = ROUTE_ROWS

    def body(c, carry):
        base = pl.multiple_of(c * rows, rows)
        row_id = base.astype(jnp.int16) + lax.broadcasted_iota(jnp.int16, (rows, TM), 0)
        p = jnp.zeros((rows, TM), bf16)
        for k in range(TOP_K):
            p = jnp.where(row_id == slots[k].astype(jnp.int16), jnp.ones((), bf16), p)
        xloc_ref[pl.ds(t * SLOTS + base, rows), :] = _dot(p, h_hi).astype(bf16)
        return carry

    def zero_body(c, carry):
        base = pl.multiple_of(c * rows, rows)
        xloc_ref[pl.ds(t * SLOTS + base, rows), :] = jnp.zeros((rows, D_MODEL), bf16)
        return carry

    n_used = (jnp.sum(run_len).astype(i32) + (rows - 1)) // rows
    lax.fori_loop(0, n_used, body, 0)
    lax.fori_loop(n_used, SLOTS // rows, zero_body, 0)


def _route(x_parts, o_prompt, o_sample, mod_l, g, w_out, rwt, rb):
    per_tile = lambda i: (i, 0, 0)
    rt = ROUTE_TILES
    tm = rt * TM
    return pl.pallas_call(
        functools.partial(_route_kernel, n_x=len(x_parts)),
        grid=(N_TILES // rt,),
        in_specs=_x_specs(x_parts, tm) + [
            pl.BlockSpec((tm, D_MODEL), lambda i: _prompt_block(i, tm)),
            pl.BlockSpec((tm, D_MODEL), lambda i: _sample_block(i, tm)),
            pl.BlockSpec((None, 6, D_MODEL), lambda i: (_mod_row(i, tm), 0, 0)),
            pl.BlockSpec((1, D_MODEL), lambda i: (0, 0)),
            pl.BlockSpec((D_MODEL, D_MODEL), lambda i: (0, 0)),
            pl.BlockSpec((N_EXPERTS, D_MODEL), lambda i: (0, 0)),
            pl.BlockSpec((N_EXPERTS, 1), lambda i: (0, 0)),
        ],
        out_specs=[
            pl.BlockSpec((tm, D_MODEL), lambda i: (i, 0)),
            pl.BlockSpec((rt * SLOTS, D_MODEL), lambda i: (i, 0)),
            pl.BlockSpec((rt, 8, TM), per_tile),
            pl.BlockSpec((rt, 8, TM), per_tile),
            pl.BlockSpec((rt, 8, N_EXPERTS), per_tile),
        ],
        out_shape=[
            jax.ShapeDtypeStruct((N_TOK, D_MODEL), f32),
            jax.ShapeDtypeStruct((N_TILES * SLOTS, D_MODEL), bf16),
            jax.ShapeDtypeStruct((N_TILES, 8, TM), i32),
            jax.ShapeDtypeStruct((N_TILES, 8, TM), f32),
            jax.ShapeDtypeStruct((N_TILES, 8, N_EXPERTS), i32),
        ],
        compiler_params=_cparams(("parallel",)),
        name="route",
    )(*x_parts, o_prompt, o_sample, mod_l, g, w_out, rwt, rb)


def _moe_plan(run_len):
    nt, ne = run_len.shape

    def excl_cumsum(a):
        n = a.shape[-1]
        earlier = np.arange(n)[None, :] < np.arange(n)[:, None]
        return jnp.sum(jnp.where(earlier, a[..., None, :], 0), axis=-1)

    def first_diff(a):
        return a - jnp.concatenate([jnp.zeros_like(a[..., :1]), a[..., :-1]], axis=-1)

    off_loc = excl_cumsum(run_len)
    before = excl_cumsum(run_len.T).T
    n_e = jnp.sum(run_len, axis=0)
    n_pad = -(-n_e // GM) * GM
    g_start = excl_cumsum(n_pad)
    total = jnp.sum(n_pad)
    run_dst = g_start[None, :] + before
    run_src = jnp.arange(nt, dtype=i32)[:, None] * SLOTS + off_loc
    dst_f = run_dst.T.reshape(-1)
    shift_f = first_diff((run_src - run_dst).T.reshape(-1))
    rows = jnp.arange((G_TILES + GMM_ITEM_TILES - 1) * G_CHUNKS, dtype=i32) * CHUNK
    shift = jnp.sum(jnp.where(dst_f[None, :] <= rows[:, None], shift_f[None, :], 0), axis=1)
    in_run = jnp.any((g_start[None, :] <= rows[:, None]) & (rows[:, None] < (g_start + n_e)[None, :]), axis=1)
    chunk_src = (jnp.where(in_run, rows + shift, 0) // CHUNK).astype(i32)
    loc_rows = jnp.arange(SLOT_CHUNKS, dtype=i32) * CHUNK
    shift_l = first_diff(run_dst - off_loc)
    shift = jnp.sum(jnp.where(off_loc[:, None, :] <= loc_rows[None, :, None], shift_l[:, None, :], 0), axis=2)
    used = jnp.sum(run_len, axis=1)
    chunk_map = jnp.where(loc_rows[None, :] < used[:, None], (loc_rows[None, :] + shift) // CHUNK, 0).astype(i32)
    tile_start, n_tiles = g_start // GM, n_pad // GM
    n_items = -(-n_tiles // GMM_ITEM_TILES)
    item_start = excl_cumsum(n_items)
    items = jnp.arange(GMM_MAX_ITEMS, dtype=i32)
    owner = items[:, None] >= item_start[None, :]
    e_first = jnp.sum(jnp.where(owner, first_diff(tile_start - GMM_ITEM_TILES * item_start)[None, :], 0), axis=1)
    item_tile = e_first + GMM_ITEM_TILES * items
    e_end = jnp.sum(jnp.where(owner, first_diff(tile_start + n_tiles)[None, :], 0), axis=1)
    item_cnt = jnp.clip(e_end - item_tile, 0, GMM_ITEM_TILES)
    gmm_plan = tuple(a.astype(i32) for a in (item_start, n_items, item_tile, item_cnt, chunk_src))
    long_tiles = (used > COMBINE_SURE_CHUNKS * CHUNK).astype(i32)
    return gmm_plan, chunk_map.reshape(-1), long_tiles


def _gmm_in_copy(xloc_hbm, xbuf, sem, src_chunk, slot, c):
    return pltpu.make_async_copy(xloc_hbm.at[src_chunk], xbuf.at[slot, c], sem.at[slot])


def _gmm_out_copy(ybuf, y_hbm, sem, tile, slot, n_tiles):
    chunks = n_tiles * G_CHUNKS
    return pltpu.make_async_copy(ybuf.at[slot, pl.ds(0, chunks)],
                                 y_hbm.at[pl.ds(tile * G_CHUNKS, chunks)], sem.at[slot])


def _gmm_kernel(i0_ref, ni_ref, it_ref, ic_ref, cs_ref, xloc_hbm, wg_ref, wu_ref, wd_ref, y_hbm,
                xbuf, ybuf, zbuf, wg_b, wu_b, wd_b, in_sem, out_sem, zsem):
    e = pl.program_id(0)
    last = pl.num_programs(0) - 1
    n_items = ni_ref[e]
    first_item = i0_ref[e]
    total_items = i0_ref[last] + ni_ref[last]
    last_item = total_items - 1
    total_tiles = it_ref[last_item] + ic_ref[last_item]

    def start_in(item):
        first = it_ref[item] * G_CHUNKS
        for c in range(GMM_ITEM_TILES * G_CHUNKS):
            _gmm_in_copy(xloc_hbm, xbuf, in_sem, cs_ref[first + c], item % GMM_SLOTS, c).start()

    def wait_in(item):
        for c in range(GMM_ITEM_TILES * G_CHUNKS):
            _gmm_in_copy(xloc_hbm, xbuf, in_sem, 0, item % GMM_SLOTS, c).wait()

    def out_copy(item, fn):
        for cnt in range(1, GMM_ITEM_TILES + 1):
            @pl.when(ic_ref[item] == cnt)
            def _():
                fn(_gmm_out_copy(ybuf, y_hbm, out_sem, it_ref[item], item % GMM_SLOTS, cnt))

    @pl.when(e == 0)
    def _():
        for item in range(GMM_SLOTS - 1):
            start_in(item)
        zbuf[...] = jnp.zeros(zbuf.shape, zbuf.dtype)

    def tail_copies(fn):
        for j in range(_GMM_TAIL_PER_STEP):
            tile = total_tiles + e + j * N_EXPERTS

            @pl.when(tile < G_TILES)
            def _():
                fn(pltpu.make_async_copy(zbuf, y_hbm.at[pl.ds(tile * G_CHUNKS, G_CHUNKS)], zsem.at[0]))

    tail_copies(lambda cp: cp.start())

    @pl.when(n_items > 0)
    def _():
        wg_b[...] = wg_ref[...].astype(bf16)
        wu_b[...] = wu_ref[...].astype(bf16)
        wd_b[...] = wd_ref[...].astype(bf16)

    def body(j, carry):
        item = first_item + j
        slot = item % GMM_SLOTS

        @pl.when(item + (GMM_SLOTS - 1) < total_items)
        def _():
            start_in(item + (GMM_SLOTS - 1))

        wait_in(item)

        @pl.when(item >= GMM_SLOTS)
        def _():
            out_copy(item - GMM_SLOTS, lambda cp: cp.wait())

        for cnt in range(1, GMM_ITEM_TILES + 1):
            @pl.when(ic_ref[item] == cnt)
            def _():
                rows = cnt * GM
                chunks = cnt * G_CHUNKS
                x = xbuf[slot, 0:chunks].reshape(rows, D_MODEL)
                act = _silu(_dot(x, wg_b[...])) * _dot(x, wu_b[...])
                y = _dot(act.astype(bf16), wd_b[...]).astype(bf16)
                ybuf[slot, 0:chunks] = y.reshape(chunks, CHUNK, D_MODEL)

        out_copy(item, lambda cp: cp.start())
        return carry

    lax.fori_loop(0, n_items, body, 0)
    tail_copies(lambda cp: cp.wait())

    @pl.when(e == last)
    def _():
        for back in range(1, GMM_SLOTS + 1):
            out_copy(total_items - back, lambda cp: cp.wait())


def _gmm(plan, xloc, wg, wu, wd, layer):
    rows = GMM_ITEM_TILES * GM
    w_idx = lambda e, *_: (layer, e, 0, 0)
    grid_spec = pltpu.PrefetchScalarGridSpec(
        num_scalar_prefetch=5,
        grid=(N_EXPERTS,),
        in_specs=[
            pl.BlockSpec(memory_space=pl.ANY),
            pl.BlockSpec((None, None, D_MODEL, EXPERT_DIM), w_idx),
            pl.BlockSpec((None, None, D_MODEL, EXPERT_DIM), w_idx),
            pl.BlockSpec((None, None, EXPERT_DIM, D_MODEL), w_idx),
        ],
        out_specs=pl.BlockSpec(memory_space=pl.ANY),
        scratch_shapes=[pltpu.VMEM((GMM_SLOTS, rows // CHUNK, CHUNK, D_MODEL), bf16),
                        pltpu.VMEM((GMM_SLOTS, rows // CHUNK, CHUNK, D_MODEL), bf16),
                        pltpu.VMEM((G_CHUNKS, CHUNK, D_MODEL), bf16),
                        pltpu.VMEM((D_MODEL, EXPERT_DIM), bf16), pltpu.VMEM((D_MODEL, EXPERT_DIM), bf16),
                        pltpu.VMEM((EXPERT_DIM, D_MODEL), bf16),
                        pltpu.SemaphoreType.DMA((GMM_SLOTS,)), pltpu.SemaphoreType.DMA((GMM_SLOTS,)),
                        pltpu.SemaphoreType.DMA((1,))],
    )
    return pl.pallas_call(
        _gmm_kernel,
        grid_spec=grid_spec,
        out_shape=jax.ShapeDtypeStruct((G_TILES * G_CHUNKS, CHUNK, D_MODEL), bf16),
        compiler_params=_cparams(("arbitrary",)),
        name="gmm",
    )(*plan, xloc, wg, wu, wd)


def _combine_copy(y_hbm, ybuf, sem, sorted_chunk, slot, c):
    return pltpu.make_async_copy(y_hbm.at[sorted_chunk], ybuf.at[slot, c], sem.at[slot])


def _combine_kernel(cm_ref, long_ref, y_hbm, slots_ref, gate_ref, x_ref, mod_ref, g_ref, sg_ref, su_ref, sd_ref,
                    *rest, final):
    if final:
        gf_ref, yp_ref, ys_ref, ybuf, sem = rest
    else:
        o_ref, ybuf, sem = rest
    i = pl.program_id(0)
    n = pl.num_programs(0)
    ahead = COMBINE_SLOTS - 1
    slot = i % COMBINE_SLOTS

    def for_chunks(tile, fn):
        for c in range(COMBINE_SURE_CHUNKS):
            fn(c)

        @pl.when(long_ref[tile] == 1)
        def _():
            for c in range(COMBINE_SURE_CHUNKS, SLOT_CHUNKS):
                fn(c)

    def start(tile, s):
        for_chunks(tile, lambda c: _combine_copy(y_hbm, ybuf, sem, cm_ref[tile * SLOT_CHUNKS + c], s, c).start())

    def wait(tile, s):
        for_chunks(tile, lambda c: _combine_copy(y_hbm, ybuf, sem, 0, s, c).wait())

    @pl.when(i == 0)
    def _():
        ybuf[:, COMBINE_SURE_CHUNKS:] = jnp.zeros((COMBINE_SLOTS, SLOT_CHUNKS - COMBINE_SURE_CHUNKS, CHUNK, D_MODEL), bf16)
        for tile in range(ahead):
            start(tile, tile)

    wait(i, slot)
    start((i + ahead) % n, (i + ahead) % COMBINE_SLOTS)

    x = x_ref[...]
    hb = _norm_mod(x, g_ref[...], mod_ref[4:5, :], mod_ref[3:4, :]).astype(bf16)
    shared = _dot((_silu(_dot(hb, sg_ref[...])) * _dot(hb, su_ref[...])).astype(bf16), sd_ref[...])
    row_id = lax.broadcasted_iota(jnp.int16, (SLOTS, TM), 0)
    p = jnp.zeros((SLOTS, TM), bf16)
    for k in range(TOP_K):
        p = jnp.where(row_id == slots_ref[k:k + 1, :].astype(jnp.int16), gate_ref[k:k + 1, :].astype(bf16), p)
    routed = lax.dot_general(p, ybuf[slot].reshape(SLOTS, D_MODEL), (((0,), (0,)), ((), ())),
                             preferred_element_type=f32)
    out = x + mod_ref[5:6, :] * (routed + shared)
    if final:
        y = (out * lax.rsqrt(jnp.mean(out * out, axis=-1, keepdims=True) + EPS)) * gf_ref[...]

        @pl.when(i < N_PROMPT_TILES)
        def _():
            yp_ref[...] = y

        @pl.when(i >= N_PROMPT_TILES)
        def _():
            ys_ref[...] = y
    else:
        o_ref[...] = out

    @pl.when(i == n - 1)
    def _():
        for k in range(1, ahead + 1):
            wait((i + k) % n, (i + k) % COMBINE_SLOTS)


def _combine(chunk_map, long_tiles, y, slots, gates, x, mod_l, g, sg, su, sd, final_g=None):
    shd = sg.shape[1]
    final = final_g is not None
    row_spec = pl.BlockSpec((TM, D_MODEL), lambda i, *_: (i, 0))
    vec_spec = pl.BlockSpec((1, D_MODEL), lambda i, *_: (0, 0))
    if final:
        out_specs = [pl.BlockSpec((TM, D_MODEL), lambda i, *_: _prompt_block(i)),
                     pl.BlockSpec((TM, D_MODEL), lambda i, *_: _sample_block(i))]
        out_shape = [jax.ShapeDtypeStruct((N_PROMPT, D_MODEL), f32), jax.ShapeDtypeStruct((N_SAMPLE, D_MODEL), f32)]
    else:
        out_specs, out_shape = row_spec, jax.ShapeDtypeStruct((N_TOK, D_MODEL), f32)
    grid_spec = pltpu.PrefetchScalarGridSpec(
        num_scalar_prefetch=2,
        grid=(N_TILES,),
        in_specs=[
            pl.BlockSpec(memory_space=pl.ANY),
            pl.BlockSpec((None, 8, TM), lambda i, *_: (i, 0, 0)),
            pl.BlockSpec((None, 8, TM), lambda i, *_: (i, 0, 0)),
            row_spec,
            pl.BlockSpec((None, 6, D_MODEL), lambda i, *_: (_mod_row(i), 0, 0)),
            vec_spec,
            pl.BlockSpec((D_MODEL, shd), lambda i, *_: (0, 0)),
            pl.BlockSpec((D_MODEL, shd), lambda i, *_: (0, 0)),
            pl.BlockSpec((shd, D_MODEL), lambda i, *_: (0, 0)),
        ] + ([vec_spec] if final else []),
        out_specs=out_specs,
        scratch_shapes=[pltpu.VMEM((COMBINE_SLOTS, SLOT_CHUNKS, CHUNK, D_MODEL), bf16),
                        pltpu.SemaphoreType.DMA((COMBINE_SLOTS,))],
    )
    args = (chunk_map, long_tiles, y, slots, gates, x, mod_l, g, sg, su, sd) + ((final_g,) if final else ())
    return pl.pallas_call(
        functools.partial(_combine_kernel, final=final),
        grid_spec=grid_spec,
        out_shape=out_shape,
        compiler_params=_cparams(("arbitrary",)),
        name="combine",
    )(*args)


def _moe(x_parts, o_prompt, o_sample, w_out, mod_l, g, rwt, rb, wg, wu, wd, layer, sg, su, sd, final_g=None):
    x, xloc, slots, gates, run_len = _route(x_parts, o_prompt, o_sample, mod_l, g, w_out, rwt, rb)
    gmm_plan, chunk_map, long_tiles = _moe_plan(run_len[:, 0, :])
    y = _gmm(gmm_plan, xloc.reshape(N_TILES * SLOT_CHUNKS, CHUNK, D_MODEL), wg, wu, wd, layer)
    return _combine(chunk_map, long_tiles, y, slots, gates, x, mod_l, g, sg, su, sd, final_g)


_L0_CHUNKS = (
    (0, 512, 0, (0, 1, 2, 3), ()),
    (512, 1024, 768, (), ()),
    (1024, 1536, 1280, (), ((2, 0, 512, 0, False),)),
    (1536, 2048, 1792, (), ((3, 0, 512, 0, False),)),
    (2048, 2304, 512, (0,), ((0, 0, 128, 0, False), (1, 128, 256, 0, False))),
)
_L0_KV_OUTS = (("T", 128), ("T", 128), ("T", 512), ("T", 512))
_L1_CHUNKS = (
    (0, 512, 0, (0, 1, 2, 3), ()),
    (512, 1024, 512, (0, 1, 2, 3), ()),
    (1024, 1536, 1024, (0, 1, 2, 3), ((0, 0, 512, 0, False),)),
    (1536, 2048, 1536, (0, 1, 2, 3), ((0, 0, 512, 512, False),)),
    (2048, 2560, 2048, (), ((1, 0, 512, 0, True),)),
    (2560, 3072, 2560, (), ((1, 0, 512, 512, True),)),
)
_L1_KV_OUTS = (("T", 1024), ("H", 8))


def _from_feature_major(kt, *head_dims):
    nb, _, s = kt.shape
    nd = len(head_dims)
    return kt.reshape(nb, *head_dims, s).transpose(0, nd + 1, *range(1, nd + 1))[:, None]


def kernel(x_prompt, x_sample, cache_a_k, cache_a_v, cache_b_k, cache_b_v, cache_c_k, cache_c_v, c, c_ctx, w_mod, b_mod, norm_mix, norm_ffn, w_in_ab, w_out_ab, sink_a, rel_bias_b, w_in_c, w_out_c, lam_q1, lam_k1, lam_q2, lam_k2, subln_c, router_w, router_bias, exp_w_gate, exp_w_up, exp_w_down, sh_w_gate, sh_w_up, sh_w_down, final_norm):
    x = (x_prompt.reshape(N_PROMPT, D_MODEL), x_sample.reshape(N_SAMPLE, D_MODEL))
    cond8 = jnp.concatenate([c_ctx[None, :], c, jnp.zeros((8 - 1 - N_SAMPLE_BATCH, D_MODEL), f32)], axis=0)
    mod = _adaln(cond8, w_mod, b_mod).reshape(DEPTH, 8, 6, D_MODEL)
    rope_tabs = _rope_tables()
    new_kv = {}
    for layer in range(DEPTH):
        li = layer // 2
        mod_l = mod[layer]
        g_mix = norm_mix[layer][None, :]
        g_ffn = norm_ffn[layer][None, :]
        if layer % 2 == 0:
            w_in = w_in_ab[li].astype(bf16)
            qkv, ak, av, bk, bv = _inproj(x, mod_l, g_mix, w_in, rope_tabs, _L0_CHUNKS, _L0_KV_OUTS)
            new_kv["a_k"], new_kv["a_v"], new_kv["b_k"], new_kv["b_v"] = ak, av, bk, bv
            o_p = _ctx0(sink_a[li], qkv)
            o_s = _lat0(sink_a[li], qkv,
                        cache_a_k[:, li].reshape(N_SAMPLE_BATCH, PAST_LEN, LANES),
                        cache_a_v[:, li].reshape(N_SAMPLE_BATCH, PAST_LEN, LANES),
                        cache_b_k[:, li].reshape(N_SAMPLE_BATCH, PAST_LEN, 512),
                        cache_b_v[:, li].reshape(N_SAMPLE_BATCH, PAST_LEN, 512),
                        _na_bias_tiles(rel_bias_b[li]))
            w_out = w_out_ab[li].astype(bf16)
        else:
            lam_init = 0.8 - 0.6 * math.exp(-0.3 * layer)
            qkv, ck, cv = _inproj(x, mod_l, g_mix, w_in_c[li].astype(bf16), rope_tabs, _L1_CHUNKS, _L1_KV_OUTS)
            new_kv["c_k"], new_kv["c_v"] = ck, cv
            lamv = jnp.concatenate([lam_q1[li][None], lam_k1[li][None], lam_q2[li][None], lam_k2[li][None],
                                    jnp.zeros((4, HEAD_DIM), f32)], axis=0)
            subln = subln_c[li][None, :]
            o_p = _ctx1(lamv, subln, qkv, lam_init)
            o_s = _lat1(lamv, subln, qkv,
                        cache_c_k[:, li].reshape(N_SAMPLE_BATCH, PAST_LEN, D_MODEL),
                        cache_c_v[:, li].reshape(N_SAMPLE_BATCH, PAST_LEN, D_MODEL), lam_init)
            w_out = w_out_c[li].astype(bf16)
        last = layer == DEPTH - 1
        x = _moe(x, o_p, o_s, w_out, mod_l, g_ffn, router_w[layer].T, router_bias[layer][:, None],
                 exp_w_gate, exp_w_up, exp_w_down, layer,
                 sh_w_gate[layer].astype(bf16), sh_w_up[layer].astype(bf16), sh_w_down[layer].astype(bf16),
                 final_norm[None, :] if last else None)
        x = x if last else (x,)
    y_prompt, y_sample = x
    nb, s = N_PROMPT_BATCH, PROMPT_SEQ
    return (y_prompt.reshape(nb, s, D_MODEL), y_sample.reshape(N_SAMPLE_BATCH, SAMPLE_SEQ, D_MODEL),
            _from_feature_major(new_kv["a_k"], 2, HEAD_DIM), _from_feature_major(new_kv["a_v"], 2, HEAD_DIM),
            _from_feature_major(new_kv["b_k"], 8, HEAD_DIM), _from_feature_major(new_kv["b_v"], 8, HEAD_DIM),
            _from_feature_major(new_kv["c_k"], 8, 2, HEAD_DIM), new_kv["c_v"].reshape(nb, 1, s, 8, 2 * HEAD_DIM))
```

```python
import functools
import math

import jax
import jax.numpy as jnp
import numpy as np
from jax import lax
from jax.experimental import pallas as pl
from jax.experimental.pallas import tpu as pltpu

f32 = jnp.float32
bf16 = jnp.bfloat16
i32 = jnp.int32

D_MODEL = 1024
N_PROMPT_BATCH = 16
PROMPT_SEQ = 256
DEPTH = 2
N_SAMPLE_BATCH = 2
SAMPLE_SEQ = 2048
PAST_LEN = 512
GRID_W = 64
HEAD_DIM = 64
ROPE_THETA = 10000.0
EPS = 1e-6
A_WINDOW = 128
NA_ROWS = 8
NA_COLS = 16
N_EXPERTS = 64
TOP_K = 6
N_GROUPS = 8
TOPK_GROUPS = 4
EXPERT_DIM = 256
ROUTED_SCALE = 2.5
Q_SCALE = HEAD_DIM ** -0.5

N_PROMPT = N_PROMPT_BATCH * PROMPT_SEQ
N_SAMPLE = N_SAMPLE_BATCH * SAMPLE_SEQ
N_TOK = N_PROMPT + N_SAMPLE

LANES = 128
TM = 256
TD = 512
N_PROMPT_TILES = N_PROMPT // TM
N_TILES = N_TOK // TM
QB = 128
CHUNK = 16
SLOTS = -(-(TM * TOP_K + N_EXPERTS * (CHUNK - 1)) // 256) * 256
SLOT_CHUNKS = SLOTS // CHUNK
ROUTE_ROWS = 512
ROUTE_TILES = 2
COMBINE_SLOTS = 3
COMBINE_SURE_CHUNKS = 136
GM = 256
_MAX_SORTED = TM * TOP_K * N_TILES + N_TILES * N_EXPERTS * (CHUNK - 1) + N_EXPERTS * (GM - CHUNK)
G_TILES = -(-_MAX_SORTED // GM)
G_CHUNKS = GM // CHUNK
GMM_ITEM_TILES = 2
GMM_SLOTS = 4
GMM_MAX_ITEMS = (G_TILES + N_EXPERTS * (GMM_ITEM_TILES - 1)) // GMM_ITEM_TILES
_GMM_TAIL_PER_STEP = -(-(G_TILES - TM * TOP_K * N_TILES // GM) // N_EXPERTS)
ADA_COLS = 1536
DIFF_HEADS_PER_PASS = 4
VMEM_LIMIT = 56 * 1024 * 1024

NEG = -1e30


def _cparams(sem):
    return pltpu.CompilerParams(dimension_semantics=sem, vmem_limit_bytes=VMEM_LIMIT)


def _mod_row(i, tm=TM):
    return jnp.where(i < N_PROMPT // tm, 0, 1 + (i - N_PROMPT // tm) // (SAMPLE_SEQ // tm))


def _prompt_block(i, tm=TM):
    return (jnp.minimum(i, N_PROMPT // tm - 1), 0)


def _sample_block(i, tm=TM):
    return (jnp.maximum(i - N_PROMPT // tm, 0), 0)


def _x_specs(parts, tm=TM):
    if len(parts) == 1:
        return [pl.BlockSpec((tm, D_MODEL), lambda i, *_: (i, 0))]
    return [pl.BlockSpec((tm, D_MODEL), lambda i, *_: _prompt_block(i, tm)),
            pl.BlockSpec((tm, D_MODEL), lambda i, *_: _sample_block(i, tm))]


def _load_x(i, x_refs, tm=TM):
    if len(x_refs) == 1:
        return x_refs[0][...]
    return jnp.where(i < N_PROMPT // tm, x_refs[0][...], x_refs[1][...])


def _norm_mod(x, g, scale, shift):
    y = x * lax.rsqrt(jnp.mean(x * x, axis=-1, keepdims=True) + EPS)
    return (y * g) * (1.0 + scale) + shift


def _silu(x):
    return x * jax.nn.sigmoid(x)


def _dot(a, b):
    return jnp.dot(a, b, preferred_element_type=f32)


def _dot_nt(a, b):
    return lax.dot_general(a, b, (((1,), (1,)), ((), ())), preferred_element_type=f32)


def _adaln_kernel(cond_ref, w_ref, b_ref, o_ref):
    s = _silu(cond_ref[...]).astype(bf16)
    o_ref[...] = _dot(s, w_ref[...].astype(bf16)) + b_ref[...]


def _adaln(cond8, w_mod, b_mod):
    n6 = 6 * D_MODEL
    return pl.pallas_call(
        _adaln_kernel,
        grid=(DEPTH, n6 // ADA_COLS),
        in_specs=[
            pl.BlockSpec((8, D_MODEL), lambda l, j: (0, 0)),
            pl.BlockSpec((None, D_MODEL, ADA_COLS), lambda l, j: (l, 0, j)),
            pl.BlockSpec((None, 1, ADA_COLS), lambda l, j: (l, 0, j)),
        ],
        out_specs=pl.BlockSpec((None, 8, ADA_COLS), lambda l, j: (l, 0, j)),
        out_shape=jax.ShapeDtypeStruct((DEPTH, 8, n6), f32),
        compiler_params=_cparams(("parallel", "parallel")),
        name="adaln",
    )(cond8, w_mod, b_mod.reshape(DEPTH, 1, n6))


def _rope_block(blk, cos, sin_a, sin_b):
    return blk * cos + pltpu.roll(blk, LANES - 16, 1) * sin_a + pltpu.roll(blk, 16, 1) * sin_b


def _inproj_kernel(*refs, chunks, n_x):
    x_refs, kv_refs = refs[:n_x], refs[n_x + 7:]
    mod_ref, g_ref, w_ref, cos_ref, sa_ref, sb_ref, qkv_ref = refs[n_x:n_x + 7]
    i = pl.program_id(0)
    is_prompt = i < N_PROMPT // TD
    kb = 256

    def normed_input():
        x = _load_x(i, x_refs, TD)
        rinv = lax.rsqrt(jnp.mean(x * x, axis=-1, keepdims=True) + EPS)
        c0, c1, s0 = chunks[0][:3]
        blocks, acc = [], None
        for k in range(D_MODEL // kb):
            sl = slice(k * kb, (k + 1) * kb)
            hk = ((((x[:, sl] * rinv) * g_ref[:, sl]) * (1.0 + mod_ref[1:2, sl])) + mod_ref[0:1, sl]).astype(bf16)
            part = _dot(hk, w_ref[sl, s0:s0 + (c1 - c0)])
            acc = part if acc is None else acc + part
            blocks.append(hk)
        return jnp.concatenate(blocks, axis=1), acc

    @pl.when(is_prompt)
    def _():
        h, acc0 = normed_input()
        for n, (c0, c1, s0, _, kv_out) in enumerate(chunks):
            acc = acc0 if n == 0 else _dot(h, w_ref[:, s0:s0 + (c1 - c0)])
            qkv_ref[:, c0:c1] = acc.astype(bf16)
            for ridx, a0, a1, o0, per_head in kv_out:
                if per_head:
                    heads = kv_refs[ridx].shape[0] // TD
                    for j in range((a1 - a0) // LANES):
                        kv_refs[ridx][pl.ds(o0 // LANES + j, TD, stride=heads), :] = (
                            acc[:, a0 + j * LANES:a0 + (j + 1) * LANES])
                else:
                    t = acc[:, a0:a1].T
                    for b in range(TD // PROMPT_SEQ):
                        kv_refs[ridx][b, o0:o0 + (a1 - a0), :] = t[:, b * PROMPT_SEQ:(b + 1) * PROMPT_SEQ]

    @pl.when(jnp.logical_not(is_prompt))
    def _():
        cos, sa, sb = cos_ref[...], sa_ref[...], sb_ref[...]
        h, acc0 = normed_input()
        for n, (c0, c1, s0, rope_blocks, _) in enumerate(chunks):
            acc = acc0 if n == 0 else _dot(h, w_ref[:, s0:s0 + (c1 - c0)])
            for b in range((c1 - c0) // LANES):
                blk = acc[:, b * LANES:(b + 1) * LANES]
                if b in rope_blocks:
                    blk = _rope_block(blk, cos, sa, sb)
                qkv_ref[:, c0 + b * LANES:c0 + (b + 1) * LANES] = blk.astype(bf16)


def _inproj(x_parts, mod_l, g, w, rope_tabs, chunks, kv_outs):
    n = w.shape[1]
    cos, sa, sb = rope_tabs
    bpt = TD // PROMPT_SEQ

    def rope_idx(i):
        return (jnp.where(i < N_PROMPT // TD, 0, (i - N_PROMPT // TD) % (SAMPLE_SEQ // TD)), 0)

    kv_specs, kv_shapes = [], []
    for kind, size in kv_outs:
        if kind == "T":
            kv_specs.append(pl.BlockSpec((bpt, size, PROMPT_SEQ), lambda i: _prompt_block(i, TD) + (0,)))
            kv_shapes.append(jax.ShapeDtypeStruct((N_PROMPT_BATCH, size, PROMPT_SEQ), f32))
        else:
            kv_specs.append(pl.BlockSpec((TD * size, LANES), lambda i: _prompt_block(i, TD)))
            kv_shapes.append(jax.ShapeDtypeStruct((N_PROMPT * size, LANES), f32))

    return pl.pallas_call(
        functools.partial(_inproj_kernel, chunks=chunks, n_x=len(x_parts)),
        grid=(N_TOK // TD,),
        in_specs=_x_specs(x_parts, TD) + [
            pl.BlockSpec((None, 6, D_MODEL), lambda i: (_mod_row(i, TD), 0, 0)),
            pl.BlockSpec((1, D_MODEL), lambda i: (0, 0)),
            pl.BlockSpec((D_MODEL, n), lambda i: (0, 0)),
            pl.BlockSpec((TD, LANES), rope_idx),
            pl.BlockSpec((TD, LANES), rope_idx),
            pl.BlockSpec((TD, LANES), rope_idx),
        ],
        out_specs=[pl.BlockSpec((TD, n), lambda i: (i, 0))] + kv_specs,
        out_shape=[jax.ShapeDtypeStruct((N_TOK, n), bf16)] + kv_shapes,
        compiler_params=_cparams(("arbitrary",)),
        name="inproj",
    )(*x_parts, mod_l, g, w, cos, sa, sb)


def _rope_tables():
    nq = HEAD_DIM // 4
    t = np.arange(SAMPLE_SEQ)
    inv = np.power(np.float32(ROPE_THETA), -np.arange(nq, dtype=np.float32) / np.float32(nq))
    ang_r = (t // GRID_W).astype(np.float32)[:, None] * inv
    ang_c = (t % GRID_W).astype(np.float32)[:, None] * inv
    zero = np.zeros_like(ang_r)

    def head(fr, fc):
        return np.concatenate([fr[0], fr[1], fc[0], fc[1]], axis=-1)

    cos = head((np.cos(ang_r), np.cos(ang_r)), (np.cos(ang_c), np.cos(ang_c)))
    sin_a = head((-np.sin(ang_r), zero), (-np.sin(ang_c), zero))
    sin_b = head((zero, np.sin(ang_r)), (zero, np.sin(ang_c)))
    two = lambda a: jnp.asarray(np.concatenate([a, a], axis=-1).astype(np.float32))
    return two(cos), two(sin_a), two(sin_b)


def _lane_lo(shape):
    return lax.broadcasted_iota(i32, shape, len(shape) - 1) < HEAD_DIM


def _half(q, lo_mask, half):
    keep = lo_mask if half == 0 else jnp.logical_not(lo_mask)
    return jnp.where(keep, q, jnp.zeros_like(q)) * Q_SCALE


def _swap_halves(x):
    return pltpu.roll(x.astype(f32), HEAD_DIM, 1).astype(x.dtype)


def _stack_halves(q, lo_mask):
    return jnp.concatenate([_half(q, lo_mask, 0), _half(q, lo_mask, 1)], axis=0)


def _with_ones(v):
    return jnp.concatenate([v, jnp.ones_like(v)], axis=1)


def _attend_many(problems):
    scores = [[_dot_nt(q_rows, k) for k in key_blocks] for q_rows, _, key_blocks, _, _, _ in problems]
    exps, maxes = [], []
    for (q_rows, n_heads, key_blocks, _, fix_scores, sinks), sc in zip(problems, scores):
        r = q_rows.shape[0] // n_heads
        e_p, m_p = [[] for _ in key_blocks], []
        for h in range(n_heads):
            blocks = [s[h * r:(h + 1) * r] for s in sc]
            if fix_scores is not None:
                blocks = [fix_scores(h, i, s) for i, s in enumerate(blocks)]
            m = functools.reduce(jnp.maximum, [jnp.max(s, axis=-1, keepdims=True) for s in blocks])
            if sinks is not None:
                m = jnp.maximum(m, sinks[h])
            m_p.append(m)
            for i, s in enumerate(blocks):
                e_p[i].append(jnp.exp((s - m).astype(bf16)))
        exps.append(e_p)
        maxes.append(m_p)
    outs = [functools.reduce(lambda a, b: a + b,
                             [_dot(e[0] if n_heads == 1 else jnp.concatenate(e, axis=0), vx)
                              for e, vx in zip(e_p, vx_blocks)])
            for (_, n_heads, _, vx_blocks, _, _), e_p in zip(problems, exps)]
    results = []
    for (q_rows, n_heads, _, _, _, sinks), out, m_p in zip(problems, outs, maxes):
        r = q_rows.shape[0] // n_heads
        res = []
        for h in range(n_heads):
            den = out[h * r:(h + 1) * r, LANES:]
            if sinks is not None:
                den = den + jnp.exp(sinks[h] - m_p[h])
            res.append(out[h * r:(h + 1) * r, :LANES] * (1.0 / den))
        results.append(res)
    return results


def _gqa_rows(q_blocks, group, lo_mask):
    parts = []
    for q in q_blocks:
        for half in range(2):
            qh = _half(q, lo_mask, half)
            parts.append(qh if half == group else _swap_halves(qh))
    return jnp.concatenate(parts, axis=0)


def _gqa_merge(outs, group, lo_mask):
    fixed = [o if idx % 2 == group else pltpu.roll(o, HEAD_DIM, 1) for idx, o in enumerate(outs)]
    return [jnp.where(lo_mask, fixed[2 * p], fixed[2 * p + 1]) for p in range(len(outs) // 2)]


L0_QA, L0_QB, L0_KB, L0_VB, L0_KA, L0_VA, L0_N = 0, 512, 1024, 1536, 2048, 2176, 2304


def _ctx0_kernel(sink_ref, qkv_ref, o_ref):
    lo = _lane_lo((1, LANES))
    blk = lambda base, j: qkv_ref[:, base + j * LANES:base + (j + 1) * LANES]
    k_a = blk(L0_KA, 0)
    vx_a = _with_ones(blk(L0_VA, 0))
    problems = []
    for g in range(2):
        q_rows = _gqa_rows([blk(L0_QA, 2 * g), blk(L0_QA, 2 * g + 1)], g, lo)
        problems.append((q_rows, 4, [k_a], [vx_a], None, [sink_ref[4 * g + idx] for idx in range(4)]))
    for j in range(4):
        problems.append((_stack_halves(blk(L0_QB, j), lo), 2, [blk(L0_KB, j)], [_with_ones(blk(L0_VB, j))],
                         None, None))
    results = _attend_many(problems)
    for g in range(2):
        for p, o in enumerate(_gqa_merge(results[g], g, lo)):
            j = 2 * g + p
            o_ref[:, j * LANES:(j + 1) * LANES] = o.astype(bf16)
    for j in range(4):
        outs = results[2 + j]
        o_ref[:, 512 + j * LANES:512 + (j + 1) * LANES] = jnp.where(lo, outs[0], outs[1]).astype(bf16)


def _ctx0(sink, qkv):
    return pl.pallas_call(
        _ctx0_kernel,
        grid=(N_PROMPT_BATCH,),
        in_specs=[
            pl.BlockSpec(memory_space=pltpu.SMEM),
            pl.BlockSpec((PROMPT_SEQ, L0_N), lambda b: (b, 0)),
        ],
        out_specs=pl.BlockSpec((PROMPT_SEQ, D_MODEL), lambda b: (b, 0)),
        out_shape=jax.ShapeDtypeStruct((N_PROMPT, D_MODEL), bf16),
        compiler_params=_cparams(("parallel",)),
        name="ctx0",
    )(sink, qkv)


WIN_KEYS = 3 * QB
NA_KEY_ROWS = 10
NA_KEYS = NA_KEY_ROWS * GRID_W
N_QB = SAMPLE_SEQ // QB
N_NA_PATTERNS = 5
_PROMPT_QBLOCKS = N_PROMPT // QB


def _na_pattern(n):
    return jnp.where(n < 2, n, jnp.where(n > N_QB - 3, n - (N_QB - 5), 2))


def _lat0_kernel(sink_ref, roff_ref, q_ref, kvb_ref, kva_ref, cak_ref, cav_ref, cbk_ref, cbv_ref, tiles_ref, o_ref):
    n = pl.program_id(1)
    lo = _lane_lo((1, LANES))
    kstart = pl.multiple_of(jnp.clip((n - 1) * QB, 0, SAMPLE_SEQ - WIN_KEYS), QB)
    k_a = kva_ref[pl.ds(kstart, WIN_KEYS), 0:LANES]
    v_a = kva_ref[pl.ds(kstart, WIN_KEYS), LANES:2 * LANES]
    c_k = cak_ref[...].astype(bf16)
    keys_a = [c_k, k_a]
    vx_a = [_with_ones(cav_ref[...].astype(bf16)), _with_ones(v_a)]
    qpos = n * QB + lax.broadcasted_iota(i32, (QB, WIN_KEYS), 0)
    kpos = kstart + lax.broadcasted_iota(i32, (QB, WIN_KEYS), 1)
    in_window = jnp.abs(qpos - kpos) <= A_WINDOW
    mask_window = lambda h, i, s: jnp.where(in_window, s, NEG) if i == 1 else s
    problems = []
    for g in range(2):
        q_rows = _gqa_rows([q_ref[:, L0_QA + j * LANES:L0_QA + (j + 1) * LANES] for j in (2 * g, 2 * g + 1)], g, lo)
        problems.append((q_rows, 4, keys_a, vx_a, mask_window, [sink_ref[4 * g + idx] for idx in range(4)]))
    krow = jnp.clip(2 * n - NA_ROWS // 2, 0, SAMPLE_SEQ // GRID_W - NA_KEY_ROWS)
    ktok = pl.multiple_of(krow * GRID_W, QB)
    pattern = _na_pattern(n)

    def na_bias(head):
        rows = []
        for rq in range(QB // GRID_W):
            blocks = []
            for kb in range(NA_KEY_ROWS // 2):
                d0, d1 = (roff_ref[(pattern * 2 + rq) * NA_KEY_ROWS + 2 * kb + t] for t in range(2))
                blocks.append(jnp.where(lo, tiles_ref[head, d0], tiles_ref[head, d1]))
            rows.append(jnp.concatenate(blocks, axis=1))
        return jnp.concatenate(rows, axis=0)

    for j in range(4):
        q_b = q_ref[:, L0_QB + j * LANES:L0_QB + (j + 1) * LANES]
        k_b = kvb_ref[pl.ds(ktok, NA_KEYS), j * LANES:(j + 1) * LANES]
        v_b = kvb_ref[pl.ds(ktok, NA_KEYS), 512 + j * LANES:512 + (j + 1) * LANES]
        cb_k = cbk_ref[:, j * LANES:(j + 1) * LANES].astype(bf16)
        cb_v = cbv_ref[:, j * LANES:(j + 1) * LANES].astype(bf16)
        add_bias = lambda h, i, s, j=j: s + na_bias(2 * j + h) if i == 1 else s
        problems.append((_stack_halves(q_b, lo), 2, [cb_k, k_b], [_with_ones(cb_v), _with_ones(v_b)], add_bias, None))

    results = _attend_many(problems)
    for g in range(2):
        for p, o in enumerate(_gqa_merge(results[g], g, lo)):
            j = 2 * g + p
            o_ref[:, j * LANES:(j + 1) * LANES] = o.astype(bf16)
    for j in range(4):
        outs = results[2 + j]
        o_ref[:, 512 + j * LANES:512 + (j + 1) * LANES] = jnp.where(lo, outs[0], outs[1]).astype(bf16)


def _lat0(sink, qkv, cak, cav, cbk, cbv, bias_tiles):
    sb = N_PROMPT // SAMPLE_SEQ
    return pl.pallas_call(
        _lat0_kernel,
        grid=(N_SAMPLE_BATCH, N_QB),
        in_specs=[
            pl.BlockSpec(memory_space=pltpu.SMEM),
            pl.BlockSpec(memory_space=pltpu.SMEM),
            pl.BlockSpec((QB, 1024), lambda b, n: (_PROMPT_QBLOCKS + b * N_QB + n, 0)),
            pl.BlockSpec((SAMPLE_SEQ, 1024), lambda b, n: (sb + b, 1)),
            pl.BlockSpec((SAMPLE_SEQ, 256), lambda b, n: (sb + b, L0_KA // 256)),
            pl.BlockSpec((None, PAST_LEN, LANES), lambda b, n: (b, 0, 0)),
            pl.BlockSpec((None, PAST_LEN, LANES), lambda b, n: (b, 0, 0)),
            pl.BlockSpec((None, PAST_LEN, 512), lambda b, n: (b, 0, 0)),
            pl.BlockSpec((None, PAST_LEN, 512), lambda b, n: (b, 0, 0)),
            pl.BlockSpec((8, N_ROW_OFFSETS + 1, GRID_W, LANES), lambda b, n: (0, 0, 0, 0)),
        ],
        out_specs=pl.BlockSpec((QB, D_MODEL), lambda b, n: (b * N_QB + n, 0)),
        out_shape=jax.ShapeDtypeStruct((N_SAMPLE, D_MODEL), bf16),
        compiler_params=_cparams(("parallel", "arbitrary")),
        name="lat0",
    )(sink, jnp.asarray(_na_row_offsets()), qkv, qkv, qkv, cak, cav, cbk, cbv, bias_tiles)


N_ROW_OFFSETS = 2 * NA_ROWS - 1


def _na_row_offsets():
    rows = SAMPLE_SEQ // GRID_W
    idx = np.full((N_NA_PATTERNS, 2, NA_KEY_ROWS), N_ROW_OFFSETS, np.int32)
    for p, n in enumerate((0, 1, 2, N_QB - 2, N_QB - 1)):
        k0 = int(np.clip(2 * n - NA_ROWS // 2, 0, rows - NA_KEY_ROWS))
        for rq in range(2):
            r = 2 * n + rq
            rs = int(np.clip(r - NA_ROWS // 2, 0, rows - NA_ROWS))
            for kl in range(NA_KEY_ROWS):
                if rs <= k0 + kl < rs + NA_ROWS:
                    idx[p, rq, kl] = k0 + kl - r + NA_ROWS - 1
    return idx.reshape(-1)


def _na_bias_tiles(rel_bias):
    n_dc = 2 * NA_COLS - 1
    c = np.arange(GRID_W)[:, None]
    kc = np.arange(GRID_W)[None, :]
    cs = np.clip(c - NA_COLS // 2, 0, GRID_W - NA_COLS)
    col_ok = (kc >= cs) & (kc < cs + NA_COLS)
    col_hot = ((kc - c + NA_COLS - 1)[None] == np.arange(n_dc)[:, None, None]) & col_ok[None]
    hp = lax.Precision.HIGHEST
    tiles = jnp.einsum("hdx,xck->hdck", rel_bias.astype(f32), col_hot.astype(np.float32), precision=hp)
    tiles = tiles + np.where(col_ok, 0.0, NEG).astype(np.float32)
    tiles = jnp.concatenate([tiles, jnp.full((tiles.shape[0], 1, GRID_W, GRID_W), NEG, f32)], axis=1)
    return jnp.concatenate([tiles, tiles], axis=-1)


def _diff_lambda(lam_ref, lam_init):
    lv = lam_ref[...]
    s1 = jnp.sum(lv[0:1, :] * lv[1:2, :], axis=-1, keepdims=True)
    s2 = jnp.sum(lv[2:3, :] * lv[3:4, :], axis=-1, keepdims=True)
    return jnp.exp(s1) - jnp.exp(s2) + lam_init


def _diff_heads(heads, o_ref, lam, subln, lo, lam_init):
    for p0 in range(0, len(heads), DIFF_HEADS_PER_PASS):
        group = heads[p0:p0 + DIFF_HEADS_PER_PASS]
        results = _attend_many([(_stack_halves(q, lo), 2, ks, [_with_ones(v) for v in vs], None, None)
                                for _, q, ks, vs in group])
        for (h, _, _, _), (o1, o2) in zip(group, results):
            o = o1 - lam * o2
            o = o * lax.rsqrt(jnp.mean(o * o, axis=-1, keepdims=True) + EPS)
            o_ref[:, h * LANES:(h + 1) * LANES] = ((o * subln) * (1.0 - lam_init)).astype(bf16)


def _ctx1_kernel(lam_ref, subln_ref, qkv_ref, o_ref, *, lam_init):
    lo = _lane_lo((1, LANES))
    lam = _diff_lambda(lam_ref, lam_init)
    blk = lambda base, h: qkv_ref[:, base + h * LANES:base + (h + 1) * LANES]
    heads = [(h, blk(0, h), [blk(D_MODEL, h)], [blk(2 * D_MODEL, h)]) for h in range(8)]
    _diff_heads(heads, o_ref, lam, subln_ref[...], lo, lam_init)


def _ctx1(lamv, subln, qkv, lam_init):
    return pl.pallas_call(
        functools.partial(_ctx1_kernel, lam_init=lam_init),
        grid=(N_PROMPT_BATCH,),
        in_specs=[
            pl.BlockSpec((8, HEAD_DIM), lambda b: (0, 0)),
            pl.BlockSpec((1, LANES), lambda b: (0, 0)),
            pl.BlockSpec((PROMPT_SEQ, 3 * D_MODEL), lambda b: (b, 0)),
        ],
        out_specs=pl.BlockSpec((PROMPT_SEQ, D_MODEL), lambda b: (b, 0)),
        out_shape=jax.ShapeDtypeStruct((N_PROMPT, D_MODEL), bf16),
        compiler_params=_cparams(("parallel",)),
        name="ctx1",
    )(lamv, subln, qkv)


def _lat1_kernel(lam_ref, subln_ref, q_ref, k_ref, v_ref, ck_ref, cv_ref, o_ref, *, lam_init):
    lo = _lane_lo((1, LANES))
    lam = _diff_lambda(lam_ref, lam_init)
    heads = []
    for h in range(8):
        sl = slice(h * LANES, (h + 1) * LANES)
        heads.append((h, q_ref[:, sl], [ck_ref[:, sl].astype(bf16), k_ref[:, sl]],
                      [cv_ref[:, sl].astype(bf16), v_ref[:, sl]]))
    _diff_heads(heads, o_ref, lam, subln_ref[...], lo, lam_init)


def _lat1(lamv, subln, qkv, ck, cv, lam_init):
    sb = N_PROMPT // SAMPLE_SEQ
    nq = SAMPLE_SEQ // TM
    return pl.pallas_call(
        functools.partial(_lat1_kernel, lam_init=lam_init),
        grid=(N_SAMPLE_BATCH, nq),
        in_specs=[
            pl.BlockSpec((8, HEAD_DIM), lambda b, n: (0, 0)),
            pl.BlockSpec((1, LANES), lambda b, n: (0, 0)),
            pl.BlockSpec((TM, D_MODEL), lambda b, n: (N_PROMPT_TILES + b * nq + n, 0)),
            pl.BlockSpec((SAMPLE_SEQ, D_MODEL), lambda b, n: (sb + b, 1)),
            pl.BlockSpec((SAMPLE_SEQ, D_MODEL), lambda b, n: (sb + b, 2)),
            pl.BlockSpec((None, PAST_LEN, D_MODEL), lambda b, n: (b, 0, 0)),
            pl.BlockSpec((None, PAST_LEN, D_MODEL), lambda b, n: (b, 0, 0)),
        ],
        out_specs=pl.BlockSpec((TM, D_MODEL), lambda b, n: (b * nq + n, 0)),
        out_shape=jax.ShapeDtypeStruct((N_SAMPLE, D_MODEL), bf16),
        compiler_params=_cparams(("parallel", "arbitrary")),
        name="lat1",
    )(lamv, subln, qkv, qkv, qkv, ck, cv)


def _split_bf16(a):
    hi = a.astype(bf16)
    return hi, (a - hi.astype(f32)).astype(bf16)


def _route_kernel(*refs, n_x):
    x_refs = refs[:n_x]
    (op_ref, os_ref, mod_ref, g_ref, wo_ref, rwt_ref, rb_ref,
     xnew_ref, xloc_ref, slots_ref, gate_ref, len_ref) = refs[n_x:]
    logits = _route_logits(x_refs, op_ref, os_ref, mod_ref, g_ref, wo_ref, rwt_ref, xnew_ref)
    tiles = _route_tiles(jnp.concatenate([lg for _, lg in logits], axis=1), rb_ref, slots_ref, gate_ref, len_ref)
    _route_dispatch(tiles, [h_hi for h_hi, _ in logits], xloc_ref)


def _route_logits(x_refs, op_ref, os_ref, mod_ref, g_ref, wo_ref, rwt_ref, xnew_ref):
    is_prompt = pl.program_id(0) < N_PROMPT_TILES // ROUTE_TILES
    tiles = [slice(t * TM, (t + 1) * TM) for t in range(ROUTE_TILES)]
    proj = [_dot(jnp.where(is_prompt, op_ref[rows, :], os_ref[rows, :]), wo_ref[...]) for rows in tiles]
    w_hi, w_lo = _split_bf16(rwt_ref[...])
    splits = []
    for rows, y in zip(tiles, proj):
        x_in = x_refs[0][rows, :] if len(x_refs) == 1 else jnp.where(is_prompt, x_refs[0][rows, :], x_refs[1][rows, :])
        x = x_in + mod_ref[2:3, :] * y
        xnew_ref[rows, :] = x
        splits.append(_split_bf16(_norm_mod(x, g_ref[...], mod_ref[4:5, :], mod_ref[3:4, :])))
    return [(h_hi, _dot_nt(w_hi, h_hi) + (_dot_nt(w_hi, h_lo) + _dot_nt(w_lo, h_hi)))
            for h_hi, h_lo in splits]


def _route_tiles(logits, rb_ref, slots_ref, gate_ref, len_ref):
    ng, ge = N_GROUPS, N_EXPERTS // N_GROUPS
    n = ROUTE_TILES * TM
    tile = lambda a, t: a[..., t * TM:(t + 1) * TM]
    scores = jax.nn.sigmoid(logits)
    biased = scores + rb_ref[...]
    s3 = scores.reshape(ng, ge, n)
    b3 = biased.reshape(ng, ge, n)
    in_group = lax.broadcasted_iota(i32, (ng, ge, n), 1).astype(f32)
    group_id = lax.broadcasted_iota(i32, (ng, 1, n), 0).astype(f32)
    expert_id = lax.broadcasted_iota(i32, (ng, ge, n), 0).astype(f32) * ge + in_group

    def max01(a):
        return jnp.max(jnp.max(a, axis=0, keepdims=True), axis=1, keepdims=True)

    def min01(a):
        return jnp.min(jnp.min(a, axis=0, keepdims=True), axis=1, keepdims=True)

    def sum01(a):
        return jnp.sum(jnp.sum(a, axis=0, keepdims=True), axis=1, keepdims=True)

    m1 = jnp.max(b3, axis=1, keepdims=True)
    first = jnp.min(jnp.where(b3 == m1, in_group, ge), axis=1, keepdims=True)
    m2 = jnp.max(jnp.where(in_group == first, -jnp.inf, b3), axis=1, keepdims=True)
    gscore = m1 + m2
    gsel = jnp.zeros((ng, 1, n), f32)
    for _ in range(TOPK_GROUPS):
        gm = jnp.max(gscore, axis=0, keepdims=True)
        gi = jnp.min(jnp.where(gscore == gm, group_id, ng), axis=0, keepdims=True)
        hit = group_id == gi
        gsel = jnp.where(hit, 1.0, gsel)
        gscore = jnp.where(hit, -jnp.inf, gscore)
    cand = jnp.where(jnp.broadcast_to(gsel, (ng, ge, n)) > 0.0, b3, -jnp.inf)
    top_e, top_w = [], []
    for _ in range(TOP_K):
        em = max01(cand)
        ei = min01(jnp.where(cand == em, expert_id, N_EXPERTS))
        hit = expert_id == ei
        top_e.append(ei)
        top_w.append(sum01(jnp.where(hit, s3, 0.0)))
        cand = jnp.where(hit, -jnp.inf, cand)
    wsum = functools.reduce(lambda a, b: a + b, top_w)
    sel3 = jnp.zeros((ng, ge, n), f32)
    for k, (ei, w) in enumerate(zip(top_e, top_w)):
        gate = (w / wsum * ROUTED_SCALE).reshape(1, n)
        for t in range(ROUTE_TILES):
            gate_ref[t, k:k + 1, :] = tile(gate, t)
        sel3 = jnp.where(expert_id == ei, 1.0, sel3)
    sel = sel3.reshape(N_EXPERTS, n)

    r_i = lax.broadcasted_iota(i32, (N_EXPERTS, N_EXPERTS), 0)
    c_i = lax.broadcasted_iota(i32, (N_EXPERTS, N_EXPERTS), 1)
    lower = jnp.where(c_i < r_i, 1.0, 0.0).astype(bf16)
    run_lens, run_offs = [], []
    for t in range(ROUTE_TILES):
        cnt = jnp.sum(tile(sel, t), axis=1, keepdims=True)
        run_len = jnp.ceil(cnt * (1.0 / CHUNK)) * CHUNK
        run_off = _dot(lower, jnp.broadcast_to(run_len, (N_EXPERTS, LANES)).astype(bf16))[:, 0:1]
        run_lens.append(run_len)
        run_offs.append(jnp.broadcast_to(run_off, (N_EXPERTS, TM)))
    t_r = lax.broadcasted_iota(i32, (n, n), 0)
    t_c = lax.broadcasted_iota(i32, (n, n), 1)
    before = jnp.where(jnp.logical_and(t_r < t_c, t_r // TM == t_c // TM), 1.0, 0.0).astype(bf16)
    rank = _dot(sel.astype(bf16), before)
    slot3 = (jnp.concatenate(run_offs, axis=1) + rank).reshape(ng, ge, n)
    slots = [sum01(jnp.where(expert_id == ei, slot3, 0.0)).reshape(1, n).astype(i32) for ei in top_e]
    out = []
    for t in range(ROUTE_TILES):
        for k in range(TOP_K):
            slots_ref[t, k:k + 1, :] = tile(slots[k], t)
        slots_ref[t, TOP_K:8, :] = jnp.full((8 - TOP_K, TM), -1, i32)
        gate_ref[t, TOP_K:8, :] = jnp.zeros((8 - TOP_K, TM), f32)
        cnt_row = _dot_nt(jnp.ones((8, TM), bf16), tile(sel, t).astype(bf16))
        len_ref[t] = (jnp.ceil(cnt_row * (1.0 / CHUNK)) * CHUNK).astype(i32)
        out.append(([tile(sl, t) for sl in slots], run_lens[t]))
    return out


def _route_dispatch(tiles, h_his, xloc_ref):
    rows = ROUTE_ROWS

    def onehot(t, base):
        row_id = base.astype(jnp.int16) + lax.broadcasted_iota(jnp.int16, (rows, TM), 0)
        p = jnp.zeros((rows, TM), bf16)
        for k in range(TOP_K):
            p = jnp.where(row_id == tiles[t][0][k].astype(jnp.int16), jnp.ones((), bf16), p)
        return p

    def blocks_of(ts):
        def body(c, carry):
            base = pl.multiple_of(c * rows, rows)
            ps = [onehot(t, base) for t in ts]
            for t, p in zip(ts, ps):
                xloc_ref[pl.ds(t * SLOTS + base, rows), :] = _dot(p, h_his[t]).astype(bf16)
            return carry
        return body

    n_used = [(jnp.sum(run_len).astype(i32) + (rows - 1)) // rows for _, run_len in tiles]
    common = functools.reduce(jnp.minimum, n_used)
    lax.fori_loop(0, common, blocks_of(list(range(len(tiles)))), 0)
    for t in range(len(tiles)):
        lax.fori_loop(common, n_used[t], blocks_of([t]), 0)

        def zero_body(c, carry, t=t):
            base = pl.multiple_of(c * rows, rows)
            xloc_ref[pl.ds(t * SLOTS + base, rows), :] = jnp.zeros((rows, D_MODEL), bf16)
            return carry

        lax.fori_loop(n_used[t], SLOTS // rows, zero_body, 0)


def _route(x_parts, o_prompt, o_sample, mod_l, g, w_out, rwt, rb):
    per_tile = lambda i: (i, 0, 0)
    rt = ROUTE_TILES
    tm = rt * TM
    return pl.pallas_call(
        functools.partial(_route_kernel, n_x=len(x_parts)),
        grid=(N_TILES // rt,),
        in_specs=_x_specs(x_parts, tm) + [
            pl.BlockSpec((tm, D_MODEL), lambda i: _prompt_block(i, tm)),
            pl.BlockSpec((tm, D_MODEL), lambda i: _sample_block(i, tm)),
            pl.BlockSpec((None, 6, D_MODEL), lambda i: (_mod_row(i, tm), 0, 0)),
            pl.BlockSpec((1, D_MODEL), lambda i: (0, 0)),
            pl.BlockSpec((D_MODEL, D_MODEL), lambda i: (0, 0)),
            pl.BlockSpec((N_EXPERTS, D_MODEL), lambda i: (0, 0)),
            pl.BlockSpec((N_EXPERTS, 1), lambda i: (0, 0)),
        ],
        out_specs=[
            pl.BlockSpec((tm, D_MODEL), lambda i: (i, 0)),
            pl.BlockSpec((rt * SLOTS, D_MODEL), lambda i: (i, 0)),
            pl.BlockSpec((rt, 8, TM), per_tile),
            pl.BlockSpec((rt, 8, TM), per_tile),
            pl.BlockSpec((rt, 8, N_EXPERTS), per_tile),
        ],
        out_shape=[
            jax.ShapeDtypeStruct((N_TOK, D_MODEL), f32),
            jax.ShapeDtypeStruct((N_TILES * SLOTS, D_MODEL), bf16),
            jax.ShapeDtypeStruct((N_TILES, 8, TM), i32),
            jax.ShapeDtypeStruct((N_TILES, 8, TM), f32),
            jax.ShapeDtypeStruct((N_TILES, 8, N_EXPERTS), i32),
        ],
        compiler_params=_cparams(("parallel",)),
        name="route",
    )(*x_parts, o_prompt, o_sample, mod_l, g, w_out, rwt, rb)


def _moe_plan(run_len):
    nt, ne = run_len.shape

    def excl_cumsum(a):
        n = a.shape[-1]
        earlier = np.arange(n)[None, :] < np.arange(n)[:, None]
        return jnp.sum(jnp.where(earlier, a[..., None, :], 0), axis=-1)

    def first_diff(a):
        return a - jnp.concatenate([jnp.zeros_like(a[..., :1]), a[..., :-1]], axis=-1)

    off_loc = excl_cumsum(run_len)
    before = excl_cumsum(run_len.T).T
    n_e = jnp.sum(run_len, axis=0)
    n_pad = -(-n_e // GM) * GM
    g_start = excl_cumsum(n_pad)
    total = jnp.sum(n_pad)
    run_dst = g_start[None, :] + before
    run_src = jnp.arange(nt, dtype=i32)[:, None] * SLOTS + off_loc
    dst_f = run_dst.T.reshape(-1)
    shift_f = first_diff((run_src - run_dst).T.reshape(-1))
    rows = jnp.arange((G_TILES + GMM_ITEM_TILES - 1) * G_CHUNKS, dtype=i32) * CHUNK
    shift = jnp.sum(jnp.where(dst_f[None, :] <= rows[:, None], shift_f[None, :], 0), axis=1)
    in_run = jnp.any((g_start[None, :] <= rows[:, None]) & (rows[:, None] < (g_start + n_e)[None, :]), axis=1)
    chunk_src = (jnp.where(in_run, rows + shift, 0) // CHUNK).astype(i32)
    loc_rows = jnp.arange(SLOT_CHUNKS, dtype=i32) * CHUNK
    shift_l = first_diff(run_dst - off_loc)
    shift = jnp.sum(jnp.where(off_loc[:, None, :] <= loc_rows[None, :, None], shift_l[:, None, :], 0), axis=2)
    used = jnp.sum(run_len, axis=1)
    chunk_map = jnp.where(loc_rows[None, :] < used[:, None], (loc_rows[None, :] + shift) // CHUNK, 0).astype(i32)
    tile_start, n_tiles = g_start // GM, n_pad // GM
    n_items = -(-n_tiles // GMM_ITEM_TILES)
    item_start = excl_cumsum(n_items)
    items = jnp.arange(GMM_MAX_ITEMS, dtype=i32)
    owner = items[:, None] >= item_start[None, :]
    e_first = jnp.sum(jnp.where(owner, first_diff(tile_start - GMM_ITEM_TILES * item_start)[None, :], 0), axis=1)
    item_tile = e_first + GMM_ITEM_TILES * items
    e_end = jnp.sum(jnp.where(owner, first_diff(tile_start + n_tiles)[None, :], 0), axis=1)
    item_cnt = jnp.clip(e_end - item_tile, 0, GMM_ITEM_TILES)
    gmm_plan = tuple(a.astype(i32) for a in (item_start, n_items, item_tile, item_cnt, chunk_src))
    long_tiles = (used > COMBINE_SURE_CHUNKS * CHUNK).astype(i32)
    return gmm_plan, chunk_map.reshape(-1), long_tiles


def _gmm_in_copy(xloc_hbm, xbuf, sem, src_chunk, slot, c):
    return pltpu.make_async_copy(xloc_hbm.at[src_chunk], xbuf.at[slot, c], sem.at[slot])


def _gmm_out_copy(ybuf, y_hbm, sem, tile, slot, n_tiles):
    chunks = n_tiles * G_CHUNKS
    return pltpu.make_async_copy(ybuf.at[slot, pl.ds(0, chunks)],
                                 y_hbm.at[pl.ds(tile * G_CHUNKS, chunks)], sem.at[slot])


def _gmm_kernel(i0_ref, ni_ref, it_ref, ic_ref, cs_ref, xloc_hbm, wg_ref, wu_ref, wd_ref, y_hbm,
                xbuf, ybuf, zbuf, wg_b, wu_b, wd_b, in_sem, out_sem, zsem):
    e = pl.program_id(0)
    last = pl.num_programs(0) - 1
    n_items = ni_ref[e]
    first_item = i0_ref[e]
    total_items = i0_ref[last] + ni_ref[last]
    last_item = total_items - 1
    total_tiles = it_ref[last_item] + ic_ref[last_item]

    def start_in(item):
        first = it_ref[item] * G_CHUNKS
        for c in range(GMM_ITEM_TILES * G_CHUNKS):
            _gmm_in_copy(xloc_hbm, xbuf, in_sem, cs_ref[first + c], item % GMM_SLOTS, c).start()

    def wait_in(item):
        for c in range(GMM_ITEM_TILES * G_CHUNKS):
            _gmm_in_copy(xloc_hbm, xbuf, in_sem, 0, item % GMM_SLOTS, c).wait()

    def out_copy(item, fn):
        for cnt in range(1, GMM_ITEM_TILES + 1):
            @pl.when(ic_ref[item] == cnt)
            def _():
                fn(_gmm_out_copy(ybuf, y_hbm, out_sem, it_ref[item], item % GMM_SLOTS, cnt))

    @pl.when(e == 0)
    def _():
        for item in range(GMM_SLOTS - 1):
            start_in(item)
        zbuf[...] = jnp.zeros(zbuf.shape, zbuf.dtype)

    def tail_copies(fn):
        for j in range(_GMM_TAIL_PER_STEP):
            tile = total_tiles + e + j * N_EXPERTS

            @pl.when(tile < G_TILES)
            def _():
                fn(pltpu.make_async_copy(zbuf, y_hbm.at[pl.ds(tile * G_CHUNKS, G_CHUNKS)], zsem.at[0]))

    tail_copies(lambda cp: cp.start())

    @pl.when(n_items > 0)
    def _():
        wg_b[...] = wg_ref[...].astype(bf16)
        wu_b[...] = wu_ref[...].astype(bf16)
        wd_b[...] = wd_ref[...].astype(bf16)

    def body(j, carry):
        item = first_item + j
        slot = item % GMM_SLOTS

        @pl.when(item + (GMM_SLOTS - 1) < total_items)
        def _():
            start_in(item + (GMM_SLOTS - 1))

        wait_in(item)

        @pl.when(item >= GMM_SLOTS)
        def _():
            out_copy(item - GMM_SLOTS, lambda cp: cp.wait())

        for cnt in range(1, GMM_ITEM_TILES + 1):
            @pl.when(ic_ref[item] == cnt)
            def _():
                rows = cnt * GM
                chunks = cnt * G_CHUNKS
                x = xbuf[slot, 0:chunks].reshape(rows, D_MODEL)
                act = _silu(_dot(x, wg_b[...])) * _dot(x, wu_b[...])
                y = _dot(act.astype(bf16), wd_b[...]).astype(bf16)
                ybuf[slot, 0:chunks] = y.reshape(chunks, CHUNK, D_MODEL)

        out_copy(item, lambda cp: cp.start())
        return carry

    lax.fori_loop(0, n_items, body, 0)
    tail_copies(lambda cp: cp.wait())

    @pl.when(e == last)
    def _():
        for back in range(1, GMM_SLOTS + 1):
            out_copy(total_items - back, lambda cp: cp.wait())


def _gmm(plan, xloc, wg, wu, wd, layer):
    rows = GMM_ITEM_TILES * GM
    w_idx = lambda e, *_: (layer, e, 0, 0)
    grid_spec = pltpu.PrefetchScalarGridSpec(
        num_scalar_prefetch=5,
        grid=(N_EXPERTS,),
        in_specs=[
            pl.BlockSpec(memory_space=pl.ANY),
            pl.BlockSpec((None, None, D_MODEL, EXPERT_DIM), w_idx),
            pl.BlockSpec((None, None, D_MODEL, EXPERT_DIM), w_idx),
            pl.BlockSpec((None, None, EXPERT_DIM, D_MODEL), w_idx),
        ],
        out_specs=pl.BlockSpec(memory_space=pl.ANY),
        scratch_shapes=[pltpu.VMEM((GMM_SLOTS, rows // CHUNK, CHUNK, D_MODEL), bf16),
                        pltpu.VMEM((GMM_SLOTS, rows // CHUNK, CHUNK, D_MODEL), bf16),
                        pltpu.VMEM((G_CHUNKS, CHUNK, D_MODEL), bf16),
                        pltpu.VMEM((D_MODEL, EXPERT_DIM), bf16), pltpu.VMEM((D_MODEL, EXPERT_DIM), bf16),
                        pltpu.VMEM((EXPERT_DIM, D_MODEL), bf16),
                        pltpu.SemaphoreType.DMA((GMM_SLOTS,)), pltpu.SemaphoreType.DMA((GMM_SLOTS,)),
                        pltpu.SemaphoreType.DMA((1,))],
    )
    return pl.pallas_call(
        _gmm_kernel,
        grid_spec=grid_spec,
        out_shape=jax.ShapeDtypeStruct((G_TILES * G_CHUNKS, CHUNK, D_MODEL), bf16),
        compiler_params=_cparams(("arbitrary",)),
        name="gmm",
    )(*plan, xloc, wg, wu, wd)


def _combine_copy(y_hbm, ybuf, sem, sorted_chunk, slot, c):
    return pltpu.make_async_copy(y_hbm.at[sorted_chunk], ybuf.at[slot, c], sem.at[slot])


def _combine_kernel(cm_ref, long_ref, y_hbm, slots_ref, gate_ref, x_ref, mod_ref, g_ref, sg_ref, su_ref, sd_ref,
                    *rest, final):
    if final:
        gf_ref, yp_ref, ys_ref, ybuf, sem = rest
    else:
        o_ref, ybuf, sem = rest
    i = pl.program_id(0)
    n = pl.num_programs(0)
    ahead = COMBINE_SLOTS - 1
    slot = i % COMBINE_SLOTS

    def for_chunks(tile, fn):
        for c in range(COMBINE_SURE_CHUNKS):
            fn(c)

        @pl.when(long_ref[tile] == 1)
        def _():
            for c in range(COMBINE_SURE_CHUNKS, SLOT_CHUNKS):
                fn(c)

    def start(tile, s):
        for_chunks(tile, lambda c: _combine_copy(y_hbm, ybuf, sem, cm_ref[tile * SLOT_CHUNKS + c], s, c).start())

    def wait(tile, s):
        for_chunks(tile, lambda c: _combine_copy(y_hbm, ybuf, sem, 0, s, c).wait())

    @pl.when(i == 0)
    def _():
        ybuf[:, COMBINE_SURE_CHUNKS:] = jnp.zeros((COMBINE_SLOTS, SLOT_CHUNKS - COMBINE_SURE_CHUNKS, CHUNK, D_MODEL), bf16)
        for tile in range(ahead):
            start(tile, tile)

    wait(i, slot)
    start((i + ahead) % n, (i + ahead) % COMBINE_SLOTS)

    x = x_ref[...]
    hb = _norm_mod(x, g_ref[...], mod_ref[4:5, :], mod_ref[3:4, :]).astype(bf16)
    shared = _dot((_silu(_dot(hb, sg_ref[...])) * _dot(hb, su_ref[...])).astype(bf16), sd_ref[...])
    row_id = lax.broadcasted_iota(jnp.int16, (SLOTS, TM), 0)
    p = jnp.zeros((SLOTS, TM), bf16)
    for k in range(TOP_K):
        p = jnp.where(row_id == slots_ref[k:k + 1, :].astype(jnp.int16), gate_ref[k:k + 1, :].astype(bf16), p)
    routed = lax.dot_general(p, ybuf[slot].reshape(SLOTS, D_MODEL), (((0,), (0,)), ((), ())),
                             preferred_element_type=f32)
    out = x + mod_ref[5:6, :] * (routed + shared)
    if final:
        y = (out * lax.rsqrt(jnp.mean(out * out, axis=-1, keepdims=True) + EPS)) * gf_ref[...]

        @pl.when(i < N_PROMPT_TILES)
        def _():
            yp_ref[...] = y

        @pl.when(i >= N_PROMPT_TILES)
        def _():
            ys_ref[...] = y
    else:
        o_ref[...] = out

    @pl.when(i == n - 1)
    def _():
        for k in range(1, ahead + 1):
            wait((i + k) % n, (i + k) % COMBINE_SLOTS)


def _combine(chunk_map, long_tiles, y, slots, gates, x, mod_l, g, sg, su, sd, final_g=None):
    shd = sg.shape[1]
    final = final_g is not None
    row_spec = pl.BlockSpec((TM, D_MODEL), lambda i, *_: (i, 0))
    vec_spec = pl.BlockSpec((1, D_MODEL), lambda i, *_: (0, 0))
    if final:
        out_specs = [pl.BlockSpec((TM, D_MODEL), lambda i, *_: _prompt_block(i)),
                     pl.BlockSpec((TM, D_MODEL), lambda i, *_: _sample_block(i))]
        out_shape = [jax.ShapeDtypeStruct((N_PROMPT, D_MODEL), f32), jax.ShapeDtypeStruct((N_SAMPLE, D_MODEL), f32)]
    else:
        out_specs, out_shape = row_spec, jax.ShapeDtypeStruct((N_TOK, D_MODEL), f32)
    grid_spec = pltpu.PrefetchScalarGridSpec(
        num_scalar_prefetch=2,
        grid=(N_TILES,),
        in_specs=[
            pl.BlockSpec(memory_space=pl.ANY),
            pl.BlockSpec((None, 8, TM), lambda i, *_: (i, 0, 0)),
            pl.BlockSpec((None, 8, TM), lambda i, *_: (i, 0, 0)),
            row_spec,
            pl.BlockSpec((None, 6, D_MODEL), lambda i, *_: (_mod_row(i), 0, 0)),
            vec_spec,
            pl.BlockSpec((D_MODEL, shd), lambda i, *_: (0, 0)),
            pl.BlockSpec((D_MODEL, shd), lambda i, *_: (0, 0)),
            pl.BlockSpec((shd, D_MODEL), lambda i, *_: (0, 0)),
        ] + ([vec_spec] if final else []),
        out_specs=out_specs,
        scratch_shapes=[pltpu.VMEM((COMBINE_SLOTS, SLOT_CHUNKS, CHUNK, D_MODEL), bf16),
                        pltpu.SemaphoreType.DMA((COMBINE_SLOTS,))],
    )
    args = (chunk_map, long_tiles, y, slots, gates, x, mod_l, g, sg, su, sd) + ((final_g,) if final else ())
    return pl.pallas_call(
        functools.partial(_combine_kernel, final=final),
        grid_spec=grid_spec,
        out_shape=out_shape,
        compiler_params=_cparams(("arbitrary",)),
        name="combine",
    )(*args)


def _moe(x_parts, o_prompt, o_sample, w_out, mod_l, g, rwt, rb, wg, wu, wd, layer, sg, su, sd, final_g=None):
    x, xloc, slots, gates, run_len = _route(x_parts, o_prompt, o_sample, mod_l, g, w_out, rwt, rb)
    gmm_plan, chunk_map, long_tiles = _moe_plan(run_len[:, 0, :])
    y = _gmm(gmm_plan, xloc.reshape(N_TILES * SLOT_CHUNKS, CHUNK, D_MODEL), wg, wu, wd, layer)
    return _combine(chunk_map, long_tiles, y, slots, gates, x, mod_l, g, sg, su, sd, final_g)


_L0_CHUNKS = (
    (0, 512, 0, (0, 1, 2, 3), ()),
    (512, 1024, 768, (), ()),
    (1024, 1536, 1280, (), ((2, 0, 512, 0, False),)),
    (1536, 2048, 1792, (), ((3, 0, 512, 0, False),)),
    (2048, 2304, 512, (0,), ((0, 0, 128, 0, False), (1, 128, 256, 0, False))),
)
_L0_KV_OUTS = (("T", 128), ("T", 128), ("T", 512), ("T", 512))
_L1_CHUNKS = (
    (0, 512, 0, (0, 1, 2, 3), ()),
    (512, 1024, 512, (0, 1, 2, 3), ()),
    (1024, 1536, 1024, (0, 1, 2, 3), ((0, 0, 512, 0, False),)),
    (1536, 2048, 1536, (0, 1, 2, 3), ((0, 0, 512, 512, False),)),
    (2048, 2560, 2048, (), ((1, 0, 512, 0, True),)),
    (2560, 3072, 2560, (), ((1, 0, 512, 512, True),)),
)
_L1_KV_OUTS = (("T", 1024), ("H", 8))


def _from_feature_major(kt, *head_dims):
    nb, _, s = kt.shape
    nd = len(head_dims)
    return kt.reshape(nb, *head_dims, s).transpose(0, nd + 1, *range(1, nd + 1))[:, None]


def kernel(x_prompt, x_sample, cache_a_k, cache_a_v, cache_b_k, cache_b_v, cache_c_k, cache_c_v, c, c_ctx, w_mod, b_mod, norm_mix, norm_ffn, w_in_ab, w_out_ab, sink_a, rel_bias_b, w_in_c, w_out_c, lam_q1, lam_k1, lam_q2, lam_k2, subln_c, router_w, router_bias, exp_w_gate, exp_w_up, exp_w_down, sh_w_gate, sh_w_up, sh_w_down, final_norm):
    x = (x_prompt.reshape(N_PROMPT, D_MODEL), x_sample.reshape(N_SAMPLE, D_MODEL))
    cond8 = jnp.concatenate([c_ctx[None, :], c, jnp.zeros((8 - 1 - N_SAMPLE_BATCH, D_MODEL), f32)], axis=0)
    mod = _adaln(cond8, w_mod, b_mod).reshape(DEPTH, 8, 6, D_MODEL)
    rope_tabs = _rope_tables()
    new_kv = {}
    for layer in range(DEPTH):
        li = layer // 2
        mod_l = mod[layer]
        g_mix = norm_mix[layer][None, :]
        g_ffn = norm_ffn[layer][None, :]
        if layer % 2 == 0:
            w_in = w_in_ab[li].astype(bf16)
            qkv, ak, av, bk, bv = _inproj(x, mod_l, g_mix, w_in, rope_tabs, _L0_CHUNKS, _L0_KV_OUTS)
            new_kv["a_k"], new_kv["a_v"], new_kv["b_k"], new_kv["b_v"] = ak, av, bk, bv
            o_p = _ctx0(sink_a[li], qkv)
            o_s = _lat0(sink_a[li], qkv,
                        cache_a_k[:, li].reshape(N_SAMPLE_BATCH, PAST_LEN, LANES),
                        cache_a_v[:, li].reshape(N_SAMPLE_BATCH, PAST_LEN, LANES),
                        cache_b_k[:, li].reshape(N_SAMPLE_BATCH, PAST_LEN, 512),
                        cache_b_v[:, li].reshape(N_SAMPLE_BATCH, PAST_LEN, 512),
                        _na_bias_tiles(rel_bias_b[li]))
            w_out = w_out_ab[li].astype(bf16)
        else:
            lam_init = 0.8 - 0.6 * math.exp(-0.3 * layer)
            qkv, ck, cv = _inproj(x, mod_l, g_mix, w_in_c[li].astype(bf16), rope_tabs, _L1_CHUNKS, _L1_KV_OUTS)
            new_kv["c_k"], new_kv["c_v"] = ck, cv
            lamv = jnp.concatenate([lam_q1[li][None], lam_k1[li][None], lam_q2[li][None], lam_k2[li][None],
                                    jnp.zeros((4, HEAD_DIM), f32)], axis=0)
            subln = subln_c[li][None, :]
            o_p = _ctx1(lamv, subln, qkv, lam_init)
            o_s = _lat1(lamv, subln, qkv,
                        cache_c_k[:, li].reshape(N_SAMPLE_BATCH, PAST_LEN, D_MODEL),
                        cache_c_v[:, li].reshape(N_SAMPLE_BATCH, PAST_LEN, D_MODEL), lam_init)
            w_out = w_out_c[li].astype(bf16)
        last = layer == DEPTH - 1
        x = _moe(x, o_p, o_s, w_out, mod_l, g_ffn, router_w[layer].T, router_bias[layer][:, None],
                 exp_w_gate, exp_w_up, exp_w_down, layer,
                 sh_w_gate[layer].astype(bf16), sh_w_up[layer].astype(bf16), sh_w_down[layer].astype(bf16),
                 final_norm[None, :] if last else None)
        x = x if last else (x,)
    y_prompt, y_sample = x
    nb, s = N_PROMPT_BATCH, PROMPT_SEQ
    return (y_prompt.reshape(nb, s, D_MODEL), y_sample.reshape(N_SAMPLE_BATCH, SAMPLE_SEQ, D_MODEL),
            _from_feature_major(new_kv["a_k"], 2, HEAD_DIM), _from_feature_major(new_kv["a_v"], 2, HEAD_DIM),
            _from_feature_major(new_kv["b_k"], 8, HEAD_DIM), _from_feature_major(new_kv["b_v"], 8, HEAD_DIM),
            _from_feature_major(new_kv["c_k"], 8, 2, HEAD_DIM), new_kv["c_v"].reshape(nb, 1, s, 8, 2 * HEAD_DIM))
```

```python
import functools
import math

import jax
import jax.numpy as jnp
import numpy as np
from jax import lax
from jax.experimental import pallas as pl
from jax.experimental.pallas import tpu as pltpu

f32 = jnp.float32
bf16 = jnp.bfloat16
i32 = jnp.int32

D_MODEL = 1024
N_PROMPT_BATCH = 16
PROMPT_SEQ = 256
DEPTH = 2
N_SAMPLE_BATCH = 2
SAMPLE_SEQ = 2048
PAST_LEN = 512
GRID_W = 64
HEAD_DIM = 64
ROPE_THETA = 10000.0
EPS = 1e-6
A_WINDOW = 128
NA_ROWS = 8
NA_COLS = 16
N_EXPERTS = 64
TOP_K = 6
N_GROUPS = 8
TOPK_GROUPS = 4
EXPERT_DIM = 256
ROUTED_SCALE = 2.5
Q_SCALE = HEAD_DIM ** -0.5

N_PROMPT = N_PROMPT_BATCH * PROMPT_SEQ
N_SAMPLE = N_SAMPLE_BATCH * SAMPLE_SEQ
N_TOK = N_PROMPT + N_SAMPLE

LANES = 128
TM = 256
TD = 512
N_PROMPT_TILES = N_PROMPT // TM
N_TILES = N_TOK // TM
QB = 128
CHUNK = 16
SLOTS = -(-(TM * TOP_K + N_EXPERTS * (CHUNK - 1)) // 256) * 256
SLOT_CHUNKS = SLOTS // CHUNK
ROUTE_ROWS = 512
ROUTE_TILES = 2
COMBINE_SLOTS = 3
COMBINE_SURE_CHUNKS = 136
GM = 256
_MAX_SORTED = TM * TOP_K * N_TILES + N_TILES * N_EXPERTS * (CHUNK - 1) + N_EXPERTS * (GM - CHUNK)
G_TILES = -(-_MAX_SORTED // GM)
G_CHUNKS = GM // CHUNK
GMM_ITEM_TILES = 2
GMM_SLOTS = 4
GMM_MAX_ITEMS = (G_TILES + N_EXPERTS * (GMM_ITEM_TILES - 1)) // GMM_ITEM_TILES
_GMM_TAIL_PER_STEP = -(-(G_TILES - TM * TOP_K * N_TILES // GM) // N_EXPERTS)
ADA_COLS = 1536
DIFF_HEADS_PER_PASS = 4
VMEM_LIMIT = 56 * 1024 * 1024

NEG = -1e30


def _cparams(sem):
    return pltpu.CompilerParams(dimension_semantics=sem, vmem_limit_bytes=VMEM_LIMIT)


def _mod_row(i, tm=TM):
    return jnp.where(i < N_PROMPT // tm, 0, 1 + (i - N_PROMPT // tm) // (SAMPLE_SEQ // tm))


def _prompt_block(i, tm=TM):
    return (jnp.minimum(i, N_PROMPT // tm - 1), 0)


def _sample_block(i, tm=TM):
    return (jnp.maximum(i - N_PROMPT // tm, 0), 0)


def _x_specs(parts, tm=TM):
    if len(parts) == 1:
        return [pl.BlockSpec((tm, D_MODEL), lambda i, *_: (i, 0))]
    return [pl.BlockSpec((tm, D_MODEL), lambda i, *_: _prompt_block(i, tm)),
            pl.BlockSpec((tm, D_MODEL), lambda i, *_: _sample_block(i, tm))]


def _load_x(i, x_refs, tm=TM):
    if len(x_refs) == 1:
        return x_refs[0][...]
    return jnp.where(i < N_PROMPT // tm, x_refs[0][...], x_refs[1][...])


def _norm_mod(x, g, scale, shift):
    y = x * lax.rsqrt(jnp.mean(x * x, axis=-1, keepdims=True) + EPS)
    return (y * g) * (1.0 + scale) + shift


def _silu(x):
    return x * jax.nn.sigmoid(x)


def _dot(a, b):
    return jnp.dot(a, b, preferred_element_type=f32)


def _dot_nt(a, b):
    return lax.dot_general(a, b, (((1,), (1,)), ((), ())), preferred_element_type=f32)


def _adaln_kernel(cond_ref, w_ref, b_ref, o_ref):
    s = _silu(cond_ref[...]).astype(bf16)
    o_ref[...] = _dot(s, w_ref[...].astype(bf16)) + b_ref[...]


def _adaln(cond8, w_mod, b_mod):
    n6 = 6 * D_MODEL
    return pl.pallas_call(
        _adaln_kernel,
        grid=(DEPTH, n6 // ADA_COLS),
        in_specs=[
            pl.BlockSpec((8, D_MODEL), lambda l, j: (0, 0)),
            pl.BlockSpec((None, D_MODEL, ADA_COLS), lambda l, j: (l, 0, j)),
            pl.BlockSpec((None, 1, ADA_COLS), lambda l, j: (l, 0, j)),
        ],
        out_specs=pl.BlockSpec((None, 8, ADA_COLS), lambda l, j: (l, 0, j)),
        out_shape=jax.ShapeDtypeStruct((DEPTH, 8, n6), f32),
        compiler_params=_cparams(("parallel", "parallel")),
        name="adaln",
    )(cond8, w_mod, b_mod.reshape(DEPTH, 1, n6))


def _rope_block(blk, cos, sin_a, sin_b):
    return blk * cos + pltpu.roll(blk, LANES - 16, 1) * sin_a + pltpu.roll(blk, 16, 1) * sin_b


def _inproj_kernel(*refs, chunks, n_x):
    x_refs, kv_refs = refs[:n_x], refs[n_x + 7:]
    mod_ref, g_ref, w_ref, cos_ref, sa_ref, sb_ref, qkv_ref = refs[n_x:n_x + 7]
    i = pl.program_id(0)
    is_prompt = i < N_PROMPT // TD
    kb = 256

    def normed_input():
        x = _load_x(i, x_refs, TD)
        rinv = lax.rsqrt(jnp.mean(x * x, axis=-1, keepdims=True) + EPS)
        c0, c1, s0 = chunks[0][:3]
        blocks, acc = [], None
        for k in range(D_MODEL // kb):
            sl = slice(k * kb, (k + 1) * kb)
            hk = ((((x[:, sl] * rinv) * g_ref[:, sl]) * (1.0 + mod_ref[1:2, sl])) + mod_ref[0:1, sl]).astype(bf16)
            part = _dot(hk, w_ref[sl, s0:s0 + (c1 - c0)])
            acc = part if acc is None else acc + part
            blocks.append(hk)
        return jnp.concatenate(blocks, axis=1), acc

    @pl.when(is_prompt)
    def _():
        h, acc0 = normed_input()
        for n, (c0, c1, s0, _, kv_out) in enumerate(chunks):
            acc = acc0 if n == 0 else _dot(h, w_ref[:, s0:s0 + (c1 - c0)])
            qkv_ref[:, c0:c1] = acc.astype(bf16)
            for ridx, a0, a1, o0, per_head in kv_out:
                if per_head:
                    heads = kv_refs[ridx].shape[0] // TD
                    for j in range((a1 - a0) // LANES):
                        kv_refs[ridx][pl.ds(o0 // LANES + j, TD, stride=heads), :] = (
                            acc[:, a0 + j * LANES:a0 + (j + 1) * LANES])
                else:
                    t = acc[:, a0:a1].T
                    for b in range(TD // PROMPT_SEQ):
                        kv_refs[ridx][b, o0:o0 + (a1 - a0), :] = t[:, b * PROMPT_SEQ:(b + 1) * PROMPT_SEQ]

    @pl.when(jnp.logical_not(is_prompt))
    def _():
        cos, sa, sb = cos_ref[...], sa_ref[...], sb_ref[...]
        h, acc0 = normed_input()
        for n, (c0, c1, s0, rope_blocks, _) in enumerate(chunks):
            acc = acc0 if n == 0 else _dot(h, w_ref[:, s0:s0 + (c1 - c0)])
            for b in range((c1 - c0) // LANES):
                blk = acc[:, b * LANES:(b + 1) * LANES]
                if b in rope_blocks:
                    blk = _rope_block(blk, cos, sa, sb)
                qkv_ref[:, c0 + b * LANES:c0 + (b + 1) * LANES] = blk.astype(bf16)


def _inproj(x_parts, mod_l, g, w, rope_tabs, chunks, kv_outs):
    n = w.shape[1]
    cos, sa, sb = rope_tabs
    bpt = TD // PROMPT_SEQ

    def rope_idx(i):
        return (jnp.where(i < N_PROMPT // TD, 0, (i - N_PROMPT // TD) % (SAMPLE_SEQ // TD)), 0)

    kv_specs, kv_shapes = [], []
    for kind, size in kv_outs:
        if kind == "T":
            kv_specs.append(pl.BlockSpec((bpt, size, PROMPT_SEQ), lambda i: _prompt_block(i, TD) + (0,)))
            kv_shapes.append(jax.ShapeDtypeStruct((N_PROMPT_BATCH, size, PROMPT_SEQ), f32))
        else:
            kv_specs.append(pl.BlockSpec((TD * size, LANES), lambda i: _prompt_block(i, TD)))
            kv_shapes.append(jax.ShapeDtypeStruct((N_PROMPT * size, LANES), f32))

    return pl.pallas_call(
        functools.partial(_inproj_kernel, chunks=chunks, n_x=len(x_parts)),
        grid=(N_TOK // TD,),
        in_specs=_x_specs(x_parts, TD) + [
            pl.BlockSpec((None, 6, D_MODEL), lambda i: (_mod_row(i, TD), 0, 0)),
            pl.BlockSpec((1, D_MODEL), lambda i: (0, 0)),
            pl.BlockSpec((D_MODEL, n), lambda i: (0, 0)),
            pl.BlockSpec((TD, LANES), rope_idx),
            pl.BlockSpec((TD, LANES), rope_idx),
            pl.BlockSpec((TD, LANES), rope_idx),
        ],
        out_specs=[pl.BlockSpec((TD, n), lambda i: (i, 0))] + kv_specs,
        out_shape=[jax.ShapeDtypeStruct((N_TOK, n), bf16)] + kv_shapes,
        compiler_params=_cparams(("arbitrary",)),
        name="inproj",
    )(*x_parts, mod_l, g, w, cos, sa, sb)


def _rope_tables():
    nq = HEAD_DIM // 4
    t = np.arange(SAMPLE_SEQ)
    inv = np.power(np.float32(ROPE_THETA), -np.arange(nq, dtype=np.float32) / np.float32(nq))
    ang_r = (t // GRID_W).astype(np.float32)[:, None] * inv
    ang_c = (t % GRID_W).astype(np.float32)[:, None] * inv
    zero = np.zeros_like(ang_r)

    def head(fr, fc):
        return np.concatenate([fr[0], fr[1], fc[0], fc[1]], axis=-1)

    cos = head((np.cos(ang_r), np.cos(ang_r)), (np.cos(ang_c), np.cos(ang_c)))
    sin_a = head((-np.sin(ang_r), zero), (-np.sin(ang_c), zero))
    sin_b = head((zero, np.sin(ang_r)), (zero, np.sin(ang_c)))
    two = lambda a: jnp.asarray(np.concatenate([a, a], axis=-1).astype(np.float32))
    return two(cos), two(sin_a), two(sin_b)


def _lane_lo(shape):
    return lax.broadcasted_iota(i32, shape, len(shape) - 1) < HEAD_DIM


def _half(q, lo_mask, half):
    keep = lo_mask if half == 0 else jnp.logical_not(lo_mask)
    return jnp.where(keep, q, jnp.zeros_like(q)) * Q_SCALE


def _swap_halves(x):
    return pltpu.roll(x.astype(f32), HEAD_DIM, 1).astype(x.dtype)


def _stack_halves(q, lo_mask):
    return jnp.concatenate([_half(q, lo_mask, 0), _half(q, lo_mask, 1)], axis=0)


def _with_ones(v):
    return jnp.concatenate([v, jnp.ones_like(v)], axis=1)


def _attend_many(problems):
    scores = [[_dot_nt(q_rows, k) for k in key_blocks] for q_rows, _, key_blocks, _, _, _ in problems]
    exps, maxes = [], []
    for (q_rows, n_heads, key_blocks, _, fix_scores, sinks), sc in zip(problems, scores):
        r = q_rows.shape[0] // n_heads
        e_p, m_p = [[] for _ in key_blocks], []
        for h in range(n_heads):
            blocks = [s[h * r:(h + 1) * r] for s in sc]
            if fix_scores is not None:
                blocks = [fix_scores(h, i, s) for i, s in enumerate(blocks)]
            m = functools.reduce(jnp.maximum, [jnp.max(s, axis=-1, keepdims=True) for s in blocks])
            if sinks is not None:
                m = jnp.maximum(m, sinks[h])
            m_p.append(m)
            for i, s in enumerate(blocks):
                e_p[i].append(jnp.exp((s - m).astype(bf16)))
        exps.append(e_p)
        maxes.append(m_p)
    outs = [functools.reduce(lambda a, b: a + b,
                             [_dot(e[0] if n_heads == 1 else jnp.concatenate(e, axis=0), vx)
                              for e, vx in zip(e_p, vx_blocks)])
            for (_, n_heads, _, vx_blocks, _, _), e_p in zip(problems, exps)]
    results = []
    for (q_rows, n_heads, _, _, _, sinks), out, m_p in zip(problems, outs, maxes):
        r = q_rows.shape[0] // n_heads
        res = []
        for h in range(n_heads):
            den = out[h * r:(h + 1) * r, LANES:]
            if sinks is not None:
                den = den + jnp.exp(sinks[h] - m_p[h])
            res.append(out[h * r:(h + 1) * r, :LANES] * (1.0 / den))
        results.append(res)
    return results


def _gqa_rows(q_blocks, group, lo_mask):
    parts = []
    for q in q_blocks:
        for half in range(2):
            qh = _half(q, lo_mask, half)
            parts.append(qh if half == group else _swap_halves(qh))
    return jnp.concatenate(parts, axis=0)


def _gqa_merge(outs, group, lo_mask):
    fixed = [o if idx % 2 == group else pltpu.roll(o, HEAD_DIM, 1) for idx, o in enumerate(outs)]
    return [jnp.where(lo_mask, fixed[2 * p], fixed[2 * p + 1]) for p in range(len(outs) // 2)]


L0_QA, L0_QB, L0_KB, L0_VB, L0_KA, L0_VA, L0_N = 0, 512, 1024, 1536, 2048, 2176, 2304


def _ctx0_kernel(sink_ref, qkv_ref, o_ref):
    lo = _lane_lo((1, LANES))
    blk = lambda base, j: qkv_ref[:, base + j * LANES:base + (j + 1) * LANES]
    k_a = blk(L0_KA, 0)
    vx_a = _with_ones(blk(L0_VA, 0))
    problems = []
    for g in range(2):
        q_rows = _gqa_rows([blk(L0_QA, 2 * g), blk(L0_QA, 2 * g + 1)], g, lo)
        problems.append((q_rows, 4, [k_a], [vx_a], None, [sink_ref[4 * g + idx] for idx in range(4)]))
    for j in range(4):
        problems.append((_stack_halves(blk(L0_QB, j), lo), 2, [blk(L0_KB, j)], [_with_ones(blk(L0_VB, j))],
                         None, None))
    results = _attend_many(problems)
    for g in range(2):
        for p, o in enumerate(_gqa_merge(results[g], g, lo)):
            j = 2 * g + p
            o_ref[:, j * LANES:(j + 1) * LANES] = o.astype(bf16)
    for j in range(4):
        outs = results[2 + j]
        o_ref[:, 512 + j * LANES:512 + (j + 1) * LANES] = jnp.where(lo, outs[0], outs[1]).astype(bf16)


def _ctx0(sink, qkv):
    return pl.pallas_call(
        _ctx0_kernel,
        grid=(N_PROMPT_BATCH,),
        in_specs=[
            pl.BlockSpec(memory_space=pltpu.SMEM),
            pl.BlockSpec((PROMPT_SEQ, L0_N), lambda b: (b, 0)),
        ],
        out_specs=pl.BlockSpec((PROMPT_SEQ, D_MODEL), lambda b: (b, 0)),
        out_shape=jax.ShapeDtypeStruct((N_PROMPT, D_MODEL), bf16),
        compiler_params=_cparams(("parallel",)),
        name="ctx0",
    )(sink, qkv)


WIN_KEYS = 3 * QB
NA_KEY_ROWS = 10
NA_KEYS = NA_KEY_ROWS * GRID_W
N_QB = SAMPLE_SEQ // QB
N_NA_PATTERNS = 5
_PROMPT_QBLOCKS = N_PROMPT // QB


def _na_pattern(n):
    return jnp.where(n < 2, n, jnp.where(n > N_QB - 3, n - (N_QB - 5), 2))


def _lat0_kernel(sink_ref, roff_ref, q_ref, kvb_ref, kva_ref, cak_ref, cav_ref, cbk_ref, cbv_ref, tiles_ref, o_ref):
    n = pl.program_id(1)
    lo = _lane_lo((1, LANES))
    kstart = pl.multiple_of(jnp.clip((n - 1) * QB, 0, SAMPLE_SEQ - WIN_KEYS), QB)
    k_a = kva_ref[pl.ds(kstart, WIN_KEYS), 0:LANES]
    v_a = kva_ref[pl.ds(kstart, WIN_KEYS), LANES:2 * LANES]
    c_k = cak_ref[...].astype(bf16)
    keys_a = [c_k, k_a]
    vx_a = [_with_ones(cav_ref[...].astype(bf16)), _with_ones(v_a)]
    qpos = n * QB + lax.broadcasted_iota(i32, (QB, WIN_KEYS), 0)
    kpos = kstart + lax.broadcasted_iota(i32, (QB, WIN_KEYS), 1)
    in_window = jnp.abs(qpos - kpos) <= A_WINDOW
    mask_window = lambda h, i, s: jnp.where(in_window, s, NEG) if i == 1 else s
    problems = []
    for g in range(2):
        q_rows = _gqa_rows([q_ref[:, L0_QA + j * LANES:L0_QA + (j + 1) * LANES] for j in (2 * g, 2 * g + 1)], g, lo)
        problems.append((q_rows, 4, keys_a, vx_a, mask_window, [sink_ref[4 * g + idx] for idx in range(4)]))
    krow = jnp.clip(2 * n - NA_ROWS // 2, 0, SAMPLE_SEQ // GRID_W - NA_KEY_ROWS)
    ktok = pl.multiple_of(krow * GRID_W, QB)
    pattern = _na_pattern(n)

    def na_bias(head):
        rows = []
        for rq in range(QB // GRID_W):
            blocks = []
            for kb in range(NA_KEY_ROWS // 2):
                d0, d1 = (roff_ref[(pattern * 2 + rq) * NA_KEY_ROWS + 2 * kb + t] for t in range(2))
                blocks.append(jnp.where(lo, tiles_ref[head, d0], tiles_ref[head, d1]))
            rows.append(jnp.concatenate(blocks, axis=1))
        return jnp.concatenate(rows, axis=0)

    for j in range(4):
        q_b = q_ref[:, L0_QB + j * LANES:L0_QB + (j + 1) * LANES]
        k_b = kvb_ref[pl.ds(ktok, NA_KEYS), j * LANES:(j + 1) * LANES]
        v_b = kvb_ref[pl.ds(ktok, NA_KEYS), 512 + j * LANES:512 + (j + 1) * LANES]
        cb_k = cbk_ref[:, j * LANES:(j + 1) * LANES].astype(bf16)
        cb_v = cbv_ref[:, j * LANES:(j + 1) * LANES].astype(bf16)
        add_bias = lambda h, i, s, j=j: s + na_bias(2 * j + h) if i == 1 else s
        problems.append((_stack_halves(q_b, lo), 2, [cb_k, k_b], [_with_ones(cb_v), _with_ones(v_b)], add_bias, None))

    results = _attend_many(problems)
    for g in range(2):
        for p, o in enumerate(_gqa_merge(results[g], g, lo)):
            j = 2 * g + p
            o_ref[:, j * LANES:(j + 1) * LANES] = o.astype(bf16)
    for j in range(4):
        outs = results[2 + j]
        o_ref[:, 512 + j * LANES:512 + (j + 1) * LANES] = jnp.where(lo, outs[0], outs[1]).astype(bf16)


def _lat0(sink, qkv, cak, cav, cbk, cbv, bias_tiles):
    sb = N_PROMPT // SAMPLE_SEQ
    return pl.pallas_call(
        _lat0_kernel,
        grid=(N_SAMPLE_BATCH, N_QB),
        in_specs=[
            pl.BlockSpec(memory_space=pltpu.SMEM),
            pl.BlockSpec(memory_space=pltpu.SMEM),
            pl.BlockSpec((QB, 1024), lambda b, n: (_PROMPT_QBLOCKS + b * N_QB + n, 0)),
            pl.BlockSpec((SAMPLE_SEQ, 1024), lambda b, n: (sb + b, 1)),
            pl.BlockSpec((SAMPLE_SEQ, 256), lambda b, n: (sb + b, L0_KA // 256)),
            pl.BlockSpec((None, PAST_LEN, LANES), lambda b, n: (b, 0, 0)),
            pl.BlockSpec((None, PAST_LEN, LANES), lambda b, n: (b, 0, 0)),
            pl.BlockSpec((None, PAST_LEN, 512), lambda b, n: (b, 0, 0)),
            pl.BlockSpec((None, PAST_LEN, 512), lambda b, n: (b, 0, 0)),
            pl.BlockSpec((8, N_ROW_OFFSETS + 1, GRID_W, LANES), lambda b, n: (0, 0, 0, 0)),
        ],
        out_specs=pl.BlockSpec((QB, D_MODEL), lambda b, n: (b * N_QB + n, 0)),
        out_shape=jax.ShapeDtypeStruct((N_SAMPLE, D_MODEL), bf16),
        compiler_params=_cparams(("parallel", "arbitrary")),
        name="lat0",
    )(sink, jnp.asarray(_na_row_offsets()), qkv, qkv, qkv, cak, cav, cbk, cbv, bias_tiles)


N_ROW_OFFSETS = 2 * NA_ROWS - 1


def _na_row_offsets():
    rows = SAMPLE_SEQ // GRID_W
    idx = np.full((N_NA_PATTERNS, 2, NA_KEY_ROWS), N_ROW_OFFSETS, np.int32)
    for p, n in enumerate((0, 1, 2, N_QB - 2, N_QB - 1)):
        k0 = int(np.clip(2 * n - NA_ROWS // 2, 0, rows - NA_KEY_ROWS))
        for rq in range(2):
            r = 2 * n + rq
            rs = int(np.clip(r - NA_ROWS // 2, 0, rows - NA_ROWS))
            for kl in range(NA_KEY_ROWS):
                if rs <= k0 + kl < rs + NA_ROWS:
                    idx[p, rq, kl] = k0 + kl - r + NA_ROWS - 1
    return idx.reshape(-1)


def _na_bias_tiles(rel_bias):
    n_dc = 2 * NA_COLS - 1
    c = np.arange(GRID_W)[:, None]
    kc = np.arange(GRID_W)[None, :]
    cs = np.clip(c - NA_COLS // 2, 0, GRID_W - NA_COLS)
    col_ok = (kc >= cs) & (kc < cs + NA_COLS)
    col_hot = ((kc - c + NA_COLS - 1)[None] == np.arange(n_dc)[:, None, None]) & col_ok[None]
    hp = lax.Precision.HIGHEST
    tiles = jnp.einsum("hdx,xck->hdck", rel_bias.astype(f32), col_hot.astype(np.float32), precision=hp)
    tiles = tiles + np.where(col_ok, 0.0, NEG).astype(np.float32)
    tiles = jnp.concatenate([tiles, jnp.full((tiles.shape[0], 1, GRID_W, GRID_W), NEG, f32)], axis=1)
    return jnp.concatenate([tiles, tiles], axis=-1)


def _diff_lambda(lam_ref, lam_init):
    lv = lam_ref[...]
    s1 = jnp.sum(lv[0:1, :] * lv[1:2, :], axis=-1, keepdims=True)
    s2 = jnp.sum(lv[2:3, :] * lv[3:4, :], axis=-1, keepdims=True)
    return jnp.exp(s1) - jnp.exp(s2) + lam_init


def _diff_heads(heads, o_ref, lam, subln, lo, lam_init):
    for p0 in range(0, len(heads), DIFF_HEADS_PER_PASS):
        group = heads[p0:p0 + DIFF_HEADS_PER_PASS]
        results = _attend_many([(_stack_halves(q, lo), 2, ks, [_with_ones(v) for v in vs], None, None)
                                for _, q, ks, vs in group])
        for (h, _, _, _), (o1, o2) in zip(group, results):
            o = o1 - lam * o2
            o = o * lax.rsqrt(jnp.mean(o * o, axis=-1, keepdims=True) + EPS)
            o_ref[:, h * LANES:(h + 1) * LANES] = ((o * subln) * (1.0 - lam_init)).astype(bf16)


def _ctx1_kernel(lam_ref, subln_ref, qkv_ref, o_ref, *, lam_init):
    lo = _lane_lo((1, LANES))
    lam = _diff_lambda(lam_ref, lam_init)
    blk = lambda base, h: qkv_ref[:, base + h * LANES:base + (h + 1) * LANES]
    heads = [(h, blk(0, h), [blk(D_MODEL, h)], [blk(2 * D_MODEL, h)]) for h in range(8)]
    _diff_heads(heads, o_ref, lam, subln_ref[...], lo, lam_init)


def _ctx1(lamv, subln, qkv, lam_init):
    return pl.pallas_call(
        functools.partial(_ctx1_kernel, lam_init=lam_init),
        grid=(N_PROMPT_BATCH,),
        in_specs=[
            pl.BlockSpec((8, HEAD_DIM), lambda b: (0, 0)),
            pl.BlockSpec((1, LANES), lambda b: (0, 0)),
            pl.BlockSpec((PROMPT_SEQ, 3 * D_MODEL), lambda b: (b, 0)),
        ],
        out_specs=pl.BlockSpec((PROMPT_SEQ, D_MODEL), lambda b: (b, 0)),
        out_shape=jax.ShapeDtypeStruct((N_PROMPT, D_MODEL), bf16),
        compiler_params=_cparams(("parallel",)),
        name="ctx1",
    )(lamv, subln, qkv)


def _lat1_kernel(lam_ref, subln_ref, q_ref, k_ref, v_ref, ck_ref, cv_ref, o_ref, *, lam_init):
    lo = _lane_lo((1, LANES))
    lam = _diff_lambda(lam_ref, lam_init)
    heads = []
    for h in range(8):
        sl = slice(h * LANES, (h + 1) * LANES)
        heads.append((h, q_ref[:, sl], [ck_ref[:, sl].astype(bf16), k_ref[:, sl]],
                      [cv_ref[:, sl].astype(bf16), v_ref[:, sl]]))
    _diff_heads(heads, o_ref, lam, subln_ref[...], lo, lam_init)


def _lat1(lamv, subln, qkv, ck, cv, lam_init):
    sb = N_PROMPT // SAMPLE_SEQ
    nq = SAMPLE_SEQ // TM
    return pl.pallas_call(
        functools.partial(_lat1_kernel, lam_init=lam_init),
        grid=(N_SAMPLE_BATCH, nq),
        in_specs=[
            pl.BlockSpec((8, HEAD_DIM), lambda b, n: (0, 0)),
            pl.BlockSpec((1, LANES), lambda b, n: (0, 0)),
            pl.BlockSpec((TM, D_MODEL), lambda b, n: (N_PROMPT_TILES + b * nq + n, 0)),
            pl.BlockSpec((SAMPLE_SEQ, D_MODEL), lambda b, n: (sb + b, 1)),
            pl.BlockSpec((SAMPLE_SEQ, D_MODEL), lambda b, n: (sb + b, 2)),
            pl.BlockSpec((None, PAST_LEN, D_MODEL), lambda b, n: (b, 0, 0)),
            pl.BlockSpec((None, PAST_LEN, D_MODEL), lambda b, n: (b, 0, 0)),
        ],
        out_specs=pl.BlockSpec((TM, D_MODEL), lambda b, n: (b * nq + n, 0)),
        out_shape=jax.ShapeDtypeStruct((N_SAMPLE, D_MODEL), bf16),
        compiler_params=_cparams(("parallel", "arbitrary")),
        name="lat1",
    )(lamv, subln, qkv, qkv, qkv, ck, cv)


def _split_bf16(a):
    hi = a.astype(bf16)
    return hi, (a - hi.astype(f32)).astype(bf16)


def _route_kernel(*refs, n_x):
    x_refs = refs[:n_x]
    (op_ref, os_ref, mod_ref, g_ref, wo_ref, rwt_ref, rb_ref,
     xnew_ref, xloc_ref, slots_ref, gate_ref, len_ref) = refs[n_x:]
    logits = _route_logits(x_refs, op_ref, os_ref, mod_ref, g_ref, wo_ref, rwt_ref, xnew_ref)
    tiles = _route_tiles(jnp.concatenate([lg for _, lg in logits], axis=1), rb_ref, slots_ref, gate_ref, len_ref)
    _route_dispatch(tiles, [h_hi for h_hi, _ in logits], xloc_ref)


def _route_logits(x_refs, op_ref, os_ref, mod_ref, g_ref, wo_ref, rwt_ref, xnew_ref):
    is_prompt = pl.program_id(0) < N_PROMPT_TILES // ROUTE_TILES
    tiles = [slice(t * TM, (t + 1) * TM) for t in range(ROUTE_TILES)]
    proj = [_dot(jnp.where(is_prompt, op_ref[rows, :], os_ref[rows, :]), wo_ref[...]) for rows in tiles]
    w_hi, w_lo = _split_bf16(rwt_ref[...])
    splits = []
    for rows, y in zip(tiles, proj):
        x_in = x_refs[0][rows, :] if len(x_refs) == 1 else jnp.where(is_prompt, x_refs[0][rows, :], x_refs[1][rows, :])
        x = x_in + mod_ref[2:3, :] * y
        xnew_ref[rows, :] = x
        splits.append(_split_bf16(_norm_mod(x, g_ref[...], mod_ref[4:5, :], mod_ref[3:4, :])))
    return [(h_hi, _dot_nt(w_hi, h_hi) + (_dot_nt(w_hi, h_lo) + _dot_nt(w_lo, h_hi)))
            for h_hi, h_lo in splits]


def _route_tiles(logits, rb_ref, slots_ref, gate_ref, len_ref):
    ng, ge = N_GROUPS, N_EXPERTS // N_GROUPS
    n = ROUTE_TILES * TM
    tile = lambda a, t: a[..., t * TM:(t + 1) * TM]
    scores = jax.nn.sigmoid(logits)
    biased = scores + rb_ref[...]
    s3 = scores.reshape(ng, ge, n)
    b3 = biased.reshape(ng, ge, n)
    in_group = lax.broadcasted_iota(i32, (ng, ge, n), 1).astype(f32)
    group_id = lax.broadcasted_iota(i32, (ng, 1, n), 0).astype(f32)
    expert_id = lax.broadcasted_iota(i32, (ng, ge, n), 0).astype(f32) * ge + in_group

    def max01(a):
        return jnp.max(jnp.max(a, axis=0, keepdims=True), axis=1, keepdims=True)

    def min01(a):
        return jnp.min(jnp.min(a, axis=0, keepdims=True), axis=1, keepdims=True)

    def sum01(a):
        return jnp.sum(jnp.sum(a, axis=0, keepdims=True), axis=1, keepdims=True)

    m1 = jnp.max(b3, axis=1, keepdims=True)
    first = jnp.min(jnp.where(b3 == m1, in_group, ge), axis=1, keepdims=True)
    m2 = jnp.max(jnp.where(in_group == first, -jnp.inf, b3), axis=1, keepdims=True)
    gscore = m1 + m2
    gsel = jnp.zeros((ng, 1, n), f32)
    for _ in range(TOPK_GROUPS):
        gm = jnp.max(gscore, axis=0, keepdims=True)
        gi = jnp.min(jnp.where(gscore == gm, group_id, ng), axis=0, keepdims=True)
        hit = group_id == gi
        gsel = jnp.where(hit, 1.0, gsel)
        gscore = jnp.where(hit, -jnp.inf, gscore)
    cand = jnp.where(jnp.broadcast_to(gsel, (ng, ge, n)) > 0.0, b3, -jnp.inf)
    top_e, top_w = [], []
    for _ in range(TOP_K):
        em = max01(cand)
        ei = min01(jnp.where(cand == em, expert_id, N_EXPERTS))
        hit = expert_id == ei
        top_e.append(ei)
        top_w.append(sum01(jnp.where(hit, s3, 0.0)))
        cand = jnp.where(hit, -jnp.inf, cand)
    wsum = functools.reduce(lambda a, b: a + b, top_w)
    sel3 = jnp.zeros((ng, ge, n), f32)
    for k, (ei, w) in enumerate(zip(top_e, top_w)):
        gate = (w / wsum * ROUTED_SCALE).reshape(1, n)
        for t in range(ROUTE_TILES):
            gate_ref[t, k:k + 1, :] = tile(gate, t)
        sel3 = jnp.where(expert_id == ei, 1.0, sel3)
    sel = sel3.reshape(N_EXPERTS, n)

    r_i = lax.broadcasted_iota(i32, (N_EXPERTS, N_EXPERTS), 0)
    c_i = lax.broadcasted_iota(i32, (N_EXPERTS, N_EXPERTS), 1)
    lower = jnp.where(c_i < r_i, 1.0, 0.0).astype(bf16)
    run_lens, run_offs = [], []
    for t in range(ROUTE_TILES):
        cnt = jnp.sum(tile(sel, t), axis=1, keepdims=True)
        run_len = jnp.ceil(cnt * (1.0 / CHUNK)) * CHUNK
        run_off = _dot(lower, jnp.broadcast_to(run_len, (N_EXPERTS, LANES)).astype(bf16))[:, 0:1]
        run_lens.append(run_len)
        run_offs.append(jnp.broadcast_to(run_off, (N_EXPERTS, TM)))
    t_r = lax.broadcasted_iota(i32, (n, n), 0)
    t_c = lax.broadcasted_iota(i32, (n, n), 1)
    before = jnp.where(jnp.logical_and(t_r < t_c, t_r // TM == t_c // TM), 1.0, 0.0).astype(bf16)
    rank = _dot(sel.astype(bf16), before)
    slot3 = (jnp.concatenate(run_offs, axis=1) + rank).reshape(ng, ge, n)
    slots = [sum01(jnp.where(expert_id == ei, slot3, 0.0)).reshape(1, n).astype(i32) for ei in top_e]
    out = []
    for t in range(ROUTE_TILES):
        for k in range(TOP_K):
            slots_ref[t, k:k + 1, :] = tile(slots[k], t)
        slots_ref[t, TOP_K:8, :] = jnp.full((8 - TOP_K, TM), -1, i32)
        gate_ref[t, TOP_K:8, :] = jnp.zeros((8 - TOP_K, TM), f32)
        cnt_row = _dot_nt(jnp.ones((8, TM), bf16), tile(sel, t).astype(bf16))
        len_ref[t] = (jnp.ceil(cnt_row * (1.0 / CHUNK)) * CHUNK).astype(i32)
        out.append(([tile(sl, t) for sl in slots], run_lens[t]))
    return out


def _route_dispatch(tiles, h_his, xloc_ref):
    rows = ROUTE_ROWS

    def onehot(t, base):
        row_id = base.astype(jnp.int16) + lax.broadcasted_iota(jnp.int16, (rows, TM), 0)
        p = jnp.zeros((rows, TM), bf16)
        for k in range(TOP_K):
            p = jnp.where(row_id == tiles[t][0][k].astype(jnp.int16), jnp.ones((), bf16), p)
        return p

    def blocks_of(ts):
        def body(c, carry):
            base = pl.multiple_of(c * rows, rows)
            ps = [onehot(t, base) for t in ts]
            for t, p in zip(ts, ps):
                xloc_ref[pl.ds(t * SLOTS + base, rows), :] = _dot(p, h_his[t]).astype(bf16)
            return carry
        return body

    n_used = [(jnp.sum(run_len).astype(i32) + (rows - 1)) // rows for _, run_len in tiles]
    common = functools.reduce(jnp.minimum, n_used)
    lax.fori_loop(0, common, blocks_of(list(range(len(tiles)))), 0)
    for t in range(len(tiles)):
        lax.fori_loop(common, n_used[t], blocks_of([t]), 0)

        def zero_body(c, carry, t=t):
            base = pl.multiple_of(c * rows, rows)
            xloc_ref[pl.ds(t * SLOTS + base, rows), :] = jnp.zeros((rows, D_MODEL), bf16)
            return carry

        lax.fori_loop(n_used[t], SLOTS // rows, zero_body, 0)


def _route(x_parts, o_prompt, o_sample, mod_l, g, w_out, rwt, rb):
    per_tile = lambda i: (i, 0, 0)
    rt = ROUTE_TILES
    tm = rt * TM
    return pl.pallas_call(
        functools.partial(_route_kernel, n_x=len(x_parts)),
        grid=(N_TILES // rt,),
        in_specs=_x_specs(x_parts, tm) + [
            pl.BlockSpec((tm, D_MODEL), lambda i: _prompt_block(i, tm)),
            pl.BlockSpec((tm, D_MODEL), lambda i: _sample_block(i, tm)),
            pl.BlockSpec((None, 6, D_MODEL), lambda i: (_mod_row(i, tm), 0, 0)),
            pl.BlockSpec((1, D_MODEL), lambda i: (0, 0)),
            pl.BlockSpec((D_MODEL, D_MODEL), lambda i: (0, 0)),
            pl.BlockSpec((N_EXPERTS, D_MODEL), lambda i: (0, 0)),
            pl.BlockSpec((N_EXPERTS, 1), lambda i: (0, 0)),
        ],
        out_specs=[
            pl.BlockSpec((tm, D_MODEL), lambda i: (i, 0)),
            pl.BlockSpec((rt * SLOTS, D_MODEL), lambda i: (i, 0)),
            pl.BlockSpec((rt, 8, TM), per_tile),
            pl.BlockSpec((rt, 8, TM), per_tile),
            pl.BlockSpec((rt, 8, N_EXPERTS), per_tile),
        ],
        out_shape=[
            jax.ShapeDtypeStruct((N_TOK, D_MODEL), f32),
            jax.ShapeDtypeStruct((N_TILES * SLOTS, D_MODEL), bf16),
            jax.ShapeDtypeStruct((N_TILES, 8, TM), i32),
            jax.ShapeDtypeStruct((N_TILES, 8, TM), f32),
            jax.ShapeDtypeStruct((N_TILES, 8, N_EXPERTS), i32),
        ],
        compiler_params=_cparams(("parallel",)),
        name="route",
    )(*x_parts, o_prompt, o_sample, mod_l, g, w_out, rwt, rb)


def _moe_plan(run_len):
    nt, ne = run_len.shape

    def excl_cumsum(a):
        n = a.shape[-1]
        earlier = np.arange(n)[None, :] < np.arange(n)[:, None]
        return jnp.sum(jnp.where(earlier, a[..., None, :], 0), axis=-1)

    def first_diff(a):
        return a - jnp.concatenate([jnp.zeros_like(a[..., :1]), a[..., :-1]], axis=-1)

    off_loc = excl_cumsum(run_len)
    before = excl_cumsum(run_len.T).T
    n_e = jnp.sum(run_len, axis=0)
    n_pad = -(-n_e // GM) * GM
    g_start = excl_cumsum(n_pad)
    total = jnp.sum(n_pad)
    run_dst = g_start[None, :] + before
    run_src = jnp.arange(nt, dtype=i32)[:, None] * SLOTS + off_loc
    dst_f = run_dst.T.reshape(-1)
    shift_f = first_diff((run_src - run_dst).T.reshape(-1))
    rows = jnp.arange((G_TILES + GMM_ITEM_TILES - 1) * G_CHUNKS, dtype=i32) * CHUNK
    shift = jnp.sum(jnp.where(dst_f[None, :] <= rows[:, None], shift_f[None, :], 0), axis=1)
    in_run = jnp.any((g_start[None, :] <= rows[:, None]) & (rows[:, None] < (g_start + n_e)[None, :]), axis=1)
    chunk_src = (jnp.where(in_run, rows + shift, 0) // CHUNK).astype(i32)
    loc_rows = jnp.arange(SLOT_CHUNKS, dtype=i32) * CHUNK
    shift_l = first_diff(run_dst - off_loc)
    shift = jnp.sum(jnp.where(off_loc[:, None, :] <= loc_rows[None, :, None], shift_l[:, None, :], 0), axis=2)
    used = jnp.sum(run_len, axis=1)
    chunk_map = jnp.where(loc_rows[None, :] < used[:, None], (loc_rows[None, :] + shift) // CHUNK, 0).astype(i32)
    tile_start, n_tiles = g_start // GM, n_pad // GM
    n_items = -(-n_tiles // GMM_ITEM_TILES)
    item_start = excl_cumsum(n_items)
    items = jnp.arange(GMM_MAX_ITEMS, dtype=i32)
    owner = items[:, None] >= item_start[None, :]
    e_first = jnp.sum(jnp.where(owner, first_diff(tile_start - GMM_ITEM_TILES * item_start)[None, :], 0), axis=1)
    item_tile = e_first + GMM_ITEM_TILES * items
    e_end = jnp.sum(jnp.where(owner, first_diff(tile_start + n_tiles)[None, :], 0), axis=1)
    item_cnt = jnp.clip(e_end - item_tile, 0, GMM_ITEM_TILES)
    gmm_plan = tuple(a.astype(i32) for a in (item_start, n_items, item_tile, item_cnt, chunk_src))
    long_tiles = (used > COMBINE_SURE_CHUNKS * CHUNK).astype(i32)
    return gmm_plan, chunk_map.reshape(-1), long_tiles


def _gmm_in_copy(xloc_hbm, xbuf, sem, src_chunk, slot, c):
    return pltpu.make_async_copy(xloc_hbm.at[src_chunk], xbuf.at[slot, c], sem.at[slot])


def _gmm_out_copy(ybuf, y_hbm, sem, tile, slot, n_tiles):
    chunks = n_tiles * G_CHUNKS
    return pltpu.make_async_copy(ybuf.at[slot, pl.ds(0, chunks)],
                                 y_hbm.at[pl.ds(tile * G_CHUNKS, chunks)], sem.at[slot])


def _gmm_kernel(i0_ref, ni_ref, it_ref, ic_ref, cs_ref, xloc_hbm, wg_ref, wu_ref, wd_ref, y_hbm,
                xbuf, ybuf, zbuf, wg_b, wu_b, wd_b, in_sem, out_sem, zsem):
    e = pl.program_id(0)
    last = pl.num_programs(0) - 1
    n_items = ni_ref[e]
    first_item = i0_ref[e]
    total_items = i0_ref[last] + ni_ref[last]
    last_item = total_items - 1
    total_tiles = it_ref[last_item] + ic_ref[last_item]

    def start_in(item):
        first = it_ref[item] * G_CHUNKS
        for c in range(GMM_ITEM_TILES * G_CHUNKS):
            _gmm_in_copy(xloc_hbm, xbuf, in_sem, cs_ref[first + c], item % GMM_SLOTS, c).start()

    def wait_in(item):
        for c in range(GMM_ITEM_TILES * G_CHUNKS):
            _gmm_in_copy(xloc_hbm, xbuf, in_sem, 0, item % GMM_SLOTS, c).wait()

    def out_copy(item, fn):
        for cnt in range(1, GMM_ITEM_TILES + 1):
            @pl.when(ic_ref[item] == cnt)
            def _():
                fn(_gmm_out_copy(ybuf, y_hbm, out_sem, it_ref[item], item % GMM_SLOTS, cnt))

    @pl.when(e == 0)
    def _():
        for item in range(GMM_SLOTS - 1):
            start_in(item)
        zbuf[...] = jnp.zeros(zbuf.shape, zbuf.dtype)

    def tail_copies(fn):
        for j in range(_GMM_TAIL_PER_STEP):
            tile = total_tiles + e + j * N_EXPERTS

            @pl.when(tile < G_TILES)
            def _():
                fn(pltpu.make_async_copy(zbuf, y_hbm.at[pl.ds(tile * G_CHUNKS, G_CHUNKS)], zsem.at[0]))

    tail_copies(lambda cp: cp.start())

    @pl.when(n_items > 0)
    def _():
        wg_b[...] = wg_ref[...].astype(bf16)
        wu_b[...] = wu_ref[...].astype(bf16)
        wd_b[...] = wd_ref[...].astype(bf16)

    def body(j, carry):
        item = first_item + j
        slot = item % GMM_SLOTS

        @pl.when(item + (GMM_SLOTS - 1) < total_items)
        def _():
            start_in(item + (GMM_SLOTS - 1))

        wait_in(item)

        @pl.when(item >= GMM_SLOTS)
        def _():
            out_copy(item - GMM_SLOTS, lambda cp: cp.wait())

        for cnt in range(1, GMM_ITEM_TILES + 1):
            @pl.when(ic_ref[item] == cnt)
            def _():
                rows = cnt * GM
                chunks = cnt * G_CHUNKS
                x = xbuf[slot, 0:chunks].reshape(rows, D_MODEL)
                act = _silu(_dot(x, wg_b[...])) * _dot(x, wu_b[...])
                y = _dot(act.astype(bf16), wd_b[...]).astype(bf16)
                ybuf[slot, 0:chunks] = y.reshape(chunks, CHUNK, D_MODEL)

        out_copy(item, lambda cp: cp.start())
        return carry

    lax.fori_loop(0, n_items, body, 0)
    tail_copies(lambda cp: cp.wait())

    @pl.when(e == last)
    def _():
        for back in range(1, GMM_SLOTS + 1):
            out_copy(total_items - back, lambda cp: cp.wait())


def _gmm(plan, xloc, wg, wu, wd, layer):
    rows = GMM_ITEM_TILES * GM
    w_idx = lambda e, *_: (layer, e, 0, 0)
    grid_spec = pltpu.PrefetchScalarGridSpec(
        num_scalar_prefetch=5,
        grid=(N_EXPERTS,),
        in_specs=[
            pl.BlockSpec(memory_space=pl.ANY),
            pl.BlockSpec((None, None, D_MODEL, EXPERT_DIM), w_idx),
            pl.BlockSpec((None, None, D_MODEL, EXPERT_DIM), w_idx),
            pl.BlockSpec((None, None, EXPERT_DIM, D_MODEL), w_idx),
        ],
        out_specs=pl.BlockSpec(memory_space=pl.ANY),
        scratch_shapes=[pltpu.VMEM((GMM_SLOTS, rows // CHUNK, CHUNK, D_MODEL), bf16),
                        pltpu.VMEM((GMM_SLOTS, rows // CHUNK, CHUNK, D_MODEL), bf16),
                        pltpu.VMEM((G_CHUNKS, CHUNK, D_MODEL), bf16),
                        pltpu.VMEM((D_MODEL, EXPERT_DIM), bf16), pltpu.VMEM((D_MODEL, EXPERT_DIM), bf16),
                        pltpu.VMEM((EXPERT_DIM, D_MODEL), bf16),
                        pltpu.SemaphoreType.DMA((GMM_SLOTS,)), pltpu.SemaphoreType.DMA((GMM_SLOTS,)),
                        pltpu.SemaphoreType.DMA((1,))],
    )
    return pl.pallas_call(
        _gmm_kernel,
        grid_spec=grid_spec,
        out_shape=jax.ShapeDtypeStruct((G_TILES * G_CHUNKS, CHUNK, D_MODEL), bf16),
        compiler_params=_cparams(("arbitrary",)),
        name="gmm",
    )(*plan, xloc, wg, wu, wd)


def _combine_copy(y_hbm, ybuf, sem, sorted_chunk, slot, c):
    return pltpu.make_async_copy(y_hbm.at[sorted_chunk], ybuf.at[slot, c], sem.at[slot])


def _combine_kernel(cm_ref, long_ref, y_hbm, slots_ref, gate_ref, x_ref, mod_ref, g_ref, sg_ref, su_ref, sd_ref,
                    *rest, final):
    if final:
        gf_ref, yp_ref, ys_ref, ybuf, sem = rest
    else:
        o_ref, ybuf, sem = rest
    i = pl.program_id(0)
    n = pl.num_programs(0)
    ahead = COMBINE_SLOTS - 1
    slot = i % COMBINE_SLOTS

    def for_chunks(tile, fn):
        for c in range(COMBINE_SURE_CHUNKS):
            fn(c)

        @pl.when(long_ref[tile] == 1)
        def _():
            for c in range(COMBINE_SURE_CHUNKS, SLOT_CHUNKS):
                fn(c)

    def start(tile, s):
        for_chunks(tile, lambda c: _combine_copy(y_hbm, ybuf, sem, cm_ref[tile * SLOT_CHUNKS + c], s, c)
                   .start(priority=c % 2))

    def wait(tile, s):
        for_chunks(tile, lambda c: _combine_copy(y_hbm, ybuf, sem, 0, s, c).wait())

    @pl.when(i == 0)
    def _():
        ybuf[:, COMBINE_SURE_CHUNKS:] = jnp.zeros((COMBINE_SLOTS, SLOT_CHUNKS - COMBINE_SURE_CHUNKS, CHUNK, D_MODEL), bf16)
        for tile in range(ahead):
            start(tile, tile)

    wait(i, slot)
    start((i + ahead) % n, (i + ahead) % COMBINE_SLOTS)

    x = x_ref[...]
    hb = _norm_mod(x, g_ref[...], mod_ref[4:5, :], mod_ref[3:4, :]).astype(bf16)
    shared = _dot((_silu(_dot(hb, sg_ref[...])) * _dot(hb, su_ref[...])).astype(bf16), sd_ref[...])
    row_id = lax.broadcasted_iota(jnp.int16, (SLOTS, TM), 0)
    p = jnp.zeros((SLOTS, TM), bf16)
    for k in range(TOP_K):
        p = jnp.where(row_id == slots_ref[k:k + 1, :].astype(jnp.int16), gate_ref[k:k + 1, :].astype(bf16), p)
    routed = lax.dot_general(p, ybuf[slot].reshape(SLOTS, D_MODEL), (((0,), (0,)), ((), ())),
                             preferred_element_type=f32)
    out = x + mod_ref[5:6, :] * (routed + shared)
    if final:
        y = (out * lax.rsqrt(jnp.mean(out * out, axis=-1, keepdims=True) + EPS)) * gf_ref[...]

        @pl.when(i < N_PROMPT_TILES)
        def _():
            yp_ref[...] = y

        @pl.when(i >= N_PROMPT_TILES)
        def _():
            ys_ref[...] = y
    else:
        o_ref[...] = out

    @pl.when(i == n - 1)
    def _():
        for k in range(1, ahead + 1):
            wait((i + k) % n, (i + k) % COMBINE_SLOTS)


def _combine(chunk_map, long_tiles, y, slots, gates, x, mod_l, g, sg, su, sd, final_g=None):
    shd = sg.shape[1]
    final = final_g is not None
    row_spec = pl.BlockSpec((TM, D_MODEL), lambda i, *_: (i, 0))
    vec_spec = pl.BlockSpec((1, D_MODEL), lambda i, *_: (0, 0))
    if final:
        out_specs = [pl.BlockSpec((TM, D_MODEL), lambda i, *_: _prompt_block(i)),
                     pl.BlockSpec((TM, D_MODEL), lambda i, *_: _sample_block(i))]
        out_shape = [jax.ShapeDtypeStruct((N_PROMPT, D_MODEL), f32), jax.ShapeDtypeStruct((N_SAMPLE, D_MODEL), f32)]
    else:
        out_specs, out_shape = row_spec, jax.ShapeDtypeStruct((N_TOK, D_MODEL), f32)
    grid_spec = pltpu.PrefetchScalarGridSpec(
        num_scalar_prefetch=2,
        grid=(N_TILES,),
        in_specs=[
            pl.BlockSpec(memory_space=pl.ANY),
            pl.BlockSpec((None, 8, TM), lambda i, *_: (i, 0, 0)),
            pl.BlockSpec((None, 8, TM), lambda i, *_: (i, 0, 0)),
            row_spec,
            pl.BlockSpec((None, 6, D_MODEL), lambda i, *_: (_mod_row(i), 0, 0)),
            vec_spec,
            pl.BlockSpec((D_MODEL, shd), lambda i, *_: (0, 0)),
            pl.BlockSpec((D_MODEL, shd), lambda i, *_: (0, 0)),
            pl.BlockSpec((shd, D_MODEL), lambda i, *_: (0, 0)),
        ] + ([vec_spec] if final else []),
        out_specs=out_specs,
        scratch_shapes=[pltpu.VMEM((COMBINE_SLOTS, SLOT_CHUNKS, CHUNK, D_MODEL), bf16),
                        pltpu.SemaphoreType.DMA((COMBINE_SLOTS,))],
    )
    args = (chunk_map, long_tiles, y, slots, gates, x, mod_l, g, sg, su, sd) + ((final_g,) if final else ())
    return pl.pallas_call(
        functools.partial(_combine_kernel, final=final),
        grid_spec=grid_spec,
        out_shape=out_shape,
        compiler_params=_cparams(("arbitrary",)),
        name="combine",
    )(*args)


def _moe(x_parts, o_prompt, o_sample, w_out, mod_l, g, rwt, rb, wg, wu, wd, layer, sg, su, sd, final_g=None):
    x, xloc, slots, gates, run_len = _route(x_parts, o_prompt, o_sample, mod_l, g, w_out, rwt, rb)
    gmm_plan, chunk_map, long_tiles = _moe_plan(run_len[:, 0, :])
    y = _gmm(gmm_plan, xloc.reshape(N_TILES * SLOT_CHUNKS, CHUNK, D_MODEL), wg, wu, wd, layer)
    return _combine(chunk_map, long_tiles, y, slots, gates, x, mod_l, g, sg, su, sd, final_g)


_L0_CHUNKS = (
    (0, 512, 0, (0, 1, 2, 3), ()),
    (512, 1024, 768, (), ()),
    (1024, 1536, 1280, (), ((2, 0, 512, 0, False),)),
    (1536, 2048, 1792, (), ((3, 0, 512, 0, False),)),
    (2048, 2304, 512, (0,), ((0, 0, 128, 0, False), (1, 128, 256, 0, False))),
)
_L0_KV_OUTS = (("T", 128), ("T", 128), ("T", 512), ("T", 512))
_L1_CHUNKS = (
    (0, 512, 0, (0, 1, 2, 3), ()),
    (512, 1024, 512, (0, 1, 2, 3), ()),
    (1024, 1536, 1024, (0, 1, 2, 3), ((0, 0, 512, 0, False),)),
    (1536, 2048, 1536, (0, 1, 2, 3), ((0, 0, 512, 512, False),)),
    (2048, 2560, 2048, (), ((1, 0, 512, 0, True),)),
    (2560, 3072, 2560, (), ((1, 0, 512, 512, True),)),
)
_L1_KV_OUTS = (("T", 1024), ("H", 8))


def _from_feature_major(kt, *head_dims):
    nb, _, s = kt.shape
    nd = len(head_dims)
    return kt.reshape(nb, *head_dims, s).transpose(0, nd + 1, *range(1, nd + 1))[:, None]


def kernel(x_prompt, x_sample, cache_a_k, cache_a_v, cache_b_k, cache_b_v, cache_c_k, cache_c_v, c, c_ctx, w_mod, b_mod, norm_mix, norm_ffn, w_in_ab, w_out_ab, sink_a, rel_bias_b, w_in_c, w_out_c, lam_q1, lam_k1, lam_q2, lam_k2, subln_c, router_w, router_bias, exp_w_gate, exp_w_up, exp_w_down, sh_w_gate, sh_w_up, sh_w_down, final_norm):
    x = (x_prompt.reshape(N_PROMPT, D_MODEL), x_sample.reshape(N_SAMPLE, D_MODEL))
    cond8 = jnp.concatenate([c_ctx[None, :], c, jnp.zeros((8 - 1 - N_SAMPLE_BATCH, D_MODEL), f32)], axis=0)
    mod = _adaln(cond8, w_mod, b_mod).reshape(DEPTH, 8, 6, D_MODEL)
    rope_tabs = _rope_tables()
    new_kv = {}
    for layer in range(DEPTH):
        li = layer // 2
        mod_l = mod[layer]
        g_mix = norm_mix[layer][None, :]
        g_ffn = norm_ffn[layer][None, :]
        if layer % 2 == 0:
            w_in = w_in_ab[li].astype(bf16)
            qkv, ak, av, bk, bv = _inproj(x, mod_l, g_mix, w_in, rope_tabs, _L0_CHUNKS, _L0_KV_OUTS)
            new_kv["a_k"], new_kv["a_v"], new_kv["b_k"], new_kv["b_v"] = ak, av, bk, bv
            o_p = _ctx0(sink_a[li], qkv)
            o_s = _lat0(sink_a[li], qkv,
                        cache_a_k[:, li].reshape(N_SAMPLE_BATCH, PAST_LEN, LANES),
                        cache_a_v[:, li].reshape(N_SAMPLE_BATCH, PAST_LEN, LANES),
                        cache_b_k[:, li].reshape(N_SAMPLE_BATCH, PAST_LEN, 512),
                        cache_b_v[:, li].reshape(N_SAMPLE_BATCH, PAST_LEN, 512),
                        _na_bias_tiles(rel_bias_b[li]))
            w_out = w_out_ab[li].astype(bf16)
        else:
            lam_init = 0.8 - 0.6 * math.exp(-0.3 * layer)
            qkv, ck, cv = _inproj(x, mod_l, g_mix, w_in_c[li].astype(bf16), rope_tabs, _L1_CHUNKS, _L1_KV_OUTS)
            new_kv["c_k"], new_kv["c_v"] = ck, cv
            lamv = jnp.concatenate([lam_q1[li][None], lam_k1[li][None], lam_q2[li][None], lam_k2[li][None],
                                    jnp.zeros((4, HEAD_DIM), f32)], axis=0)
            subln = subln_c[li][None, :]
            o_p = _ctx1(lamv, subln, qkv, lam_init)
            o_s = _lat1(lamv, subln, qkv,
                        cache_c_k[:, li].reshape(N_SAMPLE_BATCH, PAST_LEN, D_MODEL),
                        cache_c_v[:, li].reshape(N_SAMPLE_BATCH, PAST_LEN, D_MODEL), lam_init)
            w_out = w_out_c[li].astype(bf16)
        last = layer == DEPTH - 1
        x = _moe(x, o_p, o_s, w_out, mod_l, g_ffn, router_w[layer].T, router_bias[layer][:, None],
                 exp_w_gate, exp_w_up, exp_w_down, layer,
                 sh_w_gate[layer].astype(bf16), sh_w_up[layer].astype(bf16), sh_w_down[layer].astype(bf16),
                 final_norm[None, :] if last else None)
        x = x if last else (x,)
    y_prompt, y_sample = x
    nb, s = N_PROMPT_BATCH, PROMPT_SEQ
    return (y_prompt.reshape(nb, s, D_MODEL), y_sample.reshape(N_SAMPLE_BATCH, SAMPLE_SEQ, D_MODEL),
            _from_feature_major(new_kv["a_k"], 2, HEAD_DIM), _from_feature_major(new_kv["a_v"], 2, HEAD_DIM),
            _from_feature_major(new_kv["b_k"], 8, HEAD_DIM), _from_feature_major(new_kv["b_v"], 8, HEAD_DIM),
            _from_feature_major(new_kv["c_k"], 8, 2, HEAD_DIM), new_kv["c_v"].reshape(nb, 1, s, 8, 2 * HEAD_DIM))
```
